```python
import math
import jax, jax.numpy as jnp
from jax import lax
import numpy as np

D_MODEL = 1024
BATCH = 8
SEQ = 8192
DEPTH = 2

N_EVEN = (DEPTH + 1) // 2
N_ODD = DEPTH // 2

CONV_A_CH = D_MODEL // 2
CONV_A_WIDTH = 31
LN_EPS = 1e-5

HEAD_DIM = 64
HEADS_PER_GROUP = 8
DILATED_GROUPS = ((128, 1), (512, 4), (2048, 16))
N_GROUPS = len(DILATED_GROUPS)
N_ATTN_HEADS = N_GROUPS * HEADS_PER_GROUP
ATTN_OUT = HEADS_PER_GROUP * HEAD_DIM
ATTN_IN = 3 * N_ATTN_HEADS * HEAD_DIM
AB_IN = 2 * CONV_A_CH + ATTN_IN
AB_CAT = CONV_A_CH + ATTN_OUT

NUM_BUCKETS = 32
REL_MAX_DISTANCE = 2048

SC_WIDTH = D_MODEL
SC_CONV_WIDTH = 3

D_FF = 4 * D_MODEL
RMS_EPS = 1e-6
NEG_INF = -1e30

kernel_name = 'hybrid_conformer_dilated_shortconv'


def rms_norm(x, g, eps=RMS_EPS):
    xf = x.astype(jnp.float32)
    y = xf * lax.rsqrt(jnp.mean(xf * xf, axis=-1, keepdims=True) + eps)
    return (y * g.astype(jnp.float32)).astype(x.dtype)


def layer_norm(x, g, b, eps=LN_EPS):
    xf = x.astype(jnp.float32)
    mu = jnp.mean(xf, axis=-1, keepdims=True)
    var = jnp.mean(jnp.square(xf - mu), axis=-1, keepdims=True)
    y = (xf - mu) * lax.rsqrt(var + eps) * g.astype(jnp.float32) + b.astype(jnp.float32)
    return y.astype(x.dtype)


def causal_depthwise_conv(x, w):
    k_width, ch = w.shape
    xp = jnp.pad(x, ((0, 0), (k_width - 1, 0), (0, 0)))
    return lax.conv_general_dilated(
        xp, w[:, None, :].astype(x.dtype), window_strides=(1,), padding='VALID',
        dimension_numbers=('NWC', 'WIO', 'NWC'), feature_group_count=ch)


def t5_causal_bucket(n):
    max_exact = NUM_BUCKETS // 2
    nf = jnp.maximum(n, 1).astype(jnp.float32)
    large = max_exact + (jnp.log(nf / max_exact) / math.log(REL_MAX_DISTANCE / max_exact)
                         * (NUM_BUCKETS - max_exact)).astype(jnp.int32)
    return jnp.where(n < max_exact, n, jnp.minimum(large, NUM_BUCKETS - 1))


def dilated_group_attention(q, k, v, bias_table, window, dilation):
    bsz, seq, heads, dh = q.shape
    steps = window // dilation
    span = steps * dilation
    seq_pad = -(-seq // span) * span
    n_blk = seq_pad // span

    def to_blocks(t):
        t = jnp.pad(t, ((0, 0), (0, seq_pad - seq), (0, 0), (0, 0)))
        t = t.reshape(bsz, n_blk, steps, dilation, heads, dh)
        return t.transpose(0, 3, 4, 1, 2, 5)

    def with_prev(t):
        prev = jnp.pad(t[:, :, :, :-1], ((0, 0), (0, 0), (0, 0), (1, 0), (0, 0), (0, 0)))
        return jnp.concatenate([prev, t], axis=4)

    qb = to_blocks(q)
    kc = with_prev(to_blocks(k))
    vc = with_prev(to_blocks(v))
    scores = jnp.einsum('bdhnqe,bdhnke->bdhnqk', qb, kc,
                        preferred_element_type=jnp.float32) * (dh ** -0.5)

    a_idx = jnp.arange(steps)[:, None]
    c_idx = jnp.arange(2 * steps)[None, :]
    m = a_idx + steps - c_idx
    band = (m >= 0) & (m <= steps)
    first = (jnp.arange(n_blk) == 0)[:, None, None] & (c_idx < steps)[None]
    valid = band[None] & ~first
    bucket = t5_causal_bucket(jnp.clip(m, 0, steps) * dilation)
    bias = bias_table[bucket].astype(jnp.float32).transpose(2, 0, 1)

    logits = jnp.where(valid, scores + bias[:, None], NEG_INF)
    mx = jnp.max(logits, axis=-1, keepdims=True)
    p = jnp.exp(logits - mx)
    den = jnp.sum(p, axis=-1)
    out = jnp.einsum('bdhnqk,bdhnke->bdhnqe', p, vc.astype(jnp.float32)) / den[..., None]
    lse = mx[..., 0] + jnp.log(den)
    out = out.transpose(0, 3, 4, 1, 2, 5).reshape(bsz, seq_pad, heads, dh)[:, :seq]
    lse = lse.transpose(0, 3, 4, 1, 2).reshape(bsz, seq_pad, heads)[:, :seq]
    return out, lse


def dilated_mixture_attention(qkv, rel_bias):
    bsz, seq, _ = qkv.shape
    qkv = qkv.reshape(bsz, seq, 3, N_GROUPS, HEADS_PER_GROUP, HEAD_DIM)
    outs, lses = [], []
    for g, (window, dilation) in enumerate(DILATED_GROUPS):
        o, l = dilated_group_attention(
            qkv[:, :, 0, g], qkv[:, :, 1, g], qkv[:, :, 2, g],
            rel_bias[:, g * HEADS_PER_GROUP:(g + 1) * HEADS_PER_GROUP], window, dilation)
        outs.append(o)
        lses.append(l)
    wts = jax.nn.softmax(jnp.stack(lses, axis=0), axis=0)
    out = jnp.sum(wts[..., None] * jnp.stack(outs, axis=0), axis=0)
    return out.reshape(bsz, seq, ATTN_OUT)


def conformer_conv_module(u, conv_w, conv_b, ln_g, ln_b):
    val, gate = jnp.split(u, 2, axis=-1)
    h = val * jax.nn.sigmoid(gate)
    h = causal_depthwise_conv(h, conv_w) + conv_b.astype(h.dtype)
    h = layer_norm(h, ln_g, ln_b)
    return jax.nn.silu(h)


def short_gated_conv(u, conv_w):
    b_gate, c_gate, val = jnp.split(u, 3, axis=-1)
    return b_gate * causal_depthwise_conv(c_gate * val, conv_w)


def _fwd_setup_inputs(seed: int = 0) -> dict:
    key = jax.random.key(seed)
    ks = jax.random.split(key, 17)
    D = D_MODEL

    def nrm(k, shape, scale):
        return scale * jax.random.normal(k, shape, jnp.float32)

    return {
        'x': nrm(ks[0], (BATCH, SEQ, D), 1.0),
        'rel_bias': nrm(ks[1], (NUM_BUCKETS, N_ATTN_HEADS), 0.5),
        'ab_norm': 1.0 + nrm(ks[2], (N_EVEN, D), 0.01),
        'ab_w_in': nrm(ks[3], (N_EVEN, D, AB_IN), D ** -0.5),
        'ab_conv_w': nrm(ks[4], (N_EVEN, CONV_A_WIDTH, CONV_A_CH), CONV_A_WIDTH ** -0.5),
        'ab_conv_b': nrm(ks[5], (N_EVEN, CONV_A_CH), 0.01),
        'ab_ln_g': 1.0 + nrm(ks[6], (N_EVEN, CONV_A_CH), 0.01),
        'ab_ln_b': nrm(ks[7], (N_EVEN, CONV_A_CH), 0.01),
        'ab_w_out': nrm(ks[8], (N_EVEN, AB_CAT, D), AB_CAT ** -0.5),
        'sc_norm': 1.0 + nrm(ks[9], (N_ODD, D), 0.01),
        'sc_w_in': nrm(ks[10], (N_ODD, D, 3 * SC_WIDTH), D ** -0.5),
        'sc_conv_w': nrm(ks[11], (N_ODD, SC_CONV_WIDTH, SC_WIDTH), SC_CONV_WIDTH ** -0.5),
        'sc_w_out': nrm(ks[12], (N_ODD, SC_WIDTH, D), SC_WIDTH ** -0.5),
        'mlp_norm': 1.0 + nrm(ks[13], (DEPTH, D), 0.01),
        'mlp_w_up': nrm(ks[14], (DEPTH, D, D_FF), D ** -0.5),
        'mlp_w_down': nrm(ks[15], (DEPTH, D_FF, D), D_FF ** -0.5),
        'final_norm': 1.0 + nrm(ks[16], (D,), 0.01),
    }


def _fwd_reference(x, rel_bias, ab_norm, ab_w_in, ab_conv_w, ab_conv_b, ab_ln_g, ab_ln_b,
              ab_w_out, sc_norm, sc_w_in, sc_conv_w, sc_w_out, mlp_norm, mlp_w_up,
              mlp_w_down, final_norm):
    h = x
    for layer in range(DEPTH):
        i = layer // 2
        if layer % 2 == 0:
            u = jnp.einsum('bsd,de->bse', rms_norm(h, ab_norm[i]), ab_w_in[i])
            ya = conformer_conv_module(u[..., :2 * CONV_A_CH], ab_conv_w[i], ab_conv_b[i],
                                       ab_ln_g[i], ab_ln_b[i])
            yb = dilated_mixture_attention(u[..., 2 * CONV_A_CH:], rel_bias).astype(ya.dtype)
            y = jnp.einsum('bsc,cd->bsd', jnp.concatenate([ya, yb], axis=-1), ab_w_out[i])
        else:
            u = jnp.einsum('bsd,de->bse', rms_norm(h, sc_norm[i]), sc_w_in[i])
            y = jnp.einsum('bsc,cd->bsd', short_gated_conv(u, sc_conv_w[i]), sc_w_out[i])
        h = h + y.astype(h.dtype)
        z = jnp.einsum('bsd,df->bsf', rms_norm(h, mlp_norm[layer]), mlp_w_up[layer])
        z = jnp.square(jax.nn.relu(z))
        h = h + jnp.einsum('bsf,fd->bsd', z, mlp_w_down[layer]).astype(h.dtype)
    return rms_norm(h, final_norm)


import jax as _jax
import jax.numpy as _jnp

TWIN_FORMAT = 'train_step'
FWD_PARAMS = ['x', 'rel_bias', 'ab_norm', 'ab_w_in', 'ab_conv_w', 'ab_conv_b', 'ab_ln_g', 'ab_ln_b', 'ab_w_out', 'sc_norm', 'sc_w_in', 'sc_conv_w', 'sc_w_out', 'mlp_norm', 'mlp_w_up', 'mlp_w_down', 'final_norm']
TWIN_WEIGHTS = ['rel_bias', 'ab_norm', 'ab_w_in', 'ab_conv_w', 'ab_conv_b', 'ab_ln_g', 'ab_ln_b', 'ab_w_out', 'sc_norm', 'sc_w_in', 'sc_conv_w', 'sc_w_out', 'mlp_norm', 'mlp_w_up', 'mlp_w_down', 'final_norm']
TWIN_DIFF_INPUT = 'x'
TWIN_INPUTS = ['x', 'rel_bias', 'ab_norm', 'ab_w_in', 'ab_conv_w', 'ab_conv_b', 'ab_ln_g', 'ab_ln_b', 'ab_w_out', 'sc_norm', 'sc_w_in', 'sc_conv_w', 'sc_w_out', 'mlp_norm', 'mlp_w_up', 'mlp_w_down', 'final_norm', 'loss_target', 'm_rel_bias', 'm_ab_norm', 'm_ab_w_in', 'm_ab_conv_w', 'm_ab_conv_b', 'm_ab_ln_g', 'm_ab_ln_b', 'm_ab_w_out', 'm_sc_norm', 'm_sc_w_in', 'm_sc_conv_w', 'm_sc_w_out', 'm_mlp_norm', 'm_mlp_w_up', 'm_mlp_w_down', 'm_final_norm', 'v_rel_bias', 'v_ab_norm', 'v_ab_w_in', 'v_ab_conv_w', 'v_ab_conv_b', 'v_ab_ln_g', 'v_ab_ln_b', 'v_ab_w_out', 'v_sc_norm', 'v_sc_w_in', 'v_sc_conv_w', 'v_sc_w_out', 'v_mlp_norm', 'v_mlp_w_up', 'v_mlp_w_down', 'v_final_norm']
TWIN_OUTPUTS = ['loss', 'grad_x', 'grad_rel_bias', 'grad_ab_norm', 'grad_ab_w_in', 'grad_ab_conv_w', 'grad_ab_conv_b', 'grad_ab_ln_g', 'grad_ab_ln_b', 'grad_ab_w_out', 'grad_sc_norm', 'grad_sc_w_in', 'grad_sc_conv_w', 'grad_sc_w_out', 'grad_mlp_norm', 'grad_mlp_w_up', 'grad_mlp_w_down', 'grad_final_norm', 'delta_rel_bias', 'delta_ab_norm', 'delta_ab_w_in', 'delta_ab_conv_w', 'delta_ab_conv_b', 'delta_ab_ln_g', 'delta_ab_ln_b', 'delta_ab_w_out', 'delta_sc_norm', 'delta_sc_w_in', 'delta_sc_conv_w', 'delta_sc_w_out', 'delta_mlp_norm', 'delta_mlp_w_up', 'delta_mlp_w_down', 'delta_final_norm', 'new_m_rel_bias', 'new_m_ab_norm', 'new_m_ab_w_in', 'new_m_ab_conv_w', 'new_m_ab_conv_b', 'new_m_ab_ln_g', 'new_m_ab_ln_b', 'new_m_ab_w_out', 'new_m_sc_norm', 'new_m_sc_w_in', 'new_m_sc_conv_w', 'new_m_sc_w_out', 'new_m_mlp_norm', 'new_m_mlp_w_up', 'new_m_mlp_w_down', 'new_m_final_norm', 'new_v_rel_bias', 'new_v_ab_norm', 'new_v_ab_w_in', 'new_v_ab_conv_w', 'new_v_ab_conv_b', 'new_v_ab_ln_g', 'new_v_ab_ln_b', 'new_v_ab_w_out', 'new_v_sc_norm', 'new_v_sc_w_in', 'new_v_sc_conv_w', 'new_v_sc_w_out', 'new_v_mlp_norm', 'new_v_mlp_w_up', 'new_v_mlp_w_down', 'new_v_final_norm']
TWIN_LEAF_KINDS = {'loss': 'loss', 'grad_x': 'grad_x', 'grad_rel_bias': 'grad_w', 'grad_ab_norm': 'grad_w', 'grad_ab_w_in': 'grad_w', 'grad_ab_conv_w': 'grad_w', 'grad_ab_conv_b': 'grad_w', 'grad_ab_ln_g': 'grad_w', 'grad_ab_ln_b': 'grad_w', 'grad_ab_w_out': 'grad_w', 'grad_sc_norm': 'grad_w', 'grad_sc_w_in': 'grad_w', 'grad_sc_conv_w': 'grad_w', 'grad_sc_w_out': 'grad_w', 'grad_mlp_norm': 'grad_w', 'grad_mlp_w_up': 'grad_w', 'grad_mlp_w_down': 'grad_w', 'grad_final_norm': 'grad_w', 'delta_rel_bias': 'delta_w', 'delta_ab_norm': 'delta_w', 'delta_ab_w_in': 'delta_w', 'delta_ab_conv_w': 'delta_w', 'delta_ab_conv_b': 'delta_w', 'delta_ab_ln_g': 'delta_w', 'delta_ab_ln_b': 'delta_w', 'delta_ab_w_out': 'delta_w', 'delta_sc_norm': 'delta_w', 'delta_sc_w_in': 'delta_w', 'delta_sc_conv_w': 'delta_w', 'delta_sc_w_out': 'delta_w', 'delta_mlp_norm': 'delta_w', 'delta_mlp_w_up': 'delta_w', 'delta_mlp_w_down': 'delta_w', 'delta_final_norm': 'delta_w', 'new_m_rel_bias': 'new_m', 'new_m_ab_norm': 'new_m', 'new_m_ab_w_in': 'new_m', 'new_m_ab_conv_w': 'new_m', 'new_m_ab_conv_b': 'new_m', 'new_m_ab_ln_g': 'new_m', 'new_m_ab_ln_b': 'new_m', 'new_m_ab_w_out': 'new_m', 'new_m_sc_norm': 'new_m', 'new_m_sc_w_in': 'new_m', 'new_m_sc_conv_w': 'new_m', 'new_m_sc_w_out': 'new_m', 'new_m_mlp_norm': 'new_m', 'new_m_mlp_w_up': 'new_m', 'new_m_mlp_w_down': 'new_m', 'new_m_final_norm': 'new_m', 'new_v_rel_bias': 'new_v', 'new_v_ab_norm': 'new_v', 'new_v_ab_w_in': 'new_v', 'new_v_ab_conv_w': 'new_v', 'new_v_ab_conv_b': 'new_v', 'new_v_ab_ln_g': 'new_v', 'new_v_ab_ln_b': 'new_v', 'new_v_ab_w_out': 'new_v', 'new_v_sc_norm': 'new_v', 'new_v_sc_w_in': 'new_v', 'new_v_sc_conv_w': 'new_v', 'new_v_sc_w_out': 'new_v', 'new_v_mlp_norm': 'new_v', 'new_v_mlp_w_up': 'new_v', 'new_v_mlp_w_down': 'new_v', 'new_v_final_norm': 'new_v'}


def _forward(args):
    return _fwd_reference(*[args[k] for k in FWD_PARAMS])


def _output_shape():
    def fwd():
        inp = _fwd_setup_inputs(0)
        return _fwd_reference(*[inp[k] for k in FWD_PARAMS])
    out = _jax.eval_shape(fwd)
    return out.shape, out.dtype

N_MICROBATCH = 1
ADAM_LR = 0.001
ADAM_B1 = 0.9
ADAM_B2 = 0.999
ADAM_EPS = 1e-08
ADAM_WD = 0.01
ADAM_STEP = 10
PER_EXAMPLE_BATCH_AXIS = {'x': 0, 'loss_target': 0}
SHARED_INPUTS = []
_WEIGHT_DTYPES = {'rel_bias': _jnp.float32, 'ab_norm': _jnp.float32, 'ab_w_in': _jnp.float32, 'ab_conv_w': _jnp.float32, 'ab_conv_b': _jnp.float32, 'ab_ln_g': _jnp.float32, 'ab_ln_b': _jnp.float32, 'ab_w_out': _jnp.float32, 'sc_norm': _jnp.float32, 'sc_w_in': _jnp.float32, 'sc_conv_w': _jnp.float32, 'sc_w_out': _jnp.float32, 'mlp_norm': _jnp.float32, 'mlp_w_up': _jnp.float32, 'mlp_w_down': _jnp.float32, 'final_norm': _jnp.float32}
MOMENT_SCALE = {'rel_bias': 4.285274e-02, 'ab_norm': 1.724618e-01, 'ab_w_in': 7.172616e-02, 'ab_conv_w': 2.001832e-01, 'ab_conv_b': 3.834890e-01, 'ab_ln_g': 2.290558e-01, 'ab_ln_b': 2.277474e-01, 'ab_w_out': 1.435058e-01, 'sc_norm': 2.535647e-01, 'sc_w_in': 1.363678e-01, 'sc_conv_w': 1.409458e-01, 'sc_w_out': 1.343003e-01, 'mlp_norm': 2.275078e-01, 'mlp_w_up': 1.144348e-01, 'mlp_w_down': 2.081133e-01, 'final_norm': 6.473620e+01}


def _to_microbatches(a, axis):
    t = _jnp.moveaxis(a, axis, 0)
    t = t.reshape((N_MICROBATCH, t.shape[0] // N_MICROBATCH) + t.shape[1:])
    return _jnp.moveaxis(t, 1, axis + 1)


def setup_inputs(seed: int = 0) -> dict:
    inp = _fwd_setup_inputs(seed)
    key = _jax.random.fold_in(_jax.random.key(seed), 7919)
    shape, _ = _output_shape()
    out = dict(inp)
    out["loss_target"] = _jax.random.normal(_jax.random.fold_in(key, 0), shape, _jnp.float32)
    for i, name in enumerate(TWIN_WEIGHTS):
        w = inp[name].astype(_jnp.float32)
        if MOMENT_SCALE is None:
            s = _jnp.sqrt(_jnp.mean(_jnp.square(w)) + 1e-30)
        else:
            s = MOMENT_SCALE[name]
        km, kv = _jax.random.split(_jax.random.fold_in(key, i + 1))
        out[name] = w
        out["m_" + name] = s * _jax.random.normal(km, w.shape, _jnp.float32)
        out["v_" + name] = (s * s) * _jax.random.uniform(kv, w.shape, _jnp.float32, 0.5, 1.5)
    if N_MICROBATCH > 1:
        for name, axis in PER_EXAMPLE_BATCH_AXIS.items():
            out[name] = _to_microbatches(out[name], axis)
    return {'x': out['x'], 'rel_bias': out['rel_bias'], 'ab_norm': out['ab_norm'], 'ab_w_in': out['ab_w_in'], 'ab_conv_w': out['ab_conv_w'], 'ab_conv_b': out['ab_conv_b'], 'ab_ln_g': out['ab_ln_g'], 'ab_ln_b': out['ab_ln_b'], 'ab_w_out': out['ab_w_out'], 'sc_norm': out['sc_norm'], 'sc_w_in': out['sc_w_in'], 'sc_conv_w': out['sc_conv_w'], 'sc_w_out': out['sc_w_out'], 'mlp_norm': out['mlp_norm'], 'mlp_w_up': out['mlp_w_up'], 'mlp_w_down': out['mlp_w_down'], 'final_norm': out['final_norm'], 'loss_target': out['loss_target'], 'm_rel_bias': out['m_rel_bias'], 'm_ab_norm': out['m_ab_norm'], 'm_ab_w_in': out['m_ab_w_in'], 'm_ab_conv_w': out['m_ab_conv_w'], 'm_ab_conv_b': out['m_ab_conv_b'], 'm_ab_ln_g': out['m_ab_ln_g'], 'm_ab_ln_b': out['m_ab_ln_b'], 'm_ab_w_out': out['m_ab_w_out'], 'm_sc_norm': out['m_sc_norm'], 'm_sc_w_in': out['m_sc_w_in'], 'm_sc_conv_w': out['m_sc_conv_w'], 'm_sc_w_out': out['m_sc_w_out'], 'm_mlp_norm': out['m_mlp_norm'], 'm_mlp_w_up': out['m_mlp_w_up'], 'm_mlp_w_down': out['m_mlp_w_down'], 'm_final_norm': out['m_final_norm'], 'v_rel_bias': out['v_rel_bias'], 'v_ab_norm': out['v_ab_norm'], 'v_ab_w_in': out['v_ab_w_in'], 'v_ab_conv_w': out['v_ab_conv_w'], 'v_ab_conv_b': out['v_ab_conv_b'], 'v_ab_ln_g': out['v_ab_ln_g'], 'v_ab_ln_b': out['v_ab_ln_b'], 'v_ab_w_out': out['v_ab_w_out'], 'v_sc_norm': out['v_sc_norm'], 'v_sc_w_in': out['v_sc_w_in'], 'v_sc_conv_w': out['v_sc_conv_w'], 'v_sc_w_out': out['v_sc_w_out'], 'v_mlp_norm': out['v_mlp_norm'], 'v_mlp_w_up': out['v_mlp_w_up'], 'v_mlp_w_down': out['v_mlp_w_down'], 'v_final_norm': out['v_final_norm']}


def _loss(weights, diff, rest, loss_target):
    with _jax.named_scope("forward"):
        args = {**rest, TWIN_DIFF_INPUT: diff, **{k: w.astype(_WEIGHT_DTYPES[k]) for k, w in weights.items()}}
        y = _forward(args)
    with _jax.named_scope("loss_head"):
        err = _jnp.square(y.astype(_jnp.float32) - loss_target)
        return 0.5 * _jnp.sum(_jnp.mean(err, axis=-1)) if err.ndim else 0.5 * err


def _adamw(w, g, m, v):
    m = ADAM_B1 * m + (1.0 - ADAM_B1) * g
    v = ADAM_B2 * v + (1.0 - ADAM_B2) * _jnp.square(g)
    m_hat = m / (1.0 - ADAM_B1 ** ADAM_STEP)
    v_hat = v / (1.0 - ADAM_B2 ** ADAM_STEP)
    delta = -ADAM_LR * (m_hat / (_jnp.sqrt(v_hat) + ADAM_EPS) + ADAM_WD * w)
    return delta, m, v


def reference(x, rel_bias, ab_norm, ab_w_in, ab_conv_w, ab_conv_b, ab_ln_g, ab_ln_b, ab_w_out, sc_norm, sc_w_in, sc_conv_w, sc_w_out, mlp_norm, mlp_w_up, mlp_w_down, final_norm, loss_target, m_rel_bias, m_ab_norm, m_ab_w_in, m_ab_conv_w, m_ab_conv_b, m_ab_ln_g, m_ab_ln_b, m_ab_w_out, m_sc_norm, m_sc_w_in, m_sc_conv_w, m_sc_w_out, m_mlp_norm, m_mlp_w_up, m_mlp_w_down, m_final_norm, v_rel_bias, v_ab_norm, v_ab_w_in, v_ab_conv_w, v_ab_conv_b, v_ab_ln_g, v_ab_ln_b, v_ab_w_out, v_sc_norm, v_sc_w_in, v_sc_conv_w, v_sc_w_out, v_mlp_norm, v_mlp_w_up, v_mlp_w_down, v_final_norm):
    given = dict(x=x, rel_bias=rel_bias, ab_norm=ab_norm, ab_w_in=ab_w_in, ab_conv_w=ab_conv_w, ab_conv_b=ab_conv_b, ab_ln_g=ab_ln_g, ab_ln_b=ab_ln_b, ab_w_out=ab_w_out, sc_norm=sc_norm, sc_w_in=sc_w_in, sc_conv_w=sc_conv_w, sc_w_out=sc_w_out, mlp_norm=mlp_norm, mlp_w_up=mlp_w_up, mlp_w_down=mlp_w_down, final_norm=final_norm, loss_target=loss_target, m_rel_bias=m_rel_bias, m_ab_norm=m_ab_norm, m_ab_w_in=m_ab_w_in, m_ab_conv_w=m_ab_conv_w, m_ab_conv_b=m_ab_conv_b, m_ab_ln_g=m_ab_ln_g, m_ab_ln_b=m_ab_ln_b, m_ab_w_out=m_ab_w_out, m_sc_norm=m_sc_norm, m_sc_w_in=m_sc_w_in, m_sc_conv_w=m_sc_conv_w, m_sc_w_out=m_sc_w_out, m_mlp_norm=m_mlp_norm, m_mlp_w_up=m_mlp_w_up, m_mlp_w_down=m_mlp_w_down, m_final_norm=m_final_norm, v_rel_bias=v_rel_bias, v_ab_norm=v_ab_norm, v_ab_w_in=v_ab_w_in, v_ab_conv_w=v_ab_conv_w, v_ab_conv_b=v_ab_conv_b, v_ab_ln_g=v_ab_ln_g, v_ab_ln_b=v_ab_ln_b, v_ab_w_out=v_ab_w_out, v_sc_norm=v_sc_norm, v_sc_w_in=v_sc_w_in, v_sc_conv_w=v_sc_conv_w, v_sc_w_out=v_sc_w_out, v_mlp_norm=v_mlp_norm, v_mlp_w_up=v_mlp_w_up, v_mlp_w_down=v_mlp_w_down, v_final_norm=v_final_norm)
    weights = {n: given[n] for n in TWIN_WEIGHTS}
    shared = {n: given[n] for n in SHARED_INPUTS}
    per_example = {n: given[n] for n in ['x']}
    grad_fn = _jax.value_and_grad(_loss, argnums=(0, 1))

    def one_microbatch(ex, loss_target):
        ex = dict(ex)
        diff = ex.pop(TWIN_DIFF_INPUT)
        return grad_fn(weights, diff, {**shared, **ex}, loss_target)

    if N_MICROBATCH == 1:
        loss, (grad_w, grad_x) = one_microbatch(per_example, given["loss_target"])
    else:
        def body(carry, xs):
            loss_sum, grad_sum = carry
            l_k, (gw_k, gx_k) = one_microbatch(xs[0], xs[1])
            with _jax.named_scope("update"):
                return (loss_sum + l_k, _jax.tree.map(_jnp.add, grad_sum, gw_k)), gx_k

        init = (_jnp.zeros((), _jnp.float32), _jax.tree.map(_jnp.zeros_like, weights))
        (loss, grad_w), grad_x = _jax.lax.scan(body, init, (per_example, given["loss_target"]))
    with _jax.named_scope("update"):
        delta_w, new_m, new_v = {}, {}, {}
        for n in TWIN_WEIGHTS:
            delta_w[n], new_m[n], new_v[n] = _adamw(weights[n], grad_w[n], given["m_" + n], given["v_" + n])
    return (loss, grad_x, *[grad_w[n] for n in TWIN_WEIGHTS], *[delta_w[n] for n in TWIN_WEIGHTS],
            *[new_m[n] for n in TWIN_WEIGHTS], *[new_v[n] for n in TWIN_WEIGHTS])
```

```python
import functools
import math

import numpy as np
import jax
import jax.numpy as jnp
from jax import lax
from jax.experimental import pallas as pl
from jax.experimental.pallas import tpu as pltpu

F32 = jnp.float32
BF16 = jnp.bfloat16

HEAD_DIM = 64
HEADS_PER_GROUP = 8
DILATED_GROUPS = ((128, 1), (512, 4), (2048, 16))
N_GROUPS = 3
ATTN_OUT = HEADS_PER_GROUP * HEAD_DIM
ATTN_IN = 3 * N_GROUPS * ATTN_OUT
GROUP_QKV = 3 * ATTN_OUT
ATTN_BLK = 128
CONV_A_WIDTH = 31
SC_CONV_WIDTH = 3
NUM_BUCKETS = 32
REL_MAX_DISTANCE = 2048
RMS_EPS = 1e-6
LN_EPS = 1e-5
NEG_INF = -1e30
ADAM_LR = 0.001
ADAM_B1 = 0.9
ADAM_B2 = 0.999
ADAM_EPS = 1e-08
ADAM_WD = 0.01
ADAM_STEP = 10

N_DEV = 8
HALO = 32
LANES = 128
VMEM_LIMIT = 56 * 1024 * 1024
MESH = pl.DeviceIdType.MESH
ANY = pl.BlockSpec(memory_space=pl.ANY)
VMEM_SPEC = pl.BlockSpec(memory_space=pltpu.VMEM)


def _tile(n, prefs):
    for t in prefs:
        if n % t == 0:
            return t
    return n


def _cparams(*sem):
    return pltpu.CompilerParams(dimension_semantics=sem, vmem_limit_bytes=VMEM_LIMIT)


def _relu_sq(z):
    return jnp.square(jnp.maximum(z, 0))


def _dot_nt(a, b):
    return lax.dot_general(a, b, (((1,), (1,)), ((), ())), preferred_element_type=F32)


def _dot_tn(a, b):
    return lax.dot_general(a, b, (((0,), (0,)), ((), ())), preferred_element_type=F32)


def _mm_nn(a, b, *, out_dtype, name, residual=None, a_fn=None):
    M, K = a.shape
    _, N = b.shape
    tm = _tile(M, (1024, 512, 256))
    tn = _tile(N, (512, 384, 256, 128))
    tk = _tile(K, (1024, 512, 256, 128))
    nk = K // tk
    has_res = residual is not None

    def body(*refs):
        if has_res:
            a_ref, b_ref, r_ref, o_ref = refs[:4]
        else:
            a_ref, b_ref, o_ref = refs[:3]
        av = a_ref[...]
        if a_fn is not None:
            av = a_fn(av)
        part = jnp.dot(av, b_ref[...], preferred_element_type=F32)

        def finish(acc):
            if has_res:
                acc = acc + r_ref[...]
            o_ref[...] = acc.astype(o_ref.dtype)

        if nk == 1:
            finish(part)
        else:
            acc_ref = refs[-1]
            k = pl.program_id(2)

            @pl.when(k == 0)
            def _():
                acc_ref[...] = part

            @pl.when((k > 0) & (k < nk - 1))
            def _():
                acc_ref[...] += part

            @pl.when(k == nk - 1)
            def _():
                finish(acc_ref[...] + part)

    in_specs = [pl.BlockSpec((tm, tk), lambda i, j, k: (i, k)),
                pl.BlockSpec((tk, tn), lambda i, j, k: (k, j))]
    args = [a, b]
    if has_res:
        in_specs.append(pl.BlockSpec((tm, tn), lambda i, j, k: (i, j)))
        args.append(residual)
    return pl.pallas_call(
        body, name=name,
        out_shape=jax.ShapeDtypeStruct((M, N), out_dtype),
        grid=(M // tm, N // tn, nk),
        in_specs=in_specs,
        out_specs=pl.BlockSpec((tm, tn), lambda i, j, k: (i, j)),
        scratch_shapes=[pltpu.VMEM((tm, tn), F32)] if nk > 1 else [],
        compiler_params=_cparams("parallel", "parallel", "arbitrary"),
    )(*args)


def _mm_nt(pairs, *, out_dtype, name, epilogue=None, extra=None):
    M = pairs[0][0].shape[0]
    Ko = pairs[0][1].shape[0]
    tm = _tile(M, (512, 256))
    to = _tile(Ko, (1024, 512, 256, 128))
    tk = 512 if len(pairs) > 1 else _tile(pairs[0][0].shape[1], (1024, 512, 256, 128))
    steps = [p[0].shape[1] // tk for p in pairs]
    offs = [sum(steps[:i]) for i in range(len(pairs))]
    nk = sum(steps)
    npair = len(pairs)
    has_extra = extra is not None

    def body(*refs):
        ab = refs[:2 * npair]
        pos = 2 * npair
        e_ref = None
        if has_extra:
            e_ref = refs[pos]
            pos += 1
        o_ref = refs[pos]
        acc_ref = refs[pos + 1]
        k = pl.program_id(2)

        @pl.when(k == 0)
        def _():
            acc_ref[...] = jnp.zeros_like(acc_ref)

        for p in range(npair):
            @pl.when((k >= offs[p]) & (k < offs[p] + steps[p]))
            def _(p=p):
                acc_ref[...] += _dot_nt(ab[2 * p][...], ab[2 * p + 1][...])

        @pl.when(k == nk - 1)
        def _():
            acc = acc_ref[...]
            if epilogue is not None:
                acc = epilogue(acc, e_ref[...] if has_extra else None)
            o_ref[...] = acc.astype(o_ref.dtype)

    in_specs, args = [], []
    for p, (a, b) in enumerate(pairs):
        def kidx(k, p=p):
            return jnp.clip(k - offs[p], 0, steps[p] - 1)
        in_specs.append(pl.BlockSpec((tm, tk), lambda i, j, k, kidx=kidx: (i, kidx(k))))
        in_specs.append(pl.BlockSpec((to, tk), lambda i, j, k, kidx=kidx: (j, kidx(k))))
        args += [a, b]
    if has_extra:
        in_specs.append(pl.BlockSpec((tm, to), lambda i, j, k: (i, j)))
        args.append(extra)
    return pl.pallas_call(
        body, name=name,
        out_shape=jax.ShapeDtypeStruct((M, Ko), out_dtype),
        grid=(M // tm, Ko // to, nk),
        in_specs=in_specs,
        out_specs=pl.BlockSpec((tm, to), lambda i, j, k: (i, j)),
        scratch_shapes=[pltpu.VMEM((tm, to), F32)],
        compiler_params=_cparams("parallel", "parallel", "arbitrary"),
    )(*args)


def _mm_tn(a, b, *, name, a_fn=None):
    M, K = a.shape
    _, N = b.shape
    tm = _tile(M, (512, 256))
    tk = _tile(K, (1024, 768, 512, 384, 256, 128))
    tn = _tile(N, (1024, 768, 512, 384, 256, 128))
    nm = M // tm

    def body(a_ref, b_ref, o_ref, acc_ref):
        m = pl.program_id(2)
        av = a_ref[...]
        if a_fn is not None:
            av = a_fn(av)
        part = _dot_tn(av, b_ref[...])

        @pl.when(m == 0)
        def _():
            acc_ref[...] = part

        @pl.when((m > 0) & (m < nm - 1))
        def _():
            acc_ref[...] += part

        @pl.when(m == nm - 1)
        def _():
            o_ref[...] = (acc_ref[...] + part).astype(o_ref.dtype)

    return pl.pallas_call(
        body, name=name,
        out_shape=jax.ShapeDtypeStruct((K, N), BF16),
        grid=(K // tk, N // tn, nm),
        in_specs=[pl.BlockSpec((tm, tk), lambda i, j, m: (m, i)),
                  pl.BlockSpec((tm, tn), lambda i, j, m: (m, j))],
        out_specs=pl.BlockSpec((tk, tn), lambda i, j, m: (i, j)),
        scratch_shapes=[pltpu.VMEM((tk, tn), F32)],
        compiler_params=_cparams("parallel", "parallel", "arbitrary"),
    )(a, b)


def _rmsnorm_fwd(h, g, *, name):
    S, D = h.shape
    tm = _tile(S, (512, 256))

    def body(h_ref, g_ref, o_ref):
        x = h_ref[...]
        r = lax.rsqrt(jnp.mean(x * x, axis=-1, keepdims=True) + RMS_EPS)
        o_ref[...] = (x * r * g_ref[...]).astype(o_ref.dtype)

    return pl.pallas_call(
        body, name=name,
        out_shape=jax.ShapeDtypeStruct((S, D), BF16),
        grid=(S // tm,),
        in_specs=[pl.BlockSpec((tm, D), lambda i: (i, 0)), pl.BlockSpec((1, D), lambda i: (0, 0))],
        out_specs=pl.BlockSpec((tm, D), lambda i: (i, 0)),
        compiler_params=_cparams("parallel"),
    )(h, g)


def _rms_bwd_rows(x, g, dy):
    r = lax.rsqrt(jnp.mean(x * x, axis=-1, keepdims=True) + RMS_EPS)
    xh = x * r
    gy = dy * g
    dx = r * (gy - xh * jnp.mean(xh * gy, axis=-1, keepdims=True))
    return dx, dy * xh


def _rms_bwd(x, g, dn, dres, *, name):
    S, D = x.shape
    tm = _tile(S, (256,))

    def body(x_ref, g_ref, dn_ref, dr_ref, dx_ref, dxb_ref, dg_ref):
        i = pl.program_id(0)
        dx, dgx = _rms_bwd_rows(x_ref[...], g_ref[...], dn_ref[...])
        tot = dr_ref[...] + dx
        dx_ref[...] = tot
        dxb_ref[...] = tot.astype(BF16)

        @pl.when(i == 0)
        def _():
            dg_ref[...] = jnp.zeros_like(dg_ref)

        dg_ref[0:1, :] += jnp.sum(dgx, axis=0, keepdims=True)

    row = pl.BlockSpec((tm, D), lambda i: (i, 0))
    return pl.pallas_call(
        body, name=name,
        out_shape=(jax.ShapeDtypeStruct((S, D), F32), jax.ShapeDtypeStruct((S, D), BF16),
                   jax.ShapeDtypeStruct((8, D), F32)),
        grid=(S // tm,),
        in_specs=[row, pl.BlockSpec((1, D), lambda i: (0, 0)), row, row],
        out_specs=(row, row, pl.BlockSpec((8, D), lambda i: (0, 0))),
        compiler_params=_cparams("arbitrary"),
    )(x, g, dn, dres)


def _loss_bwd(h, target, g, *, name):
    S, D = h.shape
    tm = _tile(S, (256,))

    def body(h_ref, t_ref, g_ref, dx_ref, dxb_ref, acc_ref):
        i = pl.program_id(0)
        x = h_ref[...]
        gv = g_ref[...]
        r = lax.rsqrt(jnp.mean(x * x, axis=-1, keepdims=True) + RMS_EPS)
        err = x * r * gv - t_ref[...]
        dx, dgx = _rms_bwd_rows(x, gv, err * (1.0 / D))
        dx_ref[...] = dx
        dxb_ref[...] = dx.astype(BF16)

        @pl.when(i == 0)
        def _():
            acc_ref[...] = jnp.zeros_like(acc_ref)

        acc_ref[0:1, :] += jnp.sum(dgx, axis=0, keepdims=True)
        acc_ref[1:2, :] += jnp.sum(err * err, axis=0, keepdims=True)

    row = pl.BlockSpec((tm, D), lambda i: (i, 0))
    return pl.pallas_call(
        body, name=name,
        out_shape=(jax.ShapeDtypeStruct((S, D), F32), jax.ShapeDtypeStruct((S, D), BF16),
                   jax.ShapeDtypeStruct((8, D), F32)),
        grid=(S // tm,),
        in_specs=[row, row, pl.BlockSpec((1, D), lambda i: (0, 0))],
        out_specs=(row, row, pl.BlockSpec((8, D), lambda i: (0, 0))),
        compiler_params=_cparams("arbitrary"),
    )(h, target, g)


def _conv_fwd(uc, conv_w, conv_b, ln_g, ln_b, *, name):
    S, C2 = uc.shape
    C = C2 // 2
    ts = _tile(S, (256,))
    per = ts // HALO

    def body(cur_ref, halo_ref, w_ref, b_ref, g_ref, beta_ref, ya_ref, h_ref, ct_ref, ext_ref):
        i = pl.program_id(0)
        hh = halo_ref[:, 0:C] * jax.nn.sigmoid(halo_ref[:, C:C2])
        ext_ref[0:HALO, :] = jnp.where(i == 0, 0.0, hh)
        hc = cur_ref[:, 0:C] * jax.nn.sigmoid(cur_ref[:, C:C2])
        ext_ref[HALO:HALO + ts, :] = hc
        h_ref[...] = hc
        acc = jnp.zeros((ts, C), F32)
        for k in range(CONV_A_WIDTH):
            acc = acc + w_ref[k:k + 1, :] * ext_ref[k + 2:k + 2 + ts, :]
        ct = acc + b_ref[...]
        ct_ref[...] = ct
        mu = jnp.mean(ct, axis=-1, keepdims=True)
        xc = ct - mu
        var = jnp.mean(xc * xc, axis=-1, keepdims=True)
        l = xc * lax.rsqrt(var + LN_EPS) * g_ref[...] + beta_ref[...]
        ya_ref[...] = (l * jax.nn.sigmoid(l)).astype(ya_ref.dtype)

    vec = pl.BlockSpec((1, C), lambda i: (0, 0))
    row = pl.BlockSpec((ts, C), lambda i: (i, 0))
    return pl.pallas_call(
        body, name=name,
        out_shape=(jax.ShapeDtypeStruct((S, C), BF16), jax.ShapeDtypeStruct((S, C), F32),
                   jax.ShapeDtypeStruct((S, C), F32)),
        grid=(S // ts,),
        in_specs=[pl.BlockSpec((ts, C2), lambda i: (i, 0)),
                  pl.BlockSpec((HALO, C2), lambda i: (jnp.maximum(i * per - 1, 0), 0)),
                  pl.BlockSpec((CONV_A_WIDTH, C), lambda i: (0, 0)), vec, vec, vec],
        out_specs=(row, row, row),
        scratch_shapes=[pltpu.VMEM((HALO + ts, C), F32)],
        compiler_params=_cparams("parallel"),
    )(uc, uc, conv_w, conv_b, ln_g, ln_b)


CONV_ACC_ROWS = 40


def _conv_bwd_ln(ct, dcat, hglu, ln_g, ln_b, *, name):
    S, C = ct.shape
    CW = dcat.shape[1]
    ts = _tile(S, (256,))
    per = ts // HALO

    def body(ct_ref, dcat_ref, hc_ref, hh_ref, g_ref, beta_ref, dc_ref, acc_ref, ext_ref):
        i = pl.program_id(0)
        ct = ct_ref[...]
        gv = g_ref[...]
        mu = jnp.mean(ct, axis=-1, keepdims=True)
        xc = ct - mu
        rstd = lax.rsqrt(jnp.mean(xc * xc, axis=-1, keepdims=True) + LN_EPS)
        xh = xc * rstd
        l = xh * gv + beta_ref[...]
        sg = jax.nn.sigmoid(l)
        dl = dcat_ref[:, 0:C] * (sg * (1.0 + l * (1.0 - sg)))
        dxh = dl * gv
        dc = rstd * (dxh - jnp.mean(dxh, axis=-1, keepdims=True)
                     - xh * jnp.mean(dxh * xh, axis=-1, keepdims=True))
        dc_ref[...] = dc

        @pl.when(i == 0)
        def _():
            acc_ref[...] = jnp.zeros_like(acc_ref)

        acc_ref[32:33, :] += jnp.sum(dc, axis=0, keepdims=True)
        acc_ref[33:34, :] += jnp.sum(dl * xh, axis=0, keepdims=True)
        acc_ref[34:35, :] += jnp.sum(dl, axis=0, keepdims=True)
        ext_ref[0:HALO, :] = jnp.where(i == 0, 0.0, hh_ref[...])
        ext_ref[HALO:HALO + ts, :] = hc_ref[...]
        for k in range(CONV_A_WIDTH):
            acc_ref[k:k + 1, :] += jnp.sum(dc * ext_ref[k + 2:k + 2 + ts, :], axis=0, keepdims=True)

    vec = pl.BlockSpec((1, C), lambda i: (0, 0))
    row = pl.BlockSpec((ts, C), lambda i: (i, 0))
    return pl.pallas_call(
        body, name=name,
        out_shape=(jax.ShapeDtypeStruct((S, C), F32), jax.ShapeDtypeStruct((CONV_ACC_ROWS, C), F32)),
        grid=(S // ts,),
        in_specs=[row, pl.BlockSpec((ts, CW), lambda i: (i, 0)), row,
                  pl.BlockSpec((HALO, C), lambda i: (jnp.maximum(i * per - 1, 0), 0)), vec, vec],
        out_specs=(row, pl.BlockSpec((CONV_ACC_ROWS, C), lambda i: (0, 0))),
        scratch_shapes=[pltpu.VMEM((HALO + ts, C), F32)],
        compiler_params=_cparams("arbitrary"),
    )(ct, dcat, hglu, hglu, ln_g, ln_b)


def _conv_bwd_in(dc, uc, conv_w, *, name):
    S, C = dc.shape
    C2 = 2 * C
    ts = _tile(S, (256,))
    per = ts // HALO
    nt = S // ts

    def body(dc_ref, dn_ref, uc_ref, w_ref, du_ref, ext_ref):
        i = pl.program_id(0)
        ext_ref[0:ts, :] = dc_ref[...]
        ext_ref[ts:ts + HALO, :] = jnp.where(i == nt - 1, 0.0, dn_ref[...])
        acc = jnp.zeros((ts, C), F32)
        for k in range(CONV_A_WIDTH):
            acc = acc + w_ref[k:k + 1, :] * ext_ref[30 - k:30 - k + ts, :]
        val = uc_ref[:, 0:C]
        sg = jax.nn.sigmoid(uc_ref[:, C:C2])
        du_ref[:, 0:C] = (acc * sg).astype(du_ref.dtype)
        du_ref[:, C:C2] = (acc * val * sg * (1.0 - sg)).astype(du_ref.dtype)

    return pl.pallas_call(
        body, name=name,
        out_shape=jax.ShapeDtypeStruct((S, C2), BF16),
        grid=(nt,),
        in_specs=[pl.BlockSpec((ts, C), lambda i: (i, 0)),
                  pl.BlockSpec((HALO, C), lambda i: (jnp.minimum((i + 1) * per, S // HALO - 1), 0)),
                  pl.BlockSpec((ts, C2), lambda i: (i, 0)),
                  pl.BlockSpec((CONV_A_WIDTH, C), lambda i: (0, 0))],
        out_specs=pl.BlockSpec((ts, C2), lambda i: (i, 0)),
        scratch_shapes=[pltpu.VMEM((ts + HALO, C), F32)],
        compiler_params=_cparams("parallel"),
    )(dc, dc, uc, conv_w)


def _sc_fwd(u3, conv_w, *, name):
    S, W3 = u3.shape
    W = W3 // 3
    ts = _tile(S, (256,))
    per = ts // HALO

    def body(cur_ref, halo_ref, w_ref, y_ref, ext_ref):
        i = pl.program_id(0)
        cvh = halo_ref[:, W:2 * W].astype(F32) * halo_ref[:, 2 * W:W3].astype(F32)
        ext_ref[0:HALO, :] = jnp.where(i == 0, 0.0, cvh)
        ext_ref[HALO:HALO + ts, :] = cur_ref[:, W:2 * W].astype(F32) * cur_ref[:, 2 * W:W3].astype(F32)
        k = (w_ref[0:1, :] * ext_ref[HALO - 2:HALO - 2 + ts, :]
             + w_ref[1:2, :] * ext_ref[HALO - 1:HALO - 1 + ts, :]
             + w_ref[2:3, :] * ext_ref[HALO:HALO + ts, :])
        y_ref[...] = (cur_ref[:, 0:W].astype(F32) * k).astype(y_ref.dtype)

    return pl.pallas_call(
        body, name=name,
        out_shape=jax.ShapeDtypeStruct((S, W), BF16),
        grid=(S // ts,),
        in_specs=[pl.BlockSpec((ts, W3), lambda i: (i, 0)),
                  pl.BlockSpec((HALO, W3), lambda i: (jnp.maximum(i * per - 1, 0), 0)),
                  pl.BlockSpec((SC_CONV_WIDTH, W), lambda i: (0, 0))],
        out_specs=pl.BlockSpec((ts, W), lambda i: (i, 0)),
        scratch_shapes=[pltpu.VMEM((HALO + ts, W), F32)],
        compiler_params=_cparams("parallel"),
    )(u3, u3, conv_w)


def _sc_bwd(u3, dy, conv_w, *, name):
    S, W3 = u3.shape
    W = W3 // 3
    ts = _tile(S, (256,))
    per = ts // HALO
    nt = S // ts

    def body(cur_ref, prev_ref, next_ref, dy_ref, dyn_ref, w_ref, du_ref, dw_ref, cv_ext, dk_ext):
        i = pl.program_id(0)
        cvh = prev_ref[:, W:2 * W].astype(F32) * prev_ref[:, 2 * W:W3].astype(F32)
        cv_ext[0:HALO, :] = jnp.where(i == 0, 0.0, cvh)
        c = cur_ref[:, W:2 * W].astype(F32)
        v = cur_ref[:, 2 * W:W3].astype(F32)
        b = cur_ref[:, 0:W].astype(F32)
        cv_ext[HALO:HALO + ts, :] = c * v
        dy_cur = dy_ref[...]
        dk = dy_cur * b
        dk_ext[0:ts, :] = dk
        dk_ext[ts:ts + HALO, :] = jnp.where(i == nt - 1, 0.0, dyn_ref[...] * next_ref[:, 0:W].astype(F32))
        w0, w1, w2 = w_ref[0:1, :], w_ref[1:2, :], w_ref[2:3, :]
        cv2 = cv_ext[HALO - 2:HALO - 2 + ts, :]
        cv1 = cv_ext[HALO - 1:HALO - 1 + ts, :]
        cv0 = cv_ext[HALO:HALO + ts, :]
        kconv = w0 * cv2 + w1 * cv1 + w2 * cv0
        dcv = w2 * dk + w1 * dk_ext[1:1 + ts, :] + w0 * dk_ext[2:2 + ts, :]
        du_ref[:, 0:W] = (dy_cur * kconv).astype(du_ref.dtype)
        du_ref[:, W:2 * W] = (dcv * v).astype(du_ref.dtype)
        du_ref[:, 2 * W:W3] = (dcv * c).astype(du_ref.dtype)

        @pl.when(i == 0)
        def _():
            dw_ref[...] = jnp.zeros_like(dw_ref)

        dw_ref[0:1, :] += jnp.sum(dk * cv2, axis=0, keepdims=True)
        dw_ref[1:2, :] += jnp.sum(dk * cv1, axis=0, keepdims=True)
        dw_ref[2:3, :] += jnp.sum(dk * cv0, axis=0, keepdims=True)

    nxt = lambda i: (jnp.minimum((i + 1) * per, S // HALO - 1), 0)
    return pl.pallas_call(
        body, name=name,
        out_shape=(jax.ShapeDtypeStruct((S, W3), BF16), jax.ShapeDtypeStruct((8, W), F32)),
        grid=(nt,),
        in_specs=[pl.BlockSpec((ts, W3), lambda i: (i, 0)),
                  pl.BlockSpec((HALO, W3), lambda i: (jnp.maximum(i * per - 1, 0), 0)),
                  pl.BlockSpec((HALO, W3), nxt),
                  pl.BlockSpec((ts, W), lambda i: (i, 0)),
                  pl.BlockSpec((HALO, W), nxt),
                  pl.BlockSpec((SC_CONV_WIDTH, W), lambda i: (0, 0))],
        out_specs=(pl.BlockSpec((ts, W3), lambda i: (i, 0)), pl.BlockSpec((8, W), lambda i: (0, 0))),
        scratch_shapes=[pltpu.VMEM((HALO + ts, W), F32), pltpu.VMEM((ts + HALO, W), F32)],
        compiler_params=_cparams("arbitrary"),
    )(u3, u3, u3, dy, dy, conv_w)


def _t5_causal_bucket(n):
    max_exact = NUM_BUCKETS // 2
    nf = jnp.maximum(n, 1).astype(F32)
    large = max_exact + (jnp.log(nf / max_exact) / math.log(REL_MAX_DISTANCE / max_exact)
                         * (NUM_BUCKETS - max_exact)).astype(jnp.int32)
    return jnp.where(n < max_exact, n, jnp.minimum(large, NUM_BUCKETS - 1))


def _bias_tables(rel_bias, g, dilation):
    steps = ATTN_BLK
    a_idx = jnp.arange(steps)[:, None]
    c_idx = jnp.arange(2 * steps)[None, :]
    m = a_idx + steps - c_idx
    band = (m >= 0) & (m <= steps)
    bucket = _t5_causal_bucket(jnp.clip(m, 0, steps) * dilation)
    tab = rel_bias[:, g * HEADS_PER_GROUP:(g + 1) * HEADS_PER_GROUP]
    bias = tab[bucket].astype(F32).transpose(2, 0, 1)
    normal = jnp.where(band[None], bias, NEG_INF)
    first = jnp.where((band & (c_idx >= steps))[None], bias, NEG_INF)
    return jnp.stack([first, normal]), bucket.astype(F32)


def _lane_is_low():
    return lax.broadcasted_iota(jnp.int32, (1, LANES), 1) < HEAD_DIM


def _qkv_specs(g):
    nqb = ATTN_IN // ATTN_OUT

    def spec(t, prev):
        def idx(r, n):
            row = jnp.maximum(n - 1, 0) if prev else n
            return (row, r * nqb + t * N_GROUPS + g)
        return pl.BlockSpec((ATTN_BLK, ATTN_OUT), idx)

    return [spec(0, False), spec(1, False), spec(1, True), spec(2, False), spec(2, True)]


def _attn_fwd(uq, bias, g, d, *, name):
    S = uq.shape[0]
    rows = S // d
    uv = uq.reshape(rows, d * ATTN_IN)

    def body(q_ref, kc_ref, kp_ref, vc_ref, vp_ref, bias_ref, o_ref, l_ref):
        n = pl.program_id(1)
        sel = jnp.minimum(n, 1)
        low = _lane_is_low()
        for hp in range(HEADS_PER_GROUP // 2):
            sl = slice(hp * LANES, (hp + 1) * LANES)
            q2 = q_ref[:, sl]
            k2 = jnp.concatenate([kp_ref[:, sl], kc_ref[:, sl]], axis=0)
            v2 = jnp.concatenate([vp_ref[:, sl], vc_ref[:, sl]], axis=0)
            outs, lses = [], []
            for hh in range(2):
                msk = low if hh == 0 else jnp.logical_not(low)
                qm = jnp.where(msk, q2, jnp.zeros_like(q2))
                logits = _dot_nt(qm, k2) * (HEAD_DIM ** -0.5) + bias_ref[sel, 2 * hp + hh]
                mx = jnp.max(logits, axis=-1, keepdims=True)
                p = jnp.exp(logits - mx)
                den = jnp.sum(p, axis=-1, keepdims=True)
                pv = jnp.dot(p.astype(BF16), v2, preferred_element_type=F32)
                outs.append(pv / den)
                lses.append(jnp.broadcast_to(mx + jnp.log(den), (ATTN_BLK, LANES)))
            o_ref[:, sl] = jnp.where(low, outs[0], outs[1])
            l_ref[:, sl] = jnp.where(low, lses[0], lses[1])

    out_spec = pl.BlockSpec((ATTN_BLK, ATTN_OUT), lambda r, n: (n, r))
    o, l = pl.pallas_call(
        body, name=name,
        out_shape=(jax.ShapeDtypeStruct((rows, d * ATTN_OUT), F32),) * 2,
        grid=(d, rows // ATTN_BLK),
        in_specs=_qkv_specs(g) + [pl.BlockSpec((2, HEADS_PER_GROUP, ATTN_BLK, 2 * ATTN_BLK),
                                               lambda r, n: (0, 0, 0, 0))],
        out_specs=(out_spec, out_spec),
        compiler_params=_cparams("parallel", "parallel"),
    )(uv, uv, uv, uv, uv, bias)
    return o.reshape(S, ATTN_OUT), l.reshape(S, ATTN_OUT)


def _attn_merge(outs, lses, ya, *, name):
    S, C = ya.shape
    tm = _tile(S, (256,))

    def body(o0, o1, o2, l0, l1, l2, ya_ref, cat_ref, out_ref, lse_ref):
        a0, a1, a2 = l0[...], l1[...], l2[...]
        m = jnp.maximum(jnp.maximum(a0, a1), a2)
        e0, e1, e2 = jnp.exp(a0 - m), jnp.exp(a1 - m), jnp.exp(a2 - m)
        den = e0 + e1 + e2
        out = (e0 * o0[...] + e1 * o1[...] + e2 * o2[...]) / den
        out_ref[...] = out
        lse_ref[...] = m + jnp.log(den)
        cat_ref[:, 0:C] = ya_ref[...]
        cat_ref[:, C:C + ATTN_OUT] = out.astype(cat_ref.dtype)

    blk = pl.BlockSpec((tm, ATTN_OUT), lambda i: (i, 0))
    return pl.pallas_call(
        body, name=name,
        out_shape=(jax.ShapeDtypeStruct((S, C + ATTN_OUT), BF16), jax.ShapeDtypeStruct((S, ATTN_OUT), F32),
                   jax.ShapeDtypeStruct((S, ATTN_OUT), F32)),
        grid=(S // tm,),
        in_specs=[blk] * 6 + [pl.BlockSpec((tm, C), lambda i: (i, 0))],
        out_specs=(pl.BlockSpec((tm, C + ATTN_OUT), lambda i: (i, 0)), blk, blk),
        compiler_params=_cparams("parallel"),
    )(*outs, *lses, ya)


def _attn_prep(dcat, outf, *, name):
    S, CW = dcat.shape
    C = CW - ATTN_OUT
    tm = _tile(S, (256,))
    ones = np.kron(np.eye(HEADS_PER_GROUP, dtype=np.float32), np.ones((HEAD_DIM, HEAD_DIM), np.float32))

    def body(dcat_ref, out_ref, ones_ref, dyb_ref, dl_ref):
        dyb = dcat_ref[:, C:CW]
        dyb_ref[...] = dyb.astype(BF16)
        prod = dyb * out_ref[...]
        ov = ones_ref[...]
        hi = prod.astype(BF16)
        r1 = prod - hi.astype(F32)
        mid = r1.astype(BF16)
        lo = (r1 - mid.astype(F32)).astype(BF16)
        dl_ref[...] = (jnp.dot(hi, ov, preferred_element_type=F32)
                       + jnp.dot(mid, ov, preferred_element_type=F32)
                       + jnp.dot(lo, ov, preferred_element_type=F32))

    blk = pl.BlockSpec((tm, ATTN_OUT), lambda i: (i, 0))
    return pl.pallas_call(
        body, name=name,
        out_shape=(jax.ShapeDtypeStruct((S, ATTN_OUT), BF16), jax.ShapeDtypeStruct((S, ATTN_OUT), F32)),
        grid=(S // tm,),
        in_specs=[pl.BlockSpec((tm, CW), lambda i: (i, 0)), blk,
                  pl.BlockSpec((ATTN_OUT, ATTN_OUT), lambda i: (0, 0))],
        out_specs=(blk, blk),
        compiler_params=_cparams("parallel"),
    )(dcat, outf, jnp.asarray(ones, BF16))


def _attn_bwd(uq, dyb, lse, delta, bias, g, d, *, name):
    S = uq.shape[0]
    rows = S // d
    nb = rows // ATTN_BLK
    uv = uq.reshape(rows, d * ATTN_IN)
    dov = dyb.reshape(rows, d * ATTN_OUT)
    lv = lse.reshape(rows, d * ATTN_OUT)
    dv_ = delta.reshape(rows, d * ATTN_OUT)
    scale = HEAD_DIM ** -0.5

    def body(q_ref, kc_ref, kp_ref, vc_ref, vp_ref, do_ref, l_ref, dl_ref, bias_ref,
             out_ref, db_ref, dq_s, dk_s, dv_s):
        r = pl.program_id(0)
        n = pl.program_id(1)
        low = _lane_is_low()

        @pl.when((r == 0) & (n == 0))
        def _():
            db_ref[...] = jnp.zeros_like(db_ref)

        @pl.when(n == 0)
        def _():
            dq_s[...] = jnp.zeros_like(dq_s)
            dk_s[...] = jnp.zeros_like(dk_s)
            dv_s[...] = jnp.zeros_like(dv_s)

        @pl.when(n < nb)
        def _():
            sel = jnp.minimum(n, 1)
            lane = lax.broadcasted_iota(jnp.int32, (1, LANES), 1)
            for hp in range(HEADS_PER_GROUP // 2):
                sl = slice(hp * LANES, (hp + 1) * LANES)
                q2 = q_ref[:, sl]
                do2 = do_ref[:, sl]
                k2 = jnp.concatenate([kp_ref[:, sl], kc_ref[:, sl]], axis=0)
                v2 = jnp.concatenate([vp_ref[:, sl], vc_ref[:, sl]], axis=0)
                lse2 = l_ref[:, sl]
                dl2 = dl_ref[:, sl]
                dqs, dks, dvs = [], [], []
                for hh in range(2):
                    msk = low if hh == 0 else jnp.logical_not(low)
                    one = lane == hh * HEAD_DIM
                    lse_col = jnp.sum(jnp.where(one, lse2, 0.0), axis=-1, keepdims=True)
                    dl_col = jnp.sum(jnp.where(one, dl2, 0.0), axis=-1, keepdims=True)
                    qm = jnp.where(msk, q2, jnp.zeros_like(q2))
                    dom = jnp.where(msk, do2, jnp.zeros_like(do2))
                    logits = _dot_nt(qm, k2) * scale + bias_ref[sel, 2 * hp + hh]
                    p = jnp.exp(logits - lse_col)
                    dp = _dot_nt(dom, v2)
                    ds = p * (dp - dl_col)
                    db_ref[2 * hp + hh] += ds
                    dsb = ds.astype(BF16)
                    dqs.append(jnp.dot(dsb, k2, preferred_element_type=F32) * scale)
                    dks.append(_dot_tn(dsb, q2) * scale)
                    dvs.append(_dot_tn(p.astype(BF16), do2))
                dq2 = jnp.where(low, dqs[0], dqs[1])
                dk2 = jnp.where(low, dks[0], dks[1])
                dv2 = jnp.where(low, dvs[0], dvs[1])
                out_ref[:, sl] = dq_s[:, sl].astype(out_ref.dtype)
                dq_s[:, sl] = dq2
                ksl = slice(ATTN_OUT + hp * LANES, ATTN_OUT + (hp + 1) * LANES)
                vsl = slice(2 * ATTN_OUT + hp * LANES, 2 * ATTN_OUT + (hp + 1) * LANES)
                out_ref[:, ksl] = (dk_s[:, sl] + dk2[0:ATTN_BLK]).astype(out_ref.dtype)
                dk_s[:, sl] = dk2[ATTN_BLK:2 * ATTN_BLK]
                out_ref[:, vsl] = (dv_s[:, sl] + dv2[0:ATTN_BLK]).astype(out_ref.dtype)
                dv_s[:, sl] = dv2[ATTN_BLK:2 * ATTN_BLK]

        @pl.when(n == nb)
        def _():
            out_ref[:, 0:ATTN_OUT] = dq_s[...].astype(out_ref.dtype)
            out_ref[:, ATTN_OUT:2 * ATTN_OUT] = dk_s[...].astype(out_ref.dtype)
            out_ref[:, 2 * ATTN_OUT:GROUP_QKV] = dv_s[...].astype(out_ref.dtype)

    nqb = ATTN_IN // ATTN_OUT

    def spec(t, prev):
        def idx(r, n):
            nn = jnp.minimum(n, nb - 1)
            row = jnp.maximum(nn - 1, 0) if prev else nn
            return (row, r * nqb + t * N_GROUPS + g)
        return pl.BlockSpec((ATTN_BLK, ATTN_OUT), idx)

    rowblk = pl.BlockSpec((ATTN_BLK, ATTN_OUT), lambda r, n: (jnp.minimum(n, nb - 1), r))
    dqkv, db = pl.pallas_call(
        body, name=name,
        out_shape=(jax.ShapeDtypeStruct((rows, d * GROUP_QKV), BF16),
                   jax.ShapeDtypeStruct((HEADS_PER_GROUP, ATTN_BLK, 2 * ATTN_BLK), F32)),
        grid=(d, nb + 1),
        in_specs=[spec(0, False), spec(1, False), spec(1, True), spec(2, False), spec(2, True),
                  rowblk, rowblk, rowblk,
                  pl.BlockSpec((2, HEADS_PER_GROUP, ATTN_BLK, 2 * ATTN_BLK), lambda r, n: (0, 0, 0, 0))],
        out_specs=(pl.BlockSpec((ATTN_BLK, GROUP_QKV), lambda r, n: (jnp.maximum(n - 1, 0), r)),
                   pl.BlockSpec((HEADS_PER_GROUP, ATTN_BLK, 2 * ATTN_BLK), lambda r, n: (0, 0, 0))),
        scratch_shapes=[pltpu.VMEM((ATTN_BLK, ATTN_OUT), F32)] * 3,
        compiler_params=_cparams("arbitrary", "arbitrary"),
    )(uv, uv, uv, uv, uv, dov, lv, dv_, bias)
    return dqkv.reshape(S, GROUP_QKV), db


def _relbias_reduce(dbs, buckets, *, name):
    def body(db0, db1, db2, bk0, bk1, bk2, o_ref):
        rows = lax.broadcasted_iota(jnp.int32, (NUM_BUCKETS, LANES), 0)
        cols = lax.broadcasted_iota(jnp.int32, (NUM_BUCKETS, LANES), 1)
        acc = jnp.zeros((NUM_BUCKETS, LANES), F32)
        for g, (db_ref, bk_ref) in enumerate(((db0, bk0), (db1, bk1), (db2, bk2))):
            bk = bk_ref[...]
            for h in range(HEADS_PER_GROUP):
                dbh = db_ref[h]

                def step(b, acc, dbh=dbh, bk=bk, col=g * HEADS_PER_GROUP + h):
                    tot = jnp.sum(jnp.where(bk == b.astype(F32), dbh, 0.0), keepdims=True)
                    return acc + jnp.where((rows == b) & (cols == col), tot, 0.0)

                acc = lax.fori_loop(0, NUM_BUCKETS, step, acc)
        o_ref[...] = acc

    return pl.pallas_call(
        body, name=name,
        out_shape=jax.ShapeDtypeStruct((NUM_BUCKETS, LANES), F32),
        in_specs=[VMEM_SPEC] * 6, out_specs=VMEM_SPEC,
        compiler_params=pltpu.CompilerParams(vmem_limit_bytes=VMEM_LIMIT),
    )(*dbs, *buckets)


def _my_position():
    x, y, c = lax.axis_index("x"), lax.axis_index("y"), lax.axis_index("c")
    return x, y, c


def _linear(pos):
    return 4 * pos[0] + 2 * pos[1] + pos[2]


def _peer(pos, k):
    x, y, c = pos
    return ((1 - x) if k & 4 else x, (1 - y) if k & 2 else y, (1 - c) if k & 1 else c)


def _exchange(srcs, out_shapes, send_window, recv_window, *, name):
    T = len(srcs)

    def body(*refs):
        src = refs[:T]
        dst = refs[T:2 * T]
        send_sems, recv_sems, local_sems = refs[2 * T:]
        me = _my_position()
        me_lin = _linear(me)
        local = [pltpu.make_async_copy(send_window(t, src[t], me_lin), recv_window(t, dst[t], me_lin),
                                       local_sems.at[t]) for t in range(T)]
        for cp in local:
            cp.start()
        sends = []
        for k in range(1, N_DEV):
            peer = _peer(me, k)
            peer_lin = _linear(peer)
            for t in range(T):
                cp = pltpu.make_async_remote_copy(
                    src_ref=send_window(t, src[t], peer_lin), dst_ref=recv_window(t, dst[t], me_lin),
                    send_sem=send_sems.at[t, k - 1], recv_sem=recv_sems.at[t, k - 1],
                    device_id=peer, device_id_type=MESH)
                cp.start()
                sends.append(cp)
        for k in range(1, N_DEV):
            peer = _peer(me, k)
            peer_lin = _linear(peer)
            for t in range(T):
                pltpu.make_async_remote_copy(
                    src_ref=send_window(t, src[t], me_lin), dst_ref=recv_window(t, dst[t], peer_lin),
                    send_sem=send_sems.at[t, k - 1], recv_sem=recv_sems.at[t, k - 1],
                    device_id=peer, device_id_type=MESH).wait_recv()
        for cp in sends:
            cp.wait_send()
        for cp in local:
            cp.wait()

    return pl.pallas_call(
        body, name=name,
        out_shape=tuple(out_shapes),
        in_specs=[ANY] * T, out_specs=tuple([ANY] * T),
        scratch_shapes=[pltpu.SemaphoreType.DMA((T, N_DEV - 1)), pltpu.SemaphoreType.DMA((T, N_DEV - 1)),
                        pltpu.SemaphoreType.DMA((T,))],
        compiler_params=pltpu.CompilerParams(has_side_effects=True),
    )(*srcs)


def _shard_window(kind, width):
    def win(ref, lin):
        if kind == "slot":
            return ref.at[lin]
        if kind == "col":
            return ref.at[:, pl.ds(pl.multiple_of(lin * width, LANES), width)]
        if kind == "row":
            return ref.at[pl.ds(pl.multiple_of(lin * width, 8), width), :]
        if kind == "lcol":
            return ref.at[:, :, pl.ds(pl.multiple_of(lin * width, LANES), width)]
        if kind == "lrow":
            return ref.at[:, pl.ds(pl.multiple_of(lin * width, 8), width), :]
        raise ValueError(kind)
    return win


def _allgather_weights(shards, kinds, full_shapes, *, name):
    wins = [_shard_window(k, (s.shape[-1] if k in ("col", "lcol") else s.shape[-2])) for k, s in zip(kinds, shards)]
    return _exchange(
        shards, [jax.ShapeDtypeStruct(fs, s.dtype) for fs, s in zip(full_shapes, shards)],
        send_window=lambda t, ref, lin: ref,
        recv_window=lambda t, ref, lin: wins[t](ref, lin),
        name=name)


def _scatter_grads(fulls, kinds, shard_shapes, *, name):
    wins = [_shard_window(k, (ss[-1] if k in ("col", "lcol") else ss[-2])) for k, ss in zip(kinds, shard_shapes)]
    return _exchange(
        fulls, [jax.ShapeDtypeStruct((N_DEV,) + tuple(ss), f.dtype) for ss, f in zip(shard_shapes, fulls)],
        send_window=lambda t, ref, lin: wins[t](ref, lin),
        recv_window=lambda t, ref, lin: ref.at[lin],
        name=name)


def _small_gather(pack, *, reduce, name):
    R = pack.shape[0]

    def body(p_ref, o_ref, *rest):
        if reduce:
            buf, send_sems, recv_sems = rest
        else:
            buf = o_ref
            send_sems, recv_sems = rest
        me = _my_position()
        me_lin = _linear(me)
        buf[me_lin] = p_ref[...]
        sends = []
        for k in range(1, N_DEV):
            peer = _peer(me, k)
            cp = pltpu.make_async_remote_copy(
                src_ref=p_ref, dst_ref=buf.at[me_lin],
                send_sem=send_sems.at[k - 1], recv_sem=recv_sems.at[k - 1],
                device_id=peer, device_id_type=MESH)
            cp.start()
            sends.append(cp)
        for k in range(1, N_DEV):
            peer = _peer(me, k)
            pltpu.make_async_remote_copy(
                src_ref=p_ref, dst_ref=buf.at[_linear(peer)],
                send_sem=send_sems.at[k - 1], recv_sem=recv_sems.at[k - 1],
                device_id=peer, device_id_type=MESH).wait_recv()
        for cp in sends:
            cp.wait_send()
        if reduce:
            acc = buf[0]
            for s in range(1, N_DEV):
                acc = acc + buf[s]
            o_ref[...] = acc

    scratch = [pltpu.SemaphoreType.DMA((N_DEV - 1,)), pltpu.SemaphoreType.DMA((N_DEV - 1,))]
    if reduce:
        scratch = [pltpu.VMEM((N_DEV, R, LANES), F32)] + scratch
        out_shape = jax.ShapeDtypeStruct((R, LANES), F32)
    else:
        out_shape = jax.ShapeDtypeStruct((N_DEV, R, LANES), F32)
    return pl.pallas_call(
        body, name=name, out_shape=out_shape,
        in_specs=[VMEM_SPEC], out_specs=VMEM_SPEC, scratch_shapes=scratch,
        compiler_params=pltpu.CompilerParams(has_side_effects=True, vmem_limit_bytes=VMEM_LIMIT),
    )(pack)


def _adamw_math(w, g, m, v):
    m = ADAM_B1 * m + (1.0 - ADAM_B1) * g
    v = ADAM_B2 * v + (1.0 - ADAM_B2) * jnp.square(g)
    m_hat = m / (1.0 - ADAM_B1 ** ADAM_STEP)
    v_hat = v / (1.0 - ADAM_B2 ** ADAM_STEP)
    delta = -ADAM_LR * (m_hat / (jnp.sqrt(v_hat) + ADAM_EPS) + ADAM_WD * w)
    return delta, m, v


def _adamw_from_partials(parts, w, m, v, *, name):
    _, R, C = parts.shape
    tr = _tile(R, (256, 128, 64, 32, 16))

    def body(p_ref, w_ref, m_ref, v_ref, g_ref, d_ref, nm_ref, nv_ref):
        g = p_ref[0].astype(F32)
        for s in range(1, N_DEV):
            g = g + p_ref[s].astype(F32)
        d, nm, nv = _adamw_math(w_ref[...], g, m_ref[...], v_ref[...])
        g_ref[...] = g
        d_ref[...] = d
        nm_ref[...] = nm
        nv_ref[...] = nv

    blk = pl.BlockSpec((tr, C), lambda i: (i, 0))
    return pl.pallas_call(
        body, name=name,
        out_shape=(jax.ShapeDtypeStruct((R, C), F32),) * 4,
        grid=(R // tr,),
        in_specs=[pl.BlockSpec((N_DEV, tr, C), lambda i: (0, i, 0)), blk, blk, blk],
        out_specs=(blk,) * 4,
        compiler_params=_cparams("parallel"),
    )(parts, w, m, v)


def _adamw_small(g, w, m, v, *, name):
    def body(g_ref, w_ref, m_ref, v_ref, d_ref, nm_ref, nv_ref):
        d, nm, nv = _adamw_math(w_ref[...], g_ref[...], m_ref[...], v_ref[...])
        d_ref[...] = d
        nm_ref[...] = nm
        nv_ref[...] = nv

    return pl.pallas_call(
        body, name=name,
        out_shape=(jax.ShapeDtypeStruct(g.shape, F32),) * 3,
        in_specs=[VMEM_SPEC] * 4, out_specs=(VMEM_SPEC,) * 3,
    )(g, w, m, v)


def _pack_rows(pieces):
    flat = jnp.concatenate([p.reshape(-1) for p in pieces])
    n = flat.shape[0]
    padded = -(-n // (8 * LANES)) * (8 * LANES)
    return jnp.pad(flat, (0, padded - n)).reshape(padded // LANES, LANES)


def _unpack_rows(pack, shapes):
    flat = pack.reshape(-1)
    out, pos = [], 0
    for s in shapes:
        n = int(np.prod(s))
        out.append(flat[pos:pos + n].reshape(s))
        pos += n
    return out


def kernel(x, rel_bias, ab_norm, ab_w_in, ab_conv_w, ab_conv_b, ab_ln_g, ab_ln_b, ab_w_out, sc_norm, sc_w_in, sc_conv_w, sc_w_out, mlp_norm, mlp_w_up, mlp_w_down, final_norm, loss_target, m_rel_bias, m_ab_norm, m_ab_w_in, m_ab_conv_w, m_ab_conv_b, m_ab_ln_g, m_ab_ln_b, m_ab_w_out, m_sc_norm, m_sc_w_in, m_sc_conv_w, m_sc_w_out, m_mlp_norm, m_mlp_w_up, m_mlp_w_down, m_final_norm, v_rel_bias, v_ab_norm, v_ab_w_in, v_ab_conv_w, v_ab_conv_b, v_ab_ln_g, v_ab_ln_b, v_ab_w_out, v_sc_norm, v_sc_w_in, v_sc_conv_w, v_sc_w_out, v_mlp_norm, v_mlp_w_up, v_mlp_w_down, v_final_norm):
    S, D = x.shape[1], x.shape[2]
    CA = ab_conv_b.shape[1]
    C2 = 2 * CA
    AB_IN = C2 + ATTN_IN
    me_lin = _linear(_my_position())
    xs = x.reshape(S, D)
    tgt = loss_target.reshape(S, D)

    w_in_sh = ab_w_in[0].astype(BF16)
    big_shards = [w_in_sh, ab_w_out[0].astype(BF16), sc_w_in[0].astype(BF16), sc_w_out[0].astype(BF16),
                  mlp_w_up.astype(BF16), mlp_w_down.astype(BF16)]
    kinds = ["slot", "row", "col", "row", "lcol", "lrow"]
    full_shapes = [(N_DEV,) + w_in_sh.shape, (N_DEV * ab_w_out.shape[1], D), (D, N_DEV * sc_w_in.shape[2]),
                   (N_DEV * sc_w_out.shape[1], D), (2, D, N_DEV * mlp_w_up.shape[2]),
                   (2, N_DEV * mlp_w_down.shape[1], D)]
    w_in_g, w_out, w_sc_in, w_sc_out, w_up, w_dn = _allgather_weights(
        big_shards, kinds, full_shapes, name="allgather_weights")
    w_in = jnp.transpose(w_in_g, (1, 0, 2)).reshape(D, AB_IN)
    w_c = w_in[:, :C2]
    w_q = w_in[:, C2:]
    w_grp = [jnp.concatenate([w_q[:, t * N_GROUPS * ATTN_OUT + g * ATTN_OUT:][:, :ATTN_OUT] for t in range(3)], axis=1)
             for g in range(N_GROUPS)]

    cw_sh = ab_conv_w.shape[2]
    scn_sh = sc_norm.shape[1]
    scw_sh = sc_conv_w.shape[2]
    small_sh_shapes = [(CONV_A_WIDTH, cw_sh), (scn_sh,), (SC_CONV_WIDTH, scw_sh)]
    small_params = _small_gather(_pack_rows([ab_conv_w[0], sc_norm[0], sc_conv_w[0]]), reduce=False,
                                 name="allgather_small_params")
    per_dev = [_unpack_rows(small_params[s], small_sh_shapes) for s in range(N_DEV)]
    conv_w_full = jnp.concatenate([p[0] for p in per_dev], axis=1)
    sc_norm_full = jnp.concatenate([p[1] for p in per_dev], axis=0)[None]
    sc_conv_full = jnp.concatenate([p[2] for p in per_dev], axis=1)

    biases, buckets = zip(*[_bias_tables(rel_bias, g, dil) for g, (_, dil) in enumerate(DILATED_GROUPS)])

    n0 = _rmsnorm_fwd(xs, ab_norm, name="norm_ab")
    uc = _mm_nn(n0, w_c, out_dtype=F32, name="mm_ab_in_conv")
    uq = _mm_nn(n0, w_q, out_dtype=BF16, name="mm_ab_in_qkv")
    ya, hglu, ct = _conv_fwd(uc, conv_w_full, ab_conv_b, ab_ln_g, ab_ln_b, name="conv_fwd")
    outs, lses = zip(*[_attn_fwd(uq, biases[g], g, dil, name=f"attn_fwd_{g}")
                       for g, (_, dil) in enumerate(DILATED_GROUPS)])
    cat, outf, lse = _attn_merge(outs, lses, ya, name="attn_merge")
    h1 = _mm_nn(cat, w_out, out_dtype=F32, residual=xs, name="mm_ab_out")
    n1 = _rmsnorm_fwd(h1, mlp_norm[0:1], name="norm_mlp0")
    z0 = _mm_nn(n1, w_up[0], out_dtype=BF16, name="mm_up0")
    h2 = _mm_nn(z0, w_dn[0], out_dtype=F32, residual=h1, a_fn=_relu_sq, name="mm_down0")
    n2 = _rmsnorm_fwd(h2, sc_norm_full, name="norm_sc")
    u3 = _mm_nn(n2, w_sc_in, out_dtype=BF16, name="mm_sc_in")
    ysc = _sc_fwd(u3, sc_conv_full, name="sc_fwd")
    h3 = _mm_nn(ysc, w_sc_out, out_dtype=F32, residual=h2, name="mm_sc_out")
    n3 = _rmsnorm_fwd(h3, mlp_norm[1:2], name="norm_mlp1")
    z1 = _mm_nn(n3, w_up[1], out_dtype=BF16, name="mm_up1")
    h4 = _mm_nn(z1, w_dn[1], out_dtype=F32, residual=h3, a_fn=_relu_sq, name="mm_down1")

    def dz_epilogue(acc, z):
        return acc * (2.0 * jnp.maximum(z.astype(F32), 0.0))

    dh4, dh4b, acc_final = _loss_bwd(h4, tgt, final_norm[None], name="loss_bwd")
    dz1 = _mm_nt([(dh4b, w_dn[1])], out_dtype=BF16, epilogue=dz_epilogue, extra=z1, name="mm_d_down1")
    g_dn1 = _mm_tn(z1, dh4b, a_fn=_relu_sq, name="mm_gw_down1")
    g_up1 = _mm_tn(n3, dz1, name="mm_gw_up1")
    dn3 = _mm_nt([(dz1, w_up[1])], out_dtype=F32, name="mm_d_up1")
    dh3, dh3b, acc_mlp1 = _rms_bwd(h3, mlp_norm[1:2], dn3, dh4, name="norm_mlp1_bwd")

    dysc = _mm_nt([(dh3b, w_sc_out)], out_dtype=F32, name="mm_d_sc_out")
    g_sc_out = _mm_tn(ysc, dh3b, name="mm_gw_sc_out")
    du3, acc_scw = _sc_bwd(u3, dysc, sc_conv_full, name="sc_bwd")
    g_sc_in = _mm_tn(n2, du3, name="mm_gw_sc_in")
    dn2 = _mm_nt([(du3, w_sc_in)], out_dtype=F32, name="mm_d_sc_in")
    dh2, dh2b, acc_sc = _rms_bwd(h2, sc_norm_full, dn2, dh3, name="norm_sc_bwd")

    dz0 = _mm_nt([(dh2b, w_dn[0])], out_dtype=BF16, epilogue=dz_epilogue, extra=z0, name="mm_d_down0")
    g_dn0 = _mm_tn(z0, dh2b, a_fn=_relu_sq, name="mm_gw_down0")
    g_up0 = _mm_tn(n1, dz0, name="mm_gw_up0")
    dn1 = _mm_nt([(dz0, w_up[0])], out_dtype=F32, name="mm_d_up0")
    dh1, dh1b, acc_mlp0 = _rms_bwd(h1, mlp_norm[0:1], dn1, dh2, name="norm_mlp0_bwd")

    dcat = _mm_nt([(dh1b, w_out)], out_dtype=F32, name="mm_d_ab_out")
    g_ab_out = _mm_tn(cat, dh1b, name="mm_gw_ab_out")
    dyb, delta = _attn_prep(dcat, outf, name="attn_prep")
    dqkv, dbs = zip(*[_attn_bwd(uq, dyb, lse, delta, biases[g], g, dil, name=f"attn_bwd_{g}")
                      for g, (_, dil) in enumerate(DILATED_GROUPS)])
    drel = _relbias_reduce(dbs, buckets, name="relbias_reduce")
    dc, acc_conv = _conv_bwd_ln(ct, dcat, hglu, ab_ln_g, ab_ln_b, name="conv_bwd_ln")
    duc = _conv_bwd_in(dc, uc, conv_w_full, name="conv_bwd_in")
    g_wc = _mm_tn(n0, duc, name="mm_gw_ab_in_conv")
    g_wgrp = [_mm_tn(n0, dqkv[g], name=f"mm_gw_ab_in_qkv{g}") for g in range(N_GROUPS)]
    dn0 = _mm_nt([(duc, w_c)] + [(dqkv[g], w_grp[g]) for g in range(N_GROUPS)], out_dtype=F32, name="mm_d_ab_in")
    grad_x, _, acc_ab = _rms_bwd(xs, ab_norm, dn0, dh1, name="norm_ab_bwd")

    g_wq = jnp.concatenate([g_wgrp[g][:, t * ATTN_OUT:(t + 1) * ATTN_OUT]
                            for t in range(3) for g in range(N_GROUPS)], axis=1)
    g_w_in = jnp.concatenate([g_wc, g_wq], axis=1).reshape(D, N_DEV, AB_IN // N_DEV).transpose(1, 0, 2)
    g_up = jnp.stack([g_up0, g_up1])
    g_dn = jnp.stack([g_dn0, g_dn1])
    shard_shapes = [w_in_sh.shape, ab_w_out.shape[1:], sc_w_in.shape[1:], sc_w_out.shape[1:],
                    mlp_w_up.shape, mlp_w_down.shape]
    parts = _scatter_grads([g_w_in, g_ab_out, g_sc_in, g_sc_out, g_up, g_dn], kinds, shard_shapes,
                           name="scatter_grads")
    big = {}
    for nm, part, w, m, v in (("ab_w_in", parts[0], ab_w_in, m_ab_w_in, v_ab_w_in),
                              ("ab_w_out", parts[1], ab_w_out, m_ab_w_out, v_ab_w_out),
                              ("sc_w_in", parts[2], sc_w_in, m_sc_w_in, v_sc_w_in),
                              ("sc_w_out", parts[3], sc_w_out, m_sc_w_out, v_sc_w_out),
                              ("mlp_w_up", parts[4], mlp_w_up, m_mlp_w_up, v_mlp_w_up),
                              ("mlp_w_down", parts[5], mlp_w_down, m_mlp_w_down, v_mlp_w_down)):
        C = w.shape[-1]
        res = _adamw_from_partials(part.reshape(N_DEV, -1, C), w.reshape(-1, C), m.reshape(-1, C),
                                   v.reshape(-1, C), name="adamw_" + nm)
        big[nm] = tuple(r.reshape(w.shape) for r in res)

    small_full = [drel[:, :N_GROUPS * HEADS_PER_GROUP], acc_ab[0], acc_conv[0:CONV_A_WIDTH], acc_conv[32],
                  acc_conv[33], acc_conv[34], acc_sc[0], acc_scw[0:SC_CONV_WIDTH],
                  jnp.stack([acc_mlp0[0], acc_mlp1[0]]), acc_final[0], acc_final[1]]
    small_full_shapes = [p.shape for p in small_full]
    summed = _unpack_rows(_small_gather(_pack_rows(small_full), reduce=True, name="allreduce_small"),
                          small_full_shapes)
    (s_rel, s_abn, s_cw, s_cb, s_lg, s_lb, s_scn, s_scw, s_mlpn, s_fn, s_err) = summed
    loss = (0.5 / D) * jnp.sum(s_err)
    small_grads = {
        "rel_bias": s_rel, "ab_norm": s_abn[None],
        "ab_conv_w": lax.dynamic_slice_in_dim(s_cw, me_lin * cw_sh, cw_sh, axis=1)[None],
        "ab_conv_b": s_cb[None], "ab_ln_g": s_lg[None], "ab_ln_b": s_lb[None],
        "sc_norm": lax.dynamic_slice_in_dim(s_scn, me_lin * scn_sh, scn_sh, axis=0)[None],
        "sc_conv_w": lax.dynamic_slice_in_dim(s_scw, me_lin * scw_sh, scw_sh, axis=1)[None],
        "mlp_norm": s_mlpn, "final_norm": s_fn,
    }
    small_w = {"rel_bias": (rel_bias, m_rel_bias, v_rel_bias), "ab_norm": (ab_norm, m_ab_norm, v_ab_norm),
               "ab_conv_w": (ab_conv_w, m_ab_conv_w, v_ab_conv_w), "ab_conv_b": (ab_conv_b, m_ab_conv_b, v_ab_conv_b),
               "ab_ln_g": (ab_ln_g, m_ab_ln_g, v_ab_ln_g), "ab_ln_b": (ab_ln_b, m_ab_ln_b, v_ab_ln_b),
               "sc_norm": (sc_norm, m_sc_norm, v_sc_norm), "sc_conv_w": (sc_conv_w, m_sc_conv_w, v_sc_conv_w),
               "mlp_norm": (mlp_norm, m_mlp_norm, v_mlp_norm), "final_norm": (final_norm, m_final_norm, v_final_norm)}
    small_names = list(small_grads)
    small_shapes = [small_grads[n].shape for n in small_names]
    d_pack, m_pack, v_pack = _adamw_small(
        _pack_rows([small_grads[n] for n in small_names]), _pack_rows([small_w[n][0] for n in small_names]),
        _pack_rows([small_w[n][1] for n in small_names]), _pack_rows([small_w[n][2] for n in small_names]),
        name="adamw_small")
    small = {n: (small_grads[n], d, nm_, nv_) for n, d, nm_, nv_ in zip(
        small_names, _unpack_rows(d_pack, small_shapes), _unpack_rows(m_pack, small_shapes),
        _unpack_rows(v_pack, small_shapes))}

    order = ["rel_bias", "ab_norm", "ab_w_in", "ab_conv_w", "ab_conv_b", "ab_ln_g", "ab_ln_b", "ab_w_out",
             "sc_norm", "sc_w_in", "sc_conv_w", "sc_w_out", "mlp_norm", "mlp_w_up", "mlp_w_down", "final_norm"]
    allres = {**big, **small}
    return (loss, grad_x.reshape(x.shape),
            *[allres[n][0] for n in order], *[allres[n][1] for n in order],
            *[allres[n][2] for n in order], *[allres[n][3] for n in order])
```

```python
import functools
import math

import numpy as np
import jax
import jax.numpy as jnp
from jax import lax
from jax.experimental import pallas as pl
from jax.experimental.pallas import tpu as pltpu

F32 = jnp.float32
BF16 = jnp.bfloat16

HEAD_DIM = 64
HEADS_PER_GROUP = 8
DILATED_GROUPS = ((128, 1), (512, 4), (2048, 16))
N_GROUPS = 3
ATTN_OUT = HEADS_PER_GROUP * HEAD_DIM
ATTN_IN = 3 * N_GROUPS * ATTN_OUT
GROUP_QKV = 3 * ATTN_OUT
ATTN_BLK = 128
CONV_A_WIDTH = 31
SC_CONV_WIDTH = 3
NUM_BUCKETS = 32
REL_MAX_DISTANCE = 2048
RMS_EPS = 1e-6
LN_EPS = 1e-5
NEG_INF = -1e30
ADAM_LR = 0.001
ADAM_B1 = 0.9
ADAM_B2 = 0.999
ADAM_EPS = 1e-08
ADAM_WD = 0.01
ADAM_STEP = 10

N_DEV = 8
HALO = 32
LANES = 128
VMEM_LIMIT = 56 * 1024 * 1024
MESH = pl.DeviceIdType.MESH
ANY = pl.BlockSpec(memory_space=pl.ANY)
VMEM_SPEC = pl.BlockSpec(memory_space=pltpu.VMEM)


def _tile(n, prefs):
    for t in prefs:
        if n % t == 0:
            return t
    return n


def _cparams(*sem):
    return pltpu.CompilerParams(dimension_semantics=sem, vmem_limit_bytes=VMEM_LIMIT)


def _relu_sq(z):
    return jnp.square(jnp.maximum(z, 0))


def _dot_nt(a, b):
    return lax.dot_general(a, b, (((1,), (1,)), ((), ())), preferred_element_type=F32)


def _dot_tn(a, b):
    return lax.dot_general(a, b, (((0,), (0,)), ((), ())), preferred_element_type=F32)


def _mm_nn(a, b, *, out_dtype, name, residual=None, a_fn=None):
    M, K = a.shape
    _, N = b.shape
    tm = _tile(M, (1024, 512, 256))
    tn = _tile(N, (512, 384, 256, 128))
    tk = _tile(K, (1024, 512, 256, 128))
    nk = K // tk
    has_res = residual is not None

    def body(*refs):
        if has_res:
            a_ref, b_ref, r_ref, o_ref = refs[:4]
        else:
            a_ref, b_ref, o_ref = refs[:3]
        av = a_ref[...]
        if a_fn is not None:
            av = a_fn(av)
        part = jnp.dot(av, b_ref[...], preferred_element_type=F32)

        def finish(acc):
            if has_res:
                acc = acc + r_ref[...]
            o_ref[...] = acc.astype(o_ref.dtype)

        if nk == 1:
            finish(part)
        else:
            acc_ref = refs[-1]
            k = pl.program_id(2)

            @pl.when(k == 0)
            def _():
                acc_ref[...] = part

            @pl.when((k > 0) & (k < nk - 1))
            def _():
                acc_ref[...] += part

            @pl.when(k == nk - 1)
            def _():
                finish(acc_ref[...] + part)

    in_specs = [pl.BlockSpec((tm, tk), lambda i, j, k: (i, k)),
                pl.BlockSpec((tk, tn), lambda i, j, k: (k, j))]
    args = [a, b]
    if has_res:
        in_specs.append(pl.BlockSpec((tm, tn), lambda i, j, k: (i, j)))
        args.append(residual)
    return pl.pallas_call(
        body, name=name,
        out_shape=jax.ShapeDtypeStruct((M, N), out_dtype),
        grid=(M // tm, N // tn, nk),
        in_specs=in_specs,
        out_specs=pl.BlockSpec((tm, tn), lambda i, j, k: (i, j)),
        scratch_shapes=[pltpu.VMEM((tm, tn), F32)] if nk > 1 else [],
        compiler_params=_cparams("parallel", "parallel", "arbitrary"),
    )(*args)


def _mm_nt(pairs, *, out_dtype, name, epilogue=None, extra=None, dep=None):
    M = pairs[0][0].shape[0]
    Ko = pairs[0][1].shape[0]
    tm = _tile(M, (512, 256))
    to = _tile(Ko, (1024, 512, 256, 128))
    tk = 512 if len(pairs) > 1 else _tile(pairs[0][0].shape[1], (1024, 512, 256, 128))
    steps = [p[0].shape[1] // tk for p in pairs]
    offs = [sum(steps[:i]) for i in range(len(pairs))]
    nk = sum(steps)
    npair = len(pairs)
    has_extra = extra is not None

    def body(*refs):
        ab = refs[:2 * npair]
        pos = 2 * npair
        e_ref = None
        if has_extra:
            e_ref = refs[pos]
            pos += 1
        if dep is not None:
            pos += 1
        o_ref = refs[pos]
        acc_ref = refs[pos + 1]
        k = pl.program_id(2)

        @pl.when(k == 0)
        def _():
            acc_ref[...] = jnp.zeros_like(acc_ref)

        for p in range(npair):
            @pl.when((k >= offs[p]) & (k < offs[p] + steps[p]))
            def _(p=p):
                acc_ref[...] += _dot_nt(ab[2 * p][...], ab[2 * p + 1][...])

        @pl.when(k == nk - 1)
        def _():
            acc = acc_ref[...]
            if epilogue is not None:
                acc = epilogue(acc, e_ref[...] if has_extra else None)
            o_ref[...] = acc.astype(o_ref.dtype)

    in_specs, args = [], []
    for p, (a, b) in enumerate(pairs):
        def kidx(k, p=p):
            return jnp.clip(k - offs[p], 0, steps[p] - 1)
        in_specs.append(pl.BlockSpec((tm, tk), lambda i, j, k, kidx=kidx: (i, kidx(k))))
        in_specs.append(pl.BlockSpec((to, tk), lambda i, j, k, kidx=kidx: (j, kidx(k))))
        args += [a, b]
    if has_extra:
        in_specs.append(pl.BlockSpec((tm, to), lambda i, j, k: (i, j)))
        args.append(extra)
    if dep is not None:
        in_specs.append(ANY)
        args.append(dep)
    return pl.pallas_call(
        body, name=name,
        out_shape=jax.ShapeDtypeStruct((M, Ko), out_dtype),
        grid=(M // tm, Ko // to, nk),
        in_specs=in_specs,
        out_specs=pl.BlockSpec((tm, to), lambda i, j, k: (i, j)),
        scratch_shapes=[pltpu.VMEM((tm, to), F32)],
        compiler_params=_cparams("parallel", "parallel", "arbitrary"),
    )(*args)


def _mm_tn(a, b, *, name, a_fn=None):
    M, K = a.shape
    _, N = b.shape
    tm = _tile(M, (512, 256))
    tk = _tile(K, (1024, 768, 512, 384, 256, 128))
    tn = _tile(N, (1024, 768, 512, 384, 256, 128))
    nm = M // tm

    def body(a_ref, b_ref, o_ref, acc_ref):
        m = pl.program_id(2)
        av = a_ref[...]
        if a_fn is not None:
            av = a_fn(av)
        part = _dot_tn(av, b_ref[...])

        @pl.when(m == 0)
        def _():
            acc_ref[...] = part

        @pl.when((m > 0) & (m < nm - 1))
        def _():
            acc_ref[...] += part

        @pl.when(m == nm - 1)
        def _():
            o_ref[...] = (acc_ref[...] + part).astype(o_ref.dtype)

    return pl.pallas_call(
        body, name=name,
        out_shape=jax.ShapeDtypeStruct((K, N), BF16),
        grid=(K // tk, N // tn, nm),
        in_specs=[pl.BlockSpec((tm, tk), lambda i, j, m: (m, i)),
                  pl.BlockSpec((tm, tn), lambda i, j, m: (m, j))],
        out_specs=pl.BlockSpec((tk, tn), lambda i, j, m: (i, j)),
        scratch_shapes=[pltpu.VMEM((tk, tn), F32)],
        compiler_params=_cparams("parallel", "parallel", "arbitrary"),
    )(a, b)


def _rmsnorm_fwd(h, g, *, name, dep=None):
    S, D = h.shape
    tm = _tile(S, (512, 256))

    def body(h_ref, g_ref, *rest):
        o_ref = rest[-1]
        x = h_ref[...]
        r = lax.rsqrt(jnp.mean(x * x, axis=-1, keepdims=True) + RMS_EPS)
        o_ref[...] = (x * r * g_ref[...]).astype(o_ref.dtype)

    return pl.pallas_call(
        body, name=name,
        out_shape=jax.ShapeDtypeStruct((S, D), BF16),
        grid=(S // tm,),
        in_specs=[pl.BlockSpec((tm, D), lambda i: (i, 0)), pl.BlockSpec((1, D), lambda i: (0, 0))]
        + ([ANY] if dep is not None else []),
        out_specs=pl.BlockSpec((tm, D), lambda i: (i, 0)),
        compiler_params=_cparams("parallel"),
    )(h, g, *([dep] if dep is not None else []))


def _rms_bwd_rows(x, g, dy):
    r = lax.rsqrt(jnp.mean(x * x, axis=-1, keepdims=True) + RMS_EPS)
    xh = x * r
    gy = dy * g
    dx = r * (gy - xh * jnp.mean(xh * gy, axis=-1, keepdims=True))
    return dx, dy * xh


def _rms_bwd(x, g, dn, dres, *, name):
    S, D = x.shape
    tm = _tile(S, (256,))

    def body(x_ref, g_ref, dn_ref, dr_ref, dx_ref, dxb_ref, dg_ref):
        i = pl.program_id(0)
        dx, dgx = _rms_bwd_rows(x_ref[...], g_ref[...], dn_ref[...])
        tot = dr_ref[...] + dx
        dx_ref[...] = tot
        dxb_ref[...] = tot.astype(BF16)

        @pl.when(i == 0)
        def _():
            dg_ref[...] = jnp.zeros_like(dg_ref)

        dg_ref[0:1, :] += jnp.sum(dgx, axis=0, keepdims=True)

    row = pl.BlockSpec((tm, D), lambda i: (i, 0))
    return pl.pallas_call(
        body, name=name,
        out_shape=(jax.ShapeDtypeStruct((S, D), F32), jax.ShapeDtypeStruct((S, D), BF16),
                   jax.ShapeDtypeStruct((8, D), F32)),
        grid=(S // tm,),
        in_specs=[row, pl.BlockSpec((1, D), lambda i: (0, 0)), row, row],
        out_specs=(row, row, pl.BlockSpec((8, D), lambda i: (0, 0))),
        compiler_params=_cparams("arbitrary"),
    )(x, g, dn, dres)


def _loss_bwd(h, target, g, *, name):
    S, D = h.shape
    tm = _tile(S, (256,))

    def body(h_ref, t_ref, g_ref, dx_ref, dxb_ref, acc_ref):
        i = pl.program_id(0)
        x = h_ref[...]
        gv = g_ref[...]
        r = lax.rsqrt(jnp.mean(x * x, axis=-1, keepdims=True) + RMS_EPS)
        err = x * r * gv - t_ref[...]
        dx, dgx = _rms_bwd_rows(x, gv, err * (1.0 / D))
        dx_ref[...] = dx
        dxb_ref[...] = dx.astype(BF16)

        @pl.when(i == 0)
        def _():
            acc_ref[...] = jnp.zeros_like(acc_ref)

        acc_ref[0:1, :] += jnp.sum(dgx, axis=0, keepdims=True)
        acc_ref[1:2, :] += jnp.sum(err * err, axis=0, keepdims=True)

    row = pl.BlockSpec((tm, D), lambda i: (i, 0))
    return pl.pallas_call(
        body, name=name,
        out_shape=(jax.ShapeDtypeStruct((S, D), F32), jax.ShapeDtypeStruct((S, D), BF16),
                   jax.ShapeDtypeStruct((8, D), F32)),
        grid=(S // tm,),
        in_specs=[row, row, pl.BlockSpec((1, D), lambda i: (0, 0))],
        out_specs=(row, row, pl.BlockSpec((8, D), lambda i: (0, 0))),
        compiler_params=_cparams("arbitrary"),
    )(h, target, g)


def _conv_fwd(uc, conv_w, conv_b, ln_g, ln_b, *, name):
    S, C2 = uc.shape
    C = C2 // 2
    ts = _tile(S, (256,))
    per = ts // HALO

    def body(cur_ref, halo_ref, w_ref, b_ref, g_ref, beta_ref, ya_ref, h_ref, ct_ref, ext_ref):
        i = pl.program_id(0)
        hh = halo_ref[:, 0:C] * jax.nn.sigmoid(halo_ref[:, C:C2])
        ext_ref[0:HALO, :] = jnp.where(i == 0, 0.0, hh)
        hc = cur_ref[:, 0:C] * jax.nn.sigmoid(cur_ref[:, C:C2])
        ext_ref[HALO:HALO + ts, :] = hc
        h_ref[...] = hc
        acc = jnp.zeros((ts, C), F32)
        for k in range(CONV_A_WIDTH):
            acc = acc + w_ref[k:k + 1, :] * ext_ref[k + 2:k + 2 + ts, :]
        ct = acc + b_ref[...]
        ct_ref[...] = ct
        mu = jnp.mean(ct, axis=-1, keepdims=True)
        xc = ct - mu
        var = jnp.mean(xc * xc, axis=-1, keepdims=True)
        l = xc * lax.rsqrt(var + LN_EPS) * g_ref[...] + beta_ref[...]
        ya_ref[...] = (l * jax.nn.sigmoid(l)).astype(ya_ref.dtype)

    vec = pl.BlockSpec((1, C), lambda i: (0, 0))
    row = pl.BlockSpec((ts, C), lambda i: (i, 0))
    return pl.pallas_call(
        body, name=name,
        out_shape=(jax.ShapeDtypeStruct((S, C), BF16), jax.ShapeDtypeStruct((S, C), F32),
                   jax.ShapeDtypeStruct((S, C), F32)),
        grid=(S // ts,),
        in_specs=[pl.BlockSpec((ts, C2), lambda i: (i, 0)),
                  pl.BlockSpec((HALO, C2), lambda i: (jnp.maximum(i * per - 1, 0), 0)),
                  pl.BlockSpec((CONV_A_WIDTH, C), lambda i: (0, 0)), vec, vec, vec],
        out_specs=(row, row, row),
        scratch_shapes=[pltpu.VMEM((HALO + ts, C), F32)],
        compiler_params=_cparams("parallel"),
    )(uc, uc, conv_w, conv_b, ln_g, ln_b)


CONV_ACC_ROWS = 40


def _conv_bwd_ln(ct, dcat, hglu, ln_g, ln_b, *, name):
    S, C = ct.shape
    CW = dcat.shape[1]
    ts = _tile(S, (256,))
    per = ts // HALO

    def body(ct_ref, dcat_ref, hc_ref, hh_ref, g_ref, beta_ref, dc_ref, acc_ref, ext_ref):
        i = pl.program_id(0)
        ct = ct_ref[...]
        gv = g_ref[...]
        mu = jnp.mean(ct, axis=-1, keepdims=True)
        xc = ct - mu
        rstd = lax.rsqrt(jnp.mean(xc * xc, axis=-1, keepdims=True) + LN_EPS)
        xh = xc * rstd
        l = xh * gv + beta_ref[...]
        sg = jax.nn.sigmoid(l)
        dl = dcat_ref[:, 0:C] * (sg * (1.0 + l * (1.0 - sg)))
        dxh = dl * gv
        dc = rstd * (dxh - jnp.mean(dxh, axis=-1, keepdims=True)
                     - xh * jnp.mean(dxh * xh, axis=-1, keepdims=True))
        dc_ref[...] = dc

        @pl.when(i == 0)
        def _():
            acc_ref[...] = jnp.zeros_like(acc_ref)

        acc_ref[32:33, :] += jnp.sum(dc, axis=0, keepdims=True)
        acc_ref[33:34, :] += jnp.sum(dl * xh, axis=0, keepdims=True)
        acc_ref[34:35, :] += jnp.sum(dl, axis=0, keepdims=True)
        ext_ref[0:HALO, :] = jnp.where(i == 0, 0.0, hh_ref[...])
        ext_ref[HALO:HALO + ts, :] = hc_ref[...]
        for k in range(CONV_A_WIDTH):
            acc_ref[k:k + 1, :] += jnp.sum(dc * ext_ref[k + 2:k + 2 + ts, :], axis=0, keepdims=True)

    vec = pl.BlockSpec((1, C), lambda i: (0, 0))
    row = pl.BlockSpec((ts, C), lambda i: (i, 0))
    return pl.pallas_call(
        body, name=name,
        out_shape=(jax.ShapeDtypeStruct((S, C), F32), jax.ShapeDtypeStruct((CONV_ACC_ROWS, C), F32)),
        grid=(S // ts,),
        in_specs=[row, pl.BlockSpec((ts, CW), lambda i: (i, 0)), row,
                  pl.BlockSpec((HALO, C), lambda i: (jnp.maximum(i * per - 1, 0), 0)), vec, vec],
        out_specs=(row, pl.BlockSpec((CONV_ACC_ROWS, C), lambda i: (0, 0))),
        scratch_shapes=[pltpu.VMEM((HALO + ts, C), F32)],
        compiler_params=_cparams("arbitrary"),
    )(ct, dcat, hglu, hglu, ln_g, ln_b)


def _conv_bwd_in(dc, uc, conv_w, *, name):
    S, C = dc.shape
    C2 = 2 * C
    ts = _tile(S, (256,))
    per = ts // HALO
    nt = S // ts

    def body(dc_ref, dn_ref, uc_ref, w_ref, du_ref, ext_ref):
        i = pl.program_id(0)
        ext_ref[0:ts, :] = dc_ref[...]
        ext_ref[ts:ts + HALO, :] = jnp.where(i == nt - 1, 0.0, dn_ref[...])
        acc = jnp.zeros((ts, C), F32)
        for k in range(CONV_A_WIDTH):
            acc = acc + w_ref[k:k + 1, :] * ext_ref[30 - k:30 - k + ts, :]
        val = uc_ref[:, 0:C]
        sg = jax.nn.sigmoid(uc_ref[:, C:C2])
        du_ref[:, 0:C] = (acc * sg).astype(du_ref.dtype)
        du_ref[:, C:C2] = (acc * val * sg * (1.0 - sg)).astype(du_ref.dtype)

    return pl.pallas_call(
        body, name=name,
        out_shape=jax.ShapeDtypeStruct((S, C2), BF16),
        grid=(nt,),
        in_specs=[pl.BlockSpec((ts, C), lambda i: (i, 0)),
                  pl.BlockSpec((HALO, C), lambda i: (jnp.minimum((i + 1) * per, S // HALO - 1), 0)),
                  pl.BlockSpec((ts, C2), lambda i: (i, 0)),
                  pl.BlockSpec((CONV_A_WIDTH, C), lambda i: (0, 0))],
        out_specs=pl.BlockSpec((ts, C2), lambda i: (i, 0)),
        scratch_shapes=[pltpu.VMEM((ts + HALO, C), F32)],
        compiler_params=_cparams("parallel"),
    )(dc, dc, uc, conv_w)


def _sc_fwd(u3, conv_w, *, name):
    S, W3 = u3.shape
    W = W3 // 3
    ts = _tile(S, (256,))
    per = ts // HALO

    def body(cur_ref, halo_ref, w_ref, y_ref, ext_ref):
        i = pl.program_id(0)
        cvh = halo_ref[:, W:2 * W].astype(F32) * halo_ref[:, 2 * W:W3].astype(F32)
        ext_ref[0:HALO, :] = jnp.where(i == 0, 0.0, cvh)
        ext_ref[HALO:HALO + ts, :] = cur_ref[:, W:2 * W].astype(F32) * cur_ref[:, 2 * W:W3].astype(F32)
        k = (w_ref[0:1, :] * ext_ref[HALO - 2:HALO - 2 + ts, :]
             + w_ref[1:2, :] * ext_ref[HALO - 1:HALO - 1 + ts, :]
             + w_ref[2:3, :] * ext_ref[HALO:HALO + ts, :])
        y_ref[...] = (cur_ref[:, 0:W].astype(F32) * k).astype(y_ref.dtype)

    return pl.pallas_call(
        body, name=name,
        out_shape=jax.ShapeDtypeStruct((S, W), BF16),
        grid=(S // ts,),
        in_specs=[pl.BlockSpec((ts, W3), lambda i: (i, 0)),
                  pl.BlockSpec((HALO, W3), lambda i: (jnp.maximum(i * per - 1, 0), 0)),
                  pl.BlockSpec((SC_CONV_WIDTH, W), lambda i: (0, 0))],
        out_specs=pl.BlockSpec((ts, W), lambda i: (i, 0)),
        scratch_shapes=[pltpu.VMEM((HALO + ts, W), F32)],
        compiler_params=_cparams("parallel"),
    )(u3, u3, conv_w)


def _sc_bwd(u3, dy, conv_w, *, name):
    S, W3 = u3.shape
    W = W3 // 3
    ts = _tile(S, (256,))
    per = ts // HALO
    nt = S // ts

    def body(cur_ref, prev_ref, next_ref, dy_ref, dyn_ref, w_ref, du_ref, dw_ref, cv_ext, dk_ext):
        i = pl.program_id(0)
        cvh = prev_ref[:, W:2 * W].astype(F32) * prev_ref[:, 2 * W:W3].astype(F32)
        cv_ext[0:HALO, :] = jnp.where(i == 0, 0.0, cvh)
        c = cur_ref[:, W:2 * W].astype(F32)
        v = cur_ref[:, 2 * W:W3].astype(F32)
        b = cur_ref[:, 0:W].astype(F32)
        cv_ext[HALO:HALO + ts, :] = c * v
        dy_cur = dy_ref[...]
        dk = dy_cur * b
        dk_ext[0:ts, :] = dk
        dk_ext[ts:ts + HALO, :] = jnp.where(i == nt - 1, 0.0, dyn_ref[...] * next_ref[:, 0:W].astype(F32))
        w0, w1, w2 = w_ref[0:1, :], w_ref[1:2, :], w_ref[2:3, :]
        cv2 = cv_ext[HALO - 2:HALO - 2 + ts, :]
        cv1 = cv_ext[HALO - 1:HALO - 1 + ts, :]
        cv0 = cv_ext[HALO:HALO + ts, :]
        kconv = w0 * cv2 + w1 * cv1 + w2 * cv0
        dcv = w2 * dk + w1 * dk_ext[1:1 + ts, :] + w0 * dk_ext[2:2 + ts, :]
        du_ref[:, 0:W] = (dy_cur * kconv).astype(du_ref.dtype)
        du_ref[:, W:2 * W] = (dcv * v).astype(du_ref.dtype)
        du_ref[:, 2 * W:W3] = (dcv * c).astype(du_ref.dtype)

        @pl.when(i == 0)
        def _():
            dw_ref[...] = jnp.zeros_like(dw_ref)

        dw_ref[0:1, :] += jnp.sum(dk * cv2, axis=0, keepdims=True)
        dw_ref[1:2, :] += jnp.sum(dk * cv1, axis=0, keepdims=True)
        dw_ref[2:3, :] += jnp.sum(dk * cv0, axis=0, keepdims=True)

    nxt = lambda i: (jnp.minimum((i + 1) * per, S // HALO - 1), 0)
    return pl.pallas_call(
        body, name=name,
        out_shape=(jax.ShapeDtypeStruct((S, W3), BF16), jax.ShapeDtypeStruct((8, W), F32)),
        grid=(nt,),
        in_specs=[pl.BlockSpec((ts, W3), lambda i: (i, 0)),
                  pl.BlockSpec((HALO, W3), lambda i: (jnp.maximum(i * per - 1, 0), 0)),
                  pl.BlockSpec((HALO, W3), nxt),
                  pl.BlockSpec((ts, W), lambda i: (i, 0)),
                  pl.BlockSpec((HALO, W), nxt),
                  pl.BlockSpec((SC_CONV_WIDTH, W), lambda i: (0, 0))],
        out_specs=(pl.BlockSpec((ts, W3), lambda i: (i, 0)), pl.BlockSpec((8, W), lambda i: (0, 0))),
        scratch_shapes=[pltpu.VMEM((HALO + ts, W), F32), pltpu.VMEM((ts + HALO, W), F32)],
        compiler_params=_cparams("arbitrary"),
    )(u3, u3, u3, dy, dy, conv_w)


def _t5_causal_bucket(n):
    max_exact = NUM_BUCKETS // 2
    nf = jnp.maximum(n, 1).astype(F32)
    large = max_exact + (jnp.log(nf / max_exact) / math.log(REL_MAX_DISTANCE / max_exact)
                         * (NUM_BUCKETS - max_exact)).astype(jnp.int32)
    return jnp.where(n < max_exact, n, jnp.minimum(large, NUM_BUCKETS - 1))


def _bucket_tables():
    steps = ATTN_BLK
    m = jnp.arange(steps)[:, None] + steps - jnp.arange(2 * steps)[None, :]
    return jnp.stack([_t5_causal_bucket(jnp.clip(m, 0, steps) * dil).astype(F32) for _, dil in DILATED_GROUPS])


def _bias_tables(rel_bias, buckets, *, name):
    steps = ATTN_BLK

    def body(tab_ref, bk_ref, o_ref):
        g = pl.program_id(0)
        bk = bk_ref[0]
        a_idx = lax.broadcasted_iota(jnp.int32, (steps, 2 * steps), 0)
        c_idx = lax.broadcasted_iota(jnp.int32, (steps, 2 * steps), 1)
        m = a_idx + steps - c_idx
        band = (m >= 0) & (m <= steps)
        band_first = band & (c_idx >= steps)
        for h in range(HEADS_PER_GROUP):
            bias = jnp.zeros((steps, 2 * steps), F32)
            for b in range(NUM_BUCKETS):
                bias = jnp.where(bk == float(b), tab_ref[b, g * HEADS_PER_GROUP + h], bias)
            o_ref[0, 0, h] = jnp.where(band_first, bias, NEG_INF)
            o_ref[0, 1, h] = jnp.where(band, bias, NEG_INF)

    return pl.pallas_call(
        body, name=name,
        out_shape=jax.ShapeDtypeStruct((N_GROUPS, 2, HEADS_PER_GROUP, steps, 2 * steps), F32),
        grid=(N_GROUPS,),
        in_specs=[pl.BlockSpec(memory_space=pltpu.SMEM),
                  pl.BlockSpec((1, steps, 2 * steps), lambda g: (g, 0, 0))],
        out_specs=pl.BlockSpec((1, 2, HEADS_PER_GROUP, steps, 2 * steps), lambda g: (g, 0, 0, 0, 0)),
        compiler_params=_cparams("parallel"),
    )(rel_bias, buckets)


def _lane_is_low():
    return lax.broadcasted_iota(jnp.int32, (1, LANES), 1) < HEAD_DIM


def _qkv_specs(g):
    nqb = ATTN_IN // ATTN_OUT

    def spec(t, prev):
        def idx(r, n):
            row = jnp.maximum(n - 1, 0) if prev else n
            return (row, r * nqb + t * N_GROUPS + g)
        return pl.BlockSpec((ATTN_BLK, ATTN_OUT), idx)

    return [spec(0, False), spec(1, False), spec(1, True), spec(2, False), spec(2, True)]


def _attn_fwd(uq, bias, g, d, *, name):
    S = uq.shape[0]
    rows = S // d
    uv = uq.reshape(rows, d * ATTN_IN)

    def body(q_ref, kc_ref, kp_ref, vc_ref, vp_ref, bias_ref, o_ref, l_ref):
        n = pl.program_id(1)
        sel = jnp.minimum(n, 1)
        low = _lane_is_low()
        for hp in range(HEADS_PER_GROUP // 2):
            sl = slice(hp * LANES, (hp + 1) * LANES)
            q2 = q_ref[:, sl]
            k2 = jnp.concatenate([kp_ref[:, sl], kc_ref[:, sl]], axis=0)
            v2 = jnp.concatenate([vp_ref[:, sl], vc_ref[:, sl]], axis=0)
            outs, lses = [], []
            for hh in range(2):
                msk = low if hh == 0 else jnp.logical_not(low)
                qm = jnp.where(msk, q2, jnp.zeros_like(q2))
                logits = _dot_nt(qm, k2) * (HEAD_DIM ** -0.5) + bias_ref[sel, 2 * hp + hh]
                mx = jnp.max(logits, axis=-1, keepdims=True)
                p = jnp.exp(logits - mx)
                den = jnp.sum(p, axis=-1, keepdims=True)
                pv = jnp.dot(p.astype(BF16), v2, preferred_element_type=F32)
                outs.append(pv / den)
                lses.append(jnp.broadcast_to(mx + jnp.log(den), (ATTN_BLK, LANES)))
            o_ref[:, sl] = jnp.where(low, outs[0], outs[1])
            l_ref[:, sl] = jnp.where(low, lses[0], lses[1])

    out_spec = pl.BlockSpec((ATTN_BLK, ATTN_OUT), lambda r, n: (n, r))
    o, l = pl.pallas_call(
        body, name=name,
        out_shape=(jax.ShapeDtypeStruct((rows, d * ATTN_OUT), F32),) * 2,
        grid=(d, rows // ATTN_BLK),
        in_specs=_qkv_specs(g) + [pl.BlockSpec((None, 2, HEADS_PER_GROUP, ATTN_BLK, 2 * ATTN_BLK),
                                               lambda r, n: (g, 0, 0, 0, 0))],
        out_specs=(out_spec, out_spec),
        compiler_params=_cparams("parallel", "parallel"),
    )(uv, uv, uv, uv, uv, bias)
    return o.reshape(S, ATTN_OUT), l.reshape(S, ATTN_OUT)


def _attn_merge(outs, lses, ya, *, name):
    S, C = ya.shape
    tm = _tile(S, (256,))

    def body(o0, o1, o2, l0, l1, l2, ya_ref, cat_ref, out_ref, lse_ref):
        a0, a1, a2 = l0[...], l1[...], l2[...]
        m = jnp.maximum(jnp.maximum(a0, a1), a2)
        e0, e1, e2 = jnp.exp(a0 - m), jnp.exp(a1 - m), jnp.exp(a2 - m)
        den = e0 + e1 + e2
        out = (e0 * o0[...] + e1 * o1[...] + e2 * o2[...]) / den
        out_ref[...] = out
        lse_ref[...] = m + jnp.log(den)
        cat_ref[:, 0:C] = ya_ref[...]
        cat_ref[:, C:C + ATTN_OUT] = out.astype(cat_ref.dtype)

    blk = pl.BlockSpec((tm, ATTN_OUT), lambda i: (i, 0))
    return pl.pallas_call(
        body, name=name,
        out_shape=(jax.ShapeDtypeStruct((S, C + ATTN_OUT), BF16), jax.ShapeDtypeStruct((S, ATTN_OUT), F32),
                   jax.ShapeDtypeStruct((S, ATTN_OUT), F32)),
        grid=(S // tm,),
        in_specs=[blk] * 6 + [pl.BlockSpec((tm, C), lambda i: (i, 0))],
        out_specs=(pl.BlockSpec((tm, C + ATTN_OUT), lambda i: (i, 0)), blk, blk),
        compiler_params=_cparams("parallel"),
    )(*outs, *lses, ya)


def _attn_prep(dcat, outf, *, name):
    S, CW = dcat.shape
    C = CW - ATTN_OUT
    tm = _tile(S, (256,))
    ones = np.kron(np.eye(HEADS_PER_GROUP, dtype=np.float32), np.ones((HEAD_DIM, HEAD_DIM), np.float32))

    def body(dcat_ref, out_ref, ones_ref, dyb_ref, dl_ref):
        dyb = dcat_ref[:, C:CW]
        dyb_ref[...] = dyb.astype(BF16)
        prod = dyb * out_ref[...]
        ov = ones_ref[...]
        hi, mid, lo = _split_bf16(prod)
        dl_ref[...] = (jnp.dot(hi, ov, preferred_element_type=F32)
                       + jnp.dot(mid, ov, preferred_element_type=F32)
                       + jnp.dot(lo, ov, preferred_element_type=F32))

    blk = pl.BlockSpec((tm, ATTN_OUT), lambda i: (i, 0))
    return pl.pallas_call(
        body, name=name,
        out_shape=(jax.ShapeDtypeStruct((S, ATTN_OUT), BF16), jax.ShapeDtypeStruct((S, ATTN_OUT), F32)),
        grid=(S // tm,),
        in_specs=[pl.BlockSpec((tm, CW), lambda i: (i, 0)), blk,
                  pl.BlockSpec((ATTN_OUT, ATTN_OUT), lambda i: (0, 0))],
        out_specs=(blk, blk),
        compiler_params=_cparams("parallel"),
    )(dcat, outf, jnp.asarray(ones, BF16))


def _attn_bwd(uq, dyb, lse, delta, bias, g, d, *, name):
    S = uq.shape[0]
    rows = S // d
    nb = rows // ATTN_BLK
    uv = uq.reshape(rows, d * ATTN_IN)
    dov = dyb.reshape(rows, d * ATTN_OUT)
    lv = lse.reshape(rows, d * ATTN_OUT)
    dv_ = delta.reshape(rows, d * ATTN_OUT)
    scale = HEAD_DIM ** -0.5

    def body(q_ref, kc_ref, kp_ref, vc_ref, vp_ref, do_ref, l_ref, dl_ref, bias_ref,
             out_ref, db_ref, dq_s, dk_s, dv_s):
        r = pl.program_id(0)
        n = pl.program_id(1)
        low = _lane_is_low()

        @pl.when((r == 0) & (n == 0))
        def _():
            db_ref[...] = jnp.zeros_like(db_ref)

        @pl.when(n == 0)
        def _():
            dq_s[...] = jnp.zeros_like(dq_s)
            dk_s[...] = jnp.zeros_like(dk_s)
            dv_s[...] = jnp.zeros_like(dv_s)

        @pl.when(n < nb)
        def _():
            sel = jnp.minimum(n, 1)
            lane = lax.broadcasted_iota(jnp.int32, (1, LANES), 1)
            for hp in range(HEADS_PER_GROUP // 2):
                sl = slice(hp * LANES, (hp + 1) * LANES)
                q2 = q_ref[:, sl]
                do2 = do_ref[:, sl]
                k2 = jnp.concatenate([kp_ref[:, sl], kc_ref[:, sl]], axis=0)
                v2 = jnp.concatenate([vp_ref[:, sl], vc_ref[:, sl]], axis=0)
                lse2 = l_ref[:, sl]
                dl2 = dl_ref[:, sl]
                dqs, dks, dvs = [], [], []
                for hh in range(2):
                    msk = low if hh == 0 else jnp.logical_not(low)
                    one = lane == hh * HEAD_DIM
                    lse_col = jnp.sum(jnp.where(one, lse2, 0.0), axis=-1, keepdims=True)
                    dl_col = jnp.sum(jnp.where(one, dl2, 0.0), axis=-1, keepdims=True)
                    qm = jnp.where(msk, q2, jnp.zeros_like(q2))
                    dom = jnp.where(msk, do2, jnp.zeros_like(do2))
                    logits = _dot_nt(qm, k2) * scale + bias_ref[sel, 2 * hp + hh]
                    p = jnp.exp(logits - lse_col)
                    dp = _dot_nt(dom, v2)
                    ds = p * (dp - dl_col)
                    db_ref[2 * hp + hh] += ds
                    dsb = ds.astype(BF16)
                    dqs.append(jnp.dot(dsb, k2, preferred_element_type=F32) * scale)
                    dks.append(_dot_tn(dsb, q2) * scale)
                    dvs.append(_dot_tn(p.astype(BF16), do2))
                dq2 = jnp.where(low, dqs[0], dqs[1])
                dk2 = jnp.where(low, dks[0], dks[1])
                dv2 = jnp.where(low, dvs[0], dvs[1])
                out_ref[:, sl] = dq_s[:, sl].astype(out_ref.dtype)
                dq_s[:, sl] = dq2
                ksl = slice(ATTN_OUT + hp * LANES, ATTN_OUT + (hp + 1) * LANES)
                vsl = slice(2 * ATTN_OUT + hp * LANES, 2 * ATTN_OUT + (hp + 1) * LANES)
                out_ref[:, ksl] = (dk_s[:, sl] + dk2[0:ATTN_BLK]).astype(out_ref.dtype)
                dk_s[:, sl] = dk2[ATTN_BLK:2 * ATTN_BLK]
                out_ref[:, vsl] = (dv_s[:, sl] + dv2[0:ATTN_BLK]).astype(out_ref.dtype)
                dv_s[:, sl] = dv2[ATTN_BLK:2 * ATTN_BLK]

        @pl.when(n == nb)
        def _():
            out_ref[:, 0:ATTN_OUT] = dq_s[...].astype(out_ref.dtype)
            out_ref[:, ATTN_OUT:2 * ATTN_OUT] = dk_s[...].astype(out_ref.dtype)
            out_ref[:, 2 * ATTN_OUT:GROUP_QKV] = dv_s[...].astype(out_ref.dtype)

    nqb = ATTN_IN // ATTN_OUT

    def spec(t, prev):
        def idx(r, n):
            nn = jnp.minimum(n, nb - 1)
            row = jnp.maximum(nn - 1, 0) if prev else nn
            return (row, r * nqb + t * N_GROUPS + g)
        return pl.BlockSpec((ATTN_BLK, ATTN_OUT), idx)

    rowblk = pl.BlockSpec((ATTN_BLK, ATTN_OUT), lambda r, n: (jnp.minimum(n, nb - 1), r))
    dqkv, db = pl.pallas_call(
        body, name=name,
        out_shape=(jax.ShapeDtypeStruct((rows, d * GROUP_QKV), BF16),
                   jax.ShapeDtypeStruct((HEADS_PER_GROUP, ATTN_BLK, 2 * ATTN_BLK), F32)),
        grid=(d, nb + 1),
        in_specs=[spec(0, False), spec(1, False), spec(1, True), spec(2, False), spec(2, True),
                  rowblk, rowblk, rowblk,
                  pl.BlockSpec((None, 2, HEADS_PER_GROUP, ATTN_BLK, 2 * ATTN_BLK), lambda r, n: (g, 0, 0, 0, 0))],
        out_specs=(pl.BlockSpec((ATTN_BLK, GROUP_QKV), lambda r, n: (jnp.maximum(n - 1, 0), r)),
                   pl.BlockSpec((HEADS_PER_GROUP, ATTN_BLK, 2 * ATTN_BLK), lambda r, n: (0, 0, 0))),
        scratch_shapes=[pltpu.VMEM((ATTN_BLK, ATTN_OUT), F32)] * 3,
        compiler_params=_cparams("arbitrary", "arbitrary"),
    )(uv, uv, uv, uv, uv, dov, lv, dv_, bias)
    return dqkv.reshape(S, GROUP_QKV), db


def _split_bf16(x):
    hi = x.astype(BF16)
    r1 = x - hi.astype(F32)
    mid = r1.astype(BF16)
    lo = (r1 - mid.astype(F32)).astype(BF16)
    return hi, mid, lo


RELBIAS_CHUNK = 4096


def _relbias_reduce(dbs, buckets, *, name):
    flat = ATTN_BLK * 2 * ATTN_BLK
    dbf = jnp.stack([db.reshape(HEADS_PER_GROUP, flat) for db in dbs])
    bkf = buckets.reshape(N_GROUPS, 1, flat)

    def body(db_ref, bk_ref, o_ref):
        c = pl.program_id(1)
        rows = lax.broadcasted_iota(jnp.int32, (LANES, RELBIAS_CHUNK), 0).astype(F32)
        onehot = jnp.where(rows == bk_ref[0], 1.0, 0.0).astype(BF16)
        hi, mid, lo = _split_bf16(db_ref[0])
        part = _dot_nt(hi, onehot) + _dot_nt(mid, onehot) + _dot_nt(lo, onehot)

        @pl.when(c == 0)
        def _():
            o_ref[0] = part

        @pl.when(c > 0)
        def _():
            o_ref[0] += part

    return pl.pallas_call(
        body, name=name,
        out_shape=jax.ShapeDtypeStruct((N_GROUPS, HEADS_PER_GROUP, LANES), F32),
        grid=(N_GROUPS, flat // RELBIAS_CHUNK),
        in_specs=[pl.BlockSpec((1, HEADS_PER_GROUP, RELBIAS_CHUNK), lambda g, c: (g, 0, c)),
                  pl.BlockSpec((1, 1, RELBIAS_CHUNK), lambda g, c: (g, 0, c))],
        out_specs=pl.BlockSpec((1, HEADS_PER_GROUP, LANES), lambda g, c: (g, 0, 0)),
        compiler_params=_cparams("parallel", "arbitrary"),
    )(dbf, bkf)


def _my_position():
    x, y, c = lax.axis_index("x"), lax.axis_index("y"), lax.axis_index("c")
    return x, y, c


def _linear(pos):
    return 4 * pos[0] + 2 * pos[1] + pos[2]


def _peer(pos, k):
    x, y, c = pos
    return ((1 - x) if k & 4 else x, (1 - y) if k & 2 else y, (1 - c) if k & 1 else c)


HBM_SPEC = pl.BlockSpec(memory_space=pltpu.HBM)
SEM_SPEC = pl.BlockSpec(memory_space=pltpu.SEMAPHORE)
DATAFLOW = pltpu.SideEffectType.DATAFLOW_SIDE_EFFECTING


def _exchange_copies(src, land, sems, send_window, recv_window, with_arrivals):
    send_sems, recv_sems, local_sems = sems
    T = len(src)
    me = _my_position()
    me_lin = _linear(me)
    local = [pltpu.make_async_copy(send_window(t, src[t], me_lin), recv_window(t, land[t], me_lin),
                                   local_sems.at[t]) for t in range(T)]
    sends, arrivals = [], []
    for t in range(T):
        for k in range(1, N_DEV):
            peer = _peer(me, k)
            peer_lin = _linear(peer)
            sem = t * (N_DEV - 1) + k - 1
            sends.append(pltpu.make_async_remote_copy(
                src_ref=send_window(t, src[t], peer_lin), dst_ref=recv_window(t, land[t], me_lin),
                send_sem=send_sems.at[sem], recv_sem=recv_sems.at[sem],
                device_id=peer, device_id_type=MESH))
            if with_arrivals:
                arrivals.append(pltpu.make_async_remote_copy(
                    src_ref=send_window(t, src[t], me_lin), dst_ref=recv_window(t, land[t], peer_lin),
                    send_sem=send_sems.at[sem], recv_sem=recv_sems.at[sem],
                    device_id=peer, device_id_type=MESH))
    return local, sends, arrivals


def _exchange_start(srcs, land_shapes, send_window, recv_window, *, name, dep=None):
    T = len(srcs)
    n_in = 2 * T + (1 if dep is not None else 0)

    def body(*refs):
        src = refs[:T]
        land = refs[T:2 * T]
        sems = refs[n_in:n_in + 3]
        token = refs[-1]
        local, sends, _ = _exchange_copies(src, land, sems, send_window, recv_window, False)
        for cp in local + sends:
            cp.start()
        token[...] = jnp.zeros_like(token)

    lands = [lax.empty(ls.shape, ls.dtype) for ls in land_shapes]
    operands = [pltpu.with_memory_space_constraint(a, pltpu.HBM) for a in list(srcs) + lands]
    outs = pl.pallas_call(
        body, name=name,
        out_shape=(pltpu.SemaphoreType.DMA((T * (N_DEV - 1),)), pltpu.SemaphoreType.DMA((T * (N_DEV - 1),)),
                   pltpu.SemaphoreType.DMA((T,)),
                   *[pltpu.HBM(a.shape, a.dtype) for a in operands],
                   jax.ShapeDtypeStruct((8, LANES), F32)),
        in_specs=[HBM_SPEC] * (2 * T) + ([ANY] if dep is not None else []),
        out_specs=(SEM_SPEC,) * 3 + (HBM_SPEC,) * (2 * T) + (VMEM_SPEC,),
        input_output_aliases={i: 3 + i for i in range(2 * T)},
        compiler_params=pltpu.CompilerParams(has_side_effects=DATAFLOW),
    )(*operands, *([dep] if dep is not None else []))
    return outs[:3], outs[3:3 + T], outs[3 + T:3 + 2 * T], outs[-1]


def _exchange_wait(started, after, send_window, recv_window, *, name):
    sems, srcs, lands, _ = started
    T = len(srcs)

    def body(*refs):
        src = refs[:T]
        land = refs[T:2 * T]
        sem_refs = refs[2 * T:2 * T + 3]
        local, sends, arrivals = _exchange_copies(src, land, sem_refs, send_window, recv_window, True)
        for cp in arrivals:
            cp.wait_recv()
        for cp in sends:
            cp.wait_send()
        for cp in local:
            cp.wait()

    outs = pl.pallas_call(
        body, name=name,
        out_shape=tuple(pltpu.HBM(a.shape, a.dtype) for a in list(srcs) + list(lands)),
        in_specs=[HBM_SPEC] * (2 * T) + [SEM_SPEC] * 3 + [ANY],
        out_specs=(HBM_SPEC,) * (2 * T),
        input_output_aliases={i: i for i in range(2 * T)},
        compiler_params=pltpu.CompilerParams(has_side_effects=DATAFLOW),
    )(*srcs, *lands, *sems, after)
    return outs[T:]


def _shard_window(kind, width):
    def win(ref, lin):
        if kind == "slot":
            return ref.at[lin]
        if kind == "col":
            return ref.at[:, pl.ds(pl.multiple_of(lin * width, LANES), width)]
        if kind == "row":
            return ref.at[pl.ds(pl.multiple_of(lin * width, 8), width), :]
        if kind == "lcol":
            return ref.at[:, :, pl.ds(pl.multiple_of(lin * width, LANES), width)]
        if kind == "lrow":
            return ref.at[:, pl.ds(pl.multiple_of(lin * width, 8), width), :]
        raise ValueError(kind)
    return win


def _shard_windows(kinds, shard_shapes):
    return [_shard_window(k, (ss[-1] if k in ("col", "lcol") else ss[-2])) for k, ss in zip(kinds, shard_shapes)]


def _allgather_start(shards, kinds, full_shapes, *, name, dep=None):
    wins = _shard_windows(kinds, [s.shape for s in shards])
    send_window = lambda t, ref, lin: ref
    recv_window = lambda t, ref, lin: wins[t](ref, lin)
    started = _exchange_start(shards, [jax.ShapeDtypeStruct(fs, s.dtype) for fs, s in zip(full_shapes, shards)],
                              send_window, recv_window, name=name + "_start", dep=dep)
    return started, lambda after: _exchange_wait(started, after, send_window, recv_window, name=name + "_wait")


def _scatter_start(fulls, kinds, shard_shapes, *, name):
    wins = _shard_windows(kinds, shard_shapes)
    send_window = lambda t, ref, lin: wins[t](ref, lin)
    recv_window = lambda t, ref, lin: ref.at[lin]
    started = _exchange_start(
        fulls, [jax.ShapeDtypeStruct((N_DEV,) + tuple(ss), f.dtype) for ss, f in zip(shard_shapes, fulls)],
        send_window, recv_window, name=name + "_start")
    return started, lambda after: _exchange_wait(started, after, send_window, recv_window, name=name + "_wait")


def _small_gather(pack, *, reduce, name):
    R = pack.shape[0]

    def body(p_ref, o_ref, *rest):
        if reduce:
            buf, send_sems, recv_sems = rest
        else:
            buf = o_ref
            send_sems, recv_sems = rest
        me = _my_position()
        me_lin = _linear(me)
        buf[me_lin] = p_ref[...]
        sends = []
        for k in range(1, N_DEV):
            peer = _peer(me, k)
            cp = pltpu.make_async_remote_copy(
                src_ref=p_ref, dst_ref=buf.at[me_lin],
                send_sem=send_sems.at[k - 1], recv_sem=recv_sems.at[k - 1],
                device_id=peer, device_id_type=MESH)
            cp.start()
            sends.append(cp)
        for k in range(1, N_DEV):
            peer = _peer(me, k)
            pltpu.make_async_remote_copy(
                src_ref=p_ref, dst_ref=buf.at[_linear(peer)],
                send_sem=send_sems.at[k - 1], recv_sem=recv_sems.at[k - 1],
                device_id=peer, device_id_type=MESH).wait_recv()
        for cp in sends:
            cp.wait_send()
        if reduce:
            acc = buf[0]
            for s in range(1, N_DEV):
                acc = acc + buf[s]
            o_ref[...] = acc

    scratch = [pltpu.SemaphoreType.DMA((N_DEV - 1,)), pltpu.SemaphoreType.DMA((N_DEV - 1,))]
    if reduce:
        scratch = [pltpu.VMEM((N_DEV, R, LANES), F32)] + scratch
        out_shape = jax.ShapeDtypeStruct((R, LANES), F32)
    else:
        out_shape = jax.ShapeDtypeStruct((N_DEV, R, LANES), F32)
    return pl.pallas_call(
        body, name=name, out_shape=out_shape,
        in_specs=[VMEM_SPEC], out_specs=VMEM_SPEC, scratch_shapes=scratch,
        compiler_params=pltpu.CompilerParams(has_side_effects=True, vmem_limit_bytes=VMEM_LIMIT),
    )(pack)


def _adamw_math(w, g, m, v):
    m = ADAM_B1 * m + (1.0 - ADAM_B1) * g
    v = ADAM_B2 * v + (1.0 - ADAM_B2) * jnp.square(g)
    m_hat = m / (1.0 - ADAM_B1 ** ADAM_STEP)
    v_hat = v / (1.0 - ADAM_B2 ** ADAM_STEP)
    delta = -ADAM_LR * (m_hat / (jnp.sqrt(v_hat) + ADAM_EPS) + ADAM_WD * w)
    return delta, m, v


def _adamw_from_partials(parts, w, m, v, *, name):
    R, C = w.shape
    rl = parts[0].shape[1]
    assert all(p.shape == (N_DEV, rl, C) for p in parts) and rl * len(parts) == R
    tr = _tile(rl, (256, 128, 64, 32, 16))
    per = rl // tr
    L = len(parts)

    def body(*refs):
        p_refs = refs[:L]
        w_ref, m_ref, v_ref, g_ref, d_ref, nm_ref, nv_ref = refs[L:]
        i = pl.program_id(0)
        for l in range(L):
            @pl.when((i >= l * per) & (i < (l + 1) * per))
            def _(l=l):
                p_ref = p_refs[l]
                g = p_ref[0].astype(F32)
                for s in range(1, N_DEV):
                    g = g + p_ref[s].astype(F32)
                d, nm, nv = _adamw_math(w_ref[...], g, m_ref[...], v_ref[...])
                g_ref[...] = g
                d_ref[...] = d
                nm_ref[...] = nm
                nv_ref[...] = nv

    blk = pl.BlockSpec((tr, C), lambda i: (i, 0))
    part_specs = [pl.BlockSpec((N_DEV, tr, C), lambda i, l=l: (0, jnp.clip(i - l * per, 0, per - 1), 0))
                  for l in range(L)]
    return pl.pallas_call(
        body, name=name,
        out_shape=(jax.ShapeDtypeStruct((R, C), F32),) * 4,
        grid=(R // tr,),
        in_specs=part_specs + [blk, blk, blk],
        out_specs=(blk,) * 4,
        compiler_params=_cparams("parallel"),
    )(*parts, w, m, v)


def _adamw_small(g, w, m, v, *, name):
    def body(g_ref, w_ref, m_ref, v_ref, d_ref, nm_ref, nv_ref):
        d, nm, nv = _adamw_math(w_ref[...], g_ref[...], m_ref[...], v_ref[...])
        d_ref[...] = d
        nm_ref[...] = nm
        nv_ref[...] = nv

    return pl.pallas_call(
        body, name=name,
        out_shape=(jax.ShapeDtypeStruct(g.shape, F32),) * 3,
        in_specs=[VMEM_SPEC] * 4, out_specs=(VMEM_SPEC,) * 3,
    )(g, w, m, v)


def _pack_rows(pieces):
    flat = jnp.concatenate([p.reshape(-1) for p in pieces])
    n = flat.shape[0]
    padded = -(-n // (8 * LANES)) * (8 * LANES)
    return jnp.pad(flat, (0, padded - n)).reshape(padded // LANES, LANES)


def _unpack_rows(pack, shapes):
    flat = pack.reshape(-1)
    out, pos = [], 0
    for s in shapes:
        n = int(np.prod(s))
        out.append(flat[pos:pos + n].reshape(s))
        pos += n
    return out


def kernel(x, rel_bias, ab_norm, ab_w_in, ab_conv_w, ab_conv_b, ab_ln_g, ab_ln_b, ab_w_out, sc_norm, sc_w_in, sc_conv_w, sc_w_out, mlp_norm, mlp_w_up, mlp_w_down, final_norm, loss_target, m_rel_bias, m_ab_norm, m_ab_w_in, m_ab_conv_w, m_ab_conv_b, m_ab_ln_g, m_ab_ln_b, m_ab_w_out, m_sc_norm, m_sc_w_in, m_sc_conv_w, m_sc_w_out, m_mlp_norm, m_mlp_w_up, m_mlp_w_down, m_final_norm, v_rel_bias, v_ab_norm, v_ab_w_in, v_ab_conv_w, v_ab_conv_b, v_ab_ln_g, v_ab_ln_b, v_ab_w_out, v_sc_norm, v_sc_w_in, v_sc_conv_w, v_sc_w_out, v_mlp_norm, v_mlp_w_up, v_mlp_w_down, v_final_norm):
    S, D = x.shape[1], x.shape[2]
    CA = ab_conv_b.shape[1]
    C2 = 2 * CA
    AB_IN = C2 + ATTN_IN
    me_lin = _linear(_my_position())
    xs = x.reshape(S, D)
    tgt = loss_target.reshape(S, D)

    w_in_sh = ab_w_in[0].astype(BF16)
    ag_ab, wait_ab = _allgather_start(
        [w_in_sh, ab_w_out[0].astype(BF16)], ["slot", "row"],
        [(N_DEV,) + w_in_sh.shape, (N_DEV * ab_w_out.shape[1], D)], name="allgather_ab")
    ag_mlp, wait_mlp = _allgather_start(
        [mlp_w_up.astype(BF16), mlp_w_down.astype(BF16)], ["lcol", "lrow"],
        [(2, D, N_DEV * mlp_w_up.shape[2]), (2, N_DEV * mlp_w_down.shape[1], D)], name="allgather_mlp",
        dep=ag_ab[3])
    ag_sc, wait_sc = _allgather_start(
        [sc_w_in[0].astype(BF16), sc_w_out[0].astype(BF16)], ["col", "row"],
        [(D, N_DEV * sc_w_in.shape[2]), (N_DEV * sc_w_out.shape[1], D)], name="allgather_sc",
        dep=ag_mlp[3])

    cw_sh = ab_conv_w.shape[2]
    scn_sh = sc_norm.shape[1]
    scw_sh = sc_conv_w.shape[2]
    small_sh_shapes = [(CONV_A_WIDTH, cw_sh), (scn_sh,), (SC_CONV_WIDTH, scw_sh)]
    small_params = _small_gather(_pack_rows([ab_conv_w[0], sc_norm[0], sc_conv_w[0]]), reduce=False,
                                 name="allgather_small_params")
    per_dev = [_unpack_rows(small_params[s], small_sh_shapes) for s in range(N_DEV)]
    conv_w_full = jnp.concatenate([p[0] for p in per_dev], axis=1)
    sc_norm_full = jnp.concatenate([p[1] for p in per_dev], axis=0)[None]
    sc_conv_full = jnp.concatenate([p[2] for p in per_dev], axis=1)

    buckets = _bucket_tables()
    biases = _bias_tables(rel_bias, buckets, name="bias_tables")

    n0 = _rmsnorm_fwd(xs, ab_norm, name="norm_ab", dep=ag_sc[3])
    w_in_g, w_out = wait_ab(n0)
    w_in = jnp.transpose(w_in_g, (1, 0, 2)).reshape(D, AB_IN)
    w_c = w_in[:, :C2]
    w_q = w_in[:, C2:]
    w_grp = [jnp.concatenate([w_q[:, t * N_GROUPS * ATTN_OUT + g * ATTN_OUT:][:, :ATTN_OUT] for t in range(3)], axis=1)
             for g in range(N_GROUPS)]
    uc = _mm_nn(n0, w_c, out_dtype=F32, name="mm_ab_in_conv")
    uq = _mm_nn(n0, w_q, out_dtype=BF16, name="mm_ab_in_qkv")
    ya, hglu, ct = _conv_fwd(uc, conv_w_full, ab_conv_b, ab_ln_g, ab_ln_b, name="conv_fwd")
    outs, lses = zip(*[_attn_fwd(uq, biases, g, dil, name=f"attn_fwd_{g}")
                       for g, (_, dil) in enumerate(DILATED_GROUPS)])
    cat, outf, lse = _attn_merge(outs, lses, ya, name="attn_merge")
    h1 = _mm_nn(cat, w_out, out_dtype=F32, residual=xs, name="mm_ab_out")
    n1 = _rmsnorm_fwd(h1, mlp_norm[0:1], name="norm_mlp0")
    w_up, w_dn = wait_mlp(n1)
    z0 = _mm_nn(n1, w_up[0], out_dtype=BF16, name="mm_up0")
    h2 = _mm_nn(z0, w_dn[0], out_dtype=F32, residual=h1, a_fn=_relu_sq, name="mm_down0")
    n2 = _rmsnorm_fwd(h2, sc_norm_full, name="norm_sc")
    w_sc_in, w_sc_out = wait_sc(n2)
    u3 = _mm_nn(n2, w_sc_in, out_dtype=BF16, name="mm_sc_in")
    ysc = _sc_fwd(u3, sc_conv_full, name="sc_fwd")
    h3 = _mm_nn(ysc, w_sc_out, out_dtype=F32, residual=h2, name="mm_sc_out")
    n3 = _rmsnorm_fwd(h3, mlp_norm[1:2], name="norm_mlp1")
    z1 = _mm_nn(n3, w_up[1], out_dtype=BF16, name="mm_up1")
    h4 = _mm_nn(z1, w_dn[1], out_dtype=F32, residual=h3, a_fn=_relu_sq, name="mm_down1")

    def dz_epilogue(acc, z):
        return acc * (2.0 * jnp.maximum(z.astype(F32), 0.0))

    dh4, dh4b, acc_final = _loss_bwd(h4, tgt, final_norm[None], name="loss_bwd")
    dz1 = _mm_nt([(dh4b, w_dn[1])], out_dtype=BF16, epilogue=dz_epilogue, extra=z1, name="mm_d_down1")
    g_dn1 = _mm_tn(z1, dh4b, a_fn=_relu_sq, name="mm_gw_down1")
    g_up1 = _mm_tn(n3, dz1, name="mm_gw_up1")
    rs_mlp1, wait_rs_mlp1 = _scatter_start([g_up1, g_dn1], ["col", "row"],
                                           [mlp_w_up.shape[1:], mlp_w_down.shape[1:]], name="scatter_mlp1")
    dn3 = _mm_nt([(dz1, w_up[1])], out_dtype=F32, name="mm_d_up1", dep=rs_mlp1[3])
    dh3, dh3b, acc_mlp1 = _rms_bwd(h3, mlp_norm[1:2], dn3, dh4, name="norm_mlp1_bwd")

    dysc = _mm_nt([(dh3b, w_sc_out)], out_dtype=F32, name="mm_d_sc_out")
    g_sc_out = _mm_tn(ysc, dh3b, name="mm_gw_sc_out")
    du3, acc_scw = _sc_bwd(u3, dysc, sc_conv_full, name="sc_bwd")
    g_sc_in = _mm_tn(n2, du3, name="mm_gw_sc_in")
    rs_sc, wait_rs_sc = _scatter_start([g_sc_in, g_sc_out], ["col", "row"],
                                       [sc_w_in.shape[1:], sc_w_out.shape[1:]], name="scatter_sc")
    dn2 = _mm_nt([(du3, w_sc_in)], out_dtype=F32, name="mm_d_sc_in", dep=rs_sc[3])
    dh2, dh2b, acc_sc = _rms_bwd(h2, sc_norm_full, dn2, dh3, name="norm_sc_bwd")

    dz0 = _mm_nt([(dh2b, w_dn[0])], out_dtype=BF16, epilogue=dz_epilogue, extra=z0, name="mm_d_down0")
    g_dn0 = _mm_tn(z0, dh2b, a_fn=_relu_sq, name="mm_gw_down0")
    g_up0 = _mm_tn(n1, dz0, name="mm_gw_up0")
    rs_mlp0, wait_rs_mlp0 = _scatter_start([g_up0, g_dn0], ["col", "row"],
                                           [mlp_w_up.shape[1:], mlp_w_down.shape[1:]], name="scatter_mlp0")
    dn1 = _mm_nt([(dz0, w_up[0])], out_dtype=F32, name="mm_d_up0", dep=rs_mlp0[3])
    dh1, dh1b, acc_mlp0 = _rms_bwd(h1, mlp_norm[0:1], dn1, dh2, name="norm_mlp0_bwd")

    dcat = _mm_nt([(dh1b, w_out)], out_dtype=F32, name="mm_d_ab_out")
    g_ab_out = _mm_tn(cat, dh1b, name="mm_gw_ab_out")
    dyb, delta = _attn_prep(dcat, outf, name="attn_prep")
    dqkv, dbs = zip(*[_attn_bwd(uq, dyb, lse, delta, biases, g, dil, name=f"attn_bwd_{g}")
                      for g, (_, dil) in enumerate(DILATED_GROUPS)])
    drel = _relbias_reduce(dbs, buckets, name="relbias_reduce")
    dc, acc_conv = _conv_bwd_ln(ct, dcat, hglu, ab_ln_g, ab_ln_b, name="conv_bwd_ln")
    duc = _conv_bwd_in(dc, uc, conv_w_full, name="conv_bwd_in")
    g_wc = _mm_tn(n0, duc, name="mm_gw_ab_in_conv")
    g_wgrp = [_mm_tn(n0, dqkv[g], name=f"mm_gw_ab_in_qkv{g}") for g in range(N_GROUPS)]
    g_wq = jnp.concatenate([g_wgrp[g][:, t * ATTN_OUT:(t + 1) * ATTN_OUT]
                            for t in range(3) for g in range(N_GROUPS)], axis=1)
    g_w_in = jnp.concatenate([g_wc, g_wq], axis=1).reshape(D, N_DEV, AB_IN // N_DEV).transpose(1, 0, 2)
    rs_ab, wait_rs_ab = _scatter_start([g_w_in, g_ab_out], ["slot", "row"],
                                       [w_in_sh.shape, ab_w_out.shape[1:]], name="scatter_ab")
    dn0 = _mm_nt([(duc, w_c)] + [(dqkv[g], w_grp[g]) for g in range(N_GROUPS)], out_dtype=F32, name="mm_d_ab_in",
                 dep=rs_ab[3])
    grad_x, grad_xb, acc_ab = _rms_bwd(xs, ab_norm, dn0, dh1, name="norm_ab_bwd")

    small_full = [drel[:, :, :NUM_BUCKETS].transpose(2, 0, 1).reshape(NUM_BUCKETS, N_GROUPS * HEADS_PER_GROUP),
                  acc_ab[0], acc_conv[0:CONV_A_WIDTH], acc_conv[32],
                  acc_conv[33], acc_conv[34], acc_sc[0], acc_scw[0:SC_CONV_WIDTH],
                  jnp.stack([acc_mlp0[0], acc_mlp1[0]]), acc_final[0], acc_final[1]]
    small_full_shapes = [p.shape for p in small_full]
    summed = _unpack_rows(_small_gather(_pack_rows(small_full), reduce=True, name="allreduce_small"),
                          small_full_shapes)
    (s_rel, s_abn, s_cw, s_cb, s_lg, s_lb, s_scn, s_scw, s_mlpn, s_fn, s_err) = summed
    loss = (0.5 / D) * jnp.sum(s_err)
    small_grads = {
        "rel_bias": s_rel, "ab_norm": s_abn[None],
        "ab_conv_w": lax.dynamic_slice_in_dim(s_cw, me_lin * cw_sh, cw_sh, axis=1)[None],
        "ab_conv_b": s_cb[None], "ab_ln_g": s_lg[None], "ab_ln_b": s_lb[None],
        "sc_norm": lax.dynamic_slice_in_dim(s_scn, me_lin * scn_sh, scn_sh, axis=0)[None],
        "sc_conv_w": lax.dynamic_slice_in_dim(s_scw, me_lin * scw_sh, scw_sh, axis=1)[None],
        "mlp_norm": s_mlpn, "final_norm": s_fn,
    }
    small_w = {"rel_bias": (rel_bias, m_rel_bias, v_rel_bias), "ab_norm": (ab_norm, m_ab_norm, v_ab_norm),
               "ab_conv_w": (ab_conv_w, m_ab_conv_w, v_ab_conv_w), "ab_conv_b": (ab_conv_b, m_ab_conv_b, v_ab_conv_b),
               "ab_ln_g": (ab_ln_g, m_ab_ln_g, v_ab_ln_g), "ab_ln_b": (ab_ln_b, m_ab_ln_b, v_ab_ln_b),
               "sc_norm": (sc_norm, m_sc_norm, v_sc_norm), "sc_conv_w": (sc_conv_w, m_sc_conv_w, v_sc_conv_w),
               "mlp_norm": (mlp_norm, m_mlp_norm, v_mlp_norm), "final_norm": (final_norm, m_final_norm, v_final_norm)}
    small_names = list(small_grads)
    small_shapes = [small_grads[n].shape for n in small_names]
    d_pack, m_pack, v_pack = _adamw_small(
        _pack_rows([small_grads[n] for n in small_names]), _pack_rows([small_w[n][0] for n in small_names]),
        _pack_rows([small_w[n][1] for n in small_names]), _pack_rows([small_w[n][2] for n in small_names]),
        name="adamw_small")
    small = {n: (small_grads[n], d, nm_, nv_) for n, d, nm_, nv_ in zip(
        small_names, _unpack_rows(d_pack, small_shapes), _unpack_rows(m_pack, small_shapes),
        _unpack_rows(v_pack, small_shapes))}

    p_up1, p_dn1 = wait_rs_mlp1(grad_xb)
    p_sc_in, p_sc_out = wait_rs_sc(grad_xb)
    p_up0, p_dn0 = wait_rs_mlp0(grad_xb)
    p_w_in, p_ab_out = wait_rs_ab(grad_xb)
    big = {}
    for nm, parts, w, m, v in (("ab_w_in", [p_w_in], ab_w_in, m_ab_w_in, v_ab_w_in),
                               ("ab_w_out", [p_ab_out], ab_w_out, m_ab_w_out, v_ab_w_out),
                               ("sc_w_in", [p_sc_in], sc_w_in, m_sc_w_in, v_sc_w_in),
                               ("sc_w_out", [p_sc_out], sc_w_out, m_sc_w_out, v_sc_w_out),
                               ("mlp_w_up", [p_up0, p_up1], mlp_w_up, m_mlp_w_up, v_mlp_w_up),
                               ("mlp_w_down", [p_dn0, p_dn1], mlp_w_down, m_mlp_w_down, v_mlp_w_down)):
        C = w.shape[-1]
        res = _adamw_from_partials(parts, w.reshape(-1, C), m.reshape(-1, C), v.reshape(-1, C), name="adamw_" + nm)
        big[nm] = tuple(r.reshape(w.shape) for r in res)

    order = ["rel_bias", "ab_norm", "ab_w_in", "ab_conv_w", "ab_conv_b", "ab_ln_g", "ab_ln_b", "ab_w_out",
             "sc_norm", "sc_w_in", "sc_conv_w", "sc_w_out", "mlp_norm", "mlp_w_up", "mlp_w_down", "final_norm"]
    allres = {**big, **small}
    return (loss, grad_x.reshape(x.shape),
            *[allres[n][0] for n in order], *[allres[n][1] for n in order],
            *[allres[n][2] for n in order], *[allres[n][3] for n in order])
```

```python
import functools
import math

import numpy as np
import jax
import jax.numpy as jnp
from jax import lax
from jax.experimental import pallas as pl
from jax.experimental.pallas import tpu as pltpu

F32 = jnp.float32
BF16 = jnp.bfloat16

HEAD_DIM = 64
HEADS_PER_GROUP = 8
DILATED_GROUPS = ((128, 1), (512, 4), (2048, 16))
N_GROUPS = 3
ATTN_OUT = HEADS_PER_GROUP * HEAD_DIM
ATTN_IN = 3 * N_GROUPS * ATTN_OUT
GROUP_QKV = 3 * ATTN_OUT
ATTN_BLK = 128
CONV_A_WIDTH = 31
SC_CONV_WIDTH = 3
NUM_BUCKETS = 32
REL_MAX_DISTANCE = 2048
RMS_EPS = 1e-6
LN_EPS = 1e-5
NEG_INF = -1e30
ADAM_LR = 0.001
ADAM_B1 = 0.9
ADAM_B2 = 0.999
ADAM_EPS = 1e-08
ADAM_WD = 0.01
ADAM_STEP = 10

N_DEV = 8
HALO = 32
LANES = 128
VMEM_LIMIT = 56 * 1024 * 1024
MESH = pl.DeviceIdType.MESH
ANY = pl.BlockSpec(memory_space=pl.ANY)
VMEM_SPEC = pl.BlockSpec(memory_space=pltpu.VMEM)


def _tile(n, prefs):
    for t in prefs:
        if n % t == 0:
            return t
    return n


def _cparams(*sem):
    return pltpu.CompilerParams(dimension_semantics=sem, vmem_limit_bytes=VMEM_LIMIT)


def _relu_sq(z):
    return jnp.square(jnp.maximum(z, 0))


def _dot_nt(a, b):
    return lax.dot_general(a, b, (((1,), (1,)), ((), ())), preferred_element_type=F32)


def _dot_tn(a, b):
    return lax.dot_general(a, b, (((0,), (0,)), ((), ())), preferred_element_type=F32)


def _mm_nn(a, b, *, out_dtype, name, residual=None, a_fn=None):
    M, K = a.shape
    _, N = b.shape
    tm = _tile(M, (2048, 1024, 512, 256))
    tn = _tile(N, (512, 384, 256, 128))
    tk = _tile(K, (1024, 512, 256, 128))
    nk = K // tk
    has_res = residual is not None

    def body(*refs):
        if has_res:
            a_ref, b_ref, r_ref, o_ref = refs[:4]
        else:
            a_ref, b_ref, o_ref = refs[:3]
        av = a_ref[...]
        if a_fn is not None:
            av = a_fn(av)
        part = jnp.dot(av, b_ref[...], preferred_element_type=F32)

        def finish(acc):
            if has_res:
                acc = acc + r_ref[...]
            o_ref[...] = acc.astype(o_ref.dtype)

        if nk == 1:
            finish(part)
        else:
            acc_ref = refs[-1]
            k = pl.program_id(2)

            @pl.when(k == 0)
            def _():
                acc_ref[...] = part

            @pl.when((k > 0) & (k < nk - 1))
            def _():
                acc_ref[...] += part

            @pl.when(k == nk - 1)
            def _():
                finish(acc_ref[...] + part)

    in_specs = [pl.BlockSpec((tm, tk), lambda i, j, k: (i, k)),
                pl.BlockSpec((tk, tn), lambda i, j, k: (k, j))]
    args = [a, b]
    if has_res:
        in_specs.append(pl.BlockSpec((tm, tn), lambda i, j, k: (i, j)))
        args.append(residual)
    return pl.pallas_call(
        body, name=name,
        out_shape=jax.ShapeDtypeStruct((M, N), out_dtype),
        grid=(M // tm, N // tn, nk),
        in_specs=in_specs,
        out_specs=pl.BlockSpec((tm, tn), lambda i, j, k: (i, j)),
        scratch_shapes=[pltpu.VMEM((tm, tn), F32)] if nk > 1 else [],
        compiler_params=_cparams("parallel", "parallel", "arbitrary"),
    )(*args)


def _mm_nt(pairs, *, out_dtype, name, epilogue=None, extra=None, dep=None):
    M = pairs[0][0].shape[0]
    Ko = pairs[0][1].shape[0]
    tm = _tile(M, (1024, 512, 256))
    to = _tile(Ko, (1024, 512, 256, 128))
    tks = [_tile(p[0].shape[1], (1024, 768, 512, 256, 128)) for p in pairs]
    steps = [p[0].shape[1] // tk for p, tk in zip(pairs, tks)]
    offs = [sum(steps[:i]) for i in range(len(pairs))]
    nk = sum(steps)
    npair = len(pairs)
    has_extra = extra is not None

    def body(*refs):
        ab = refs[:2 * npair]
        pos = 2 * npair
        e_ref = None
        if has_extra:
            e_ref = refs[pos]
            pos += 1
        if dep is not None:
            pos += 1
        o_ref = refs[pos]
        acc_ref = refs[pos + 1]
        k = pl.program_id(2)

        @pl.when(k == 0)
        def _():
            acc_ref[...] = jnp.zeros_like(acc_ref)

        for p in range(npair):
            @pl.when((k >= offs[p]) & (k < offs[p] + steps[p]))
            def _(p=p):
                acc_ref[...] += _dot_nt(ab[2 * p][...], ab[2 * p + 1][...])

        @pl.when(k == nk - 1)
        def _():
            acc = acc_ref[...]
            if epilogue is not None:
                acc = epilogue(acc, e_ref[...] if has_extra else None)
            o_ref[...] = acc.astype(o_ref.dtype)

    in_specs, args = [], []
    for p, (a, b) in enumerate(pairs):
        def kidx(k, p=p):
            return jnp.clip(k - offs[p], 0, steps[p] - 1)
        in_specs.append(pl.BlockSpec((tm, tks[p]), lambda i, j, k, kidx=kidx: (i, kidx(k))))
        in_specs.append(pl.BlockSpec((to, tks[p]), lambda i, j, k, kidx=kidx: (j, kidx(k))))
        args += [a, b]
    if has_extra:
        in_specs.append(pl.BlockSpec((tm, to), lambda i, j, k: (i, j)))
        args.append(extra)
    if dep is not None:
        in_specs.append(ANY)
        args.append(dep)
    return pl.pallas_call(
        body, name=name,
        out_shape=jax.ShapeDtypeStruct((M, Ko), out_dtype),
        grid=(M // tm, Ko // to, nk),
        in_specs=in_specs,
        out_specs=pl.BlockSpec((tm, to), lambda i, j, k: (i, j)),
        scratch_shapes=[pltpu.VMEM((tm, to), F32)],
        compiler_params=_cparams("parallel", "parallel", "arbitrary"),
    )(*args)


def _mm_tn(a, b, *, name, a_fn=None):
    M, K = a.shape
    _, N = b.shape
    tm = _tile(M, (2048, 1024, 512, 256))
    tk = _tile(K, (1024, 768, 512, 384, 256, 128))
    tn = _tile(N, (1024, 768, 512, 384, 256, 128))
    nm = M // tm

    def body(a_ref, b_ref, o_ref, acc_ref):
        m = pl.program_id(2)
        av = a_ref[...]
        if a_fn is not None:
            av = a_fn(av)
        part = _dot_tn(av, b_ref[...])

        @pl.when(m == 0)
        def _():
            acc_ref[...] = part

        @pl.when((m > 0) & (m < nm - 1))
        def _():
            acc_ref[...] += part

        @pl.when(m == nm - 1)
        def _():
            o_ref[...] = (acc_ref[...] + part).astype(o_ref.dtype)

    return pl.pallas_call(
        body, name=name,
        out_shape=jax.ShapeDtypeStruct((K, N), BF16),
        grid=(K // tk, N // tn, nm),
        in_specs=[pl.BlockSpec((tm, tk), lambda i, j, m: (m, i)),
                  pl.BlockSpec((tm, tn), lambda i, j, m: (m, j))],
        out_specs=pl.BlockSpec((tk, tn), lambda i, j, m: (i, j)),
        scratch_shapes=[pltpu.VMEM((tk, tn), F32)],
        compiler_params=_cparams("parallel", "parallel", "arbitrary"),
    )(a, b)


def _rmsnorm_fwd(h, g, *, name, dep=None):
    S, D = h.shape
    tm = _tile(S, (512, 256))

    def body(h_ref, g_ref, *rest):
        o_ref = rest[-1]
        x = h_ref[...]
        r = lax.rsqrt(jnp.mean(x * x, axis=-1, keepdims=True) + RMS_EPS)
        o_ref[...] = (x * r * g_ref[...]).astype(o_ref.dtype)

    return pl.pallas_call(
        body, name=name,
        out_shape=jax.ShapeDtypeStruct((S, D), BF16),
        grid=(S // tm,),
        in_specs=[pl.BlockSpec((tm, D), lambda i: (i, 0)), pl.BlockSpec((1, D), lambda i: (0, 0))]
        + ([ANY] if dep is not None else []),
        out_specs=pl.BlockSpec((tm, D), lambda i: (i, 0)),
        compiler_params=_cparams("parallel"),
    )(h, g, *([dep] if dep is not None else []))


def _rms_bwd_rows(x, g, dy):
    r = lax.rsqrt(jnp.mean(x * x, axis=-1, keepdims=True) + RMS_EPS)
    xh = x * r
    gy = dy * g
    dx = r * (gy - xh * jnp.mean(xh * gy, axis=-1, keepdims=True))
    return dx, dy * xh


def _rms_bwd(x, g, dn, dres, *, name):
    S, D = x.shape
    tm = _tile(S, (256,))

    def body(x_ref, g_ref, dn_ref, dr_ref, dx_ref, dxb_ref, dg_ref):
        i = pl.program_id(0)
        dx, dgx = _rms_bwd_rows(x_ref[...], g_ref[...], dn_ref[...])
        tot = dr_ref[...] + dx
        dx_ref[...] = tot
        dxb_ref[...] = tot.astype(BF16)

        @pl.when(i == 0)
        def _():
            dg_ref[...] = jnp.zeros_like(dg_ref)

        dg_ref[0:1, :] += jnp.sum(dgx, axis=0, keepdims=True)

    row = pl.BlockSpec((tm, D), lambda i: (i, 0))
    return pl.pallas_call(
        body, name=name,
        out_shape=(jax.ShapeDtypeStruct((S, D), F32), jax.ShapeDtypeStruct((S, D), BF16),
                   jax.ShapeDtypeStruct((8, D), F32)),
        grid=(S // tm,),
        in_specs=[row, pl.BlockSpec((1, D), lambda i: (0, 0)), row, row],
        out_specs=(row, row, pl.BlockSpec((8, D), lambda i: (0, 0))),
        compiler_params=_cparams("arbitrary"),
    )(x, g, dn, dres)


def _loss_bwd(h, target, g, *, name):
    S, D = h.shape
    tm = _tile(S, (256,))

    def body(h_ref, t_ref, g_ref, dx_ref, dxb_ref, acc_ref):
        i = pl.program_id(0)
        x = h_ref[...]
        gv = g_ref[...]
        r = lax.rsqrt(jnp.mean(x * x, axis=-1, keepdims=True) + RMS_EPS)
        err = x * r * gv - t_ref[...]
        dx, dgx = _rms_bwd_rows(x, gv, err * (1.0 / D))
        dx_ref[...] = dx
        dxb_ref[...] = dx.astype(BF16)

        @pl.when(i == 0)
        def _():
            acc_ref[...] = jnp.zeros_like(acc_ref)

        acc_ref[0:1, :] += jnp.sum(dgx, axis=0, keepdims=True)
        acc_ref[1:2, :] += jnp.sum(err * err, axis=0, keepdims=True)

    row = pl.BlockSpec((tm, D), lambda i: (i, 0))
    return pl.pallas_call(
        body, name=name,
        out_shape=(jax.ShapeDtypeStruct((S, D), F32), jax.ShapeDtypeStruct((S, D), BF16),
                   jax.ShapeDtypeStruct((8, D), F32)),
        grid=(S // tm,),
        in_specs=[row, row, pl.BlockSpec((1, D), lambda i: (0, 0))],
        out_specs=(row, row, pl.BlockSpec((8, D), lambda i: (0, 0))),
        compiler_params=_cparams("arbitrary"),
    )(h, target, g)


def _conv_fwd(uc, conv_w, conv_b, ln_g, ln_b, *, name):
    S, C2 = uc.shape
    C = C2 // 2
    ts = _tile(S, (256,))
    per = ts // HALO

    def body(cur_ref, halo_ref, w_ref, b_ref, g_ref, beta_ref, ya_ref, h_ref, ct_ref, ext_ref):
        i = pl.program_id(0)
        hh = halo_ref[:, 0:C] * jax.nn.sigmoid(halo_ref[:, C:C2])
        ext_ref[0:HALO, :] = jnp.where(i == 0, 0.0, hh)
        hc = cur_ref[:, 0:C] * jax.nn.sigmoid(cur_ref[:, C:C2])
        ext_ref[HALO:HALO + ts, :] = hc
        h_ref[...] = hc
        acc = jnp.zeros((ts, C), F32)
        for k in range(CONV_A_WIDTH):
            acc = acc + w_ref[k:k + 1, :] * ext_ref[k + 2:k + 2 + ts, :]
        ct = acc + b_ref[...]
        ct_ref[...] = ct
        mu = jnp.mean(ct, axis=-1, keepdims=True)
        xc = ct - mu
        var = jnp.mean(xc * xc, axis=-1, keepdims=True)
        l = xc * lax.rsqrt(var + LN_EPS) * g_ref[...] + beta_ref[...]
        ya_ref[...] = (l * jax.nn.sigmoid(l)).astype(ya_ref.dtype)

    vec = pl.BlockSpec((1, C), lambda i: (0, 0))
    row = pl.BlockSpec((ts, C), lambda i: (i, 0))
    return pl.pallas_call(
        body, name=name,
        out_shape=(jax.ShapeDtypeStruct((S, C), BF16), jax.ShapeDtypeStruct((S, C), F32),
                   jax.ShapeDtypeStruct((S, C), F32)),
        grid=(S // ts,),
        in_specs=[pl.BlockSpec((ts, C2), lambda i: (i, 0)),
                  pl.BlockSpec((HALO, C2), lambda i: (jnp.maximum(i * per - 1, 0), 0)),
                  pl.BlockSpec((CONV_A_WIDTH, C), lambda i: (0, 0)), vec, vec, vec],
        out_specs=(row, row, row),
        scratch_shapes=[pltpu.VMEM((HALO + ts, C), F32)],
        compiler_params=_cparams("parallel"),
    )(uc, uc, conv_w, conv_b, ln_g, ln_b)


CONV_ACC_ROWS = 40


def _conv_bwd_ln(ct, dcat, hglu, ln_g, ln_b, *, name):
    S, C = ct.shape
    CW = dcat.shape[1]
    ts = _tile(S, (256,))
    per = ts // HALO

    def body(ct_ref, dcat_ref, hc_ref, hh_ref, g_ref, beta_ref, dc_ref, acc_ref, ext_ref):
        i = pl.program_id(0)
        ct = ct_ref[...]
        gv = g_ref[...]
        mu = jnp.mean(ct, axis=-1, keepdims=True)
        xc = ct - mu
        rstd = lax.rsqrt(jnp.mean(xc * xc, axis=-1, keepdims=True) + LN_EPS)
        xh = xc * rstd
        l = xh * gv + beta_ref[...]
        sg = jax.nn.sigmoid(l)
        dl = dcat_ref[:, 0:C] * (sg * (1.0 + l * (1.0 - sg)))
        dxh = dl * gv
        dc = rstd * (dxh - jnp.mean(dxh, axis=-1, keepdims=True)
                     - xh * jnp.mean(dxh * xh, axis=-1, keepdims=True))
        dc_ref[...] = dc

        @pl.when(i == 0)
        def _():
            acc_ref[...] = jnp.zeros_like(acc_ref)

        acc_ref[32:33, :] += jnp.sum(dc, axis=0, keepdims=True)
        acc_ref[33:34, :] += jnp.sum(dl * xh, axis=0, keepdims=True)
        acc_ref[34:35, :] += jnp.sum(dl, axis=0, keepdims=True)
        ext_ref[0:HALO, :] = jnp.where(i == 0, 0.0, hh_ref[...])
        ext_ref[HALO:HALO + ts, :] = hc_ref[...]
        for k in range(CONV_A_WIDTH):
            acc_ref[k:k + 1, :] += jnp.sum(dc * ext_ref[k + 2:k + 2 + ts, :], axis=0, keepdims=True)

    vec = pl.BlockSpec((1, C), lambda i: (0, 0))
    row = pl.BlockSpec((ts, C), lambda i: (i, 0))
    return pl.pallas_call(
        body, name=name,
        out_shape=(jax.ShapeDtypeStruct((S, C), F32), jax.ShapeDtypeStruct((CONV_ACC_ROWS, C), F32)),
        grid=(S // ts,),
        in_specs=[row, pl.BlockSpec((ts, CW), lambda i: (i, 0)), row,
                  pl.BlockSpec((HALO, C), lambda i: (jnp.maximum(i * per - 1, 0), 0)), vec, vec],
        out_specs=(row, pl.BlockSpec((CONV_ACC_ROWS, C), lambda i: (0, 0))),
        scratch_shapes=[pltpu.VMEM((HALO + ts, C), F32)],
        compiler_params=_cparams("arbitrary"),
    )(ct, dcat, hglu, hglu, ln_g, ln_b)


def _conv_bwd_in(dc, uc, conv_w, *, name):
    S, C = dc.shape
    C2 = 2 * C
    ts = _tile(S, (256,))
    per = ts // HALO
    nt = S // ts

    def body(dc_ref, dn_ref, uc_ref, w_ref, du_ref, ext_ref):
        i = pl.program_id(0)
        ext_ref[0:ts, :] = dc_ref[...]
        ext_ref[ts:ts + HALO, :] = jnp.where(i == nt - 1, 0.0, dn_ref[...])
        acc = jnp.zeros((ts, C), F32)
        for k in range(CONV_A_WIDTH):
            acc = acc + w_ref[k:k + 1, :] * ext_ref[30 - k:30 - k + ts, :]
        val = uc_ref[:, 0:C]
        sg = jax.nn.sigmoid(uc_ref[:, C:C2])
        du_ref[:, 0:C] = (acc * sg).astype(du_ref.dtype)
        du_ref[:, C:C2] = (acc * val * sg * (1.0 - sg)).astype(du_ref.dtype)

    return pl.pallas_call(
        body, name=name,
        out_shape=jax.ShapeDtypeStruct((S, C2), BF16),
        grid=(nt,),
        in_specs=[pl.BlockSpec((ts, C), lambda i: (i, 0)),
                  pl.BlockSpec((HALO, C), lambda i: (jnp.minimum((i + 1) * per, S // HALO - 1), 0)),
                  pl.BlockSpec((ts, C2), lambda i: (i, 0)),
                  pl.BlockSpec((CONV_A_WIDTH, C), lambda i: (0, 0))],
        out_specs=pl.BlockSpec((ts, C2), lambda i: (i, 0)),
        scratch_shapes=[pltpu.VMEM((ts + HALO, C), F32)],
        compiler_params=_cparams("parallel"),
    )(dc, dc, uc, conv_w)


def _sc_fwd(u3, conv_w, *, name):
    S, W3 = u3.shape
    W = W3 // 3
    ts = _tile(S, (256,))
    per = ts // HALO

    def body(cur_ref, halo_ref, w_ref, y_ref, ext_ref):
        i = pl.program_id(0)
        cvh = halo_ref[:, W:2 * W].astype(F32) * halo_ref[:, 2 * W:W3].astype(F32)
        ext_ref[0:HALO, :] = jnp.where(i == 0, 0.0, cvh)
        ext_ref[HALO:HALO + ts, :] = cur_ref[:, W:2 * W].astype(F32) * cur_ref[:, 2 * W:W3].astype(F32)
        k = (w_ref[0:1, :] * ext_ref[HALO - 2:HALO - 2 + ts, :]
             + w_ref[1:2, :] * ext_ref[HALO - 1:HALO - 1 + ts, :]
             + w_ref[2:3, :] * ext_ref[HALO:HALO + ts, :])
        y_ref[...] = (cur_ref[:, 0:W].astype(F32) * k).astype(y_ref.dtype)

    return pl.pallas_call(
        body, name=name,
        out_shape=jax.ShapeDtypeStruct((S, W), BF16),
        grid=(S // ts,),
        in_specs=[pl.BlockSpec((ts, W3), lambda i: (i, 0)),
                  pl.BlockSpec((HALO, W3), lambda i: (jnp.maximum(i * per - 1, 0), 0)),
                  pl.BlockSpec((SC_CONV_WIDTH, W), lambda i: (0, 0))],
        out_specs=pl.BlockSpec((ts, W), lambda i: (i, 0)),
        scratch_shapes=[pltpu.VMEM((HALO + ts, W), F32)],
        compiler_params=_cparams("parallel"),
    )(u3, u3, conv_w)


def _sc_bwd(u3, dy, conv_w, *, name):
    S, W3 = u3.shape
    W = W3 // 3
    ts = _tile(S, (256,))
    per = ts // HALO
    nt = S // ts

    def body(cur_ref, prev_ref, next_ref, dy_ref, dyn_ref, w_ref, du_ref, dw_ref, cv_ext, dk_ext):
        i = pl.program_id(0)
        cvh = prev_ref[:, W:2 * W].astype(F32) * prev_ref[:, 2 * W:W3].astype(F32)
        cv_ext[0:HALO, :] = jnp.where(i == 0, 0.0, cvh)
        c = cur_ref[:, W:2 * W].astype(F32)
        v = cur_ref[:, 2 * W:W3].astype(F32)
        b = cur_ref[:, 0:W].astype(F32)
        cv_ext[HALO:HALO + ts, :] = c * v
        dy_cur = dy_ref[...]
        dk = dy_cur * b
        dk_ext[0:ts, :] = dk
        dk_ext[ts:ts + HALO, :] = jnp.where(i == nt - 1, 0.0, dyn_ref[...] * next_ref[:, 0:W].astype(F32))
        w0, w1, w2 = w_ref[0:1, :], w_ref[1:2, :], w_ref[2:3, :]
        cv2 = cv_ext[HALO - 2:HALO - 2 + ts, :]
        cv1 = cv_ext[HALO - 1:HALO - 1 + ts, :]
        cv0 = cv_ext[HALO:HALO + ts, :]
        kconv = w0 * cv2 + w1 * cv1 + w2 * cv0
        dcv = w2 * dk + w1 * dk_ext[1:1 + ts, :] + w0 * dk_ext[2:2 + ts, :]
        du_ref[:, 0:W] = (dy_cur * kconv).astype(du_ref.dtype)
        du_ref[:, W:2 * W] = (dcv * v).astype(du_ref.dtype)
        du_ref[:, 2 * W:W3] = (dcv * c).astype(du_ref.dtype)

        @pl.when(i == 0)
        def _():
            dw_ref[...] = jnp.zeros_like(dw_ref)

        dw_ref[0:1, :] += jnp.sum(dk * cv2, axis=0, keepdims=True)
        dw_ref[1:2, :] += jnp.sum(dk * cv1, axis=0, keepdims=True)
        dw_ref[2:3, :] += jnp.sum(dk * cv0, axis=0, keepdims=True)

    nxt = lambda i: (jnp.minimum((i + 1) * per, S // HALO - 1), 0)
    return pl.pallas_call(
        body, name=name,
        out_shape=(jax.ShapeDtypeStruct((S, W3), BF16), jax.ShapeDtypeStruct((8, W), F32)),
        grid=(nt,),
        in_specs=[pl.BlockSpec((ts, W3), lambda i: (i, 0)),
                  pl.BlockSpec((HALO, W3), lambda i: (jnp.maximum(i * per - 1, 0), 0)),
                  pl.BlockSpec((HALO, W3), nxt),
                  pl.BlockSpec((ts, W), lambda i: (i, 0)),
                  pl.BlockSpec((HALO, W), nxt),
                  pl.BlockSpec((SC_CONV_WIDTH, W), lambda i: (0, 0))],
        out_specs=(pl.BlockSpec((ts, W3), lambda i: (i, 0)), pl.BlockSpec((8, W), lambda i: (0, 0))),
        scratch_shapes=[pltpu.VMEM((HALO + ts, W), F32), pltpu.VMEM((ts + HALO, W), F32)],
        compiler_params=_cparams("arbitrary"),
    )(u3, u3, u3, dy, dy, conv_w)


def _t5_causal_bucket(n):
    max_exact = NUM_BUCKETS // 2
    nf = jnp.maximum(n, 1).astype(F32)
    large = max_exact + (jnp.log(nf / max_exact) / math.log(REL_MAX_DISTANCE / max_exact)
                         * (NUM_BUCKETS - max_exact)).astype(jnp.int32)
    return jnp.where(n < max_exact, n, jnp.minimum(large, NUM_BUCKETS - 1))


def _bucket_tables():
    steps = ATTN_BLK
    m = jnp.arange(steps)[:, None] + steps - jnp.arange(2 * steps)[None, :]
    return jnp.stack([_t5_causal_bucket(jnp.clip(m, 0, steps) * dil).astype(F32) for _, dil in DILATED_GROUPS])


def _bias_tables(rel_bias, buckets, *, name):
    steps = ATTN_BLK

    def body(tab_ref, bk_ref, o_ref):
        g = pl.program_id(0)
        bk = bk_ref[0]
        a_idx = lax.broadcasted_iota(jnp.int32, (steps, 2 * steps), 0)
        c_idx = lax.broadcasted_iota(jnp.int32, (steps, 2 * steps), 1)
        m = a_idx + steps - c_idx
        band = (m >= 0) & (m <= steps)
        band_first = band & (c_idx >= steps)
        for h in range(HEADS_PER_GROUP):
            bias = jnp.zeros((steps, 2 * steps), F32)
            for b in range(NUM_BUCKETS):
                bias = jnp.where(bk == float(b), tab_ref[b, g * HEADS_PER_GROUP + h], bias)
            o_ref[0, 0, h] = jnp.where(band_first, bias, NEG_INF)
            o_ref[0, 1, h] = jnp.where(band, bias, NEG_INF)

    return pl.pallas_call(
        body, name=name,
        out_shape=jax.ShapeDtypeStruct((N_GROUPS, 2, HEADS_PER_GROUP, steps, 2 * steps), F32),
        grid=(N_GROUPS,),
        in_specs=[pl.BlockSpec(memory_space=pltpu.SMEM),
                  pl.BlockSpec((1, steps, 2 * steps), lambda g: (g, 0, 0))],
        out_specs=pl.BlockSpec((1, 2, HEADS_PER_GROUP, steps, 2 * steps), lambda g: (g, 0, 0, 0, 0)),
        compiler_params=_cparams("parallel"),
    )(rel_bias, buckets)


def _lane_is_low():
    return lax.broadcasted_iota(jnp.int32, (1, LANES), 1) < HEAD_DIM


def _qkv_specs(g):
    nqb = ATTN_IN // ATTN_OUT

    def spec(t, prev):
        def idx(r, n):
            row = jnp.maximum(n - 1, 0) if prev else n
            return (row, r * nqb + t * N_GROUPS + g)
        return pl.BlockSpec((ATTN_BLK, ATTN_OUT), idx)

    return [spec(0, False), spec(1, False), spec(1, True), spec(2, False), spec(2, True)]


def _attn_fwd(uq, bias, g, d, *, name):
    S = uq.shape[0]
    rows = S // d
    uv = uq.reshape(rows, d * ATTN_IN)

    def body(q_ref, kc_ref, kp_ref, vc_ref, vp_ref, bias_ref, o_ref, l_ref):
        n = pl.program_id(1)
        sel = jnp.minimum(n, 1)
        low = _lane_is_low()
        for hp in range(HEADS_PER_GROUP // 2):
            sl = slice(hp * LANES, (hp + 1) * LANES)
            q2 = q_ref[:, sl]
            k2 = jnp.concatenate([kp_ref[:, sl], kc_ref[:, sl]], axis=0)
            v2 = jnp.concatenate([vp_ref[:, sl], vc_ref[:, sl]], axis=0)
            outs, lses = [], []
            for hh in range(2):
                msk = low if hh == 0 else jnp.logical_not(low)
                qm = jnp.where(msk, q2, jnp.zeros_like(q2))
                logits = _dot_nt(qm, k2) * (HEAD_DIM ** -0.5) + bias_ref[sel, 2 * hp + hh]
                mx = jnp.max(logits, axis=-1, keepdims=True)
                p = jnp.exp(logits - mx)
                den = jnp.sum(p, axis=-1, keepdims=True)
                pv = jnp.dot(p.astype(BF16), v2, preferred_element_type=F32)
                outs.append(pv / den)
                lses.append(jnp.broadcast_to(mx + jnp.log(den), (ATTN_BLK, LANES)))
            o_ref[:, sl] = jnp.where(low, outs[0], outs[1])
            l_ref[:, sl] = jnp.where(low, lses[0], lses[1])

    out_spec = pl.BlockSpec((ATTN_BLK, ATTN_OUT), lambda r, n: (n, r))
    o, l = pl.pallas_call(
        body, name=name,
        out_shape=(jax.ShapeDtypeStruct((rows, d * ATTN_OUT), F32),) * 2,
        grid=(d, rows // ATTN_BLK),
        in_specs=_qkv_specs(g) + [pl.BlockSpec((None, 2, HEADS_PER_GROUP, ATTN_BLK, 2 * ATTN_BLK),
                                               lambda r, n: (g, 0, 0, 0, 0))],
        out_specs=(out_spec, out_spec),
        compiler_params=_cparams("parallel", "parallel"),
    )(uv, uv, uv, uv, uv, bias)
    return o.reshape(S, ATTN_OUT), l.reshape(S, ATTN_OUT)


def _attn_merge(outs, lses, ya, *, name):
    S, C = ya.shape
    tm = _tile(S, (256,))

    def body(o0, o1, o2, l0, l1, l2, ya_ref, cat_ref, out_ref, lse_ref):
        a0, a1, a2 = l0[...], l1[...], l2[...]
        m = jnp.maximum(jnp.maximum(a0, a1), a2)
        e0, e1, e2 = jnp.exp(a0 - m), jnp.exp(a1 - m), jnp.exp(a2 - m)
        den = e0 + e1 + e2
        out = (e0 * o0[...] + e1 * o1[...] + e2 * o2[...]) / den
        out_ref[...] = out
        lse_ref[...] = m + jnp.log(den)
        cat_ref[:, 0:C] = ya_ref[...]
        cat_ref[:, C:C + ATTN_OUT] = out.astype(cat_ref.dtype)

    blk = pl.BlockSpec((tm, ATTN_OUT), lambda i: (i, 0))
    return pl.pallas_call(
        body, name=name,
        out_shape=(jax.ShapeDtypeStruct((S, C + ATTN_OUT), BF16), jax.ShapeDtypeStruct((S, ATTN_OUT), F32),
                   jax.ShapeDtypeStruct((S, ATTN_OUT), F32)),
        grid=(S // tm,),
        in_specs=[blk] * 6 + [pl.BlockSpec((tm, C), lambda i: (i, 0))],
        out_specs=(pl.BlockSpec((tm, C + ATTN_OUT), lambda i: (i, 0)), blk, blk),
        compiler_params=_cparams("parallel"),
    )(*outs, *lses, ya)


def _attn_prep(dcat, outf, *, name):
    S, CW = dcat.shape
    C = CW - ATTN_OUT
    tm = _tile(S, (256,))
    ones = np.kron(np.eye(HEADS_PER_GROUP, dtype=np.float32), np.ones((HEAD_DIM, HEAD_DIM), np.float32))

    def body(dcat_ref, out_ref, ones_ref, dyb_ref, dl_ref):
        dyb = dcat_ref[:, C:CW]
        dyb_ref[...] = dyb.astype(BF16)
        prod = dyb * out_ref[...]
        ov = ones_ref[...]
        hi, mid, lo = _split_bf16(prod)
        dl_ref[...] = (jnp.dot(hi, ov, preferred_element_type=F32)
                       + jnp.dot(mid, ov, preferred_element_type=F32)
                       + jnp.dot(lo, ov, preferred_element_type=F32))

    blk = pl.BlockSpec((tm, ATTN_OUT), lambda i: (i, 0))
    return pl.pallas_call(
        body, name=name,
        out_shape=(jax.ShapeDtypeStruct((S, ATTN_OUT), BF16), jax.ShapeDtypeStruct((S, ATTN_OUT), F32)),
        grid=(S // tm,),
        in_specs=[pl.BlockSpec((tm, CW), lambda i: (i, 0)), blk,
                  pl.BlockSpec((ATTN_OUT, ATTN_OUT), lambda i: (0, 0))],
        out_specs=(blk, blk),
        compiler_params=_cparams("parallel"),
    )(dcat, outf, jnp.asarray(ones, BF16))


def _attn_bwd(uq, dyb, lse, delta, bias, g, d, *, name):
    S = uq.shape[0]
    rows = S // d
    nb = rows // ATTN_BLK
    uv = uq.reshape(rows, d * ATTN_IN)
    dov = dyb.reshape(rows, d * ATTN_OUT)
    lv = lse.reshape(rows, d * ATTN_OUT)
    dv_ = delta.reshape(rows, d * ATTN_OUT)
    scale = HEAD_DIM ** -0.5

    def body(q_ref, kc_ref, kp_ref, vc_ref, vp_ref, do_ref, l_ref, dl_ref, bias_ref,
             out_ref, db_ref, dq_s, dk_s, dv_s):
        r = pl.program_id(0)
        n = pl.program_id(1)
        low = _lane_is_low()

        @pl.when((r == 0) & (n == 0))
        def _():
            db_ref[...] = jnp.zeros_like(db_ref)

        @pl.when(n == 0)
        def _():
            dq_s[...] = jnp.zeros_like(dq_s)
            dk_s[...] = jnp.zeros_like(dk_s)
            dv_s[...] = jnp.zeros_like(dv_s)

        @pl.when(n < nb)
        def _():
            sel = jnp.minimum(n, 1)
            lane = lax.broadcasted_iota(jnp.int32, (1, LANES), 1)
            for hp in range(HEADS_PER_GROUP // 2):
                sl = slice(hp * LANES, (hp + 1) * LANES)
                q2 = q_ref[:, sl]
                do2 = do_ref[:, sl]
                k2 = jnp.concatenate([kp_ref[:, sl], kc_ref[:, sl]], axis=0)
                v2 = jnp.concatenate([vp_ref[:, sl], vc_ref[:, sl]], axis=0)
                lse2 = l_ref[:, sl]
                dl2 = dl_ref[:, sl]
                dqs, dks, dvs = [], [], []
                for hh in range(2):
                    msk = low if hh == 0 else jnp.logical_not(low)
                    one = lane == hh * HEAD_DIM
                    lse_col = jnp.sum(jnp.where(one, lse2, 0.0), axis=-1, keepdims=True)
                    dl_col = jnp.sum(jnp.where(one, dl2, 0.0), axis=-1, keepdims=True)
                    qm = jnp.where(msk, q2, jnp.zeros_like(q2))
                    dom = jnp.where(msk, do2, jnp.zeros_like(do2))
                    logits = _dot_nt(qm, k2) * scale + bias_ref[sel, 2 * hp + hh]
                    p = jnp.exp(logits - lse_col)
                    dp = _dot_nt(dom, v2)
                    ds = p * (dp - dl_col)
                    db_ref[2 * hp + hh] += ds
                    dsb = ds.astype(BF16)
                    dqs.append(jnp.dot(dsb, k2, preferred_element_type=F32) * scale)
                    dks.append(_dot_tn(dsb, q2) * scale)
                    dvs.append(_dot_tn(p.astype(BF16), do2))
                dq2 = jnp.where(low, dqs[0], dqs[1])
                dk2 = jnp.where(low, dks[0], dks[1])
                dv2 = jnp.where(low, dvs[0], dvs[1])
                out_ref[:, sl] = dq_s[:, sl].astype(out_ref.dtype)
                dq_s[:, sl] = dq2
                ksl = slice(ATTN_OUT + hp * LANES, ATTN_OUT + (hp + 1) * LANES)
                vsl = slice(2 * ATTN_OUT + hp * LANES, 2 * ATTN_OUT + (hp + 1) * LANES)
                out_ref[:, ksl] = (dk_s[:, sl] + dk2[0:ATTN_BLK]).astype(out_ref.dtype)
                dk_s[:, sl] = dk2[ATTN_BLK:2 * ATTN_BLK]
                out_ref[:, vsl] = (dv_s[:, sl] + dv2[0:ATTN_BLK]).astype(out_ref.dtype)
                dv_s[:, sl] = dv2[ATTN_BLK:2 * ATTN_BLK]

        @pl.when(n == nb)
        def _():
            out_ref[:, 0:ATTN_OUT] = dq_s[...].astype(out_ref.dtype)
            out_ref[:, ATTN_OUT:2 * ATTN_OUT] = dk_s[...].astype(out_ref.dtype)
            out_ref[:, 2 * ATTN_OUT:GROUP_QKV] = dv_s[...].astype(out_ref.dtype)

    nqb = ATTN_IN // ATTN_OUT

    def spec(t, prev):
        def idx(r, n):
            nn = jnp.minimum(n, nb - 1)
            row = jnp.maximum(nn - 1, 0) if prev else nn
            return (row, r * nqb + t * N_GROUPS + g)
        return pl.BlockSpec((ATTN_BLK, ATTN_OUT), idx)

    rowblk = pl.BlockSpec((ATTN_BLK, ATTN_OUT), lambda r, n: (jnp.minimum(n, nb - 1), r))
    dqkv, db = pl.pallas_call(
        body, name=name,
        out_shape=(jax.ShapeDtypeStruct((rows, d * GROUP_QKV), BF16),
                   jax.ShapeDtypeStruct((HEADS_PER_GROUP, ATTN_BLK, 2 * ATTN_BLK), F32)),
        grid=(d, nb + 1),
        in_specs=[spec(0, False), spec(1, False), spec(1, True), spec(2, False), spec(2, True),
                  rowblk, rowblk, rowblk,
                  pl.BlockSpec((None, 2, HEADS_PER_GROUP, ATTN_BLK, 2 * ATTN_BLK), lambda r, n: (g, 0, 0, 0, 0))],
        out_specs=(pl.BlockSpec((ATTN_BLK, GROUP_QKV), lambda r, n: (jnp.maximum(n - 1, 0), r)),
                   pl.BlockSpec((HEADS_PER_GROUP, ATTN_BLK, 2 * ATTN_BLK), lambda r, n: (0, 0, 0))),
        scratch_shapes=[pltpu.VMEM((ATTN_BLK, ATTN_OUT), F32)] * 3,
        compiler_params=_cparams("arbitrary", "arbitrary"),
    )(uv, uv, uv, uv, uv, dov, lv, dv_, bias)
    return dqkv.reshape(S, GROUP_QKV), db


def _split_bf16(x):
    hi = x.astype(BF16)
    r1 = x - hi.astype(F32)
    mid = r1.astype(BF16)
    lo = (r1 - mid.astype(F32)).astype(BF16)
    return hi, mid, lo


RELBIAS_CHUNK = 4096


def _relbias_reduce(dbs, buckets, *, name):
    flat = ATTN_BLK * 2 * ATTN_BLK
    dbf = jnp.stack([db.reshape(HEADS_PER_GROUP, flat) for db in dbs])
    bkf = buckets.reshape(N_GROUPS, 1, flat)

    def body(db_ref, bk_ref, o_ref):
        c = pl.program_id(1)
        rows = lax.broadcasted_iota(jnp.int32, (LANES, RELBIAS_CHUNK), 0).astype(F32)
        onehot = jnp.where(rows == bk_ref[0], 1.0, 0.0).astype(BF16)
        hi, mid, lo = _split_bf16(db_ref[0])
        part = _dot_nt(hi, onehot) + _dot_nt(mid, onehot) + _dot_nt(lo, onehot)

        @pl.when(c == 0)
        def _():
            o_ref[0] = part

        @pl.when(c > 0)
        def _():
            o_ref[0] += part

    return pl.pallas_call(
        body, name=name,
        out_shape=jax.ShapeDtypeStruct((N_GROUPS, HEADS_PER_GROUP, LANES), F32),
        grid=(N_GROUPS, flat // RELBIAS_CHUNK),
        in_specs=[pl.BlockSpec((1, HEADS_PER_GROUP, RELBIAS_CHUNK), lambda g, c: (g, 0, c)),
                  pl.BlockSpec((1, 1, RELBIAS_CHUNK), lambda g, c: (g, 0, c))],
        out_specs=pl.BlockSpec((1, HEADS_PER_GROUP, LANES), lambda g, c: (g, 0, 0)),
        compiler_params=_cparams("parallel", "arbitrary"),
    )(dbf, bkf)


def _my_position():
    x, y, c = lax.axis_index("x"), lax.axis_index("y"), lax.axis_index("c")
    return x, y, c


def _linear(pos):
    return 4 * pos[0] + 2 * pos[1] + pos[2]


def _peer(pos, k):
    x, y, c = pos
    return ((1 - x) if k & 4 else x, (1 - y) if k & 2 else y, (1 - c) if k & 1 else c)


HBM_SPEC = pl.BlockSpec(memory_space=pltpu.HBM)
SEM_SPEC = pl.BlockSpec(memory_space=pltpu.SEMAPHORE)
DATAFLOW = pltpu.SideEffectType.DATAFLOW_SIDE_EFFECTING


def _exchange_copies(src, land, sems, send_window, recv_window, with_arrivals):
    send_sems, recv_sems, local_sems = sems
    T = len(src)
    me = _my_position()
    me_lin = _linear(me)
    local = [pltpu.make_async_copy(send_window(t, src[t], me_lin), recv_window(t, land[t], me_lin),
                                   local_sems.at[t]) for t in range(T)]
    sends, arrivals = [], []
    for t in range(T):
        for k in range(1, N_DEV):
            peer = _peer(me, k)
            peer_lin = _linear(peer)
            sem = t * (N_DEV - 1) + k - 1
            sends.append(pltpu.make_async_remote_copy(
                src_ref=send_window(t, src[t], peer_lin), dst_ref=recv_window(t, land[t], me_lin),
                send_sem=send_sems.at[sem], recv_sem=recv_sems.at[sem],
                device_id=peer, device_id_type=MESH))
            if with_arrivals:
                arrivals.append(pltpu.make_async_remote_copy(
                    src_ref=send_window(t, src[t], me_lin), dst_ref=recv_window(t, land[t], peer_lin),
                    send_sem=send_sems.at[sem], recv_sem=recv_sems.at[sem],
                    device_id=peer, device_id_type=MESH))
    return local, sends, arrivals


def _exchange_start(srcs, land_shapes, send_window, recv_window, *, name, dep=None):
    T = len(srcs)
    n_in = 2 * T + (1 if dep is not None else 0)

    def body(*refs):
        src = refs[:T]
        land = refs[T:2 * T]
        sems = refs[n_in:n_in + 3]
        token = refs[-1]
        local, sends, _ = _exchange_copies(src, land, sems, send_window, recv_window, False)
        for cp in local + sends:
            cp.start()
        token[...] = jnp.zeros_like(token)

    lands = [lax.empty(ls.shape, ls.dtype) for ls in land_shapes]
    operands = [pltpu.with_memory_space_constraint(a, pltpu.HBM) for a in list(srcs) + lands]
    outs = pl.pallas_call(
        body, name=name,
        out_shape=(pltpu.SemaphoreType.DMA((T * (N_DEV - 1),)), pltpu.SemaphoreType.DMA((T * (N_DEV - 1),)),
                   pltpu.SemaphoreType.DMA((T,)),
                   *[pltpu.HBM(a.shape, a.dtype) for a in operands],
                   jax.ShapeDtypeStruct((8, LANES), F32)),
        in_specs=[HBM_SPEC] * (2 * T) + ([ANY] if dep is not None else []),
        out_specs=(SEM_SPEC,) * 3 + (HBM_SPEC,) * (2 * T) + (VMEM_SPEC,),
        input_output_aliases={i: 3 + i for i in range(2 * T)},
        compiler_params=pltpu.CompilerParams(has_side_effects=DATAFLOW),
    )(*operands, *([dep] if dep is not None else []))
    return outs[:3], outs[3:3 + T], outs[3 + T:3 + 2 * T], outs[-1]


def _exchange_wait(started, after, send_window, recv_window, *, name):
    sems, srcs, lands, _ = started
    T = len(srcs)

    def body(*refs):
        src = refs[:T]
        land = refs[T:2 * T]
        sem_refs = refs[2 * T:2 * T + 3]
        local, sends, arrivals = _exchange_copies(src, land, sem_refs, send_window, recv_window, True)
        for cp in arrivals:
            cp.wait_recv()
        for cp in sends:
            cp.wait_send()
        for cp in local:
            cp.wait()

    outs = pl.pallas_call(
        body, name=name,
        out_shape=tuple(pltpu.HBM(a.shape, a.dtype) for a in list(srcs) + list(lands)),
        in_specs=[HBM_SPEC] * (2 * T) + [SEM_SPEC] * 3 + [ANY],
        out_specs=(HBM_SPEC,) * (2 * T),
        input_output_aliases={i: i for i in range(2 * T)},
        compiler_params=pltpu.CompilerParams(has_side_effects=DATAFLOW),
    )(*srcs, *lands, *sems, after)
    return outs[T:]


def _shard_window(kind, width):
    def win(ref, lin):
        if kind == "slot":
            return ref.at[lin]
        if kind == "col":
            return ref.at[:, pl.ds(pl.multiple_of(lin * width, LANES), width)]
        if kind == "row":
            return ref.at[pl.ds(pl.multiple_of(lin * width, 8), width), :]
        if kind == "lcol":
            return ref.at[:, :, pl.ds(pl.multiple_of(lin * width, LANES), width)]
        if kind == "lrow":
            return ref.at[:, pl.ds(pl.multiple_of(lin * width, 8), width), :]
        raise ValueError(kind)
    return win


def _shard_windows(kinds, shard_shapes):
    return [_shard_window(k, (ss[-1] if k in ("col", "lcol") else ss[-2])) for k, ss in zip(kinds, shard_shapes)]


def _allgather_start(shards, kinds, full_shapes, *, name, dep=None):
    wins = _shard_windows(kinds, [s.shape for s in shards])
    send_window = lambda t, ref, lin: ref
    recv_window = lambda t, ref, lin: wins[t](ref, lin)
    started = _exchange_start(shards, [jax.ShapeDtypeStruct(fs, s.dtype) for fs, s in zip(full_shapes, shards)],
                              send_window, recv_window, name=name + "_start", dep=dep)
    return started, lambda after: _exchange_wait(started, after, send_window, recv_window, name=name + "_wait")


def _scatter_start(fulls, kinds, shard_shapes, *, name):
    wins = _shard_windows(kinds, shard_shapes)
    send_window = lambda t, ref, lin: wins[t](ref, lin)
    recv_window = lambda t, ref, lin: ref.at[lin]
    started = _exchange_start(
        fulls, [jax.ShapeDtypeStruct((N_DEV,) + tuple(ss), f.dtype) for ss, f in zip(shard_shapes, fulls)],
        send_window, recv_window, name=name + "_start")
    return started, lambda after: _exchange_wait(started, after, send_window, recv_window, name=name + "_wait")


def _small_gather(pack, *, reduce, name):
    R = pack.shape[0]

    def body(p_ref, o_ref, *rest):
        if reduce:
            buf, send_sems, recv_sems = rest
        else:
            buf = o_ref
            send_sems, recv_sems = rest
        me = _my_position()
        me_lin = _linear(me)
        buf[me_lin] = p_ref[...]
        sends = []
        for k in range(1, N_DEV):
            peer = _peer(me, k)
            cp = pltpu.make_async_remote_copy(
                src_ref=p_ref, dst_ref=buf.at[me_lin],
                send_sem=send_sems.at[k - 1], recv_sem=recv_sems.at[k - 1],
                device_id=peer, device_id_type=MESH)
            cp.start()
            sends.append(cp)
        for k in range(1, N_DEV):
            peer = _peer(me, k)
            pltpu.make_async_remote_copy(
                src_ref=p_ref, dst_ref=buf.at[_linear(peer)],
                send_sem=send_sems.at[k - 1], recv_sem=recv_sems.at[k - 1],
                device_id=peer, device_id_type=MESH).wait_recv()
        for cp in sends:
            cp.wait_send()
        if reduce:
            acc = buf[0]
            for s in range(1, N_DEV):
                acc = acc + buf[s]
            o_ref[...] = acc

    scratch = [pltpu.SemaphoreType.DMA((N_DEV - 1,)), pltpu.SemaphoreType.DMA((N_DEV - 1,))]
    if reduce:
        scratch = [pltpu.VMEM((N_DEV, R, LANES), F32)] + scratch
        out_shape = jax.ShapeDtypeStruct((R, LANES), F32)
    else:
        out_shape = jax.ShapeDtypeStruct((N_DEV, R, LANES), F32)
    return pl.pallas_call(
        body, name=name, out_shape=out_shape,
        in_specs=[VMEM_SPEC], out_specs=VMEM_SPEC, scratch_shapes=scratch,
        compiler_params=pltpu.CompilerParams(has_side_effects=True, vmem_limit_bytes=VMEM_LIMIT),
    )(pack)


def _adamw_math(w, g, m, v):
    m = ADAM_B1 * m + (1.0 - ADAM_B1) * g
    v = ADAM_B2 * v + (1.0 - ADAM_B2) * jnp.square(g)
    m_hat = m / (1.0 - ADAM_B1 ** ADAM_STEP)
    v_hat = v / (1.0 - ADAM_B2 ** ADAM_STEP)
    delta = -ADAM_LR * (m_hat / (jnp.sqrt(v_hat) + ADAM_EPS) + ADAM_WD * w)
    return delta, m, v


def _adamw_from_partials(parts, w, m, v, *, name):
    R, C = w.shape
    rl = parts[0].shape[1]
    assert all(p.shape == (N_DEV, rl, C) for p in parts) and rl * len(parts) == R
    tr = _tile(rl, (256, 128, 64, 32, 16))
    per = rl // tr
    L = len(parts)

    def body(*refs):
        p_refs = refs[:L]
        w_ref, m_ref, v_ref, g_ref, d_ref, nm_ref, nv_ref = refs[L:]
        i = pl.program_id(0)
        for l in range(L):
            @pl.when((i >= l * per) & (i < (l + 1) * per))
            def _(l=l):
                p_ref = p_refs[l]
                g = p_ref[0].astype(F32)
                for s in range(1, N_DEV):
                    g = g + p_ref[s].astype(F32)
                d, nm, nv = _adamw_math(w_ref[...], g, m_ref[...], v_ref[...])
                g_ref[...] = g
                d_ref[...] = d
                nm_ref[...] = nm
                nv_ref[...] = nv

    blk = pl.BlockSpec((tr, C), lambda i: (i, 0))
    part_specs = [pl.BlockSpec((N_DEV, tr, C), lambda i, l=l: (0, jnp.clip(i - l * per, 0, per - 1), 0))
                  for l in range(L)]
    return pl.pallas_call(
        body, name=name,
        out_shape=(jax.ShapeDtypeStruct((R, C), F32),) * 4,
        grid=(R // tr,),
        in_specs=part_specs + [blk, blk, blk],
        out_specs=(blk,) * 4,
        compiler_params=_cparams("parallel"),
    )(*parts, w, m, v)


def _adamw_small(g, w, m, v, *, name):
    def body(g_ref, w_ref, m_ref, v_ref, d_ref, nm_ref, nv_ref):
        d, nm, nv = _adamw_math(w_ref[...], g_ref[...], m_ref[...], v_ref[...])
        d_ref[...] = d
        nm_ref[...] = nm
        nv_ref[...] = nv

    return pl.pallas_call(
        body, name=name,
        out_shape=(jax.ShapeDtypeStruct(g.shape, F32),) * 3,
        in_specs=[VMEM_SPEC] * 4, out_specs=(VMEM_SPEC,) * 3,
    )(g, w, m, v)


def _pack_rows(pieces):
    flat = jnp.concatenate([p.reshape(-1) for p in pieces])
    n = flat.shape[0]
    padded = -(-n // (8 * LANES)) * (8 * LANES)
    return jnp.pad(flat, (0, padded - n)).reshape(padded // LANES, LANES)


def _unpack_rows(pack, shapes):
    flat = pack.reshape(-1)
    out, pos = [], 0
    for s in shapes:
        n = int(np.prod(s))
        out.append(flat[pos:pos + n].reshape(s))
        pos += n
    return out


def kernel(x, rel_bias, ab_norm, ab_w_in, ab_conv_w, ab_conv_b, ab_ln_g, ab_ln_b, ab_w_out, sc_norm, sc_w_in, sc_conv_w, sc_w_out, mlp_norm, mlp_w_up, mlp_w_down, final_norm, loss_target, m_rel_bias, m_ab_norm, m_ab_w_in, m_ab_conv_w, m_ab_conv_b, m_ab_ln_g, m_ab_ln_b, m_ab_w_out, m_sc_norm, m_sc_w_in, m_sc_conv_w, m_sc_w_out, m_mlp_norm, m_mlp_w_up, m_mlp_w_down, m_final_norm, v_rel_bias, v_ab_norm, v_ab_w_in, v_ab_conv_w, v_ab_conv_b, v_ab_ln_g, v_ab_ln_b, v_ab_w_out, v_sc_norm, v_sc_w_in, v_sc_conv_w, v_sc_w_out, v_mlp_norm, v_mlp_w_up, v_mlp_w_down, v_final_norm):
    S, D = x.shape[1], x.shape[2]
    CA = ab_conv_b.shape[1]
    C2 = 2 * CA
    AB_IN = C2 + ATTN_IN
    me_lin = _linear(_my_position())
    xs = x.reshape(S, D)
    tgt = loss_target.reshape(S, D)

    cw_sh = ab_conv_w.shape[2]
    scn_sh = sc_norm.shape[1]
    scw_sh = sc_conv_w.shape[2]
    small_sh_shapes = [(CONV_A_WIDTH, cw_sh), (scn_sh,), (SC_CONV_WIDTH, scw_sh)]
    small_params = _small_gather(_pack_rows([ab_conv_w[0], sc_norm[0], sc_conv_w[0]]), reduce=False,
                                 name="allgather_small_params")
    w_in_sh = ab_w_in[0].astype(BF16)
    ag_ab, wait_ab = _allgather_start(
        [w_in_sh, ab_w_out[0].astype(BF16)], ["slot", "row"],
        [(N_DEV,) + w_in_sh.shape, (N_DEV * ab_w_out.shape[1], D)], name="allgather_ab", dep=small_params)
    ag_mlp, wait_mlp = _allgather_start(
        [mlp_w_up.astype(BF16), mlp_w_down.astype(BF16)], ["lcol", "lrow"],
        [(2, D, N_DEV * mlp_w_up.shape[2]), (2, N_DEV * mlp_w_down.shape[1], D)], name="allgather_mlp",
        dep=ag_ab[3])
    ag_sc, wait_sc = _allgather_start(
        [sc_w_in[0].astype(BF16), sc_w_out[0].astype(BF16)], ["col", "row"],
        [(D, N_DEV * sc_w_in.shape[2]), (N_DEV * sc_w_out.shape[1], D)], name="allgather_sc",
        dep=ag_mlp[3])

    per_dev = [_unpack_rows(small_params[s], small_sh_shapes) for s in range(N_DEV)]
    conv_w_full = jnp.concatenate([p[0] for p in per_dev], axis=1)
    sc_norm_full = jnp.concatenate([p[1] for p in per_dev], axis=0)[None]
    sc_conv_full = jnp.concatenate([p[2] for p in per_dev], axis=1)

    buckets = _bucket_tables()
    biases = _bias_tables(rel_bias, buckets, name="bias_tables")

    n0 = _rmsnorm_fwd(xs, ab_norm, name="norm_ab", dep=ag_sc[3])
    w_in_g, w_out = wait_ab(n0)
    w_in = jnp.transpose(w_in_g, (1, 0, 2)).reshape(D, AB_IN)
    w_c = w_in[:, :C2]
    w_q = w_in[:, C2:]
    w_grp = [jnp.concatenate([w_q[:, t * N_GROUPS * ATTN_OUT + g * ATTN_OUT:][:, :ATTN_OUT] for t in range(3)], axis=1)
             for g in range(N_GROUPS)]
    uc = _mm_nn(n0, w_c, out_dtype=F32, name="mm_ab_in_conv")
    uq = _mm_nn(n0, w_q, out_dtype=BF16, name="mm_ab_in_qkv")
    ya, hglu, ct = _conv_fwd(uc, conv_w_full, ab_conv_b, ab_ln_g, ab_ln_b, name="conv_fwd")
    outs, lses = zip(*[_attn_fwd(uq, biases, g, dil, name=f"attn_fwd_{g}")
                       for g, (_, dil) in enumerate(DILATED_GROUPS)])
    cat, outf, lse = _attn_merge(outs, lses, ya, name="attn_merge")
    h1 = _mm_nn(cat, w_out, out_dtype=F32, residual=xs, name="mm_ab_out")
    n1 = _rmsnorm_fwd(h1, mlp_norm[0:1], name="norm_mlp0")
    w_up, w_dn = wait_mlp(n1)
    z0 = _mm_nn(n1, w_up[0], out_dtype=BF16, name="mm_up0")
    h2 = _mm_nn(z0, w_dn[0], out_dtype=F32, residual=h1, a_fn=_relu_sq, name="mm_down0")
    n2 = _rmsnorm_fwd(h2, sc_norm_full, name="norm_sc")
    w_sc_in, w_sc_out = wait_sc(n2)
    u3 = _mm_nn(n2, w_sc_in, out_dtype=BF16, name="mm_sc_in")
    ysc = _sc_fwd(u3, sc_conv_full, name="sc_fwd")
    h3 = _mm_nn(ysc, w_sc_out, out_dtype=F32, residual=h2, name="mm_sc_out")
    n3 = _rmsnorm_fwd(h3, mlp_norm[1:2], name="norm_mlp1")
    z1 = _mm_nn(n3, w_up[1], out_dtype=BF16, name="mm_up1")
    h4 = _mm_nn(z1, w_dn[1], out_dtype=F32, residual=h3, a_fn=_relu_sq, name="mm_down1")

    def dz_epilogue(acc, z):
        return acc * (2.0 * jnp.maximum(z.astype(F32), 0.0))

    dh4, dh4b, acc_final = _loss_bwd(h4, tgt, final_norm[None], name="loss_bwd")
    dz1 = _mm_nt([(dh4b, w_dn[1])], out_dtype=BF16, epilogue=dz_epilogue, extra=z1, name="mm_d_down1")
    g_dn1 = _mm_tn(z1, dh4b, a_fn=_relu_sq, name="mm_gw_down1")
    g_up1 = _mm_tn(n3, dz1, name="mm_gw_up1")
    rs_mlp1, wait_rs_mlp1 = _scatter_start([g_up1, g_dn1], ["col", "row"],
                                           [mlp_w_up.shape[1:], mlp_w_down.shape[1:]], name="scatter_mlp1")
    dn3 = _mm_nt([(dz1, w_up[1])], out_dtype=F32, name="mm_d_up1", dep=rs_mlp1[3])
    dh3, dh3b, acc_mlp1 = _rms_bwd(h3, mlp_norm[1:2], dn3, dh4, name="norm_mlp1_bwd")

    dysc = _mm_nt([(dh3b, w_sc_out)], out_dtype=F32, name="mm_d_sc_out")
    g_sc_out = _mm_tn(ysc, dh3b, name="mm_gw_sc_out")
    du3, acc_scw = _sc_bwd(u3, dysc, sc_conv_full, name="sc_bwd")
    g_sc_in = _mm_tn(n2, du3, name="mm_gw_sc_in")
    rs_sc, wait_rs_sc = _scatter_start([g_sc_in, g_sc_out], ["col", "row"],
                                       [sc_w_in.shape[1:], sc_w_out.shape[1:]], name="scatter_sc")
    dn2 = _mm_nt([(du3, w_sc_in)], out_dtype=F32, name="mm_d_sc_in", dep=rs_sc[3])
    dh2, dh2b, acc_sc = _rms_bwd(h2, sc_norm_full, dn2, dh3, name="norm_sc_bwd")

    dz0 = _mm_nt([(dh2b, w_dn[0])], out_dtype=BF16, epilogue=dz_epilogue, extra=z0, name="mm_d_down0")
    g_dn0 = _mm_tn(z0, dh2b, a_fn=_relu_sq, name="mm_gw_down0")
    g_up0 = _mm_tn(n1, dz0, name="mm_gw_up0")
    rs_mlp0, wait_rs_mlp0 = _scatter_start([g_up0, g_dn0], ["col", "row"],
                                           [mlp_w_up.shape[1:], mlp_w_down.shape[1:]], name="scatter_mlp0")
    dn1 = _mm_nt([(dz0, w_up[0])], out_dtype=F32, name="mm_d_up0", dep=rs_mlp0[3])
    dh1, dh1b, acc_mlp0 = _rms_bwd(h1, mlp_norm[0:1], dn1, dh2, name="norm_mlp0_bwd")

    dcat = _mm_nt([(dh1b, w_out)], out_dtype=F32, name="mm_d_ab_out")
    g_ab_out = _mm_tn(cat, dh1b, name="mm_gw_ab_out")
    dyb, delta = _attn_prep(dcat, outf, name="attn_prep")
    dqkv, dbs = zip(*[_attn_bwd(uq, dyb, lse, delta, biases, g, dil, name=f"attn_bwd_{g}")
                      for g, (_, dil) in enumerate(DILATED_GROUPS)])
    drel = _relbias_reduce(dbs, buckets, name="relbias_reduce")
    dc, acc_conv = _conv_bwd_ln(ct, dcat, hglu, ab_ln_g, ab_ln_b, name="conv_bwd_ln")
    duc = _conv_bwd_in(dc, uc, conv_w_full, name="conv_bwd_in")
    g_wc = _mm_tn(n0, duc, name="mm_gw_ab_in_conv")
    g_wgrp = [_mm_tn(n0, dqkv[g], name=f"mm_gw_ab_in_qkv{g}") for g in range(N_GROUPS)]
    g_wq = jnp.concatenate([g_wgrp[g][:, t * ATTN_OUT:(t + 1) * ATTN_OUT]
                            for t in range(3) for g in range(N_GROUPS)], axis=1)
    g_w_in = jnp.concatenate([g_wc, g_wq], axis=1).reshape(D, N_DEV, AB_IN // N_DEV).transpose(1, 0, 2)
    rs_ab, wait_rs_ab = _scatter_start([g_w_in, g_ab_out], ["slot", "row"],
                                       [w_in_sh.shape, ab_w_out.shape[1:]], name="scatter_ab")
    dn0 = _mm_nt([(duc, w_c)] + [(dqkv[g], w_grp[g]) for g in range(N_GROUPS)], out_dtype=F32, name="mm_d_ab_in",
                 dep=rs_ab[3])
    grad_x, grad_xb, acc_ab = _rms_bwd(xs, ab_norm, dn0, dh1, name="norm_ab_bwd")

    small_full = [drel[:, :, :NUM_BUCKETS].transpose(2, 0, 1).reshape(NUM_BUCKETS, N_GROUPS * HEADS_PER_GROUP),
                  acc_ab[0], acc_conv[0:CONV_A_WIDTH], acc_conv[32],
                  acc_conv[33], acc_conv[34], acc_sc[0], acc_scw[0:SC_CONV_WIDTH],
                  jnp.stack([acc_mlp0[0], acc_mlp1[0]]), acc_final[0], acc_final[1]]
    small_full_shapes = [p.shape for p in small_full]
    summed = _unpack_rows(_small_gather(_pack_rows(small_full), reduce=True, name="allreduce_small"),
                          small_full_shapes)
    (s_rel, s_abn, s_cw, s_cb, s_lg, s_lb, s_scn, s_scw, s_mlpn, s_fn, s_err) = summed
    loss = (0.5 / D) * jnp.sum(s_err)
    small_grads = {
        "rel_bias": s_rel, "ab_norm": s_abn[None],
        "ab_conv_w": lax.dynamic_slice_in_dim(s_cw, me_lin * cw_sh, cw_sh, axis=1)[None],
        "ab_conv_b": s_cb[None], "ab_ln_g": s_lg[None], "ab_ln_b": s_lb[None],
        "sc_norm": lax.dynamic_slice_in_dim(s_scn, me_lin * scn_sh, scn_sh, axis=0)[None],
        "sc_conv_w": lax.dynamic_slice_in_dim(s_scw, me_lin * scw_sh, scw_sh, axis=1)[None],
        "mlp_norm": s_mlpn, "final_norm": s_fn,
    }
    small_w = {"rel_bias": (rel_bias, m_rel_bias, v_rel_bias), "ab_norm": (ab_norm, m_ab_norm, v_ab_norm),
               "ab_conv_w": (ab_conv_w, m_ab_conv_w, v_ab_conv_w), "ab_conv_b": (ab_conv_b, m_ab_conv_b, v_ab_conv_b),
               "ab_ln_g": (ab_ln_g, m_ab_ln_g, v_ab_ln_g), "ab_ln_b": (ab_ln_b, m_ab_ln_b, v_ab_ln_b),
               "sc_norm": (sc_norm, m_sc_norm, v_sc_norm), "sc_conv_w": (sc_conv_w, m_sc_conv_w, v_sc_conv_w),
               "mlp_norm": (mlp_norm, m_mlp_norm, v_mlp_norm), "final_norm": (final_norm, m_final_norm, v_final_norm)}
    small_names = list(small_grads)
    small_shapes = [small_grads[n].shape for n in small_names]
    d_pack, m_pack, v_pack = _adamw_small(
        _pack_rows([small_grads[n] for n in small_names]), _pack_rows([small_w[n][0] for n in small_names]),
        _pack_rows([small_w[n][1] for n in small_names]), _pack_rows([small_w[n][2] for n in small_names]),
        name="adamw_small")
    small = {n: (small_grads[n], d, nm_, nv_) for n, d, nm_, nv_ in zip(
        small_names, _unpack_rows(d_pack, small_shapes), _unpack_rows(m_pack, small_shapes),
        _unpack_rows(v_pack, small_shapes))}

    p_up1, p_dn1 = wait_rs_mlp1(grad_xb)
    p_sc_in, p_sc_out = wait_rs_sc(grad_xb)
    p_up0, p_dn0 = wait_rs_mlp0(grad_xb)
    p_w_in, p_ab_out = wait_rs_ab(grad_xb)
    big = {}
    for nm, parts, w, m, v in (("ab_w_in", [p_w_in], ab_w_in, m_ab_w_in, v_ab_w_in),
                               ("ab_w_out", [p_ab_out], ab_w_out, m_ab_w_out, v_ab_w_out),
                               ("sc_w_in", [p_sc_in], sc_w_in, m_sc_w_in, v_sc_w_in),
                               ("sc_w_out", [p_sc_out], sc_w_out, m_sc_w_out, v_sc_w_out),
                               ("mlp_w_up", [p_up0, p_up1], mlp_w_up, m_mlp_w_up, v_mlp_w_up),
                               ("mlp_w_down", [p_dn0, p_dn1], mlp_w_down, m_mlp_w_down, v_mlp_w_down)):
        C = w.shape[-1]
        res = _adamw_from_partials(parts, w.reshape(-1, C), m.reshape(-1, C), v.reshape(-1, C), name="adamw_" + nm)
        big[nm] = tuple(r.reshape(w.shape) for r in res)

    order = ["rel_bias", "ab_norm", "ab_w_in", "ab_conv_w", "ab_conv_b", "ab_ln_g", "ab_ln_b", "ab_w_out",
             "sc_norm", "sc_w_in", "sc_conv_w", "sc_w_out", "mlp_norm", "mlp_w_up", "mlp_w_down", "final_norm"]
    allres = {**big, **small}
    return (loss, grad_x.reshape(x.shape),
            *[allres[n][0] for n in order], *[allres[n][1] for n in order],
            *[allres[n][2] for n in order], *[allres[n][3] for n in order])
```

```python
import functools
import math

import numpy as np
import jax
import jax.numpy as jnp
from jax import lax
from jax.experimental import pallas as pl
from jax.experimental.pallas import tpu as pltpu

F32 = jnp.float32
BF16 = jnp.bfloat16

HEAD_DIM = 64
HEADS_PER_GROUP = 8
DILATED_GROUPS = ((128, 1), (512, 4), (2048, 16))
N_GROUPS = 3
ATTN_OUT = HEADS_PER_GROUP * HEAD_DIM
ATTN_IN = 3 * N_GROUPS * ATTN_OUT
GROUP_QKV = 3 * ATTN_OUT
ATTN_BLK = 128
CONV_A_WIDTH = 31
SC_CONV_WIDTH = 3
NUM_BUCKETS = 32
REL_MAX_DISTANCE = 2048
RMS_EPS = 1e-6
LN_EPS = 1e-5
NEG_INF = -1e30
ADAM_LR = 0.001
ADAM_B1 = 0.9
ADAM_B2 = 0.999
ADAM_EPS = 1e-08
ADAM_WD = 0.01
ADAM_STEP = 10

N_DEV = 8
HALO = 32
LANES = 128
VMEM_LIMIT = 56 * 1024 * 1024
MESH = pl.DeviceIdType.MESH
ANY = pl.BlockSpec(memory_space=pl.ANY)
VMEM_SPEC = pl.BlockSpec(memory_space=pltpu.VMEM)


def _tile(n, prefs):
    for t in prefs:
        if n % t == 0:
            return t
    return n


def _cparams(*sem):
    return pltpu.CompilerParams(dimension_semantics=sem, vmem_limit_bytes=VMEM_LIMIT)


def _relu_sq(z):
    return jnp.square(jnp.maximum(z, 0))


def _dot_nt(a, b):
    return lax.dot_general(a, b, (((1,), (1,)), ((), ())), preferred_element_type=F32)


def _dot_tn(a, b):
    return lax.dot_general(a, b, (((0,), (0,)), ((), ())), preferred_element_type=F32)


def _mm_nn(a, b, *, out_dtype, name, residual=None, a_fn=None):
    M, K = a.shape
    _, N = b.shape
    tm = _tile(M, (2048, 1024, 512, 256))
    tn = _tile(N, (512, 384, 256, 128))
    tk = _tile(K, (1024, 512, 256, 128))
    nk = K // tk
    has_res = residual is not None

    def body(*refs):
        if has_res:
            a_ref, b_ref, r_ref, o_ref = refs[:4]
        else:
            a_ref, b_ref, o_ref = refs[:3]
        av = a_ref[...]
        if a_fn is not None:
            av = a_fn(av)
        part = jnp.dot(av, b_ref[...], preferred_element_type=F32)

        def finish(acc):
            if has_res:
                acc = acc + r_ref[...]
            o_ref[...] = acc.astype(o_ref.dtype)

        if nk == 1:
            finish(part)
        else:
            acc_ref = refs[-1]
            k = pl.program_id(2)

            @pl.when(k == 0)
            def _():
                acc_ref[...] = part

            @pl.when((k > 0) & (k < nk - 1))
            def _():
                acc_ref[...] += part

            @pl.when(k == nk - 1)
            def _():
                finish(acc_ref[...] + part)

    in_specs = [pl.BlockSpec((tm, tk), lambda i, j, k: (i, k)),
                pl.BlockSpec((tk, tn), lambda i, j, k: (k, j))]
    args = [a, b]
    if has_res:
        in_specs.append(pl.BlockSpec((tm, tn), lambda i, j, k: (i, j)))
        args.append(residual)
    return pl.pallas_call(
        body, name=name,
        out_shape=jax.ShapeDtypeStruct((M, N), out_dtype),
        grid=(M // tm, N // tn, nk),
        in_specs=in_specs,
        out_specs=pl.BlockSpec((tm, tn), lambda i, j, k: (i, j)),
        scratch_shapes=[pltpu.VMEM((tm, tn), F32)] if nk > 1 else [],
        compiler_params=_cparams("parallel", "parallel", "arbitrary"),
    )(*args)


def _mm_nt(pairs, *, out_dtype, name, epilogue=None, extra=None, dep=None):
    M = pairs[0][0].shape[0]
    Ko = pairs[0][1].shape[0]
    tm = _tile(M, (1024, 512, 256))
    to = _tile(Ko, (1024, 512, 256, 128))
    tks = [_tile(p[0].shape[1], (1024, 768, 512, 256, 128)) for p in pairs]
    steps = [p[0].shape[1] // tk for p, tk in zip(pairs, tks)]
    offs = [sum(steps[:i]) for i in range(len(pairs))]
    nk = sum(steps)
    npair = len(pairs)
    has_extra = extra is not None

    def body(*refs):
        ab = refs[:2 * npair]
        pos = 2 * npair
        e_ref = None
        if has_extra:
            e_ref = refs[pos]
            pos += 1
        if dep is not None:
            pos += 1
        o_ref = refs[pos]
        acc_ref = refs[pos + 1]
        k = pl.program_id(2)

        @pl.when(k == 0)
        def _():
            acc_ref[...] = jnp.zeros_like(acc_ref)

        for p in range(npair):
            @pl.when((k >= offs[p]) & (k < offs[p] + steps[p]))
            def _(p=p):
                acc_ref[...] += _dot_nt(ab[2 * p][...], ab[2 * p + 1][...])

        @pl.when(k == nk - 1)
        def _():
            acc = acc_ref[...]
            if epilogue is not None:
                acc = epilogue(acc, e_ref[...] if has_extra else None)
            o_ref[...] = acc.astype(o_ref.dtype)

    in_specs, args = [], []
    for p, (a, b) in enumerate(pairs):
        def kidx(k, p=p):
            return jnp.clip(k - offs[p], 0, steps[p] - 1)
        in_specs.append(pl.BlockSpec((tm, tks[p]), lambda i, j, k, kidx=kidx: (i, kidx(k))))
        in_specs.append(pl.BlockSpec((to, tks[p]), lambda i, j, k, kidx=kidx: (j, kidx(k))))
        args += [a, b]
    if has_extra:
        in_specs.append(pl.BlockSpec((tm, to), lambda i, j, k: (i, j)))
        args.append(extra)
    if dep is not None:
        in_specs.append(ANY)
        args.append(dep)
    return pl.pallas_call(
        body, name=name,
        out_shape=jax.ShapeDtypeStruct((M, Ko), out_dtype),
        grid=(M // tm, Ko // to, nk),
        in_specs=in_specs,
        out_specs=pl.BlockSpec((tm, to), lambda i, j, k: (i, j)),
        scratch_shapes=[pltpu.VMEM((tm, to), F32)],
        compiler_params=_cparams("parallel", "parallel", "arbitrary"),
    )(*args)


def _mm_tn(a, b, *, name, a_fn=None):
    M, K = a.shape
    _, N = b.shape
    tm = _tile(M, (2048, 1024, 512, 256))
    tk = _tile(K, (1024, 768, 512, 384, 256, 128))
    tn = _tile(N, (1024, 768, 512, 384, 256, 128))
    nm = M // tm

    def body(a_ref, b_ref, o_ref, acc_ref):
        m = pl.program_id(2)
        av = a_ref[...]
        if a_fn is not None:
            av = a_fn(av)
        part = _dot_tn(av, b_ref[...])
        if nm == 1:
            o_ref[...] = part.astype(o_ref.dtype)
            return

        @pl.when(m == 0)
        def _():
            acc_ref[...] = part

        @pl.when((m > 0) & (m < nm - 1))
        def _():
            acc_ref[...] += part

        @pl.when(m == nm - 1)
        def _():
            o_ref[...] = (acc_ref[...] + part).astype(o_ref.dtype)

    return pl.pallas_call(
        body, name=name,
        out_shape=jax.ShapeDtypeStruct((K, N), BF16),
        grid=(K // tk, N // tn, nm),
        in_specs=[pl.BlockSpec((tm, tk), lambda i, j, m: (m, i)),
                  pl.BlockSpec((tm, tn), lambda i, j, m: (m, j))],
        out_specs=pl.BlockSpec((tk, tn), lambda i, j, m: (i, j)),
        scratch_shapes=[pltpu.VMEM((tk, tn), F32)],
        compiler_params=_cparams("parallel", "parallel", "arbitrary"),
    )(a, b)


def _rmsnorm_fwd(h, g, *, name, dep=None):
    S, D = h.shape
    tm = _tile(S, (512, 256))

    def body(h_ref, g_ref, *rest):
        o_ref = rest[-1]
        x = h_ref[...]
        r = lax.rsqrt(jnp.mean(x * x, axis=-1, keepdims=True) + RMS_EPS)
        o_ref[...] = (x * r * g_ref[...]).astype(o_ref.dtype)

    return pl.pallas_call(
        body, name=name,
        out_shape=jax.ShapeDtypeStruct((S, D), BF16),
        grid=(S // tm,),
        in_specs=[pl.BlockSpec((tm, D), lambda i: (i, 0)), pl.BlockSpec((1, D), lambda i: (0, 0))]
        + ([ANY] if dep is not None else []),
        out_specs=pl.BlockSpec((tm, D), lambda i: (i, 0)),
        compiler_params=_cparams("parallel"),
    )(h, g, *([dep] if dep is not None else []))


def _rms_bwd_rows(x, g, dy):
    r = lax.rsqrt(jnp.mean(x * x, axis=-1, keepdims=True) + RMS_EPS)
    xh = x * r
    gy = dy * g
    dx = r * (gy - xh * jnp.mean(xh * gy, axis=-1, keepdims=True))
    return dx, dy * xh


def _rms_bwd(x, g, dn, dres, *, name):
    S, D = x.shape
    tm = _tile(S, (256,))

    def body(x_ref, g_ref, dn_ref, dr_ref, dx_ref, dxb_ref, dg_ref):
        i = pl.program_id(0)
        dx, dgx = _rms_bwd_rows(x_ref[...], g_ref[...], dn_ref[...])
        tot = dr_ref[...] + dx
        dx_ref[...] = tot
        dxb_ref[...] = tot.astype(BF16)

        @pl.when(i == 0)
        def _():
            dg_ref[...] = jnp.zeros_like(dg_ref)

        dg_ref[0:1, :] += jnp.sum(dgx, axis=0, keepdims=True)

    row = pl.BlockSpec((tm, D), lambda i: (i, 0))
    return pl.pallas_call(
        body, name=name,
        out_shape=(jax.ShapeDtypeStruct((S, D), F32), jax.ShapeDtypeStruct((S, D), BF16),
                   jax.ShapeDtypeStruct((8, D), F32)),
        grid=(S // tm,),
        in_specs=[row, pl.BlockSpec((1, D), lambda i: (0, 0)), row, row],
        out_specs=(row, row, pl.BlockSpec((8, D), lambda i: (0, 0))),
        compiler_params=_cparams("arbitrary"),
    )(x, g, dn, dres)


def _mm_nt_rms_bwd(a, b, x, g, dres, *, name, dep=None):
    M, N = a.shape
    D = b.shape[0]
    tm = _tile(M, (1024, 512, 256))
    tk = _tile(N, (1024, 512, 256, 128))
    nk = N // tk

    def body(a_ref, b_ref, x_ref, g_ref, dr_ref, *rest):
        dx_ref, dxb_ref, dg_ref, acc_ref = rest[-4:]
        i = pl.program_id(0)
        k = pl.program_id(1)
        part = _dot_nt(a_ref[...], b_ref[...])

        @pl.when((i == 0) & (k == 0))
        def _():
            dg_ref[...] = jnp.zeros_like(dg_ref)

        @pl.when(k == 0)
        def _():
            acc_ref[...] = part

        @pl.when((k > 0) & (k < nk - 1))
        def _():
            acc_ref[...] += part

        @pl.when(k == nk - 1)
        def _():
            dn = part if nk == 1 else acc_ref[...] + part
            dx, dgx = _rms_bwd_rows(x_ref[...], g_ref[...], dn)
            tot = dr_ref[...] + dx
            dx_ref[...] = tot
            dxb_ref[...] = tot.astype(BF16)
            dg_ref[0:1, :] += jnp.sum(dgx, axis=0, keepdims=True)

    row = pl.BlockSpec((tm, D), lambda i, k: (i, 0))
    in_specs = [pl.BlockSpec((tm, tk), lambda i, k: (i, k)), pl.BlockSpec((D, tk), lambda i, k: (0, k)),
                row, pl.BlockSpec((1, D), lambda i, k: (0, 0)), row]
    args = [a, b, x, g, dres]
    if dep is not None:
        in_specs.append(ANY)
        args.append(dep)
    return pl.pallas_call(
        body, name=name,
        out_shape=(jax.ShapeDtypeStruct((M, D), F32), jax.ShapeDtypeStruct((M, D), BF16),
                   jax.ShapeDtypeStruct((8, D), F32)),
        grid=(M // tm, nk),
        in_specs=in_specs,
        out_specs=(row, row, pl.BlockSpec((8, D), lambda i, k: (0, 0))),
        scratch_shapes=[pltpu.VMEM((tm, D), F32)],
        compiler_params=_cparams("arbitrary", "arbitrary"),
    )(*args)


def _loss_bwd(h, target, g, *, name):
    S, D = h.shape
    tm = _tile(S, (256,))

    def body(h_ref, t_ref, g_ref, dx_ref, dxb_ref, acc_ref):
        i = pl.program_id(0)
        x = h_ref[...]
        gv = g_ref[...]
        r = lax.rsqrt(jnp.mean(x * x, axis=-1, keepdims=True) + RMS_EPS)
        err = x * r * gv - t_ref[...]
        dx, dgx = _rms_bwd_rows(x, gv, err * (1.0 / D))
        dx_ref[...] = dx
        dxb_ref[...] = dx.astype(BF16)

        @pl.when(i == 0)
        def _():
            acc_ref[...] = jnp.zeros_like(acc_ref)

        acc_ref[0:1, :] += jnp.sum(dgx, axis=0, keepdims=True)
        acc_ref[1:2, :] += jnp.sum(err * err, axis=0, keepdims=True)

    row = pl.BlockSpec((tm, D), lambda i: (i, 0))
    return pl.pallas_call(
        body, name=name,
        out_shape=(jax.ShapeDtypeStruct((S, D), F32), jax.ShapeDtypeStruct((S, D), BF16),
                   jax.ShapeDtypeStruct((8, D), F32)),
        grid=(S // tm,),
        in_specs=[row, row, pl.BlockSpec((1, D), lambda i: (0, 0))],
        out_specs=(row, row, pl.BlockSpec((8, D), lambda i: (0, 0))),
        compiler_params=_cparams("arbitrary"),
    )(h, target, g)


SUBLANES = 8
CONV_ROWS = 64


def _build_shifted(ext_ref, rot_ref, ts):
    rows = ts + HALO - SUBLANES
    for j in range(1, SUBLANES):
        rot_ref[j, 0:rows, :] = ext_ref[j:j + rows, :]


def _shifted(ext_ref, rot_ref, off, r0, nrows, cs):
    q, j = divmod(off, SUBLANES)
    start = SUBLANES * q + r0
    if j == 0:
        return ext_ref[start:start + nrows, cs]
    return rot_ref[j, start:start + nrows, cs]


def _conv_fwd(uc, conv_w, conv_b, ln_g, ln_b, *, name):
    S, C2 = uc.shape
    C = C2 // 2
    ts = _tile(S, (256,))
    per = ts // HALO

    def body(cur_ref, halo_ref, w_ref, b_ref, g_ref, beta_ref, ya_ref, h_ref, ct_ref, ext_ref, rot_ref):
        i = pl.program_id(0)
        hh = halo_ref[:, 0:C] * jax.nn.sigmoid(halo_ref[:, C:C2])
        ext_ref[0:HALO, :] = jnp.where(i == 0, 0.0, hh)
        hc = cur_ref[:, 0:C] * jax.nn.sigmoid(cur_ref[:, C:C2])
        ext_ref[HALO:HALO + ts, :] = hc
        h_ref[...] = hc
        _build_shifted(ext_ref, rot_ref, ts)
        for c0 in range(0, C, LANES):
            cs = slice(c0, c0 + LANES)
            for r0 in range(0, ts, CONV_ROWS):
                acc = jnp.zeros((CONV_ROWS, LANES), F32)
                for k in range(CONV_A_WIDTH):
                    acc = acc + w_ref[k:k + 1, cs] * _shifted(ext_ref, rot_ref, k + 2, r0, CONV_ROWS, cs)
                ct_ref[r0:r0 + CONV_ROWS, cs] = acc + b_ref[:, cs]
        ct = ct_ref[...]
        mu = jnp.mean(ct, axis=-1, keepdims=True)
        xc = ct - mu
        var = jnp.mean(xc * xc, axis=-1, keepdims=True)
        l = xc * lax.rsqrt(var + LN_EPS) * g_ref[...] + beta_ref[...]
        ya_ref[...] = (l * jax.nn.sigmoid(l)).astype(ya_ref.dtype)

    vec = pl.BlockSpec((1, C), lambda i: (0, 0))
    row = pl.BlockSpec((ts, C), lambda i: (i, 0))
    return pl.pallas_call(
        body, name=name,
        out_shape=(jax.ShapeDtypeStruct((S, C), BF16), jax.ShapeDtypeStruct((S, C), F32),
                   jax.ShapeDtypeStruct((S, C), F32)),
        grid=(S // ts,),
        in_specs=[pl.BlockSpec((ts, C2), lambda i: (i, 0)),
                  pl.BlockSpec((HALO, C2), lambda i: (jnp.maximum(i * per - 1, 0), 0)),
                  pl.BlockSpec((CONV_A_WIDTH, C), lambda i: (0, 0)), vec, vec, vec],
        out_specs=(row, row, row),
        scratch_shapes=[pltpu.VMEM((HALO + ts, C), F32), pltpu.VMEM((8, HALO + ts, C), F32)],
        compiler_params=_cparams("parallel"),
    )(uc, uc, conv_w, conv_b, ln_g, ln_b)


CONV_ACC_ROWS = 40


def _conv_bwd_ln(ct, dcat, hglu, ln_g, ln_b, *, name):
    S, C = ct.shape
    CW = dcat.shape[1]
    ts = _tile(S, (256,))
    per = ts // HALO

    def body(ct_ref, dcat_ref, hc_ref, hh_ref, g_ref, beta_ref, dc_ref, acc_ref, ext_ref, rot_ref):
        i = pl.program_id(0)
        ct = ct_ref[...]
        gv = g_ref[...]
        mu = jnp.mean(ct, axis=-1, keepdims=True)
        xc = ct - mu
        rstd = lax.rsqrt(jnp.mean(xc * xc, axis=-1, keepdims=True) + LN_EPS)
        xh = xc * rstd
        l = xh * gv + beta_ref[...]
        sg = jax.nn.sigmoid(l)
        dl = dcat_ref[:, 0:C] * (sg * (1.0 + l * (1.0 - sg)))
        dxh = dl * gv
        dc = rstd * (dxh - jnp.mean(dxh, axis=-1, keepdims=True)
                     - xh * jnp.mean(dxh * xh, axis=-1, keepdims=True))
        dc_ref[...] = dc

        @pl.when(i == 0)
        def _():
            acc_ref[...] = jnp.zeros_like(acc_ref)

        acc_ref[32:33, :] += jnp.sum(dc, axis=0, keepdims=True)
        acc_ref[33:34, :] += jnp.sum(dl * xh, axis=0, keepdims=True)
        acc_ref[34:35, :] += jnp.sum(dl, axis=0, keepdims=True)
        ext_ref[0:HALO, :] = jnp.where(i == 0, 0.0, hh_ref[...])
        ext_ref[HALO:HALO + ts, :] = hc_ref[...]
        _build_shifted(ext_ref, rot_ref, ts)
        for c0 in range(0, C, LANES):
            cs = slice(c0, c0 + LANES)
            dcc = dc_ref[:, cs]
            for k in range(CONV_A_WIDTH):
                acc_ref[k:k + 1, cs] += jnp.sum(dcc * _shifted(ext_ref, rot_ref, k + 2, 0, ts, cs),
                                                axis=0, keepdims=True)

    vec = pl.BlockSpec((1, C), lambda i: (0, 0))
    row = pl.BlockSpec((ts, C), lambda i: (i, 0))
    return pl.pallas_call(
        body, name=name,
        out_shape=(jax.ShapeDtypeStruct((S, C), F32), jax.ShapeDtypeStruct((CONV_ACC_ROWS, C), F32)),
        grid=(S // ts,),
        in_specs=[row, pl.BlockSpec((ts, CW), lambda i: (i, 0)), row,
                  pl.BlockSpec((HALO, C), lambda i: (jnp.maximum(i * per - 1, 0), 0)), vec, vec],
        out_specs=(row, pl.BlockSpec((CONV_ACC_ROWS, C), lambda i: (0, 0))),
        scratch_shapes=[pltpu.VMEM((HALO + ts, C), F32), pltpu.VMEM((8, HALO + ts, C), F32)],
        compiler_params=_cparams("arbitrary"),
    )(ct, dcat, hglu, hglu, ln_g, ln_b)


def _conv_bwd_in(dc, uc, conv_w, *, name):
    S, C = dc.shape
    C2 = 2 * C
    ts = _tile(S, (256,))
    per = ts // HALO
    nt = S // ts

    def body(dc_ref, dn_ref, uc_ref, w_ref, du_ref, ext_ref, rot_ref):
        i = pl.program_id(0)
        ext_ref[0:ts, :] = dc_ref[...]
        ext_ref[ts:ts + HALO, :] = jnp.where(i == nt - 1, 0.0, dn_ref[...])
        _build_shifted(ext_ref, rot_ref, ts)
        for c0 in range(0, C, LANES):
            cs = slice(c0, c0 + LANES)
            gs = slice(C + c0, C + c0 + LANES)
            for r0 in range(0, ts, CONV_ROWS):
                rs = slice(r0, r0 + CONV_ROWS)
                acc = jnp.zeros((CONV_ROWS, LANES), F32)
                for k in range(CONV_A_WIDTH):
                    acc = acc + w_ref[k:k + 1, cs] * _shifted(ext_ref, rot_ref, 30 - k, r0, CONV_ROWS, cs)
                sg = jax.nn.sigmoid(uc_ref[rs, gs])
                du_ref[rs, cs] = (acc * sg).astype(du_ref.dtype)
                du_ref[rs, gs] = (acc * uc_ref[rs, cs] * sg * (1.0 - sg)).astype(du_ref.dtype)

    return pl.pallas_call(
        body, name=name,
        out_shape=jax.ShapeDtypeStruct((S, C2), BF16),
        grid=(nt,),
        in_specs=[pl.BlockSpec((ts, C), lambda i: (i, 0)),
                  pl.BlockSpec((HALO, C), lambda i: (jnp.minimum((i + 1) * per, S // HALO - 1), 0)),
                  pl.BlockSpec((ts, C2), lambda i: (i, 0)),
                  pl.BlockSpec((CONV_A_WIDTH, C), lambda i: (0, 0))],
        out_specs=pl.BlockSpec((ts, C2), lambda i: (i, 0)),
        scratch_shapes=[pltpu.VMEM((ts + HALO, C), F32), pltpu.VMEM((8, ts + HALO, C), F32)],
        compiler_params=_cparams("parallel"),
    )(dc, dc, uc, conv_w)


def _sc_fwd(u3, conv_w, *, name):
    S, W3 = u3.shape
    W = W3 // 3
    ts = _tile(S, (256,))
    per = ts // HALO

    def body(cur_ref, halo_ref, w_ref, y_ref, ext_ref):
        i = pl.program_id(0)
        cvh = halo_ref[:, W:2 * W].astype(F32) * halo_ref[:, 2 * W:W3].astype(F32)
        ext_ref[0:HALO, :] = jnp.where(i == 0, 0.0, cvh)
        ext_ref[HALO:HALO + ts, :] = cur_ref[:, W:2 * W].astype(F32) * cur_ref[:, 2 * W:W3].astype(F32)
        k = (w_ref[0:1, :] * ext_ref[HALO - 2:HALO - 2 + ts, :]
             + w_ref[1:2, :] * ext_ref[HALO - 1:HALO - 1 + ts, :]
             + w_ref[2:3, :] * ext_ref[HALO:HALO + ts, :])
        y_ref[...] = (cur_ref[:, 0:W].astype(F32) * k).astype(y_ref.dtype)

    return pl.pallas_call(
        body, name=name,
        out_shape=jax.ShapeDtypeStruct((S, W), BF16),
        grid=(S // ts,),
        in_specs=[pl.BlockSpec((ts, W3), lambda i: (i, 0)),
                  pl.BlockSpec((HALO, W3), lambda i: (jnp.maximum(i * per - 1, 0), 0)),
                  pl.BlockSpec((SC_CONV_WIDTH, W), lambda i: (0, 0))],
        out_specs=pl.BlockSpec((ts, W), lambda i: (i, 0)),
        scratch_shapes=[pltpu.VMEM((HALO + ts, W), F32)],
        compiler_params=_cparams("parallel"),
    )(u3, u3, conv_w)


def _sc_bwd(u3, dy, conv_w, *, name):
    S, W3 = u3.shape
    W = W3 // 3
    ts = _tile(S, (256,))
    per = ts // HALO
    nt = S // ts

    def body(cur_ref, prev_ref, next_ref, dy_ref, dyn_ref, w_ref, du_ref, dw_ref, cv_ext, dk_ext):
        i = pl.program_id(0)
        cvh = prev_ref[:, W:2 * W].astype(F32) * prev_ref[:, 2 * W:W3].astype(F32)
        cv_ext[0:HALO, :] = jnp.where(i == 0, 0.0, cvh)
        c = cur_ref[:, W:2 * W].astype(F32)
        v = cur_ref[:, 2 * W:W3].astype(F32)
        b = cur_ref[:, 0:W].astype(F32)
        cv_ext[HALO:HALO + ts, :] = c * v
        dy_cur = dy_ref[...]
        dk = dy_cur * b
        dk_ext[0:ts, :] = dk
        dk_ext[ts:ts + HALO, :] = jnp.where(i == nt - 1, 0.0, dyn_ref[...] * next_ref[:, 0:W].astype(F32))
        w0, w1, w2 = w_ref[0:1, :], w_ref[1:2, :], w_ref[2:3, :]
        cv2 = cv_ext[HALO - 2:HALO - 2 + ts, :]
        cv1 = cv_ext[HALO - 1:HALO - 1 + ts, :]
        cv0 = cv_ext[HALO:HALO + ts, :]
        kconv = w0 * cv2 + w1 * cv1 + w2 * cv0
        dcv = w2 * dk + w1 * dk_ext[1:1 + ts, :] + w0 * dk_ext[2:2 + ts, :]
        du_ref[:, 0:W] = (dy_cur * kconv).astype(du_ref.dtype)
        du_ref[:, W:2 * W] = (dcv * v).astype(du_ref.dtype)
        du_ref[:, 2 * W:W3] = (dcv * c).astype(du_ref.dtype)

        @pl.when(i == 0)
        def _():
            dw_ref[...] = jnp.zeros_like(dw_ref)

        dw_ref[0:1, :] += jnp.sum(dk * cv2, axis=0, keepdims=True)
        dw_ref[1:2, :] += jnp.sum(dk * cv1, axis=0, keepdims=True)
        dw_ref[2:3, :] += jnp.sum(dk * cv0, axis=0, keepdims=True)

    nxt = lambda i: (jnp.minimum((i + 1) * per, S // HALO - 1), 0)
    return pl.pallas_call(
        body, name=name,
        out_shape=(jax.ShapeDtypeStruct((S, W3), BF16), jax.ShapeDtypeStruct((8, W), F32)),
        grid=(nt,),
        in_specs=[pl.BlockSpec((ts, W3), lambda i: (i, 0)),
                  pl.BlockSpec((HALO, W3), lambda i: (jnp.maximum(i * per - 1, 0), 0)),
                  pl.BlockSpec((HALO, W3), nxt),
                  pl.BlockSpec((ts, W), lambda i: (i, 0)),
                  pl.BlockSpec((HALO, W), nxt),
                  pl.BlockSpec((SC_CONV_WIDTH, W), lambda i: (0, 0))],
        out_specs=(pl.BlockSpec((ts, W3), lambda i: (i, 0)), pl.BlockSpec((8, W), lambda i: (0, 0))),
        scratch_shapes=[pltpu.VMEM((HALO + ts, W), F32), pltpu.VMEM((ts + HALO, W), F32)],
        compiler_params=_cparams("arbitrary"),
    )(u3, u3, u3, dy, dy, conv_w)


def _t5_causal_bucket(n):
    max_exact = NUM_BUCKETS // 2
    nf = jnp.maximum(n, 1).astype(F32)
    large = max_exact + (jnp.log(nf / max_exact) / math.log(REL_MAX_DISTANCE / max_exact)
                         * (NUM_BUCKETS - max_exact)).astype(jnp.int32)
    return jnp.where(n < max_exact, n, jnp.minimum(large, NUM_BUCKETS - 1))


def _bucket_tables():
    steps = ATTN_BLK
    m = jnp.arange(steps)[:, None] + steps - jnp.arange(2 * steps)[None, :]
    return jnp.stack([_t5_causal_bucket(jnp.clip(m, 0, steps) * dil).astype(F32) for _, dil in DILATED_GROUPS])


def _bias_tables(rel_bias, buckets, *, name):
    steps = ATTN_BLK

    def body(tab_ref, bk_ref, o_ref):
        g = pl.program_id(0)
        bk = bk_ref[0]
        a_idx = lax.broadcasted_iota(jnp.int32, (steps, 2 * steps), 0)
        c_idx = lax.broadcasted_iota(jnp.int32, (steps, 2 * steps), 1)
        m = a_idx + steps - c_idx
        band = (m >= 0) & (m <= steps)
        band_first = band & (c_idx >= steps)
        for h in range(HEADS_PER_GROUP):
            bias = jnp.zeros((steps, 2 * steps), F32)
            for b in range(NUM_BUCKETS):
                bias = jnp.where(bk == float(b), tab_ref[b, g * HEADS_PER_GROUP + h], bias)
            o_ref[0, 0, h] = jnp.where(band_first, bias, NEG_INF)
            o_ref[0, 1, h] = jnp.where(band, bias, NEG_INF)

    return pl.pallas_call(
        body, name=name,
        out_shape=jax.ShapeDtypeStruct((N_GROUPS, 2, HEADS_PER_GROUP, steps, 2 * steps), F32),
        grid=(N_GROUPS,),
        in_specs=[pl.BlockSpec(memory_space=pltpu.SMEM),
                  pl.BlockSpec((1, steps, 2 * steps), lambda g: (g, 0, 0))],
        out_specs=pl.BlockSpec((1, 2, HEADS_PER_GROUP, steps, 2 * steps), lambda g: (g, 0, 0, 0, 0)),
        compiler_params=_cparams("parallel"),
    )(rel_bias, buckets)


def _lane_is_low():
    return lax.broadcasted_iota(jnp.int32, (1, LANES), 1) < HEAD_DIM


def _qkv_specs(g):
    nqb = ATTN_IN // ATTN_OUT

    def spec(t, prev):
        def idx(r, n):
            row = jnp.maximum(n - 1, 0) if prev else n
            return (row, r * nqb + t * N_GROUPS + g)
        return pl.BlockSpec((ATTN_BLK, ATTN_OUT), idx)

    return [spec(0, False), spec(1, False), spec(1, True), spec(2, False), spec(2, True)]


def _attn_fwd(uq, bias, g, d, *, name):
    S = uq.shape[0]
    rows = S // d
    uv = uq.reshape(rows, d * ATTN_IN)

    def body(q_ref, kc_ref, kp_ref, vc_ref, vp_ref, bias_ref, o_ref, l_ref):
        n = pl.program_id(1)
        sel = jnp.minimum(n, 1)
        low = _lane_is_low()
        for hp in range(HEADS_PER_GROUP // 2):
            sl = slice(hp * LANES, (hp + 1) * LANES)
            q2 = q_ref[:, sl]
            k2 = jnp.concatenate([kp_ref[:, sl], kc_ref[:, sl]], axis=0)
            v2 = jnp.concatenate([vp_ref[:, sl], vc_ref[:, sl]], axis=0)
            outs, lses = [], []
            for hh in range(2):
                msk = low if hh == 0 else jnp.logical_not(low)
                qm = jnp.where(msk, q2, jnp.zeros_like(q2))
                logits = _dot_nt(qm, k2) * (HEAD_DIM ** -0.5) + bias_ref[sel, 2 * hp + hh]
                mx = jnp.max(logits, axis=-1, keepdims=True)
                p = jnp.exp(logits - mx)
                den = jnp.sum(p, axis=-1, keepdims=True)
                pv = jnp.dot(p.astype(BF16), v2, preferred_element_type=F32)
                outs.append(pv / den)
                lses.append(jnp.broadcast_to(mx + jnp.log(den), (ATTN_BLK, LANES)))
            o_ref[:, sl] = jnp.where(low, outs[0], outs[1])
            l_ref[:, sl] = jnp.where(low, lses[0], lses[1])

    out_spec = pl.BlockSpec((ATTN_BLK, ATTN_OUT), lambda r, n: (n, r))
    o, l = pl.pallas_call(
        body, name=name,
        out_shape=(jax.ShapeDtypeStruct((rows, d * ATTN_OUT), F32),) * 2,
        grid=(d, rows // ATTN_BLK),
        in_specs=_qkv_specs(g) + [pl.BlockSpec((None, 2, HEADS_PER_GROUP, ATTN_BLK, 2 * ATTN_BLK),
                                               lambda r, n: (g, 0, 0, 0, 0))],
        out_specs=(out_spec, out_spec),
        compiler_params=_cparams("parallel", "parallel"),
    )(uv, uv, uv, uv, uv, bias)
    return o.reshape(S, ATTN_OUT), l.reshape(S, ATTN_OUT)


def _attn_merge(outs, lses, ya, *, name):
    S, C = ya.shape
    tm = _tile(S, (256,))

    def body(o0, o1, o2, l0, l1, l2, ya_ref, cat_ref, out_ref, lse_ref):
        a0, a1, a2 = l0[...], l1[...], l2[...]
        m = jnp.maximum(jnp.maximum(a0, a1), a2)
        e0, e1, e2 = jnp.exp(a0 - m), jnp.exp(a1 - m), jnp.exp(a2 - m)
        den = e0 + e1 + e2
        out = (e0 * o0[...] + e1 * o1[...] + e2 * o2[...]) / den
        out_ref[...] = out
        lse_ref[...] = m + jnp.log(den)
        cat_ref[:, 0:C] = ya_ref[...]
        cat_ref[:, C:C + ATTN_OUT] = out.astype(cat_ref.dtype)

    blk = pl.BlockSpec((tm, ATTN_OUT), lambda i: (i, 0))
    return pl.pallas_call(
        body, name=name,
        out_shape=(jax.ShapeDtypeStruct((S, C + ATTN_OUT), BF16), jax.ShapeDtypeStruct((S, ATTN_OUT), F32),
                   jax.ShapeDtypeStruct((S, ATTN_OUT), F32)),
        grid=(S // tm,),
        in_specs=[blk] * 6 + [pl.BlockSpec((tm, C), lambda i: (i, 0))],
        out_specs=(pl.BlockSpec((tm, C + ATTN_OUT), lambda i: (i, 0)), blk, blk),
        compiler_params=_cparams("parallel"),
    )(*outs, *lses, ya)


def _attn_prep(dcat, outf, *, name):
    S, CW = dcat.shape
    C = CW - ATTN_OUT
    tm = _tile(S, (256,))
    ones = np.kron(np.eye(HEADS_PER_GROUP, dtype=np.float32), np.ones((HEAD_DIM, HEAD_DIM), np.float32))

    def body(dcat_ref, out_ref, ones_ref, dyb_ref, dl_ref):
        dyb = dcat_ref[:, C:CW]
        dyb_ref[...] = dyb.astype(BF16)
        prod = dyb * out_ref[...]
        ov = ones_ref[...]
        hi, mid, lo = _split_bf16(prod)
        dl_ref[...] = (jnp.dot(hi, ov, preferred_element_type=F32)
                       + jnp.dot(mid, ov, preferred_element_type=F32)
                       + jnp.dot(lo, ov, preferred_element_type=F32))

    blk = pl.BlockSpec((tm, ATTN_OUT), lambda i: (i, 0))
    return pl.pallas_call(
        body, name=name,
        out_shape=(jax.ShapeDtypeStruct((S, ATTN_OUT), BF16), jax.ShapeDtypeStruct((S, ATTN_OUT), F32)),
        grid=(S // tm,),
        in_specs=[pl.BlockSpec((tm, CW), lambda i: (i, 0)), blk,
                  pl.BlockSpec((ATTN_OUT, ATTN_OUT), lambda i: (0, 0))],
        out_specs=(blk, blk),
        compiler_params=_cparams("parallel"),
    )(dcat, outf, jnp.asarray(ones, BF16))


def _attn_bwd(uq, dyb, lse, delta, bias, g, d, *, name):
    S = uq.shape[0]
    rows = S // d
    nb = rows // ATTN_BLK
    uv = uq.reshape(rows, d * ATTN_IN)
    dov = dyb.reshape(rows, d * ATTN_OUT)
    lv = lse.reshape(rows, d * ATTN_OUT)
    dv_ = delta.reshape(rows, d * ATTN_OUT)
    scale = HEAD_DIM ** -0.5

    def body(q_ref, kc_ref, kp_ref, vc_ref, vp_ref, do_ref, l_ref, dl_ref, bias_ref,
             out_ref, db_ref, dq_s, dk_s, dv_s):
        r = pl.program_id(0)
        n = pl.program_id(1)
        low = _lane_is_low()

        @pl.when((r == 0) & (n == 0))
        def _():
            db_ref[...] = jnp.zeros_like(db_ref)

        @pl.when(n == 0)
        def _():
            dq_s[...] = jnp.zeros_like(dq_s)
            dk_s[...] = jnp.zeros_like(dk_s)
            dv_s[...] = jnp.zeros_like(dv_s)

        @pl.when(n < nb)
        def _():
            sel = jnp.minimum(n, 1)
            lane = lax.broadcasted_iota(jnp.int32, (1, LANES), 1)
            for hp in range(HEADS_PER_GROUP // 2):
                sl = slice(hp * LANES, (hp + 1) * LANES)
                q2 = q_ref[:, sl]
                do2 = do_ref[:, sl]
                k2 = jnp.concatenate([kp_ref[:, sl], kc_ref[:, sl]], axis=0)
                v2 = jnp.concatenate([vp_ref[:, sl], vc_ref[:, sl]], axis=0)
                lse2 = l_ref[:, sl]
                dl2 = dl_ref[:, sl]
                dqs, dks, dvs = [], [], []
                for hh in range(2):
                    msk = low if hh == 0 else jnp.logical_not(low)
                    one = lane == hh * HEAD_DIM
                    lse_col = jnp.sum(jnp.where(one, lse2, 0.0), axis=-1, keepdims=True)
                    dl_col = jnp.sum(jnp.where(one, dl2, 0.0), axis=-1, keepdims=True)
                    qm = jnp.where(msk, q2, jnp.zeros_like(q2))
                    dom = jnp.where(msk, do2, jnp.zeros_like(do2))
                    logits = _dot_nt(qm, k2) * scale + bias_ref[sel, 2 * hp + hh]
                    p = jnp.exp(logits - lse_col)
                    dp = _dot_nt(dom, v2)
                    ds = p * (dp - dl_col)
                    db_ref[2 * hp + hh] += ds
                    dsb = ds.astype(BF16)
                    dqs.append(jnp.dot(dsb, k2, preferred_element_type=F32) * scale)
                    dks.append(_dot_tn(dsb, q2) * scale)
                    dvs.append(_dot_tn(p.astype(BF16), do2))
                dq2 = jnp.where(low, dqs[0], dqs[1])
                dk2 = jnp.where(low, dks[0], dks[1])
                dv2 = jnp.where(low, dvs[0], dvs[1])
                out_ref[:, sl] = dq_s[:, sl].astype(out_ref.dtype)
                dq_s[:, sl] = dq2
                ksl = slice(ATTN_OUT + hp * LANES, ATTN_OUT + (hp + 1) * LANES)
                vsl = slice(2 * ATTN_OUT + hp * LANES, 2 * ATTN_OUT + (hp + 1) * LANES)
                out_ref[:, ksl] = (dk_s[:, sl] + dk2[0:ATTN_BLK]).astype(out_ref.dtype)
                dk_s[:, sl] = dk2[ATTN_BLK:2 * ATTN_BLK]
                out_ref[:, vsl] = (dv_s[:, sl] + dv2[0:ATTN_BLK]).astype(out_ref.dtype)
                dv_s[:, sl] = dv2[ATTN_BLK:2 * ATTN_BLK]

        @pl.when(n == nb)
        def _():
            out_ref[:, 0:ATTN_OUT] = dq_s[...].astype(out_ref.dtype)
            out_ref[:, ATTN_OUT:2 * ATTN_OUT] = dk_s[...].astype(out_ref.dtype)
            out_ref[:, 2 * ATTN_OUT:GROUP_QKV] = dv_s[...].astype(out_ref.dtype)

    nqb = ATTN_IN // ATTN_OUT

    def spec(t, prev):
        def idx(r, n):
            nn = jnp.minimum(n, nb - 1)
            row = jnp.maximum(nn - 1, 0) if prev else nn
            return (row, r * nqb + t * N_GROUPS + g)
        return pl.BlockSpec((ATTN_BLK, ATTN_OUT), idx)

    rowblk = pl.BlockSpec((ATTN_BLK, ATTN_OUT), lambda r, n: (jnp.minimum(n, nb - 1), r))
    dqkv, db = pl.pallas_call(
        body, name=name,
        out_shape=(jax.ShapeDtypeStruct((rows, d * GROUP_QKV), BF16),
                   jax.ShapeDtypeStruct((HEADS_PER_GROUP, ATTN_BLK, 2 * ATTN_BLK), F32)),
        grid=(d, nb + 1),
        in_specs=[spec(0, False), spec(1, False), spec(1, True), spec(2, False), spec(2, True),
                  rowblk, rowblk, rowblk,
                  pl.BlockSpec((None, 2, HEADS_PER_GROUP, ATTN_BLK, 2 * ATTN_BLK), lambda r, n: (g, 0, 0, 0, 0))],
        out_specs=(pl.BlockSpec((ATTN_BLK, GROUP_QKV), lambda r, n: (jnp.maximum(n - 1, 0), r)),
                   pl.BlockSpec((HEADS_PER_GROUP, ATTN_BLK, 2 * ATTN_BLK), lambda r, n: (0, 0, 0))),
        scratch_shapes=[pltpu.VMEM((ATTN_BLK, ATTN_OUT), F32)] * 3,
        compiler_params=_cparams("arbitrary", "arbitrary"),
    )(uv, uv, uv, uv, uv, dov, lv, dv_, bias)
    return dqkv.reshape(S, GROUP_QKV), db


def _split_bf16(x):
    hi = x.astype(BF16)
    r1 = x - hi.astype(F32)
    mid = r1.astype(BF16)
    lo = (r1 - mid.astype(F32)).astype(BF16)
    return hi, mid, lo


RELBIAS_CHUNK = 4096


def _relbias_reduce(dbs, buckets, *, name):
    flat = ATTN_BLK * 2 * ATTN_BLK
    dbf = jnp.stack([db.reshape(HEADS_PER_GROUP, flat) for db in dbs])
    bkf = buckets.reshape(N_GROUPS, 1, flat)

    def body(db_ref, bk_ref, o_ref):
        c = pl.program_id(1)
        rows = lax.broadcasted_iota(jnp.int32, (LANES, RELBIAS_CHUNK), 0).astype(F32)
        onehot = jnp.where(rows == bk_ref[0], 1.0, 0.0).astype(BF16)
        hi, mid, lo = _split_bf16(db_ref[0])
        part = _dot_nt(hi, onehot) + _dot_nt(mid, onehot) + _dot_nt(lo, onehot)

        @pl.when(c == 0)
        def _():
            o_ref[0] = part

        @pl.when(c > 0)
        def _():
            o_ref[0] += part

    return pl.pallas_call(
        body, name=name,
        out_shape=jax.ShapeDtypeStruct((N_GROUPS, HEADS_PER_GROUP, LANES), F32),
        grid=(N_GROUPS, flat // RELBIAS_CHUNK),
        in_specs=[pl.BlockSpec((1, HEADS_PER_GROUP, RELBIAS_CHUNK), lambda g, c: (g, 0, c)),
                  pl.BlockSpec((1, 1, RELBIAS_CHUNK), lambda g, c: (g, 0, c))],
        out_specs=pl.BlockSpec((1, HEADS_PER_GROUP, LANES), lambda g, c: (g, 0, 0)),
        compiler_params=_cparams("parallel", "arbitrary"),
    )(dbf, bkf)


def _my_position():
    x, y, c = lax.axis_index("x"), lax.axis_index("y"), lax.axis_index("c")
    return x, y, c


def _linear(pos):
    return 4 * pos[0] + 2 * pos[1] + pos[2]


def _peer(pos, k):
    x, y, c = pos
    return ((1 - x) if k & 4 else x, (1 - y) if k & 2 else y, (1 - c) if k & 1 else c)


HBM_SPEC = pl.BlockSpec(memory_space=pltpu.HBM)
SEM_SPEC = pl.BlockSpec(memory_space=pltpu.SEMAPHORE)
DATAFLOW = pltpu.SideEffectType.DATAFLOW_SIDE_EFFECTING


def _exchange_copies(src, land, sems, send_window, recv_window, with_arrivals):
    send_sems, recv_sems, local_sems = sems
    T = len(src)
    me = _my_position()
    me_lin = _linear(me)
    local = [pltpu.make_async_copy(send_window(t, src[t], me_lin), recv_window(t, land[t], me_lin),
                                   local_sems.at[t]) for t in range(T)]
    sends, arrivals = [], []
    for t in range(T):
        for k in range(1, N_DEV):
            peer = _peer(me, k)
            peer_lin = _linear(peer)
            sem = t * (N_DEV - 1) + k - 1
            sends.append(pltpu.make_async_remote_copy(
                src_ref=send_window(t, src[t], peer_lin), dst_ref=recv_window(t, land[t], me_lin),
                send_sem=send_sems.at[sem], recv_sem=recv_sems.at[sem],
                device_id=peer, device_id_type=MESH))
            if with_arrivals:
                arrivals.append(pltpu.make_async_remote_copy(
                    src_ref=send_window(t, src[t], me_lin), dst_ref=recv_window(t, land[t], peer_lin),
                    send_sem=send_sems.at[sem], recv_sem=recv_sems.at[sem],
                    device_id=peer, device_id_type=MESH))
    return local, sends, arrivals


def _exchange_start(srcs, land_shapes, send_window, recv_window, *, name, dep=None):
    T = len(srcs)
    n_in = 2 * T + (1 if dep is not None else 0)

    def body(*refs):
        src = refs[:T]
        land = refs[T:2 * T]
        sems = refs[n_in:n_in + 3]
        token = refs[-1]
        local, sends, _ = _exchange_copies(src, land, sems, send_window, recv_window, False)
        for cp in local + sends:
            cp.start()
        token[...] = jnp.zeros_like(token)

    lands = [lax.empty(ls.shape, ls.dtype) for ls in land_shapes]
    operands = [pltpu.with_memory_space_constraint(a, pltpu.HBM) for a in list(srcs) + lands]
    outs = pl.pallas_call(
        body, name=name,
        out_shape=(pltpu.SemaphoreType.DMA((T * (N_DEV - 1),)), pltpu.SemaphoreType.DMA((T * (N_DEV - 1),)),
                   pltpu.SemaphoreType.DMA((T,)),
                   *[pltpu.HBM(a.shape, a.dtype) for a in operands],
                   jax.ShapeDtypeStruct((8, LANES), F32)),
        in_specs=[HBM_SPEC] * (2 * T) + ([ANY] if dep is not None else []),
        out_specs=(SEM_SPEC,) * 3 + (HBM_SPEC,) * (2 * T) + (VMEM_SPEC,),
        input_output_aliases={i: 3 + i for i in range(2 * T)},
        compiler_params=pltpu.CompilerParams(has_side_effects=DATAFLOW),
    )(*operands, *([dep] if dep is not None else []))
    return outs[:3], outs[3:3 + T], outs[3 + T:3 + 2 * T], outs[-1]


def _exchange_wait(started, after, send_window, recv_window, *, name):
    sems, srcs, lands, _ = started
    T = len(srcs)

    def body(*refs):
        src = refs[:T]
        land = refs[T:2 * T]
        sem_refs = refs[2 * T:2 * T + 3]
        local, sends, arrivals = _exchange_copies(src, land, sem_refs, send_window, recv_window, True)
        for cp in arrivals:
            cp.wait_recv()
        for cp in sends:
            cp.wait_send()
        for cp in local:
            cp.wait()

    outs = pl.pallas_call(
        body, name=name,
        out_shape=tuple(pltpu.HBM(a.shape, a.dtype) for a in list(srcs) + list(lands)),
        in_specs=[HBM_SPEC] * (2 * T) + [SEM_SPEC] * 3 + [ANY],
        out_specs=(HBM_SPEC,) * (2 * T),
        input_output_aliases={i: i for i in range(2 * T)},
        compiler_params=pltpu.CompilerParams(has_side_effects=DATAFLOW),
    )(*srcs, *lands, *sems, after)
    return outs[T:]


def _shard_window(kind, width):
    def win(ref, lin):
        if kind == "slot":
            return ref.at[lin]
        if kind == "col":
            return ref.at[:, pl.ds(pl.multiple_of(lin * width, LANES), width)]
        if kind == "row":
            return ref.at[pl.ds(pl.multiple_of(lin * width, 8), width), :]
        if kind == "lcol":
            return ref.at[:, :, pl.ds(pl.multiple_of(lin * width, LANES), width)]
        if kind == "lrow":
            return ref.at[:, pl.ds(pl.multiple_of(lin * width, 8), width), :]
        raise ValueError(kind)
    return win


def _shard_windows(kinds, shard_shapes):
    return [_shard_window(k, (ss[-1] if k in ("col", "lcol") else ss[-2])) for k, ss in zip(kinds, shard_shapes)]


def _allgather_start(shards, kinds, full_shapes, *, name, dep=None):
    wins = _shard_windows(kinds, [s.shape for s in shards])
    send_window = lambda t, ref, lin: ref
    recv_window = lambda t, ref, lin: wins[t](ref, lin)
    started = _exchange_start(shards, [jax.ShapeDtypeStruct(fs, s.dtype) for fs, s in zip(full_shapes, shards)],
                              send_window, recv_window, name=name + "_start", dep=dep)
    return started, lambda after: _exchange_wait(started, after, send_window, recv_window, name=name + "_wait")


def _scatter_start(fulls, kinds, shard_shapes, *, name):
    wins = _shard_windows(kinds, shard_shapes)
    send_window = lambda t, ref, lin: wins[t](ref, lin)
    recv_window = lambda t, ref, lin: ref.at[lin]
    started = _exchange_start(
        fulls, [jax.ShapeDtypeStruct((N_DEV,) + tuple(ss), f.dtype) for ss, f in zip(shard_shapes, fulls)],
        send_window, recv_window, name=name + "_start")
    return started, lambda after: _exchange_wait(started, after, send_window, recv_window, name=name + "_wait")


def _small_gather(pack, *, reduce, name):
    R = pack.shape[0]

    def body(p_ref, o_ref, *rest):
        if reduce:
            buf, send_sems, recv_sems = rest
        else:
            buf = o_ref
            send_sems, recv_sems = rest
        me = _my_position()
        me_lin = _linear(me)
        buf[me_lin] = p_ref[...]
        sends = []
        for k in range(1, N_DEV):
            peer = _peer(me, k)
            cp = pltpu.make_async_remote_copy(
                src_ref=p_ref, dst_ref=buf.at[me_lin],
                send_sem=send_sems.at[k - 1], recv_sem=recv_sems.at[k - 1],
                device_id=peer, device_id_type=MESH)
            cp.start()
            sends.append(cp)
        for k in range(1, N_DEV):
            peer = _peer(me, k)
            pltpu.make_async_remote_copy(
                src_ref=p_ref, dst_ref=buf.at[_linear(peer)],
                send_sem=send_sems.at[k - 1], recv_sem=recv_sems.at[k - 1],
                device_id=peer, device_id_type=MESH).wait_recv()
        for cp in sends:
            cp.wait_send()
        if reduce:
            acc = buf[0]
            for s in range(1, N_DEV):
                acc = acc + buf[s]
            o_ref[...] = acc

    scratch = [pltpu.SemaphoreType.DMA((N_DEV - 1,)), pltpu.SemaphoreType.DMA((N_DEV - 1,))]
    if reduce:
        scratch = [pltpu.VMEM((N_DEV, R, LANES), F32)] + scratch
        out_shape = jax.ShapeDtypeStruct((R, LANES), F32)
    else:
        out_shape = jax.ShapeDtypeStruct((N_DEV, R, LANES), F32)
    return pl.pallas_call(
        body, name=name, out_shape=out_shape,
        in_specs=[VMEM_SPEC], out_specs=VMEM_SPEC, scratch_shapes=scratch,
        compiler_params=pltpu.CompilerParams(has_side_effects=True, vmem_limit_bytes=VMEM_LIMIT),
    )(pack)


def _adamw_math(w, g, m, v):
    m = ADAM_B1 * m + (1.0 - ADAM_B1) * g
    v = ADAM_B2 * v + (1.0 - ADAM_B2) * jnp.square(g)
    m_hat = m / (1.0 - ADAM_B1 ** ADAM_STEP)
    v_hat = v / (1.0 - ADAM_B2 ** ADAM_STEP)
    delta = -ADAM_LR * (m_hat / (jnp.sqrt(v_hat) + ADAM_EPS) + ADAM_WD * w)
    return delta, m, v


def _adamw_from_partials(parts, w, m, v, *, name):
    R, C = w.shape
    rl = parts[0].shape[1]
    assert all(p.shape == (N_DEV, rl, C) for p in parts) and rl * len(parts) == R
    tr = _tile(rl, (256, 128, 64, 32, 16))
    per = rl // tr
    L = len(parts)

    def body(*refs):
        p_refs = refs[:L]
        w_ref, m_ref, v_ref, g_ref, d_ref, nm_ref, nv_ref = refs[L:]
        i = pl.program_id(0)
        for l in range(L):
            @pl.when((i >= l * per) & (i < (l + 1) * per))
            def _(l=l):
                p_ref = p_refs[l]
                g = p_ref[0].astype(F32)
                for s in range(1, N_DEV):
                    g = g + p_ref[s].astype(F32)
                d, nm, nv = _adamw_math(w_ref[...], g, m_ref[...], v_ref[...])
                g_ref[...] = g
                d_ref[...] = d
                nm_ref[...] = nm
                nv_ref[...] = nv

    blk = pl.BlockSpec((tr, C), lambda i: (i, 0))
    part_specs = [pl.BlockSpec((N_DEV, tr, C), lambda i, l=l: (0, jnp.clip(i - l * per, 0, per - 1), 0))
                  for l in range(L)]
    return pl.pallas_call(
        body, name=name,
        out_shape=(jax.ShapeDtypeStruct((R, C), F32),) * 4,
        grid=(R // tr,),
        in_specs=part_specs + [blk, blk, blk],
        out_specs=(blk,) * 4,
        compiler_params=_cparams("parallel"),
    )(*parts, w, m, v)


def _adamw_small(g, w, m, v, *, name):
    def body(g_ref, w_ref, m_ref, v_ref, d_ref, nm_ref, nv_ref):
        d, nm, nv = _adamw_math(w_ref[...], g_ref[...], m_ref[...], v_ref[...])
        d_ref[...] = d
        nm_ref[...] = nm
        nv_ref[...] = nv

    return pl.pallas_call(
        body, name=name,
        out_shape=(jax.ShapeDtypeStruct(g.shape, F32),) * 3,
        in_specs=[VMEM_SPEC] * 4, out_specs=(VMEM_SPEC,) * 3,
    )(g, w, m, v)


def _pack_rows(pieces):
    flat = jnp.concatenate([p.reshape(-1) for p in pieces])
    n = flat.shape[0]
    padded = -(-n // (8 * LANES)) * (8 * LANES)
    return jnp.pad(flat, (0, padded - n)).reshape(padded // LANES, LANES)


def _unpack_rows(pack, shapes):
    flat = pack.reshape(-1)
    out, pos = [], 0
    for s in shapes:
        n = int(np.prod(s))
        out.append(flat[pos:pos + n].reshape(s))
        pos += n
    return out


def kernel(x, rel_bias, ab_norm, ab_w_in, ab_conv_w, ab_conv_b, ab_ln_g, ab_ln_b, ab_w_out, sc_norm, sc_w_in, sc_conv_w, sc_w_out, mlp_norm, mlp_w_up, mlp_w_down, final_norm, loss_target, m_rel_bias, m_ab_norm, m_ab_w_in, m_ab_conv_w, m_ab_conv_b, m_ab_ln_g, m_ab_ln_b, m_ab_w_out, m_sc_norm, m_sc_w_in, m_sc_conv_w, m_sc_w_out, m_mlp_norm, m_mlp_w_up, m_mlp_w_down, m_final_norm, v_rel_bias, v_ab_norm, v_ab_w_in, v_ab_conv_w, v_ab_conv_b, v_ab_ln_g, v_ab_ln_b, v_ab_w_out, v_sc_norm, v_sc_w_in, v_sc_conv_w, v_sc_w_out, v_mlp_norm, v_mlp_w_up, v_mlp_w_down, v_final_norm):
    S, D = x.shape[1], x.shape[2]
    CA = ab_conv_b.shape[1]
    C2 = 2 * CA
    AB_IN = C2 + ATTN_IN
    me_lin = _linear(_my_position())
    xs = x.reshape(S, D)
    tgt = loss_target.reshape(S, D)

    cw_sh = ab_conv_w.shape[2]
    scn_sh = sc_norm.shape[1]
    scw_sh = sc_conv_w.shape[2]
    small_sh_shapes = [(CONV_A_WIDTH, cw_sh), (scn_sh,), (SC_CONV_WIDTH, scw_sh)]
    small_params = _small_gather(_pack_rows([ab_conv_w[0], sc_norm[0], sc_conv_w[0]]), reduce=False,
                                 name="allgather_small_params")
    w_in_sh = ab_w_in[0].astype(BF16)
    ag_ab, wait_ab = _allgather_start(
        [w_in_sh, ab_w_out[0].astype(BF16)], ["slot", "row"],
        [(N_DEV,) + w_in_sh.shape, (N_DEV * ab_w_out.shape[1], D)], name="allgather_ab", dep=small_params)
    ag_mlp, wait_mlp = _allgather_start(
        [mlp_w_up.astype(BF16), mlp_w_down.astype(BF16)], ["lcol", "lrow"],
        [(2, D, N_DEV * mlp_w_up.shape[2]), (2, N_DEV * mlp_w_down.shape[1], D)], name="allgather_mlp",
        dep=ag_ab[3])
    ag_sc, wait_sc = _allgather_start(
        [sc_w_in[0].astype(BF16), sc_w_out[0].astype(BF16)], ["col", "row"],
        [(D, N_DEV * sc_w_in.shape[2]), (N_DEV * sc_w_out.shape[1], D)], name="allgather_sc",
        dep=ag_mlp[3])

    per_dev = [_unpack_rows(small_params[s], small_sh_shapes) for s in range(N_DEV)]
    conv_w_full = jnp.concatenate([p[0] for p in per_dev], axis=1)
    sc_norm_full = jnp.concatenate([p[1] for p in per_dev], axis=0)[None]
    sc_conv_full = jnp.concatenate([p[2] for p in per_dev], axis=1)

    buckets = _bucket_tables()
    biases = _bias_tables(rel_bias, buckets, name="bias_tables")

    n0 = _rmsnorm_fwd(xs, ab_norm, name="norm_ab", dep=ag_sc[3])
    w_in_g, w_out = wait_ab(n0)
    w_in = jnp.transpose(w_in_g, (1, 0, 2)).reshape(D, AB_IN)
    w_c = w_in[:, :C2]
    w_q = w_in[:, C2:]
    w_grp = [jnp.concatenate([w_q[:, t * N_GROUPS * ATTN_OUT + g * ATTN_OUT:][:, :ATTN_OUT] for t in range(3)], axis=1)
             for g in range(N_GROUPS)]
    uc = _mm_nn(n0, w_c, out_dtype=F32, name="mm_ab_in_conv")
    uq = _mm_nn(n0, w_q, out_dtype=BF16, name="mm_ab_in_qkv")
    ya, hglu, ct = _conv_fwd(uc, conv_w_full, ab_conv_b, ab_ln_g, ab_ln_b, name="conv_fwd")
    outs, lses = zip(*[_attn_fwd(uq, biases, g, dil, name=f"attn_fwd_{g}")
                       for g, (_, dil) in enumerate(DILATED_GROUPS)])
    cat, outf, lse = _attn_merge(outs, lses, ya, name="attn_merge")
    h1 = _mm_nn(cat, w_out, out_dtype=F32, residual=xs, name="mm_ab_out")
    n1 = _rmsnorm_fwd(h1, mlp_norm[0:1], name="norm_mlp0")
    w_up, w_dn = wait_mlp(n1)
    z0 = _mm_nn(n1, w_up[0], out_dtype=BF16, name="mm_up0")
    h2 = _mm_nn(z0, w_dn[0], out_dtype=F32, residual=h1, a_fn=_relu_sq, name="mm_down0")
    n2 = _rmsnorm_fwd(h2, sc_norm_full, name="norm_sc")
    w_sc_in, w_sc_out = wait_sc(n2)
    u3 = _mm_nn(n2, w_sc_in, out_dtype=BF16, name="mm_sc_in")
    ysc = _sc_fwd(u3, sc_conv_full, name="sc_fwd")
    h3 = _mm_nn(ysc, w_sc_out, out_dtype=F32, residual=h2, name="mm_sc_out")
    n3 = _rmsnorm_fwd(h3, mlp_norm[1:2], name="norm_mlp1")
    z1 = _mm_nn(n3, w_up[1], out_dtype=BF16, name="mm_up1")
    h4 = _mm_nn(z1, w_dn[1], out_dtype=F32, residual=h3, a_fn=_relu_sq, name="mm_down1")

    def dz_epilogue(acc, z):
        return acc * (2.0 * jnp.maximum(z.astype(F32), 0.0))

    dh4, dh4b, acc_final = _loss_bwd(h4, tgt, final_norm[None], name="loss_bwd")
    dz1 = _mm_nt([(dh4b, w_dn[1])], out_dtype=BF16, epilogue=dz_epilogue, extra=z1, name="mm_d_down1")
    g_dn1 = _mm_tn(z1, dh4b, a_fn=_relu_sq, name="mm_gw_down1")
    g_up1 = _mm_tn(n3, dz1, name="mm_gw_up1")
    rs_mlp1, wait_rs_mlp1 = _scatter_start([g_up1, g_dn1], ["col", "row"],
                                           [mlp_w_up.shape[1:], mlp_w_down.shape[1:]], name="scatter_mlp1")
    dh3, dh3b, acc_mlp1 = _mm_nt_rms_bwd(dz1, w_up[1], h3, mlp_norm[1:2], dh4, name="mm_d_up1_norm_bwd",
                                         dep=rs_mlp1[3])

    dysc = _mm_nt([(dh3b, w_sc_out)], out_dtype=F32, name="mm_d_sc_out")
    g_sc_out = _mm_tn(ysc, dh3b, name="mm_gw_sc_out")
    du3, acc_scw = _sc_bwd(u3, dysc, sc_conv_full, name="sc_bwd")
    g_sc_in = _mm_tn(n2, du3, name="mm_gw_sc_in")
    rs_sc, wait_rs_sc = _scatter_start([g_sc_in, g_sc_out], ["col", "row"],
                                       [sc_w_in.shape[1:], sc_w_out.shape[1:]], name="scatter_sc")
    dh2, dh2b, acc_sc = _mm_nt_rms_bwd(du3, w_sc_in, h2, sc_norm_full, dh3, name="mm_d_sc_in_norm_bwd",
                                       dep=rs_sc[3])

    dz0 = _mm_nt([(dh2b, w_dn[0])], out_dtype=BF16, epilogue=dz_epilogue, extra=z0, name="mm_d_down0")
    g_dn0 = _mm_tn(z0, dh2b, a_fn=_relu_sq, name="mm_gw_down0")
    g_up0 = _mm_tn(n1, dz0, name="mm_gw_up0")
    rs_mlp0, wait_rs_mlp0 = _scatter_start([g_up0, g_dn0], ["col", "row"],
                                           [mlp_w_up.shape[1:], mlp_w_down.shape[1:]], name="scatter_mlp0")
    dh1, dh1b, acc_mlp0 = _mm_nt_rms_bwd(dz0, w_up[0], h1, mlp_norm[0:1], dh2, name="mm_d_up0_norm_bwd",
                                         dep=rs_mlp0[3])

    dcat = _mm_nt([(dh1b, w_out)], out_dtype=F32, name="mm_d_ab_out")
    g_ab_out = _mm_tn(cat, dh1b, name="mm_gw_ab_out")
    dyb, delta = _attn_prep(dcat, outf, name="attn_prep")
    dqkv, dbs = zip(*[_attn_bwd(uq, dyb, lse, delta, biases, g, dil, name=f"attn_bwd_{g}")
                      for g, (_, dil) in enumerate(DILATED_GROUPS)])
    drel = _relbias_reduce(dbs, buckets, name="relbias_reduce")
    dc, acc_conv = _conv_bwd_ln(ct, dcat, hglu, ab_ln_g, ab_ln_b, name="conv_bwd_ln")
    duc = _conv_bwd_in(dc, uc, conv_w_full, name="conv_bwd_in")
    g_wc = _mm_tn(n0, duc, name="mm_gw_ab_in_conv")
    g_wgrp = [_mm_tn(n0, dqkv[g], name=f"mm_gw_ab_in_qkv{g}") for g in range(N_GROUPS)]
    g_wq = jnp.concatenate([g_wgrp[g][:, t * ATTN_OUT:(t + 1) * ATTN_OUT]
                            for t in range(3) for g in range(N_GROUPS)], axis=1)
    g_w_in = jnp.concatenate([g_wc, g_wq], axis=1).reshape(D, N_DEV, AB_IN // N_DEV).transpose(1, 0, 2)
    rs_ab, wait_rs_ab = _scatter_start([g_w_in, g_ab_out], ["slot", "row"],
                                       [w_in_sh.shape, ab_w_out.shape[1:]], name="scatter_ab")
    dn0 = _mm_nt([(duc, w_c)] + [(dqkv[g], w_grp[g]) for g in range(N_GROUPS)], out_dtype=F32, name="mm_d_ab_in",
                 dep=rs_ab[3])
    grad_x, grad_xb, acc_ab = _rms_bwd(xs, ab_norm, dn0, dh1, name="norm_ab_bwd")

    small_full = [drel[:, :, :NUM_BUCKETS].transpose(2, 0, 1).reshape(NUM_BUCKETS, N_GROUPS * HEADS_PER_GROUP),
                  acc_ab[0], acc_conv[0:CONV_A_WIDTH], acc_conv[32],
                  acc_conv[33], acc_conv[34], acc_sc[0], acc_scw[0:SC_CONV_WIDTH],
                  jnp.stack([acc_mlp0[0], acc_mlp1[0]]), acc_final[0], acc_final[1]]
    small_full_shapes = [p.shape for p in small_full]
    summed = _unpack_rows(_small_gather(_pack_rows(small_full), reduce=True, name="allreduce_small"),
                          small_full_shapes)
    (s_rel, s_abn, s_cw, s_cb, s_lg, s_lb, s_scn, s_scw, s_mlpn, s_fn, s_err) = summed
    loss = (0.5 / D) * jnp.sum(s_err)
    small_grads = {
        "rel_bias": s_rel, "ab_norm": s_abn[None],
        "ab_conv_w": lax.dynamic_slice_in_dim(s_cw, me_lin * cw_sh, cw_sh, axis=1)[None],
        "ab_conv_b": s_cb[None], "ab_ln_g": s_lg[None], "ab_ln_b": s_lb[None],
        "sc_norm": lax.dynamic_slice_in_dim(s_scn, me_lin * scn_sh, scn_sh, axis=0)[None],
        "sc_conv_w": lax.dynamic_slice_in_dim(s_scw, me_lin * scw_sh, scw_sh, axis=1)[None],
        "mlp_norm": s_mlpn, "final_norm": s_fn,
    }
    small_w = {"rel_bias": (rel_bias, m_rel_bias, v_rel_bias), "ab_norm": (ab_norm, m_ab_norm, v_ab_norm),
               "ab_conv_w": (ab_conv_w, m_ab_conv_w, v_ab_conv_w), "ab_conv_b": (ab_conv_b, m_ab_conv_b, v_ab_conv_b),
               "ab_ln_g": (ab_ln_g, m_ab_ln_g, v_ab_ln_g), "ab_ln_b": (ab_ln_b, m_ab_ln_b, v_ab_ln_b),
               "sc_norm": (sc_norm, m_sc_norm, v_sc_norm), "sc_conv_w": (sc_conv_w, m_sc_conv_w, v_sc_conv_w),
               "mlp_norm": (mlp_norm, m_mlp_norm, v_mlp_norm), "final_norm": (final_norm, m_final_norm, v_final_norm)}
    small_names = list(small_grads)
    small_shapes = [small_grads[n].shape for n in small_names]
    d_pack, m_pack, v_pack = _adamw_small(
        _pack_rows([small_grads[n] for n in small_names]), _pack_rows([small_w[n][0] for n in small_names]),
        _pack_rows([small_w[n][1] for n in small_names]), _pack_rows([small_w[n][2] for n in small_names]),
        name="adamw_small")
    small = {n: (small_grads[n], d, nm_, nv_) for n, d, nm_, nv_ in zip(
        small_names, _unpack_rows(d_pack, small_shapes), _unpack_rows(m_pack, small_shapes),
        _unpack_rows(v_pack, small_shapes))}

    p_up1, p_dn1 = wait_rs_mlp1(grad_xb)
    p_sc_in, p_sc_out = wait_rs_sc(grad_xb)
    p_up0, p_dn0 = wait_rs_mlp0(grad_xb)
    p_w_in, p_ab_out = wait_rs_ab(grad_xb)
    big = {}
    for nm, parts, w, m, v in (("ab_w_in", [p_w_in], ab_w_in, m_ab_w_in, v_ab_w_in),
                               ("ab_w_out", [p_ab_out], ab_w_out, m_ab_w_out, v_ab_w_out),
                               ("sc_w_in", [p_sc_in], sc_w_in, m_sc_w_in, v_sc_w_in),
                               ("sc_w_out", [p_sc_out], sc_w_out, m_sc_w_out, v_sc_w_out),
                               ("mlp_w_up", [p_up0, p_up1], mlp_w_up, m_mlp_w_up, v_mlp_w_up),
                               ("mlp_w_down", [p_dn0, p_dn1], mlp_w_down, m_mlp_w_down, v_mlp_w_down)):
        C = w.shape[-1]
        res = _adamw_from_partials(parts, w.reshape(-1, C), m.reshape(-1, C), v.reshape(-1, C), name="adamw_" + nm)
        big[nm] = tuple(r.reshape(w.shape) for r in res)

    order = ["rel_bias", "ab_norm", "ab_w_in", "ab_conv_w", "ab_conv_b", "ab_ln_g", "ab_ln_b", "ab_w_out",
             "sc_norm", "sc_w_in", "sc_conv_w", "sc_w_out", "mlp_norm", "mlp_w_up", "mlp_w_down", "final_norm"]
    allres = {**big, **small}
    return (loss, grad_x.reshape(x.shape),
            *[allres[n][0] for n in order], *[allres[n][1] for n in order],
            *[allres[n][2] for n in order], *[allres[n][3] for n in order])
```

```python
import functools
import math

import numpy as np
import jax
import jax.numpy as jnp
from jax import lax
from jax.experimental import pallas as pl
from jax.experimental.pallas import tpu as pltpu

F32 = jnp.float32
BF16 = jnp.bfloat16

HEAD_DIM = 64
HEADS_PER_GROUP = 8
DILATED_GROUPS = ((128, 1), (512, 4), (2048, 16))
N_GROUPS = 3
ATTN_OUT = HEADS_PER_GROUP * HEAD_DIM
ATTN_IN = 3 * N_GROUPS * ATTN_OUT
GROUP_QKV = 3 * ATTN_OUT
ATTN_BLK = 128
CONV_A_WIDTH = 31
SC_CONV_WIDTH = 3
NUM_BUCKETS = 32
REL_MAX_DISTANCE = 2048
RMS_EPS = 1e-6
LN_EPS = 1e-5
NEG_INF = -1e30
ADAM_LR = 0.001
ADAM_B1 = 0.9
ADAM_B2 = 0.999
ADAM_EPS = 1e-08
ADAM_WD = 0.01
ADAM_STEP = 10

N_DEV = 8
HALO = 32
LANES = 128
VMEM_LIMIT = 56 * 1024 * 1024
MESH = pl.DeviceIdType.MESH
ANY = pl.BlockSpec(memory_space=pl.ANY)
VMEM_SPEC = pl.BlockSpec(memory_space=pltpu.VMEM)


def _tile(n, prefs):
    for t in prefs:
        if n % t == 0:
            return t
    return n


def _cparams(*sem):
    return pltpu.CompilerParams(dimension_semantics=sem, vmem_limit_bytes=VMEM_LIMIT)


def _relu_sq(z):
    return jnp.square(jnp.maximum(z, 0))


def _dot_nt(a, b):
    return lax.dot_general(a, b, (((1,), (1,)), ((), ())), preferred_element_type=F32)


def _dot_tn(a, b):
    return lax.dot_general(a, b, (((0,), (0,)), ((), ())), preferred_element_type=F32)


def _mm_nn(a, b, *, out_dtype, name, residual=None, a_fn=None, slabs=1):
    M, K = a.shape
    K //= slabs
    _, N = b.shape
    tm = _tile(M, (2048, 1024, 512, 256))
    tn = _tile(N, (512, 384, 256, 128))
    tk = _tile(K, (1024, 512, 256, 128))
    nk = K // tk
    nj = N // tn
    has_res = residual is not None

    def body(*refs):
        if has_res:
            a_ref, b_ref, r_ref, o_ref = refs[:4]
        else:
            a_ref, b_ref, o_ref = refs[:3]
        av = a_ref[...]
        if a_fn is not None:
            av = a_fn(av)
        part = jnp.dot(av, b_ref[...], preferred_element_type=F32)

        def finish(acc):
            if has_res:
                acc = acc + r_ref[...]
            o_ref[...] = acc.astype(o_ref.dtype)

        if nk == 1:
            finish(part)
        else:
            acc_ref = refs[-1]
            k = pl.program_id(2)

            @pl.when(k == 0)
            def _():
                acc_ref[...] = part

            @pl.when((k > 0) & (k < nk - 1))
            def _():
                acc_ref[...] += part

            @pl.when(k == nk - 1)
            def _():
                finish(acc_ref[...] + part)

    in_specs = [pl.BlockSpec((tm, tk), lambda i, j, k: (i, (j // nj) * nk + k)),
                pl.BlockSpec((tk, tn), lambda i, j, k: (k, j % nj))]
    args = [a, b]
    if has_res:
        in_specs.append(pl.BlockSpec((tm, tn), lambda i, j, k: (i, j)))
        args.append(residual)
    return pl.pallas_call(
        body, name=name,
        out_shape=jax.ShapeDtypeStruct((M, slabs * N), out_dtype),
        grid=(M // tm, slabs * nj, nk),
        in_specs=in_specs,
        out_specs=pl.BlockSpec((tm, tn), lambda i, j, k: (i, j)),
        scratch_shapes=[pltpu.VMEM((tm, tn), F32)] if nk > 1 else [],
        compiler_params=_cparams("parallel", "parallel", "arbitrary"),
    )(*args)


def _mm_nt(pairs, *, out_dtype, name, epilogue=None, extra=None, dep=None, slabs=1):
    assert slabs == 1 or len(pairs) == 1
    M = pairs[0][0].shape[0]
    Ko = pairs[0][1].shape[0]
    tm = _tile(M, (1024, 512, 256))
    to = _tile(Ko, (1024, 512, 256, 128))
    njo = Ko // to
    tks = [_tile(p[0].shape[1] // slabs, (1024, 768, 512, 256, 128)) for p in pairs]
    steps = [p[0].shape[1] // slabs // tk for p, tk in zip(pairs, tks)]
    offs = [sum(steps[:i]) for i in range(len(pairs))]
    nk = sum(steps)
    npair = len(pairs)
    has_extra = extra is not None

    def body(*refs):
        ab = refs[:2 * npair]
        pos = 2 * npair
        e_ref = None
        if has_extra:
            e_ref = refs[pos]
            pos += 1
        if dep is not None:
            pos += 1
        o_ref = refs[pos]
        acc_ref = refs[pos + 1]
        k = pl.program_id(2)

        @pl.when(k == 0)
        def _():
            acc_ref[...] = jnp.zeros_like(acc_ref)

        for p in range(npair):
            @pl.when((k >= offs[p]) & (k < offs[p] + steps[p]))
            def _(p=p):
                acc_ref[...] += _dot_nt(ab[2 * p][...], ab[2 * p + 1][...])

        @pl.when(k == nk - 1)
        def _():
            acc = acc_ref[...]
            if epilogue is not None:
                acc = epilogue(acc, e_ref[...] if has_extra else None)
            o_ref[...] = acc.astype(o_ref.dtype)

    in_specs, args = [], []
    for p, (a, b) in enumerate(pairs):
        def kidx(k, p=p):
            return jnp.clip(k - offs[p], 0, steps[p] - 1)
        in_specs.append(pl.BlockSpec((tm, tks[p]),
                                     lambda i, j, k, kidx=kidx, p=p: (i, (j // njo) * steps[p] + kidx(k))))
        in_specs.append(pl.BlockSpec((to, tks[p]), lambda i, j, k, kidx=kidx: (j % njo, kidx(k))))
        args += [a, b]
    if has_extra:
        in_specs.append(pl.BlockSpec((tm, to), lambda i, j, k: (i, j)))
        args.append(extra)
    if dep is not None:
        in_specs.append(ANY)
        args.append(dep)
    return pl.pallas_call(
        body, name=name,
        out_shape=jax.ShapeDtypeStruct((M, slabs * Ko), out_dtype),
        grid=(M // tm, slabs * njo, nk),
        in_specs=in_specs,
        out_specs=pl.BlockSpec((tm, to), lambda i, j, k: (i, j)),
        scratch_shapes=[pltpu.VMEM((tm, to), F32)],
        compiler_params=_cparams("parallel", "parallel", "arbitrary"),
    )(*args)


def _mm_tn(a, b, *, name, a_fn=None, slabs=1):
    M, K = a.shape
    K //= slabs
    N = b.shape[1] // slabs
    tm = _tile(M, (2048, 1024, 512, 256))
    tk = _tile(K, (1024, 768, 512, 384, 256, 128))
    tn = _tile(N, (1024, 768, 512, 384, 256, 128))
    nmi = M // tm
    nm = slabs * nmi
    nki, nnj = K // tk, N // tn

    def body(a_ref, b_ref, o_ref, acc_ref):
        m = pl.program_id(2)
        av = a_ref[...]
        if a_fn is not None:
            av = a_fn(av)
        part = _dot_tn(av, b_ref[...])
        if nm == 1:
            o_ref[...] = part.astype(o_ref.dtype)
            return

        @pl.when(m == 0)
        def _():
            acc_ref[...] = part

        @pl.when((m > 0) & (m < nm - 1))
        def _():
            acc_ref[...] += part

        @pl.when(m == nm - 1)
        def _():
            o_ref[...] = (acc_ref[...] + part).astype(o_ref.dtype)

    return pl.pallas_call(
        body, name=name,
        out_shape=jax.ShapeDtypeStruct((K, N), BF16),
        grid=(K // tk, N // tn, nm),
        in_specs=[pl.BlockSpec((tm, tk), lambda i, j, m: (m % nmi, (m // nmi) * nki + i)),
                  pl.BlockSpec((tm, tn), lambda i, j, m: (m % nmi, (m // nmi) * nnj + j))],
        out_specs=pl.BlockSpec((tk, tn), lambda i, j, m: (i, j)),
        scratch_shapes=[pltpu.VMEM((tk, tn), F32)],
        compiler_params=_cparams("parallel", "parallel", "arbitrary"),
    )(a, b)


def _rmsnorm_fwd(h, g, *, name, dep=None, views=()):
    S, D = h.shape
    tm = _tile(S, (512, 256))
    nv = len(views)

    def body(h_ref, g_ref, *rest):
        n_out = 1 + nv
        outs = rest[len(rest) - n_out - (1 if nv else 0):len(rest) - (1 if nv else 0)]
        x = h_ref[...]
        r = lax.rsqrt(jnp.mean(x * x, axis=-1, keepdims=True) + RMS_EPS)
        y = x * r * g_ref[...]
        outs[0][...] = y.astype(BF16)
        if nv:
            scr = rest[-1]
            _to_chunks(scr, y)
            for v_ref, d in zip(outs[1:], views):
                _slabs_from_chunks(v_ref, scr, d, BF16)

    res = pl.pallas_call(
        body, name=name,
        out_shape=(jax.ShapeDtypeStruct((S, D), BF16),)
        + tuple(jax.ShapeDtypeStruct((S // d, d * D), BF16) for d in views),
        grid=(S // tm,),
        in_specs=[pl.BlockSpec((tm, D), lambda i: (i, 0)), pl.BlockSpec((1, D), lambda i: (0, 0))]
        + ([ANY] if dep is not None else []),
        out_specs=(pl.BlockSpec((tm, D), lambda i: (i, 0)),)
        + tuple(pl.BlockSpec((tm // d, d * D), lambda i: (i, 0)) for d in views),
        scratch_shapes=[_chunk_scratch(tm, D)] if nv else [],
        compiler_params=_cparams("parallel"),
    )(h, g, *([dep] if dep is not None else []))
    return res if nv else res[0]


def _rms_bwd_rows(x, g, dy):
    r = lax.rsqrt(jnp.mean(x * x, axis=-1, keepdims=True) + RMS_EPS)
    xh = x * r
    gy = dy * g
    dx = r * (gy - xh * jnp.mean(xh * gy, axis=-1, keepdims=True))
    return dx, dy * xh


def _rms_bwd(x, g, dn, dres, *, name, dn_views=()):
    S, D = x.shape
    tm = _tile(S, (256,))
    nv = len(dn_views)

    def body(x_ref, g_ref, dn_ref, dr_ref, *rest):
        v_refs = rest[:nv]
        dx_ref, dxb_ref, dg_ref = rest[nv:nv + 3]
        scr = rest[nv + 3:]
        i = pl.program_id(0)
        dn = dn_ref[...]
        for v_ref, s_ref, (_, d) in zip(v_refs, scr, dn_views):
            _chunks_from_slabs(s_ref, v_ref, d)
            dn = dn + _from_chunks(s_ref)
        dx, dgx = _rms_bwd_rows(x_ref[...], g_ref[...], dn)
        tot = dr_ref[...] + dx
        dx_ref[...] = tot
        dxb_ref[...] = tot.astype(BF16)

        @pl.when(i == 0)
        def _():
            dg_ref[...] = jnp.zeros_like(dg_ref)

        dg_ref[0:1, :] += jnp.sum(dgx, axis=0, keepdims=True)

    row = pl.BlockSpec((tm, D), lambda i: (i, 0))
    return pl.pallas_call(
        body, name=name,
        out_shape=(jax.ShapeDtypeStruct((S, D), F32), jax.ShapeDtypeStruct((S, D), BF16),
                   jax.ShapeDtypeStruct((8, D), F32)),
        grid=(S // tm,),
        in_specs=[row, pl.BlockSpec((1, D), lambda i: (0, 0)), row, row]
        + [pl.BlockSpec((tm // d, d * D), lambda i: (i, 0)) for _, d in dn_views],
        out_specs=(row, row, pl.BlockSpec((8, D), lambda i: (0, 0))),
        scratch_shapes=[_chunk_scratch(tm, D)] * nv,
        compiler_params=_cparams("arbitrary"),
    )(x, g, dn, dres, *[a for a, _ in dn_views])


def _mm_nt_rms_bwd(a, b, x, g, dres, *, name, dep=None):
    M, N = a.shape
    D = b.shape[0]
    tm = _tile(M, (1024, 512, 256))
    tk = _tile(N, (1024, 512, 256, 128))
    nk = N // tk

    def body(a_ref, b_ref, x_ref, g_ref, dr_ref, *rest):
        dx_ref, dxb_ref, dg_ref, acc_ref = rest[-4:]
        i = pl.program_id(0)
        k = pl.program_id(1)
        part = _dot_nt(a_ref[...], b_ref[...])

        @pl.when((i == 0) & (k == 0))
        def _():
            dg_ref[...] = jnp.zeros_like(dg_ref)

        @pl.when(k == 0)
        def _():
            acc_ref[...] = part

        @pl.when((k > 0) & (k < nk - 1))
        def _():
            acc_ref[...] += part

        @pl.when(k == nk - 1)
        def _():
            dn = part if nk == 1 else acc_ref[...] + part
            dx, dgx = _rms_bwd_rows(x_ref[...], g_ref[...], dn)
            tot = dr_ref[...] + dx
            dx_ref[...] = tot
            dxb_ref[...] = tot.astype(BF16)
            dg_ref[0:1, :] += jnp.sum(dgx, axis=0, keepdims=True)

    row = pl.BlockSpec((tm, D), lambda i, k: (i, 0))
    in_specs = [pl.BlockSpec((tm, tk), lambda i, k: (i, k)), pl.BlockSpec((D, tk), lambda i, k: (0, k)),
                row, pl.BlockSpec((1, D), lambda i, k: (0, 0)), row]
    args = [a, b, x, g, dres]
    if dep is not None:
        in_specs.append(ANY)
        args.append(dep)
    return pl.pallas_call(
        body, name=name,
        out_shape=(jax.ShapeDtypeStruct((M, D), F32), jax.ShapeDtypeStruct((M, D), BF16),
                   jax.ShapeDtypeStruct((8, D), F32)),
        grid=(M // tm, nk),
        in_specs=in_specs,
        out_specs=(row, row, pl.BlockSpec((8, D), lambda i, k: (0, 0))),
        scratch_shapes=[pltpu.VMEM((tm, D), F32)],
        compiler_params=_cparams("arbitrary", "arbitrary"),
    )(*args)


def _loss_bwd(h, target, g, *, name):
    S, D = h.shape
    tm = _tile(S, (256,))

    def body(h_ref, t_ref, g_ref, dx_ref, dxb_ref, acc_ref):
        i = pl.program_id(0)
        x = h_ref[...]
        gv = g_ref[...]
        r = lax.rsqrt(jnp.mean(x * x, axis=-1, keepdims=True) + RMS_EPS)
        err = x * r * gv - t_ref[...]
        dx, dgx = _rms_bwd_rows(x, gv, err * (1.0 / D))
        dx_ref[...] = dx
        dxb_ref[...] = dx.astype(BF16)

        @pl.when(i == 0)
        def _():
            acc_ref[...] = jnp.zeros_like(acc_ref)

        acc_ref[0:1, :] += jnp.sum(dgx, axis=0, keepdims=True)
        acc_ref[1:2, :] += jnp.sum(err * err, axis=0, keepdims=True)

    row = pl.BlockSpec((tm, D), lambda i: (i, 0))
    return pl.pallas_call(
        body, name=name,
        out_shape=(jax.ShapeDtypeStruct((S, D), F32), jax.ShapeDtypeStruct((S, D), BF16),
                   jax.ShapeDtypeStruct((8, D), F32)),
        grid=(S // tm,),
        in_specs=[row, row, pl.BlockSpec((1, D), lambda i: (0, 0))],
        out_specs=(row, row, pl.BlockSpec((8, D), lambda i: (0, 0))),
        compiler_params=_cparams("arbitrary"),
    )(h, target, g)


SUBLANES = 8
CONV_ROWS = 64


def _build_shifted(ext_ref, rot_ref, ts):
    rows = ts + HALO - SUBLANES
    for j in range(1, SUBLANES):
        rot_ref[j, 0:rows, :] = ext_ref[j:j + rows, :]


def _shifted(ext_ref, rot_ref, off, r0, nrows, cs):
    q, j = divmod(off, SUBLANES)
    start = SUBLANES * q + r0
    if j == 0:
        return ext_ref[start:start + nrows, cs]
    return rot_ref[j, start:start + nrows, cs]


def _conv_fwd(uc, conv_w, conv_b, ln_g, ln_b, *, name):
    S, C2 = uc.shape
    C = C2 // 2
    ts = _tile(S, (256,))
    per = ts // HALO

    def body(cur_ref, halo_ref, w_ref, b_ref, g_ref, beta_ref, ya_ref, h_ref, ct_ref, ext_ref, rot_ref):
        i = pl.program_id(0)
        hh = halo_ref[:, 0:C] * jax.nn.sigmoid(halo_ref[:, C:C2])
        ext_ref[0:HALO, :] = jnp.where(i == 0, 0.0, hh)
        hc = cur_ref[:, 0:C] * jax.nn.sigmoid(cur_ref[:, C:C2])
        ext_ref[HALO:HALO + ts, :] = hc
        h_ref[...] = hc
        _build_shifted(ext_ref, rot_ref, ts)
        for c0 in range(0, C, LANES):
            cs = slice(c0, c0 + LANES)
            for r0 in range(0, ts, CONV_ROWS):
                acc = jnp.zeros((CONV_ROWS, LANES), F32)
                for k in range(CONV_A_WIDTH):
                    acc = acc + w_ref[k:k + 1, cs] * _shifted(ext_ref, rot_ref, k + 2, r0, CONV_ROWS, cs)
                ct_ref[r0:r0 + CONV_ROWS, cs] = acc + b_ref[:, cs]
        ct = ct_ref[...]
        mu = jnp.mean(ct, axis=-1, keepdims=True)
        xc = ct - mu
        var = jnp.mean(xc * xc, axis=-1, keepdims=True)
        l = xc * lax.rsqrt(var + LN_EPS) * g_ref[...] + beta_ref[...]
        ya_ref[...] = (l * jax.nn.sigmoid(l)).astype(ya_ref.dtype)

    vec = pl.BlockSpec((1, C), lambda i: (0, 0))
    row = pl.BlockSpec((ts, C), lambda i: (i, 0))
    return pl.pallas_call(
        body, name=name,
        out_shape=(jax.ShapeDtypeStruct((S, C), BF16), jax.ShapeDtypeStruct((S, C), F32),
                   jax.ShapeDtypeStruct((S, C), F32)),
        grid=(S // ts,),
        in_specs=[pl.BlockSpec((ts, C2), lambda i: (i, 0)),
                  pl.BlockSpec((HALO, C2), lambda i: (jnp.maximum(i * per - 1, 0), 0)),
                  pl.BlockSpec((CONV_A_WIDTH, C), lambda i: (0, 0)), vec, vec, vec],
        out_specs=(row, row, row),
        scratch_shapes=[pltpu.VMEM((HALO + ts, C), F32), pltpu.VMEM((8, HALO + ts, C), F32)],
        compiler_params=_cparams("parallel"),
    )(uc, uc, conv_w, conv_b, ln_g, ln_b)


CONV_ACC_ROWS = 40


def _conv_bwd_ln(ct, dcat, hglu, ln_g, ln_b, *, name):
    S, C = ct.shape
    CW = dcat.shape[1]
    ts = _tile(S, (256,))
    per = ts // HALO

    def body(ct_ref, dcat_ref, hc_ref, hh_ref, g_ref, beta_ref, dc_ref, acc_ref, ext_ref, rot_ref):
        i = pl.program_id(0)
        ct = ct_ref[...]
        gv = g_ref[...]
        mu = jnp.mean(ct, axis=-1, keepdims=True)
        xc = ct - mu
        rstd = lax.rsqrt(jnp.mean(xc * xc, axis=-1, keepdims=True) + LN_EPS)
        xh = xc * rstd
        l = xh * gv + beta_ref[...]
        sg = jax.nn.sigmoid(l)
        dl = dcat_ref[:, 0:C] * (sg * (1.0 + l * (1.0 - sg)))
        dxh = dl * gv
        dc = rstd * (dxh - jnp.mean(dxh, axis=-1, keepdims=True)
                     - xh * jnp.mean(dxh * xh, axis=-1, keepdims=True))
        dc_ref[...] = dc

        @pl.when(i == 0)
        def _():
            acc_ref[...] = jnp.zeros_like(acc_ref)

        acc_ref[32:33, :] += jnp.sum(dc, axis=0, keepdims=True)
        acc_ref[33:34, :] += jnp.sum(dl * xh, axis=0, keepdims=True)
        acc_ref[34:35, :] += jnp.sum(dl, axis=0, keepdims=True)
        ext_ref[0:HALO, :] = jnp.where(i == 0, 0.0, hh_ref[...])
        ext_ref[HALO:HALO + ts, :] = hc_ref[...]
        _build_shifted(ext_ref, rot_ref, ts)
        for c0 in range(0, C, LANES):
            cs = slice(c0, c0 + LANES)
            dcc = dc_ref[:, cs]
            for k in range(CONV_A_WIDTH):
                acc_ref[k:k + 1, cs] += jnp.sum(dcc * _shifted(ext_ref, rot_ref, k + 2, 0, ts, cs),
                                                axis=0, keepdims=True)

    vec = pl.BlockSpec((1, C), lambda i: (0, 0))
    row = pl.BlockSpec((ts, C), lambda i: (i, 0))
    return pl.pallas_call(
        body, name=name,
        out_shape=(jax.ShapeDtypeStruct((S, C), F32), jax.ShapeDtypeStruct((CONV_ACC_ROWS, C), F32)),
        grid=(S // ts,),
        in_specs=[row, pl.BlockSpec((ts, CW), lambda i: (i, 0)), row,
                  pl.BlockSpec((HALO, C), lambda i: (jnp.maximum(i * per - 1, 0), 0)), vec, vec],
        out_specs=(row, pl.BlockSpec((CONV_ACC_ROWS, C), lambda i: (0, 0))),
        scratch_shapes=[pltpu.VMEM((HALO + ts, C), F32), pltpu.VMEM((8, HALO + ts, C), F32)],
        compiler_params=_cparams("arbitrary"),
    )(ct, dcat, hglu, hglu, ln_g, ln_b)


def _conv_bwd_in(dc, uc, conv_w, *, name):
    S, C = dc.shape
    C2 = 2 * C
    ts = _tile(S, (256,))
    per = ts // HALO
    nt = S // ts

    def body(dc_ref, dn_ref, uc_ref, w_ref, du_ref, ext_ref, rot_ref):
        i = pl.program_id(0)
        ext_ref[0:ts, :] = dc_ref[...]
        ext_ref[ts:ts + HALO, :] = jnp.where(i == nt - 1, 0.0, dn_ref[...])
        _build_shifted(ext_ref, rot_ref, ts)
        for c0 in range(0, C, LANES):
            cs = slice(c0, c0 + LANES)
            gs = slice(C + c0, C + c0 + LANES)
            for r0 in range(0, ts, CONV_ROWS):
                rs = slice(r0, r0 + CONV_ROWS)
                acc = jnp.zeros((CONV_ROWS, LANES), F32)
                for k in range(CONV_A_WIDTH):
                    acc = acc + w_ref[k:k + 1, cs] * _shifted(ext_ref, rot_ref, 30 - k, r0, CONV_ROWS, cs)
                sg = jax.nn.sigmoid(uc_ref[rs, gs])
                du_ref[rs, cs] = (acc * sg).astype(du_ref.dtype)
                du_ref[rs, gs] = (acc * uc_ref[rs, cs] * sg * (1.0 - sg)).astype(du_ref.dtype)

    return pl.pallas_call(
        body, name=name,
        out_shape=jax.ShapeDtypeStruct((S, C2), BF16),
        grid=(nt,),
        in_specs=[pl.BlockSpec((ts, C), lambda i: (i, 0)),
                  pl.BlockSpec((HALO, C), lambda i: (jnp.minimum((i + 1) * per, S // HALO - 1), 0)),
                  pl.BlockSpec((ts, C2), lambda i: (i, 0)),
                  pl.BlockSpec((CONV_A_WIDTH, C), lambda i: (0, 0))],
        out_specs=pl.BlockSpec((ts, C2), lambda i: (i, 0)),
        scratch_shapes=[pltpu.VMEM((ts + HALO, C), F32), pltpu.VMEM((8, ts + HALO, C), F32)],
        compiler_params=_cparams("parallel"),
    )(dc, dc, uc, conv_w)


def _sc_fwd(u3, conv_w, *, name):
    S, W3 = u3.shape
    W = W3 // 3
    ts = _tile(S, (256,))
    per = ts // HALO

    def body(cur_ref, halo_ref, w_ref, y_ref, ext_ref):
        i = pl.program_id(0)
        cvh = halo_ref[:, W:2 * W].astype(F32) * halo_ref[:, 2 * W:W3].astype(F32)
        ext_ref[0:HALO, :] = jnp.where(i == 0, 0.0, cvh)
        ext_ref[HALO:HALO + ts, :] = cur_ref[:, W:2 * W].astype(F32) * cur_ref[:, 2 * W:W3].astype(F32)
        k = (w_ref[0:1, :] * ext_ref[HALO - 2:HALO - 2 + ts, :]
             + w_ref[1:2, :] * ext_ref[HALO - 1:HALO - 1 + ts, :]
             + w_ref[2:3, :] * ext_ref[HALO:HALO + ts, :])
        y_ref[...] = (cur_ref[:, 0:W].astype(F32) * k).astype(y_ref.dtype)

    return pl.pallas_call(
        body, name=name,
        out_shape=jax.ShapeDtypeStruct((S, W), BF16),
        grid=(S // ts,),
        in_specs=[pl.BlockSpec((ts, W3), lambda i: (i, 0)),
                  pl.BlockSpec((HALO, W3), lambda i: (jnp.maximum(i * per - 1, 0), 0)),
                  pl.BlockSpec((SC_CONV_WIDTH, W), lambda i: (0, 0))],
        out_specs=pl.BlockSpec((ts, W), lambda i: (i, 0)),
        scratch_shapes=[pltpu.VMEM((HALO + ts, W), F32)],
        compiler_params=_cparams("parallel"),
    )(u3, u3, conv_w)


def _sc_bwd(u3, dy, conv_w, *, name):
    S, W3 = u3.shape
    W = W3 // 3
    ts = _tile(S, (256,))
    per = ts // HALO
    nt = S // ts

    def body(cur_ref, prev_ref, next_ref, dy_ref, dyn_ref, w_ref, du_ref, dw_ref, cv_ext, dk_ext):
        i = pl.program_id(0)
        cvh = prev_ref[:, W:2 * W].astype(F32) * prev_ref[:, 2 * W:W3].astype(F32)
        cv_ext[0:HALO, :] = jnp.where(i == 0, 0.0, cvh)
        c = cur_ref[:, W:2 * W].astype(F32)
        v = cur_ref[:, 2 * W:W3].astype(F32)
        b = cur_ref[:, 0:W].astype(F32)
        cv_ext[HALO:HALO + ts, :] = c * v
        dy_cur = dy_ref[...]
        dk = dy_cur * b
        dk_ext[0:ts, :] = dk
        dk_ext[ts:ts + HALO, :] = jnp.where(i == nt - 1, 0.0, dyn_ref[...] * next_ref[:, 0:W].astype(F32))
        w0, w1, w2 = w_ref[0:1, :], w_ref[1:2, :], w_ref[2:3, :]
        cv2 = cv_ext[HALO - 2:HALO - 2 + ts, :]
        cv1 = cv_ext[HALO - 1:HALO - 1 + ts, :]
        cv0 = cv_ext[HALO:HALO + ts, :]
        kconv = w0 * cv2 + w1 * cv1 + w2 * cv0
        dcv = w2 * dk + w1 * dk_ext[1:1 + ts, :] + w0 * dk_ext[2:2 + ts, :]
        du_ref[:, 0:W] = (dy_cur * kconv).astype(du_ref.dtype)
        du_ref[:, W:2 * W] = (dcv * v).astype(du_ref.dtype)
        du_ref[:, 2 * W:W3] = (dcv * c).astype(du_ref.dtype)

        @pl.when(i == 0)
        def _():
            dw_ref[...] = jnp.zeros_like(dw_ref)

        dw_ref[0:1, :] += jnp.sum(dk * cv2, axis=0, keepdims=True)
        dw_ref[1:2, :] += jnp.sum(dk * cv1, axis=0, keepdims=True)
        dw_ref[2:3, :] += jnp.sum(dk * cv0, axis=0, keepdims=True)

    nxt = lambda i: (jnp.minimum((i + 1) * per, S // HALO - 1), 0)
    return pl.pallas_call(
        body, name=name,
        out_shape=(jax.ShapeDtypeStruct((S, W3), BF16), jax.ShapeDtypeStruct((8, W), F32)),
        grid=(nt,),
        in_specs=[pl.BlockSpec((ts, W3), lambda i: (i, 0)),
                  pl.BlockSpec((HALO, W3), lambda i: (jnp.maximum(i * per - 1, 0), 0)),
                  pl.BlockSpec((HALO, W3), nxt),
                  pl.BlockSpec((ts, W), lambda i: (i, 0)),
                  pl.BlockSpec((HALO, W), nxt),
                  pl.BlockSpec((SC_CONV_WIDTH, W), lambda i: (0, 0))],
        out_specs=(pl.BlockSpec((ts, W3), lambda i: (i, 0)), pl.BlockSpec((8, W), lambda i: (0, 0))),
        scratch_shapes=[pltpu.VMEM((HALO + ts, W), F32), pltpu.VMEM((ts + HALO, W), F32)],
        compiler_params=_cparams("arbitrary"),
    )(u3, u3, u3, dy, dy, conv_w)


def _t5_causal_bucket(n):
    max_exact = NUM_BUCKETS // 2
    nf = jnp.maximum(n, 1).astype(F32)
    large = max_exact + (jnp.log(nf / max_exact) / math.log(REL_MAX_DISTANCE / max_exact)
                         * (NUM_BUCKETS - max_exact)).astype(jnp.int32)
    return jnp.where(n < max_exact, n, jnp.minimum(large, NUM_BUCKETS - 1))


def _bucket_tables():
    steps = ATTN_BLK
    m = jnp.arange(steps)[:, None] + steps - jnp.arange(2 * steps)[None, :]
    return jnp.stack([_t5_causal_bucket(jnp.clip(m, 0, steps) * dil).astype(F32) for _, dil in DILATED_GROUPS])


def _bias_tables(rel_bias, buckets, *, name):
    steps = ATTN_BLK

    def body(tab_ref, bk_ref, o_ref):
        g = pl.program_id(0)
        bk = bk_ref[0]
        a_idx = lax.broadcasted_iota(jnp.int32, (steps, 2 * steps), 0)
        c_idx = lax.broadcasted_iota(jnp.int32, (steps, 2 * steps), 1)
        m = a_idx + steps - c_idx
        band = (m >= 0) & (m <= steps)
        band_first = band & (c_idx >= steps)
        for h in range(HEADS_PER_GROUP):
            bias = jnp.zeros((steps, 2 * steps), F32)
            for b in range(NUM_BUCKETS):
                bias = jnp.where(bk == float(b), tab_ref[b, g * HEADS_PER_GROUP + h], bias)
            o_ref[0, 0, h] = jnp.where(band_first, bias, NEG_INF)
            o_ref[0, 1, h] = jnp.where(band, bias, NEG_INF)

    return pl.pallas_call(
        body, name=name,
        out_shape=jax.ShapeDtypeStruct((N_GROUPS, 2, HEADS_PER_GROUP, steps, 2 * steps), F32),
        grid=(N_GROUPS,),
        in_specs=[pl.BlockSpec(memory_space=pltpu.SMEM),
                  pl.BlockSpec((1, steps, 2 * steps), lambda g: (g, 0, 0))],
        out_specs=pl.BlockSpec((1, 2, HEADS_PER_GROUP, steps, 2 * steps), lambda g: (g, 0, 0, 0, 0)),
        compiler_params=_cparams("parallel"),
    )(rel_bias, buckets)


def _lane_is_low():
    return lax.broadcasted_iota(jnp.int32, (1, LANES), 1) < HEAD_DIM


def _qkv_specs(nb):
    nqb = GROUP_QKV // ATTN_OUT

    def spec(t, prev):
        def idx(r, n):
            nn = jnp.minimum(n, nb - 1)
            row = jnp.maximum(nn - 1, 0) if prev else nn
            return (row, r * nqb + t)
        return pl.BlockSpec((ATTN_BLK, ATTN_OUT), idx)

    return [spec(0, False), spec(1, False), spec(1, True), spec(2, False), spec(2, True)]


def _attn_fwd(uv, bias, g, d, *, name):
    rows = uv.shape[0]

    def body(q_ref, kc_ref, kp_ref, vc_ref, vp_ref, bias_ref, o_ref, l_ref):
        n = pl.program_id(1)
        sel = jnp.minimum(n, 1)
        low = _lane_is_low()
        for hp in range(HEADS_PER_GROUP // 2):
            sl = slice(hp * LANES, (hp + 1) * LANES)
            q2 = q_ref[:, sl]
            k2 = jnp.concatenate([kp_ref[:, sl], kc_ref[:, sl]], axis=0)
            v2 = jnp.concatenate([vp_ref[:, sl], vc_ref[:, sl]], axis=0)
            outs, lses = [], []
            for hh in range(2):
                msk = low if hh == 0 else jnp.logical_not(low)
                qm = jnp.where(msk, q2, jnp.zeros_like(q2))
                logits = _dot_nt(qm, k2) * (HEAD_DIM ** -0.5) + bias_ref[sel, 2 * hp + hh]
                mx = jnp.max(logits, axis=-1, keepdims=True)
                p = jnp.exp(logits - mx)
                den = jnp.sum(p, axis=-1, keepdims=True)
                pv = jnp.dot(p.astype(BF16), v2, preferred_element_type=F32)
                outs.append(pv / den)
                lses.append(jnp.broadcast_to(mx + jnp.log(den), (ATTN_BLK, LANES)))
            o_ref[:, sl] = jnp.where(low, outs[0], outs[1])
            l_ref[:, sl] = jnp.where(low, lses[0], lses[1])

    out_spec = pl.BlockSpec((ATTN_BLK, ATTN_OUT), lambda r, n: (n, r))
    return pl.pallas_call(
        body, name=name,
        out_shape=(jax.ShapeDtypeStruct((rows, d * ATTN_OUT), F32),) * 2,
        grid=(d, rows // ATTN_BLK),
        in_specs=_qkv_specs(rows // ATTN_BLK) + [pl.BlockSpec((None, 2, HEADS_PER_GROUP, ATTN_BLK, 2 * ATTN_BLK),
                                                              lambda r, n: (g, 0, 0, 0, 0))],
        out_specs=(out_spec, out_spec),
        compiler_params=_cparams("parallel", "parallel"),
    )(uv, uv, uv, uv, uv, bias)


def _chunk_scratch(n, width):
    return pltpu.VMEM((width // LANES, n, LANES), F32)


def _to_chunks(scr, val):
    for c in range(scr.shape[0]):
        scr[c] = val[:, c * LANES:(c + 1) * LANES]


def _from_chunks(scr):
    return jnp.concatenate([scr[c] for c in range(scr.shape[0])], axis=1)


def _slabs_from_chunks(dst_ref, scr, d, dtype):
    nc, n, _ = scr.shape
    for r in range(d):
        for c in range(nc):
            col = r * nc * LANES + c * LANES
            dst_ref[:, col:col + LANES] = scr[c, pl.ds(r, n // d, stride=d), :].astype(dtype)


def _chunks_from_slabs(scr, src_ref, d):
    nc, n, _ = scr.shape
    for r in range(d):
        for c in range(nc):
            col = r * nc * LANES + c * LANES
            scr[c, pl.ds(r, n // d, stride=d), :] = src_ref[:, col:col + LANES]


def _attn_merge(outs, lses, ya, *, name):
    S, C = ya.shape
    tm = _tile(S, (256,))
    dils = [dil for _, dil in DILATED_GROUPS]

    def body(o0, o1, o2, l0, l1, l2, ya_ref, cat_ref, out_ref, lse_ref, so1, so2, sl1, sl2):
        _chunks_from_slabs(so1, o1, dils[1])
        _chunks_from_slabs(so2, o2, dils[2])
        _chunks_from_slabs(sl1, l1, dils[1])
        _chunks_from_slabs(sl2, l2, dils[2])
        a0, a1, a2 = l0[...], _from_chunks(sl1), _from_chunks(sl2)
        m = jnp.maximum(jnp.maximum(a0, a1), a2)
        e0, e1, e2 = jnp.exp(a0 - m), jnp.exp(a1 - m), jnp.exp(a2 - m)
        den = e0 + e1 + e2
        out = (e0 * o0[...] + e1 * _from_chunks(so1) + e2 * _from_chunks(so2)) / den
        out_ref[...] = out
        lse_ref[...] = m + jnp.log(den)
        cat_ref[:, 0:C] = ya_ref[...]
        cat_ref[:, C:C + ATTN_OUT] = out.astype(cat_ref.dtype)

    blk = pl.BlockSpec((tm, ATTN_OUT), lambda i: (i, 0))
    vblk = [pl.BlockSpec((tm // d, d * ATTN_OUT), lambda i: (i, 0)) for d in dils]
    assert dils[0] == 1
    return pl.pallas_call(
        body, name=name,
        out_shape=(jax.ShapeDtypeStruct((S, C + ATTN_OUT), BF16), jax.ShapeDtypeStruct((S, ATTN_OUT), F32),
                   jax.ShapeDtypeStruct((S, ATTN_OUT), F32)),
        grid=(S // tm,),
        in_specs=vblk + vblk + [pl.BlockSpec((tm, C), lambda i: (i, 0))],
        out_specs=(pl.BlockSpec((tm, C + ATTN_OUT), lambda i: (i, 0)), blk, blk),
        scratch_shapes=[_chunk_scratch(tm, ATTN_OUT)] * 4,
        compiler_params=_cparams("parallel"),
    )(*outs, *lses, ya)


def _attn_prep(dcat, outf, lse, *, name):
    S, CW = dcat.shape
    C = CW - ATTN_OUT
    tm = _tile(S, (256,))
    dils = [dil for _, dil in DILATED_GROUPS]
    assert dils[0] == 1
    ones = np.kron(np.eye(HEADS_PER_GROUP, dtype=np.float32), np.ones((HEAD_DIM, HEAD_DIM), np.float32))

    nviews = 3 * (len(dils) - 1)

    def body(dcat_ref, out_ref, l_ref, ones_ref, dyb_ref, dl_ref, *rest):
        views = rest[:nviews]
        s_dyb, s_dl, s_l = rest[nviews:]
        dyb = dcat_ref[:, C:CW]
        dyb_ref[...] = dyb.astype(BF16)
        prod = dyb * out_ref[...]
        ov = ones_ref[...]
        hi, mid, lo = _split_bf16(prod)
        delta = (jnp.dot(hi, ov, preferred_element_type=F32)
                 + jnp.dot(mid, ov, preferred_element_type=F32)
                 + jnp.dot(lo, ov, preferred_element_type=F32))
        dl_ref[...] = delta
        _to_chunks(s_dyb, dyb)
        _to_chunks(s_dl, delta)
        _to_chunks(s_l, l_ref[...])
        for gi, d in enumerate(dils[1:]):
            dyb_v, dl_v, l_v = views[3 * gi:3 * gi + 3]
            _slabs_from_chunks(dyb_v, s_dyb, d, BF16)
            _slabs_from_chunks(dl_v, s_dl, d, F32)
            _slabs_from_chunks(l_v, s_l, d, F32)

    blk = pl.BlockSpec((tm, ATTN_OUT), lambda i: (i, 0))
    view_shapes, view_specs = [], []
    for d in dils[1:]:
        for dt in (BF16, F32, F32):
            view_shapes.append(jax.ShapeDtypeStruct((S // d, d * ATTN_OUT), dt))
            view_specs.append(pl.BlockSpec((tm // d, d * ATTN_OUT), lambda i: (i, 0)))
    res = pl.pallas_call(
        body, name=name,
        out_shape=(jax.ShapeDtypeStruct((S, ATTN_OUT), BF16), jax.ShapeDtypeStruct((S, ATTN_OUT), F32),
                   *view_shapes),
        grid=(S // tm,),
        in_specs=[pl.BlockSpec((tm, CW), lambda i: (i, 0)), blk, blk,
                  pl.BlockSpec((ATTN_OUT, ATTN_OUT), lambda i: (0, 0))],
        out_specs=(blk, blk, *view_specs),
        scratch_shapes=[_chunk_scratch(tm, ATTN_OUT)] * 3,
        compiler_params=_cparams("parallel"),
    )(dcat, outf, lse, jnp.asarray(ones, BF16))
    return [(res[0], res[1], lse)] + [tuple(res[2 + 3 * gi:5 + 3 * gi]) for gi in range(len(dils) - 1)]


def _attn_bwd(uv, dov, lv, dv_, bias, g, d, *, name):
    rows = uv.shape[0]
    nb = rows // ATTN_BLK
    scale = HEAD_DIM ** -0.5

    def body(q_ref, kc_ref, kp_ref, vc_ref, vp_ref, do_ref, l_ref, dl_ref, bias_ref,
             out_ref, db_ref, dq_s, dk_s, dv_s):
        r = pl.program_id(0)
        n = pl.program_id(1)
        low = _lane_is_low()

        @pl.when((r == 0) & (n == 0))
        def _():
            db_ref[...] = jnp.zeros_like(db_ref)

        @pl.when(n == 0)
        def _():
            dq_s[...] = jnp.zeros_like(dq_s)
            dk_s[...] = jnp.zeros_like(dk_s)
            dv_s[...] = jnp.zeros_like(dv_s)

        @pl.when(n < nb)
        def _():
            sel = jnp.minimum(n, 1)
            lane = lax.broadcasted_iota(jnp.int32, (1, LANES), 1)
            for hp in range(HEADS_PER_GROUP // 2):
                sl = slice(hp * LANES, (hp + 1) * LANES)
                q2 = q_ref[:, sl]
                do2 = do_ref[:, sl]
                k2 = jnp.concatenate([kp_ref[:, sl], kc_ref[:, sl]], axis=0)
                v2 = jnp.concatenate([vp_ref[:, sl], vc_ref[:, sl]], axis=0)
                lse2 = l_ref[:, sl]
                dl2 = dl_ref[:, sl]
                dqs, dks, dvs = [], [], []
                for hh in range(2):
                    msk = low if hh == 0 else jnp.logical_not(low)
                    one = lane == hh * HEAD_DIM
                    lse_col = jnp.sum(jnp.where(one, lse2, 0.0), axis=-1, keepdims=True)
                    dl_col = jnp.sum(jnp.where(one, dl2, 0.0), axis=-1, keepdims=True)
                    qm = jnp.where(msk, q2, jnp.zeros_like(q2))
                    dom = jnp.where(msk, do2, jnp.zeros_like(do2))
                    logits = _dot_nt(qm, k2) * scale + bias_ref[sel, 2 * hp + hh]
                    p = jnp.exp(logits - lse_col)
                    dp = _dot_nt(dom, v2)
                    ds = p * (dp - dl_col)
                    db_ref[2 * hp + hh] += ds
                    dsb = ds.astype(BF16)
                    dqs.append(jnp.dot(dsb, k2, preferred_element_type=F32) * scale)
                    dks.append(_dot_tn(dsb, q2) * scale)
                    dvs.append(_dot_tn(p.astype(BF16), do2))
                dq2 = jnp.where(low, dqs[0], dqs[1])
                dk2 = jnp.where(low, dks[0], dks[1])
                dv2 = jnp.where(low, dvs[0], dvs[1])
                out_ref[:, sl] = dq_s[:, sl].astype(out_ref.dtype)
                dq_s[:, sl] = dq2
                ksl = slice(ATTN_OUT + hp * LANES, ATTN_OUT + (hp + 1) * LANES)
                vsl = slice(2 * ATTN_OUT + hp * LANES, 2 * ATTN_OUT + (hp + 1) * LANES)
                out_ref[:, ksl] = (dk_s[:, sl] + dk2[0:ATTN_BLK]).astype(out_ref.dtype)
                dk_s[:, sl] = dk2[ATTN_BLK:2 * ATTN_BLK]
                out_ref[:, vsl] = (dv_s[:, sl] + dv2[0:ATTN_BLK]).astype(out_ref.dtype)
                dv_s[:, sl] = dv2[ATTN_BLK:2 * ATTN_BLK]

        @pl.when(n == nb)
        def _():
            out_ref[:, 0:ATTN_OUT] = dq_s[...].astype(out_ref.dtype)
            out_ref[:, ATTN_OUT:2 * ATTN_OUT] = dk_s[...].astype(out_ref.dtype)
            out_ref[:, 2 * ATTN_OUT:GROUP_QKV] = dv_s[...].astype(out_ref.dtype)

    rowblk = pl.BlockSpec((ATTN_BLK, ATTN_OUT), lambda r, n: (jnp.minimum(n, nb - 1), r))
    return pl.pallas_call(
        body, name=name,
        out_shape=(jax.ShapeDtypeStruct((rows, d * GROUP_QKV), BF16),
                   jax.ShapeDtypeStruct((HEADS_PER_GROUP, ATTN_BLK, 2 * ATTN_BLK), F32)),
        grid=(d, nb + 1),
        in_specs=_qkv_specs(nb) + [
            rowblk, rowblk, rowblk,
            pl.BlockSpec((None, 2, HEADS_PER_GROUP, ATTN_BLK, 2 * ATTN_BLK), lambda r, n: (g, 0, 0, 0, 0))],
        out_specs=(pl.BlockSpec((ATTN_BLK, GROUP_QKV), lambda r, n: (jnp.maximum(n - 1, 0), r)),
                   pl.BlockSpec((HEADS_PER_GROUP, ATTN_BLK, 2 * ATTN_BLK), lambda r, n: (0, 0, 0))),
        scratch_shapes=[pltpu.VMEM((ATTN_BLK, ATTN_OUT), F32)] * 3,
        compiler_params=_cparams("arbitrary", "arbitrary"),
    )(uv, uv, uv, uv, uv, dov, lv, dv_, bias)


def _split_bf16(x):
    hi = x.astype(BF16)
    r1 = x - hi.astype(F32)
    mid = r1.astype(BF16)
    lo = (r1 - mid.astype(F32)).astype(BF16)
    return hi, mid, lo


RELBIAS_CHUNK = 4096


def _relbias_reduce(dbs, buckets, *, name):
    flat = ATTN_BLK * 2 * ATTN_BLK
    dbf = jnp.stack([db.reshape(HEADS_PER_GROUP, flat) for db in dbs])
    bkf = buckets.reshape(N_GROUPS, 1, flat)

    def body(db_ref, bk_ref, o_ref):
        c = pl.program_id(1)
        rows = lax.broadcasted_iota(jnp.int32, (LANES, RELBIAS_CHUNK), 0).astype(F32)
        onehot = jnp.where(rows == bk_ref[0], 1.0, 0.0).astype(BF16)
        hi, mid, lo = _split_bf16(db_ref[0])
        part = _dot_nt(hi, onehot) + _dot_nt(mid, onehot) + _dot_nt(lo, onehot)

        @pl.when(c == 0)
        def _():
            o_ref[0] = part

        @pl.when(c > 0)
        def _():
            o_ref[0] += part

    return pl.pallas_call(
        body, name=name,
        out_shape=jax.ShapeDtypeStruct((N_GROUPS, HEADS_PER_GROUP, LANES), F32),
        grid=(N_GROUPS, flat // RELBIAS_CHUNK),
        in_specs=[pl.BlockSpec((1, HEADS_PER_GROUP, RELBIAS_CHUNK), lambda g, c: (g, 0, c)),
                  pl.BlockSpec((1, 1, RELBIAS_CHUNK), lambda g, c: (g, 0, c))],
        out_specs=pl.BlockSpec((1, HEADS_PER_GROUP, LANES), lambda g, c: (g, 0, 0)),
        compiler_params=_cparams("parallel", "arbitrary"),
    )(dbf, bkf)


def _my_position():
    x, y, c = lax.axis_index("x"), lax.axis_index("y"), lax.axis_index("c")
    return x, y, c


def _linear(pos):
    return 4 * pos[0] + 2 * pos[1] + pos[2]


def _peer(pos, k):
    x, y, c = pos
    return ((1 - x) if k & 4 else x, (1 - y) if k & 2 else y, (1 - c) if k & 1 else c)


HBM_SPEC = pl.BlockSpec(memory_space=pltpu.HBM)
SEM_SPEC = pl.BlockSpec(memory_space=pltpu.SEMAPHORE)
DATAFLOW = pltpu.SideEffectType.DATAFLOW_SIDE_EFFECTING


def _exchange_copies(src, land, sems, send_window, recv_window, with_arrivals):
    send_sems, recv_sems, local_sems = sems
    T = len(src)
    me = _my_position()
    me_lin = _linear(me)
    local = [pltpu.make_async_copy(send_window(t, src[t], me_lin), recv_window(t, land[t], me_lin),
                                   local_sems.at[t]) for t in range(T)]
    sends, arrivals = [], []
    for t in range(T):
        for k in range(1, N_DEV):
            peer = _peer(me, k)
            peer_lin = _linear(peer)
            sem = t * (N_DEV - 1) + k - 1
            sends.append(pltpu.make_async_remote_copy(
                src_ref=send_window(t, src[t], peer_lin), dst_ref=recv_window(t, land[t], me_lin),
                send_sem=send_sems.at[sem], recv_sem=recv_sems.at[sem],
                device_id=peer, device_id_type=MESH))
            if with_arrivals:
                arrivals.append(pltpu.make_async_remote_copy(
                    src_ref=send_window(t, src[t], me_lin), dst_ref=recv_window(t, land[t], peer_lin),
                    send_sem=send_sems.at[sem], recv_sem=recv_sems.at[sem],
                    device_id=peer, device_id_type=MESH))
    return local, sends, arrivals


def _exchange_start(srcs, land_shapes, send_window, recv_window, *, name, dep=None):
    T = len(srcs)
    n_in = 2 * T + (1 if dep is not None else 0)

    def body(*refs):
        src = refs[:T]
        land = refs[T:2 * T]
        sems = refs[n_in:n_in + 3]
        token = refs[-1]
        local, sends, _ = _exchange_copies(src, land, sems, send_window, recv_window, False)
        for cp in local + sends:
            cp.start()
        token[...] = jnp.zeros_like(token)

    lands = [lax.empty(ls.shape, ls.dtype) for ls in land_shapes]
    operands = [pltpu.with_memory_space_constraint(a, pltpu.HBM) for a in list(srcs) + lands]
    outs = pl.pallas_call(
        body, name=name,
        out_shape=(pltpu.SemaphoreType.DMA((T * (N_DEV - 1),)), pltpu.SemaphoreType.DMA((T * (N_DEV - 1),)),
                   pltpu.SemaphoreType.DMA((T,)),
                   *[pltpu.HBM(a.shape, a.dtype) for a in operands],
                   jax.ShapeDtypeStruct((8, LANES), F32)),
        in_specs=[HBM_SPEC] * (2 * T) + ([ANY] if dep is not None else []),
        out_specs=(SEM_SPEC,) * 3 + (HBM_SPEC,) * (2 * T) + (VMEM_SPEC,),
        input_output_aliases={i: 3 + i for i in range(2 * T)},
        compiler_params=pltpu.CompilerParams(has_side_effects=DATAFLOW),
    )(*operands, *([dep] if dep is not None else []))
    return outs[:3], outs[3:3 + T], outs[3 + T:3 + 2 * T], outs[-1]


def _exchange_wait(started, after, send_window, recv_window, *, name):
    sems, srcs, lands, _ = started
    T = len(srcs)

    def body(*refs):
        src = refs[:T]
        land = refs[T:2 * T]
        sem_refs = refs[2 * T:2 * T + 3]
        local, sends, arrivals = _exchange_copies(src, land, sem_refs, send_window, recv_window, True)
        for cp in arrivals:
            cp.wait_recv()
        for cp in sends:
            cp.wait_send()
        for cp in local:
            cp.wait()

    outs = pl.pallas_call(
        body, name=name,
        out_shape=tuple(pltpu.HBM(a.shape, a.dtype) for a in list(srcs) + list(lands)),
        in_specs=[HBM_SPEC] * (2 * T) + [SEM_SPEC] * 3 + [ANY],
        out_specs=(HBM_SPEC,) * (2 * T),
        input_output_aliases={i: i for i in range(2 * T)},
        compiler_params=pltpu.CompilerParams(has_side_effects=DATAFLOW),
    )(*srcs, *lands, *sems, after)
    return outs[T:]


def _shard_window(kind, width):
    def win(ref, lin):
        if kind == "slot":
            return ref.at[lin]
        if kind == "col":
            return ref.at[:, pl.ds(pl.multiple_of(lin * width, LANES), width)]
        if kind == "row":
            return ref.at[pl.ds(pl.multiple_of(lin * width, 8), width), :]
        if kind == "lcol":
            return ref.at[:, :, pl.ds(pl.multiple_of(lin * width, LANES), width)]
        if kind == "lrow":
            return ref.at[:, pl.ds(pl.multiple_of(lin * width, 8), width), :]
        raise ValueError(kind)
    return win


def _shard_windows(kinds, shard_shapes):
    return [_shard_window(k, (ss[-1] if k in ("col", "lcol") else ss[-2])) for k, ss in zip(kinds, shard_shapes)]


def _allgather_start(shards, kinds, full_shapes, *, name, dep=None):
    wins = _shard_windows(kinds, [s.shape for s in shards])
    send_window = lambda t, ref, lin: ref
    recv_window = lambda t, ref, lin: wins[t](ref, lin)
    started = _exchange_start(shards, [jax.ShapeDtypeStruct(fs, s.dtype) for fs, s in zip(full_shapes, shards)],
                              send_window, recv_window, name=name + "_start", dep=dep)
    return started, lambda after: _exchange_wait(started, after, send_window, recv_window, name=name + "_wait")


def _scatter_start(fulls, kinds, shard_shapes, *, name):
    wins = _shard_windows(kinds, shard_shapes)
    send_window = lambda t, ref, lin: wins[t](ref, lin)
    recv_window = lambda t, ref, lin: ref.at[lin]
    started = _exchange_start(
        fulls, [jax.ShapeDtypeStruct((N_DEV,) + tuple(ss), f.dtype) for ss, f in zip(shard_shapes, fulls)],
        send_window, recv_window, name=name + "_start")
    return started, lambda after: _exchange_wait(started, after, send_window, recv_window, name=name + "_wait")


def _small_gather(pack, *, reduce, name):
    R = pack.shape[0]

    def body(p_ref, o_ref, *rest):
        if reduce:
            buf, send_sems, recv_sems = rest
        else:
            buf = o_ref
            send_sems, recv_sems = rest
        me = _my_position()
        me_lin = _linear(me)
        buf[me_lin] = p_ref[...]
        sends = []
        for k in range(1, N_DEV):
            peer = _peer(me, k)
            cp = pltpu.make_async_remote_copy(
                src_ref=p_ref, dst_ref=buf.at[me_lin],
                send_sem=send_sems.at[k - 1], recv_sem=recv_sems.at[k - 1],
                device_id=peer, device_id_type=MESH)
            cp.start()
            sends.append(cp)
        for k in range(1, N_DEV):
            peer = _peer(me, k)
            pltpu.make_async_remote_copy(
                src_ref=p_ref, dst_ref=buf.at[_linear(peer)],
                send_sem=send_sems.at[k - 1], recv_sem=recv_sems.at[k - 1],
                device_id=peer, device_id_type=MESH).wait_recv()
        for cp in sends:
            cp.wait_send()
        if reduce:
            acc = buf[0]
            for s in range(1, N_DEV):
                acc = acc + buf[s]
            o_ref[...] = acc

    scratch = [pltpu.SemaphoreType.DMA((N_DEV - 1,)), pltpu.SemaphoreType.DMA((N_DEV - 1,))]
    if reduce:
        scratch = [pltpu.VMEM((N_DEV, R, LANES), F32)] + scratch
        out_shape = jax.ShapeDtypeStruct((R, LANES), F32)
    else:
        out_shape = jax.ShapeDtypeStruct((N_DEV, R, LANES), F32)
    return pl.pallas_call(
        body, name=name, out_shape=out_shape,
        in_specs=[VMEM_SPEC], out_specs=VMEM_SPEC, scratch_shapes=scratch,
        compiler_params=pltpu.CompilerParams(has_side_effects=True, vmem_limit_bytes=VMEM_LIMIT),
    )(pack)


def _adamw_math(w, g, m, v):
    m = ADAM_B1 * m + (1.0 - ADAM_B1) * g
    v = ADAM_B2 * v + (1.0 - ADAM_B2) * jnp.square(g)
    m_hat = m / (1.0 - ADAM_B1 ** ADAM_STEP)
    v_hat = v / (1.0 - ADAM_B2 ** ADAM_STEP)
    delta = -ADAM_LR * (m_hat / (jnp.sqrt(v_hat) + ADAM_EPS) + ADAM_WD * w)
    return delta, m, v


def _adamw_from_partials(parts, w, m, v, *, name):
    R, C = w.shape
    rl = parts[0].shape[1]
    assert all(p.shape == (N_DEV, rl, C) for p in parts) and rl * len(parts) == R
    tr = _tile(rl, (256, 128, 64, 32, 16))
    per = rl // tr
    L = len(parts)

    def body(*refs):
        p_refs = refs[:L]
        w_ref, m_ref, v_ref, g_ref, d_ref, nm_ref, nv_ref = refs[L:]
        i = pl.program_id(0)
        for l in range(L):
            @pl.when((i >= l * per) & (i < (l + 1) * per))
            def _(l=l):
                p_ref = p_refs[l]
                g = p_ref[0].astype(F32)
                for s in range(1, N_DEV):
                    g = g + p_ref[s].astype(F32)
                d, nm, nv = _adamw_math(w_ref[...], g, m_ref[...], v_ref[...])
                g_ref[...] = g
                d_ref[...] = d
                nm_ref[...] = nm
                nv_ref[...] = nv

    blk = pl.BlockSpec((tr, C), lambda i: (i, 0))
    part_specs = [pl.BlockSpec((N_DEV, tr, C), lambda i, l=l: (0, jnp.clip(i - l * per, 0, per - 1), 0))
                  for l in range(L)]
    return pl.pallas_call(
        body, name=name,
        out_shape=(jax.ShapeDtypeStruct((R, C), F32),) * 4,
        grid=(R // tr,),
        in_specs=part_specs + [blk, blk, blk],
        out_specs=(blk,) * 4,
        compiler_params=_cparams("parallel"),
    )(*parts, w, m, v)


def _adamw_small(g, w, m, v, *, name):
    def body(g_ref, w_ref, m_ref, v_ref, d_ref, nm_ref, nv_ref):
        d, nm, nv = _adamw_math(w_ref[...], g_ref[...], m_ref[...], v_ref[...])
        d_ref[...] = d
        nm_ref[...] = nm
        nv_ref[...] = nv

    return pl.pallas_call(
        body, name=name,
        out_shape=(jax.ShapeDtypeStruct(g.shape, F32),) * 3,
        in_specs=[VMEM_SPEC] * 4, out_specs=(VMEM_SPEC,) * 3,
    )(g, w, m, v)


def _pack_rows(pieces):
    flat = jnp.concatenate([p.reshape(-1) for p in pieces])
    n = flat.shape[0]
    padded = -(-n // (8 * LANES)) * (8 * LANES)
    return jnp.pad(flat, (0, padded - n)).reshape(padded // LANES, LANES)


def _unpack_rows(pack, shapes):
    flat = pack.reshape(-1)
    out, pos = [], 0
    for s in shapes:
        n = int(np.prod(s))
        out.append(flat[pos:pos + n].reshape(s))
        pos += n
    return out


def kernel(x, rel_bias, ab_norm, ab_w_in, ab_conv_w, ab_conv_b, ab_ln_g, ab_ln_b, ab_w_out, sc_norm, sc_w_in, sc_conv_w, sc_w_out, mlp_norm, mlp_w_up, mlp_w_down, final_norm, loss_target, m_rel_bias, m_ab_norm, m_ab_w_in, m_ab_conv_w, m_ab_conv_b, m_ab_ln_g, m_ab_ln_b, m_ab_w_out, m_sc_norm, m_sc_w_in, m_sc_conv_w, m_sc_w_out, m_mlp_norm, m_mlp_w_up, m_mlp_w_down, m_final_norm, v_rel_bias, v_ab_norm, v_ab_w_in, v_ab_conv_w, v_ab_conv_b, v_ab_ln_g, v_ab_ln_b, v_ab_w_out, v_sc_norm, v_sc_w_in, v_sc_conv_w, v_sc_w_out, v_mlp_norm, v_mlp_w_up, v_mlp_w_down, v_final_norm):
    S, D = x.shape[1], x.shape[2]
    CA = ab_conv_b.shape[1]
    C2 = 2 * CA
    AB_IN = C2 + ATTN_IN
    me_lin = _linear(_my_position())
    xs = x.reshape(S, D)
    tgt = loss_target.reshape(S, D)

    cw_sh = ab_conv_w.shape[2]
    scn_sh = sc_norm.shape[1]
    scw_sh = sc_conv_w.shape[2]
    small_sh_shapes = [(CONV_A_WIDTH, cw_sh), (scn_sh,), (SC_CONV_WIDTH, scw_sh)]
    small_params = _small_gather(_pack_rows([ab_conv_w[0], sc_norm[0], sc_conv_w[0]]), reduce=False,
                                 name="allgather_small_params")
    w_in_sh = ab_w_in[0].astype(BF16)
    ag_ab, wait_ab = _allgather_start(
        [w_in_sh, ab_w_out[0].astype(BF16)], ["slot", "row"],
        [(N_DEV,) + w_in_sh.shape, (N_DEV * ab_w_out.shape[1], D)], name="allgather_ab", dep=small_params)
    ag_mlp, wait_mlp = _allgather_start(
        [mlp_w_up.astype(BF16), mlp_w_down.astype(BF16)], ["lcol", "lrow"],
        [(2, D, N_DEV * mlp_w_up.shape[2]), (2, N_DEV * mlp_w_down.shape[1], D)], name="allgather_mlp",
        dep=ag_ab[3])
    ag_sc, wait_sc = _allgather_start(
        [sc_w_in[0].astype(BF16), sc_w_out[0].astype(BF16)], ["col", "row"],
        [(D, N_DEV * sc_w_in.shape[2]), (N_DEV * sc_w_out.shape[1], D)], name="allgather_sc",
        dep=ag_mlp[3])

    per_dev = [_unpack_rows(small_params[s], small_sh_shapes) for s in range(N_DEV)]
    conv_w_full = jnp.concatenate([p[0] for p in per_dev], axis=1)
    sc_norm_full = jnp.concatenate([p[1] for p in per_dev], axis=0)[None]
    sc_conv_full = jnp.concatenate([p[2] for p in per_dev], axis=1)

    buckets = _bucket_tables()
    biases = _bias_tables(rel_bias, buckets, name="bias_tables")

    dils = [dil for _, dil in DILATED_GROUPS]
    n0_all = _rmsnorm_fwd(xs, ab_norm, name="norm_ab", dep=ag_sc[3], views=dils[1:])
    n0 = n0_all[0]
    w_in_g, w_out = wait_ab(n0)
    w_in = jnp.transpose(w_in_g, (1, 0, 2)).reshape(D, AB_IN)
    w_c = w_in[:, :C2]
    w_q = w_in[:, C2:]
    w_grp = [jnp.concatenate([w_q[:, t * N_GROUPS * ATTN_OUT + g * ATTN_OUT:][:, :ATTN_OUT] for t in range(3)], axis=1)
             for g in range(N_GROUPS)]
    uc = _mm_nn(n0, w_c, out_dtype=F32, name="mm_ab_in_conv")
    uqs = [_mm_nn(n0_all[g], w_grp[g], out_dtype=BF16, slabs=dils[g], name=f"mm_ab_in_qkv{g}")
           for g in range(N_GROUPS)]
    ya, hglu, ct = _conv_fwd(uc, conv_w_full, ab_conv_b, ab_ln_g, ab_ln_b, name="conv_fwd")
    outs, lses = zip(*[_attn_fwd(uqs[g], biases, g, dils[g], name=f"attn_fwd_{g}") for g in range(N_GROUPS)])
    cat, outf, lse = _attn_merge(outs, lses, ya, name="attn_merge")
    h1 = _mm_nn(cat, w_out, out_dtype=F32, residual=xs, name="mm_ab_out")
    n1 = _rmsnorm_fwd(h1, mlp_norm[0:1], name="norm_mlp0")
    w_up, w_dn = wait_mlp(n1)
    z0 = _mm_nn(n1, w_up[0], out_dtype=BF16, name="mm_up0")
    h2 = _mm_nn(z0, w_dn[0], out_dtype=F32, residual=h1, a_fn=_relu_sq, name="mm_down0")
    n2 = _rmsnorm_fwd(h2, sc_norm_full, name="norm_sc")
    w_sc_in, w_sc_out = wait_sc(n2)
    u3 = _mm_nn(n2, w_sc_in, out_dtype=BF16, name="mm_sc_in")
    ysc = _sc_fwd(u3, sc_conv_full, name="sc_fwd")
    h3 = _mm_nn(ysc, w_sc_out, out_dtype=F32, residual=h2, name="mm_sc_out")
    n3 = _rmsnorm_fwd(h3, mlp_norm[1:2], name="norm_mlp1")
    z1 = _mm_nn(n3, w_up[1], out_dtype=BF16, name="mm_up1")
    h4 = _mm_nn(z1, w_dn[1], out_dtype=F32, residual=h3, a_fn=_relu_sq, name="mm_down1")

    def dz_epilogue(acc, z):
        return acc * (2.0 * jnp.maximum(z.astype(F32), 0.0))

    dh4, dh4b, acc_final = _loss_bwd(h4, tgt, final_norm[None], name="loss_bwd")
    dz1 = _mm_nt([(dh4b, w_dn[1])], out_dtype=BF16, epilogue=dz_epilogue, extra=z1, name="mm_d_down1")
    g_dn1 = _mm_tn(z1, dh4b, a_fn=_relu_sq, name="mm_gw_down1")
    g_up1 = _mm_tn(n3, dz1, name="mm_gw_up1")
    rs_mlp1, wait_rs_mlp1 = _scatter_start([g_up1, g_dn1], ["col", "row"],
                                           [mlp_w_up.shape[1:], mlp_w_down.shape[1:]], name="scatter_mlp1")
    dh3, dh3b, acc_mlp1 = _mm_nt_rms_bwd(dz1, w_up[1], h3, mlp_norm[1:2], dh4, name="mm_d_up1_norm_bwd",
                                         dep=rs_mlp1[3])

    dysc = _mm_nt([(dh3b, w_sc_out)], out_dtype=F32, name="mm_d_sc_out")
    g_sc_out = _mm_tn(ysc, dh3b, name="mm_gw_sc_out")
    du3, acc_scw = _sc_bwd(u3, dysc, sc_conv_full, name="sc_bwd")
    g_sc_in = _mm_tn(n2, du3, name="mm_gw_sc_in")
    rs_sc, wait_rs_sc = _scatter_start([g_sc_in, g_sc_out], ["col", "row"],
                                       [sc_w_in.shape[1:], sc_w_out.shape[1:]], name="scatter_sc")
    dh2, dh2b, acc_sc = _mm_nt_rms_bwd(du3, w_sc_in, h2, sc_norm_full, dh3, name="mm_d_sc_in_norm_bwd",
                                       dep=rs_sc[3])

    dz0 = _mm_nt([(dh2b, w_dn[0])], out_dtype=BF16, epilogue=dz_epilogue, extra=z0, name="mm_d_down0")
    g_dn0 = _mm_tn(z0, dh2b, a_fn=_relu_sq, name="mm_gw_down0")
    g_up0 = _mm_tn(n1, dz0, name="mm_gw_up0")
    rs_mlp0, wait_rs_mlp0 = _scatter_start([g_up0, g_dn0], ["col", "row"],
                                           [mlp_w_up.shape[1:], mlp_w_down.shape[1:]], name="scatter_mlp0")
    dh1, dh1b, acc_mlp0 = _mm_nt_rms_bwd(dz0, w_up[0], h1, mlp_norm[0:1], dh2, name="mm_d_up0_norm_bwd",
                                         dep=rs_mlp0[3])

    dcat = _mm_nt([(dh1b, w_out)], out_dtype=F32, name="mm_d_ab_out")
    g_ab_out = _mm_tn(cat, dh1b, name="mm_gw_ab_out")
    prep = _attn_prep(dcat, outf, lse, name="attn_prep")
    dqkv, dbs = zip(*[_attn_bwd(uqs[g], prep[g][0], prep[g][2], prep[g][1], biases, g, dils[g],
                                name=f"attn_bwd_{g}") for g in range(N_GROUPS)])
    drel = _relbias_reduce(dbs, buckets, name="relbias_reduce")
    dc, acc_conv = _conv_bwd_ln(ct, dcat, hglu, ab_ln_g, ab_ln_b, name="conv_bwd_ln")
    duc = _conv_bwd_in(dc, uc, conv_w_full, name="conv_bwd_in")
    g_wc = _mm_tn(n0, duc, name="mm_gw_ab_in_conv")
    g_wgrp = [_mm_tn(n0_all[g], dqkv[g], slabs=dils[g], name=f"mm_gw_ab_in_qkv{g}") for g in range(N_GROUPS)]
    g_wq = jnp.concatenate([g_wgrp[g][:, t * ATTN_OUT:(t + 1) * ATTN_OUT]
                            for t in range(3) for g in range(N_GROUPS)], axis=1)
    g_w_in = jnp.concatenate([g_wc, g_wq], axis=1).reshape(D, N_DEV, AB_IN // N_DEV).transpose(1, 0, 2)
    rs_ab, wait_rs_ab = _scatter_start([g_w_in, g_ab_out], ["slot", "row"],
                                       [w_in_sh.shape, ab_w_out.shape[1:]], name="scatter_ab")
    dn0 = _mm_nt([(duc, w_c), (dqkv[0], w_grp[0])], out_dtype=F32, name="mm_d_ab_in", dep=rs_ab[3])
    dn0_views = [(_mm_nt([(dqkv[g], w_grp[g])], out_dtype=F32, slabs=dils[g], name=f"mm_d_ab_in_qkv{g}"), dils[g])
                 for g in range(1, N_GROUPS)]
    grad_x, grad_xb, acc_ab = _rms_bwd(xs, ab_norm, dn0, dh1, name="norm_ab_bwd", dn_views=dn0_views)

    small_full = [drel[:, :, :NUM_BUCKETS].transpose(2, 0, 1).reshape(NUM_BUCKETS, N_GROUPS * HEADS_PER_GROUP),
                  acc_ab[0], acc_conv[0:CONV_A_WIDTH], acc_conv[32],
                  acc_conv[33], acc_conv[34], acc_sc[0], acc_scw[0:SC_CONV_WIDTH],
                  jnp.stack([acc_mlp0[0], acc_mlp1[0]]), acc_final[0], acc_final[1]]
    small_full_shapes = [p.shape for p in small_full]
    summed = _unpack_rows(_small_gather(_pack_rows(small_full), reduce=True, name="allreduce_small"),
                          small_full_shapes)
    (s_rel, s_abn, s_cw, s_cb, s_lg, s_lb, s_scn, s_scw, s_mlpn, s_fn, s_err) = summed
    loss = (0.5 / D) * jnp.sum(s_err)
    small_grads = {
        "rel_bias": s_rel, "ab_norm": s_abn[None],
        "ab_conv_w": lax.dynamic_slice_in_dim(s_cw, me_lin * cw_sh, cw_sh, axis=1)[None],
        "ab_conv_b": s_cb[None], "ab_ln_g": s_lg[None], "ab_ln_b": s_lb[None],
        "sc_norm": lax.dynamic_slice_in_dim(s_scn, me_lin * scn_sh, scn_sh, axis=0)[None],
        "sc_conv_w": lax.dynamic_slice_in_dim(s_scw, me_lin * scw_sh, scw_sh, axis=1)[None],
        "mlp_norm": s_mlpn, "final_norm": s_fn,
    }
    small_w = {"rel_bias": (rel_bias, m_rel_bias, v_rel_bias), "ab_norm": (ab_norm, m_ab_norm, v_ab_norm),
               "ab_conv_w": (ab_conv_w, m_ab_conv_w, v_ab_conv_w), "ab_conv_b": (ab_conv_b, m_ab_conv_b, v_ab_conv_b),
               "ab_ln_g": (ab_ln_g, m_ab_ln_g, v_ab_ln_g), "ab_ln_b": (ab_ln_b, m_ab_ln_b, v_ab_ln_b),
               "sc_norm": (sc_norm, m_sc_norm, v_sc_norm), "sc_conv_w": (sc_conv_w, m_sc_conv_w, v_sc_conv_w),
               "mlp_norm": (mlp_norm, m_mlp_norm, v_mlp_norm), "final_norm": (final_norm, m_final_norm, v_final_norm)}
    small_names = list(small_grads)
    small_shapes = [small_grads[n].shape for n in small_names]
    d_pack, m_pack, v_pack = _adamw_small(
        _pack_rows([small_grads[n] for n in small_names]), _pack_rows([small_w[n][0] for n in small_names]),
        _pack_rows([small_w[n][1] for n in small_names]), _pack_rows([small_w[n][2] for n in small_names]),
        name="adamw_small")
    small = {n: (small_grads[n], d, nm_, nv_) for n, d, nm_, nv_ in zip(
        small_names, _unpack_rows(d_pack, small_shapes), _unpack_rows(m_pack, small_shapes),
        _unpack_rows(v_pack, small_shapes))}

    p_up1, p_dn1 = wait_rs_mlp1(grad_xb)
    p_sc_in, p_sc_out = wait_rs_sc(grad_xb)
    p_up0, p_dn0 = wait_rs_mlp0(grad_xb)
    p_w_in, p_ab_out = wait_rs_ab(grad_xb)
    big = {}
    for nm, parts, w, m, v in (("ab_w_in", [p_w_in], ab_w_in, m_ab_w_in, v_ab_w_in),
                               ("ab_w_out", [p_ab_out], ab_w_out, m_ab_w_out, v_ab_w_out),
                               ("sc_w_in", [p_sc_in], sc_w_in, m_sc_w_in, v_sc_w_in),
                               ("sc_w_out", [p_sc_out], sc_w_out, m_sc_w_out, v_sc_w_out),
                               ("mlp_w_up", [p_up0, p_up1], mlp_w_up, m_mlp_w_up, v_mlp_w_up),
                               ("mlp_w_down", [p_dn0, p_dn1], mlp_w_down, m_mlp_w_down, v_mlp_w_down)):
        C = w.shape[-1]
        res = _adamw_from_partials(parts, w.reshape(-1, C), m.reshape(-1, C), v.reshape(-1, C), name="adamw_" + nm)
        big[nm] = tuple(r.reshape(w.shape) for r in res)

    order = ["rel_bias", "ab_norm", "ab_w_in", "ab_conv_w", "ab_conv_b", "ab_ln_g", "ab_ln_b", "ab_w_out",
             "sc_norm", "sc_w_in", "sc_conv_w", "sc_w_out", "mlp_norm", "mlp_w_up", "mlp_w_down", "final_norm"]
    allres = {**big, **small}
    return (loss, grad_x.reshape(x.shape),
            *[allres[n][0] for n in order], *[allres[n][1] for n in order],
            *[allres[n][2] for n in order], *[allres[n][3] for n in order])
```

```python
import functools
import math

import numpy as np
import jax
import jax.numpy as jnp
from jax import lax
from jax.experimental import pallas as pl
from jax.experimental.pallas import tpu as pltpu

F32 = jnp.float32
BF16 = jnp.bfloat16

HEAD_DIM = 64
HEADS_PER_GROUP = 8
DILATED_GROUPS = ((128, 1), (512, 4), (2048, 16))
N_GROUPS = 3
ATTN_OUT = HEADS_PER_GROUP * HEAD_DIM
ATTN_IN = 3 * N_GROUPS * ATTN_OUT
GROUP_QKV = 3 * ATTN_OUT
ATTN_BLK = 128
CONV_A_WIDTH = 31
SC_CONV_WIDTH = 3
NUM_BUCKETS = 32
REL_MAX_DISTANCE = 2048
RMS_EPS = 1e-6
LN_EPS = 1e-5
NEG_INF = -1e30
ADAM_LR = 0.001
ADAM_B1 = 0.9
ADAM_B2 = 0.999
ADAM_EPS = 1e-08
ADAM_WD = 0.01
ADAM_STEP = 10

N_DEV = 8
HALO = 32
LANES = 128
VMEM_LIMIT = 56 * 1024 * 1024
MESH = pl.DeviceIdType.MESH
ANY = pl.BlockSpec(memory_space=pl.ANY)
VMEM_SPEC = pl.BlockSpec(memory_space=pltpu.VMEM)


def _tile(n, prefs):
    for t in prefs:
        if n % t == 0:
            return t
    return n


def _cparams(*sem):
    return pltpu.CompilerParams(dimension_semantics=sem, vmem_limit_bytes=VMEM_LIMIT)


def _relu_sq(z):
    return jnp.square(jnp.maximum(z, 0))


def _dot_nt(a, b):
    return lax.dot_general(a, b, (((1,), (1,)), ((), ())), preferred_element_type=F32)


def _dot_tn(a, b):
    return lax.dot_general(a, b, (((0,), (0,)), ((), ())), preferred_element_type=F32)


def _weight(b):
    if not isinstance(b, tuple):
        return b, b.shape, pl.BlockSpec
    arr, layer = b

    def spec(block, index_map):
        return pl.BlockSpec((None,) + tuple(block), lambda *g: (layer,) + tuple(index_map(*g)))

    return arr, arr.shape[1:], spec


def _mm_nn(a, b, *, out_dtype, name, residual=None, a_fn=None, slabs=1):
    M, K = a.shape
    K //= slabs
    b, (_, N), b_spec = _weight(b)
    tm = _tile(M, (2048, 1024, 512, 256))
    tn = _tile(N, (512, 384, 256, 128))
    tk = _tile(K, (1024, 512, 256, 128))
    nk = K // tk
    nj = N // tn
    has_res = residual is not None

    def body(*refs):
        if has_res:
            a_ref, b_ref, r_ref, o_ref = refs[:4]
        else:
            a_ref, b_ref, o_ref = refs[:3]
        av = a_ref[...]
        if a_fn is not None:
            av = a_fn(av)
        part = jnp.dot(av, b_ref[...], preferred_element_type=F32)

        def finish(acc):
            if has_res:
                acc = acc + r_ref[...]
            o_ref[...] = acc.astype(o_ref.dtype)

        if nk == 1:
            finish(part)
        else:
            acc_ref = refs[-1]
            k = pl.program_id(2)

            @pl.when(k == 0)
            def _():
                acc_ref[...] = part

            @pl.when((k > 0) & (k < nk - 1))
            def _():
                acc_ref[...] += part

            @pl.when(k == nk - 1)
            def _():
                finish(acc_ref[...] + part)

    in_specs = [pl.BlockSpec((tm, tk), lambda i, j, k: (i, (j // nj) * nk + k)),
                b_spec((tk, tn), lambda i, j, k: (k, j % nj))]
    args = [a, b]
    if has_res:
        in_specs.append(pl.BlockSpec((tm, tn), lambda i, j, k: (i, j)))
        args.append(residual)
    return pl.pallas_call(
        body, name=name,
        out_shape=jax.ShapeDtypeStruct((M, slabs * N), out_dtype),
        grid=(M // tm, slabs * nj, nk),
        in_specs=in_specs,
        out_specs=pl.BlockSpec((tm, tn), lambda i, j, k: (i, j)),
        scratch_shapes=[pltpu.VMEM((tm, tn), F32)] if nk > 1 else [],
        compiler_params=_cparams("parallel", "parallel", "arbitrary"),
    )(*args)


def _mm_nt(pairs, *, out_dtype, name, epilogue=None, extra=None, dep=None, slabs=1):
    assert slabs == 1 or len(pairs) == 1
    M = pairs[0][0].shape[0]
    weights = [_weight(p[1]) for p in pairs]
    Ko = weights[0][1][0]
    tm = _tile(M, (1024, 512, 256))
    to = _tile(Ko, (1024, 512, 256, 128))
    njo = Ko // to
    tks = [_tile(p[0].shape[1] // slabs, (1024, 768, 512, 256, 128)) for p in pairs]
    steps = [p[0].shape[1] // slabs // tk for p, tk in zip(pairs, tks)]
    offs = [sum(steps[:i]) for i in range(len(pairs))]
    nk = sum(steps)
    npair = len(pairs)
    has_extra = extra is not None

    def body(*refs):
        ab = refs[:2 * npair]
        pos = 2 * npair
        e_ref = None
        if has_extra:
            e_ref = refs[pos]
            pos += 1
        if dep is not None:
            pos += 1
        o_ref = refs[pos]
        acc_ref = refs[pos + 1]
        k = pl.program_id(2)

        @pl.when(k == 0)
        def _():
            acc_ref[...] = jnp.zeros_like(acc_ref)

        for p in range(npair):
            @pl.when((k >= offs[p]) & (k < offs[p] + steps[p]))
            def _(p=p):
                acc_ref[...] += _dot_nt(ab[2 * p][...], ab[2 * p + 1][...])

        @pl.when(k == nk - 1)
        def _():
            acc = acc_ref[...]
            if epilogue is not None:
                acc = epilogue(acc, e_ref[...] if has_extra else None)
            o_ref[...] = acc.astype(o_ref.dtype)

    in_specs, args = [], []
    for p, (a, b) in enumerate(pairs):
        def kidx(k, p=p):
            return jnp.clip(k - offs[p], 0, steps[p] - 1)
        in_specs.append(pl.BlockSpec((tm, tks[p]),
                                     lambda i, j, k, kidx=kidx, p=p: (i, (j // njo) * steps[p] + kidx(k))))
        in_specs.append(weights[p][2]((to, tks[p]), lambda i, j, k, kidx=kidx: (j % njo, kidx(k))))
        args += [a, weights[p][0]]
    if has_extra:
        in_specs.append(pl.BlockSpec((tm, to), lambda i, j, k: (i, j)))
        args.append(extra)
    if dep is not None:
        in_specs.append(ANY)
        args.append(dep)
    return pl.pallas_call(
        body, name=name,
        out_shape=jax.ShapeDtypeStruct((M, slabs * Ko), out_dtype),
        grid=(M // tm, slabs * njo, nk),
        in_specs=in_specs,
        out_specs=pl.BlockSpec((tm, to), lambda i, j, k: (i, j)),
        scratch_shapes=[pltpu.VMEM((tm, to), F32)],
        compiler_params=_cparams("parallel", "parallel", "arbitrary"),
    )(*args)


def _mm_tn(a, b, *, name, a_fn=None, slabs=1):
    M, K = a.shape
    K //= slabs
    N = b.shape[1] // slabs
    tm = _tile(M, (2048, 1024, 512, 256))
    tk = _tile(K, (1024, 768, 512, 384, 256, 128))
    tn = _tile(N, (1024, 768, 512, 384, 256, 128))
    nmi = M // tm
    nm = slabs * nmi
    nki, nnj = K // tk, N // tn

    def body(a_ref, b_ref, o_ref, acc_ref):
        m = pl.program_id(2)
        av = a_ref[...]
        if a_fn is not None:
            av = a_fn(av)
        part = _dot_tn(av, b_ref[...])
        if nm == 1:
            o_ref[...] = part.astype(o_ref.dtype)
            return

        @pl.when(m == 0)
        def _():
            acc_ref[...] = part

        @pl.when((m > 0) & (m < nm - 1))
        def _():
            acc_ref[...] += part

        @pl.when(m == nm - 1)
        def _():
            o_ref[...] = (acc_ref[...] + part).astype(o_ref.dtype)

    return pl.pallas_call(
        body, name=name,
        out_shape=jax.ShapeDtypeStruct((K, N), BF16),
        grid=(K // tk, N // tn, nm),
        in_specs=[pl.BlockSpec((tm, tk), lambda i, j, m: (m % nmi, (m // nmi) * nki + i)),
                  pl.BlockSpec((tm, tn), lambda i, j, m: (m % nmi, (m // nmi) * nnj + j))],
        out_specs=pl.BlockSpec((tk, tn), lambda i, j, m: (i, j)),
        scratch_shapes=[pltpu.VMEM((tk, tn), F32)],
        compiler_params=_cparams("parallel", "parallel", "arbitrary"),
    )(a, b)


def _rmsnorm_fwd(h, g, *, name, dep=None, views=()):
    S, D = h.shape
    tm = _tile(S, (512, 256))
    nv = len(views)

    def body(h_ref, g_ref, *rest):
        n_out = 1 + nv
        outs = rest[len(rest) - n_out - (1 if nv else 0):len(rest) - (1 if nv else 0)]
        x = h_ref[...]
        r = lax.rsqrt(jnp.mean(x * x, axis=-1, keepdims=True) + RMS_EPS)
        y = x * r * g_ref[...]
        outs[0][...] = y.astype(BF16)
        if nv:
            scr = rest[-1]
            _to_chunks(scr, y)
            for v_ref, d in zip(outs[1:], views):
                _slabs_from_chunks(v_ref, scr, d, BF16)

    res = pl.pallas_call(
        body, name=name,
        out_shape=(jax.ShapeDtypeStruct((S, D), BF16),)
        + tuple(jax.ShapeDtypeStruct((S // d, d * D), BF16) for d in views),
        grid=(S // tm,),
        in_specs=[pl.BlockSpec((tm, D), lambda i: (i, 0)), pl.BlockSpec((1, D), lambda i: (0, 0))]
        + ([ANY] if dep is not None else []),
        out_specs=(pl.BlockSpec((tm, D), lambda i: (i, 0)),)
        + tuple(pl.BlockSpec((tm // d, d * D), lambda i: (i, 0)) for d in views),
        scratch_shapes=[_chunk_scratch(tm, D)] if nv else [],
        compiler_params=_cparams("parallel"),
    )(h, g, *([dep] if dep is not None else []))
    return res if nv else res[0]


def _rms_bwd_rows(x, g, dy):
    r = lax.rsqrt(jnp.mean(x * x, axis=-1, keepdims=True) + RMS_EPS)
    xh = x * r
    gy = dy * g
    dx = r * (gy - xh * jnp.mean(xh * gy, axis=-1, keepdims=True))
    return dx, dy * xh


def _rms_bwd(x, g, dn, dres, *, name, dn_views=()):
    S, D = x.shape
    tm = _tile(S, (256,))
    nv = len(dn_views)

    def body(x_ref, g_ref, dn_ref, dr_ref, *rest):
        v_refs = rest[:nv]
        dx_ref, dxb_ref, dg_ref = rest[nv:nv + 3]
        scr = rest[nv + 3:]
        i = pl.program_id(0)
        dn = dn_ref[...]
        for v_ref, s_ref, (_, d) in zip(v_refs, scr, dn_views):
            _chunks_from_slabs(s_ref, v_ref, d)
            dn = dn + _from_chunks(s_ref)
        dx, dgx = _rms_bwd_rows(x_ref[...], g_ref[...], dn)
        tot = dr_ref[...] + dx
        dx_ref[...] = tot
        dxb_ref[...] = tot.astype(BF16)

        @pl.when(i == 0)
        def _():
            dg_ref[...] = jnp.zeros_like(dg_ref)

        dg_ref[0:1, :] += jnp.sum(dgx, axis=0, keepdims=True)

    row = pl.BlockSpec((tm, D), lambda i: (i, 0))
    return pl.pallas_call(
        body, name=name,
        out_shape=(jax.ShapeDtypeStruct((S, D), F32), jax.ShapeDtypeStruct((S, D), BF16),
                   jax.ShapeDtypeStruct((8, D), F32)),
        grid=(S // tm,),
        in_specs=[row, pl.BlockSpec((1, D), lambda i: (0, 0)), row, row]
        + [pl.BlockSpec((tm // d, d * D), lambda i: (i, 0)) for _, d in dn_views],
        out_specs=(row, row, pl.BlockSpec((8, D), lambda i: (0, 0))),
        scratch_shapes=[_chunk_scratch(tm, D)] * nv,
        compiler_params=_cparams("arbitrary"),
    )(x, g, dn, dres, *[a for a, _ in dn_views])


def _mm_nt_rms_bwd(a, b, x, g, dres, *, name, dep=None):
    M, N = a.shape
    b, (D, _), b_spec = _weight(b)
    tm = _tile(M, (1024, 512, 256))
    tk = _tile(N, (1024, 512, 256, 128))
    nk = N // tk

    def body(a_ref, b_ref, x_ref, g_ref, dr_ref, *rest):
        dx_ref, dxb_ref, dg_ref, acc_ref = rest[-4:]
        i = pl.program_id(0)
        k = pl.program_id(1)
        part = _dot_nt(a_ref[...], b_ref[...])

        @pl.when((i == 0) & (k == 0))
        def _():
            dg_ref[...] = jnp.zeros_like(dg_ref)

        @pl.when(k == 0)
        def _():
            acc_ref[...] = part

        @pl.when((k > 0) & (k < nk - 1))
        def _():
            acc_ref[...] += part

        @pl.when(k == nk - 1)
        def _():
            dn = part if nk == 1 else acc_ref[...] + part
            dx, dgx = _rms_bwd_rows(x_ref[...], g_ref[...], dn)
            tot = dr_ref[...] + dx
            dx_ref[...] = tot
            dxb_ref[...] = tot.astype(BF16)
            dg_ref[0:1, :] += jnp.sum(dgx, axis=0, keepdims=True)

    row = pl.BlockSpec((tm, D), lambda i, k: (i, 0))
    in_specs = [pl.BlockSpec((tm, tk), lambda i, k: (i, k)), b_spec((D, tk), lambda i, k: (0, k)),
                row, pl.BlockSpec((1, D), lambda i, k: (0, 0)), row]
    args = [a, b, x, g, dres]
    if dep is not None:
        in_specs.append(ANY)
        args.append(dep)
    return pl.pallas_call(
        body, name=name,
        out_shape=(jax.ShapeDtypeStruct((M, D), F32), jax.ShapeDtypeStruct((M, D), BF16),
                   jax.ShapeDtypeStruct((8, D), F32)),
        grid=(M // tm, nk),
        in_specs=in_specs,
        out_specs=(row, row, pl.BlockSpec((8, D), lambda i, k: (0, 0))),
        scratch_shapes=[pltpu.VMEM((tm, D), F32)],
        compiler_params=_cparams("arbitrary", "arbitrary"),
    )(*args)


def _loss_bwd(h, target, g, *, name):
    S, D = h.shape
    tm = _tile(S, (256,))

    def body(h_ref, t_ref, g_ref, dx_ref, dxb_ref, acc_ref):
        i = pl.program_id(0)
        x = h_ref[...]
        gv = g_ref[...]
        r = lax.rsqrt(jnp.mean(x * x, axis=-1, keepdims=True) + RMS_EPS)
        err = x * r * gv - t_ref[...]
        dx, dgx = _rms_bwd_rows(x, gv, err * (1.0 / D))
        dx_ref[...] = dx
        dxb_ref[...] = dx.astype(BF16)

        @pl.when(i == 0)
        def _():
            acc_ref[...] = jnp.zeros_like(acc_ref)

        acc_ref[0:1, :] += jnp.sum(dgx, axis=0, keepdims=True)
        acc_ref[1:2, :] += jnp.sum(err * err, axis=0, keepdims=True)

    row = pl.BlockSpec((tm, D), lambda i: (i, 0))
    return pl.pallas_call(
        body, name=name,
        out_shape=(jax.ShapeDtypeStruct((S, D), F32), jax.ShapeDtypeStruct((S, D), BF16),
                   jax.ShapeDtypeStruct((8, D), F32)),
        grid=(S // tm,),
        in_specs=[row, row, pl.BlockSpec((1, D), lambda i: (0, 0))],
        out_specs=(row, row, pl.BlockSpec((8, D), lambda i: (0, 0))),
        compiler_params=_cparams("arbitrary"),
    )(h, target, g)


SUBLANES = 8
CONV_ROWS = 64


def _build_shifted(ext_ref, rot_ref, ts):
    rows = ts + HALO - SUBLANES
    for j in range(1, SUBLANES):
        rot_ref[j, 0:rows, :] = ext_ref[j:j + rows, :]


def _shifted(ext_ref, rot_ref, off, r0, nrows, cs):
    q, j = divmod(off, SUBLANES)
    start = SUBLANES * q + r0
    if j == 0:
        return ext_ref[start:start + nrows, cs]
    return rot_ref[j, start:start + nrows, cs]


def _conv_fwd(uc, conv_w, conv_b, ln_g, ln_b, *, name):
    S, C2 = uc.shape
    C = C2 // 2
    ts = _tile(S, (256,))
    per = ts // HALO

    def body(cur_ref, halo_ref, w_ref, b_ref, g_ref, beta_ref, ya_ref, h_ref, ct_ref, ext_ref, rot_ref):
        i = pl.program_id(0)
        hh = halo_ref[:, 0:C] * jax.nn.sigmoid(halo_ref[:, C:C2])
        ext_ref[0:HALO, :] = jnp.where(i == 0, 0.0, hh)
        hc = cur_ref[:, 0:C] * jax.nn.sigmoid(cur_ref[:, C:C2])
        ext_ref[HALO:HALO + ts, :] = hc
        h_ref[...] = hc
        _build_shifted(ext_ref, rot_ref, ts)
        for c0 in range(0, C, LANES):
            cs = slice(c0, c0 + LANES)
            for r0 in range(0, ts, CONV_ROWS):
                acc = jnp.zeros((CONV_ROWS, LANES), F32)
                for k in range(CONV_A_WIDTH):
                    acc = acc + w_ref[k:k + 1, cs] * _shifted(ext_ref, rot_ref, k + 2, r0, CONV_ROWS, cs)
                ct_ref[r0:r0 + CONV_ROWS, cs] = acc + b_ref[:, cs]
        ct = ct_ref[...]
        mu = jnp.mean(ct, axis=-1, keepdims=True)
        xc = ct - mu
        var = jnp.mean(xc * xc, axis=-1, keepdims=True)
        l = xc * lax.rsqrt(var + LN_EPS) * g_ref[...] + beta_ref[...]
        ya_ref[...] = (l * jax.nn.sigmoid(l)).astype(ya_ref.dtype)

    vec = pl.BlockSpec((1, C), lambda i: (0, 0))
    row = pl.BlockSpec((ts, C), lambda i: (i, 0))
    return pl.pallas_call(
        body, name=name,
        out_shape=(jax.ShapeDtypeStruct((S, C), BF16), jax.ShapeDtypeStruct((S, C), F32),
                   jax.ShapeDtypeStruct((S, C), F32)),
        grid=(S // ts,),
        in_specs=[pl.BlockSpec((ts, C2), lambda i: (i, 0)),
                  pl.BlockSpec((HALO, C2), lambda i: (jnp.maximum(i * per - 1, 0), 0)),
                  pl.BlockSpec((CONV_A_WIDTH, C), lambda i: (0, 0)), vec, vec, vec],
        out_specs=(row, row, row),
        scratch_shapes=[pltpu.VMEM((HALO + ts, C), F32), pltpu.VMEM((8, HALO + ts, C), F32)],
        compiler_params=_cparams("parallel"),
    )(uc, uc, conv_w, conv_b, ln_g, ln_b)


CONV_ACC_ROWS = 40


def _conv_bwd_ln(ct, dcat, hglu, ln_g, ln_b, *, name):
    S, C = ct.shape
    CW = dcat.shape[1]
    ts = _tile(S, (256,))
    per = ts // HALO

    def body(ct_ref, dcat_ref, hc_ref, hh_ref, g_ref, beta_ref, dc_ref, acc_ref, ext_ref, rot_ref):
        i = pl.program_id(0)
        ct = ct_ref[...]
        gv = g_ref[...]
        mu = jnp.mean(ct, axis=-1, keepdims=True)
        xc = ct - mu
        rstd = lax.rsqrt(jnp.mean(xc * xc, axis=-1, keepdims=True) + LN_EPS)
        xh = xc * rstd
        l = xh * gv + beta_ref[...]
        sg = jax.nn.sigmoid(l)
        dl = dcat_ref[:, 0:C] * (sg * (1.0 + l * (1.0 - sg)))
        dxh = dl * gv
        dc = rstd * (dxh - jnp.mean(dxh, axis=-1, keepdims=True)
                     - xh * jnp.mean(dxh * xh, axis=-1, keepdims=True))
        dc_ref[...] = dc

        @pl.when(i == 0)
        def _():
            acc_ref[...] = jnp.zeros_like(acc_ref)

        acc_ref[32:33, :] += jnp.sum(dc, axis=0, keepdims=True)
        acc_ref[33:34, :] += jnp.sum(dl * xh, axis=0, keepdims=True)
        acc_ref[34:35, :] += jnp.sum(dl, axis=0, keepdims=True)
        ext_ref[0:HALO, :] = jnp.where(i == 0, 0.0, hh_ref[...])
        ext_ref[HALO:HALO + ts, :] = hc_ref[...]
        _build_shifted(ext_ref, rot_ref, ts)
        for c0 in range(0, C, LANES):
            cs = slice(c0, c0 + LANES)
            dcc = dc_ref[:, cs]
            for k in range(CONV_A_WIDTH):
                acc_ref[k:k + 1, cs] += jnp.sum(dcc * _shifted(ext_ref, rot_ref, k + 2, 0, ts, cs),
                                                axis=0, keepdims=True)

    vec = pl.BlockSpec((1, C), lambda i: (0, 0))
    row = pl.BlockSpec((ts, C), lambda i: (i, 0))
    return pl.pallas_call(
        body, name=name,
        out_shape=(jax.ShapeDtypeStruct((S, C), F32), jax.ShapeDtypeStruct((CONV_ACC_ROWS, C), F32)),
        grid=(S // ts,),
        in_specs=[row, pl.BlockSpec((ts, CW), lambda i: (i, 0)), row,
                  pl.BlockSpec((HALO, C), lambda i: (jnp.maximum(i * per - 1, 0), 0)), vec, vec],
        out_specs=(row, pl.BlockSpec((CONV_ACC_ROWS, C), lambda i: (0, 0))),
        scratch_shapes=[pltpu.VMEM((HALO + ts, C), F32), pltpu.VMEM((8, HALO + ts, C), F32)],
        compiler_params=_cparams("arbitrary"),
    )(ct, dcat, hglu, hglu, ln_g, ln_b)


def _conv_bwd_in(dc, uc, conv_w, *, name):
    S, C = dc.shape
    C2 = 2 * C
    ts = _tile(S, (256,))
    per = ts // HALO
    nt = S // ts

    def body(dc_ref, dn_ref, uc_ref, w_ref, du_ref, ext_ref, rot_ref):
        i = pl.program_id(0)
        ext_ref[0:ts, :] = dc_ref[...]
        ext_ref[ts:ts + HALO, :] = jnp.where(i == nt - 1, 0.0, dn_ref[...])
        _build_shifted(ext_ref, rot_ref, ts)
        for c0 in range(0, C, LANES):
            cs = slice(c0, c0 + LANES)
            gs = slice(C + c0, C + c0 + LANES)
            for r0 in range(0, ts, CONV_ROWS):
                rs = slice(r0, r0 + CONV_ROWS)
                acc = jnp.zeros((CONV_ROWS, LANES), F32)
                for k in range(CONV_A_WIDTH):
                    acc = acc + w_ref[k:k + 1, cs] * _shifted(ext_ref, rot_ref, 30 - k, r0, CONV_ROWS, cs)
                sg = jax.nn.sigmoid(uc_ref[rs, gs])
                du_ref[rs, cs] = (acc * sg).astype(du_ref.dtype)
                du_ref[rs, gs] = (acc * uc_ref[rs, cs] * sg * (1.0 - sg)).astype(du_ref.dtype)

    return pl.pallas_call(
        body, name=name,
        out_shape=jax.ShapeDtypeStruct((S, C2), BF16),
        grid=(nt,),
        in_specs=[pl.BlockSpec((ts, C), lambda i: (i, 0)),
                  pl.BlockSpec((HALO, C), lambda i: (jnp.minimum((i + 1) * per, S // HALO - 1), 0)),
                  pl.BlockSpec((ts, C2), lambda i: (i, 0)),
                  pl.BlockSpec((CONV_A_WIDTH, C), lambda i: (0, 0))],
        out_specs=pl.BlockSpec((ts, C2), lambda i: (i, 0)),
        scratch_shapes=[pltpu.VMEM((ts + HALO, C), F32), pltpu.VMEM((8, ts + HALO, C), F32)],
        compiler_params=_cparams("parallel"),
    )(dc, dc, uc, conv_w)


def _sc_fwd(u3, conv_w, *, name):
    S, W3 = u3.shape
    W = W3 // 3
    ts = _tile(S, (256,))
    per = ts // HALO

    def body(cur_ref, halo_ref, w_ref, y_ref, ext_ref):
        i = pl.program_id(0)
        cvh = halo_ref[:, W:2 * W].astype(F32) * halo_ref[:, 2 * W:W3].astype(F32)
        ext_ref[0:HALO, :] = jnp.where(i == 0, 0.0, cvh)
        ext_ref[HALO:HALO + ts, :] = cur_ref[:, W:2 * W].astype(F32) * cur_ref[:, 2 * W:W3].astype(F32)
        k = (w_ref[0:1, :] * ext_ref[HALO - 2:HALO - 2 + ts, :]
             + w_ref[1:2, :] * ext_ref[HALO - 1:HALO - 1 + ts, :]
             + w_ref[2:3, :] * ext_ref[HALO:HALO + ts, :])
        y_ref[...] = (cur_ref[:, 0:W].astype(F32) * k).astype(y_ref.dtype)

    return pl.pallas_call(
        body, name=name,
        out_shape=jax.ShapeDtypeStruct((S, W), BF16),
        grid=(S // ts,),
        in_specs=[pl.BlockSpec((ts, W3), lambda i: (i, 0)),
                  pl.BlockSpec((HALO, W3), lambda i: (jnp.maximum(i * per - 1, 0), 0)),
                  pl.BlockSpec((SC_CONV_WIDTH, W), lambda i: (0, 0))],
        out_specs=pl.BlockSpec((ts, W), lambda i: (i, 0)),
        scratch_shapes=[pltpu.VMEM((HALO + ts, W), F32)],
        compiler_params=_cparams("parallel"),
    )(u3, u3, conv_w)


def _sc_bwd(u3, dy, conv_w, *, name):
    S, W3 = u3.shape
    W = W3 // 3
    ts = _tile(S, (256,))
    per = ts // HALO
    nt = S // ts

    def body(cur_ref, prev_ref, next_ref, dy_ref, dyn_ref, w_ref, du_ref, dw_ref, cv_ext, dk_ext):
        i = pl.program_id(0)
        cvh = prev_ref[:, W:2 * W].astype(F32) * prev_ref[:, 2 * W:W3].astype(F32)
        cv_ext[0:HALO, :] = jnp.where(i == 0, 0.0, cvh)
        c = cur_ref[:, W:2 * W].astype(F32)
        v = cur_ref[:, 2 * W:W3].astype(F32)
        b = cur_ref[:, 0:W].astype(F32)
        cv_ext[HALO:HALO + ts, :] = c * v
        dy_cur = dy_ref[...]
        dk = dy_cur * b
        dk_ext[0:ts, :] = dk
        dk_ext[ts:ts + HALO, :] = jnp.where(i == nt - 1, 0.0, dyn_ref[...] * next_ref[:, 0:W].astype(F32))
        w0, w1, w2 = w_ref[0:1, :], w_ref[1:2, :], w_ref[2:3, :]
        cv2 = cv_ext[HALO - 2:HALO - 2 + ts, :]
        cv1 = cv_ext[HALO - 1:HALO - 1 + ts, :]
        cv0 = cv_ext[HALO:HALO + ts, :]
        kconv = w0 * cv2 + w1 * cv1 + w2 * cv0
        dcv = w2 * dk + w1 * dk_ext[1:1 + ts, :] + w0 * dk_ext[2:2 + ts, :]
        du_ref[:, 0:W] = (dy_cur * kconv).astype(du_ref.dtype)
        du_ref[:, W:2 * W] = (dcv * v).astype(du_ref.dtype)
        du_ref[:, 2 * W:W3] = (dcv * c).astype(du_ref.dtype)

        @pl.when(i == 0)
        def _():
            dw_ref[...] = jnp.zeros_like(dw_ref)

        dw_ref[0:1, :] += jnp.sum(dk * cv2, axis=0, keepdims=True)
        dw_ref[1:2, :] += jnp.sum(dk * cv1, axis=0, keepdims=True)
        dw_ref[2:3, :] += jnp.sum(dk * cv0, axis=0, keepdims=True)

    nxt = lambda i: (jnp.minimum((i + 1) * per, S // HALO - 1), 0)
    return pl.pallas_call(
        body, name=name,
        out_shape=(jax.ShapeDtypeStruct((S, W3), BF16), jax.ShapeDtypeStruct((8, W), F32)),
        grid=(nt,),
        in_specs=[pl.BlockSpec((ts, W3), lambda i: (i, 0)),
                  pl.BlockSpec((HALO, W3), lambda i: (jnp.maximum(i * per - 1, 0), 0)),
                  pl.BlockSpec((HALO, W3), nxt),
                  pl.BlockSpec((ts, W), lambda i: (i, 0)),
                  pl.BlockSpec((HALO, W), nxt),
                  pl.BlockSpec((SC_CONV_WIDTH, W), lambda i: (0, 0))],
        out_specs=(pl.BlockSpec((ts, W3), lambda i: (i, 0)), pl.BlockSpec((8, W), lambda i: (0, 0))),
        scratch_shapes=[pltpu.VMEM((HALO + ts, W), F32), pltpu.VMEM((ts + HALO, W), F32)],
        compiler_params=_cparams("arbitrary"),
    )(u3, u3, u3, dy, dy, conv_w)


def _t5_causal_bucket(n):
    max_exact = NUM_BUCKETS // 2
    nf = jnp.maximum(n, 1).astype(F32)
    large = max_exact + (jnp.log(nf / max_exact) / math.log(REL_MAX_DISTANCE / max_exact)
                         * (NUM_BUCKETS - max_exact)).astype(jnp.int32)
    return jnp.where(n < max_exact, n, jnp.minimum(large, NUM_BUCKETS - 1))


def _bucket_tables():
    steps = ATTN_BLK
    m = jnp.arange(steps)[:, None] + steps - jnp.arange(2 * steps)[None, :]
    return jnp.stack([_t5_causal_bucket(jnp.clip(m, 0, steps) * dil).astype(F32) for _, dil in DILATED_GROUPS])


def _bias_tables(rel_bias, buckets, *, name):
    steps = ATTN_BLK

    def body(tab_ref, bk_ref, o_ref):
        g = pl.program_id(0)
        bk = bk_ref[0]
        a_idx = lax.broadcasted_iota(jnp.int32, (steps, 2 * steps), 0)
        c_idx = lax.broadcasted_iota(jnp.int32, (steps, 2 * steps), 1)
        m = a_idx + steps - c_idx
        band = (m >= 0) & (m <= steps)
        band_first = band & (c_idx >= steps)
        for h in range(HEADS_PER_GROUP):
            bias = jnp.zeros((steps, 2 * steps), F32)
            for b in range(NUM_BUCKETS):
                bias = jnp.where(bk == float(b), tab_ref[b, g * HEADS_PER_GROUP + h], bias)
            o_ref[0, 0, h] = jnp.where(band_first, bias, NEG_INF)
            o_ref[0, 1, h] = jnp.where(band, bias, NEG_INF)

    return pl.pallas_call(
        body, name=name,
        out_shape=jax.ShapeDtypeStruct((N_GROUPS, 2, HEADS_PER_GROUP, steps, 2 * steps), F32),
        grid=(N_GROUPS,),
        in_specs=[pl.BlockSpec(memory_space=pltpu.SMEM),
                  pl.BlockSpec((1, steps, 2 * steps), lambda g: (g, 0, 0))],
        out_specs=pl.BlockSpec((1, 2, HEADS_PER_GROUP, steps, 2 * steps), lambda g: (g, 0, 0, 0, 0)),
        compiler_params=_cparams("parallel"),
    )(rel_bias, buckets)


def _lane_is_low():
    return lax.broadcasted_iota(jnp.int32, (1, LANES), 1) < HEAD_DIM


def _stack_heads(x2, low):
    zero = jnp.zeros_like(x2)
    return jnp.concatenate([jnp.where(low, x2, zero), jnp.where(low, zero, x2)], axis=0)


def _qkv_specs(nb):
    nqb = GROUP_QKV // ATTN_OUT

    def spec(t, prev):
        def idx(r, n):
            nn = jnp.minimum(n, nb - 1)
            row = jnp.maximum(nn - 1, 0) if prev else nn
            return (row, r * nqb + t)
        return pl.BlockSpec((ATTN_BLK, ATTN_OUT), idx)

    return [spec(0, False), spec(1, False), spec(1, True), spec(2, False), spec(2, True)]


def _attn_fwd(uv, bias, g, d, *, name):
    rows = uv.shape[0]

    def body(q_ref, kc_ref, kp_ref, vc_ref, vp_ref, bias_ref, o_ref, l_ref):
        n = pl.program_id(1)
        sel = jnp.minimum(n, 1)
        low = _lane_is_low()
        for hp in range(HEADS_PER_GROUP // 2):
            sl = slice(hp * LANES, (hp + 1) * LANES)
            q2 = q_ref[:, sl]
            k2 = jnp.concatenate([kp_ref[:, sl], kc_ref[:, sl]], axis=0)
            v2 = jnp.concatenate([vp_ref[:, sl], vc_ref[:, sl]], axis=0)
            s = _dot_nt(_stack_heads(q2, low), k2) * (HEAD_DIM ** -0.5)
            ps, dens, lses = [], [], []
            for hh in range(2):
                logits = s[hh * ATTN_BLK:(hh + 1) * ATTN_BLK] + bias_ref[sel, 2 * hp + hh]
                mx = jnp.max(logits, axis=-1, keepdims=True)
                p = jnp.exp(logits - mx)
                den = jnp.sum(p, axis=-1, keepdims=True)
                ps.append(p.astype(BF16))
                dens.append(den)
                lses.append(jnp.broadcast_to(mx + jnp.log(den), (ATTN_BLK, LANES)))
            pv = jnp.dot(jnp.concatenate(ps, axis=0), v2, preferred_element_type=F32)
            o_ref[:, sl] = jnp.where(low, pv[0:ATTN_BLK] / dens[0], pv[ATTN_BLK:2 * ATTN_BLK] / dens[1])
            l_ref[:, sl] = jnp.where(low, lses[0], lses[1])

    out_spec = pl.BlockSpec((ATTN_BLK, ATTN_OUT), lambda r, n: (n, r))
    return pl.pallas_call(
        body, name=name,
        out_shape=(jax.ShapeDtypeStruct((rows, d * ATTN_OUT), F32),) * 2,
        grid=(d, rows // ATTN_BLK),
        in_specs=_qkv_specs(rows // ATTN_BLK) + [pl.BlockSpec((None, 2, HEADS_PER_GROUP, ATTN_BLK, 2 * ATTN_BLK),
                                                              lambda r, n: (g, 0, 0, 0, 0))],
        out_specs=(out_spec, out_spec),
        compiler_params=_cparams("parallel", "parallel"),
    )(uv, uv, uv, uv, uv, bias)


def _chunk_scratch(n, width):
    return pltpu.VMEM((width // LANES, n, LANES), F32)


def _to_chunks(scr, val):
    for c in range(scr.shape[0]):
        scr[c] = val[:, c * LANES:(c + 1) * LANES]


def _from_chunks(scr):
    return jnp.concatenate([scr[c] for c in range(scr.shape[0])], axis=1)


def _slabs_from_chunks(dst_ref, scr, d, dtype):
    nc, n, _ = scr.shape
    for r in range(d):
        for c in range(nc):
            col = r * nc * LANES + c * LANES
            dst_ref[:, col:col + LANES] = scr[c, pl.ds(r, n // d, stride=d), :].astype(dtype)


def _chunks_from_slabs(scr, src_ref, d):
    nc, n, _ = scr.shape
    for r in range(d):
        for c in range(nc):
            col = r * nc * LANES + c * LANES
            scr[c, pl.ds(r, n // d, stride=d), :] = src_ref[:, col:col + LANES]


def _attn_merge(outs, lses, ya, *, name):
    S, C = ya.shape
    tm = _tile(S, (256,))
    dils = [dil for _, dil in DILATED_GROUPS]

    def body(o0, o1, o2, l0, l1, l2, ya_ref, cat_ref, out_ref, lse_ref, so1, so2, sl1, sl2):
        _chunks_from_slabs(so1, o1, dils[1])
        _chunks_from_slabs(so2, o2, dils[2])
        _chunks_from_slabs(sl1, l1, dils[1])
        _chunks_from_slabs(sl2, l2, dils[2])
        a0, a1, a2 = l0[...], _from_chunks(sl1), _from_chunks(sl2)
        m = jnp.maximum(jnp.maximum(a0, a1), a2)
        e0, e1, e2 = jnp.exp(a0 - m), jnp.exp(a1 - m), jnp.exp(a2 - m)
        den = e0 + e1 + e2
        out = (e0 * o0[...] + e1 * _from_chunks(so1) + e2 * _from_chunks(so2)) / den
        out_ref[...] = out
        lse_ref[...] = m + jnp.log(den)
        cat_ref[:, 0:C] = ya_ref[...]
        cat_ref[:, C:C + ATTN_OUT] = out.astype(cat_ref.dtype)

    blk = pl.BlockSpec((tm, ATTN_OUT), lambda i: (i, 0))
    vblk = [pl.BlockSpec((tm // d, d * ATTN_OUT), lambda i: (i, 0)) for d in dils]
    assert dils[0] == 1
    return pl.pallas_call(
        body, name=name,
        out_shape=(jax.ShapeDtypeStruct((S, C + ATTN_OUT), BF16), jax.ShapeDtypeStruct((S, ATTN_OUT), F32),
                   jax.ShapeDtypeStruct((S, ATTN_OUT), F32)),
        grid=(S // tm,),
        in_specs=vblk + vblk + [pl.BlockSpec((tm, C), lambda i: (i, 0))],
        out_specs=(pl.BlockSpec((tm, C + ATTN_OUT), lambda i: (i, 0)), blk, blk),
        scratch_shapes=[_chunk_scratch(tm, ATTN_OUT)] * 4,
        compiler_params=_cparams("parallel"),
    )(*outs, *lses, ya)


def _attn_prep(dcat, outf, lse, *, name):
    S, CW = dcat.shape
    C = CW - ATTN_OUT
    tm = _tile(S, (256,))
    dils = [dil for _, dil in DILATED_GROUPS]
    assert dils[0] == 1
    ones = np.kron(np.eye(HEADS_PER_GROUP, dtype=np.float32), np.ones((HEAD_DIM, HEAD_DIM), np.float32))

    nviews = 3 * (len(dils) - 1)

    def body(dcat_ref, out_ref, l_ref, ones_ref, dyb_ref, dl_ref, *rest):
        views = rest[:nviews]
        s_dyb, s_dl, s_l = rest[nviews:]
        dyb = dcat_ref[:, C:CW]
        dyb_ref[...] = dyb.astype(BF16)
        prod = dyb * out_ref[...]
        ov = ones_ref[...]
        hi, mid, lo = _split_bf16(prod)
        delta = (jnp.dot(hi, ov, preferred_element_type=F32)
                 + jnp.dot(mid, ov, preferred_element_type=F32)
                 + jnp.dot(lo, ov, preferred_element_type=F32))
        dl_ref[...] = delta
        _to_chunks(s_dyb, dyb)
        _to_chunks(s_dl, delta)
        _to_chunks(s_l, l_ref[...])
        for gi, d in enumerate(dils[1:]):
            dyb_v, dl_v, l_v = views[3 * gi:3 * gi + 3]
            _slabs_from_chunks(dyb_v, s_dyb, d, BF16)
            _slabs_from_chunks(dl_v, s_dl, d, F32)
            _slabs_from_chunks(l_v, s_l, d, F32)

    blk = pl.BlockSpec((tm, ATTN_OUT), lambda i: (i, 0))
    view_shapes, view_specs = [], []
    for d in dils[1:]:
        for dt in (BF16, F32, F32):
            view_shapes.append(jax.ShapeDtypeStruct((S // d, d * ATTN_OUT), dt))
            view_specs.append(pl.BlockSpec((tm // d, d * ATTN_OUT), lambda i: (i, 0)))
    res = pl.pallas_call(
        body, name=name,
        out_shape=(jax.ShapeDtypeStruct((S, ATTN_OUT), BF16), jax.ShapeDtypeStruct((S, ATTN_OUT), F32),
                   *view_shapes),
        grid=(S // tm,),
        in_specs=[pl.BlockSpec((tm, CW), lambda i: (i, 0)), blk, blk,
                  pl.BlockSpec((ATTN_OUT, ATTN_OUT), lambda i: (0, 0))],
        out_specs=(blk, blk, *view_specs),
        scratch_shapes=[_chunk_scratch(tm, ATTN_OUT)] * 3,
        compiler_params=_cparams("parallel"),
    )(dcat, outf, lse, jnp.asarray(ones, BF16))
    return [(res[0], res[1], lse)] + [tuple(res[2 + 3 * gi:5 + 3 * gi]) for gi in range(len(dils) - 1)]


def _attn_bwd(uv, dov, lv, dv_, bias, g, d, *, name):
    rows = uv.shape[0]
    nb = rows // ATTN_BLK
    scale = HEAD_DIM ** -0.5

    def body(q_ref, kc_ref, kp_ref, vc_ref, vp_ref, do_ref, l_ref, dl_ref, bias_ref,
             out_ref, db_ref, dq_s, dk_s, dv_s):
        r = pl.program_id(0)
        n = pl.program_id(1)
        low = _lane_is_low()

        @pl.when((r == 0) & (n == 0))
        def _():
            db_ref[...] = jnp.zeros_like(db_ref)

        @pl.when(n == 0)
        def _():
            dq_s[...] = jnp.zeros_like(dq_s)
            dk_s[...] = jnp.zeros_like(dk_s)
            dv_s[...] = jnp.zeros_like(dv_s)

        @pl.when(n < nb)
        def _():
            sel = jnp.minimum(n, 1)
            lane = lax.broadcasted_iota(jnp.int32, (1, LANES), 1)
            for hp in range(HEADS_PER_GROUP // 2):
                sl = slice(hp * LANES, (hp + 1) * LANES)
                q2 = q_ref[:, sl]
                do2 = do_ref[:, sl]
                k2 = jnp.concatenate([kp_ref[:, sl], kc_ref[:, sl]], axis=0)
                v2 = jnp.concatenate([vp_ref[:, sl], vc_ref[:, sl]], axis=0)
                lse2 = l_ref[:, sl]
                dl2 = dl_ref[:, sl]
                s = _dot_nt(_stack_heads(q2, low), k2) * scale
                dp = _dot_nt(_stack_heads(do2, low), v2)
                pbs, dsbs = [], []
                for hh in range(2):
                    rows = slice(hh * ATTN_BLK, (hh + 1) * ATTN_BLK)
                    one = lane == hh * HEAD_DIM
                    lse_col = jnp.sum(jnp.where(one, lse2, 0.0), axis=-1, keepdims=True)
                    dl_col = jnp.sum(jnp.where(one, dl2, 0.0), axis=-1, keepdims=True)
                    p = jnp.exp(s[rows] + bias_ref[sel, 2 * hp + hh] - lse_col)
                    ds = p * (dp[rows] - dl_col)
                    db_ref[2 * hp + hh] += ds
                    pbs.append(p.astype(BF16))
                    dsbs.append(ds.astype(BF16))
                dq = jnp.dot(jnp.concatenate(dsbs, axis=0), k2, preferred_element_type=F32) * scale
                dk = _dot_tn(jnp.concatenate(dsbs, axis=1), q2) * scale
                dv = _dot_tn(jnp.concatenate(pbs, axis=1), do2)
                nk2 = 2 * ATTN_BLK
                dq2 = jnp.where(low, dq[0:ATTN_BLK], dq[ATTN_BLK:nk2])
                dk2 = jnp.where(low, dk[0:nk2], dk[nk2:2 * nk2])
                dv2 = jnp.where(low, dv[0:nk2], dv[nk2:2 * nk2])
                out_ref[:, sl] = dq_s[:, sl].astype(out_ref.dtype)
                dq_s[:, sl] = dq2
                ksl = slice(ATTN_OUT + hp * LANES, ATTN_OUT + (hp + 1) * LANES)
                vsl = slice(2 * ATTN_OUT + hp * LANES, 2 * ATTN_OUT + (hp + 1) * LANES)
                out_ref[:, ksl] = (dk_s[:, sl] + dk2[0:ATTN_BLK]).astype(out_ref.dtype)
                dk_s[:, sl] = dk2[ATTN_BLK:2 * ATTN_BLK]
                out_ref[:, vsl] = (dv_s[:, sl] + dv2[0:ATTN_BLK]).astype(out_ref.dtype)
                dv_s[:, sl] = dv2[ATTN_BLK:2 * ATTN_BLK]

        @pl.when(n == nb)
        def _():
            out_ref[:, 0:ATTN_OUT] = dq_s[...].astype(out_ref.dtype)
            out_ref[:, ATTN_OUT:2 * ATTN_OUT] = dk_s[...].astype(out_ref.dtype)
            out_ref[:, 2 * ATTN_OUT:GROUP_QKV] = dv_s[...].astype(out_ref.dtype)

    rowblk = pl.BlockSpec((ATTN_BLK, ATTN_OUT), lambda r, n: (jnp.minimum(n, nb - 1), r))
    return pl.pallas_call(
        body, name=name,
        out_shape=(jax.ShapeDtypeStruct((rows, d * GROUP_QKV), BF16),
                   jax.ShapeDtypeStruct((HEADS_PER_GROUP, ATTN_BLK, 2 * ATTN_BLK), F32)),
        grid=(d, nb + 1),
        in_specs=_qkv_specs(nb) + [
            rowblk, rowblk, rowblk,
            pl.BlockSpec((None, 2, HEADS_PER_GROUP, ATTN_BLK, 2 * ATTN_BLK), lambda r, n: (g, 0, 0, 0, 0))],
        out_specs=(pl.BlockSpec((ATTN_BLK, GROUP_QKV), lambda r, n: (jnp.maximum(n - 1, 0), r)),
                   pl.BlockSpec((HEADS_PER_GROUP, ATTN_BLK, 2 * ATTN_BLK), lambda r, n: (0, 0, 0))),
        scratch_shapes=[pltpu.VMEM((ATTN_BLK, ATTN_OUT), F32)] * 3,
        compiler_params=_cparams("arbitrary", "arbitrary"),
    )(uv, uv, uv, uv, uv, dov, lv, dv_, bias)


def _split_bf16(x):
    hi = x.astype(BF16)
    r1 = x - hi.astype(F32)
    mid = r1.astype(BF16)
    lo = (r1 - mid.astype(F32)).astype(BF16)
    return hi, mid, lo


RELBIAS_CHUNK = 4096


def _relbias_reduce(dbs, buckets, *, name):
    flat = ATTN_BLK * 2 * ATTN_BLK
    dbf = jnp.stack([db.reshape(HEADS_PER_GROUP, flat) for db in dbs])
    bkf = buckets.reshape(N_GROUPS, 1, flat)

    def body(db_ref, bk_ref, o_ref):
        c = pl.program_id(1)
        rows = lax.broadcasted_iota(jnp.int32, (LANES, RELBIAS_CHUNK), 0).astype(F32)
        onehot = jnp.where(rows == bk_ref[0], 1.0, 0.0).astype(BF16)
        hi, mid, lo = _split_bf16(db_ref[0])
        part = _dot_nt(hi, onehot) + _dot_nt(mid, onehot) + _dot_nt(lo, onehot)

        @pl.when(c == 0)
        def _():
            o_ref[0] = part

        @pl.when(c > 0)
        def _():
            o_ref[0] += part

    return pl.pallas_call(
        body, name=name,
        out_shape=jax.ShapeDtypeStruct((N_GROUPS, HEADS_PER_GROUP, LANES), F32),
        grid=(N_GROUPS, flat // RELBIAS_CHUNK),
        in_specs=[pl.BlockSpec((1, HEADS_PER_GROUP, RELBIAS_CHUNK), lambda g, c: (g, 0, c)),
                  pl.BlockSpec((1, 1, RELBIAS_CHUNK), lambda g, c: (g, 0, c))],
        out_specs=pl.BlockSpec((1, HEADS_PER_GROUP, LANES), lambda g, c: (g, 0, 0)),
        compiler_params=_cparams("parallel", "arbitrary"),
    )(dbf, bkf)


def _my_position():
    x, y, c = lax.axis_index("x"), lax.axis_index("y"), lax.axis_index("c")
    return x, y, c


def _linear(pos):
    return 4 * pos[0] + 2 * pos[1] + pos[2]


def _peer(pos, k):
    x, y, c = pos
    return ((1 - x) if k & 4 else x, (1 - y) if k & 2 else y, (1 - c) if k & 1 else c)


HBM_SPEC = pl.BlockSpec(memory_space=pltpu.HBM)
SEM_SPEC = pl.BlockSpec(memory_space=pltpu.SEMAPHORE)
DATAFLOW = pltpu.SideEffectType.DATAFLOW_SIDE_EFFECTING


def _exchange_copies(src, land, sems, send_window, recv_window, with_arrivals):
    send_sems, recv_sems, local_sems = sems
    T = len(src)
    me = _my_position()
    me_lin = _linear(me)
    local = [pltpu.make_async_copy(send_window(t, src[t], me_lin), recv_window(t, land[t], me_lin),
                                   local_sems.at[t]) for t in range(T)]
    sends, arrivals = [], []
    for t in range(T):
        for k in range(1, N_DEV):
            peer = _peer(me, k)
            peer_lin = _linear(peer)
            sem = t * (N_DEV - 1) + k - 1
            sends.append(pltpu.make_async_remote_copy(
                src_ref=send_window(t, src[t], peer_lin), dst_ref=recv_window(t, land[t], me_lin),
                send_sem=send_sems.at[sem], recv_sem=recv_sems.at[sem],
                device_id=peer, device_id_type=MESH))
            if with_arrivals:
                arrivals.append(pltpu.make_async_remote_copy(
                    src_ref=send_window(t, src[t], me_lin), dst_ref=recv_window(t, land[t], peer_lin),
                    send_sem=send_sems.at[sem], recv_sem=recv_sems.at[sem],
                    device_id=peer, device_id_type=MESH))
    return local, sends, arrivals


def _exchange_start(srcs, land_shapes, send_window, recv_window, *, name, dep=None):
    T = len(srcs)
    n_in = 2 * T + (1 if dep is not None else 0)

    def body(*refs):
        src = refs[:T]
        land = refs[T:2 * T]
        sems = refs[n_in:n_in + 3]
        token = refs[-1]
        local, sends, _ = _exchange_copies(src, land, sems, send_window, recv_window, False)
        for cp in local + sends:
            cp.start()
        token[...] = jnp.zeros_like(token)

    lands = [lax.empty(ls.shape, ls.dtype) for ls in land_shapes]
    operands = [pltpu.with_memory_space_constraint(a, pltpu.HBM) for a in list(srcs) + lands]
    outs = pl.pallas_call(
        body, name=name,
        out_shape=(pltpu.SemaphoreType.DMA((T * (N_DEV - 1),)), pltpu.SemaphoreType.DMA((T * (N_DEV - 1),)),
                   pltpu.SemaphoreType.DMA((T,)),
                   *[pltpu.HBM(a.shape, a.dtype) for a in operands],
                   jax.ShapeDtypeStruct((8, LANES), F32)),
        in_specs=[HBM_SPEC] * (2 * T) + ([ANY] if dep is not None else []),
        out_specs=(SEM_SPEC,) * 3 + (HBM_SPEC,) * (2 * T) + (VMEM_SPEC,),
        input_output_aliases={i: 3 + i for i in range(2 * T)},
        compiler_params=pltpu.CompilerParams(has_side_effects=DATAFLOW),
    )(*operands, *([dep] if dep is not None else []))
    return outs[:3], outs[3:3 + T], outs[3 + T:3 + 2 * T], outs[-1]


def _exchange_wait(started, after, send_window, recv_window, *, name):
    sems, srcs, lands, _ = started
    T = len(srcs)

    def body(*refs):
        src = refs[:T]
        land = refs[T:2 * T]
        sem_refs = refs[2 * T:2 * T + 3]
        local, sends, arrivals = _exchange_copies(src, land, sem_refs, send_window, recv_window, True)
        for cp in arrivals:
            cp.wait_recv()
        for cp in sends:
            cp.wait_send()
        for cp in local:
            cp.wait()

    outs = pl.pallas_call(
        body, name=name,
        out_shape=tuple(pltpu.HBM(a.shape, a.dtype) for a in list(srcs) + list(lands)),
        in_specs=[HBM_SPEC] * (2 * T) + [SEM_SPEC] * 3 + [ANY],
        out_specs=(HBM_SPEC,) * (2 * T),
        input_output_aliases={i: i for i in range(2 * T)},
        compiler_params=pltpu.CompilerParams(has_side_effects=DATAFLOW),
    )(*srcs, *lands, *sems, after)
    return outs[T:]


def _shard_window(kind, width):
    def win(ref, lin):
        if kind == "slot":
            return ref.at[lin]
        if kind == "col":
            return ref.at[:, pl.ds(pl.multiple_of(lin * width, LANES), width)]
        if kind == "row":
            return ref.at[pl.ds(pl.multiple_of(lin * width, 8), width), :]
        if kind == "lcol":
            return ref.at[:, :, pl.ds(pl.multiple_of(lin * width, LANES), width)]
        if kind == "lrow":
            return ref.at[:, pl.ds(pl.multiple_of(lin * width, 8), width), :]
        raise ValueError(kind)
    return win


def _shard_windows(kinds, shard_shapes):
    return [_shard_window(k, (ss[-1] if k in ("col", "lcol") else ss[-2])) for k, ss in zip(kinds, shard_shapes)]


def _allgather_start(shards, kinds, full_shapes, *, name, dep=None):
    wins = _shard_windows(kinds, [s.shape for s in shards])
    send_window = lambda t, ref, lin: ref
    recv_window = lambda t, ref, lin: wins[t](ref, lin)
    started = _exchange_start(shards, [jax.ShapeDtypeStruct(fs, s.dtype) for fs, s in zip(full_shapes, shards)],
                              send_window, recv_window, name=name + "_start", dep=dep)
    return started, lambda after: _exchange_wait(started, after, send_window, recv_window, name=name + "_wait")


def _scatter_start(fulls, kinds, shard_shapes, *, name):
    wins = _shard_windows(kinds, shard_shapes)
    send_window = lambda t, ref, lin: wins[t](ref, lin)
    recv_window = lambda t, ref, lin: ref.at[lin]
    started = _exchange_start(
        fulls, [jax.ShapeDtypeStruct((N_DEV,) + tuple(ss), f.dtype) for ss, f in zip(shard_shapes, fulls)],
        send_window, recv_window, name=name + "_start")
    return started, lambda after: _exchange_wait(started, after, send_window, recv_window, name=name + "_wait")


def _small_gather(pack, *, reduce, name):
    R = pack.shape[0]

    def body(p_ref, o_ref, *rest):
        if reduce:
            buf, send_sems, recv_sems = rest
        else:
            buf = o_ref
            send_sems, recv_sems = rest
        me = _my_position()
        me_lin = _linear(me)
        buf[me_lin] = p_ref[...]
        sends = []
        for k in range(1, N_DEV):
            peer = _peer(me, k)
            cp = pltpu.make_async_remote_copy(
                src_ref=p_ref, dst_ref=buf.at[me_lin],
                send_sem=send_sems.at[k - 1], recv_sem=recv_sems.at[k - 1],
                device_id=peer, device_id_type=MESH)
            cp.start()
            sends.append(cp)
        for k in range(1, N_DEV):
            peer = _peer(me, k)
            pltpu.make_async_remote_copy(
                src_ref=p_ref, dst_ref=buf.at[_linear(peer)],
                send_sem=send_sems.at[k - 1], recv_sem=recv_sems.at[k - 1],
                device_id=peer, device_id_type=MESH).wait_recv()
        for cp in sends:
            cp.wait_send()
        if reduce:
            acc = buf[0]
            for s in range(1, N_DEV):
                acc = acc + buf[s]
            o_ref[...] = acc

    scratch = [pltpu.SemaphoreType.DMA((N_DEV - 1,)), pltpu.SemaphoreType.DMA((N_DEV - 1,))]
    if reduce:
        scratch = [pltpu.VMEM((N_DEV, R, LANES), F32)] + scratch
        out_shape = jax.ShapeDtypeStruct((R, LANES), F32)
    else:
        out_shape = jax.ShapeDtypeStruct((N_DEV, R, LANES), F32)
    return pl.pallas_call(
        body, name=name, out_shape=out_shape,
        in_specs=[VMEM_SPEC], out_specs=VMEM_SPEC, scratch_shapes=scratch,
        compiler_params=pltpu.CompilerParams(has_side_effects=True, vmem_limit_bytes=VMEM_LIMIT),
    )(pack)


def _adamw_math(w, g, m, v):
    m = ADAM_B1 * m + (1.0 - ADAM_B1) * g
    v = ADAM_B2 * v + (1.0 - ADAM_B2) * jnp.square(g)
    m_hat = m / (1.0 - ADAM_B1 ** ADAM_STEP)
    v_hat = v / (1.0 - ADAM_B2 ** ADAM_STEP)
    delta = -ADAM_LR * (m_hat / (jnp.sqrt(v_hat) + ADAM_EPS) + ADAM_WD * w)
    return delta, m, v


def _adamw_from_partials(parts, w, m, v, *, name):
    R, C = w.shape
    rl = parts[0].shape[1]
    assert all(p.shape == (N_DEV, rl, C) for p in parts) and rl * len(parts) == R
    tr = _tile(rl, (256, 128, 64, 32, 16))
    per = rl // tr
    L = len(parts)

    def body(*refs):
        p_refs = refs[:L]
        w_ref, m_ref, v_ref, g_ref, d_ref, nm_ref, nv_ref = refs[L:]
        i = pl.program_id(0)
        for l in range(L):
            @pl.when((i >= l * per) & (i < (l + 1) * per))
            def _(l=l):
                p_ref = p_refs[l]
                g = p_ref[0].astype(F32)
                for s in range(1, N_DEV):
                    g = g + p_ref[s].astype(F32)
                d, nm, nv = _adamw_math(w_ref[...], g, m_ref[...], v_ref[...])
                g_ref[...] = g
                d_ref[...] = d
                nm_ref[...] = nm
                nv_ref[...] = nv

    blk = pl.BlockSpec((tr, C), lambda i: (i, 0))
    part_specs = [pl.BlockSpec((N_DEV, tr, C), lambda i, l=l: (0, jnp.clip(i - l * per, 0, per - 1), 0))
                  for l in range(L)]
    return pl.pallas_call(
        body, name=name,
        out_shape=(jax.ShapeDtypeStruct((R, C), F32),) * 4,
        grid=(R // tr,),
        in_specs=part_specs + [blk, blk, blk],
        out_specs=(blk,) * 4,
        compiler_params=_cparams("parallel"),
    )(*parts, w, m, v)


def _adamw_small(g, w, m, v, *, name):
    def body(g_ref, w_ref, m_ref, v_ref, d_ref, nm_ref, nv_ref):
        d, nm, nv = _adamw_math(w_ref[...], g_ref[...], m_ref[...], v_ref[...])
        d_ref[...] = d
        nm_ref[...] = nm
        nv_ref[...] = nv

    return pl.pallas_call(
        body, name=name,
        out_shape=(jax.ShapeDtypeStruct(g.shape, F32),) * 3,
        in_specs=[VMEM_SPEC] * 4, out_specs=(VMEM_SPEC,) * 3,
    )(g, w, m, v)


def _pack_rows(pieces):
    flat = jnp.concatenate([p.reshape(-1) for p in pieces])
    n = flat.shape[0]
    padded = -(-n // (8 * LANES)) * (8 * LANES)
    return jnp.pad(flat, (0, padded - n)).reshape(padded // LANES, LANES)


def _unpack_rows(pack, shapes):
    flat = pack.reshape(-1)
    out, pos = [], 0
    for s in shapes:
        n = int(np.prod(s))
        out.append(flat[pos:pos + n].reshape(s))
        pos += n
    return out


def kernel(x, rel_bias, ab_norm, ab_w_in, ab_conv_w, ab_conv_b, ab_ln_g, ab_ln_b, ab_w_out, sc_norm, sc_w_in, sc_conv_w, sc_w_out, mlp_norm, mlp_w_up, mlp_w_down, final_norm, loss_target, m_rel_bias, m_ab_norm, m_ab_w_in, m_ab_conv_w, m_ab_conv_b, m_ab_ln_g, m_ab_ln_b, m_ab_w_out, m_sc_norm, m_sc_w_in, m_sc_conv_w, m_sc_w_out, m_mlp_norm, m_mlp_w_up, m_mlp_w_down, m_final_norm, v_rel_bias, v_ab_norm, v_ab_w_in, v_ab_conv_w, v_ab_conv_b, v_ab_ln_g, v_ab_ln_b, v_ab_w_out, v_sc_norm, v_sc_w_in, v_sc_conv_w, v_sc_w_out, v_mlp_norm, v_mlp_w_up, v_mlp_w_down, v_final_norm):
    S, D = x.shape[1], x.shape[2]
    CA = ab_conv_b.shape[1]
    C2 = 2 * CA
    AB_IN = C2 + ATTN_IN
    me_lin = _linear(_my_position())
    xs = x.reshape(S, D)
    tgt = loss_target.reshape(S, D)

    cw_sh = ab_conv_w.shape[2]
    scn_sh = sc_norm.shape[1]
    scw_sh = sc_conv_w.shape[2]
    small_sh_shapes = [(CONV_A_WIDTH, cw_sh), (scn_sh,), (SC_CONV_WIDTH, scw_sh)]
    small_params = _small_gather(_pack_rows([ab_conv_w[0], sc_norm[0], sc_conv_w[0]]), reduce=False,
                                 name="allgather_small_params")
    w_in_sh = ab_w_in[0].astype(BF16)
    ag_ab, wait_ab = _allgather_start(
        [w_in_sh, ab_w_out[0].astype(BF16)], ["slot", "row"],
        [(N_DEV,) + w_in_sh.shape, (N_DEV * ab_w_out.shape[1], D)], name="allgather_ab", dep=small_params)
    ag_mlp, wait_mlp = _allgather_start(
        [mlp_w_up.astype(BF16), mlp_w_down.astype(BF16)], ["lcol", "lrow"],
        [(2, D, N_DEV * mlp_w_up.shape[2]), (2, N_DEV * mlp_w_down.shape[1], D)], name="allgather_mlp",
        dep=ag_ab[3])
    ag_sc, wait_sc = _allgather_start(
        [sc_w_in[0].astype(BF16), sc_w_out[0].astype(BF16)], ["col", "row"],
        [(D, N_DEV * sc_w_in.shape[2]), (N_DEV * sc_w_out.shape[1], D)], name="allgather_sc",
        dep=ag_mlp[3])

    per_dev = [_unpack_rows(small_params[s], small_sh_shapes) for s in range(N_DEV)]
    conv_w_full = jnp.concatenate([p[0] for p in per_dev], axis=1)
    sc_norm_full = jnp.concatenate([p[1] for p in per_dev], axis=0)[None]
    sc_conv_full = jnp.concatenate([p[2] for p in per_dev], axis=1)

    buckets = _bucket_tables()
    biases = _bias_tables(rel_bias, buckets, name="bias_tables")

    dils = [dil for _, dil in DILATED_GROUPS]
    n0_all = _rmsnorm_fwd(xs, ab_norm, name="norm_ab", dep=ag_sc[3], views=dils[1:])
    n0 = n0_all[0]
    w_in_g, w_out = wait_ab(n0)
    w_in = jnp.transpose(w_in_g, (1, 0, 2)).reshape(D, AB_IN)
    w_c = w_in[:, :C2]
    w_q = w_in[:, C2:]
    w_grp = [jnp.concatenate([w_q[:, t * N_GROUPS * ATTN_OUT + g * ATTN_OUT:][:, :ATTN_OUT] for t in range(3)], axis=1)
             for g in range(N_GROUPS)]
    uc = _mm_nn(n0, w_c, out_dtype=F32, name="mm_ab_in_conv")
    uqs = [_mm_nn(n0_all[g], w_grp[g], out_dtype=BF16, slabs=dils[g], name=f"mm_ab_in_qkv{g}")
           for g in range(N_GROUPS)]
    ya, hglu, ct = _conv_fwd(uc, conv_w_full, ab_conv_b, ab_ln_g, ab_ln_b, name="conv_fwd")
    outs, lses = zip(*[_attn_fwd(uqs[g], biases, g, dils[g], name=f"attn_fwd_{g}") for g in range(N_GROUPS)])
    cat, outf, lse = _attn_merge(outs, lses, ya, name="attn_merge")
    h1 = _mm_nn(cat, w_out, out_dtype=F32, residual=xs, name="mm_ab_out")
    n1 = _rmsnorm_fwd(h1, mlp_norm[0:1], name="norm_mlp0")
    w_up, w_dn = wait_mlp(n1)
    z0 = _mm_nn(n1, (w_up, 0), out_dtype=BF16, name="mm_up0")
    h2 = _mm_nn(z0, (w_dn, 0), out_dtype=F32, residual=h1, a_fn=_relu_sq, name="mm_down0")
    n2 = _rmsnorm_fwd(h2, sc_norm_full, name="norm_sc")
    w_sc_in, w_sc_out = wait_sc(n2)
    u3 = _mm_nn(n2, w_sc_in, out_dtype=BF16, name="mm_sc_in")
    ysc = _sc_fwd(u3, sc_conv_full, name="sc_fwd")
    h3 = _mm_nn(ysc, w_sc_out, out_dtype=F32, residual=h2, name="mm_sc_out")
    n3 = _rmsnorm_fwd(h3, mlp_norm[1:2], name="norm_mlp1")
    z1 = _mm_nn(n3, (w_up, 1), out_dtype=BF16, name="mm_up1")
    h4 = _mm_nn(z1, (w_dn, 1), out_dtype=F32, residual=h3, a_fn=_relu_sq, name="mm_down1")

    def dz_epilogue(acc, z):
        return acc * (2.0 * jnp.maximum(z.astype(F32), 0.0))

    dh4, dh4b, acc_final = _loss_bwd(h4, tgt, final_norm[None], name="loss_bwd")
    dz1 = _mm_nt([(dh4b, (w_dn, 1))], out_dtype=BF16, epilogue=dz_epilogue, extra=z1, name="mm_d_down1")
    g_dn1 = _mm_tn(z1, dh4b, a_fn=_relu_sq, name="mm_gw_down1")
    g_up1 = _mm_tn(n3, dz1, name="mm_gw_up1")
    rs_mlp1, wait_rs_mlp1 = _scatter_start([g_up1, g_dn1], ["col", "row"],
                                           [mlp_w_up.shape[1:], mlp_w_down.shape[1:]], name="scatter_mlp1")
    dh3, dh3b, acc_mlp1 = _mm_nt_rms_bwd(dz1, (w_up, 1), h3, mlp_norm[1:2], dh4, name="mm_d_up1_norm_bwd",
                                         dep=rs_mlp1[3])

    dysc = _mm_nt([(dh3b, w_sc_out)], out_dtype=F32, name="mm_d_sc_out")
    g_sc_out = _mm_tn(ysc, dh3b, name="mm_gw_sc_out")
    du3, acc_scw = _sc_bwd(u3, dysc, sc_conv_full, name="sc_bwd")
    g_sc_in = _mm_tn(n2, du3, name="mm_gw_sc_in")
    rs_sc, wait_rs_sc = _scatter_start([g_sc_in, g_sc_out], ["col", "row"],
                                       [sc_w_in.shape[1:], sc_w_out.shape[1:]], name="scatter_sc")
    dh2, dh2b, acc_sc = _mm_nt_rms_bwd(du3, w_sc_in, h2, sc_norm_full, dh3, name="mm_d_sc_in_norm_bwd",
                                       dep=rs_sc[3])

    dz0 = _mm_nt([(dh2b, (w_dn, 0))], out_dtype=BF16, epilogue=dz_epilogue, extra=z0, name="mm_d_down0")
    g_dn0 = _mm_tn(z0, dh2b, a_fn=_relu_sq, name="mm_gw_down0")
    g_up0 = _mm_tn(n1, dz0, name="mm_gw_up0")
    rs_mlp0, wait_rs_mlp0 = _scatter_start([g_up0, g_dn0], ["col", "row"],
                                           [mlp_w_up.shape[1:], mlp_w_down.shape[1:]], name="scatter_mlp0")
    dh1, dh1b, acc_mlp0 = _mm_nt_rms_bwd(dz0, (w_up, 0), h1, mlp_norm[0:1], dh2, name="mm_d_up0_norm_bwd",
                                         dep=rs_mlp0[3])

    dcat = _mm_nt([(dh1b, w_out)], out_dtype=F32, name="mm_d_ab_out")
    g_ab_out = _mm_tn(cat, dh1b, name="mm_gw_ab_out")
    prep = _attn_prep(dcat, outf, lse, name="attn_prep")
    dqkv, dbs = zip(*[_attn_bwd(uqs[g], prep[g][0], prep[g][2], prep[g][1], biases, g, dils[g],
                                name=f"attn_bwd_{g}") for g in range(N_GROUPS)])
    drel = _relbias_reduce(dbs, buckets, name="relbias_reduce")
    dc, acc_conv = _conv_bwd_ln(ct, dcat, hglu, ab_ln_g, ab_ln_b, name="conv_bwd_ln")
    duc = _conv_bwd_in(dc, uc, conv_w_full, name="conv_bwd_in")
    g_wc = _mm_tn(n0, duc, name="mm_gw_ab_in_conv")
    g_wgrp = [_mm_tn(n0_all[g], dqkv[g], slabs=dils[g], name=f"mm_gw_ab_in_qkv{g}") for g in range(N_GROUPS)]
    g_wq = jnp.concatenate([g_wgrp[g][:, t * ATTN_OUT:(t + 1) * ATTN_OUT]
                            for t in range(3) for g in range(N_GROUPS)], axis=1)
    g_w_in = jnp.concatenate([g_wc, g_wq], axis=1).reshape(D, N_DEV, AB_IN // N_DEV).transpose(1, 0, 2)
    rs_ab, wait_rs_ab = _scatter_start([g_w_in, g_ab_out], ["slot", "row"],
                                       [w_in_sh.shape, ab_w_out.shape[1:]], name="scatter_ab")
    dn0 = _mm_nt([(duc, w_c), (dqkv[0], w_grp[0])], out_dtype=F32, name="mm_d_ab_in", dep=rs_ab[3])
    dn0_views = [(_mm_nt([(dqkv[g], w_grp[g])], out_dtype=F32, slabs=dils[g], name=f"mm_d_ab_in_qkv{g}"), dils[g])
                 for g in range(1, N_GROUPS)]
    grad_x, grad_xb, acc_ab = _rms_bwd(xs, ab_norm, dn0, dh1, name="norm_ab_bwd", dn_views=dn0_views)

    small_full = [drel[:, :, :NUM_BUCKETS].transpose(2, 0, 1).reshape(NUM_BUCKETS, N_GROUPS * HEADS_PER_GROUP),
                  acc_ab[0], acc_conv[0:CONV_A_WIDTH], acc_conv[32],
                  acc_conv[33], acc_conv[34], acc_sc[0], acc_scw[0:SC_CONV_WIDTH],
                  jnp.stack([acc_mlp0[0], acc_mlp1[0]]), acc_final[0], acc_final[1]]
    small_full_shapes = [p.shape for p in small_full]
    summed = _unpack_rows(_small_gather(_pack_rows(small_full), reduce=True, name="allreduce_small"),
                          small_full_shapes)
    (s_rel, s_abn, s_cw, s_cb, s_lg, s_lb, s_scn, s_scw, s_mlpn, s_fn, s_err) = summed
    loss = (0.5 / D) * jnp.sum(s_err)
    small_grads = {
        "rel_bias": s_rel, "ab_norm": s_abn[None],
        "ab_conv_w": lax.dynamic_slice_in_dim(s_cw, me_lin * cw_sh, cw_sh, axis=1)[None],
        "ab_conv_b": s_cb[None], "ab_ln_g": s_lg[None], "ab_ln_b": s_lb[None],
        "sc_norm": lax.dynamic_slice_in_dim(s_scn, me_lin * scn_sh, scn_sh, axis=0)[None],
        "sc_conv_w": lax.dynamic_slice_in_dim(s_scw, me_lin * scw_sh, scw_sh, axis=1)[None],
        "mlp_norm": s_mlpn, "final_norm": s_fn,
    }
    small_w = {"rel_bias": (rel_bias, m_rel_bias, v_rel_bias), "ab_norm": (ab_norm, m_ab_norm, v_ab_norm),
               "ab_conv_w": (ab_conv_w, m_ab_conv_w, v_ab_conv_w), "ab_conv_b": (ab_conv_b, m_ab_conv_b, v_ab_conv_b),
               "ab_ln_g": (ab_ln_g, m_ab_ln_g, v_ab_ln_g), "ab_ln_b": (ab_ln_b, m_ab_ln_b, v_ab_ln_b),
               "sc_norm": (sc_norm, m_sc_norm, v_sc_norm), "sc_conv_w": (sc_conv_w, m_sc_conv_w, v_sc_conv_w),
               "mlp_norm": (mlp_norm, m_mlp_norm, v_mlp_norm), "final_norm": (final_norm, m_final_norm, v_final_norm)}
    small_names = list(small_grads)
    small_shapes = [small_grads[n].shape for n in small_names]
    d_pack, m_pack, v_pack = _adamw_small(
        _pack_rows([small_grads[n] for n in small_names]), _pack_rows([small_w[n][0] for n in small_names]),
        _pack_rows([small_w[n][1] for n in small_names]), _pack_rows([small_w[n][2] for n in small_names]),
        name="adamw_small")
    small = {n: (small_grads[n], d, nm_, nv_) for n, d, nm_, nv_ in zip(
        small_names, _unpack_rows(d_pack, small_shapes), _unpack_rows(m_pack, small_shapes),
        _unpack_rows(v_pack, small_shapes))}

    p_up1, p_dn1 = wait_rs_mlp1(grad_xb)
    p_sc_in, p_sc_out = wait_rs_sc(grad_xb)
    p_up0, p_dn0 = wait_rs_mlp0(grad_xb)
    p_w_in, p_ab_out = wait_rs_ab(grad_xb)
    big = {}
    for nm, parts, w, m, v in (("ab_w_in", [p_w_in], ab_w_in, m_ab_w_in, v_ab_w_in),
                               ("ab_w_out", [p_ab_out], ab_w_out, m_ab_w_out, v_ab_w_out),
                               ("sc_w_in", [p_sc_in], sc_w_in, m_sc_w_in, v_sc_w_in),
                               ("sc_w_out", [p_sc_out], sc_w_out, m_sc_w_out, v_sc_w_out),
                               ("mlp_w_up", [p_up0, p_up1], mlp_w_up, m_mlp_w_up, v_mlp_w_up),
                               ("mlp_w_down", [p_dn0, p_dn1], mlp_w_down, m_mlp_w_down, v_mlp_w_down)):
        C = w.shape[-1]
        res = _adamw_from_partials(parts, w.reshape(-1, C), m.reshape(-1, C), v.reshape(-1, C), name="adamw_" + nm)
        big[nm] = tuple(r.reshape(w.shape) for r in res)

    order = ["rel_bias", "ab_norm", "ab_w_in", "ab_conv_w", "ab_conv_b", "ab_ln_g", "ab_ln_b", "ab_w_out",
             "sc_norm", "sc_w_in", "sc_conv_w", "sc_w_out", "mlp_norm", "mlp_w_up", "mlp_w_down", "final_norm"]
    allres = {**big, **small}
    return (loss, grad_x.reshape(x.shape),
            *[allres[n][0] for n in order], *[allres[n][1] for n in order],
            *[allres[n][2] for n in order], *[allres[n][3] for n in order])
```

```python
import functools
import math

import numpy as np
import jax
import jax.numpy as jnp
from jax import lax
from jax.experimental import pallas as pl
from jax.experimental.pallas import tpu as pltpu

F32 = jnp.float32
BF16 = jnp.bfloat16

HEAD_DIM = 64
HEADS_PER_GROUP = 8
DILATED_GROUPS = ((128, 1), (512, 4), (2048, 16))
N_GROUPS = 3
ATTN_OUT = HEADS_PER_GROUP * HEAD_DIM
ATTN_IN = 3 * N_GROUPS * ATTN_OUT
GROUP_QKV = 3 * ATTN_OUT
ATTN_BLK = 128
CONV_A_WIDTH = 31
SC_CONV_WIDTH = 3
NUM_BUCKETS = 32
REL_MAX_DISTANCE = 2048
RMS_EPS = 1e-6
LN_EPS = 1e-5
NEG_INF = -1e30
ADAM_LR = 0.001
ADAM_B1 = 0.9
ADAM_B2 = 0.999
ADAM_EPS = 1e-08
ADAM_WD = 0.01
ADAM_STEP = 10

N_DEV = 8
HALO = 32
LANES = 128
VMEM_LIMIT = 56 * 1024 * 1024
MESH = pl.DeviceIdType.MESH
ANY = pl.BlockSpec(memory_space=pl.ANY)
VMEM_SPEC = pl.BlockSpec(memory_space=pltpu.VMEM)


def _tile(n, prefs):
    for t in prefs:
        if n % t == 0:
            return t
    return n


def _cparams(*sem):
    return pltpu.CompilerParams(dimension_semantics=sem, vmem_limit_bytes=VMEM_LIMIT)


def _relu_sq(z):
    return jnp.square(jnp.maximum(z, 0))


def _dot_nt(a, b):
    return lax.dot_general(a, b, (((1,), (1,)), ((), ())), preferred_element_type=F32)


def _dot_tn(a, b):
    return lax.dot_general(a, b, (((0,), (0,)), ((), ())), preferred_element_type=F32)


def _weight(b):
    if not isinstance(b, tuple):
        return b, b.shape, pl.BlockSpec
    arr, layer = b

    def spec(block, index_map):
        return pl.BlockSpec((None,) + tuple(block), lambda *g: (layer,) + tuple(index_map(*g)))

    return arr, arr.shape[1:], spec


def _mm_nn(a, b, *, out_dtype, name, residual=None, a_fn=None, slabs=1):
    M, K = a.shape
    K //= slabs
    b, (_, N), b_spec = _weight(b)
    tm = _tile(M, (2048, 1024, 512, 256))
    tn = _tile(N, (512, 384, 256, 128))
    tk = _tile(K, (1024, 512, 256, 128))
    nk = K // tk
    nj = N // tn
    has_res = residual is not None

    def body(*refs):
        if has_res:
            a_ref, b_ref, r_ref, o_ref = refs[:4]
        else:
            a_ref, b_ref, o_ref = refs[:3]
        av = a_ref[...]
        if a_fn is not None:
            av = a_fn(av)
        part = jnp.dot(av, b_ref[...], preferred_element_type=F32)

        def finish(acc):
            if has_res:
                acc = acc + r_ref[...]
            o_ref[...] = acc.astype(o_ref.dtype)

        if nk == 1:
            finish(part)
        else:
            acc_ref = refs[-1]
            k = pl.program_id(2)

            @pl.when(k == 0)
            def _():
                acc_ref[...] = part

            @pl.when((k > 0) & (k < nk - 1))
            def _():
                acc_ref[...] += part

            @pl.when(k == nk - 1)
            def _():
                finish(acc_ref[...] + part)

    in_specs = [pl.BlockSpec((tm, tk), lambda i, j, k: (i, (j // nj) * nk + k)),
                b_spec((tk, tn), lambda i, j, k: (k, j % nj))]
    args = [a, b]
    if has_res:
        in_specs.append(pl.BlockSpec((tm, tn), lambda i, j, k: (i, j)))
        args.append(residual)
    return pl.pallas_call(
        body, name=name,
        out_shape=jax.ShapeDtypeStruct((M, slabs * N), out_dtype),
        grid=(M // tm, slabs * nj, nk),
        in_specs=in_specs,
        out_specs=pl.BlockSpec((tm, tn), lambda i, j, k: (i, j)),
        scratch_shapes=[pltpu.VMEM((tm, tn), F32)] if nk > 1 else [],
        compiler_params=_cparams("parallel", "parallel", "arbitrary"),
    )(*args)


def _norm_mm_nn(h, g, b, *, out_dtype, name):
    M, K = h.shape
    b, (_, N), b_spec = _weight(b)
    tm = _tile(M, (2048, 1024, 512, 256))
    tn = _tile(N, (512, 384, 256, 128))

    def body(h_ref, g_ref, b_ref, n_ref, o_ref):
        @pl.when(pl.program_id(1) == 0)
        def _():
            x = h_ref[...]
            r = lax.rsqrt(jnp.mean(x * x, axis=-1, keepdims=True) + RMS_EPS)
            n_ref[...] = (x * r * g_ref[...]).astype(BF16)

        o_ref[...] = jnp.dot(n_ref[...], b_ref[...], preferred_element_type=F32).astype(o_ref.dtype)

    return pl.pallas_call(
        body, name=name,
        out_shape=(jax.ShapeDtypeStruct((M, K), BF16), jax.ShapeDtypeStruct((M, N), out_dtype)),
        grid=(M // tm, N // tn),
        in_specs=[pl.BlockSpec((tm, K), lambda i, j: (i, 0)), pl.BlockSpec((1, K), lambda i, j: (0, 0)),
                  b_spec((K, tn), lambda i, j: (0, j))],
        out_specs=(pl.BlockSpec((tm, K), lambda i, j: (i, 0)), pl.BlockSpec((tm, tn), lambda i, j: (i, j))),
        compiler_params=_cparams("parallel", "arbitrary"),
    )(h, g, b)


def _mm_nt(pairs, *, out_dtype, name, epilogue=None, extra=None, dep=None, slabs=1):
    assert slabs == 1 or len(pairs) == 1
    M = pairs[0][0].shape[0]
    weights = [_weight(p[1]) for p in pairs]
    Ko = weights[0][1][0]
    tm = _tile(M, (1024, 512, 256))
    to = _tile(Ko, (1024, 512, 256, 128))
    njo = Ko // to
    tks = [_tile(p[0].shape[1] // slabs, (1024, 768, 512, 256, 128)) for p in pairs]
    steps = [p[0].shape[1] // slabs // tk for p, tk in zip(pairs, tks)]
    offs = [sum(steps[:i]) for i in range(len(pairs))]
    nk = sum(steps)
    npair = len(pairs)
    has_extra = extra is not None

    def body(*refs):
        ab = refs[:2 * npair]
        pos = 2 * npair
        e_ref = None
        if has_extra:
            e_ref = refs[pos]
            pos += 1
        if dep is not None:
            pos += 1
        o_ref = refs[pos]
        acc_ref = refs[pos + 1]
        k = pl.program_id(2)

        @pl.when(k == 0)
        def _():
            acc_ref[...] = jnp.zeros_like(acc_ref)

        for p in range(npair):
            @pl.when((k >= offs[p]) & (k < offs[p] + steps[p]))
            def _(p=p):
                acc_ref[...] += _dot_nt(ab[2 * p][...], ab[2 * p + 1][...])

        @pl.when(k == nk - 1)
        def _():
            acc = acc_ref[...]
            if epilogue is not None:
                acc = epilogue(acc, e_ref[...] if has_extra else None)
            o_ref[...] = acc.astype(o_ref.dtype)

    in_specs, args = [], []
    for p, (a, b) in enumerate(pairs):
        def kidx(k, p=p):
            return jnp.clip(k - offs[p], 0, steps[p] - 1)
        in_specs.append(pl.BlockSpec((tm, tks[p]),
                                     lambda i, j, k, kidx=kidx, p=p: (i, (j // njo) * steps[p] + kidx(k))))
        in_specs.append(weights[p][2]((to, tks[p]), lambda i, j, k, kidx=kidx: (j % njo, kidx(k))))
        args += [a, weights[p][0]]
    if has_extra:
        in_specs.append(pl.BlockSpec((tm, to), lambda i, j, k: (i, j)))
        args.append(extra)
    if dep is not None:
        in_specs.append(ANY)
        args.append(dep)
    return pl.pallas_call(
        body, name=name,
        out_shape=jax.ShapeDtypeStruct((M, slabs * Ko), out_dtype),
        grid=(M // tm, slabs * njo, nk),
        in_specs=in_specs,
        out_specs=pl.BlockSpec((tm, to), lambda i, j, k: (i, j)),
        scratch_shapes=[pltpu.VMEM((tm, to), F32)],
        compiler_params=_cparams("parallel", "parallel", "arbitrary"),
    )(*args)


def _mm_tn(a, b, *, name, a_fn=None, slabs=1):
    M, K = a.shape
    K //= slabs
    N = b.shape[1] // slabs
    tm = _tile(M, (2048, 1024, 512, 256))
    tk = _tile(K, (1024, 768, 512, 384, 256, 128))
    tn = _tile(N, (1024, 768, 512, 384, 256, 128))
    nmi = M // tm
    nm = slabs * nmi
    nki, nnj = K // tk, N // tn

    def body(a_ref, b_ref, o_ref, acc_ref):
        m = pl.program_id(2)
        av = a_ref[...]
        if a_fn is not None:
            av = a_fn(av)
        part = _dot_tn(av, b_ref[...])
        if nm == 1:
            o_ref[...] = part.astype(o_ref.dtype)
            return

        @pl.when(m == 0)
        def _():
            acc_ref[...] = part

        @pl.when((m > 0) & (m < nm - 1))
        def _():
            acc_ref[...] += part

        @pl.when(m == nm - 1)
        def _():
            o_ref[...] = (acc_ref[...] + part).astype(o_ref.dtype)

    return pl.pallas_call(
        body, name=name,
        out_shape=jax.ShapeDtypeStruct((K, N), BF16),
        grid=(K // tk, N // tn, nm),
        in_specs=[pl.BlockSpec((tm, tk), lambda i, j, m: (m % nmi, (m // nmi) * nki + i)),
                  pl.BlockSpec((tm, tn), lambda i, j, m: (m % nmi, (m // nmi) * nnj + j))],
        out_specs=pl.BlockSpec((tk, tn), lambda i, j, m: (i, j)),
        scratch_shapes=[pltpu.VMEM((tk, tn), F32)],
        compiler_params=_cparams("parallel", "parallel", "arbitrary"),
    )(a, b)


def _rmsnorm_fwd(h, g, *, name, dep=None, views=()):
    S, D = h.shape
    tm = _tile(S, (512, 256))
    nv = len(views)

    def body(h_ref, g_ref, *rest):
        n_out = 1 + nv
        outs = rest[len(rest) - n_out - (1 if nv else 0):len(rest) - (1 if nv else 0)]
        x = h_ref[...]
        r = lax.rsqrt(jnp.mean(x * x, axis=-1, keepdims=True) + RMS_EPS)
        y = x * r * g_ref[...]
        outs[0][...] = y.astype(BF16)
        if nv:
            scr = rest[-1]
            _to_chunks(scr, y)
            for v_ref, d in zip(outs[1:], views):
                _slabs_from_chunks(v_ref, scr, d, BF16)

    res = pl.pallas_call(
        body, name=name,
        out_shape=(jax.ShapeDtypeStruct((S, D), BF16),)
        + tuple(jax.ShapeDtypeStruct((S // d, d * D), BF16) for d in views),
        grid=(S // tm,),
        in_specs=[pl.BlockSpec((tm, D), lambda i: (i, 0)), pl.BlockSpec((1, D), lambda i: (0, 0))]
        + ([ANY] if dep is not None else []),
        out_specs=(pl.BlockSpec((tm, D), lambda i: (i, 0)),)
        + tuple(pl.BlockSpec((tm // d, d * D), lambda i: (i, 0)) for d in views),
        scratch_shapes=[_chunk_scratch(tm, D)] if nv else [],
        compiler_params=_cparams("parallel"),
    )(h, g, *([dep] if dep is not None else []))
    return res if nv else res[0]


def _rms_bwd_rows(x, g, dy):
    r = lax.rsqrt(jnp.mean(x * x, axis=-1, keepdims=True) + RMS_EPS)
    xh = x * r
    gy = dy * g
    dx = r * (gy - xh * jnp.mean(xh * gy, axis=-1, keepdims=True))
    return dx, dy * xh


def _rms_bwd(x, g, dn, dres, *, name, dn_views=()):
    S, D = x.shape
    tm = _tile(S, (256,))
    nv = len(dn_views)

    def body(x_ref, g_ref, dn_ref, dr_ref, *rest):
        v_refs = rest[:nv]
        dx_ref, dxb_ref, dg_ref = rest[nv:nv + 3]
        scr = rest[nv + 3:]
        i = pl.program_id(0)
        dn = dn_ref[...]
        for v_ref, s_ref, (_, d) in zip(v_refs, scr, dn_views):
            _chunks_from_slabs(s_ref, v_ref, d)
            dn = dn + _from_chunks(s_ref)
        dx, dgx = _rms_bwd_rows(x_ref[...], g_ref[...], dn)
        tot = dr_ref[...] + dx
        dx_ref[...] = tot
        dxb_ref[...] = tot.astype(BF16)

        @pl.when(i == 0)
        def _():
            dg_ref[...] = jnp.zeros_like(dg_ref)

        dg_ref[0:1, :] += jnp.sum(dgx, axis=0, keepdims=True)

    row = pl.BlockSpec((tm, D), lambda i: (i, 0))
    return pl.pallas_call(
        body, name=name,
        out_shape=(jax.ShapeDtypeStruct((S, D), F32), jax.ShapeDtypeStruct((S, D), BF16),
                   jax.ShapeDtypeStruct((8, D), F32)),
        grid=(S // tm,),
        in_specs=[row, pl.BlockSpec((1, D), lambda i: (0, 0)), row, row]
        + [pl.BlockSpec((tm // d, d * D), lambda i: (i, 0)) for _, d in dn_views],
        out_specs=(row, row, pl.BlockSpec((8, D), lambda i: (0, 0))),
        scratch_shapes=[_chunk_scratch(tm, D)] * nv,
        compiler_params=_cparams("arbitrary"),
    )(x, g, dn, dres, *[a for a, _ in dn_views])


def _mm_nt_rms_bwd(a, b, x, g, dres, *, name, dep=None):
    M, N = a.shape
    b, (D, _), b_spec = _weight(b)
    tm = _tile(M, (1024, 512, 256))
    tk = _tile(N, (1024, 512, 256, 128))
    nk = N // tk

    def body(a_ref, b_ref, x_ref, g_ref, dr_ref, *rest):
        dx_ref, dxb_ref, dg_ref, acc_ref = rest[-4:]
        i = pl.program_id(0)
        k = pl.program_id(1)
        part = _dot_nt(a_ref[...], b_ref[...])

        @pl.when((i == 0) & (k == 0))
        def _():
            dg_ref[...] = jnp.zeros_like(dg_ref)

        @pl.when(k == 0)
        def _():
            acc_ref[...] = part

        @pl.when((k > 0) & (k < nk - 1))
        def _():
            acc_ref[...] += part

        @pl.when(k == nk - 1)
        def _():
            dn = part if nk == 1 else acc_ref[...] + part
            dx, dgx = _rms_bwd_rows(x_ref[...], g_ref[...], dn)
            tot = dr_ref[...] + dx
            dx_ref[...] = tot
            dxb_ref[...] = tot.astype(BF16)
            dg_ref[0:1, :] += jnp.sum(dgx, axis=0, keepdims=True)

    row = pl.BlockSpec((tm, D), lambda i, k: (i, 0))
    in_specs = [pl.BlockSpec((tm, tk), lambda i, k: (i, k)), b_spec((D, tk), lambda i, k: (0, k)),
                row, pl.BlockSpec((1, D), lambda i, k: (0, 0)), row]
    args = [a, b, x, g, dres]
    if dep is not None:
        in_specs.append(ANY)
        args.append(dep)
    return pl.pallas_call(
        body, name=name,
        out_shape=(jax.ShapeDtypeStruct((M, D), F32), jax.ShapeDtypeStruct((M, D), BF16),
                   jax.ShapeDtypeStruct((8, D), F32)),
        grid=(M // tm, nk),
        in_specs=in_specs,
        out_specs=(row, row, pl.BlockSpec((8, D), lambda i, k: (0, 0))),
        scratch_shapes=[pltpu.VMEM((tm, D), F32)],
        compiler_params=_cparams("arbitrary", "arbitrary"),
    )(*args)


def _loss_bwd(h, target, g, *, name):
    S, D = h.shape
    tm = _tile(S, (256,))

    def body(h_ref, t_ref, g_ref, dx_ref, dxb_ref, acc_ref):
        i = pl.program_id(0)
        x = h_ref[...]
        gv = g_ref[...]
        r = lax.rsqrt(jnp.mean(x * x, axis=-1, keepdims=True) + RMS_EPS)
        err = x * r * gv - t_ref[...]
        dx, dgx = _rms_bwd_rows(x, gv, err * (1.0 / D))
        dx_ref[...] = dx
        dxb_ref[...] = dx.astype(BF16)

        @pl.when(i == 0)
        def _():
            acc_ref[...] = jnp.zeros_like(acc_ref)

        acc_ref[0:1, :] += jnp.sum(dgx, axis=0, keepdims=True)
        acc_ref[1:2, :] += jnp.sum(err * err, axis=0, keepdims=True)

    row = pl.BlockSpec((tm, D), lambda i: (i, 0))
    return pl.pallas_call(
        body, name=name,
        out_shape=(jax.ShapeDtypeStruct((S, D), F32), jax.ShapeDtypeStruct((S, D), BF16),
                   jax.ShapeDtypeStruct((8, D), F32)),
        grid=(S // tm,),
        in_specs=[row, row, pl.BlockSpec((1, D), lambda i: (0, 0))],
        out_specs=(row, row, pl.BlockSpec((8, D), lambda i: (0, 0))),
        compiler_params=_cparams("arbitrary"),
    )(h, target, g)


SUBLANES = 8
CONV_ROWS = 64


def _build_shifted(ext_ref, rot_ref, ts, shifts=tuple(range(1, SUBLANES))):
    rows = ts + HALO - SUBLANES
    for j in shifts:
        rot_ref[j, 0:rows, :] = ext_ref[j:j + rows, :]


def _shifted(ext_ref, rot_ref, off, r0, nrows, cs):
    q, j = divmod(off, SUBLANES)
    start = SUBLANES * q + r0
    if j == 0:
        return ext_ref[start:start + nrows, cs]
    return rot_ref[j, start:start + nrows, cs]


def _conv_fwd(uc, conv_w, conv_b, ln_g, ln_b, *, name):
    S, C2 = uc.shape
    C = C2 // 2
    ts = _tile(S, (256,))
    per = ts // HALO

    def body(cur_ref, halo_ref, w_ref, b_ref, g_ref, beta_ref, ya_ref, h_ref, ct_ref, ext_ref, rot_ref):
        i = pl.program_id(0)
        hh = halo_ref[:, 0:C] * jax.nn.sigmoid(halo_ref[:, C:C2])
        ext_ref[0:HALO, :] = jnp.where(i == 0, 0.0, hh)
        hc = cur_ref[:, 0:C] * jax.nn.sigmoid(cur_ref[:, C:C2])
        ext_ref[HALO:HALO + ts, :] = hc
        h_ref[...] = hc
        _build_shifted(ext_ref, rot_ref, ts)
        for c0 in range(0, C, LANES):
            cs = slice(c0, c0 + LANES)
            for r0 in range(0, ts, CONV_ROWS):
                acc = jnp.zeros((CONV_ROWS, LANES), F32)
                for k in range(CONV_A_WIDTH):
                    acc = acc + w_ref[k:k + 1, cs] * _shifted(ext_ref, rot_ref, k + 2, r0, CONV_ROWS, cs)
                ct_ref[r0:r0 + CONV_ROWS, cs] = acc + b_ref[:, cs]
        ct = ct_ref[...]
        mu = jnp.mean(ct, axis=-1, keepdims=True)
        xc = ct - mu
        var = jnp.mean(xc * xc, axis=-1, keepdims=True)
        l = xc * lax.rsqrt(var + LN_EPS) * g_ref[...] + beta_ref[...]
        ya_ref[...] = (l * jax.nn.sigmoid(l)).astype(ya_ref.dtype)

    vec = pl.BlockSpec((1, C), lambda i: (0, 0))
    row = pl.BlockSpec((ts, C), lambda i: (i, 0))
    return pl.pallas_call(
        body, name=name,
        out_shape=(jax.ShapeDtypeStruct((S, C), BF16), jax.ShapeDtypeStruct((S, C), F32),
                   jax.ShapeDtypeStruct((S, C), F32)),
        grid=(S // ts,),
        in_specs=[pl.BlockSpec((ts, C2), lambda i: (i, 0)),
                  pl.BlockSpec((HALO, C2), lambda i: (jnp.maximum(i * per - 1, 0), 0)),
                  pl.BlockSpec((CONV_A_WIDTH, C), lambda i: (0, 0)), vec, vec, vec],
        out_specs=(row, row, row),
        scratch_shapes=[pltpu.VMEM((HALO + ts, C), F32), pltpu.VMEM((8, HALO + ts, C), F32)],
        compiler_params=_cparams("parallel"),
    )(uc, uc, conv_w, conv_b, ln_g, ln_b)


CONV_ACC_ROWS = 40


def _conv_bwd_ln(ct, dcat, hglu, ln_g, ln_b, *, name):
    S, C = ct.shape
    CW = dcat.shape[1]
    ts = _tile(S, (256,))
    per = ts // HALO

    def body(ct_ref, dcat_ref, hc_ref, hh_ref, g_ref, beta_ref, dc_ref, acc_ref, ext_ref, rot_ref):
        i = pl.program_id(0)
        ct = ct_ref[...]
        gv = g_ref[...]
        mu = jnp.mean(ct, axis=-1, keepdims=True)
        xc = ct - mu
        rstd = lax.rsqrt(jnp.mean(xc * xc, axis=-1, keepdims=True) + LN_EPS)
        xh = xc * rstd
        l = xh * gv + beta_ref[...]
        sg = jax.nn.sigmoid(l)
        dl = dcat_ref[:, 0:C] * (sg * (1.0 + l * (1.0 - sg)))
        dxh = dl * gv
        dc = rstd * (dxh - jnp.mean(dxh, axis=-1, keepdims=True)
                     - xh * jnp.mean(dxh * xh, axis=-1, keepdims=True))
        dc_ref[...] = dc

        @pl.when(i == 0)
        def _():
            acc_ref[...] = jnp.zeros_like(acc_ref)

        acc_ref[32:33, :] += jnp.sum(dc, axis=0, keepdims=True)
        acc_ref[33:34, :] += jnp.sum(dl * xh, axis=0, keepdims=True)
        acc_ref[34:35, :] += jnp.sum(dl, axis=0, keepdims=True)
        ext_ref[0:HALO, :] = jnp.where(i == 0, 0.0, hh_ref[...])
        ext_ref[HALO:HALO + ts, :] = hc_ref[...]
        _build_shifted(ext_ref, rot_ref, ts)
        for c0 in range(0, C, LANES):
            cs = slice(c0, c0 + LANES)
            dcc = dc_ref[:, cs]
            for k in range(CONV_A_WIDTH):
                acc_ref[k:k + 1, cs] += jnp.sum(dcc * _shifted(ext_ref, rot_ref, k + 2, 0, ts, cs),
                                                axis=0, keepdims=True)

    vec = pl.BlockSpec((1, C), lambda i: (0, 0))
    row = pl.BlockSpec((ts, C), lambda i: (i, 0))
    return pl.pallas_call(
        body, name=name,
        out_shape=(jax.ShapeDtypeStruct((S, C), F32), jax.ShapeDtypeStruct((CONV_ACC_ROWS, C), F32)),
        grid=(S // ts,),
        in_specs=[row, pl.BlockSpec((ts, CW), lambda i: (i, 0)), row,
                  pl.BlockSpec((HALO, C), lambda i: (jnp.maximum(i * per - 1, 0), 0)), vec, vec],
        out_specs=(row, pl.BlockSpec((CONV_ACC_ROWS, C), lambda i: (0, 0))),
        scratch_shapes=[pltpu.VMEM((HALO + ts, C), F32), pltpu.VMEM((8, HALO + ts, C), F32)],
        compiler_params=_cparams("arbitrary"),
    )(ct, dcat, hglu, hglu, ln_g, ln_b)


def _conv_bwd_in(dc, uc, conv_w, *, name):
    S, C = dc.shape
    C2 = 2 * C
    ts = _tile(S, (256,))
    per = ts // HALO
    nt = S // ts

    def body(dc_ref, dn_ref, uc_ref, w_ref, du_ref, ext_ref, rot_ref):
        i = pl.program_id(0)
        ext_ref[0:ts, :] = dc_ref[...]
        ext_ref[ts:ts + HALO, :] = jnp.where(i == nt - 1, 0.0, dn_ref[...])
        _build_shifted(ext_ref, rot_ref, ts)
        for c0 in range(0, C, LANES):
            cs = slice(c0, c0 + LANES)
            gs = slice(C + c0, C + c0 + LANES)
            for r0 in range(0, ts, CONV_ROWS):
                rs = slice(r0, r0 + CONV_ROWS)
                acc = jnp.zeros((CONV_ROWS, LANES), F32)
                for k in range(CONV_A_WIDTH):
                    acc = acc + w_ref[k:k + 1, cs] * _shifted(ext_ref, rot_ref, 30 - k, r0, CONV_ROWS, cs)
                sg = jax.nn.sigmoid(uc_ref[rs, gs])
                du_ref[rs, cs] = (acc * sg).astype(du_ref.dtype)
                du_ref[rs, gs] = (acc * uc_ref[rs, cs] * sg * (1.0 - sg)).astype(du_ref.dtype)

    return pl.pallas_call(
        body, name=name,
        out_shape=jax.ShapeDtypeStruct((S, C2), BF16),
        grid=(nt,),
        in_specs=[pl.BlockSpec((ts, C), lambda i: (i, 0)),
                  pl.BlockSpec((HALO, C), lambda i: (jnp.minimum((i + 1) * per, S // HALO - 1), 0)),
                  pl.BlockSpec((ts, C2), lambda i: (i, 0)),
                  pl.BlockSpec((CONV_A_WIDTH, C), lambda i: (0, 0))],
        out_specs=pl.BlockSpec((ts, C2), lambda i: (i, 0)),
        scratch_shapes=[pltpu.VMEM((ts + HALO, C), F32), pltpu.VMEM((8, ts + HALO, C), F32)],
        compiler_params=_cparams("parallel"),
    )(dc, dc, uc, conv_w)


SC_SHIFTS_BACK = ((HALO - 2) % SUBLANES, (HALO - 1) % SUBLANES)
SC_SHIFTS_AHEAD = (1, 2)


def _sc_fwd(u3, conv_w, *, name):
    S, W3 = u3.shape
    W = W3 // 3
    ts = _tile(S, (256,))
    per = ts // HALO

    def body(cur_ref, halo_ref, w_ref, y_ref, ext_ref, rot_ref):
        i = pl.program_id(0)
        cvh = halo_ref[:, W:2 * W].astype(F32) * halo_ref[:, 2 * W:W3].astype(F32)
        ext_ref[0:HALO, :] = jnp.where(i == 0, 0.0, cvh)
        ext_ref[HALO:HALO + ts, :] = cur_ref[:, W:2 * W].astype(F32) * cur_ref[:, 2 * W:W3].astype(F32)
        _build_shifted(ext_ref, rot_ref, ts, SC_SHIFTS_BACK)
        for c0 in range(0, W, LANES):
            cs = slice(c0, c0 + LANES)
            for r0 in range(0, ts, CONV_ROWS):
                rs = slice(r0, r0 + CONV_ROWS)
                k = (w_ref[0:1, cs] * _shifted(ext_ref, rot_ref, HALO - 2, r0, CONV_ROWS, cs)
                     + w_ref[1:2, cs] * _shifted(ext_ref, rot_ref, HALO - 1, r0, CONV_ROWS, cs)
                     + w_ref[2:3, cs] * _shifted(ext_ref, rot_ref, HALO, r0, CONV_ROWS, cs))
                y_ref[rs, cs] = (cur_ref[rs, cs].astype(F32) * k).astype(y_ref.dtype)

    return pl.pallas_call(
        body, name=name,
        out_shape=jax.ShapeDtypeStruct((S, W), BF16),
        grid=(S // ts,),
        in_specs=[pl.BlockSpec((ts, W3), lambda i: (i, 0)),
                  pl.BlockSpec((HALO, W3), lambda i: (jnp.maximum(i * per - 1, 0), 0)),
                  pl.BlockSpec((SC_CONV_WIDTH, W), lambda i: (0, 0))],
        out_specs=pl.BlockSpec((ts, W), lambda i: (i, 0)),
        scratch_shapes=[pltpu.VMEM((HALO + ts, W), F32), pltpu.VMEM((8, HALO + ts, W), F32)],
        compiler_params=_cparams("parallel"),
    )(u3, u3, conv_w)


def _sc_bwd(u3, dy, conv_w, *, name):
    S, W3 = u3.shape
    W = W3 // 3
    ts = _tile(S, (256,))
    per = ts // HALO
    nt = S // ts

    def body(cur_ref, prev_ref, next_ref, dy_ref, dyn_ref, w_ref, du_ref, dw_ref, cv_ext, dk_ext, cv_rot, dk_rot):
        i = pl.program_id(0)
        cvh = prev_ref[:, W:2 * W].astype(F32) * prev_ref[:, 2 * W:W3].astype(F32)
        cv_ext[0:HALO, :] = jnp.where(i == 0, 0.0, cvh)
        cv_ext[HALO:HALO + ts, :] = cur_ref[:, W:2 * W].astype(F32) * cur_ref[:, 2 * W:W3].astype(F32)
        dk_ext[0:ts, :] = dy_ref[...] * cur_ref[:, 0:W].astype(F32)
        dk_ext[ts:ts + HALO, :] = jnp.where(i == nt - 1, 0.0, dyn_ref[...] * next_ref[:, 0:W].astype(F32))
        _build_shifted(cv_ext, cv_rot, ts, SC_SHIFTS_BACK)
        _build_shifted(dk_ext, dk_rot, ts, SC_SHIFTS_AHEAD)

        @pl.when(i == 0)
        def _():
            dw_ref[...] = jnp.zeros_like(dw_ref)

        for c0 in range(0, W, LANES):
            cs = slice(c0, c0 + LANES)
            w0, w1, w2 = w_ref[0:1, cs], w_ref[1:2, cs], w_ref[2:3, cs]
            sums = [jnp.zeros((1, LANES), F32)] * SC_CONV_WIDTH
            for r0 in range(0, ts, CONV_ROWS):
                rs = slice(r0, r0 + CONV_ROWS)
                cv2 = _shifted(cv_ext, cv_rot, HALO - 2, r0, CONV_ROWS, cs)
                cv1 = _shifted(cv_ext, cv_rot, HALO - 1, r0, CONV_ROWS, cs)
                cv0 = _shifted(cv_ext, cv_rot, HALO, r0, CONV_ROWS, cs)
                dk = dk_ext[rs, cs]
                dcv = (w2 * dk + w1 * _shifted(dk_ext, dk_rot, 1, r0, CONV_ROWS, cs)
                       + w0 * _shifted(dk_ext, dk_rot, 2, r0, CONV_ROWS, cs))
                du_ref[rs, cs] = (dy_ref[rs, cs] * (w0 * cv2 + w1 * cv1 + w2 * cv0)).astype(du_ref.dtype)
                du_ref[rs, W + c0:W + c0 + LANES] = (
                    dcv * cur_ref[rs, 2 * W + c0:2 * W + c0 + LANES].astype(F32)).astype(du_ref.dtype)
                du_ref[rs, 2 * W + c0:2 * W + c0 + LANES] = (
                    dcv * cur_ref[rs, W + c0:W + c0 + LANES].astype(F32)).astype(du_ref.dtype)
                for t, cvt in enumerate((cv2, cv1, cv0)):
                    sums[t] = sums[t] + jnp.sum(dk * cvt, axis=0, keepdims=True)
            for t in range(SC_CONV_WIDTH):
                dw_ref[t:t + 1, cs] += sums[t]

    nxt = lambda i: (jnp.minimum((i + 1) * per, S // HALO - 1), 0)
    return pl.pallas_call(
        body, name=name,
        out_shape=(jax.ShapeDtypeStruct((S, W3), BF16), jax.ShapeDtypeStruct((8, W), F32)),
        grid=(nt,),
        in_specs=[pl.BlockSpec((ts, W3), lambda i: (i, 0)),
                  pl.BlockSpec((HALO, W3), lambda i: (jnp.maximum(i * per - 1, 0), 0)),
                  pl.BlockSpec((HALO, W3), nxt),
                  pl.BlockSpec((ts, W), lambda i: (i, 0)),
                  pl.BlockSpec((HALO, W), nxt),
                  pl.BlockSpec((SC_CONV_WIDTH, W), lambda i: (0, 0))],
        out_specs=(pl.BlockSpec((ts, W3), lambda i: (i, 0)), pl.BlockSpec((8, W), lambda i: (0, 0))),
        scratch_shapes=[pltpu.VMEM((HALO + ts, W), F32), pltpu.VMEM((ts + HALO, W), F32),
                        pltpu.VMEM((8, HALO + ts, W), F32), pltpu.VMEM((8, ts + HALO, W), F32)],
        compiler_params=_cparams("arbitrary"),
    )(u3, u3, u3, dy, dy, conv_w)


def _t5_causal_bucket(n):
    max_exact = NUM_BUCKETS // 2
    nf = jnp.maximum(n, 1).astype(F32)
    large = max_exact + (jnp.log(nf / max_exact) / math.log(REL_MAX_DISTANCE / max_exact)
                         * (NUM_BUCKETS - max_exact)).astype(jnp.int32)
    return jnp.where(n < max_exact, n, jnp.minimum(large, NUM_BUCKETS - 1))


def _bucket_tables():
    steps = ATTN_BLK
    m = jnp.arange(steps)[:, None] + steps - jnp.arange(2 * steps)[None, :]
    return jnp.stack([_t5_causal_bucket(jnp.clip(m, 0, steps) * dil).astype(F32) for _, dil in DILATED_GROUPS])


def _bias_tables(rel_bias, buckets, *, name):
    steps = ATTN_BLK

    def body(tab_ref, bk_ref, o_ref):
        g = pl.program_id(0)
        bk = bk_ref[0]
        a_idx = lax.broadcasted_iota(jnp.int32, (steps, 2 * steps), 0)
        c_idx = lax.broadcasted_iota(jnp.int32, (steps, 2 * steps), 1)
        m = a_idx + steps - c_idx
        band = (m >= 0) & (m <= steps)
        band_first = band & (c_idx >= steps)
        for h in range(HEADS_PER_GROUP):
            bias = jnp.zeros((steps, 2 * steps), F32)
            for b in range(NUM_BUCKETS):
                bias = jnp.where(bk == float(b), tab_ref[b, g * HEADS_PER_GROUP + h], bias)
            o_ref[0, 0, h] = jnp.where(band_first, bias, NEG_INF)
            o_ref[0, 1, h] = jnp.where(band, bias, NEG_INF)

    return pl.pallas_call(
        body, name=name,
        out_shape=jax.ShapeDtypeStruct((N_GROUPS, 2, HEADS_PER_GROUP, steps, 2 * steps), F32),
        grid=(N_GROUPS,),
        in_specs=[pl.BlockSpec(memory_space=pltpu.SMEM),
                  pl.BlockSpec((1, steps, 2 * steps), lambda g: (g, 0, 0))],
        out_specs=pl.BlockSpec((1, 2, HEADS_PER_GROUP, steps, 2 * steps), lambda g: (g, 0, 0, 0, 0)),
        compiler_params=_cparams("parallel"),
    )(rel_bias, buckets)


def _lane_is_low():
    return lax.broadcasted_iota(jnp.int32, (1, LANES), 1) < HEAD_DIM


def _stack_heads(x2, low):
    zero = jnp.zeros_like(x2)
    return jnp.concatenate([jnp.where(low, x2, zero), jnp.where(low, zero, x2)], axis=0)


def _qkv_specs(nb):
    nqb = GROUP_QKV // ATTN_OUT

    def spec(t, prev):
        def idx(r, n):
            nn = jnp.minimum(n, nb - 1)
            row = jnp.maximum(nn - 1, 0) if prev else nn
            return (row, r * nqb + t)
        return pl.BlockSpec((ATTN_BLK, ATTN_OUT), idx)

    return [spec(0, False), spec(1, False), spec(1, True), spec(2, False), spec(2, True)]


def _attn_fwd(uv, bias, g, d, *, name):
    rows = uv.shape[0]

    def body(q_ref, kc_ref, kp_ref, vc_ref, vp_ref, bias_ref, o_ref, l_ref):
        n = pl.program_id(1)
        sel = jnp.minimum(n, 1)
        low = _lane_is_low()
        for hp in range(HEADS_PER_GROUP // 2):
            sl = slice(hp * LANES, (hp + 1) * LANES)
            q2 = q_ref[:, sl]
            k2 = jnp.concatenate([kp_ref[:, sl], kc_ref[:, sl]], axis=0)
            v2 = jnp.concatenate([vp_ref[:, sl], vc_ref[:, sl]], axis=0)
            s = _dot_nt(_stack_heads(q2 * (HEAD_DIM ** -0.5), low), k2)
            ps, dens, lses = [], [], []
            for hh in range(2):
                logits = s[hh * ATTN_BLK:(hh + 1) * ATTN_BLK] + bias_ref[sel, 2 * hp + hh]
                mx = jnp.max(logits, axis=-1, keepdims=True)
                p = jnp.exp(logits - mx)
                den = jnp.sum(p, axis=-1, keepdims=True)
                ps.append(p.astype(BF16))
                dens.append(den)
                lses.append(jnp.broadcast_to(mx + jnp.log(den), (ATTN_BLK, LANES)))
            pv = jnp.dot(jnp.concatenate(ps, axis=0), v2, preferred_element_type=F32)
            o_ref[:, sl] = jnp.where(low, pv[0:ATTN_BLK] / dens[0], pv[ATTN_BLK:2 * ATTN_BLK] / dens[1])
            l_ref[:, sl] = jnp.where(low, lses[0], lses[1])

    out_spec = pl.BlockSpec((ATTN_BLK, ATTN_OUT), lambda r, n: (n, r))
    return pl.pallas_call(
        body, name=name,
        out_shape=(jax.ShapeDtypeStruct((rows, d * ATTN_OUT), F32),) * 2,
        grid=(d, rows // ATTN_BLK),
        in_specs=_qkv_specs(rows // ATTN_BLK) + [pl.BlockSpec((None, 2, HEADS_PER_GROUP, ATTN_BLK, 2 * ATTN_BLK),
                                                              lambda r, n: (g, 0, 0, 0, 0))],
        out_specs=(out_spec, out_spec),
        compiler_params=_cparams("parallel", "parallel"),
    )(uv, uv, uv, uv, uv, bias)


def _chunk_scratch(n, width):
    return pltpu.VMEM((width // LANES, n, LANES), F32)


def _to_chunks(scr, val):
    for c in range(scr.shape[0]):
        scr[c] = val[:, c * LANES:(c + 1) * LANES]


def _from_chunks(scr):
    return jnp.concatenate([scr[c] for c in range(scr.shape[0])], axis=1)


def _slabs_from_chunks(dst_ref, scr, d, dtype):
    nc, n, _ = scr.shape
    for r in range(d):
        for c in range(nc):
            col = r * nc * LANES + c * LANES
            dst_ref[:, col:col + LANES] = scr[c, pl.ds(r, n // d, stride=d), :].astype(dtype)


def _chunks_from_slabs(scr, src_ref, d):
    nc, n, _ = scr.shape
    for r in range(d):
        for c in range(nc):
            col = r * nc * LANES + c * LANES
            scr[c, pl.ds(r, n // d, stride=d), :] = src_ref[:, col:col + LANES]


def _attn_merge(outs, lses, ya, *, name):
    S, C = ya.shape
    tm = _tile(S, (256,))
    dils = [dil for _, dil in DILATED_GROUPS]

    def body(o0, o1, o2, l0, l1, l2, ya_ref, cat_ref, out_ref, lse_ref, so1, so2, sl1, sl2):
        _chunks_from_slabs(so1, o1, dils[1])
        _chunks_from_slabs(so2, o2, dils[2])
        _chunks_from_slabs(sl1, l1, dils[1])
        _chunks_from_slabs(sl2, l2, dils[2])
        a0, a1, a2 = l0[...], _from_chunks(sl1), _from_chunks(sl2)
        m = jnp.maximum(jnp.maximum(a0, a1), a2)
        e0, e1, e2 = jnp.exp(a0 - m), jnp.exp(a1 - m), jnp.exp(a2 - m)
        den = e0 + e1 + e2
        out = (e0 * o0[...] + e1 * _from_chunks(so1) + e2 * _from_chunks(so2)) / den
        out_ref[...] = out
        lse_ref[...] = m + jnp.log(den)
        cat_ref[:, 0:C] = ya_ref[...]
        cat_ref[:, C:C + ATTN_OUT] = out.astype(cat_ref.dtype)

    blk = pl.BlockSpec((tm, ATTN_OUT), lambda i: (i, 0))
    vblk = [pl.BlockSpec((tm // d, d * ATTN_OUT), lambda i: (i, 0)) for d in dils]
    assert dils[0] == 1
    return pl.pallas_call(
        body, name=name,
        out_shape=(jax.ShapeDtypeStruct((S, C + ATTN_OUT), BF16), jax.ShapeDtypeStruct((S, ATTN_OUT), F32),
                   jax.ShapeDtypeStruct((S, ATTN_OUT), F32)),
        grid=(S // tm,),
        in_specs=vblk + vblk + [pl.BlockSpec((tm, C), lambda i: (i, 0))],
        out_specs=(pl.BlockSpec((tm, C + ATTN_OUT), lambda i: (i, 0)), blk, blk),
        scratch_shapes=[_chunk_scratch(tm, ATTN_OUT)] * 4,
        compiler_params=_cparams("parallel"),
    )(*outs, *lses, ya)


def _attn_prep(dcat, outf, lse, *, name):
    S, CW = dcat.shape
    C = CW - ATTN_OUT
    tm = _tile(S, (256,))
    dils = [dil for _, dil in DILATED_GROUPS]
    assert dils[0] == 1
    ones = np.kron(np.eye(HEADS_PER_GROUP, dtype=np.float32), np.ones((HEAD_DIM, HEAD_DIM), np.float32))

    nviews = 3 * (len(dils) - 1)

    def body(dcat_ref, out_ref, l_ref, ones_ref, dyb_ref, dl_ref, *rest):
        views = rest[:nviews]
        s_dyb, s_dl, s_l = rest[nviews:]
        dyb = dcat_ref[:, C:CW]
        dyb_ref[...] = dyb.astype(BF16)
        prod = dyb * out_ref[...]
        ov = ones_ref[...]
        hi, mid, lo = _split_bf16(prod)
        delta = (jnp.dot(hi, ov, preferred_element_type=F32)
                 + jnp.dot(mid, ov, preferred_element_type=F32)
                 + jnp.dot(lo, ov, preferred_element_type=F32))
        dl_ref[...] = delta
        _to_chunks(s_dyb, dyb)
        _to_chunks(s_dl, delta)
        _to_chunks(s_l, l_ref[...])
        for gi, d in enumerate(dils[1:]):
            dyb_v, dl_v, l_v = views[3 * gi:3 * gi + 3]
            _slabs_from_chunks(dyb_v, s_dyb, d, BF16)
            _slabs_from_chunks(dl_v, s_dl, d, F32)
            _slabs_from_chunks(l_v, s_l, d, F32)

    blk = pl.BlockSpec((tm, ATTN_OUT), lambda i: (i, 0))
    view_shapes, view_specs = [], []
    for d in dils[1:]:
        for dt in (BF16, F32, F32):
            view_shapes.append(jax.ShapeDtypeStruct((S // d, d * ATTN_OUT), dt))
            view_specs.append(pl.BlockSpec((tm // d, d * ATTN_OUT), lambda i: (i, 0)))
    res = pl.pallas_call(
        body, name=name,
        out_shape=(jax.ShapeDtypeStruct((S, ATTN_OUT), BF16), jax.ShapeDtypeStruct((S, ATTN_OUT), F32),
                   *view_shapes),
        grid=(S // tm,),
        in_specs=[pl.BlockSpec((tm, CW), lambda i: (i, 0)), blk, blk,
                  pl.BlockSpec((ATTN_OUT, ATTN_OUT), lambda i: (0, 0))],
        out_specs=(blk, blk, *view_specs),
        scratch_shapes=[_chunk_scratch(tm, ATTN_OUT)] * 3,
        compiler_params=_cparams("parallel"),
    )(dcat, outf, lse, jnp.asarray(ones, BF16))
    return [(res[0], res[1], lse)] + [tuple(res[2 + 3 * gi:5 + 3 * gi]) for gi in range(len(dils) - 1)]


def _attn_bwd(uv, dov, lv, dv_, bias, g, d, *, name):
    rows = uv.shape[0]
    nb = rows // ATTN_BLK
    scale = HEAD_DIM ** -0.5

    def body(q_ref, kc_ref, kp_ref, vc_ref, vp_ref, do_ref, l_ref, dl_ref, bias_ref,
             out_ref, db_ref, dq_s, dk_s, dv_s):
        r = pl.program_id(0)
        n = pl.program_id(1)
        low = _lane_is_low()

        @pl.when((r == 0) & (n == 0))
        def _():
            db_ref[...] = jnp.zeros_like(db_ref)

        @pl.when(n == 0)
        def _():
            dq_s[...] = jnp.zeros_like(dq_s)
            dk_s[...] = jnp.zeros_like(dk_s)
            dv_s[...] = jnp.zeros_like(dv_s)

        @pl.when(n < nb)
        def _():
            sel = jnp.minimum(n, 1)
            lane = lax.broadcasted_iota(jnp.int32, (1, LANES), 1)
            for hp in range(HEADS_PER_GROUP // 2):
                sl = slice(hp * LANES, (hp + 1) * LANES)
                q2 = q_ref[:, sl]
                do2 = do_ref[:, sl]
                k2 = jnp.concatenate([kp_ref[:, sl], kc_ref[:, sl]], axis=0)
                v2 = jnp.concatenate([vp_ref[:, sl], vc_ref[:, sl]], axis=0)
                lse2 = l_ref[:, sl]
                dl2 = dl_ref[:, sl]
                s = _dot_nt(_stack_heads(q2 * scale, low), k2)
                dp = _dot_nt(_stack_heads(do2, low), v2)
                pbs, dsbs = [], []
                for hh in range(2):
                    rows = slice(hh * ATTN_BLK, (hh + 1) * ATTN_BLK)
                    one = lane == hh * HEAD_DIM
                    lse_col = jnp.sum(jnp.where(one, lse2, 0.0), axis=-1, keepdims=True)
                    dl_col = jnp.sum(jnp.where(one, dl2, 0.0), axis=-1, keepdims=True)
                    p = jnp.exp(s[rows] + bias_ref[sel, 2 * hp + hh] - lse_col)
                    ds = p * (dp[rows] - dl_col)
                    db_ref[2 * hp + hh] += ds
                    pbs.append(p.astype(BF16))
                    dsbs.append((ds * scale).astype(BF16))
                dq = jnp.dot(jnp.concatenate(dsbs, axis=0), k2, preferred_element_type=F32)
                dk = _dot_tn(jnp.concatenate(dsbs, axis=1), q2)
                dv = _dot_tn(jnp.concatenate(pbs, axis=1), do2)
                nk2 = 2 * ATTN_BLK
                dq2 = jnp.where(low, dq[0:ATTN_BLK], dq[ATTN_BLK:nk2])
                dk2 = jnp.where(low, dk[0:nk2], dk[nk2:2 * nk2])
                dv2 = jnp.where(low, dv[0:nk2], dv[nk2:2 * nk2])
                out_ref[:, sl] = dq_s[:, sl].astype(out_ref.dtype)
                dq_s[:, sl] = dq2
                ksl = slice(ATTN_OUT + hp * LANES, ATTN_OUT + (hp + 1) * LANES)
                vsl = slice(2 * ATTN_OUT + hp * LANES, 2 * ATTN_OUT + (hp + 1) * LANES)
                out_ref[:, ksl] = (dk_s[:, sl] + dk2[0:ATTN_BLK]).astype(out_ref.dtype)
                dk_s[:, sl] = dk2[ATTN_BLK:2 * ATTN_BLK]
                out_ref[:, vsl] = (dv_s[:, sl] + dv2[0:ATTN_BLK]).astype(out_ref.dtype)
                dv_s[:, sl] = dv2[ATTN_BLK:2 * ATTN_BLK]

        @pl.when(n == nb)
        def _():
            out_ref[:, 0:ATTN_OUT] = dq_s[...].astype(out_ref.dtype)
            out_ref[:, ATTN_OUT:2 * ATTN_OUT] = dk_s[...].astype(out_ref.dtype)
            out_ref[:, 2 * ATTN_OUT:GROUP_QKV] = dv_s[...].astype(out_ref.dtype)

    rowblk = pl.BlockSpec((ATTN_BLK, ATTN_OUT), lambda r, n: (jnp.minimum(n, nb - 1), r))
    return pl.pallas_call(
        body, name=name,
        out_shape=(jax.ShapeDtypeStruct((rows, d * GROUP_QKV), BF16),
                   jax.ShapeDtypeStruct((HEADS_PER_GROUP, ATTN_BLK, 2 * ATTN_BLK), F32)),
        grid=(d, nb + 1),
        in_specs=_qkv_specs(nb) + [
            rowblk, rowblk, rowblk,
            pl.BlockSpec((None, 2, HEADS_PER_GROUP, ATTN_BLK, 2 * ATTN_BLK), lambda r, n: (g, 0, 0, 0, 0))],
        out_specs=(pl.BlockSpec((ATTN_BLK, GROUP_QKV), lambda r, n: (jnp.maximum(n - 1, 0), r)),
                   pl.BlockSpec((HEADS_PER_GROUP, ATTN_BLK, 2 * ATTN_BLK), lambda r, n: (0, 0, 0))),
        scratch_shapes=[pltpu.VMEM((ATTN_BLK, ATTN_OUT), F32)] * 3,
        compiler_params=_cparams("arbitrary", "arbitrary"),
    )(uv, uv, uv, uv, uv, dov, lv, dv_, bias)


def _split_bf16(x):
    hi = x.astype(BF16)
    r1 = x - hi.astype(F32)
    mid = r1.astype(BF16)
    lo = (r1 - mid.astype(F32)).astype(BF16)
    return hi, mid, lo


RELBIAS_CHUNK = 4096


def _relbias_reduce(dbs, buckets, *, name):
    flat = ATTN_BLK * 2 * ATTN_BLK
    dbf = jnp.stack([db.reshape(HEADS_PER_GROUP, flat) for db in dbs])
    bkf = buckets.reshape(N_GROUPS, 1, flat)

    def body(db_ref, bk_ref, o_ref):
        c = pl.program_id(1)
        rows = lax.broadcasted_iota(jnp.int32, (LANES, RELBIAS_CHUNK), 0).astype(F32)
        onehot = jnp.where(rows == bk_ref[0], 1.0, 0.0).astype(BF16)
        hi, mid, lo = _split_bf16(db_ref[0])
        part = _dot_nt(hi, onehot) + _dot_nt(mid, onehot) + _dot_nt(lo, onehot)

        @pl.when(c == 0)
        def _():
            o_ref[0] = part

        @pl.when(c > 0)
        def _():
            o_ref[0] += part

    return pl.pallas_call(
        body, name=name,
        out_shape=jax.ShapeDtypeStruct((N_GROUPS, HEADS_PER_GROUP, LANES), F32),
        grid=(N_GROUPS, flat // RELBIAS_CHUNK),
        in_specs=[pl.BlockSpec((1, HEADS_PER_GROUP, RELBIAS_CHUNK), lambda g, c: (g, 0, c)),
                  pl.BlockSpec((1, 1, RELBIAS_CHUNK), lambda g, c: (g, 0, c))],
        out_specs=pl.BlockSpec((1, HEADS_PER_GROUP, LANES), lambda g, c: (g, 0, 0)),
        compiler_params=_cparams("parallel", "arbitrary"),
    )(dbf, bkf)


def _my_position():
    x, y, c = lax.axis_index("x"), lax.axis_index("y"), lax.axis_index("c")
    return x, y, c


def _linear(pos):
    return 4 * pos[0] + 2 * pos[1] + pos[2]


def _peer(pos, k):
    x, y, c = pos
    return ((1 - x) if k & 4 else x, (1 - y) if k & 2 else y, (1 - c) if k & 1 else c)


HBM_SPEC = pl.BlockSpec(memory_space=pltpu.HBM)
SEM_SPEC = pl.BlockSpec(memory_space=pltpu.SEMAPHORE)
DATAFLOW = pltpu.SideEffectType.DATAFLOW_SIDE_EFFECTING


def _exchange_copies(src, land, sems, send_window, recv_window, with_arrivals):
    send_sems, recv_sems, local_sems = sems
    T = len(src)
    me = _my_position()
    me_lin = _linear(me)
    local = [pltpu.make_async_copy(send_window(t, src[t], me_lin), recv_window(t, land[t], me_lin),
                                   local_sems.at[t]) for t in range(T)]
    sends, arrivals = [], []
    for t in range(T):
        for k in range(1, N_DEV):
            peer = _peer(me, k)
            peer_lin = _linear(peer)
            sem = t * (N_DEV - 1) + k - 1
            sends.append(pltpu.make_async_remote_copy(
                src_ref=send_window(t, src[t], peer_lin), dst_ref=recv_window(t, land[t], me_lin),
                send_sem=send_sems.at[sem], recv_sem=recv_sems.at[sem],
                device_id=peer, device_id_type=MESH))
            if with_arrivals:
                arrivals.append(pltpu.make_async_remote_copy(
                    src_ref=send_window(t, src[t], me_lin), dst_ref=recv_window(t, land[t], peer_lin),
                    send_sem=send_sems.at[sem], recv_sem=recv_sems.at[sem],
                    device_id=peer, device_id_type=MESH))
    return local, sends, arrivals


def _exchange_start(srcs, land_shapes, send_window, recv_window, *, name, dep=None):
    T = len(srcs)
    n_in = 2 * T + (1 if dep is not None else 0)

    def body(*refs):
        src = refs[:T]
        land = refs[T:2 * T]
        sems = refs[n_in:n_in + 3]
        token = refs[-1]
        local, sends, _ = _exchange_copies(src, land, sems, send_window, recv_window, False)
        for cp in local + sends:
            cp.start()
        token[...] = jnp.zeros_like(token)

    lands = [lax.empty(ls.shape, ls.dtype) for ls in land_shapes]
    operands = [pltpu.with_memory_space_constraint(a, pltpu.HBM) for a in list(srcs) + lands]
    outs = pl.pallas_call(
        body, name=name,
        out_shape=(pltpu.SemaphoreType.DMA((T * (N_DEV - 1),)), pltpu.SemaphoreType.DMA((T * (N_DEV - 1),)),
                   pltpu.SemaphoreType.DMA((T,)),
                   *[pltpu.HBM(a.shape, a.dtype) for a in operands],
                   jax.ShapeDtypeStruct((8, LANES), F32)),
        in_specs=[HBM_SPEC] * (2 * T) + ([ANY] if dep is not None else []),
        out_specs=(SEM_SPEC,) * 3 + (HBM_SPEC,) * (2 * T) + (VMEM_SPEC,),
        input_output_aliases={i: 3 + i for i in range(2 * T)},
        compiler_params=pltpu.CompilerParams(has_side_effects=DATAFLOW),
    )(*operands, *([dep] if dep is not None else []))
    return outs[:3], outs[3:3 + T], outs[3 + T:3 + 2 * T], outs[-1]


def _exchange_wait(started, after, send_window, recv_window, *, name):
    sems, srcs, lands, _ = started
    T = len(srcs)

    def body(*refs):
        src = refs[:T]
        land = refs[T:2 * T]
        sem_refs = refs[2 * T:2 * T + 3]
        local, sends, arrivals = _exchange_copies(src, land, sem_refs, send_window, recv_window, True)
        for cp in arrivals:
            cp.wait_recv()
        for cp in sends:
            cp.wait_send()
        for cp in local:
            cp.wait()

    outs = pl.pallas_call(
        body, name=name,
        out_shape=tuple(pltpu.HBM(a.shape, a.dtype) for a in list(srcs) + list(lands)),
        in_specs=[HBM_SPEC] * (2 * T) + [SEM_SPEC] * 3 + [ANY],
        out_specs=(HBM_SPEC,) * (2 * T),
        input_output_aliases={i: i for i in range(2 * T)},
        compiler_params=pltpu.CompilerParams(has_side_effects=DATAFLOW),
    )(*srcs, *lands, *sems, after)
    return outs[T:]


def _shard_window(kind, width):
    def win(ref, lin):
        if kind == "slot":
            return ref.at[lin]
        if kind == "col":
            return ref.at[:, pl.ds(pl.multiple_of(lin * width, LANES), width)]
        if kind == "row":
            return ref.at[pl.ds(pl.multiple_of(lin * width, 8), width), :]
        if kind == "lcol":
            return ref.at[:, :, pl.ds(pl.multiple_of(lin * width, LANES), width)]
        if kind == "lrow":
            return ref.at[:, pl.ds(pl.multiple_of(lin * width, 8), width), :]
        raise ValueError(kind)
    return win


def _shard_windows(kinds, shard_shapes):
    return [_shard_window(k, (ss[-1] if k in ("col", "lcol") else ss[-2])) for k, ss in zip(kinds, shard_shapes)]


def _allgather_start(shards, kinds, full_shapes, *, name, dep=None):
    wins = _shard_windows(kinds, [s.shape for s in shards])
    send_window = lambda t, ref, lin: ref
    recv_window = lambda t, ref, lin: wins[t](ref, lin)
    started = _exchange_start(shards, [jax.ShapeDtypeStruct(fs, s.dtype) for fs, s in zip(full_shapes, shards)],
                              send_window, recv_window, name=name + "_start", dep=dep)
    return started, lambda after: _exchange_wait(started, after, send_window, recv_window, name=name + "_wait")


def _scatter_start(fulls, kinds, shard_shapes, *, name):
    wins = _shard_windows(kinds, shard_shapes)
    send_window = lambda t, ref, lin: wins[t](ref, lin)
    recv_window = lambda t, ref, lin: ref.at[lin]
    started = _exchange_start(
        fulls, [jax.ShapeDtypeStruct((N_DEV,) + tuple(ss), f.dtype) for ss, f in zip(shard_shapes, fulls)],
        send_window, recv_window, name=name + "_start")
    return started, lambda after: _exchange_wait(started, after, send_window, recv_window, name=name + "_wait")


def _small_gather(pack, *, reduce, name):
    R = pack.shape[0]

    def body(p_ref, o_ref, *rest):
        if reduce:
            buf, send_sems, recv_sems = rest
        else:
            buf = o_ref
            send_sems, recv_sems = rest
        me = _my_position()
        me_lin = _linear(me)
        buf[me_lin] = p_ref[...]
        sends = []
        for k in range(1, N_DEV):
            peer = _peer(me, k)
            cp = pltpu.make_async_remote_copy(
                src_ref=p_ref, dst_ref=buf.at[me_lin],
                send_sem=send_sems.at[k - 1], recv_sem=recv_sems.at[k - 1],
                device_id=peer, device_id_type=MESH)
            cp.start()
            sends.append(cp)
        for k in range(1, N_DEV):
            peer = _peer(me, k)
            pltpu.make_async_remote_copy(
                src_ref=p_ref, dst_ref=buf.at[_linear(peer)],
                send_sem=send_sems.at[k - 1], recv_sem=recv_sems.at[k - 1],
                device_id=peer, device_id_type=MESH).wait_recv()
        for cp in sends:
            cp.wait_send()
        if reduce:
            acc = buf[0]
            for s in range(1, N_DEV):
                acc = acc + buf[s]
            o_ref[...] = acc

    scratch = [pltpu.SemaphoreType.DMA((N_DEV - 1,)), pltpu.SemaphoreType.DMA((N_DEV - 1,))]
    if reduce:
        scratch = [pltpu.VMEM((N_DEV, R, LANES), F32)] + scratch
        out_shape = jax.ShapeDtypeStruct((R, LANES), F32)
    else:
        out_shape = jax.ShapeDtypeStruct((N_DEV, R, LANES), F32)
    return pl.pallas_call(
        body, name=name, out_shape=out_shape,
        in_specs=[VMEM_SPEC], out_specs=VMEM_SPEC, scratch_shapes=scratch,
        compiler_params=pltpu.CompilerParams(has_side_effects=True, vmem_limit_bytes=VMEM_LIMIT),
    )(pack)


def _adamw_math(w, g, m, v):
    m = ADAM_B1 * m + (1.0 - ADAM_B1) * g
    v = ADAM_B2 * v + (1.0 - ADAM_B2) * jnp.square(g)
    m_hat = m / (1.0 - ADAM_B1 ** ADAM_STEP)
    v_hat = v / (1.0 - ADAM_B2 ** ADAM_STEP)
    delta = -ADAM_LR * (m_hat / (jnp.sqrt(v_hat) + ADAM_EPS) + ADAM_WD * w)
    return delta, m, v


def _adamw_from_partials(parts, w, m, v, *, name):
    R, C = w.shape
    rl = parts[0].shape[1]
    assert all(p.shape == (N_DEV, rl, C) for p in parts) and rl * len(parts) == R
    tr = _tile(rl, (256, 128, 64, 32, 16))
    per = rl // tr
    L = len(parts)

    def body(*refs):
        p_refs = refs[:L]
        w_ref, m_ref, v_ref, g_ref, d_ref, nm_ref, nv_ref = refs[L:]
        i = pl.program_id(0)
        for l in range(L):
            @pl.when((i >= l * per) & (i < (l + 1) * per))
            def _(l=l):
                p_ref = p_refs[l]
                g = p_ref[0].astype(F32)
                for s in range(1, N_DEV):
                    g = g + p_ref[s].astype(F32)
                d, nm, nv = _adamw_math(w_ref[...], g, m_ref[...], v_ref[...])
                g_ref[...] = g
                d_ref[...] = d
                nm_ref[...] = nm
                nv_ref[...] = nv

    blk = pl.BlockSpec((tr, C), lambda i: (i, 0))
    part_specs = [pl.BlockSpec((N_DEV, tr, C), lambda i, l=l: (0, jnp.clip(i - l * per, 0, per - 1), 0))
                  for l in range(L)]
    return pl.pallas_call(
        body, name=name,
        out_shape=(jax.ShapeDtypeStruct((R, C), F32),) * 4,
        grid=(R // tr,),
        in_specs=part_specs + [blk, blk, blk],
        out_specs=(blk,) * 4,
        compiler_params=_cparams("parallel"),
    )(*parts, w, m, v)


def _adamw_small(g, w, m, v, *, name):
    def body(g_ref, w_ref, m_ref, v_ref, d_ref, nm_ref, nv_ref):
        d, nm, nv = _adamw_math(w_ref[...], g_ref[...], m_ref[...], v_ref[...])
        d_ref[...] = d
        nm_ref[...] = nm
        nv_ref[...] = nv

    return pl.pallas_call(
        body, name=name,
        out_shape=(jax.ShapeDtypeStruct(g.shape, F32),) * 3,
        in_specs=[VMEM_SPEC] * 4, out_specs=(VMEM_SPEC,) * 3,
    )(g, w, m, v)


def _pack_rows(pieces):
    flat = jnp.concatenate([p.reshape(-1) for p in pieces])
    n = flat.shape[0]
    padded = -(-n // (8 * LANES)) * (8 * LANES)
    return jnp.pad(flat, (0, padded - n)).reshape(padded // LANES, LANES)


def _unpack_rows(pack, shapes):
    flat = pack.reshape(-1)
    out, pos = [], 0
    for s in shapes:
        n = int(np.prod(s))
        out.append(flat[pos:pos + n].reshape(s))
        pos += n
    return out


def kernel(x, rel_bias, ab_norm, ab_w_in, ab_conv_w, ab_conv_b, ab_ln_g, ab_ln_b, ab_w_out, sc_norm, sc_w_in, sc_conv_w, sc_w_out, mlp_norm, mlp_w_up, mlp_w_down, final_norm, loss_target, m_rel_bias, m_ab_norm, m_ab_w_in, m_ab_conv_w, m_ab_conv_b, m_ab_ln_g, m_ab_ln_b, m_ab_w_out, m_sc_norm, m_sc_w_in, m_sc_conv_w, m_sc_w_out, m_mlp_norm, m_mlp_w_up, m_mlp_w_down, m_final_norm, v_rel_bias, v_ab_norm, v_ab_w_in, v_ab_conv_w, v_ab_conv_b, v_ab_ln_g, v_ab_ln_b, v_ab_w_out, v_sc_norm, v_sc_w_in, v_sc_conv_w, v_sc_w_out, v_mlp_norm, v_mlp_w_up, v_mlp_w_down, v_final_norm):
    S, D = x.shape[1], x.shape[2]
    CA = ab_conv_b.shape[1]
    C2 = 2 * CA
    AB_IN = C2 + ATTN_IN
    me_lin = _linear(_my_position())
    xs = x.reshape(S, D)
    tgt = loss_target.reshape(S, D)

    cw_sh = ab_conv_w.shape[2]
    scn_sh = sc_norm.shape[1]
    scw_sh = sc_conv_w.shape[2]
    small_sh_shapes = [(CONV_A_WIDTH, cw_sh), (scn_sh,), (SC_CONV_WIDTH, scw_sh)]
    small_params = _small_gather(_pack_rows([ab_conv_w[0], sc_norm[0], sc_conv_w[0]]), reduce=False,
                                 name="allgather_small_params")
    w_in_sh = ab_w_in[0].astype(BF16)
    ag_ab, wait_ab = _allgather_start(
        [w_in_sh, ab_w_out[0].astype(BF16)], ["slot", "row"],
        [(N_DEV,) + w_in_sh.shape, (N_DEV * ab_w_out.shape[1], D)], name="allgather_ab", dep=small_params)
    ag_mlp, wait_mlp = _allgather_start(
        [mlp_w_up.astype(BF16), mlp_w_down.astype(BF16)], ["lcol", "lrow"],
        [(2, D, N_DEV * mlp_w_up.shape[2]), (2, N_DEV * mlp_w_down.shape[1], D)], name="allgather_mlp",
        dep=ag_ab[3])
    ag_sc, wait_sc = _allgather_start(
        [sc_w_in[0].astype(BF16), sc_w_out[0].astype(BF16)], ["col", "row"],
        [(D, N_DEV * sc_w_in.shape[2]), (N_DEV * sc_w_out.shape[1], D)], name="allgather_sc",
        dep=ag_mlp[3])

    per_dev = [_unpack_rows(small_params[s], small_sh_shapes) for s in range(N_DEV)]
    conv_w_full = jnp.concatenate([p[0] for p in per_dev], axis=1)
    sc_norm_full = jnp.concatenate([p[1] for p in per_dev], axis=0)[None]
    sc_conv_full = jnp.concatenate([p[2] for p in per_dev], axis=1)

    buckets = _bucket_tables()
    biases = _bias_tables(rel_bias, buckets, name="bias_tables")

    dils = [dil for _, dil in DILATED_GROUPS]
    n0_all = _rmsnorm_fwd(xs, ab_norm, name="norm_ab", dep=ag_sc[3], views=dils[1:])
    n0 = n0_all[0]
    w_in_g, w_out = wait_ab(n0)
    w_in = jnp.transpose(w_in_g, (1, 0, 2)).reshape(D, AB_IN)
    w_c = w_in[:, :C2]
    w_q = w_in[:, C2:]
    w_grp = [jnp.concatenate([w_q[:, t * N_GROUPS * ATTN_OUT + g * ATTN_OUT:][:, :ATTN_OUT] for t in range(3)], axis=1)
             for g in range(N_GROUPS)]
    uc = _mm_nn(n0, w_c, out_dtype=F32, name="mm_ab_in_conv")
    uqs = [_mm_nn(n0_all[g], w_grp[g], out_dtype=BF16, slabs=dils[g], name=f"mm_ab_in_qkv{g}")
           for g in range(N_GROUPS)]
    ya, hglu, ct = _conv_fwd(uc, conv_w_full, ab_conv_b, ab_ln_g, ab_ln_b, name="conv_fwd")
    outs, lses = zip(*[_attn_fwd(uqs[g], biases, g, dils[g], name=f"attn_fwd_{g}") for g in range(N_GROUPS)])
    cat, outf, lse = _attn_merge(outs, lses, ya, name="attn_merge")
    h1 = _mm_nn(cat, w_out, out_dtype=F32, residual=xs, name="mm_ab_out")
    w_up, w_dn = wait_mlp(h1)
    n1, z0 = _norm_mm_nn(h1, mlp_norm[0:1], (w_up, 0), out_dtype=BF16, name="norm_mm_up0")
    h2 = _mm_nn(z0, (w_dn, 0), out_dtype=F32, residual=h1, a_fn=_relu_sq, name="mm_down0")
    w_sc_in, w_sc_out = wait_sc(h2)
    n2, u3 = _norm_mm_nn(h2, sc_norm_full, w_sc_in, out_dtype=BF16, name="norm_mm_sc_in")
    ysc = _sc_fwd(u3, sc_conv_full, name="sc_fwd")
    h3 = _mm_nn(ysc, w_sc_out, out_dtype=F32, residual=h2, name="mm_sc_out")
    n3, z1 = _norm_mm_nn(h3, mlp_norm[1:2], (w_up, 1), out_dtype=BF16, name="norm_mm_up1")
    h4 = _mm_nn(z1, (w_dn, 1), out_dtype=F32, residual=h3, a_fn=_relu_sq, name="mm_down1")

    def dz_epilogue(acc, z):
        return acc * (2.0 * jnp.maximum(z.astype(F32), 0.0))

    dh4, dh4b, acc_final = _loss_bwd(h4, tgt, final_norm[None], name="loss_bwd")
    dz1 = _mm_nt([(dh4b, (w_dn, 1))], out_dtype=BF16, epilogue=dz_epilogue, extra=z1, name="mm_d_down1")
    g_dn1 = _mm_tn(z1, dh4b, a_fn=_relu_sq, name="mm_gw_down1")
    g_up1 = _mm_tn(n3, dz1, name="mm_gw_up1")
    rs_mlp1, wait_rs_mlp1 = _scatter_start([g_up1, g_dn1], ["col", "row"],
                                           [mlp_w_up.shape[1:], mlp_w_down.shape[1:]], name="scatter_mlp1")
    dh3, dh3b, acc_mlp1 = _mm_nt_rms_bwd(dz1, (w_up, 1), h3, mlp_norm[1:2], dh4, name="mm_d_up1_norm_bwd",
                                         dep=rs_mlp1[3])

    dysc = _mm_nt([(dh3b, w_sc_out)], out_dtype=F32, name="mm_d_sc_out")
    g_sc_out = _mm_tn(ysc, dh3b, name="mm_gw_sc_out")
    du3, acc_scw = _sc_bwd(u3, dysc, sc_conv_full, name="sc_bwd")
    g_sc_in = _mm_tn(n2, du3, name="mm_gw_sc_in")
    rs_sc, wait_rs_sc = _scatter_start([g_sc_in, g_sc_out], ["col", "row"],
                                       [sc_w_in.shape[1:], sc_w_out.shape[1:]], name="scatter_sc")
    dh2, dh2b, acc_sc = _mm_nt_rms_bwd(du3, w_sc_in, h2, sc_norm_full, dh3, name="mm_d_sc_in_norm_bwd",
                                       dep=rs_sc[3])

    dz0 = _mm_nt([(dh2b, (w_dn, 0))], out_dtype=BF16, epilogue=dz_epilogue, extra=z0, name="mm_d_down0")
    g_dn0 = _mm_tn(z0, dh2b, a_fn=_relu_sq, name="mm_gw_down0")
    g_up0 = _mm_tn(n1, dz0, name="mm_gw_up0")
    rs_mlp0, wait_rs_mlp0 = _scatter_start([g_up0, g_dn0], ["col", "row"],
                                           [mlp_w_up.shape[1:], mlp_w_down.shape[1:]], name="scatter_mlp0")
    dh1, dh1b, acc_mlp0 = _mm_nt_rms_bwd(dz0, (w_up, 0), h1, mlp_norm[0:1], dh2, name="mm_d_up0_norm_bwd",
                                         dep=rs_mlp0[3])

    dcat = _mm_nt([(dh1b, w_out)], out_dtype=F32, name="mm_d_ab_out")
    g_ab_out = _mm_tn(cat, dh1b, name="mm_gw_ab_out")
    prep = _attn_prep(dcat, outf, lse, name="attn_prep")
    dqkv, dbs = zip(*[_attn_bwd(uqs[g], prep[g][0], prep[g][2], prep[g][1], biases, g, dils[g],
                                name=f"attn_bwd_{g}") for g in range(N_GROUPS)])
    drel = _relbias_reduce(dbs, buckets, name="relbias_reduce")
    dc, acc_conv = _conv_bwd_ln(ct, dcat, hglu, ab_ln_g, ab_ln_b, name="conv_bwd_ln")
    duc = _conv_bwd_in(dc, uc, conv_w_full, name="conv_bwd_in")
    g_wc = _mm_tn(n0, duc, name="mm_gw_ab_in_conv")
    g_wgrp = [_mm_tn(n0_all[g], dqkv[g], slabs=dils[g], name=f"mm_gw_ab_in_qkv{g}") for g in range(N_GROUPS)]
    g_wq = jnp.concatenate([g_wgrp[g][:, t * ATTN_OUT:(t + 1) * ATTN_OUT]
                            for t in range(3) for g in range(N_GROUPS)], axis=1)
    g_w_in = jnp.concatenate([g_wc, g_wq], axis=1).reshape(D, N_DEV, AB_IN // N_DEV).transpose(1, 0, 2)
    rs_ab, wait_rs_ab = _scatter_start([g_w_in, g_ab_out], ["slot", "row"],
                                       [w_in_sh.shape, ab_w_out.shape[1:]], name="scatter_ab")
    dn0 = _mm_nt([(duc, w_c), (dqkv[0], w_grp[0])], out_dtype=F32, name="mm_d_ab_in", dep=rs_ab[3])
    dn0_views = [(_mm_nt([(dqkv[g], w_grp[g])], out_dtype=F32, slabs=dils[g], name=f"mm_d_ab_in_qkv{g}"), dils[g])
                 for g in range(1, N_GROUPS)]
    grad_x, grad_xb, acc_ab = _rms_bwd(xs, ab_norm, dn0, dh1, name="norm_ab_bwd", dn_views=dn0_views)

    small_full = [drel[:, :, :NUM_BUCKETS].transpose(2, 0, 1).reshape(NUM_BUCKETS, N_GROUPS * HEADS_PER_GROUP),
                  acc_ab[0], acc_conv[0:CONV_A_WIDTH], acc_conv[32],
                  acc_conv[33], acc_conv[34], acc_sc[0], acc_scw[0:SC_CONV_WIDTH],
                  jnp.stack([acc_mlp0[0], acc_mlp1[0]]), acc_final[0], acc_final[1]]
    small_full_shapes = [p.shape for p in small_full]
    summed = _unpack_rows(_small_gather(_pack_rows(small_full), reduce=True, name="allreduce_small"),
                          small_full_shapes)
    (s_rel, s_abn, s_cw, s_cb, s_lg, s_lb, s_scn, s_scw, s_mlpn, s_fn, s_err) = summed
    loss = (0.5 / D) * jnp.sum(s_err)
    small_grads = {
        "rel_bias": s_rel, "ab_norm": s_abn[None],
        "ab_conv_w": lax.dynamic_slice_in_dim(s_cw, me_lin * cw_sh, cw_sh, axis=1)[None],
        "ab_conv_b": s_cb[None], "ab_ln_g": s_lg[None], "ab_ln_b": s_lb[None],
        "sc_norm": lax.dynamic_slice_in_dim(s_scn, me_lin * scn_sh, scn_sh, axis=0)[None],
        "sc_conv_w": lax.dynamic_slice_in_dim(s_scw, me_lin * scw_sh, scw_sh, axis=1)[None],
        "mlp_norm": s_mlpn, "final_norm": s_fn,
    }
    small_w = {"rel_bias": (rel_bias, m_rel_bias, v_rel_bias), "ab_norm": (ab_norm, m_ab_norm, v_ab_norm),
               "ab_conv_w": (ab_conv_w, m_ab_conv_w, v_ab_conv_w), "ab_conv_b": (ab_conv_b, m_ab_conv_b, v_ab_conv_b),
               "ab_ln_g": (ab_ln_g, m_ab_ln_g, v_ab_ln_g), "ab_ln_b": (ab_ln_b, m_ab_ln_b, v_ab_ln_b),
               "sc_norm": (sc_norm, m_sc_norm, v_sc_norm), "sc_conv_w": (sc_conv_w, m_sc_conv_w, v_sc_conv_w),
               "mlp_norm": (mlp_norm, m_mlp_norm, v_mlp_norm), "final_norm": (final_norm, m_final_norm, v_final_norm)}
    small_names = list(small_grads)
    small_shapes = [small_grads[n].shape for n in small_names]
    d_pack, m_pack, v_pack = _adamw_small(
        _pack_rows([small_grads[n] for n in small_names]), _pack_rows([small_w[n][0] for n in small_names]),
        _pack_rows([small_w[n][1] for n in small_names]), _pack_rows([small_w[n][2] for n in small_names]),
        name="adamw_small")
    small = {n: (small_grads[n], d, nm_, nv_) for n, d, nm_, nv_ in zip(
        small_names, _unpack_rows(d_pack, small_shapes), _unpack_rows(m_pack, small_shapes),
        _unpack_rows(v_pack, small_shapes))}

    p_up1, p_dn1 = wait_rs_mlp1(grad_xb)
    p_sc_in, p_sc_out = wait_rs_sc(grad_xb)
    p_up0, p_dn0 = wait_rs_mlp0(grad_xb)
    p_w_in, p_ab_out = wait_rs_ab(grad_xb)
    big = {}
    for nm, parts, w, m, v in (("ab_w_in", [p_w_in], ab_w_in, m_ab_w_in, v_ab_w_in),
                               ("ab_w_out", [p_ab_out], ab_w_out, m_ab_w_out, v_ab_w_out),
                               ("sc_w_in", [p_sc_in], sc_w_in, m_sc_w_in, v_sc_w_in),
                               ("sc_w_out", [p_sc_out], sc_w_out, m_sc_w_out, v_sc_w_out),
                               ("mlp_w_up", [p_up0, p_up1], mlp_w_up, m_mlp_w_up, v_mlp_w_up),
                               ("mlp_w_down", [p_dn0, p_dn1], mlp_w_down, m_mlp_w_down, v_mlp_w_down)):
        C = w.shape[-1]
        res = _adamw_from_partials(parts, w.reshape(-1, C), m.reshape(-1, C), v.reshape(-1, C), name="adamw_" + nm)
        big[nm] = tuple(r.reshape(w.shape) for r in res)

    order = ["rel_bias", "ab_norm", "ab_w_in", "ab_conv_w", "ab_conv_b", "ab_ln_g", "ab_ln_b", "ab_w_out",
             "sc_norm", "sc_w_in", "sc_conv_w", "sc_w_out", "mlp_norm", "mlp_w_up", "mlp_w_down", "final_norm"]
    allres = {**big, **small}
    return (loss, grad_x.reshape(x.shape),
            *[allres[n][0] for n in order], *[allres[n][1] for n in order],
            *[allres[n][2] for n in order], *[allres[n][3] for n in order])
```

```python
import functools
import math

import numpy as np
import jax
import jax.numpy as jnp
from jax import lax
from jax.experimental import pallas as pl
from jax.experimental.pallas import tpu as pltpu

F32 = jnp.float32
BF16 = jnp.bfloat16

HEAD_DIM = 64
HEADS_PER_GROUP = 8
DILATED_GROUPS = ((128, 1), (512, 4), (2048, 16))
N_GROUPS = 3
ATTN_OUT = HEADS_PER_GROUP * HEAD_DIM
ATTN_IN = 3 * N_GROUPS * ATTN_OUT
GROUP_QKV = 3 * ATTN_OUT
ATTN_BLK = 128
CONV_A_WIDTH = 31
SC_CONV_WIDTH = 3
NUM_BUCKETS = 32
REL_MAX_DISTANCE = 2048
RMS_EPS = 1e-6
LN_EPS = 1e-5
NEG_INF = -1e30
ADAM_LR = 0.001
ADAM_B1 = 0.9
ADAM_B2 = 0.999
ADAM_EPS = 1e-08
ADAM_WD = 0.01
ADAM_STEP = 10

N_DEV = 8
HALO = 32
LANES = 128
VMEM_LIMIT = 56 * 1024 * 1024
MESH = pl.DeviceIdType.MESH
ANY = pl.BlockSpec(memory_space=pl.ANY)
VMEM_SPEC = pl.BlockSpec(memory_space=pltpu.VMEM)


def _tile(n, prefs):
    for t in prefs:
        if n % t == 0:
            return t
    return n


def _cparams(*sem):
    return pltpu.CompilerParams(dimension_semantics=sem, vmem_limit_bytes=VMEM_LIMIT)


def _relu_sq(z):
    return jnp.square(jnp.maximum(z, 0))


def _dot_nt(a, b):
    return lax.dot_general(a, b, (((1,), (1,)), ((), ())), preferred_element_type=F32)


def _dot_tn(a, b):
    return lax.dot_general(a, b, (((0,), (0,)), ((), ())), preferred_element_type=F32)


def _weight(b):
    if not isinstance(b, tuple):
        return b, b.shape, pl.BlockSpec
    arr, layer = b

    def spec(block, index_map):
        return pl.BlockSpec((None,) + tuple(block), lambda *g: (layer,) + tuple(index_map(*g)))

    return arr, arr.shape[1:], spec


def _mm_nn(a, b, *, out_dtype, name, residual=None, a_fn=None, slabs=1, wide=False):
    M, K = a.shape
    K //= slabs
    b, (_, N), b_spec = _weight(b)
    tm = _tile(M, (1024, 512, 256) if wide else (2048, 1024, 512, 256))
    tn = _tile(N, (1024, 512, 384, 256, 128) if wide else (512, 384, 256, 128))
    tk = _tile(K, (1024, 512, 256, 128))
    nk = K // tk
    nj = N // tn
    has_res = residual is not None

    def body(*refs):
        if has_res:
            a_ref, b_ref, r_ref, o_ref = refs[:4]
        else:
            a_ref, b_ref, o_ref = refs[:3]
        av = a_ref[...]
        if a_fn is not None:
            av = a_fn(av)
        part = jnp.dot(av, b_ref[...], preferred_element_type=F32)

        def finish(acc):
            if has_res:
                acc = acc + r_ref[...]
            o_ref[...] = acc.astype(o_ref.dtype)

        if nk == 1:
            finish(part)
        else:
            acc_ref = refs[-1]
            k = pl.program_id(2)

            @pl.when(k == 0)
            def _():
                acc_ref[...] = part

            @pl.when((k > 0) & (k < nk - 1))
            def _():
                acc_ref[...] += part

            @pl.when(k == nk - 1)
            def _():
                finish(acc_ref[...] + part)

    in_specs = [pl.BlockSpec((tm, tk), lambda i, j, k: (i, (j // nj) * nk + k)),
                b_spec((tk, tn), lambda i, j, k: (k, j % nj))]
    args = [a, b]
    if has_res:
        in_specs.append(pl.BlockSpec((tm, tn), lambda i, j, k: (i, j)))
        args.append(residual)
    return pl.pallas_call(
        body, name=name,
        out_shape=jax.ShapeDtypeStruct((M, slabs * N), out_dtype),
        grid=(M // tm, slabs * nj, nk),
        in_specs=in_specs,
        out_specs=pl.BlockSpec((tm, tn), lambda i, j, k: (i, j)),
        scratch_shapes=[pltpu.VMEM((tm, tn), F32)] if nk > 1 else [],
        compiler_params=_cparams("parallel", "parallel", "arbitrary"),
    )(*args)


def _norm_mm_nn(h, g, b, *, out_dtype, name, wide=False):
    M, K = h.shape
    b, (_, N), b_spec = _weight(b)
    tm = _tile(M, (2048, 1024, 512, 256))
    tn = _tile(N, (1024, 512, 384, 256, 128) if wide else (512, 384, 256, 128))

    def body(h_ref, g_ref, b_ref, n_ref, o_ref):
        @pl.when(pl.program_id(1) == 0)
        def _():
            x = h_ref[...]
            r = lax.rsqrt(jnp.mean(x * x, axis=-1, keepdims=True) + RMS_EPS)
            n_ref[...] = (x * r * g_ref[...]).astype(BF16)

        o_ref[...] = jnp.dot(n_ref[...], b_ref[...], preferred_element_type=F32).astype(o_ref.dtype)

    return pl.pallas_call(
        body, name=name,
        out_shape=(jax.ShapeDtypeStruct((M, K), BF16), jax.ShapeDtypeStruct((M, N), out_dtype)),
        grid=(M // tm, N // tn),
        in_specs=[pl.BlockSpec((tm, K), lambda i, j: (i, 0)), pl.BlockSpec((1, K), lambda i, j: (0, 0)),
                  b_spec((K, tn), lambda i, j: (0, j))],
        out_specs=(pl.BlockSpec((tm, K), lambda i, j: (i, 0)), pl.BlockSpec((tm, tn), lambda i, j: (i, j))),
        compiler_params=_cparams("parallel", "arbitrary"),
    )(h, g, b)


def _mm_nt(pairs, *, out_dtype, name, epilogue=None, extra=None, dep=None, slabs=1):
    assert slabs == 1 or len(pairs) == 1
    M = pairs[0][0].shape[0]
    weights = [_weight(p[1]) for p in pairs]
    Ko = weights[0][1][0]
    tm = _tile(M, (2048, 1024, 512, 256))
    to = _tile(Ko, (1024, 512, 256, 128))
    njo = Ko // to
    tks = [_tile(p[0].shape[1] // slabs, (1024, 768, 512, 256, 128)) for p in pairs]
    steps = [p[0].shape[1] // slabs // tk for p, tk in zip(pairs, tks)]
    offs = [sum(steps[:i]) for i in range(len(pairs))]
    nk = sum(steps)
    npair = len(pairs)
    has_extra = extra is not None

    def body(*refs):
        ab = refs[:2 * npair]
        pos = 2 * npair
        e_ref = None
        if has_extra:
            e_ref = refs[pos]
            pos += 1
        if dep is not None:
            pos += 1
        o_ref = refs[pos]
        acc_ref = refs[pos + 1]
        k = pl.program_id(2)

        @pl.when(k == 0)
        def _():
            acc_ref[...] = jnp.zeros_like(acc_ref)

        for p in range(npair):
            @pl.when((k >= offs[p]) & (k < offs[p] + steps[p]))
            def _(p=p):
                acc_ref[...] += _dot_nt(ab[2 * p][...], ab[2 * p + 1][...])

        @pl.when(k == nk - 1)
        def _():
            acc = acc_ref[...]
            if epilogue is not None:
                acc = epilogue(acc, e_ref[...] if has_extra else None)
            o_ref[...] = acc.astype(o_ref.dtype)

    in_specs, args = [], []
    for p, (a, b) in enumerate(pairs):
        def kidx(k, p=p):
            return jnp.clip(k - offs[p], 0, steps[p] - 1)
        in_specs.append(pl.BlockSpec((tm, tks[p]),
                                     lambda i, j, k, kidx=kidx, p=p: (i, (j // njo) * steps[p] + kidx(k))))
        in_specs.append(weights[p][2]((to, tks[p]), lambda i, j, k, kidx=kidx: (j % njo, kidx(k))))
        args += [a, weights[p][0]]
    if has_extra:
        in_specs.append(pl.BlockSpec((tm, to), lambda i, j, k: (i, j)))
        args.append(extra)
    if dep is not None:
        in_specs.append(ANY)
        args.append(dep)
    return pl.pallas_call(
        body, name=name,
        out_shape=jax.ShapeDtypeStruct((M, slabs * Ko), out_dtype),
        grid=(M // tm, slabs * njo, nk),
        in_specs=in_specs,
        out_specs=pl.BlockSpec((tm, to), lambda i, j, k: (i, j)),
        scratch_shapes=[pltpu.VMEM((tm, to), F32)],
        compiler_params=_cparams("parallel", "parallel", "arbitrary"),
    )(*args)


def _mm_tn(a, b, *, name, a_fn=None, slabs=1):
    M, K = a.shape
    K //= slabs
    N = b.shape[1] // slabs
    tm = _tile(M, (4096, 2048, 1024, 512, 256))
    tk = _tile(K, (1024, 768, 512, 384, 256, 128))
    tn = _tile(N, (1024, 768, 512, 384, 256, 128))
    nmi = M // tm
    nm = slabs * nmi
    nki, nnj = K // tk, N // tn

    def body(a_ref, b_ref, o_ref, acc_ref):
        m = pl.program_id(2)
        av = a_ref[...]
        if a_fn is not None:
            av = a_fn(av)
        part = _dot_tn(av, b_ref[...])
        if nm == 1:
            o_ref[...] = part.astype(o_ref.dtype)
            return

        @pl.when(m == 0)
        def _():
            acc_ref[...] = part

        @pl.when((m > 0) & (m < nm - 1))
        def _():
            acc_ref[...] += part

        @pl.when(m == nm - 1)
        def _():
            o_ref[...] = (acc_ref[...] + part).astype(o_ref.dtype)

    return pl.pallas_call(
        body, name=name,
        out_shape=jax.ShapeDtypeStruct((K, N), BF16),
        grid=(K // tk, N // tn, nm),
        in_specs=[pl.BlockSpec((tm, tk), lambda i, j, m: (m % nmi, (m // nmi) * nki + i)),
                  pl.BlockSpec((tm, tn), lambda i, j, m: (m % nmi, (m // nmi) * nnj + j))],
        out_specs=pl.BlockSpec((tk, tn), lambda i, j, m: (i, j)),
        scratch_shapes=[pltpu.VMEM((tk, tn), F32)],
        compiler_params=_cparams("parallel", "parallel", "arbitrary"),
    )(a, b)


def _rmsnorm_fwd(h, g, *, name, dep=None, views=()):
    S, D = h.shape
    tm = _tile(S, (512, 256))
    nv = len(views)

    def body(h_ref, g_ref, *rest):
        n_out = 1 + nv
        outs = rest[len(rest) - n_out - (1 if nv else 0):len(rest) - (1 if nv else 0)]
        x = h_ref[...]
        r = lax.rsqrt(jnp.mean(x * x, axis=-1, keepdims=True) + RMS_EPS)
        y = x * r * g_ref[...]
        outs[0][...] = y.astype(BF16)
        if nv:
            scr = rest[-1]
            _to_chunks(scr, y)
            for v_ref, d in zip(outs[1:], views):
                _slabs_from_chunks(v_ref, scr, d, BF16)

    res = pl.pallas_call(
        body, name=name,
        out_shape=(jax.ShapeDtypeStruct((S, D), BF16),)
        + tuple(jax.ShapeDtypeStruct((S // d, d * D), BF16) for d in views),
        grid=(S // tm,),
        in_specs=[pl.BlockSpec((tm, D), lambda i: (i, 0)), pl.BlockSpec((1, D), lambda i: (0, 0))]
        + ([ANY] if dep is not None else []),
        out_specs=(pl.BlockSpec((tm, D), lambda i: (i, 0)),)
        + tuple(pl.BlockSpec((tm // d, d * D), lambda i: (i, 0)) for d in views),
        scratch_shapes=[_chunk_scratch(tm, D)] if nv else [],
        compiler_params=_cparams("parallel"),
    )(h, g, *([dep] if dep is not None else []))
    return res if nv else res[0]


def _rms_bwd_rows(x, g, dy):
    r = lax.rsqrt(jnp.mean(x * x, axis=-1, keepdims=True) + RMS_EPS)
    xh = x * r
    gy = dy * g
    dx = r * (gy - xh * jnp.mean(xh * gy, axis=-1, keepdims=True))
    return dx, dy * xh


def _rms_bwd(x, g, dn, dres, *, name, dn_views=()):
    S, D = x.shape
    tm = _tile(S, (256,))
    nv = len(dn_views)

    def body(x_ref, g_ref, dn_ref, dr_ref, *rest):
        v_refs = rest[:nv]
        dx_ref, dxb_ref, dg_ref = rest[nv:nv + 3]
        scr = rest[nv + 3:]
        i = pl.program_id(0)
        dn = dn_ref[...]
        for v_ref, s_ref, (_, d) in zip(v_refs, scr, dn_views):
            _chunks_from_slabs(s_ref, v_ref, d)
            dn = dn + _from_chunks(s_ref)
        dx, dgx = _rms_bwd_rows(x_ref[...], g_ref[...], dn)
        tot = dr_ref[...] + dx
        dx_ref[...] = tot
        dxb_ref[...] = tot.astype(BF16)

        @pl.when(i == 0)
        def _():
            dg_ref[...] = jnp.zeros_like(dg_ref)

        dg_ref[0:1, :] += jnp.sum(dgx, axis=0, keepdims=True)

    row = pl.BlockSpec((tm, D), lambda i: (i, 0))
    return pl.pallas_call(
        body, name=name,
        out_shape=(jax.ShapeDtypeStruct((S, D), F32), jax.ShapeDtypeStruct((S, D), BF16),
                   jax.ShapeDtypeStruct((8, D), F32)),
        grid=(S // tm,),
        in_specs=[row, pl.BlockSpec((1, D), lambda i: (0, 0)), row, row]
        + [pl.BlockSpec((tm // d, d * D), lambda i: (i, 0)) for _, d in dn_views],
        out_specs=(row, row, pl.BlockSpec((8, D), lambda i: (0, 0))),
        scratch_shapes=[_chunk_scratch(tm, D)] * nv,
        compiler_params=_cparams("arbitrary"),
    )(x, g, dn, dres, *[a for a, _ in dn_views])


def _mm_nt_rms_bwd(a, b, x, g, dres, *, name, dep=None):
    M, N = a.shape
    b, (D, _), b_spec = _weight(b)
    tm = _tile(M, (1024, 512, 256))
    tk = _tile(N, (1024, 512, 256, 128))
    nk = N // tk

    def body(a_ref, b_ref, x_ref, g_ref, dr_ref, *rest):
        dx_ref, dxb_ref, dg_ref, acc_ref = rest[-4:]
        i = pl.program_id(0)
        k = pl.program_id(1)
        part = _dot_nt(a_ref[...], b_ref[...])

        @pl.when((i == 0) & (k == 0))
        def _():
            dg_ref[...] = jnp.zeros_like(dg_ref)

        @pl.when(k == 0)
        def _():
            acc_ref[...] = part

        @pl.when((k > 0) & (k < nk - 1))
        def _():
            acc_ref[...] += part

        @pl.when(k == nk - 1)
        def _():
            dn = part if nk == 1 else acc_ref[...] + part
            dx, dgx = _rms_bwd_rows(x_ref[...], g_ref[...], dn)
            tot = dr_ref[...] + dx
            dx_ref[...] = tot
            dxb_ref[...] = tot.astype(BF16)
            dg_ref[0:1, :] += jnp.sum(dgx, axis=0, keepdims=True)

    row = pl.BlockSpec((tm, D), lambda i, k: (i, 0))
    in_specs = [pl.BlockSpec((tm, tk), lambda i, k: (i, k)), b_spec((D, tk), lambda i, k: (0, k)),
                row, pl.BlockSpec((1, D), lambda i, k: (0, 0)), row]
    args = [a, b, x, g, dres]
    if dep is not None:
        in_specs.append(ANY)
        args.append(dep)
    return pl.pallas_call(
        body, name=name,
        out_shape=(jax.ShapeDtypeStruct((M, D), F32), jax.ShapeDtypeStruct((M, D), BF16),
                   jax.ShapeDtypeStruct((8, D), F32)),
        grid=(M // tm, nk),
        in_specs=in_specs,
        out_specs=(row, row, pl.BlockSpec((8, D), lambda i, k: (0, 0))),
        scratch_shapes=[pltpu.VMEM((tm, D), F32)],
        compiler_params=_cparams("arbitrary", "arbitrary"),
    )(*args)


def _loss_bwd(h, target, g, *, name):
    S, D = h.shape
    tm = _tile(S, (256,))

    def body(h_ref, t_ref, g_ref, dx_ref, dxb_ref, acc_ref):
        i = pl.program_id(0)
        x = h_ref[...]
        gv = g_ref[...]
        r = lax.rsqrt(jnp.mean(x * x, axis=-1, keepdims=True) + RMS_EPS)
        err = x * r * gv - t_ref[...]
        dx, dgx = _rms_bwd_rows(x, gv, err * (1.0 / D))
        dx_ref[...] = dx
        dxb_ref[...] = dx.astype(BF16)

        @pl.when(i == 0)
        def _():
            acc_ref[...] = jnp.zeros_like(acc_ref)

        acc_ref[0:1, :] += jnp.sum(dgx, axis=0, keepdims=True)
        acc_ref[1:2, :] += jnp.sum(err * err, axis=0, keepdims=True)

    row = pl.BlockSpec((tm, D), lambda i: (i, 0))
    return pl.pallas_call(
        body, name=name,
        out_shape=(jax.ShapeDtypeStruct((S, D), F32), jax.ShapeDtypeStruct((S, D), BF16),
                   jax.ShapeDtypeStruct((8, D), F32)),
        grid=(S // tm,),
        in_specs=[row, row, pl.BlockSpec((1, D), lambda i: (0, 0))],
        out_specs=(row, row, pl.BlockSpec((8, D), lambda i: (0, 0))),
        compiler_params=_cparams("arbitrary"),
    )(h, target, g)


SUBLANES = 8
CONV_ROWS = 64


def _build_shifted(ext_ref, rot_ref, ts, shifts=tuple(range(1, SUBLANES))):
    rows = ts + HALO - SUBLANES
    for j in shifts:
        rot_ref[j, 0:rows, :] = ext_ref[j:j + rows, :]


def _shifted(ext_ref, rot_ref, off, r0, nrows, cs):
    q, j = divmod(off, SUBLANES)
    start = SUBLANES * q + r0
    if j == 0:
        return ext_ref[start:start + nrows, cs]
    return rot_ref[j, start:start + nrows, cs]


def _conv_fwd(uc, conv_w, conv_b, ln_g, ln_b, *, name):
    S, C2 = uc.shape
    C = C2 // 2
    ts = _tile(S, (512, 256))
    per = ts // HALO

    def body(cur_ref, halo_ref, w_ref, b_ref, g_ref, beta_ref, ya_ref, h_ref, ct_ref, ext_ref, rot_ref):
        i = pl.program_id(0)
        hh = halo_ref[:, 0:C] * jax.nn.sigmoid(halo_ref[:, C:C2])
        ext_ref[0:HALO, :] = jnp.where(i == 0, 0.0, hh)
        hc = cur_ref[:, 0:C] * jax.nn.sigmoid(cur_ref[:, C:C2])
        ext_ref[HALO:HALO + ts, :] = hc
        h_ref[...] = hc
        _build_shifted(ext_ref, rot_ref, ts)
        for c0 in range(0, C, LANES):
            cs = slice(c0, c0 + LANES)
            for r0 in range(0, ts, CONV_ROWS):
                acc = jnp.zeros((CONV_ROWS, LANES), F32)
                for k in range(CONV_A_WIDTH):
                    acc = acc + w_ref[k:k + 1, cs] * _shifted(ext_ref, rot_ref, k + 2, r0, CONV_ROWS, cs)
                ct_ref[r0:r0 + CONV_ROWS, cs] = acc + b_ref[:, cs]
        ct = ct_ref[...]
        mu = jnp.mean(ct, axis=-1, keepdims=True)
        xc = ct - mu
        var = jnp.mean(xc * xc, axis=-1, keepdims=True)
        l = xc * lax.rsqrt(var + LN_EPS) * g_ref[...] + beta_ref[...]
        ya_ref[...] = (l * jax.nn.sigmoid(l)).astype(ya_ref.dtype)

    vec = pl.BlockSpec((1, C), lambda i: (0, 0))
    row = pl.BlockSpec((ts, C), lambda i: (i, 0))
    return pl.pallas_call(
        body, name=name,
        out_shape=(jax.ShapeDtypeStruct((S, C), BF16), jax.ShapeDtypeStruct((S, C), F32),
                   jax.ShapeDtypeStruct((S, C), F32)),
        grid=(S // ts,),
        in_specs=[pl.BlockSpec((ts, C2), lambda i: (i, 0)),
                  pl.BlockSpec((HALO, C2), lambda i: (jnp.maximum(i * per - 1, 0), 0)),
                  pl.BlockSpec((CONV_A_WIDTH, C), lambda i: (0, 0)), vec, vec, vec],
        out_specs=(row, row, row),
        scratch_shapes=[pltpu.VMEM((HALO + ts, C), F32), pltpu.VMEM((8, HALO + ts, C), F32)],
        compiler_params=_cparams("parallel"),
    )(uc, uc, conv_w, conv_b, ln_g, ln_b)


CONV_ACC_ROWS = 40


def _conv_bwd_ln(ct, dcat, hglu, ln_g, ln_b, *, name):
    S, C = ct.shape
    CW = dcat.shape[1]
    ts = _tile(S, (256,))
    per = ts // HALO

    def body(ct_ref, dcat_ref, hc_ref, hh_ref, g_ref, beta_ref, dc_ref, acc_ref, ext_ref, rot_ref):
        i = pl.program_id(0)
        ct = ct_ref[...]
        gv = g_ref[...]
        mu = jnp.mean(ct, axis=-1, keepdims=True)
        xc = ct - mu
        rstd = lax.rsqrt(jnp.mean(xc * xc, axis=-1, keepdims=True) + LN_EPS)
        xh = xc * rstd
        l = xh * gv + beta_ref[...]
        sg = jax.nn.sigmoid(l)
        dl = dcat_ref[:, 0:C] * (sg * (1.0 + l * (1.0 - sg)))
        dxh = dl * gv
        dc = rstd * (dxh - jnp.mean(dxh, axis=-1, keepdims=True)
                     - xh * jnp.mean(dxh * xh, axis=-1, keepdims=True))
        dc_ref[...] = dc

        @pl.when(i == 0)
        def _():
            acc_ref[...] = jnp.zeros_like(acc_ref)

        acc_ref[32:33, :] += jnp.sum(dc, axis=0, keepdims=True)
        acc_ref[33:34, :] += jnp.sum(dl * xh, axis=0, keepdims=True)
        acc_ref[34:35, :] += jnp.sum(dl, axis=0, keepdims=True)
        ext_ref[0:HALO, :] = jnp.where(i == 0, 0.0, hh_ref[...])
        ext_ref[HALO:HALO + ts, :] = hc_ref[...]
        _build_shifted(ext_ref, rot_ref, ts)
        for c0 in range(0, C, LANES):
            cs = slice(c0, c0 + LANES)
            dcc = dc_ref[:, cs]
            for k in range(CONV_A_WIDTH):
                acc_ref[k:k + 1, cs] += jnp.sum(dcc * _shifted(ext_ref, rot_ref, k + 2, 0, ts, cs),
                                                axis=0, keepdims=True)

    vec = pl.BlockSpec((1, C), lambda i: (0, 0))
    row = pl.BlockSpec((ts, C), lambda i: (i, 0))
    return pl.pallas_call(
        body, name=name,
        out_shape=(jax.ShapeDtypeStruct((S, C), F32), jax.ShapeDtypeStruct((CONV_ACC_ROWS, C), F32)),
        grid=(S // ts,),
        in_specs=[row, pl.BlockSpec((ts, CW), lambda i: (i, 0)), row,
                  pl.BlockSpec((HALO, C), lambda i: (jnp.maximum(i * per - 1, 0), 0)), vec, vec],
        out_specs=(row, pl.BlockSpec((CONV_ACC_ROWS, C), lambda i: (0, 0))),
        scratch_shapes=[pltpu.VMEM((HALO + ts, C), F32), pltpu.VMEM((8, HALO + ts, C), F32)],
        compiler_params=_cparams("arbitrary"),
    )(ct, dcat, hglu, hglu, ln_g, ln_b)


def _conv_bwd_in(dc, uc, conv_w, *, name):
    S, C = dc.shape
    C2 = 2 * C
    ts = _tile(S, (256,))
    per = ts // HALO
    nt = S // ts

    def body(dc_ref, dn_ref, uc_ref, w_ref, du_ref, ext_ref, rot_ref):
        i = pl.program_id(0)
        ext_ref[0:ts, :] = dc_ref[...]
        ext_ref[ts:ts + HALO, :] = jnp.where(i == nt - 1, 0.0, dn_ref[...])
        _build_shifted(ext_ref, rot_ref, ts)
        for c0 in range(0, C, LANES):
            cs = slice(c0, c0 + LANES)
            gs = slice(C + c0, C + c0 + LANES)
            for r0 in range(0, ts, CONV_ROWS):
                rs = slice(r0, r0 + CONV_ROWS)
                acc = jnp.zeros((CONV_ROWS, LANES), F32)
                for k in range(CONV_A_WIDTH):
                    acc = acc + w_ref[k:k + 1, cs] * _shifted(ext_ref, rot_ref, 30 - k, r0, CONV_ROWS, cs)
                sg = jax.nn.sigmoid(uc_ref[rs, gs])
                du_ref[rs, cs] = (acc * sg).astype(du_ref.dtype)
                du_ref[rs, gs] = (acc * uc_ref[rs, cs] * sg * (1.0 - sg)).astype(du_ref.dtype)

    return pl.pallas_call(
        body, name=name,
        out_shape=jax.ShapeDtypeStruct((S, C2), BF16),
        grid=(nt,),
        in_specs=[pl.BlockSpec((ts, C), lambda i: (i, 0)),
                  pl.BlockSpec((HALO, C), lambda i: (jnp.minimum((i + 1) * per, S // HALO - 1), 0)),
                  pl.BlockSpec((ts, C2), lambda i: (i, 0)),
                  pl.BlockSpec((CONV_A_WIDTH, C), lambda i: (0, 0))],
        out_specs=pl.BlockSpec((ts, C2), lambda i: (i, 0)),
        scratch_shapes=[pltpu.VMEM((ts + HALO, C), F32), pltpu.VMEM((8, ts + HALO, C), F32)],
        compiler_params=_cparams("parallel"),
    )(dc, dc, uc, conv_w)


SC_SHIFTS_BACK = ((HALO - 2) % SUBLANES, (HALO - 1) % SUBLANES)
SC_SHIFTS_AHEAD = (1, 2)


def _sc_fwd(u3, conv_w, *, name):
    S, W3 = u3.shape
    W = W3 // 3
    ts = _tile(S, (256,))
    per = ts // HALO

    def body(cur_ref, halo_ref, w_ref, y_ref, ext_ref, rot_ref):
        i = pl.program_id(0)
        cvh = halo_ref[:, W:2 * W].astype(F32) * halo_ref[:, 2 * W:W3].astype(F32)
        ext_ref[0:HALO, :] = jnp.where(i == 0, 0.0, cvh)
        ext_ref[HALO:HALO + ts, :] = cur_ref[:, W:2 * W].astype(F32) * cur_ref[:, 2 * W:W3].astype(F32)
        _build_shifted(ext_ref, rot_ref, ts, SC_SHIFTS_BACK)
        for c0 in range(0, W, LANES):
            cs = slice(c0, c0 + LANES)
            for r0 in range(0, ts, CONV_ROWS):
                rs = slice(r0, r0 + CONV_ROWS)
                k = (w_ref[0:1, cs] * _shifted(ext_ref, rot_ref, HALO - 2, r0, CONV_ROWS, cs)
                     + w_ref[1:2, cs] * _shifted(ext_ref, rot_ref, HALO - 1, r0, CONV_ROWS, cs)
                     + w_ref[2:3, cs] * _shifted(ext_ref, rot_ref, HALO, r0, CONV_ROWS, cs))
                y_ref[rs, cs] = (cur_ref[rs, cs].astype(F32) * k).astype(y_ref.dtype)

    return pl.pallas_call(
        body, name=name,
        out_shape=jax.ShapeDtypeStruct((S, W), BF16),
        grid=(S // ts,),
        in_specs=[pl.BlockSpec((ts, W3), lambda i: (i, 0)),
                  pl.BlockSpec((HALO, W3), lambda i: (jnp.maximum(i * per - 1, 0), 0)),
                  pl.BlockSpec((SC_CONV_WIDTH, W), lambda i: (0, 0))],
        out_specs=pl.BlockSpec((ts, W), lambda i: (i, 0)),
        scratch_shapes=[pltpu.VMEM((HALO + ts, W), F32), pltpu.VMEM((8, HALO + ts, W), F32)],
        compiler_params=_cparams("parallel"),
    )(u3, u3, conv_w)


def _sc_bwd(u3, dy, conv_w, *, name):
    S, W3 = u3.shape
    W = W3 // 3
    ts = _tile(S, (256,))
    per = ts // HALO
    nt = S // ts

    def body(cur_ref, prev_ref, next_ref, dy_ref, dyn_ref, w_ref, du_ref, dw_ref, cv_ext, dk_ext, cv_rot, dk_rot):
        i = pl.program_id(0)
        cvh = prev_ref[:, W:2 * W].astype(F32) * prev_ref[:, 2 * W:W3].astype(F32)
        cv_ext[0:HALO, :] = jnp.where(i == 0, 0.0, cvh)
        cv_ext[HALO:HALO + ts, :] = cur_ref[:, W:2 * W].astype(F32) * cur_ref[:, 2 * W:W3].astype(F32)
        dk_ext[0:ts, :] = dy_ref[...] * cur_ref[:, 0:W].astype(F32)
        dk_ext[ts:ts + HALO, :] = jnp.where(i == nt - 1, 0.0, dyn_ref[...] * next_ref[:, 0:W].astype(F32))
        _build_shifted(cv_ext, cv_rot, ts, SC_SHIFTS_BACK)
        _build_shifted(dk_ext, dk_rot, ts, SC_SHIFTS_AHEAD)

        @pl.when(i == 0)
        def _():
            dw_ref[...] = jnp.zeros_like(dw_ref)

        for c0 in range(0, W, LANES):
            cs = slice(c0, c0 + LANES)
            w0, w1, w2 = w_ref[0:1, cs], w_ref[1:2, cs], w_ref[2:3, cs]
            sums = [jnp.zeros((1, LANES), F32)] * SC_CONV_WIDTH
            for r0 in range(0, ts, CONV_ROWS):
                rs = slice(r0, r0 + CONV_ROWS)
                cv2 = _shifted(cv_ext, cv_rot, HALO - 2, r0, CONV_ROWS, cs)
                cv1 = _shifted(cv_ext, cv_rot, HALO - 1, r0, CONV_ROWS, cs)
                cv0 = _shifted(cv_ext, cv_rot, HALO, r0, CONV_ROWS, cs)
                dk = dk_ext[rs, cs]
                dcv = (w2 * dk + w1 * _shifted(dk_ext, dk_rot, 1, r0, CONV_ROWS, cs)
                       + w0 * _shifted(dk_ext, dk_rot, 2, r0, CONV_ROWS, cs))
                du_ref[rs, cs] = (dy_ref[rs, cs] * (w0 * cv2 + w1 * cv1 + w2 * cv0)).astype(du_ref.dtype)
                du_ref[rs, W + c0:W + c0 + LANES] = (
                    dcv * cur_ref[rs, 2 * W + c0:2 * W + c0 + LANES].astype(F32)).astype(du_ref.dtype)
                du_ref[rs, 2 * W + c0:2 * W + c0 + LANES] = (
                    dcv * cur_ref[rs, W + c0:W + c0 + LANES].astype(F32)).astype(du_ref.dtype)
                for t, cvt in enumerate((cv2, cv1, cv0)):
                    sums[t] = sums[t] + jnp.sum(dk * cvt, axis=0, keepdims=True)
            for t in range(SC_CONV_WIDTH):
                dw_ref[t:t + 1, cs] += sums[t]

    nxt = lambda i: (jnp.minimum((i + 1) * per, S // HALO - 1), 0)
    return pl.pallas_call(
        body, name=name,
        out_shape=(jax.ShapeDtypeStruct((S, W3), BF16), jax.ShapeDtypeStruct((8, W), F32)),
        grid=(nt,),
        in_specs=[pl.BlockSpec((ts, W3), lambda i: (i, 0)),
                  pl.BlockSpec((HALO, W3), lambda i: (jnp.maximum(i * per - 1, 0), 0)),
                  pl.BlockSpec((HALO, W3), nxt),
                  pl.BlockSpec((ts, W), lambda i: (i, 0)),
                  pl.BlockSpec((HALO, W), nxt),
                  pl.BlockSpec((SC_CONV_WIDTH, W), lambda i: (0, 0))],
        out_specs=(pl.BlockSpec((ts, W3), lambda i: (i, 0)), pl.BlockSpec((8, W), lambda i: (0, 0))),
        scratch_shapes=[pltpu.VMEM((HALO + ts, W), F32), pltpu.VMEM((ts + HALO, W), F32),
                        pltpu.VMEM((8, HALO + ts, W), F32), pltpu.VMEM((8, ts + HALO, W), F32)],
        compiler_params=_cparams("arbitrary"),
    )(u3, u3, u3, dy, dy, conv_w)


def _t5_causal_bucket(n):
    max_exact = NUM_BUCKETS // 2
    nf = jnp.maximum(n, 1).astype(F32)
    large = max_exact + (jnp.log(nf / max_exact) / math.log(REL_MAX_DISTANCE / max_exact)
                         * (NUM_BUCKETS - max_exact)).astype(jnp.int32)
    return jnp.where(n < max_exact, n, jnp.minimum(large, NUM_BUCKETS - 1))


def _bucket_tables():
    steps = ATTN_BLK
    m = jnp.arange(steps)[:, None] + steps - jnp.arange(2 * steps)[None, :]
    return jnp.stack([_t5_causal_bucket(jnp.clip(m, 0, steps) * dil).astype(F32) for _, dil in DILATED_GROUPS])


def _bias_tables(rel_bias, buckets, *, name):
    steps = ATTN_BLK

    def body(tab_ref, bk_ref, o_ref):
        g = pl.program_id(0)
        bk = bk_ref[0]
        a_idx = lax.broadcasted_iota(jnp.int32, (steps, 2 * steps), 0)
        c_idx = lax.broadcasted_iota(jnp.int32, (steps, 2 * steps), 1)
        m = a_idx + steps - c_idx
        band = (m >= 0) & (m <= steps)
        band_first = band & (c_idx >= steps)
        for h in range(HEADS_PER_GROUP):
            bias = jnp.zeros((steps, 2 * steps), F32)
            for b in range(NUM_BUCKETS):
                bias = jnp.where(bk == float(b), tab_ref[b, g * HEADS_PER_GROUP + h], bias)
            o_ref[0, 0, h] = jnp.where(band_first, bias, NEG_INF)
            o_ref[0, 1, h] = jnp.where(band, bias, NEG_INF)

    return pl.pallas_call(
        body, name=name,
        out_shape=jax.ShapeDtypeStruct((N_GROUPS, 2, HEADS_PER_GROUP, steps, 2 * steps), F32),
        grid=(N_GROUPS,),
        in_specs=[pl.BlockSpec(memory_space=pltpu.SMEM),
                  pl.BlockSpec((1, steps, 2 * steps), lambda g: (g, 0, 0))],
        out_specs=pl.BlockSpec((1, 2, HEADS_PER_GROUP, steps, 2 * steps), lambda g: (g, 0, 0, 0, 0)),
        compiler_params=_cparams("parallel"),
    )(rel_bias, buckets)


def _lane_is_low():
    return lax.broadcasted_iota(jnp.int32, (1, LANES), 1) < HEAD_DIM


def _stack_heads(x2, low):
    zero = jnp.zeros_like(x2)
    return jnp.concatenate([jnp.where(low, x2, zero), jnp.where(low, zero, x2)], axis=0)


def _qkv_specs(nb):
    nqb = GROUP_QKV // ATTN_OUT

    def spec(t, prev):
        def idx(r, n):
            nn = jnp.minimum(n, nb - 1)
            row = jnp.maximum(nn - 1, 0) if prev else nn
            return (row, r * nqb + t)
        return pl.BlockSpec((ATTN_BLK, ATTN_OUT), idx)

    return [spec(0, False), spec(1, False), spec(1, True), spec(2, False), spec(2, True)]


def _attn_fwd(uv, bias, g, d, *, name):
    rows = uv.shape[0]

    def body(q_ref, kc_ref, kp_ref, vc_ref, vp_ref, bias_ref, o_ref, l_ref):
        n = pl.program_id(1)
        sel = jnp.minimum(n, 1)
        low = _lane_is_low()
        slabs = [slice(hp * LANES, (hp + 1) * LANES) for hp in range(HEADS_PER_GROUP // 2)]
        scores = [_dot_nt(_stack_heads(q_ref[:, sl] * (HEAD_DIM ** -0.5), low),
                          jnp.concatenate([kp_ref[:, sl], kc_ref[:, sl]], axis=0)) for sl in slabs]
        probs, dens_all, lses_all = [], [], []
        for hp, s in enumerate(scores):
            ps, dens, lses = [], [], []
            for hh in range(2):
                logits = s[hh * ATTN_BLK:(hh + 1) * ATTN_BLK] + bias_ref[sel, 2 * hp + hh]
                mx = jnp.max(logits, axis=-1, keepdims=True)
                p = jnp.exp(logits - mx)
                den = jnp.sum(p, axis=-1, keepdims=True)
                ps.append(p.astype(BF16))
                dens.append(den)
                lses.append(jnp.broadcast_to(mx + jnp.log(den), (ATTN_BLK, LANES)))
            probs.append(jnp.concatenate(ps, axis=0))
            dens_all.append(dens)
            lses_all.append(lses)
        for hp, sl in enumerate(slabs):
            v2 = jnp.concatenate([vp_ref[:, sl], vc_ref[:, sl]], axis=0)
            pv = jnp.dot(probs[hp], v2, preferred_element_type=F32)
            dens, lses = dens_all[hp], lses_all[hp]
            o_ref[:, sl] = jnp.where(low, pv[0:ATTN_BLK] / dens[0], pv[ATTN_BLK:2 * ATTN_BLK] / dens[1])
            l_ref[:, sl] = jnp.where(low, lses[0], lses[1])

    out_spec = pl.BlockSpec((ATTN_BLK, ATTN_OUT), lambda r, n: (n, r))
    return pl.pallas_call(
        body, name=name,
        out_shape=(jax.ShapeDtypeStruct((rows, d * ATTN_OUT), F32),) * 2,
        grid=(d, rows // ATTN_BLK),
        in_specs=_qkv_specs(rows // ATTN_BLK) + [pl.BlockSpec((None, 2, HEADS_PER_GROUP, ATTN_BLK, 2 * ATTN_BLK),
                                                              lambda r, n: (g, 0, 0, 0, 0))],
        out_specs=(out_spec, out_spec),
        compiler_params=_cparams("parallel", "parallel"),
    )(uv, uv, uv, uv, uv, bias)


def _chunk_scratch(n, width):
    return pltpu.VMEM((width // LANES, n, LANES), F32)


def _to_chunks(scr, val):
    for c in range(scr.shape[0]):
        scr[c] = val[:, c * LANES:(c + 1) * LANES]


def _from_chunks(scr):
    return jnp.concatenate([scr[c] for c in range(scr.shape[0])], axis=1)


def _slabs_from_chunks(dst_ref, scr, d, dtype):
    nc, n, _ = scr.shape
    for r in range(d):
        for c in range(nc):
            col = r * nc * LANES + c * LANES
            dst_ref[:, col:col + LANES] = scr[c, pl.ds(r, n // d, stride=d), :].astype(dtype)


def _chunks_from_slabs(scr, src_ref, d):
    nc, n, _ = scr.shape
    for r in range(d):
        for c in range(nc):
            col = r * nc * LANES + c * LANES
            scr[c, pl.ds(r, n // d, stride=d), :] = src_ref[:, col:col + LANES]


def _attn_merge(outs, lses, ya, *, name):
    S, C = ya.shape
    tm = _tile(S, (256,))
    dils = [dil for _, dil in DILATED_GROUPS]

    def body(o0, o1, o2, l0, l1, l2, ya_ref, cat_ref, out_ref, lse_ref, so1, so2, sl1, sl2):
        _chunks_from_slabs(so1, o1, dils[1])
        _chunks_from_slabs(so2, o2, dils[2])
        _chunks_from_slabs(sl1, l1, dils[1])
        _chunks_from_slabs(sl2, l2, dils[2])
        a0, a1, a2 = l0[...], _from_chunks(sl1), _from_chunks(sl2)
        m = jnp.maximum(jnp.maximum(a0, a1), a2)
        e0, e1, e2 = jnp.exp(a0 - m), jnp.exp(a1 - m), jnp.exp(a2 - m)
        den = e0 + e1 + e2
        out = (e0 * o0[...] + e1 * _from_chunks(so1) + e2 * _from_chunks(so2)) / den
        out_ref[...] = out
        lse_ref[...] = m + jnp.log(den)
        cat_ref[:, 0:C] = ya_ref[...]
        cat_ref[:, C:C + ATTN_OUT] = out.astype(cat_ref.dtype)

    blk = pl.BlockSpec((tm, ATTN_OUT), lambda i: (i, 0))
    vblk = [pl.BlockSpec((tm // d, d * ATTN_OUT), lambda i: (i, 0)) for d in dils]
    assert dils[0] == 1
    return pl.pallas_call(
        body, name=name,
        out_shape=(jax.ShapeDtypeStruct((S, C + ATTN_OUT), BF16), jax.ShapeDtypeStruct((S, ATTN_OUT), F32),
                   jax.ShapeDtypeStruct((S, ATTN_OUT), F32)),
        grid=(S // tm,),
        in_specs=vblk + vblk + [pl.BlockSpec((tm, C), lambda i: (i, 0))],
        out_specs=(pl.BlockSpec((tm, C + ATTN_OUT), lambda i: (i, 0)), blk, blk),
        scratch_shapes=[_chunk_scratch(tm, ATTN_OUT)] * 4,
        compiler_params=_cparams("parallel"),
    )(*outs, *lses, ya)


def _attn_prep(dcat, outf, lse, *, name):
    S, CW = dcat.shape
    C = CW - ATTN_OUT
    tm = _tile(S, (256,))
    dils = [dil for _, dil in DILATED_GROUPS]
    assert dils[0] == 1
    ones = np.kron(np.eye(HEADS_PER_GROUP, dtype=np.float32), np.ones((HEAD_DIM, HEAD_DIM), np.float32))

    nviews = 3 * (len(dils) - 1)

    def body(dcat_ref, out_ref, l_ref, ones_ref, dyb_ref, dl_ref, *rest):
        views = rest[:nviews]
        s_dyb, s_dl, s_l = rest[nviews:]
        dyb = dcat_ref[:, C:CW]
        dyb_ref[...] = dyb.astype(BF16)
        prod = dyb * out_ref[...]
        ov = ones_ref[...]
        hi, mid, lo = _split_bf16(prod)
        delta = (jnp.dot(hi, ov, preferred_element_type=F32)
                 + jnp.dot(mid, ov, preferred_element_type=F32)
                 + jnp.dot(lo, ov, preferred_element_type=F32))
        dl_ref[...] = delta
        _to_chunks(s_dyb, dyb)
        _to_chunks(s_dl, delta)
        _to_chunks(s_l, l_ref[...])
        for gi, d in enumerate(dils[1:]):
            dyb_v, dl_v, l_v = views[3 * gi:3 * gi + 3]
            _slabs_from_chunks(dyb_v, s_dyb, d, BF16)
            _slabs_from_chunks(dl_v, s_dl, d, F32)
            _slabs_from_chunks(l_v, s_l, d, F32)

    blk = pl.BlockSpec((tm, ATTN_OUT), lambda i: (i, 0))
    view_shapes, view_specs = [], []
    for d in dils[1:]:
        for dt in (BF16, F32, F32):
            view_shapes.append(jax.ShapeDtypeStruct((S // d, d * ATTN_OUT), dt))
            view_specs.append(pl.BlockSpec((tm // d, d * ATTN_OUT), lambda i: (i, 0)))
    res = pl.pallas_call(
        body, name=name,
        out_shape=(jax.ShapeDtypeStruct((S, ATTN_OUT), BF16), jax.ShapeDtypeStruct((S, ATTN_OUT), F32),
                   *view_shapes),
        grid=(S // tm,),
        in_specs=[pl.BlockSpec((tm, CW), lambda i: (i, 0)), blk, blk,
                  pl.BlockSpec((ATTN_OUT, ATTN_OUT), lambda i: (0, 0))],
        out_specs=(blk, blk, *view_specs),
        scratch_shapes=[_chunk_scratch(tm, ATTN_OUT)] * 3,
        compiler_params=_cparams("parallel"),
    )(dcat, outf, lse, jnp.asarray(ones, BF16))
    return [(res[0], res[1], lse)] + [tuple(res[2 + 3 * gi:5 + 3 * gi]) for gi in range(len(dils) - 1)]


def _attn_bwd(uv, dov, lv, dv_, bias, g, d, *, name):
    rows = uv.shape[0]
    nb = rows // ATTN_BLK
    scale = HEAD_DIM ** -0.5

    def body(q_ref, kc_ref, kp_ref, vc_ref, vp_ref, do_ref, l_ref, dl_ref, bias_ref,
             out_ref, db_ref, dq_s, dk_s, dv_s):
        r = pl.program_id(0)
        n = pl.program_id(1)
        low = _lane_is_low()

        @pl.when((r == 0) & (n == 0))
        def _():
            db_ref[...] = jnp.zeros_like(db_ref)

        @pl.when(n == 0)
        def _():
            dq_s[...] = jnp.zeros_like(dq_s)
            dk_s[...] = jnp.zeros_like(dk_s)
            dv_s[...] = jnp.zeros_like(dv_s)

        @pl.when(n < nb)
        def _():
            sel = jnp.minimum(n, 1)
            lane = lax.broadcasted_iota(jnp.int32, (1, LANES), 1)
            slabs = [slice(hp * LANES, (hp + 1) * LANES) for hp in range(HEADS_PER_GROUP // 2)]
            keys = [jnp.concatenate([kp_ref[:, sl], kc_ref[:, sl]], axis=0) for sl in slabs]
            scores = [_dot_nt(_stack_heads(q_ref[:, sl] * scale, low), keys[hp]) for hp, sl in enumerate(slabs)]
            dps = [_dot_nt(_stack_heads(do_ref[:, sl], low),
                           jnp.concatenate([vp_ref[:, sl], vc_ref[:, sl]], axis=0)) for sl in slabs]
            stacked = []
            for hp, sl in enumerate(slabs):
                lse2 = l_ref[:, sl]
                dl2 = dl_ref[:, sl]
                pbs, dsbs = [], []
                for hh in range(2):
                    rows = slice(hh * ATTN_BLK, (hh + 1) * ATTN_BLK)
                    one = lane == hh * HEAD_DIM
                    lse_col = jnp.sum(jnp.where(one, lse2, 0.0), axis=-1, keepdims=True)
                    dl_col = jnp.sum(jnp.where(one, dl2, 0.0), axis=-1, keepdims=True)
                    p = jnp.exp(scores[hp][rows] + bias_ref[sel, 2 * hp + hh] - lse_col)
                    ds = p * (dps[hp][rows] - dl_col)
                    db_ref[2 * hp + hh] += ds
                    pbs.append(p.astype(BF16))
                    dsbs.append((ds * scale).astype(BF16))
                stacked.append((jnp.concatenate(dsbs, axis=0), jnp.concatenate(dsbs, axis=1),
                                jnp.concatenate(pbs, axis=1)))
            nk2 = 2 * ATTN_BLK
            for hp, sl in enumerate(slabs):
                ds_rows, ds_cols, p_cols = stacked[hp]
                dq = jnp.dot(ds_rows, keys[hp], preferred_element_type=F32)
                dk = _dot_tn(ds_cols, q_ref[:, sl])
                dv = _dot_tn(p_cols, do_ref[:, sl])
                dq2 = jnp.where(low, dq[0:ATTN_BLK], dq[ATTN_BLK:nk2])
                dk2 = jnp.where(low, dk[0:nk2], dk[nk2:2 * nk2])
                dv2 = jnp.where(low, dv[0:nk2], dv[nk2:2 * nk2])
                out_ref[:, sl] = dq_s[:, sl].astype(out_ref.dtype)
                dq_s[:, sl] = dq2
                ksl = slice(ATTN_OUT + hp * LANES, ATTN_OUT + (hp + 1) * LANES)
                vsl = slice(2 * ATTN_OUT + hp * LANES, 2 * ATTN_OUT + (hp + 1) * LANES)
                out_ref[:, ksl] = (dk_s[:, sl] + dk2[0:ATTN_BLK]).astype(out_ref.dtype)
                dk_s[:, sl] = dk2[ATTN_BLK:2 * ATTN_BLK]
                out_ref[:, vsl] = (dv_s[:, sl] + dv2[0:ATTN_BLK]).astype(out_ref.dtype)
                dv_s[:, sl] = dv2[ATTN_BLK:2 * ATTN_BLK]

        @pl.when(n == nb)
        def _():
            out_ref[:, 0:ATTN_OUT] = dq_s[...].astype(out_ref.dtype)
            out_ref[:, ATTN_OUT:2 * ATTN_OUT] = dk_s[...].astype(out_ref.dtype)
            out_ref[:, 2 * ATTN_OUT:GROUP_QKV] = dv_s[...].astype(out_ref.dtype)

    rowblk = pl.BlockSpec((ATTN_BLK, ATTN_OUT), lambda r, n: (jnp.minimum(n, nb - 1), r))
    return pl.pallas_call(
        body, name=name,
        out_shape=(jax.ShapeDtypeStruct((rows, d * GROUP_QKV), BF16),
                   jax.ShapeDtypeStruct((HEADS_PER_GROUP, ATTN_BLK, 2 * ATTN_BLK), F32)),
        grid=(d, nb + 1),
        in_specs=_qkv_specs(nb) + [
            rowblk, rowblk, rowblk,
            pl.BlockSpec((None, 2, HEADS_PER_GROUP, ATTN_BLK, 2 * ATTN_BLK), lambda r, n: (g, 0, 0, 0, 0))],
        out_specs=(pl.BlockSpec((ATTN_BLK, GROUP_QKV), lambda r, n: (jnp.maximum(n - 1, 0), r)),
                   pl.BlockSpec((HEADS_PER_GROUP, ATTN_BLK, 2 * ATTN_BLK), lambda r, n: (0, 0, 0))),
        scratch_shapes=[pltpu.VMEM((ATTN_BLK, ATTN_OUT), F32)] * 3,
        compiler_params=_cparams("arbitrary", "arbitrary"),
    )(uv, uv, uv, uv, uv, dov, lv, dv_, bias)


def _split_bf16(x):
    hi = x.astype(BF16)
    r1 = x - hi.astype(F32)
    mid = r1.astype(BF16)
    lo = (r1 - mid.astype(F32)).astype(BF16)
    return hi, mid, lo


RELBIAS_CHUNK = 4096


def _relbias_reduce(dbs, buckets, *, name):
    flat = ATTN_BLK * 2 * ATTN_BLK
    dbf = jnp.stack([db.reshape(HEADS_PER_GROUP, flat) for db in dbs])
    bkf = buckets.reshape(N_GROUPS, 1, flat)

    def body(db_ref, bk_ref, o_ref):
        c = pl.program_id(1)
        rows = lax.broadcasted_iota(jnp.int32, (LANES, RELBIAS_CHUNK), 0).astype(F32)
        onehot = jnp.where(rows == bk_ref[0], 1.0, 0.0).astype(BF16)
        hi, mid, lo = _split_bf16(db_ref[0])
        part = _dot_nt(hi, onehot) + _dot_nt(mid, onehot) + _dot_nt(lo, onehot)

        @pl.when(c == 0)
        def _():
            o_ref[0] = part

        @pl.when(c > 0)
        def _():
            o_ref[0] += part

    return pl.pallas_call(
        body, name=name,
        out_shape=jax.ShapeDtypeStruct((N_GROUPS, HEADS_PER_GROUP, LANES), F32),
        grid=(N_GROUPS, flat // RELBIAS_CHUNK),
        in_specs=[pl.BlockSpec((1, HEADS_PER_GROUP, RELBIAS_CHUNK), lambda g, c: (g, 0, c)),
                  pl.BlockSpec((1, 1, RELBIAS_CHUNK), lambda g, c: (g, 0, c))],
        out_specs=pl.BlockSpec((1, HEADS_PER_GROUP, LANES), lambda g, c: (g, 0, 0)),
        compiler_params=_cparams("parallel", "arbitrary"),
    )(dbf, bkf)


def _my_position():
    x, y, c = lax.axis_index("x"), lax.axis_index("y"), lax.axis_index("c")
    return x, y, c


def _linear(pos):
    return 4 * pos[0] + 2 * pos[1] + pos[2]


def _peer(pos, k):
    x, y, c = pos
    return ((1 - x) if k & 4 else x, (1 - y) if k & 2 else y, (1 - c) if k & 1 else c)


HBM_SPEC = pl.BlockSpec(memory_space=pltpu.HBM)
SEM_SPEC = pl.BlockSpec(memory_space=pltpu.SEMAPHORE)
DATAFLOW = pltpu.SideEffectType.DATAFLOW_SIDE_EFFECTING


def _exchange_copies(src, land, sems, send_window, recv_window, with_arrivals):
    send_sems, recv_sems, local_sems = sems
    T = len(src)
    me = _my_position()
    me_lin = _linear(me)
    local = [pltpu.make_async_copy(send_window(t, src[t], me_lin), recv_window(t, land[t], me_lin),
                                   local_sems.at[t]) for t in range(T)]
    sends, arrivals = [], []
    for t in range(T):
        for k in range(1, N_DEV):
            peer = _peer(me, k)
            peer_lin = _linear(peer)
            sem = t * (N_DEV - 1) + k - 1
            sends.append(pltpu.make_async_remote_copy(
                src_ref=send_window(t, src[t], peer_lin), dst_ref=recv_window(t, land[t], me_lin),
                send_sem=send_sems.at[sem], recv_sem=recv_sems.at[sem],
                device_id=peer, device_id_type=MESH))
            if with_arrivals:
                arrivals.append(pltpu.make_async_remote_copy(
                    src_ref=send_window(t, src[t], me_lin), dst_ref=recv_window(t, land[t], peer_lin),
                    send_sem=send_sems.at[sem], recv_sem=recv_sems.at[sem],
                    device_id=peer, device_id_type=MESH))
    return local, sends, arrivals


def _exchange_start(srcs, land_shapes, send_window, recv_window, *, name, dep=None):
    T = len(srcs)
    n_in = 2 * T + (1 if dep is not None else 0)

    def body(*refs):
        src = refs[:T]
        land = refs[T:2 * T]
        sems = refs[n_in:n_in + 3]
        token = refs[-1]
        local, sends, _ = _exchange_copies(src, land, sems, send_window, recv_window, False)
        for cp in local + sends:
            cp.start()
        token[...] = jnp.zeros_like(token)

    lands = [lax.empty(ls.shape, ls.dtype) for ls in land_shapes]
    operands = [pltpu.with_memory_space_constraint(a, pltpu.HBM) for a in list(srcs) + lands]
    outs = pl.pallas_call(
        body, name=name,
        out_shape=(pltpu.SemaphoreType.DMA((T * (N_DEV - 1),)), pltpu.SemaphoreType.DMA((T * (N_DEV - 1),)),
                   pltpu.SemaphoreType.DMA((T,)),
                   *[pltpu.HBM(a.shape, a.dtype) for a in operands],
                   jax.ShapeDtypeStruct((8, LANES), F32)),
        in_specs=[HBM_SPEC] * (2 * T) + ([ANY] if dep is not None else []),
        out_specs=(SEM_SPEC,) * 3 + (HBM_SPEC,) * (2 * T) + (VMEM_SPEC,),
        input_output_aliases={i: 3 + i for i in range(2 * T)},
        compiler_params=pltpu.CompilerParams(has_side_effects=DATAFLOW),
    )(*operands, *([dep] if dep is not None else []))
    return outs[:3], outs[3:3 + T], outs[3 + T:3 + 2 * T], outs[-1]


def _exchange_wait(started, after, send_window, recv_window, *, name):
    sems, srcs, lands, _ = started
    T = len(srcs)

    def body(*refs):
        src = refs[:T]
        land = refs[T:2 * T]
        sem_refs = refs[2 * T:2 * T + 3]
        local, sends, arrivals = _exchange_copies(src, land, sem_refs, send_window, recv_window, True)
        for cp in arrivals:
            cp.wait_recv()
        for cp in sends:
            cp.wait_send()
        for cp in local:
            cp.wait()

    outs = pl.pallas_call(
        body, name=name,
        out_shape=tuple(pltpu.HBM(a.shape, a.dtype) for a in list(srcs) + list(lands)),
        in_specs=[HBM_SPEC] * (2 * T) + [SEM_SPEC] * 3 + [ANY],
        out_specs=(HBM_SPEC,) * (2 * T),
        input_output_aliases={i: i for i in range(2 * T)},
        compiler_params=pltpu.CompilerParams(has_side_effects=DATAFLOW),
    )(*srcs, *lands, *sems, after)
    return outs[T:]


def _shard_window(kind, width):
    def win(ref, lin):
        if kind == "slot":
            return ref.at[lin]
        if kind == "col":
            return ref.at[:, pl.ds(pl.multiple_of(lin * width, LANES), width)]
        if kind == "row":
            return ref.at[pl.ds(pl.multiple_of(lin * width, 8), width), :]
        if kind == "lcol":
            return ref.at[:, :, pl.ds(pl.multiple_of(lin * width, LANES), width)]
        if kind == "lrow":
            return ref.at[:, pl.ds(pl.multiple_of(lin * width, 8), width), :]
        raise ValueError(kind)
    return win


def _shard_windows(kinds, shard_shapes):
    return [_shard_window(k, (ss[-1] if k in ("col", "lcol") else ss[-2])) for k, ss in zip(kinds, shard_shapes)]


def _allgather_start(shards, kinds, full_shapes, *, name, dep=None):
    wins = _shard_windows(kinds, [s.shape for s in shards])
    send_window = lambda t, ref, lin: ref
    recv_window = lambda t, ref, lin: wins[t](ref, lin)
    started = _exchange_start(shards, [jax.ShapeDtypeStruct(fs, s.dtype) for fs, s in zip(full_shapes, shards)],
                              send_window, recv_window, name=name + "_start", dep=dep)
    return started, lambda after: _exchange_wait(started, after, send_window, recv_window, name=name + "_wait")


def _scatter_start(fulls, kinds, shard_shapes, *, name):
    wins = _shard_windows(kinds, shard_shapes)
    send_window = lambda t, ref, lin: wins[t](ref, lin)
    recv_window = lambda t, ref, lin: ref.at[lin]
    started = _exchange_start(
        fulls, [jax.ShapeDtypeStruct((N_DEV,) + tuple(ss), f.dtype) for ss, f in zip(shard_shapes, fulls)],
        send_window, recv_window, name=name + "_start")
    return started, lambda after: _exchange_wait(started, after, send_window, recv_window, name=name + "_wait")


def _small_gather(pack, *, reduce, name):
    R = pack.shape[0]

    def body(p_ref, o_ref, *rest):
        if reduce:
            buf, send_sems, recv_sems = rest
        else:
            buf = o_ref
            send_sems, recv_sems = rest
        me = _my_position()
        me_lin = _linear(me)
        buf[me_lin] = p_ref[...]
        sends = []
        for k in range(1, N_DEV):
            peer = _peer(me, k)
            cp = pltpu.make_async_remote_copy(
                src_ref=p_ref, dst_ref=buf.at[me_lin],
                send_sem=send_sems.at[k - 1], recv_sem=recv_sems.at[k - 1],
                device_id=peer, device_id_type=MESH)
            cp.start()
            sends.append(cp)
        for k in range(1, N_DEV):
            peer = _peer(me, k)
            pltpu.make_async_remote_copy(
                src_ref=p_ref, dst_ref=buf.at[_linear(peer)],
                send_sem=send_sems.at[k - 1], recv_sem=recv_sems.at[k - 1],
                device_id=peer, device_id_type=MESH).wait_recv()
        for cp in sends:
            cp.wait_send()
        if reduce:
            acc = buf[0]
            for s in range(1, N_DEV):
                acc = acc + buf[s]
            o_ref[...] = acc

    scratch = [pltpu.SemaphoreType.DMA((N_DEV - 1,)), pltpu.SemaphoreType.DMA((N_DEV - 1,))]
    if reduce:
        scratch = [pltpu.VMEM((N_DEV, R, LANES), F32)] + scratch
        out_shape = jax.ShapeDtypeStruct((R, LANES), F32)
    else:
        out_shape = jax.ShapeDtypeStruct((N_DEV, R, LANES), F32)
    return pl.pallas_call(
        body, name=name, out_shape=out_shape,
        in_specs=[VMEM_SPEC], out_specs=VMEM_SPEC, scratch_shapes=scratch,
        compiler_params=pltpu.CompilerParams(has_side_effects=True, vmem_limit_bytes=VMEM_LIMIT),
    )(pack)


def _adamw_math(w, g, m, v):
    m = ADAM_B1 * m + (1.0 - ADAM_B1) * g
    v = ADAM_B2 * v + (1.0 - ADAM_B2) * jnp.square(g)
    m_hat = m / (1.0 - ADAM_B1 ** ADAM_STEP)
    v_hat = v / (1.0 - ADAM_B2 ** ADAM_STEP)
    delta = -ADAM_LR * (m_hat / (jnp.sqrt(v_hat) + ADAM_EPS) + ADAM_WD * w)
    return delta, m, v


def _adamw_from_partials(parts, w, m, v, *, name):
    R, C = w.shape
    rl = parts[0].shape[1]
    assert all(p.shape == (N_DEV, rl, C) for p in parts) and rl * len(parts) == R
    tr = _tile(rl, (256, 128, 64, 32, 16))
    per = rl // tr
    L = len(parts)

    def body(*refs):
        p_refs = refs[:L]
        w_ref, m_ref, v_ref, g_ref, d_ref, nm_ref, nv_ref = refs[L:]
        i = pl.program_id(0)
        for l in range(L):
            @pl.when((i >= l * per) & (i < (l + 1) * per))
            def _(l=l):
                p_ref = p_refs[l]
                g = p_ref[0].astype(F32)
                for s in range(1, N_DEV):
                    g = g + p_ref[s].astype(F32)
                d, nm, nv = _adamw_math(w_ref[...], g, m_ref[...], v_ref[...])
                g_ref[...] = g
                d_ref[...] = d
                nm_ref[...] = nm
                nv_ref[...] = nv

    blk = pl.BlockSpec((tr, C), lambda i: (i, 0))
    part_specs = [pl.BlockSpec((N_DEV, tr, C), lambda i, l=l: (0, jnp.clip(i - l * per, 0, per - 1), 0))
                  for l in range(L)]
    return pl.pallas_call(
        body, name=name,
        out_shape=(jax.ShapeDtypeStruct((R, C), F32),) * 4,
        grid=(R // tr,),
        in_specs=part_specs + [blk, blk, blk],
        out_specs=(blk,) * 4,
        compiler_params=_cparams("parallel"),
    )(*parts, w, m, v)


def _adamw_small(g, w, m, v, *, name):
    def body(g_ref, w_ref, m_ref, v_ref, d_ref, nm_ref, nv_ref):
        d, nm, nv = _adamw_math(w_ref[...], g_ref[...], m_ref[...], v_ref[...])
        d_ref[...] = d
        nm_ref[...] = nm
        nv_ref[...] = nv

    return pl.pallas_call(
        body, name=name,
        out_shape=(jax.ShapeDtypeStruct(g.shape, F32),) * 3,
        in_specs=[VMEM_SPEC] * 4, out_specs=(VMEM_SPEC,) * 3,
    )(g, w, m, v)


def _pack_rows(pieces):
    flat = jnp.concatenate([p.reshape(-1) for p in pieces])
    n = flat.shape[0]
    padded = -(-n // (8 * LANES)) * (8 * LANES)
    return jnp.pad(flat, (0, padded - n)).reshape(padded // LANES, LANES)


def _unpack_rows(pack, shapes):
    flat = pack.reshape(-1)
    out, pos = [], 0
    for s in shapes:
        n = int(np.prod(s))
        out.append(flat[pos:pos + n].reshape(s))
        pos += n
    return out


def kernel(x, rel_bias, ab_norm, ab_w_in, ab_conv_w, ab_conv_b, ab_ln_g, ab_ln_b, ab_w_out, sc_norm, sc_w_in, sc_conv_w, sc_w_out, mlp_norm, mlp_w_up, mlp_w_down, final_norm, loss_target, m_rel_bias, m_ab_norm, m_ab_w_in, m_ab_conv_w, m_ab_conv_b, m_ab_ln_g, m_ab_ln_b, m_ab_w_out, m_sc_norm, m_sc_w_in, m_sc_conv_w, m_sc_w_out, m_mlp_norm, m_mlp_w_up, m_mlp_w_down, m_final_norm, v_rel_bias, v_ab_norm, v_ab_w_in, v_ab_conv_w, v_ab_conv_b, v_ab_ln_g, v_ab_ln_b, v_ab_w_out, v_sc_norm, v_sc_w_in, v_sc_conv_w, v_sc_w_out, v_mlp_norm, v_mlp_w_up, v_mlp_w_down, v_final_norm):
    S, D = x.shape[1], x.shape[2]
    CA = ab_conv_b.shape[1]
    C2 = 2 * CA
    AB_IN = C2 + ATTN_IN
    me_lin = _linear(_my_position())
    xs = x.reshape(S, D)
    tgt = loss_target.reshape(S, D)

    cw_sh = ab_conv_w.shape[2]
    scn_sh = sc_norm.shape[1]
    scw_sh = sc_conv_w.shape[2]
    small_sh_shapes = [(CONV_A_WIDTH, cw_sh), (scn_sh,), (SC_CONV_WIDTH, scw_sh)]
    small_params = _small_gather(_pack_rows([ab_conv_w[0], sc_norm[0], sc_conv_w[0]]), reduce=False,
                                 name="allgather_small_params")
    w_in_sh = ab_w_in[0].astype(BF16)
    ag_ab, wait_ab = _allgather_start(
        [w_in_sh, ab_w_out[0].astype(BF16)], ["slot", "row"],
        [(N_DEV,) + w_in_sh.shape, (N_DEV * ab_w_out.shape[1], D)], name="allgather_ab", dep=small_params)
    ag_mlp, wait_mlp = _allgather_start(
        [mlp_w_up.astype(BF16), mlp_w_down.astype(BF16)], ["lcol", "lrow"],
        [(2, D, N_DEV * mlp_w_up.shape[2]), (2, N_DEV * mlp_w_down.shape[1], D)], name="allgather_mlp",
        dep=ag_ab[3])
    ag_sc, wait_sc = _allgather_start(
        [sc_w_in[0].astype(BF16), sc_w_out[0].astype(BF16)], ["col", "row"],
        [(D, N_DEV * sc_w_in.shape[2]), (N_DEV * sc_w_out.shape[1], D)], name="allgather_sc",
        dep=ag_mlp[3])

    per_dev = [_unpack_rows(small_params[s], small_sh_shapes) for s in range(N_DEV)]
    conv_w_full = jnp.concatenate([p[0] for p in per_dev], axis=1)
    sc_norm_full = jnp.concatenate([p[1] for p in per_dev], axis=0)[None]
    sc_conv_full = jnp.concatenate([p[2] for p in per_dev], axis=1)

    buckets = _bucket_tables()
    biases = _bias_tables(rel_bias, buckets, name="bias_tables")

    dils = [dil for _, dil in DILATED_GROUPS]
    n0_all = _rmsnorm_fwd(xs, ab_norm, name="norm_ab", dep=ag_sc[3], views=dils[1:])
    n0 = n0_all[0]
    w_in_g, w_out = wait_ab(n0)
    w_in = jnp.transpose(w_in_g, (1, 0, 2)).reshape(D, AB_IN)
    w_c = w_in[:, :C2]
    w_q = w_in[:, C2:]
    w_grp = [jnp.concatenate([w_q[:, t * N_GROUPS * ATTN_OUT + g * ATTN_OUT:][:, :ATTN_OUT] for t in range(3)], axis=1)
             for g in range(N_GROUPS)]
    uc = _mm_nn(n0, w_c, out_dtype=F32, name="mm_ab_in_conv")
    uqs = [_mm_nn(n0_all[g], w_grp[g], out_dtype=BF16, slabs=dils[g], name=f"mm_ab_in_qkv{g}")
           for g in range(N_GROUPS)]
    ya, hglu, ct = _conv_fwd(uc, conv_w_full, ab_conv_b, ab_ln_g, ab_ln_b, name="conv_fwd")
    outs, lses = zip(*[_attn_fwd(uqs[g], biases, g, dils[g], name=f"attn_fwd_{g}") for g in range(N_GROUPS)])
    cat, outf, lse = _attn_merge(outs, lses, ya, name="attn_merge")
    h1 = _mm_nn(cat, w_out, out_dtype=F32, residual=xs, name="mm_ab_out")
    w_up, w_dn = wait_mlp(h1)
    n1, z0 = _norm_mm_nn(h1, mlp_norm[0:1], (w_up, 0), out_dtype=BF16, name="norm_mm_up0", wide=True)
    h2 = _mm_nn(z0, (w_dn, 0), out_dtype=F32, residual=h1, a_fn=_relu_sq, name="mm_down0", wide=True)
    w_sc_in, w_sc_out = wait_sc(h2)
    n2, u3 = _norm_mm_nn(h2, sc_norm_full, w_sc_in, out_dtype=BF16, name="norm_mm_sc_in", wide=True)
    ysc = _sc_fwd(u3, sc_conv_full, name="sc_fwd")
    h3 = _mm_nn(ysc, w_sc_out, out_dtype=F32, residual=h2, name="mm_sc_out")
    n3, z1 = _norm_mm_nn(h3, mlp_norm[1:2], (w_up, 1), out_dtype=BF16, name="norm_mm_up1", wide=True)
    h4 = _mm_nn(z1, (w_dn, 1), out_dtype=F32, residual=h3, a_fn=_relu_sq, name="mm_down1", wide=True)

    def dz_epilogue(acc, z):
        return acc * (2.0 * jnp.maximum(z.astype(F32), 0.0))

    dh4, dh4b, acc_final = _loss_bwd(h4, tgt, final_norm[None], name="loss_bwd")
    dz1 = _mm_nt([(dh4b, (w_dn, 1))], out_dtype=BF16, epilogue=dz_epilogue, extra=z1, name="mm_d_down1")
    g_dn1 = _mm_tn(z1, dh4b, a_fn=_relu_sq, name="mm_gw_down1")
    g_up1 = _mm_tn(n3, dz1, name="mm_gw_up1")
    rs_mlp1, wait_rs_mlp1 = _scatter_start([g_up1, g_dn1], ["col", "row"],
                                           [mlp_w_up.shape[1:], mlp_w_down.shape[1:]], name="scatter_mlp1")
    dh3, dh3b, acc_mlp1 = _mm_nt_rms_bwd(dz1, (w_up, 1), h3, mlp_norm[1:2], dh4, name="mm_d_up1_norm_bwd",
                                         dep=rs_mlp1[3])

    dysc = _mm_nt([(dh3b, w_sc_out)], out_dtype=F32, name="mm_d_sc_out")
    g_sc_out = _mm_tn(ysc, dh3b, name="mm_gw_sc_out")
    du3, acc_scw = _sc_bwd(u3, dysc, sc_conv_full, name="sc_bwd")
    g_sc_in = _mm_tn(n2, du3, name="mm_gw_sc_in")
    rs_sc, wait_rs_sc = _scatter_start([g_sc_in, g_sc_out], ["col", "row"],
                                       [sc_w_in.shape[1:], sc_w_out.shape[1:]], name="scatter_sc")
    dh2, dh2b, acc_sc = _mm_nt_rms_bwd(du3, w_sc_in, h2, sc_norm_full, dh3, name="mm_d_sc_in_norm_bwd",
                                       dep=rs_sc[3])

    dz0 = _mm_nt([(dh2b, (w_dn, 0))], out_dtype=BF16, epilogue=dz_epilogue, extra=z0, name="mm_d_down0")
    g_dn0 = _mm_tn(z0, dh2b, a_fn=_relu_sq, name="mm_gw_down0")
    g_up0 = _mm_tn(n1, dz0, name="mm_gw_up0")
    rs_mlp0, wait_rs_mlp0 = _scatter_start([g_up0, g_dn0], ["col", "row"],
                                           [mlp_w_up.shape[1:], mlp_w_down.shape[1:]], name="scatter_mlp0")
    dh1, dh1b, acc_mlp0 = _mm_nt_rms_bwd(dz0, (w_up, 0), h1, mlp_norm[0:1], dh2, name="mm_d_up0_norm_bwd",
                                         dep=rs_mlp0[3])

    dcat = _mm_nt([(dh1b, w_out)], out_dtype=F32, name="mm_d_ab_out")
    g_ab_out = _mm_tn(cat, dh1b, name="mm_gw_ab_out")
    prep = _attn_prep(dcat, outf, lse, name="attn_prep")
    dqkv, dbs = zip(*[_attn_bwd(uqs[g], prep[g][0], prep[g][2], prep[g][1], biases, g, dils[g],
                                name=f"attn_bwd_{g}") for g in range(N_GROUPS)])
    drel = _relbias_reduce(dbs, buckets, name="relbias_reduce")
    dc, acc_conv = _conv_bwd_ln(ct, dcat, hglu, ab_ln_g, ab_ln_b, name="conv_bwd_ln")
    duc = _conv_bwd_in(dc, uc, conv_w_full, name="conv_bwd_in")
    g_wc = _mm_tn(n0, duc, name="mm_gw_ab_in_conv")
    g_wgrp = [_mm_tn(n0_all[g], dqkv[g], slabs=dils[g], name=f"mm_gw_ab_in_qkv{g}") for g in range(N_GROUPS)]
    g_wq = jnp.concatenate([g_wgrp[g][:, t * ATTN_OUT:(t + 1) * ATTN_OUT]
                            for t in range(3) for g in range(N_GROUPS)], axis=1)
    g_w_in = jnp.concatenate([g_wc, g_wq], axis=1).reshape(D, N_DEV, AB_IN // N_DEV).transpose(1, 0, 2)
    rs_ab, wait_rs_ab = _scatter_start([g_w_in, g_ab_out], ["slot", "row"],
                                       [w_in_sh.shape, ab_w_out.shape[1:]], name="scatter_ab")
    dn0 = _mm_nt([(duc, w_c), (dqkv[0], w_grp[0])], out_dtype=F32, name="mm_d_ab_in", dep=rs_ab[3])
    dn0_views = [(_mm_nt([(dqkv[g], w_grp[g])], out_dtype=F32, slabs=dils[g], name=f"mm_d_ab_in_qkv{g}"), dils[g])
                 for g in range(1, N_GROUPS)]
    grad_x, grad_xb, acc_ab = _rms_bwd(xs, ab_norm, dn0, dh1, name="norm_ab_bwd", dn_views=dn0_views)

    small_full = [drel[:, :, :NUM_BUCKETS].transpose(2, 0, 1).reshape(NUM_BUCKETS, N_GROUPS * HEADS_PER_GROUP),
                  acc_ab[0], acc_conv[0:CONV_A_WIDTH], acc_conv[32],
                  acc_conv[33], acc_conv[34], acc_sc[0], acc_scw[0:SC_CONV_WIDTH],
                  jnp.stack([acc_mlp0[0], acc_mlp1[0]]), acc_final[0], acc_final[1]]
    small_full_shapes = [p.shape for p in small_full]
    summed = _unpack_rows(_small_gather(_pack_rows(small_full), reduce=True, name="allreduce_small"),
                          small_full_shapes)
    (s_rel, s_abn, s_cw, s_cb, s_lg, s_lb, s_scn, s_scw, s_mlpn, s_fn, s_err) = summed
    loss = (0.5 / D) * jnp.sum(s_err)
    small_grads = {
        "rel_bias": s_rel, "ab_norm": s_abn[None],
        "ab_conv_w": lax.dynamic_slice_in_dim(s_cw, me_lin * cw_sh, cw_sh, axis=1)[None],
        "ab_conv_b": s_cb[None], "ab_ln_g": s_lg[None], "ab_ln_b": s_lb[None],
        "sc_norm": lax.dynamic_slice_in_dim(s_scn, me_lin * scn_sh, scn_sh, axis=0)[None],
        "sc_conv_w": lax.dynamic_slice_in_dim(s_scw, me_lin * scw_sh, scw_sh, axis=1)[None],
        "mlp_norm": s_mlpn, "final_norm": s_fn,
    }
    small_w = {"rel_bias": (rel_bias, m_rel_bias, v_rel_bias), "ab_norm": (ab_norm, m_ab_norm, v_ab_norm),
               "ab_conv_w": (ab_conv_w, m_ab_conv_w, v_ab_conv_w), "ab_conv_b": (ab_conv_b, m_ab_conv_b, v_ab_conv_b),
               "ab_ln_g": (ab_ln_g, m_ab_ln_g, v_ab_ln_g), "ab_ln_b": (ab_ln_b, m_ab_ln_b, v_ab_ln_b),
               "sc_norm": (sc_norm, m_sc_norm, v_sc_norm), "sc_conv_w": (sc_conv_w, m_sc_conv_w, v_sc_conv_w),
               "mlp_norm": (mlp_norm, m_mlp_norm, v_mlp_norm), "final_norm": (final_norm, m_final_norm, v_final_norm)}
    small_names = list(small_grads)
    small_shapes = [small_grads[n].shape for n in small_names]
    d_pack, m_pack, v_pack = _adamw_small(
        _pack_rows([small_grads[n] for n in small_names]), _pack_rows([small_w[n][0] for n in small_names]),
        _pack_rows([small_w[n][1] for n in small_names]), _pack_rows([small_w[n][2] for n in small_names]),
        name="adamw_small")
    small = {n: (small_grads[n], d, nm_, nv_) for n, d, nm_, nv_ in zip(
        small_names, _unpack_rows(d_pack, small_shapes), _unpack_rows(m_pack, small_shapes),
        _unpack_rows(v_pack, small_shapes))}

    p_up1, p_dn1 = wait_rs_mlp1(grad_xb)
    p_sc_in, p_sc_out = wait_rs_sc(grad_xb)
    p_up0, p_dn0 = wait_rs_mlp0(grad_xb)
    p_w_in, p_ab_out = wait_rs_ab(grad_xb)
    big = {}
    for nm, parts, w, m, v in (("ab_w_in", [p_w_in], ab_w_in, m_ab_w_in, v_ab_w_in),
                               ("ab_w_out", [p_ab_out], ab_w_out, m_ab_w_out, v_ab_w_out),
                               ("sc_w_in", [p_sc_in], sc_w_in, m_sc_w_in, v_sc_w_in),
                               ("sc_w_out", [p_sc_out], sc_w_out, m_sc_w_out, v_sc_w_out),
                               ("mlp_w_up", [p_up0, p_up1], mlp_w_up, m_mlp_w_up, v_mlp_w_up),
                               ("mlp_w_down", [p_dn0, p_dn1], mlp_w_down, m_mlp_w_down, v_mlp_w_down)):
        C = w.shape[-1]
        res = _adamw_from_partials(parts, w.reshape(-1, C), m.reshape(-1, C), v.reshape(-1, C), name="adamw_" + nm)
        big[nm] = tuple(r.reshape(w.shape) for r in res)

    order = ["rel_bias", "ab_norm", "ab_w_in", "ab_conv_w", "ab_conv_b", "ab_ln_g", "ab_ln_b", "ab_w_out",
             "sc_norm", "sc_w_in", "sc_conv_w", "sc_w_out", "mlp_norm", "mlp_w_up", "mlp_w_down", "final_norm"]
    allres = {**big, **small}
    return (loss, grad_x.reshape(x.shape),
            *[allres[n][0] for n in order], *[allres[n][1] for n in order],
            *[allres[n][2] for n in order], *[allres[n][3] for n in order])
```

```python
import functools
import math

import numpy as np
import jax
import jax.numpy as jnp
from jax import lax
from jax.experimental import pallas as pl
from jax.experimental.pallas import tpu as pltpu

F32 = jnp.float32
BF16 = jnp.bfloat16

HEAD_DIM = 64
HEADS_PER_GROUP = 8
DILATED_GROUPS = ((128, 1), (512, 4), (2048, 16))
N_GROUPS = 3
ATTN_OUT = HEADS_PER_GROUP * HEAD_DIM
ATTN_IN = 3 * N_GROUPS * ATTN_OUT
GROUP_QKV = 3 * ATTN_OUT
ATTN_BLK = 128
CONV_A_WIDTH = 31
SC_CONV_WIDTH = 3
NUM_BUCKETS = 32
REL_MAX_DISTANCE = 2048
RMS_EPS = 1e-6
LN_EPS = 1e-5
NEG_INF = -1e30
ADAM_LR = 0.001
ADAM_B1 = 0.9
ADAM_B2 = 0.999
ADAM_EPS = 1e-08
ADAM_WD = 0.01
ADAM_STEP = 10

N_DEV = 8
HALO = 32
LANES = 128
VMEM_LIMIT = 56 * 1024 * 1024
MESH = pl.DeviceIdType.MESH
ANY = pl.BlockSpec(memory_space=pl.ANY)
VMEM_SPEC = pl.BlockSpec(memory_space=pltpu.VMEM)


def _tile(n, prefs):
    for t in prefs:
        if n % t == 0:
            return t
    return n


def _cparams(*sem):
    return pltpu.CompilerParams(dimension_semantics=sem, vmem_limit_bytes=VMEM_LIMIT)


def _relu_sq(z):
    return jnp.square(jnp.maximum(z, 0))


def _dot_nt(a, b):
    return lax.dot_general(a, b, (((1,), (1,)), ((), ())), preferred_element_type=F32)


def _dot_tn(a, b):
    return lax.dot_general(a, b, (((0,), (0,)), ((), ())), preferred_element_type=F32)


def _weight(b):
    if not isinstance(b, tuple):
        return b, b.shape, pl.BlockSpec
    arr, layer = b

    def spec(block, index_map):
        return pl.BlockSpec((None,) + tuple(block), lambda *g: (layer,) + tuple(index_map(*g)))

    return arr, arr.shape[1:], spec


def _mm_nn(a, b, *, out_dtype, name, residual=None, a_fn=None, slabs=1, wide=False):
    M, K = a.shape
    K //= slabs
    b, (_, N), b_spec = _weight(b)
    tm = _tile(M, (1024, 512, 256) if wide else (2048, 1024, 512, 256))
    tn = _tile(N, (1024, 512, 384, 256, 128) if wide else (512, 384, 256, 128))
    tk = _tile(K, (2048, 1024, 512, 256, 128) if wide else (1024, 512, 256, 128))
    nk = K // tk
    nj = N // tn
    has_res = residual is not None

    def body(*refs):
        if has_res:
            a_ref, b_ref, r_ref, o_ref = refs[:4]
        else:
            a_ref, b_ref, o_ref = refs[:3]
        av = a_ref[...]
        if a_fn is not None:
            av = a_fn(av)
        part = jnp.dot(av, b_ref[...], preferred_element_type=F32)

        def finish(acc):
            if has_res:
                acc = acc + r_ref[...]
            o_ref[...] = acc.astype(o_ref.dtype)

        if nk == 1:
            finish(part)
        else:
            acc_ref = refs[-1]
            k = pl.program_id(2)

            @pl.when(k == 0)
            def _():
                acc_ref[...] = part

            @pl.when((k > 0) & (k < nk - 1))
            def _():
                acc_ref[...] += part

            @pl.when(k == nk - 1)
            def _():
                finish(acc_ref[...] + part)

    in_specs = [pl.BlockSpec((tm, tk), lambda i, j, k: (i, (j // nj) * nk + k)),
                b_spec((tk, tn), lambda i, j, k: (k, j % nj))]
    args = [a, b]
    if has_res:
        in_specs.append(pl.BlockSpec((tm, tn), lambda i, j, k: (i, j)))
        args.append(residual)
    return pl.pallas_call(
        body, name=name,
        out_shape=jax.ShapeDtypeStruct((M, slabs * N), out_dtype),
        grid=(M // tm, slabs * nj, nk),
        in_specs=in_specs,
        out_specs=pl.BlockSpec((tm, tn), lambda i, j, k: (i, j)),
        scratch_shapes=[pltpu.VMEM((tm, tn), F32)] if nk > 1 else [],
        compiler_params=_cparams("parallel", "parallel", "arbitrary"),
    )(*args)


def _norm_mm_nn(h, g, b, *, out_dtype, name, wide=False):
    M, K = h.shape
    b, (_, N), b_spec = _weight(b)
    tm = _tile(M, (2048, 1024, 512, 256))
    tn = _tile(N, (1024, 512, 384, 256, 128) if wide else (512, 384, 256, 128))

    def body(h_ref, g_ref, b_ref, n_ref, o_ref):
        @pl.when(pl.program_id(1) == 0)
        def _():
            x = h_ref[...]
            r = lax.rsqrt(jnp.mean(x * x, axis=-1, keepdims=True) + RMS_EPS)
            n_ref[...] = (x * r * g_ref[...]).astype(BF16)

        o_ref[...] = jnp.dot(n_ref[...], b_ref[...], preferred_element_type=F32).astype(o_ref.dtype)

    return pl.pallas_call(
        body, name=name,
        out_shape=(jax.ShapeDtypeStruct((M, K), BF16), jax.ShapeDtypeStruct((M, N), out_dtype)),
        grid=(M // tm, N // tn),
        in_specs=[pl.BlockSpec((tm, K), lambda i, j: (i, 0)), pl.BlockSpec((1, K), lambda i, j: (0, 0)),
                  b_spec((K, tn), lambda i, j: (0, j))],
        out_specs=(pl.BlockSpec((tm, K), lambda i, j: (i, 0)), pl.BlockSpec((tm, tn), lambda i, j: (i, j))),
        compiler_params=_cparams("parallel", "arbitrary"),
    )(h, g, b)


def _mm_nt(pairs, *, out_dtype, name, epilogue=None, extra=None, dep=None, slabs=1):
    assert slabs == 1 or len(pairs) == 1
    M = pairs[0][0].shape[0]
    weights = [_weight(p[1]) for p in pairs]
    Ko = weights[0][1][0]
    tm = _tile(M, (2048, 1024, 512, 256))
    to = _tile(Ko, (1024, 512, 256, 128))
    njo = Ko // to
    tks = [_tile(p[0].shape[1] // slabs, (1024, 768, 512, 256, 128)) for p in pairs]
    steps = [p[0].shape[1] // slabs // tk for p, tk in zip(pairs, tks)]
    offs = [sum(steps[:i]) for i in range(len(pairs))]
    nk = sum(steps)
    npair = len(pairs)
    has_extra = extra is not None

    def body(*refs):
        ab = refs[:2 * npair]
        pos = 2 * npair
        e_ref = None
        if has_extra:
            e_ref = refs[pos]
            pos += 1
        if dep is not None:
            pos += 1
        o_ref = refs[pos]
        acc_ref = refs[pos + 1]
        k = pl.program_id(2)

        @pl.when(k == 0)
        def _():
            acc_ref[...] = jnp.zeros_like(acc_ref)

        for p in range(npair):
            @pl.when((k >= offs[p]) & (k < offs[p] + steps[p]))
            def _(p=p):
                acc_ref[...] += _dot_nt(ab[2 * p][...], ab[2 * p + 1][...])

        @pl.when(k == nk - 1)
        def _():
            acc = acc_ref[...]
            if epilogue is not None:
                acc = epilogue(acc, e_ref[...] if has_extra else None)
            o_ref[...] = acc.astype(o_ref.dtype)

    in_specs, args = [], []
    for p, (a, b) in enumerate(pairs):
        def kidx(k, p=p):
            return jnp.clip(k - offs[p], 0, steps[p] - 1)
        in_specs.append(pl.BlockSpec((tm, tks[p]),
                                     lambda i, j, k, kidx=kidx, p=p: (i, (j // njo) * steps[p] + kidx(k))))
        in_specs.append(weights[p][2]((to, tks[p]), lambda i, j, k, kidx=kidx: (j % njo, kidx(k))))
        args += [a, weights[p][0]]
    if has_extra:
        in_specs.append(pl.BlockSpec((tm, to), lambda i, j, k: (i, j)))
        args.append(extra)
    if dep is not None:
        in_specs.append(ANY)
        args.append(dep)
    return pl.pallas_call(
        body, name=name,
        out_shape=jax.ShapeDtypeStruct((M, slabs * Ko), out_dtype),
        grid=(M // tm, slabs * njo, nk),
        in_specs=in_specs,
        out_specs=pl.BlockSpec((tm, to), lambda i, j, k: (i, j)),
        scratch_shapes=[pltpu.VMEM((tm, to), F32)],
        compiler_params=_cparams("parallel", "parallel", "arbitrary"),
    )(*args)


def _mm_tn(a, b, *, name, a_fn=None, slabs=1):
    M, K = a.shape
    K //= slabs
    N = b.shape[1] // slabs
    tm = _tile(M, (4096, 2048, 1024, 512, 256))
    tk = _tile(K, (1024, 768, 512, 384, 256, 128))
    tn = _tile(N, (1024, 768, 512, 384, 256, 128))
    nmi = M // tm
    nm = slabs * nmi
    nki, nnj = K // tk, N // tn

    def body(a_ref, b_ref, o_ref, acc_ref):
        m = pl.program_id(2)
        av = a_ref[...]
        if a_fn is not None:
            av = a_fn(av)
        part = _dot_tn(av, b_ref[...])
        if nm == 1:
            o_ref[...] = part.astype(o_ref.dtype)
            return

        @pl.when(m == 0)
        def _():
            acc_ref[...] = part

        @pl.when((m > 0) & (m < nm - 1))
        def _():
            acc_ref[...] += part

        @pl.when(m == nm - 1)
        def _():
            o_ref[...] = (acc_ref[...] + part).astype(o_ref.dtype)

    return pl.pallas_call(
        body, name=name,
        out_shape=jax.ShapeDtypeStruct((K, N), BF16),
        grid=(K // tk, N // tn, nm),
        in_specs=[pl.BlockSpec((tm, tk), lambda i, j, m: (m % nmi, (m // nmi) * nki + i)),
                  pl.BlockSpec((tm, tn), lambda i, j, m: (m % nmi, (m // nmi) * nnj + j))],
        out_specs=pl.BlockSpec((tk, tn), lambda i, j, m: (i, j)),
        scratch_shapes=[pltpu.VMEM((tk, tn), F32)],
        compiler_params=_cparams("parallel", "parallel", "arbitrary"),
    )(a, b)


def _rmsnorm_fwd(h, g, *, name, dep=None, views=()):
    S, D = h.shape
    tm = _tile(S, (512, 256))
    nv = len(views)

    def body(h_ref, g_ref, *rest):
        n_out = 1 + nv
        outs = rest[len(rest) - n_out - (1 if nv else 0):len(rest) - (1 if nv else 0)]
        x = h_ref[...]
        r = lax.rsqrt(jnp.mean(x * x, axis=-1, keepdims=True) + RMS_EPS)
        y = x * r * g_ref[...]
        outs[0][...] = y.astype(BF16)
        if nv:
            scr = rest[-1]
            _to_chunks(scr, y)
            for v_ref, d in zip(outs[1:], views):
                _slabs_from_chunks(v_ref, scr, d, BF16)

    res = pl.pallas_call(
        body, name=name,
        out_shape=(jax.ShapeDtypeStruct((S, D), BF16),)
        + tuple(jax.ShapeDtypeStruct((S // d, d * D), BF16) for d in views),
        grid=(S // tm,),
        in_specs=[pl.BlockSpec((tm, D), lambda i: (i, 0)), pl.BlockSpec((1, D), lambda i: (0, 0))]
        + ([ANY] if dep is not None else []),
        out_specs=(pl.BlockSpec((tm, D), lambda i: (i, 0)),)
        + tuple(pl.BlockSpec((tm // d, d * D), lambda i: (i, 0)) for d in views),
        scratch_shapes=[_chunk_scratch(tm, D)] if nv else [],
        compiler_params=_cparams("parallel"),
    )(h, g, *([dep] if dep is not None else []))
    return res if nv else res[0]


def _rms_bwd_rows(x, g, dy):
    r = lax.rsqrt(jnp.mean(x * x, axis=-1, keepdims=True) + RMS_EPS)
    xh = x * r
    gy = dy * g
    dx = r * (gy - xh * jnp.mean(xh * gy, axis=-1, keepdims=True))
    return dx, dy * xh


def _rms_bwd(x, g, dn, dres, *, name, dn_views=()):
    S, D = x.shape
    tm = _tile(S, (256,))
    nv = len(dn_views)

    def body(x_ref, g_ref, dn_ref, dr_ref, *rest):
        v_refs = rest[:nv]
        dx_ref, dxb_ref, dg_ref = rest[nv:nv + 3]
        scr = rest[nv + 3:]
        i = pl.program_id(0)
        dn = dn_ref[...]
        for v_ref, s_ref, (_, d) in zip(v_refs, scr, dn_views):
            _chunks_from_slabs(s_ref, v_ref, d)
            dn = dn + _from_chunks(s_ref)
        dx, dgx = _rms_bwd_rows(x_ref[...], g_ref[...], dn)
        tot = dr_ref[...] + dx
        dx_ref[...] = tot
        dxb_ref[...] = tot.astype(BF16)

        @pl.when(i == 0)
        def _():
            dg_ref[...] = jnp.zeros_like(dg_ref)

        dg_ref[0:1, :] += jnp.sum(dgx, axis=0, keepdims=True)

    row = pl.BlockSpec((tm, D), lambda i: (i, 0))
    return pl.pallas_call(
        body, name=name,
        out_shape=(jax.ShapeDtypeStruct((S, D), F32), jax.ShapeDtypeStruct((S, D), BF16),
                   jax.ShapeDtypeStruct((8, D), F32)),
        grid=(S // tm,),
        in_specs=[row, pl.BlockSpec((1, D), lambda i: (0, 0)), row, row]
        + [pl.BlockSpec((tm // d, d * D), lambda i: (i, 0)) for _, d in dn_views],
        out_specs=(row, row, pl.BlockSpec((8, D), lambda i: (0, 0))),
        scratch_shapes=[_chunk_scratch(tm, D)] * nv,
        compiler_params=_cparams("arbitrary"),
    )(x, g, dn, dres, *[a for a, _ in dn_views])


def _mm_nt_rms_bwd(a, b, x, g, dres, *, name, dep=None):
    M, N = a.shape
    b, (D, _), b_spec = _weight(b)
    tm = _tile(M, (1024, 512, 256))
    tk = _tile(N, (1024, 512, 256, 128))
    nk = N // tk

    def body(a_ref, b_ref, x_ref, g_ref, dr_ref, *rest):
        dx_ref, dxb_ref, dg_ref, acc_ref = rest[-4:]
        i = pl.program_id(0)
        k = pl.program_id(1)
        part = _dot_nt(a_ref[...], b_ref[...])

        @pl.when((i == 0) & (k == 0))
        def _():
            dg_ref[...] = jnp.zeros_like(dg_ref)

        @pl.when(k == 0)
        def _():
            acc_ref[...] = part

        @pl.when((k > 0) & (k < nk - 1))
        def _():
            acc_ref[...] += part

        @pl.when(k == nk - 1)
        def _():
            dn = part if nk == 1 else acc_ref[...] + part
            dx, dgx = _rms_bwd_rows(x_ref[...], g_ref[...], dn)
            tot = dr_ref[...] + dx
            dx_ref[...] = tot
            dxb_ref[...] = tot.astype(BF16)
            dg_ref[0:1, :] += jnp.sum(dgx, axis=0, keepdims=True)

    row = pl.BlockSpec((tm, D), lambda i, k: (i, 0))
    in_specs = [pl.BlockSpec((tm, tk), lambda i, k: (i, k)), b_spec((D, tk), lambda i, k: (0, k)),
                row, pl.BlockSpec((1, D), lambda i, k: (0, 0)), row]
    args = [a, b, x, g, dres]
    if dep is not None:
        in_specs.append(ANY)
        args.append(dep)
    return pl.pallas_call(
        body, name=name,
        out_shape=(jax.ShapeDtypeStruct((M, D), F32), jax.ShapeDtypeStruct((M, D), BF16),
                   jax.ShapeDtypeStruct((8, D), F32)),
        grid=(M // tm, nk),
        in_specs=in_specs,
        out_specs=(row, row, pl.BlockSpec((8, D), lambda i, k: (0, 0))),
        scratch_shapes=[pltpu.VMEM((tm, D), F32)],
        compiler_params=_cparams("arbitrary", "arbitrary"),
    )(*args)


def _loss_bwd(h, target, g, *, name):
    S, D = h.shape
    tm = _tile(S, (256,))

    def body(h_ref, t_ref, g_ref, dx_ref, dxb_ref, acc_ref):
        i = pl.program_id(0)
        x = h_ref[...]
        gv = g_ref[...]
        r = lax.rsqrt(jnp.mean(x * x, axis=-1, keepdims=True) + RMS_EPS)
        err = x * r * gv - t_ref[...]
        dx, dgx = _rms_bwd_rows(x, gv, err * (1.0 / D))
        dx_ref[...] = dx
        dxb_ref[...] = dx.astype(BF16)

        @pl.when(i == 0)
        def _():
            acc_ref[...] = jnp.zeros_like(acc_ref)

        acc_ref[0:1, :] += jnp.sum(dgx, axis=0, keepdims=True)
        acc_ref[1:2, :] += jnp.sum(err * err, axis=0, keepdims=True)

    row = pl.BlockSpec((tm, D), lambda i: (i, 0))
    return pl.pallas_call(
        body, name=name,
        out_shape=(jax.ShapeDtypeStruct((S, D), F32), jax.ShapeDtypeStruct((S, D), BF16),
                   jax.ShapeDtypeStruct((8, D), F32)),
        grid=(S // tm,),
        in_specs=[row, row, pl.BlockSpec((1, D), lambda i: (0, 0))],
        out_specs=(row, row, pl.BlockSpec((8, D), lambda i: (0, 0))),
        compiler_params=_cparams("arbitrary"),
    )(h, target, g)


SUBLANES = 8
CONV_ROWS = 64


def _build_shifted(ext_ref, rot_ref, ts, shifts=tuple(range(1, SUBLANES))):
    rows = ts + HALO - SUBLANES
    for j in shifts:
        rot_ref[j, 0:rows, :] = ext_ref[j:j + rows, :]


def _shifted(ext_ref, rot_ref, off, r0, nrows, cs):
    q, j = divmod(off, SUBLANES)
    start = SUBLANES * q + r0
    if j == 0:
        return ext_ref[start:start + nrows, cs]
    return rot_ref[j, start:start + nrows, cs]


def _conv_fwd(uc, conv_w, conv_b, ln_g, ln_b, *, name):
    S, C2 = uc.shape
    C = C2 // 2
    ts = _tile(S, (512, 256))
    per = ts // HALO

    def body(cur_ref, halo_ref, w_ref, b_ref, g_ref, beta_ref, ya_ref, h_ref, ct_ref, ext_ref, rot_ref):
        i = pl.program_id(0)
        hh = halo_ref[:, 0:C] * jax.nn.sigmoid(halo_ref[:, C:C2])
        ext_ref[0:HALO, :] = jnp.where(i == 0, 0.0, hh)
        hc = cur_ref[:, 0:C] * jax.nn.sigmoid(cur_ref[:, C:C2])
        ext_ref[HALO:HALO + ts, :] = hc
        h_ref[...] = hc
        _build_shifted(ext_ref, rot_ref, ts)
        for c0 in range(0, C, LANES):
            cs = slice(c0, c0 + LANES)
            for r0 in range(0, ts, CONV_ROWS):
                acc = jnp.zeros((CONV_ROWS, LANES), F32)
                for k in range(CONV_A_WIDTH):
                    acc = acc + w_ref[k:k + 1, cs] * _shifted(ext_ref, rot_ref, k + 2, r0, CONV_ROWS, cs)
                ct_ref[r0:r0 + CONV_ROWS, cs] = acc + b_ref[:, cs]
        ct = ct_ref[...]
        mu = jnp.mean(ct, axis=-1, keepdims=True)
        xc = ct - mu
        var = jnp.mean(xc * xc, axis=-1, keepdims=True)
        l = xc * lax.rsqrt(var + LN_EPS) * g_ref[...] + beta_ref[...]
        ya_ref[...] = (l * jax.nn.sigmoid(l)).astype(ya_ref.dtype)

    vec = pl.BlockSpec((1, C), lambda i: (0, 0))
    row = pl.BlockSpec((ts, C), lambda i: (i, 0))
    return pl.pallas_call(
        body, name=name,
        out_shape=(jax.ShapeDtypeStruct((S, C), BF16), jax.ShapeDtypeStruct((S, C), F32),
                   jax.ShapeDtypeStruct((S, C), F32)),
        grid=(S // ts,),
        in_specs=[pl.BlockSpec((ts, C2), lambda i: (i, 0)),
                  pl.BlockSpec((HALO, C2), lambda i: (jnp.maximum(i * per - 1, 0), 0)),
                  pl.BlockSpec((CONV_A_WIDTH, C), lambda i: (0, 0)), vec, vec, vec],
        out_specs=(row, row, row),
        scratch_shapes=[pltpu.VMEM((HALO + ts, C), F32), pltpu.VMEM((8, HALO + ts, C), F32)],
        compiler_params=_cparams("parallel"),
    )(uc, uc, conv_w, conv_b, ln_g, ln_b)


CONV_ACC_ROWS = 40


def _conv_bwd_ln(ct, dcat, hglu, ln_g, ln_b, *, name):
    S, C = ct.shape
    CW = dcat.shape[1]
    ts = _tile(S, (256,))
    per = ts // HALO

    def body(ct_ref, dcat_ref, hc_ref, hh_ref, g_ref, beta_ref, dc_ref, acc_ref, ext_ref, rot_ref):
        i = pl.program_id(0)
        ct = ct_ref[...]
        gv = g_ref[...]
        mu = jnp.mean(ct, axis=-1, keepdims=True)
        xc = ct - mu
        rstd = lax.rsqrt(jnp.mean(xc * xc, axis=-1, keepdims=True) + LN_EPS)
        xh = xc * rstd
        l = xh * gv + beta_ref[...]
        sg = jax.nn.sigmoid(l)
        dl = dcat_ref[:, 0:C] * (sg * (1.0 + l * (1.0 - sg)))
        dxh = dl * gv
        dc = rstd * (dxh - jnp.mean(dxh, axis=-1, keepdims=True)
                     - xh * jnp.mean(dxh * xh, axis=-1, keepdims=True))
        dc_ref[...] = dc

        @pl.when(i == 0)
        def _():
            acc_ref[...] = jnp.zeros_like(acc_ref)

        acc_ref[32:33, :] += jnp.sum(dc, axis=0, keepdims=True)
        acc_ref[33:34, :] += jnp.sum(dl * xh, axis=0, keepdims=True)
        acc_ref[34:35, :] += jnp.sum(dl, axis=0, keepdims=True)
        ext_ref[0:HALO, :] = jnp.where(i == 0, 0.0, hh_ref[...])
        ext_ref[HALO:HALO + ts, :] = hc_ref[...]
        _build_shifted(ext_ref, rot_ref, ts)
        for c0 in range(0, C, LANES):
            cs = slice(c0, c0 + LANES)
            dcc = dc_ref[:, cs]
            for k in range(CONV_A_WIDTH):
                acc_ref[k:k + 1, cs] += jnp.sum(dcc * _shifted(ext_ref, rot_ref, k + 2, 0, ts, cs),
                                                axis=0, keepdims=True)

    vec = pl.BlockSpec((1, C), lambda i: (0, 0))
    row = pl.BlockSpec((ts, C), lambda i: (i, 0))
    return pl.pallas_call(
        body, name=name,
        out_shape=(jax.ShapeDtypeStruct((S, C), F32), jax.ShapeDtypeStruct((CONV_ACC_ROWS, C), F32)),
        grid=(S // ts,),
        in_specs=[row, pl.BlockSpec((ts, CW), lambda i: (i, 0)), row,
                  pl.BlockSpec((HALO, C), lambda i: (jnp.maximum(i * per - 1, 0), 0)), vec, vec],
        out_specs=(row, pl.BlockSpec((CONV_ACC_ROWS, C), lambda i: (0, 0))),
        scratch_shapes=[pltpu.VMEM((HALO + ts, C), F32), pltpu.VMEM((8, HALO + ts, C), F32)],
        compiler_params=_cparams("arbitrary"),
    )(ct, dcat, hglu, hglu, ln_g, ln_b)


def _conv_bwd_in(dc, uc, conv_w, *, name):
    S, C = dc.shape
    C2 = 2 * C
    ts = _tile(S, (256,))
    per = ts // HALO
    nt = S // ts

    def body(dc_ref, dn_ref, uc_ref, w_ref, du_ref, ext_ref, rot_ref):
        i = pl.program_id(0)
        ext_ref[0:ts, :] = dc_ref[...]
        ext_ref[ts:ts + HALO, :] = jnp.where(i == nt - 1, 0.0, dn_ref[...])
        _build_shifted(ext_ref, rot_ref, ts)
        for c0 in range(0, C, LANES):
            cs = slice(c0, c0 + LANES)
            gs = slice(C + c0, C + c0 + LANES)
            for r0 in range(0, ts, CONV_ROWS):
                rs = slice(r0, r0 + CONV_ROWS)
                acc = jnp.zeros((CONV_ROWS, LANES), F32)
                for k in range(CONV_A_WIDTH):
                    acc = acc + w_ref[k:k + 1, cs] * _shifted(ext_ref, rot_ref, 30 - k, r0, CONV_ROWS, cs)
                sg = jax.nn.sigmoid(uc_ref[rs, gs])
                du_ref[rs, cs] = (acc * sg).astype(du_ref.dtype)
                du_ref[rs, gs] = (acc * uc_ref[rs, cs] * sg * (1.0 - sg)).astype(du_ref.dtype)

    return pl.pallas_call(
        body, name=name,
        out_shape=jax.ShapeDtypeStruct((S, C2), BF16),
        grid=(nt,),
        in_specs=[pl.BlockSpec((ts, C), lambda i: (i, 0)),
                  pl.BlockSpec((HALO, C), lambda i: (jnp.minimum((i + 1) * per, S // HALO - 1), 0)),
                  pl.BlockSpec((ts, C2), lambda i: (i, 0)),
                  pl.BlockSpec((CONV_A_WIDTH, C), lambda i: (0, 0))],
        out_specs=pl.BlockSpec((ts, C2), lambda i: (i, 0)),
        scratch_shapes=[pltpu.VMEM((ts + HALO, C), F32), pltpu.VMEM((8, ts + HALO, C), F32)],
        compiler_params=_cparams("parallel"),
    )(dc, dc, uc, conv_w)


SC_SHIFTS_BACK = ((HALO - 2) % SUBLANES, (HALO - 1) % SUBLANES)
SC_SHIFTS_AHEAD = (1, 2)


def _sc_fwd(u3, conv_w, *, name):
    S, W3 = u3.shape
    W = W3 // 3
    ts = _tile(S, (256,))
    per = ts // HALO

    def body(cur_ref, halo_ref, w_ref, y_ref, ext_ref, rot_ref):
        i = pl.program_id(0)
        cvh = halo_ref[:, W:2 * W].astype(F32) * halo_ref[:, 2 * W:W3].astype(F32)
        ext_ref[0:HALO, :] = jnp.where(i == 0, 0.0, cvh)
        ext_ref[HALO:HALO + ts, :] = cur_ref[:, W:2 * W].astype(F32) * cur_ref[:, 2 * W:W3].astype(F32)
        _build_shifted(ext_ref, rot_ref, ts, SC_SHIFTS_BACK)
        for c0 in range(0, W, LANES):
            cs = slice(c0, c0 + LANES)
            for r0 in range(0, ts, CONV_ROWS):
                rs = slice(r0, r0 + CONV_ROWS)
                k = (w_ref[0:1, cs] * _shifted(ext_ref, rot_ref, HALO - 2, r0, CONV_ROWS, cs)
                     + w_ref[1:2, cs] * _shifted(ext_ref, rot_ref, HALO - 1, r0, CONV_ROWS, cs)
                     + w_ref[2:3, cs] * _shifted(ext_ref, rot_ref, HALO, r0, CONV_ROWS, cs))
                y_ref[rs, cs] = (cur_ref[rs, cs].astype(F32) * k).astype(y_ref.dtype)

    return pl.pallas_call(
        body, name=name,
        out_shape=jax.ShapeDtypeStruct((S, W), BF16),
        grid=(S // ts,),
        in_specs=[pl.BlockSpec((ts, W3), lambda i: (i, 0)),
                  pl.BlockSpec((HALO, W3), lambda i: (jnp.maximum(i * per - 1, 0), 0)),
                  pl.BlockSpec((SC_CONV_WIDTH, W), lambda i: (0, 0))],
        out_specs=pl.BlockSpec((ts, W), lambda i: (i, 0)),
        scratch_shapes=[pltpu.VMEM((HALO + ts, W), F32), pltpu.VMEM((8, HALO + ts, W), F32)],
        compiler_params=_cparams("parallel"),
    )(u3, u3, conv_w)


def _sc_bwd(u3, dy, conv_w, *, name):
    S, W3 = u3.shape
    W = W3 // 3
    ts = _tile(S, (256,))
    per = ts // HALO
    nt = S // ts

    def body(cur_ref, prev_ref, next_ref, dy_ref, dyn_ref, w_ref, du_ref, dw_ref, cv_ext, dk_ext, cv_rot, dk_rot):
        i = pl.program_id(0)
        cvh = prev_ref[:, W:2 * W].astype(F32) * prev_ref[:, 2 * W:W3].astype(F32)
        cv_ext[0:HALO, :] = jnp.where(i == 0, 0.0, cvh)
        cv_ext[HALO:HALO + ts, :] = cur_ref[:, W:2 * W].astype(F32) * cur_ref[:, 2 * W:W3].astype(F32)
        dk_ext[0:ts, :] = dy_ref[...] * cur_ref[:, 0:W].astype(F32)
        dk_ext[ts:ts + HALO, :] = jnp.where(i == nt - 1, 0.0, dyn_ref[...] * next_ref[:, 0:W].astype(F32))
        _build_shifted(cv_ext, cv_rot, ts, SC_SHIFTS_BACK)
        _build_shifted(dk_ext, dk_rot, ts, SC_SHIFTS_AHEAD)

        @pl.when(i == 0)
        def _():
            dw_ref[...] = jnp.zeros_like(dw_ref)

        for c0 in range(0, W, LANES):
            cs = slice(c0, c0 + LANES)
            w0, w1, w2 = w_ref[0:1, cs], w_ref[1:2, cs], w_ref[2:3, cs]
            sums = [jnp.zeros((1, LANES), F32)] * SC_CONV_WIDTH
            for r0 in range(0, ts, CONV_ROWS):
                rs = slice(r0, r0 + CONV_ROWS)
                cv2 = _shifted(cv_ext, cv_rot, HALO - 2, r0, CONV_ROWS, cs)
                cv1 = _shifted(cv_ext, cv_rot, HALO - 1, r0, CONV_ROWS, cs)
                cv0 = _shifted(cv_ext, cv_rot, HALO, r0, CONV_ROWS, cs)
                dk = dk_ext[rs, cs]
                dcv = (w2 * dk + w1 * _shifted(dk_ext, dk_rot, 1, r0, CONV_ROWS, cs)
                       + w0 * _shifted(dk_ext, dk_rot, 2, r0, CONV_ROWS, cs))
                du_ref[rs, cs] = (dy_ref[rs, cs] * (w0 * cv2 + w1 * cv1 + w2 * cv0)).astype(du_ref.dtype)
                du_ref[rs, W + c0:W + c0 + LANES] = (
                    dcv * cur_ref[rs, 2 * W + c0:2 * W + c0 + LANES].astype(F32)).astype(du_ref.dtype)
                du_ref[rs, 2 * W + c0:2 * W + c0 + LANES] = (
                    dcv * cur_ref[rs, W + c0:W + c0 + LANES].astype(F32)).astype(du_ref.dtype)
                for t, cvt in enumerate((cv2, cv1, cv0)):
                    sums[t] = sums[t] + jnp.sum(dk * cvt, axis=0, keepdims=True)
            for t in range(SC_CONV_WIDTH):
                dw_ref[t:t + 1, cs] += sums[t]

    nxt = lambda i: (jnp.minimum((i + 1) * per, S // HALO - 1), 0)
    return pl.pallas_call(
        body, name=name,
        out_shape=(jax.ShapeDtypeStruct((S, W3), BF16), jax.ShapeDtypeStruct((8, W), F32)),
        grid=(nt,),
        in_specs=[pl.BlockSpec((ts, W3), lambda i: (i, 0)),
                  pl.BlockSpec((HALO, W3), lambda i: (jnp.maximum(i * per - 1, 0), 0)),
                  pl.BlockSpec((HALO, W3), nxt),
                  pl.BlockSpec((ts, W), lambda i: (i, 0)),
                  pl.BlockSpec((HALO, W), nxt),
                  pl.BlockSpec((SC_CONV_WIDTH, W), lambda i: (0, 0))],
        out_specs=(pl.BlockSpec((ts, W3), lambda i: (i, 0)), pl.BlockSpec((8, W), lambda i: (0, 0))),
        scratch_shapes=[pltpu.VMEM((HALO + ts, W), F32), pltpu.VMEM((ts + HALO, W), F32),
                        pltpu.VMEM((8, HALO + ts, W), F32), pltpu.VMEM((8, ts + HALO, W), F32)],
        compiler_params=_cparams("arbitrary"),
    )(u3, u3, u3, dy, dy, conv_w)


def _t5_causal_bucket(n):
    max_exact = NUM_BUCKETS // 2
    nf = jnp.maximum(n, 1).astype(F32)
    large = max_exact + (jnp.log(nf / max_exact) / math.log(REL_MAX_DISTANCE / max_exact)
                         * (NUM_BUCKETS - max_exact)).astype(jnp.int32)
    return jnp.where(n < max_exact, n, jnp.minimum(large, NUM_BUCKETS - 1))


def _bucket_tables():
    steps = ATTN_BLK
    m = jnp.arange(steps)[:, None] + steps - jnp.arange(2 * steps)[None, :]
    return jnp.stack([_t5_causal_bucket(jnp.clip(m, 0, steps) * dil).astype(F32) for _, dil in DILATED_GROUPS])


def _bias_tables(rel_bias, buckets, *, name):
    steps = ATTN_BLK

    def body(tab_ref, bk_ref, o_ref):
        g = pl.program_id(0)
        bk = bk_ref[0]
        a_idx = lax.broadcasted_iota(jnp.int32, (steps, 2 * steps), 0)
        c_idx = lax.broadcasted_iota(jnp.int32, (steps, 2 * steps), 1)
        m = a_idx + steps - c_idx
        band = (m >= 0) & (m <= steps)
        band_first = band & (c_idx >= steps)
        for h in range(HEADS_PER_GROUP):
            bias = jnp.zeros((steps, 2 * steps), F32)
            for b in range(NUM_BUCKETS):
                bias = jnp.where(bk == float(b), tab_ref[b, g * HEADS_PER_GROUP + h], bias)
            o_ref[0, 0, h] = jnp.where(band_first, bias, NEG_INF)
            o_ref[0, 1, h] = jnp.where(band, bias, NEG_INF)

    return pl.pallas_call(
        body, name=name,
        out_shape=jax.ShapeDtypeStruct((N_GROUPS, 2, HEADS_PER_GROUP, steps, 2 * steps), F32),
        grid=(N_GROUPS,),
        in_specs=[pl.BlockSpec(memory_space=pltpu.SMEM),
                  pl.BlockSpec((1, steps, 2 * steps), lambda g: (g, 0, 0))],
        out_specs=pl.BlockSpec((1, 2, HEADS_PER_GROUP, steps, 2 * steps), lambda g: (g, 0, 0, 0, 0)),
        compiler_params=_cparams("parallel"),
    )(rel_bias, buckets)


def _lane_is_low():
    return lax.broadcasted_iota(jnp.int32, (1, LANES), 1) < HEAD_DIM


def _stack_heads(x2, low):
    zero = jnp.zeros_like(x2)
    return jnp.concatenate([jnp.where(low, x2, zero), jnp.where(low, zero, x2)], axis=0)


def _qkv_specs(nb):
    nqb = GROUP_QKV // ATTN_OUT

    def spec(t, prev):
        def idx(r, n):
            nn = jnp.minimum(n, nb - 1)
            row = jnp.maximum(nn - 1, 0) if prev else nn
            return (row, r * nqb + t)
        return pl.BlockSpec((ATTN_BLK, ATTN_OUT), idx)

    return [spec(0, False), spec(1, False), spec(1, True), spec(2, False), spec(2, True)]


def _attn_fwd(uv, bias, g, d, *, name):
    rows = uv.shape[0]

    def body(q_ref, kc_ref, kp_ref, vc_ref, vp_ref, bias_ref, o_ref, l_ref):
        n = pl.program_id(1)
        sel = jnp.minimum(n, 1)
        low = _lane_is_low()
        slabs = [slice(hp * LANES, (hp + 1) * LANES) for hp in range(HEADS_PER_GROUP // 2)]
        scores = [_dot_nt(_stack_heads(q_ref[:, sl] * (HEAD_DIM ** -0.5), low),
                          jnp.concatenate([kp_ref[:, sl], kc_ref[:, sl]], axis=0)) for sl in slabs]
        probs, dens_all, lses_all = [], [], []
        for hp, s in enumerate(scores):
            ps, dens, lses = [], [], []
            for hh in range(2):
                logits = s[hh * ATTN_BLK:(hh + 1) * ATTN_BLK] + bias_ref[sel, 2 * hp + hh]
                mx = jnp.max(logits, axis=-1, keepdims=True)
                p = jnp.exp(logits - mx)
                den = jnp.sum(p, axis=-1, keepdims=True)
                ps.append(p.astype(BF16))
                dens.append(den)
                lses.append(jnp.broadcast_to(mx + jnp.log(den), (ATTN_BLK, LANES)))
            probs.append(jnp.concatenate(ps, axis=0))
            dens_all.append(dens)
            lses_all.append(lses)
        for hp, sl in enumerate(slabs):
            v2 = jnp.concatenate([vp_ref[:, sl], vc_ref[:, sl]], axis=0)
            pv = jnp.dot(probs[hp], v2, preferred_element_type=F32)
            dens, lses = dens_all[hp], lses_all[hp]
            o_ref[:, sl] = jnp.where(low, pv[0:ATTN_BLK] / dens[0], pv[ATTN_BLK:2 * ATTN_BLK] / dens[1])
            l_ref[:, sl] = jnp.where(low, lses[0], lses[1])

    out_spec = pl.BlockSpec((ATTN_BLK, ATTN_OUT), lambda r, n: (n, r))
    return pl.pallas_call(
        body, name=name,
        out_shape=(jax.ShapeDtypeStruct((rows, d * ATTN_OUT), F32),) * 2,
        grid=(d, rows // ATTN_BLK),
        in_specs=_qkv_specs(rows // ATTN_BLK) + [pl.BlockSpec((None, 2, HEADS_PER_GROUP, ATTN_BLK, 2 * ATTN_BLK),
                                                              lambda r, n: (g, 0, 0, 0, 0))],
        out_specs=(out_spec, out_spec),
        compiler_params=_cparams("parallel", "parallel"),
    )(uv, uv, uv, uv, uv, bias)


def _chunk_scratch(n, width):
    return pltpu.VMEM((width // LANES, n, LANES), F32)


def _to_chunks(scr, val):
    for c in range(scr.shape[0]):
        scr[c] = val[:, c * LANES:(c + 1) * LANES]


def _from_chunks(scr):
    return jnp.concatenate([scr[c] for c in range(scr.shape[0])], axis=1)


def _slabs_from_chunks(dst_ref, scr, d, dtype):
    nc, n, _ = scr.shape
    for r in range(d):
        for c in range(nc):
            col = r * nc * LANES + c * LANES
            dst_ref[:, col:col + LANES] = scr[c, pl.ds(r, n // d, stride=d), :].astype(dtype)


def _chunks_from_slabs(scr, src_ref, d):
    nc, n, _ = scr.shape
    for r in range(d):
        for c in range(nc):
            col = r * nc * LANES + c * LANES
            scr[c, pl.ds(r, n // d, stride=d), :] = src_ref[:, col:col + LANES]


def _attn_merge(outs, lses, ya, *, name):
    S, C = ya.shape
    tm = _tile(S, (256,))
    dils = [dil for _, dil in DILATED_GROUPS]

    def body(o0, o1, o2, l0, l1, l2, ya_ref, cat_ref, out_ref, lse_ref, so1, so2, sl1, sl2):
        _chunks_from_slabs(so1, o1, dils[1])
        _chunks_from_slabs(so2, o2, dils[2])
        _chunks_from_slabs(sl1, l1, dils[1])
        _chunks_from_slabs(sl2, l2, dils[2])
        a0, a1, a2 = l0[...], _from_chunks(sl1), _from_chunks(sl2)
        m = jnp.maximum(jnp.maximum(a0, a1), a2)
        e0, e1, e2 = jnp.exp(a0 - m), jnp.exp(a1 - m), jnp.exp(a2 - m)
        den = e0 + e1 + e2
        out = (e0 * o0[...] + e1 * _from_chunks(so1) + e2 * _from_chunks(so2)) / den
        out_ref[...] = out
        lse_ref[...] = m + jnp.log(den)
        cat_ref[:, 0:C] = ya_ref[...]
        cat_ref[:, C:C + ATTN_OUT] = out.astype(cat_ref.dtype)

    blk = pl.BlockSpec((tm, ATTN_OUT), lambda i: (i, 0))
    vblk = [pl.BlockSpec((tm // d, d * ATTN_OUT), lambda i: (i, 0)) for d in dils]
    assert dils[0] == 1
    return pl.pallas_call(
        body, name=name,
        out_shape=(jax.ShapeDtypeStruct((S, C + ATTN_OUT), BF16), jax.ShapeDtypeStruct((S, ATTN_OUT), F32),
                   jax.ShapeDtypeStruct((S, ATTN_OUT), F32)),
        grid=(S // tm,),
        in_specs=vblk + vblk + [pl.BlockSpec((tm, C), lambda i: (i, 0))],
        out_specs=(pl.BlockSpec((tm, C + ATTN_OUT), lambda i: (i, 0)), blk, blk),
        scratch_shapes=[_chunk_scratch(tm, ATTN_OUT)] * 4,
        compiler_params=_cparams("parallel"),
    )(*outs, *lses, ya)


def _attn_prep(dcat, outf, lse, *, name):
    S, CW = dcat.shape
    C = CW - ATTN_OUT
    tm = _tile(S, (256,))
    dils = [dil for _, dil in DILATED_GROUPS]
    assert dils[0] == 1
    ones = np.kron(np.eye(HEADS_PER_GROUP, dtype=np.float32), np.ones((HEAD_DIM, HEAD_DIM), np.float32))

    nviews = 3 * (len(dils) - 1)

    def body(dcat_ref, out_ref, l_ref, ones_ref, dyb_ref, dl_ref, *rest):
        views = rest[:nviews]
        s_dyb, s_dl, s_l = rest[nviews:]
        dyb = dcat_ref[:, C:CW]
        dyb_ref[...] = dyb.astype(BF16)
        prod = dyb * out_ref[...]
        ov = ones_ref[...]
        hi, mid, lo = _split_bf16(prod)
        delta = (jnp.dot(hi, ov, preferred_element_type=F32)
                 + jnp.dot(mid, ov, preferred_element_type=F32)
                 + jnp.dot(lo, ov, preferred_element_type=F32))
        dl_ref[...] = delta
        _to_chunks(s_dyb, dyb)
        _to_chunks(s_dl, delta)
        _to_chunks(s_l, l_ref[...])
        for gi, d in enumerate(dils[1:]):
            dyb_v, dl_v, l_v = views[3 * gi:3 * gi + 3]
            _slabs_from_chunks(dyb_v, s_dyb, d, BF16)
            _slabs_from_chunks(dl_v, s_dl, d, F32)
            _slabs_from_chunks(l_v, s_l, d, F32)

    blk = pl.BlockSpec((tm, ATTN_OUT), lambda i: (i, 0))
    view_shapes, view_specs = [], []
    for d in dils[1:]:
        for dt in (BF16, F32, F32):
            view_shapes.append(jax.ShapeDtypeStruct((S // d, d * ATTN_OUT), dt))
            view_specs.append(pl.BlockSpec((tm // d, d * ATTN_OUT), lambda i: (i, 0)))
    res = pl.pallas_call(
        body, name=name,
        out_shape=(jax.ShapeDtypeStruct((S, ATTN_OUT), BF16), jax.ShapeDtypeStruct((S, ATTN_OUT), F32),
                   *view_shapes),
        grid=(S // tm,),
        in_specs=[pl.BlockSpec((tm, CW), lambda i: (i, 0)), blk, blk,
                  pl.BlockSpec((ATTN_OUT, ATTN_OUT), lambda i: (0, 0))],
        out_specs=(blk, blk, *view_specs),
        scratch_shapes=[_chunk_scratch(tm, ATTN_OUT)] * 3,
        compiler_params=_cparams("parallel"),
    )(dcat, outf, lse, jnp.asarray(ones, BF16))
    return [(res[0], res[1], lse)] + [tuple(res[2 + 3 * gi:5 + 3 * gi]) for gi in range(len(dils) - 1)]


def _attn_bwd(uv, dov, lv, dv_, bias, g, d, *, name):
    rows = uv.shape[0]
    nb = rows // ATTN_BLK
    steps = d * nb
    scale = HEAD_DIM ** -0.5

    def body(q_ref, kc_ref, kp_ref, vc_ref, vp_ref, do_ref, l_ref, dl_ref, bias_ref,
             out_ref, db_ref, dq_s, dk_s, dv_s):
        t = pl.program_id(0)
        n = t % nb
        low = _lane_is_low()

        @pl.when(t == 0)
        def _():
            db_ref[...] = jnp.zeros_like(db_ref)
            dq_s[...] = jnp.zeros_like(dq_s)
            dk_s[...] = jnp.zeros_like(dk_s)
            dv_s[...] = jnp.zeros_like(dv_s)

        @pl.when(t < steps)
        def _():
            sel = jnp.minimum(n, 1)
            lane = lax.broadcasted_iota(jnp.int32, (1, LANES), 1)
            slabs = [slice(hp * LANES, (hp + 1) * LANES) for hp in range(HEADS_PER_GROUP // 2)]
            keys = [jnp.concatenate([kp_ref[:, sl], kc_ref[:, sl]], axis=0) for sl in slabs]
            scores = [_dot_nt(_stack_heads(q_ref[:, sl] * scale, low), keys[hp]) for hp, sl in enumerate(slabs)]
            dps = [_dot_nt(_stack_heads(do_ref[:, sl], low),
                           jnp.concatenate([vp_ref[:, sl], vc_ref[:, sl]], axis=0)) for sl in slabs]
            stacked = []
            for hp, sl in enumerate(slabs):
                lse2 = l_ref[:, sl]
                dl2 = dl_ref[:, sl]
                pbs, dsbs = [], []
                for hh in range(2):
                    rows = slice(hh * ATTN_BLK, (hh + 1) * ATTN_BLK)
                    one = lane == hh * HEAD_DIM
                    lse_col = jnp.sum(jnp.where(one, lse2, 0.0), axis=-1, keepdims=True)
                    dl_col = jnp.sum(jnp.where(one, dl2, 0.0), axis=-1, keepdims=True)
                    p = jnp.exp(scores[hp][rows] + bias_ref[sel, 2 * hp + hh] - lse_col)
                    ds = p * (dps[hp][rows] - dl_col)
                    db_ref[2 * hp + hh] += ds
                    pbs.append(p.astype(BF16))
                    dsbs.append((ds * scale).astype(BF16))
                stacked.append((jnp.concatenate(dsbs, axis=0), jnp.concatenate(dsbs, axis=1),
                                jnp.concatenate(pbs, axis=1)))
            nk2 = 2 * ATTN_BLK
            for hp, sl in enumerate(slabs):
                ds_rows, ds_cols, p_cols = stacked[hp]
                dq = jnp.dot(ds_rows, keys[hp], preferred_element_type=F32)
                dk = _dot_tn(ds_cols, q_ref[:, sl])
                dv = _dot_tn(p_cols, do_ref[:, sl])
                dq2 = jnp.where(low, dq[0:ATTN_BLK], dq[ATTN_BLK:nk2])
                dk2 = jnp.where(low, dk[0:nk2], dk[nk2:2 * nk2])
                dv2 = jnp.where(low, dv[0:nk2], dv[nk2:2 * nk2])
                out_ref[:, sl] = dq_s[:, sl].astype(out_ref.dtype)
                dq_s[:, sl] = dq2
                ksl = slice(ATTN_OUT + hp * LANES, ATTN_OUT + (hp + 1) * LANES)
                vsl = slice(2 * ATTN_OUT + hp * LANES, 2 * ATTN_OUT + (hp + 1) * LANES)
                out_ref[:, ksl] = (dk_s[:, sl] + dk2[0:ATTN_BLK]).astype(out_ref.dtype)
                dk_s[:, sl] = dk2[ATTN_BLK:2 * ATTN_BLK]
                out_ref[:, vsl] = (dv_s[:, sl] + dv2[0:ATTN_BLK]).astype(out_ref.dtype)
                dv_s[:, sl] = dv2[ATTN_BLK:2 * ATTN_BLK]

        @pl.when(t == steps)
        def _():
            out_ref[:, 0:ATTN_OUT] = dq_s[...].astype(out_ref.dtype)
            out_ref[:, ATTN_OUT:2 * ATTN_OUT] = dk_s[...].astype(out_ref.dtype)
            out_ref[:, 2 * ATTN_OUT:GROUP_QKV] = dv_s[...].astype(out_ref.dtype)

    rowblk = pl.BlockSpec((ATTN_BLK, ATTN_OUT), lambda r, n: (jnp.minimum(n, nb - 1), r))

    def at_step(spec):
        def index_map(t):
            tt = jnp.minimum(t, steps - 1)
            return spec.index_map(tt // nb, tt % nb)
        return pl.BlockSpec(spec.block_shape, index_map)

    return pl.pallas_call(
        body, name=name,
        out_shape=(jax.ShapeDtypeStruct((rows, d * GROUP_QKV), BF16),
                   jax.ShapeDtypeStruct((HEADS_PER_GROUP, ATTN_BLK, 2 * ATTN_BLK), F32)),
        grid=(steps + 1,),
        in_specs=[at_step(sp) for sp in _qkv_specs(nb) + [rowblk, rowblk, rowblk]] + [
            pl.BlockSpec((None, 2, HEADS_PER_GROUP, ATTN_BLK, 2 * ATTN_BLK), lambda t: (g, 0, 0, 0, 0))],
        out_specs=(pl.BlockSpec((ATTN_BLK, GROUP_QKV),
                                lambda t: (jnp.maximum(t - 1, 0) % nb, jnp.maximum(t - 1, 0) // nb)),
                   pl.BlockSpec((HEADS_PER_GROUP, ATTN_BLK, 2 * ATTN_BLK), lambda t: (0, 0, 0))),
        scratch_shapes=[pltpu.VMEM((ATTN_BLK, ATTN_OUT), F32)] * 3,
        compiler_params=_cparams("arbitrary"),
    )(uv, uv, uv, uv, uv, dov, lv, dv_, bias)


def _split_bf16(x):
    hi = x.astype(BF16)
    r1 = x - hi.astype(F32)
    mid = r1.astype(BF16)
    lo = (r1 - mid.astype(F32)).astype(BF16)
    return hi, mid, lo


RELBIAS_CHUNK = 4096


def _relbias_reduce(dbs, buckets, *, name):
    flat = ATTN_BLK * 2 * ATTN_BLK
    dbf = jnp.stack([db.reshape(HEADS_PER_GROUP, flat) for db in dbs])
    bkf = buckets.reshape(N_GROUPS, 1, flat)

    def body(db_ref, bk_ref, o_ref):
        c = pl.program_id(1)
        rows = lax.broadcasted_iota(jnp.int32, (LANES, RELBIAS_CHUNK), 0).astype(F32)
        onehot = jnp.where(rows == bk_ref[0], 1.0, 0.0).astype(BF16)
        hi, mid, lo = _split_bf16(db_ref[0])
        part = _dot_nt(hi, onehot) + _dot_nt(mid, onehot) + _dot_nt(lo, onehot)

        @pl.when(c == 0)
        def _():
            o_ref[0] = part

        @pl.when(c > 0)
        def _():
            o_ref[0] += part

    return pl.pallas_call(
        body, name=name,
        out_shape=jax.ShapeDtypeStruct((N_GROUPS, HEADS_PER_GROUP, LANES), F32),
        grid=(N_GROUPS, flat // RELBIAS_CHUNK),
        in_specs=[pl.BlockSpec((1, HEADS_PER_GROUP, RELBIAS_CHUNK), lambda g, c: (g, 0, c)),
                  pl.BlockSpec((1, 1, RELBIAS_CHUNK), lambda g, c: (g, 0, c))],
        out_specs=pl.BlockSpec((1, HEADS_PER_GROUP, LANES), lambda g, c: (g, 0, 0)),
        compiler_params=_cparams("parallel", "arbitrary"),
    )(dbf, bkf)


def _my_position():
    x, y, c = lax.axis_index("x"), lax.axis_index("y"), lax.axis_index("c")
    return x, y, c


def _linear(pos):
    return 4 * pos[0] + 2 * pos[1] + pos[2]


def _peer(pos, k):
    x, y, c = pos
    return ((1 - x) if k & 4 else x, (1 - y) if k & 2 else y, (1 - c) if k & 1 else c)


HBM_SPEC = pl.BlockSpec(memory_space=pltpu.HBM)
SEM_SPEC = pl.BlockSpec(memory_space=pltpu.SEMAPHORE)
DATAFLOW = pltpu.SideEffectType.DATAFLOW_SIDE_EFFECTING


def _exchange_copies(src, land, sems, send_window, recv_window, with_arrivals):
    send_sems, recv_sems, local_sems = sems
    T = len(src)
    me = _my_position()
    me_lin = _linear(me)
    local = [pltpu.make_async_copy(send_window(t, src[t], me_lin), recv_window(t, land[t], me_lin),
                                   local_sems.at[t]) for t in range(T)]
    sends, arrivals = [], []
    for t in range(T):
        for k in range(1, N_DEV):
            peer = _peer(me, k)
            peer_lin = _linear(peer)
            sem = t * (N_DEV - 1) + k - 1
            sends.append(pltpu.make_async_remote_copy(
                src_ref=send_window(t, src[t], peer_lin), dst_ref=recv_window(t, land[t], me_lin),
                send_sem=send_sems.at[sem], recv_sem=recv_sems.at[sem],
                device_id=peer, device_id_type=MESH))
            if with_arrivals:
                arrivals.append(pltpu.make_async_remote_copy(
                    src_ref=send_window(t, src[t], me_lin), dst_ref=recv_window(t, land[t], peer_lin),
                    send_sem=send_sems.at[sem], recv_sem=recv_sems.at[sem],
                    device_id=peer, device_id_type=MESH))
    return local, sends, arrivals


def _exchange_start(srcs, land_shapes, send_window, recv_window, *, name, dep=None):
    T = len(srcs)
    n_in = 2 * T + (1 if dep is not None else 0)

    def body(*refs):
        src = refs[:T]
        land = refs[T:2 * T]
        sems = refs[n_in:n_in + 3]
        token = refs[-1]
        local, sends, _ = _exchange_copies(src, land, sems, send_window, recv_window, False)
        for cp in local + sends:
            cp.start()
        token[...] = jnp.zeros_like(token)

    lands = [lax.empty(ls.shape, ls.dtype) for ls in land_shapes]
    operands = [pltpu.with_memory_space_constraint(a, pltpu.HBM) for a in list(srcs) + lands]
    outs = pl.pallas_call(
        body, name=name,
        out_shape=(pltpu.SemaphoreType.DMA((T * (N_DEV - 1),)), pltpu.SemaphoreType.DMA((T * (N_DEV - 1),)),
                   pltpu.SemaphoreType.DMA((T,)),
                   *[pltpu.HBM(a.shape, a.dtype) for a in operands],
                   jax.ShapeDtypeStruct((8, LANES), F32)),
        in_specs=[HBM_SPEC] * (2 * T) + ([ANY] if dep is not None else []),
        out_specs=(SEM_SPEC,) * 3 + (HBM_SPEC,) * (2 * T) + (VMEM_SPEC,),
        input_output_aliases={i: 3 + i for i in range(2 * T)},
        compiler_params=pltpu.CompilerParams(has_side_effects=DATAFLOW),
    )(*operands, *([dep] if dep is not None else []))
    return outs[:3], outs[3:3 + T], outs[3 + T:3 + 2 * T], outs[-1]


def _exchange_wait(started, after, send_window, recv_window, *, name):
    sems, srcs, lands, _ = started
    T = len(srcs)

    def body(*refs):
        src = refs[:T]
        land = refs[T:2 * T]
        sem_refs = refs[2 * T:2 * T + 3]
        local, sends, arrivals = _exchange_copies(src, land, sem_refs, send_window, recv_window, True)
        for cp in arrivals:
            cp.wait_recv()
        for cp in sends:
            cp.wait_send()
        for cp in local:
            cp.wait()

    outs = pl.pallas_call(
        body, name=name,
        out_shape=tuple(pltpu.HBM(a.shape, a.dtype) for a in list(srcs) + list(lands)),
        in_specs=[HBM_SPEC] * (2 * T) + [SEM_SPEC] * 3 + [ANY],
        out_specs=(HBM_SPEC,) * (2 * T),
        input_output_aliases={i: i for i in range(2 * T)},
        compiler_params=pltpu.CompilerParams(has_side_effects=DATAFLOW),
    )(*srcs, *lands, *sems, after)
    return outs[T:]


def _shard_window(kind, width):
    def win(ref, lin):
        if kind == "slot":
            return ref.at[lin]
        if kind == "col":
            return ref.at[:, pl.ds(pl.multiple_of(lin * width, LANES), width)]
        if kind == "row":
            return ref.at[pl.ds(pl.multiple_of(lin * width, 8), width), :]
        if kind == "lcol":
            return ref.at[:, :, pl.ds(pl.multiple_of(lin * width, LANES), width)]
        if kind == "lrow":
            return ref.at[:, pl.ds(pl.multiple_of(lin * width, 8), width), :]
        raise ValueError(kind)
    return win


def _shard_windows(kinds, shard_shapes):
    return [_shard_window(k, (ss[-1] if k in ("col", "lcol") else ss[-2])) for k, ss in zip(kinds, shard_shapes)]


def _allgather_start(shards, kinds, full_shapes, *, name, dep=None):
    wins = _shard_windows(kinds, [s.shape for s in shards])
    send_window = lambda t, ref, lin: ref
    recv_window = lambda t, ref, lin: wins[t](ref, lin)
    started = _exchange_start(shards, [jax.ShapeDtypeStruct(fs, s.dtype) for fs, s in zip(full_shapes, shards)],
                              send_window, recv_window, name=name + "_start", dep=dep)
    return started, lambda after: _exchange_wait(started, after, send_window, recv_window, name=name + "_wait")


def _scatter_start(fulls, kinds, shard_shapes, *, name):
    wins = _shard_windows(kinds, shard_shapes)
    send_window = lambda t, ref, lin: wins[t](ref, lin)
    recv_window = lambda t, ref, lin: ref.at[lin]
    started = _exchange_start(
        fulls, [jax.ShapeDtypeStruct((N_DEV,) + tuple(ss), f.dtype) for ss, f in zip(shard_shapes, fulls)],
        send_window, recv_window, name=name + "_start")
    return started, lambda after: _exchange_wait(started, after, send_window, recv_window, name=name + "_wait")


def _small_gather(pack, *, reduce, name):
    R = pack.shape[0]

    def body(p_ref, o_ref, *rest):
        if reduce:
            buf, send_sems, recv_sems = rest
        else:
            buf = o_ref
            send_sems, recv_sems = rest
        me = _my_position()
        me_lin = _linear(me)
        buf[me_lin] = p_ref[...]
        sends = []
        for k in range(1, N_DEV):
            peer = _peer(me, k)
            cp = pltpu.make_async_remote_copy(
                src_ref=p_ref, dst_ref=buf.at[me_lin],
                send_sem=send_sems.at[k - 1], recv_sem=recv_sems.at[k - 1],
                device_id=peer, device_id_type=MESH)
            cp.start()
            sends.append(cp)
        for k in range(1, N_DEV):
            peer = _peer(me, k)
            pltpu.make_async_remote_copy(
                src_ref=p_ref, dst_ref=buf.at[_linear(peer)],
                send_sem=send_sems.at[k - 1], recv_sem=recv_sems.at[k - 1],
                device_id=peer, device_id_type=MESH).wait_recv()
        for cp in sends:
            cp.wait_send()
        if reduce:
            acc = buf[0]
            for s in range(1, N_DEV):
                acc = acc + buf[s]
            o_ref[...] = acc

    scratch = [pltpu.SemaphoreType.DMA((N_DEV - 1,)), pltpu.SemaphoreType.DMA((N_DEV - 1,))]
    if reduce:
        scratch = [pltpu.VMEM((N_DEV, R, LANES), F32)] + scratch
        out_shape = jax.ShapeDtypeStruct((R, LANES), F32)
    else:
        out_shape = jax.ShapeDtypeStruct((N_DEV, R, LANES), F32)
    return pl.pallas_call(
        body, name=name, out_shape=out_shape,
        in_specs=[VMEM_SPEC], out_specs=VMEM_SPEC, scratch_shapes=scratch,
        compiler_params=pltpu.CompilerParams(has_side_effects=True, vmem_limit_bytes=VMEM_LIMIT),
    )(pack)


def _adamw_math(w, g, m, v):
    m = ADAM_B1 * m + (1.0 - ADAM_B1) * g
    v = ADAM_B2 * v + (1.0 - ADAM_B2) * jnp.square(g)
    m_hat = m / (1.0 - ADAM_B1 ** ADAM_STEP)
    v_hat = v / (1.0 - ADAM_B2 ** ADAM_STEP)
    delta = -ADAM_LR * (m_hat / (jnp.sqrt(v_hat) + ADAM_EPS) + ADAM_WD * w)
    return delta, m, v


def _adamw_from_partials(parts, w, m, v, *, name):
    R, C = w.shape
    rl = parts[0].shape[1]
    assert all(p.shape == (N_DEV, rl, C) for p in parts) and rl * len(parts) == R
    tr = _tile(rl, (256, 128, 64, 32, 16))
    per = rl // tr
    L = len(parts)

    def body(*refs):
        p_refs = refs[:L]
        w_ref, m_ref, v_ref, g_ref, d_ref, nm_ref, nv_ref = refs[L:]
        i = pl.program_id(0)
        for l in range(L):
            @pl.when((i >= l * per) & (i < (l + 1) * per))
            def _(l=l):
                p_ref = p_refs[l]
                g = p_ref[0].astype(F32)
                for s in range(1, N_DEV):
                    g = g + p_ref[s].astype(F32)
                d, nm, nv = _adamw_math(w_ref[...], g, m_ref[...], v_ref[...])
                g_ref[...] = g
                d_ref[...] = d
                nm_ref[...] = nm
                nv_ref[...] = nv

    blk = pl.BlockSpec((tr, C), lambda i: (i, 0))
    part_specs = [pl.BlockSpec((N_DEV, tr, C), lambda i, l=l: (0, jnp.clip(i - l * per, 0, per - 1), 0))
                  for l in range(L)]
    return pl.pallas_call(
        body, name=name,
        out_shape=(jax.ShapeDtypeStruct((R, C), F32),) * 4,
        grid=(R // tr,),
        in_specs=part_specs + [blk, blk, blk],
        out_specs=(blk,) * 4,
        compiler_params=_cparams("parallel"),
    )(*parts, w, m, v)


def _adamw_small(g, w, m, v, *, name):
    def body(g_ref, w_ref, m_ref, v_ref, d_ref, nm_ref, nv_ref):
        d, nm, nv = _adamw_math(w_ref[...], g_ref[...], m_ref[...], v_ref[...])
        d_ref[...] = d
        nm_ref[...] = nm
        nv_ref[...] = nv

    return pl.pallas_call(
        body, name=name,
        out_shape=(jax.ShapeDtypeStruct(g.shape, F32),) * 3,
        in_specs=[VMEM_SPEC] * 4, out_specs=(VMEM_SPEC,) * 3,
    )(g, w, m, v)


def _pack_rows(pieces):
    flat = jnp.concatenate([p.reshape(-1) for p in pieces])
    n = flat.shape[0]
    padded = -(-n // (8 * LANES)) * (8 * LANES)
    return jnp.pad(flat, (0, padded - n)).reshape(padded // LANES, LANES)


def _unpack_rows(pack, shapes):
    flat = pack.reshape(-1)
    out, pos = [], 0
    for s in shapes:
        n = int(np.prod(s))
        out.append(flat[pos:pos + n].reshape(s))
        pos += n
    return out


def kernel(x, rel_bias, ab_norm, ab_w_in, ab_conv_w, ab_conv_b, ab_ln_g, ab_ln_b, ab_w_out, sc_norm, sc_w_in, sc_conv_w, sc_w_out, mlp_norm, mlp_w_up, mlp_w_down, final_norm, loss_target, m_rel_bias, m_ab_norm, m_ab_w_in, m_ab_conv_w, m_ab_conv_b, m_ab_ln_g, m_ab_ln_b, m_ab_w_out, m_sc_norm, m_sc_w_in, m_sc_conv_w, m_sc_w_out, m_mlp_norm, m_mlp_w_up, m_mlp_w_down, m_final_norm, v_rel_bias, v_ab_norm, v_ab_w_in, v_ab_conv_w, v_ab_conv_b, v_ab_ln_g, v_ab_ln_b, v_ab_w_out, v_sc_norm, v_sc_w_in, v_sc_conv_w, v_sc_w_out, v_mlp_norm, v_mlp_w_up, v_mlp_w_down, v_final_norm):
    S, D = x.shape[1], x.shape[2]
    CA = ab_conv_b.shape[1]
    C2 = 2 * CA
    AB_IN = C2 + ATTN_IN
    me_lin = _linear(_my_position())
    xs = x.reshape(S, D)
    tgt = loss_target.reshape(S, D)

    cw_sh = ab_conv_w.shape[2]
    scn_sh = sc_norm.shape[1]
    scw_sh = sc_conv_w.shape[2]
    small_sh_shapes = [(CONV_A_WIDTH, cw_sh), (scn_sh,), (SC_CONV_WIDTH, scw_sh)]
    small_params = _small_gather(_pack_rows([ab_conv_w[0], sc_norm[0], sc_conv_w[0]]), reduce=False,
                                 name="allgather_small_params")
    w_in_sh = ab_w_in[0].astype(BF16)
    ag_ab, wait_ab = _allgather_start(
        [w_in_sh, ab_w_out[0].astype(BF16)], ["slot", "row"],
        [(N_DEV,) + w_in_sh.shape, (N_DEV * ab_w_out.shape[1], D)], name="allgather_ab", dep=small_params)
    ag_mlp, wait_mlp = _allgather_start(
        [mlp_w_up.astype(BF16), mlp_w_down.astype(BF16)], ["lcol", "lrow"],
        [(2, D, N_DEV * mlp_w_up.shape[2]), (2, N_DEV * mlp_w_down.shape[1], D)], name="allgather_mlp",
        dep=ag_ab[3])
    ag_sc, wait_sc = _allgather_start(
        [sc_w_in[0].astype(BF16), sc_w_out[0].astype(BF16)], ["col", "row"],
        [(D, N_DEV * sc_w_in.shape[2]), (N_DEV * sc_w_out.shape[1], D)], name="allgather_sc",
        dep=ag_mlp[3])

    per_dev = [_unpack_rows(small_params[s], small_sh_shapes) for s in range(N_DEV)]
    conv_w_full = jnp.concatenate([p[0] for p in per_dev], axis=1)
    sc_norm_full = jnp.concatenate([p[1] for p in per_dev], axis=0)[None]
    sc_conv_full = jnp.concatenate([p[2] for p in per_dev], axis=1)

    buckets = _bucket_tables()
    biases = _bias_tables(rel_bias, buckets, name="bias_tables")

    dils = [dil for _, dil in DILATED_GROUPS]
    n0_all = _rmsnorm_fwd(xs, ab_norm, name="norm_ab", dep=ag_sc[3], views=dils[1:])
    n0 = n0_all[0]
    w_in_g, w_out = wait_ab(n0)
    w_in = jnp.transpose(w_in_g, (1, 0, 2)).reshape(D, AB_IN)
    w_c = w_in[:, :C2]
    w_q = w_in[:, C2:]
    w_grp = [jnp.concatenate([w_q[:, t * N_GROUPS * ATTN_OUT + g * ATTN_OUT:][:, :ATTN_OUT] for t in range(3)], axis=1)
             for g in range(N_GROUPS)]
    uc = _mm_nn(n0, w_c, out_dtype=F32, name="mm_ab_in_conv", wide=True)
    uqs = [_mm_nn(n0_all[g], w_grp[g], out_dtype=BF16, slabs=dils[g], name=f"mm_ab_in_qkv{g}")
           for g in range(N_GROUPS)]
    ya, hglu, ct = _conv_fwd(uc, conv_w_full, ab_conv_b, ab_ln_g, ab_ln_b, name="conv_fwd")
    outs, lses = zip(*[_attn_fwd(uqs[g], biases, g, dils[g], name=f"attn_fwd_{g}") for g in range(N_GROUPS)])
    cat, outf, lse = _attn_merge(outs, lses, ya, name="attn_merge")
    h1 = _mm_nn(cat, w_out, out_dtype=F32, residual=xs, name="mm_ab_out", wide=True)
    w_up, w_dn = wait_mlp(h1)
    n1, z0 = _norm_mm_nn(h1, mlp_norm[0:1], (w_up, 0), out_dtype=BF16, name="norm_mm_up0", wide=True)
    h2 = _mm_nn(z0, (w_dn, 0), out_dtype=F32, residual=h1, a_fn=_relu_sq, name="mm_down0", wide=True)
    w_sc_in, w_sc_out = wait_sc(h2)
    n2, u3 = _norm_mm_nn(h2, sc_norm_full, w_sc_in, out_dtype=BF16, name="norm_mm_sc_in", wide=True)
    ysc = _sc_fwd(u3, sc_conv_full, name="sc_fwd")
    h3 = _mm_nn(ysc, w_sc_out, out_dtype=F32, residual=h2, name="mm_sc_out", wide=True)
    n3, z1 = _norm_mm_nn(h3, mlp_norm[1:2], (w_up, 1), out_dtype=BF16, name="norm_mm_up1", wide=True)
    h4 = _mm_nn(z1, (w_dn, 1), out_dtype=F32, residual=h3, a_fn=_relu_sq, name="mm_down1", wide=True)

    def dz_epilogue(acc, z):
        return acc * (2.0 * jnp.maximum(z.astype(F32), 0.0))

    dh4, dh4b, acc_final = _loss_bwd(h4, tgt, final_norm[None], name="loss_bwd")
    dz1 = _mm_nt([(dh4b, (w_dn, 1))], out_dtype=BF16, epilogue=dz_epilogue, extra=z1, name="mm_d_down1")
    g_dn1 = _mm_tn(z1, dh4b, a_fn=_relu_sq, name="mm_gw_down1")
    g_up1 = _mm_tn(n3, dz1, name="mm_gw_up1")
    rs_mlp1, wait_rs_mlp1 = _scatter_start([g_up1, g_dn1], ["col", "row"],
                                           [mlp_w_up.shape[1:], mlp_w_down.shape[1:]], name="scatter_mlp1")
    dh3, dh3b, acc_mlp1 = _mm_nt_rms_bwd(dz1, (w_up, 1), h3, mlp_norm[1:2], dh4, name="mm_d_up1_norm_bwd",
                                         dep=rs_mlp1[3])

    dysc = _mm_nt([(dh3b, w_sc_out)], out_dtype=F32, name="mm_d_sc_out")
    g_sc_out = _mm_tn(ysc, dh3b, name="mm_gw_sc_out")
    du3, acc_scw = _sc_bwd(u3, dysc, sc_conv_full, name="sc_bwd")
    g_sc_in = _mm_tn(n2, du3, name="mm_gw_sc_in")
    rs_sc, wait_rs_sc = _scatter_start([g_sc_in, g_sc_out], ["col", "row"],
                                       [sc_w_in.shape[1:], sc_w_out.shape[1:]], name="scatter_sc")
    dh2, dh2b, acc_sc = _mm_nt_rms_bwd(du3, w_sc_in, h2, sc_norm_full, dh3, name="mm_d_sc_in_norm_bwd",
                                       dep=rs_sc[3])

    dz0 = _mm_nt([(dh2b, (w_dn, 0))], out_dtype=BF16, epilogue=dz_epilogue, extra=z0, name="mm_d_down0")
    g_dn0 = _mm_tn(z0, dh2b, a_fn=_relu_sq, name="mm_gw_down0")
    g_up0 = _mm_tn(n1, dz0, name="mm_gw_up0")
    rs_mlp0, wait_rs_mlp0 = _scatter_start([g_up0, g_dn0], ["col", "row"],
                                           [mlp_w_up.shape[1:], mlp_w_down.shape[1:]], name="scatter_mlp0")
    dh1, dh1b, acc_mlp0 = _mm_nt_rms_bwd(dz0, (w_up, 0), h1, mlp_norm[0:1], dh2, name="mm_d_up0_norm_bwd",
                                         dep=rs_mlp0[3])

    dcat = _mm_nt([(dh1b, w_out)], out_dtype=F32, name="mm_d_ab_out")
    g_ab_out = _mm_tn(cat, dh1b, name="mm_gw_ab_out")
    prep = _attn_prep(dcat, outf, lse, name="attn_prep")
    dqkv, dbs = zip(*[_attn_bwd(uqs[g], prep[g][0], prep[g][2], prep[g][1], biases, g, dils[g],
                                name=f"attn_bwd_{g}") for g in range(N_GROUPS)])
    drel = _relbias_reduce(dbs, buckets, name="relbias_reduce")
    dc, acc_conv = _conv_bwd_ln(ct, dcat, hglu, ab_ln_g, ab_ln_b, name="conv_bwd_ln")
    duc = _conv_bwd_in(dc, uc, conv_w_full, name="conv_bwd_in")
    g_wc = _mm_tn(n0, duc, name="mm_gw_ab_in_conv")
    g_wgrp = [_mm_tn(n0_all[g], dqkv[g], slabs=dils[g], name=f"mm_gw_ab_in_qkv{g}") for g in range(N_GROUPS)]
    g_wq = jnp.concatenate([g_wgrp[g][:, t * ATTN_OUT:(t + 1) * ATTN_OUT]
                            for t in range(3) for g in range(N_GROUPS)], axis=1)
    g_w_in = jnp.concatenate([g_wc, g_wq], axis=1).reshape(D, N_DEV, AB_IN // N_DEV).transpose(1, 0, 2)
    rs_ab, wait_rs_ab = _scatter_start([g_w_in, g_ab_out], ["slot", "row"],
                                       [w_in_sh.shape, ab_w_out.shape[1:]], name="scatter_ab")
    dn0 = _mm_nt([(duc, w_c), (dqkv[0], w_grp[0])], out_dtype=F32, name="mm_d_ab_in", dep=rs_ab[3])
    dn0_views = [(_mm_nt([(dqkv[g], w_grp[g])], out_dtype=F32, slabs=dils[g], name=f"mm_d_ab_in_qkv{g}"), dils[g])
                 for g in range(1, N_GROUPS)]
    grad_x, grad_xb, acc_ab = _rms_bwd(xs, ab_norm, dn0, dh1, name="norm_ab_bwd", dn_views=dn0_views)

    small_full = [drel[:, :, :NUM_BUCKETS].transpose(2, 0, 1).reshape(NUM_BUCKETS, N_GROUPS * HEADS_PER_GROUP),
                  acc_ab[0], acc_conv[0:CONV_A_WIDTH], acc_conv[32],
                  acc_conv[33], acc_conv[34], acc_sc[0], acc_scw[0:SC_CONV_WIDTH],
                  jnp.stack([acc_mlp0[0], acc_mlp1[0]]), acc_final[0], acc_final[1]]
    small_full_shapes = [p.shape for p in small_full]
    summed = _unpack_rows(_small_gather(_pack_rows(small_full), reduce=True, name="allreduce_small"),
                          small_full_shapes)
    (s_rel, s_abn, s_cw, s_cb, s_lg, s_lb, s_scn, s_scw, s_mlpn, s_fn, s_err) = summed
    loss = (0.5 / D) * jnp.sum(s_err)
    small_grads = {
        "rel_bias": s_rel, "ab_norm": s_abn[None],
        "ab_conv_w": lax.dynamic_slice_in_dim(s_cw, me_lin * cw_sh, cw_sh, axis=1)[None],
        "ab_conv_b": s_cb[None], "ab_ln_g": s_lg[None], "ab_ln_b": s_lb[None],
        "sc_norm": lax.dynamic_slice_in_dim(s_scn, me_lin * scn_sh, scn_sh, axis=0)[None],
        "sc_conv_w": lax.dynamic_slice_in_dim(s_scw, me_lin * scw_sh, scw_sh, axis=1)[None],
        "mlp_norm": s_mlpn, "final_norm": s_fn,
    }
    small_w = {"rel_bias": (rel_bias, m_rel_bias, v_rel_bias), "ab_norm": (ab_norm, m_ab_norm, v_ab_norm),
               "ab_conv_w": (ab_conv_w, m_ab_conv_w, v_ab_conv_w), "ab_conv_b": (ab_conv_b, m_ab_conv_b, v_ab_conv_b),
               "ab_ln_g": (ab_ln_g, m_ab_ln_g, v_ab_ln_g), "ab_ln_b": (ab_ln_b, m_ab_ln_b, v_ab_ln_b),
               "sc_norm": (sc_norm, m_sc_norm, v_sc_norm), "sc_conv_w": (sc_conv_w, m_sc_conv_w, v_sc_conv_w),
               "mlp_norm": (mlp_norm, m_mlp_norm, v_mlp_norm), "final_norm": (final_norm, m_final_norm, v_final_norm)}
    small_names = list(small_grads)
    small_shapes = [small_grads[n].shape for n in small_names]
    d_pack, m_pack, v_pack = _adamw_small(
        _pack_rows([small_grads[n] for n in small_names]), _pack_rows([small_w[n][0] for n in small_names]),
        _pack_rows([small_w[n][1] for n in small_names]), _pack_rows([small_w[n][2] for n in small_names]),
        name="adamw_small")
    small = {n: (small_grads[n], d, nm_, nv_) for n, d, nm_, nv_ in zip(
        small_names, _unpack_rows(d_pack, small_shapes), _unpack_rows(m_pack, small_shapes),
        _unpack_rows(v_pack, small_shapes))}

    p_up1, p_dn1 = wait_rs_mlp1(grad_xb)
    p_sc_in, p_sc_out = wait_rs_sc(grad_xb)
    p_up0, p_dn0 = wait_rs_mlp0(grad_xb)
    p_w_in, p_ab_out = wait_rs_ab(grad_xb)
    big = {}
    for nm, parts, w, m, v in (("ab_w_in", [p_w_in], ab_w_in, m_ab_w_in, v_ab_w_in),
                               ("ab_w_out", [p_ab_out], ab_w_out, m_ab_w_out, v_ab_w_out),
                               ("sc_w_in", [p_sc_in], sc_w_in, m_sc_w_in, v_sc_w_in),
                               ("sc_w_out", [p_sc_out], sc_w_out, m_sc_w_out, v_sc_w_out),
                               ("mlp_w_up", [p_up0, p_up1], mlp_w_up, m_mlp_w_up, v_mlp_w_up),
                               ("mlp_w_down", [p_dn0, p_dn1], mlp_w_down, m_mlp_w_down, v_mlp_w_down)):
        C = w.shape[-1]
        res = _adamw_from_partials(parts, w.reshape(-1, C), m.reshape(-1, C), v.reshape(-1, C), name="adamw_" + nm)
        big[nm] = tuple(r.reshape(w.shape) for r in res)

    order = ["rel_bias", "ab_norm", "ab_w_in", "ab_conv_w", "ab_conv_b", "ab_ln_g", "ab_ln_b", "ab_w_out",
             "sc_norm", "sc_w_in", "sc_conv_w", "sc_w_out", "mlp_norm", "mlp_w_up", "mlp_w_down", "final_norm"]
    allres = {**big, **small}
    return (loss, grad_x.reshape(x.shape),
            *[allres[n][0] for n in order], *[allres[n][1] for n in order],
            *[allres[n][2] for n in order], *[allres[n][3] for n in order])
```

```python
import functools
import math

import numpy as np
import jax
import jax.numpy as jnp
from jax import lax
from jax.experimental import pallas as pl
from jax.experimental.pallas import tpu as pltpu

F32 = jnp.float32
BF16 = jnp.bfloat16

HEAD_DIM = 64
HEADS_PER_GROUP = 8
DILATED_GROUPS = ((128, 1), (512, 4), (2048, 16))
N_GROUPS = 3
ATTN_OUT = HEADS_PER_GROUP * HEAD_DIM
ATTN_IN = 3 * N_GROUPS * ATTN_OUT
GROUP_QKV = 3 * ATTN_OUT
ATTN_BLK = 128
CONV_A_WIDTH = 31
SC_CONV_WIDTH = 3
NUM_BUCKETS = 32
REL_MAX_DISTANCE = 2048
RMS_EPS = 1e-6
LN_EPS = 1e-5
NEG_INF = -1e30
ADAM_LR = 0.001
ADAM_B1 = 0.9
ADAM_B2 = 0.999
ADAM_EPS = 1e-08
ADAM_WD = 0.01
ADAM_STEP = 10

N_DEV = 8
HALO = 32
LANES = 128
VMEM_LIMIT = 56 * 1024 * 1024
MESH = pl.DeviceIdType.MESH
ANY = pl.BlockSpec(memory_space=pl.ANY)
VMEM_SPEC = pl.BlockSpec(memory_space=pltpu.VMEM)


def _tile(n, prefs):
    for t in prefs:
        if n % t == 0:
            return t
    return n


def _cparams(*sem):
    return pltpu.CompilerParams(dimension_semantics=sem, vmem_limit_bytes=VMEM_LIMIT)


def _relu_sq(z):
    return jnp.square(jnp.maximum(z, 0))


def _dot_nt(a, b):
    return lax.dot_general(a, b, (((1,), (1,)), ((), ())), preferred_element_type=F32)


def _dot_tn(a, b):
    return lax.dot_general(a, b, (((0,), (0,)), ((), ())), preferred_element_type=F32)


def _weight(b):
    if not isinstance(b, tuple):
        return b, b.shape, pl.BlockSpec
    arr, layer = b

    def spec(block, index_map):
        return pl.BlockSpec((None,) + tuple(block), lambda *g: (layer,) + tuple(index_map(*g)))

    return arr, arr.shape[1:], spec


def _mm_nn(a, b, *, out_dtype, name, residual=None, a_fn=None, slabs=1, wide=False):
    M, K = a.shape
    K //= slabs
    b, (_, N), b_spec = _weight(b)
    tm = _tile(M, (1024, 512, 256) if wide else (2048, 1024, 512, 256))
    tn = _tile(N, (1024, 512, 384, 256, 128) if wide else (512, 384, 256, 128))
    tk = _tile(K, (2048, 1024, 512, 256, 128) if wide else (1024, 512, 256, 128))
    nk = K // tk
    nj = N // tn
    has_res = residual is not None

    def body(*refs):
        if has_res:
            a_ref, b_ref, r_ref, o_ref = refs[:4]
        else:
            a_ref, b_ref, o_ref = refs[:3]
        av = a_ref[...]
        if a_fn is not None:
            av = a_fn(av)
        part = jnp.dot(av, b_ref[...], preferred_element_type=F32)

        def finish(acc):
            if has_res:
                acc = acc + r_ref[...]
            o_ref[...] = acc.astype(o_ref.dtype)

        if nk == 1:
            finish(part)
        else:
            acc_ref = refs[-1]
            k = pl.program_id(2)

            @pl.when(k == 0)
            def _():
                acc_ref[...] = part

            @pl.when((k > 0) & (k < nk - 1))
            def _():
                acc_ref[...] += part

            @pl.when(k == nk - 1)
            def _():
                finish(acc_ref[...] + part)

    in_specs = [pl.BlockSpec((tm, tk), lambda i, j, k: (i, (j // nj) * nk + k)),
                b_spec((tk, tn), lambda i, j, k: (k, j % nj))]
    args = [a, b]
    if has_res:
        in_specs.append(pl.BlockSpec((tm, tn), lambda i, j, k: (i, j)))
        args.append(residual)
    return pl.pallas_call(
        body, name=name,
        out_shape=jax.ShapeDtypeStruct((M, slabs * N), out_dtype),
        grid=(M // tm, slabs * nj, nk),
        in_specs=in_specs,
        out_specs=pl.BlockSpec((tm, tn), lambda i, j, k: (i, j)),
        scratch_shapes=[pltpu.VMEM((tm, tn), F32)] if nk > 1 else [],
        compiler_params=_cparams("parallel", "parallel", "arbitrary"),
    )(*args)


def _norm_mm_nn(h, g, b, *, out_dtype, name, wide=False):
    M, K = h.shape
    b, (_, N), b_spec = _weight(b)
    tm = _tile(M, (2048, 1024, 512, 256))
    tn = _tile(N, (1024, 512, 384, 256, 128) if wide else (512, 384, 256, 128))

    def body(h_ref, g_ref, b_ref, n_ref, o_ref):
        @pl.when(pl.program_id(1) == 0)
        def _():
            x = h_ref[...]
            r = lax.rsqrt(jnp.mean(x * x, axis=-1, keepdims=True) + RMS_EPS)
            n_ref[...] = (x * r * g_ref[...]).astype(BF16)

        o_ref[...] = jnp.dot(n_ref[...], b_ref[...], preferred_element_type=F32).astype(o_ref.dtype)

    return pl.pallas_call(
        body, name=name,
        out_shape=(jax.ShapeDtypeStruct((M, K), BF16), jax.ShapeDtypeStruct((M, N), out_dtype)),
        grid=(M // tm, N // tn),
        in_specs=[pl.BlockSpec((tm, K), lambda i, j: (i, 0)), pl.BlockSpec((1, K), lambda i, j: (0, 0)),
                  b_spec((K, tn), lambda i, j: (0, j))],
        out_specs=(pl.BlockSpec((tm, K), lambda i, j: (i, 0)), pl.BlockSpec((tm, tn), lambda i, j: (i, j))),
        compiler_params=_cparams("parallel", "arbitrary"),
    )(h, g, b)


def _mm_nt(pairs, *, out_dtype, name, epilogue=None, extra=None, dep=None, slabs=1):
    assert slabs == 1 or len(pairs) == 1
    M = pairs[0][0].shape[0]
    weights = [_weight(p[1]) for p in pairs]
    Ko = weights[0][1][0]
    tm = _tile(M, (2048, 1024, 512, 256))
    to = _tile(Ko, (1024, 512, 256, 128))
    njo = Ko // to
    tks = [_tile(p[0].shape[1] // slabs, (1024, 768, 512, 256, 128)) for p in pairs]
    steps = [p[0].shape[1] // slabs // tk for p, tk in zip(pairs, tks)]
    offs = [sum(steps[:i]) for i in range(len(pairs))]
    nk = sum(steps)
    npair = len(pairs)
    has_extra = extra is not None

    def body(*refs):
        ab = refs[:2 * npair]
        pos = 2 * npair
        e_ref = None
        if has_extra:
            e_ref = refs[pos]
            pos += 1
        if dep is not None:
            pos += 1
        o_ref = refs[pos]
        acc_ref = refs[pos + 1]
        k = pl.program_id(2)

        @pl.when(k == 0)
        def _():
            acc_ref[...] = jnp.zeros_like(acc_ref)

        for p in range(npair):
            @pl.when((k >= offs[p]) & (k < offs[p] + steps[p]))
            def _(p=p):
                acc_ref[...] += _dot_nt(ab[2 * p][...], ab[2 * p + 1][...])

        @pl.when(k == nk - 1)
        def _():
            acc = acc_ref[...]
            if epilogue is not None:
                acc = epilogue(acc, e_ref[...] if has_extra else None)
            o_ref[...] = acc.astype(o_ref.dtype)

    in_specs, args = [], []
    for p, (a, b) in enumerate(pairs):
        def kidx(k, p=p):
            return jnp.clip(k - offs[p], 0, steps[p] - 1)
        in_specs.append(pl.BlockSpec((tm, tks[p]),
                                     lambda i, j, k, kidx=kidx, p=p: (i, (j // njo) * steps[p] + kidx(k))))
        in_specs.append(weights[p][2]((to, tks[p]), lambda i, j, k, kidx=kidx: (j % njo, kidx(k))))
        args += [a, weights[p][0]]
    if has_extra:
        in_specs.append(pl.BlockSpec((tm, to), lambda i, j, k: (i, j)))
        args.append(extra)
    if dep is not None:
        in_specs.append(ANY)
        args.append(dep)
    return pl.pallas_call(
        body, name=name,
        out_shape=jax.ShapeDtypeStruct((M, slabs * Ko), out_dtype),
        grid=(M // tm, slabs * njo, nk),
        in_specs=in_specs,
        out_specs=pl.BlockSpec((tm, to), lambda i, j, k: (i, j)),
        scratch_shapes=[pltpu.VMEM((tm, to), F32)],
        compiler_params=_cparams("parallel", "parallel", "arbitrary"),
    )(*args)


def _mm_tn(a, b, *, name, a_fn=None, slabs=1):
    M, K = a.shape
    K //= slabs
    N = b.shape[1] // slabs
    tm = _tile(M, (4096, 2048, 1024, 512, 256))
    tk = _tile(K, (1024, 768, 512, 384, 256, 128))
    tn = _tile(N, (1024, 768, 512, 384, 256, 128))
    nmi = M // tm
    nm = slabs * nmi
    nki, nnj = K // tk, N // tn

    def body(a_ref, b_ref, o_ref, acc_ref):
        m = pl.program_id(2)
        av = a_ref[...]
        if a_fn is not None:
            av = a_fn(av)
        part = _dot_tn(av, b_ref[...])
        if nm == 1:
            o_ref[...] = part.astype(o_ref.dtype)
            return

        @pl.when(m == 0)
        def _():
            acc_ref[...] = part

        @pl.when((m > 0) & (m < nm - 1))
        def _():
            acc_ref[...] += part

        @pl.when(m == nm - 1)
        def _():
            o_ref[...] = (acc_ref[...] + part).astype(o_ref.dtype)

    return pl.pallas_call(
        body, name=name,
        out_shape=jax.ShapeDtypeStruct((K, N), BF16),
        grid=(K // tk, N // tn, nm),
        in_specs=[pl.BlockSpec((tm, tk), lambda i, j, m: (m % nmi, (m // nmi) * nki + i)),
                  pl.BlockSpec((tm, tn), lambda i, j, m: (m % nmi, (m // nmi) * nnj + j))],
        out_specs=pl.BlockSpec((tk, tn), lambda i, j, m: (i, j)),
        scratch_shapes=[pltpu.VMEM((tk, tn), F32)],
        compiler_params=_cparams("parallel", "parallel", "arbitrary"),
    )(a, b)


def _rmsnorm_fwd(h, g, *, name, dep=None, views=()):
    S, D = h.shape
    tm = _tile(S, (512, 256))
    nv = len(views)

    def body(h_ref, g_ref, *rest):
        n_out = 1 + nv
        outs = rest[len(rest) - n_out - (1 if nv else 0):len(rest) - (1 if nv else 0)]
        x = h_ref[...]
        r = lax.rsqrt(jnp.mean(x * x, axis=-1, keepdims=True) + RMS_EPS)
        y = x * r * g_ref[...]
        outs[0][...] = y.astype(BF16)
        if nv:
            scr = rest[-1]
            _to_chunks(scr, y)
            for v_ref, d in zip(outs[1:], views):
                _slabs_from_chunks(v_ref, scr, d, BF16)

    res = pl.pallas_call(
        body, name=name,
        out_shape=(jax.ShapeDtypeStruct((S, D), BF16),)
        + tuple(jax.ShapeDtypeStruct((S // d, d * D), BF16) for d in views),
        grid=(S // tm,),
        in_specs=[pl.BlockSpec((tm, D), lambda i: (i, 0)), pl.BlockSpec((1, D), lambda i: (0, 0))]
        + ([ANY] if dep is not None else []),
        out_specs=(pl.BlockSpec((tm, D), lambda i: (i, 0)),)
        + tuple(pl.BlockSpec((tm // d, d * D), lambda i: (i, 0)) for d in views),
        scratch_shapes=[_chunk_scratch(tm, D)] if nv else [],
        compiler_params=_cparams("parallel"),
    )(h, g, *([dep] if dep is not None else []))
    return res if nv else res[0]


def _rms_bwd_rows(x, g, dy):
    r = lax.rsqrt(jnp.mean(x * x, axis=-1, keepdims=True) + RMS_EPS)
    xh = x * r
    gy = dy * g
    dx = r * (gy - xh * jnp.mean(xh * gy, axis=-1, keepdims=True))
    return dx, dy * xh


def _rms_bwd(x, g, dn, dres, *, name, dn_views=()):
    S, D = x.shape
    tm = _tile(S, (256,))
    nv = len(dn_views)

    def body(x_ref, g_ref, dn_ref, dr_ref, *rest):
        v_refs = rest[:nv]
        dx_ref, dxb_ref, dg_ref = rest[nv:nv + 3]
        scr = rest[nv + 3:]
        i = pl.program_id(0)
        dn = dn_ref[...]
        for v_ref, s_ref, (_, d) in zip(v_refs, scr, dn_views):
            _chunks_from_slabs(s_ref, v_ref, d)
            dn = dn + _from_chunks(s_ref)
        dx, dgx = _rms_bwd_rows(x_ref[...], g_ref[...], dn)
        tot = dr_ref[...] + dx
        dx_ref[...] = tot
        dxb_ref[...] = tot.astype(BF16)

        @pl.when(i == 0)
        def _():
            dg_ref[...] = jnp.zeros_like(dg_ref)

        dg_ref[0:1, :] += jnp.sum(dgx, axis=0, keepdims=True)

    row = pl.BlockSpec((tm, D), lambda i: (i, 0))
    return pl.pallas_call(
        body, name=name,
        out_shape=(jax.ShapeDtypeStruct((S, D), F32), jax.ShapeDtypeStruct((S, D), BF16),
                   jax.ShapeDtypeStruct((8, D), F32)),
        grid=(S // tm,),
        in_specs=[row, pl.BlockSpec((1, D), lambda i: (0, 0)), row, row]
        + [pl.BlockSpec((tm // d, d * D), lambda i: (i, 0)) for _, d in dn_views],
        out_specs=(row, row, pl.BlockSpec((8, D), lambda i: (0, 0))),
        scratch_shapes=[_chunk_scratch(tm, D)] * nv,
        compiler_params=_cparams("arbitrary"),
    )(x, g, dn, dres, *[a for a, _ in dn_views])


def _mm_nt_rms_bwd(a, b, x, g, dres, *, name, dep=None):
    M, N = a.shape
    b, (D, _), b_spec = _weight(b)
    tm = _tile(M, (1024, 512, 256))
    tk = _tile(N, (1024, 512, 256, 128))
    nk = N // tk

    def body(a_ref, b_ref, x_ref, g_ref, dr_ref, *rest):
        dx_ref, dxb_ref, dg_ref, acc_ref = rest[-4:]
        i = pl.program_id(0)
        k = pl.program_id(1)
        part = _dot_nt(a_ref[...], b_ref[...])

        @pl.when((i == 0) & (k == 0))
        def _():
            dg_ref[...] = jnp.zeros_like(dg_ref)

        @pl.when(k == 0)
        def _():
            acc_ref[...] = part

        @pl.when((k > 0) & (k < nk - 1))
        def _():
            acc_ref[...] += part

        @pl.when(k == nk - 1)
        def _():
            dn = part if nk == 1 else acc_ref[...] + part
            dx, dgx = _rms_bwd_rows(x_ref[...], g_ref[...], dn)
            tot = dr_ref[...] + dx
            dx_ref[...] = tot
            dxb_ref[...] = tot.astype(BF16)
            dg_ref[0:1, :] += jnp.sum(dgx, axis=0, keepdims=True)

    row = pl.BlockSpec((tm, D), lambda i, k: (i, 0))
    in_specs = [pl.BlockSpec((tm, tk), lambda i, k: (i, k)), b_spec((D, tk), lambda i, k: (0, k)),
                row, pl.BlockSpec((1, D), lambda i, k: (0, 0)), row]
    args = [a, b, x, g, dres]
    if dep is not None:
        in_specs.append(ANY)
        args.append(dep)
    return pl.pallas_call(
        body, name=name,
        out_shape=(jax.ShapeDtypeStruct((M, D), F32), jax.ShapeDtypeStruct((M, D), BF16),
                   jax.ShapeDtypeStruct((8, D), F32)),
        grid=(M // tm, nk),
        in_specs=in_specs,
        out_specs=(row, row, pl.BlockSpec((8, D), lambda i, k: (0, 0))),
        scratch_shapes=[pltpu.VMEM((tm, D), F32)],
        compiler_params=_cparams("arbitrary", "arbitrary"),
    )(*args)


def _loss_bwd(h, target, g, *, name):
    S, D = h.shape
    tm = _tile(S, (256,))

    def body(h_ref, t_ref, g_ref, dx_ref, dxb_ref, acc_ref):
        i = pl.program_id(0)
        x = h_ref[...]
        gv = g_ref[...]
        r = lax.rsqrt(jnp.mean(x * x, axis=-1, keepdims=True) + RMS_EPS)
        err = x * r * gv - t_ref[...]
        dx, dgx = _rms_bwd_rows(x, gv, err * (1.0 / D))
        dx_ref[...] = dx
        dxb_ref[...] = dx.astype(BF16)

        @pl.when(i == 0)
        def _():
            acc_ref[...] = jnp.zeros_like(acc_ref)

        acc_ref[0:1, :] += jnp.sum(dgx, axis=0, keepdims=True)
        acc_ref[1:2, :] += jnp.sum(err * err, axis=0, keepdims=True)

    row = pl.BlockSpec((tm, D), lambda i: (i, 0))
    return pl.pallas_call(
        body, name=name,
        out_shape=(jax.ShapeDtypeStruct((S, D), F32), jax.ShapeDtypeStruct((S, D), BF16),
                   jax.ShapeDtypeStruct((8, D), F32)),
        grid=(S // tm,),
        in_specs=[row, row, pl.BlockSpec((1, D), lambda i: (0, 0))],
        out_specs=(row, row, pl.BlockSpec((8, D), lambda i: (0, 0))),
        compiler_params=_cparams("arbitrary"),
    )(h, target, g)


SUBLANES = 8
CONV_ROWS = 64


def _build_shifted(ext_ref, rot_ref, ts, shifts=tuple(range(1, SUBLANES))):
    rows = ts + HALO - SUBLANES
    for j in shifts:
        rot_ref[j, 0:rows, :] = ext_ref[j:j + rows, :]


def _shifted(ext_ref, rot_ref, off, r0, nrows, cs):
    q, j = divmod(off, SUBLANES)
    start = SUBLANES * q + r0
    if j == 0:
        return ext_ref[start:start + nrows, cs]
    return rot_ref[j, start:start + nrows, cs]


def _conv_fwd(uc, conv_w, conv_b, ln_g, ln_b, *, name):
    S, C2 = uc.shape
    C = C2 // 2
    ts = _tile(S, (512, 256))
    per = ts // HALO

    def body(cur_ref, halo_ref, w_ref, b_ref, g_ref, beta_ref, ya_ref, h_ref, ct_ref, ext_ref, rot_ref):
        i = pl.program_id(0)
        hh = halo_ref[:, 0:C] * jax.nn.sigmoid(halo_ref[:, C:C2])
        ext_ref[0:HALO, :] = jnp.where(i == 0, 0.0, hh)
        hc = cur_ref[:, 0:C] * jax.nn.sigmoid(cur_ref[:, C:C2])
        ext_ref[HALO:HALO + ts, :] = hc
        h_ref[...] = hc
        _build_shifted(ext_ref, rot_ref, ts)
        for c0 in range(0, C, LANES):
            cs = slice(c0, c0 + LANES)
            for r0 in range(0, ts, CONV_ROWS):
                acc = jnp.zeros((CONV_ROWS, LANES), F32)
                for k in range(CONV_A_WIDTH):
                    acc = acc + w_ref[k:k + 1, cs] * _shifted(ext_ref, rot_ref, k + 2, r0, CONV_ROWS, cs)
                ct_ref[r0:r0 + CONV_ROWS, cs] = acc + b_ref[:, cs]
        ct = ct_ref[...]
        mu = jnp.mean(ct, axis=-1, keepdims=True)
        xc = ct - mu
        var = jnp.mean(xc * xc, axis=-1, keepdims=True)
        l = xc * lax.rsqrt(var + LN_EPS) * g_ref[...] + beta_ref[...]
        ya_ref[...] = (l * jax.nn.sigmoid(l)).astype(ya_ref.dtype)

    vec = pl.BlockSpec((1, C), lambda i: (0, 0))
    row = pl.BlockSpec((ts, C), lambda i: (i, 0))
    return pl.pallas_call(
        body, name=name,
        out_shape=(jax.ShapeDtypeStruct((S, C), BF16), jax.ShapeDtypeStruct((S, C), F32),
                   jax.ShapeDtypeStruct((S, C), F32)),
        grid=(S // ts,),
        in_specs=[pl.BlockSpec((ts, C2), lambda i: (i, 0)),
                  pl.BlockSpec((HALO, C2), lambda i: (jnp.maximum(i * per - 1, 0), 0)),
                  pl.BlockSpec((CONV_A_WIDTH, C), lambda i: (0, 0)), vec, vec, vec],
        out_specs=(row, row, row),
        scratch_shapes=[pltpu.VMEM((HALO + ts, C), F32), pltpu.VMEM((8, HALO + ts, C), F32)],
        compiler_params=_cparams("parallel"),
    )(uc, uc, conv_w, conv_b, ln_g, ln_b)


CONV_ACC_ROWS = 40


def _conv_bwd_ln(ct, dcat, hglu, ln_g, ln_b, *, name):
    S, C = ct.shape
    CW = dcat.shape[1]
    ts = _tile(S, (512, 256))
    per = ts // HALO

    def body(ct_ref, dcat_ref, hc_ref, hh_ref, g_ref, beta_ref, dc_ref, acc_ref, ext_ref, rot_ref):
        i = pl.program_id(0)
        ct = ct_ref[...]
        gv = g_ref[...]
        mu = jnp.mean(ct, axis=-1, keepdims=True)
        xc = ct - mu
        rstd = lax.rsqrt(jnp.mean(xc * xc, axis=-1, keepdims=True) + LN_EPS)
        xh = xc * rstd
        l = xh * gv + beta_ref[...]
        sg = jax.nn.sigmoid(l)
        dl = dcat_ref[:, 0:C] * (sg * (1.0 + l * (1.0 - sg)))
        dxh = dl * gv
        dc = rstd * (dxh - jnp.mean(dxh, axis=-1, keepdims=True)
                     - xh * jnp.mean(dxh * xh, axis=-1, keepdims=True))
        dc_ref[...] = dc

        @pl.when(i == 0)
        def _():
            acc_ref[...] = jnp.zeros_like(acc_ref)

        acc_ref[32:33, :] += jnp.sum(dc, axis=0, keepdims=True)
        acc_ref[33:34, :] += jnp.sum(dl * xh, axis=0, keepdims=True)
        acc_ref[34:35, :] += jnp.sum(dl, axis=0, keepdims=True)
        ext_ref[0:HALO, :] = jnp.where(i == 0, 0.0, hh_ref[...])
        ext_ref[HALO:HALO + ts, :] = hc_ref[...]
        _build_shifted(ext_ref, rot_ref, ts)
        for c0 in range(0, C, LANES):
            cs = slice(c0, c0 + LANES)
            dcc = dc_ref[:, cs]
            for k in range(CONV_A_WIDTH):
                acc_ref[k:k + 1, cs] += jnp.sum(dcc * _shifted(ext_ref, rot_ref, k + 2, 0, ts, cs),
                                                axis=0, keepdims=True)

    vec = pl.BlockSpec((1, C), lambda i: (0, 0))
    row = pl.BlockSpec((ts, C), lambda i: (i, 0))
    return pl.pallas_call(
        body, name=name,
        out_shape=(jax.ShapeDtypeStruct((S, C), F32), jax.ShapeDtypeStruct((CONV_ACC_ROWS, C), F32)),
        grid=(S // ts,),
        in_specs=[row, pl.BlockSpec((ts, CW), lambda i: (i, 0)), row,
                  pl.BlockSpec((HALO, C), lambda i: (jnp.maximum(i * per - 1, 0), 0)), vec, vec],
        out_specs=(row, pl.BlockSpec((CONV_ACC_ROWS, C), lambda i: (0, 0))),
        scratch_shapes=[pltpu.VMEM((HALO + ts, C), F32), pltpu.VMEM((8, HALO + ts, C), F32)],
        compiler_params=_cparams("arbitrary"),
    )(ct, dcat, hglu, hglu, ln_g, ln_b)


def _conv_bwd_in(dc, uc, conv_w, *, name):
    S, C = dc.shape
    C2 = 2 * C
    ts = _tile(S, (512, 256))
    per = ts // HALO
    nt = S // ts

    def body(dc_ref, dn_ref, uc_ref, w_ref, du_ref, ext_ref, rot_ref):
        i = pl.program_id(0)
        ext_ref[0:ts, :] = dc_ref[...]
        ext_ref[ts:ts + HALO, :] = jnp.where(i == nt - 1, 0.0, dn_ref[...])
        _build_shifted(ext_ref, rot_ref, ts)
        for c0 in range(0, C, LANES):
            cs = slice(c0, c0 + LANES)
            gs = slice(C + c0, C + c0 + LANES)
            for r0 in range(0, ts, CONV_ROWS):
                rs = slice(r0, r0 + CONV_ROWS)
                acc = jnp.zeros((CONV_ROWS, LANES), F32)
                for k in range(CONV_A_WIDTH):
                    acc = acc + w_ref[k:k + 1, cs] * _shifted(ext_ref, rot_ref, 30 - k, r0, CONV_ROWS, cs)
                sg = jax.nn.sigmoid(uc_ref[rs, gs])
                du_ref[rs, cs] = (acc * sg).astype(du_ref.dtype)
                du_ref[rs, gs] = (acc * uc_ref[rs, cs] * sg * (1.0 - sg)).astype(du_ref.dtype)

    return pl.pallas_call(
        body, name=name,
        out_shape=jax.ShapeDtypeStruct((S, C2), BF16),
        grid=(nt,),
        in_specs=[pl.BlockSpec((ts, C), lambda i: (i, 0)),
                  pl.BlockSpec((HALO, C), lambda i: (jnp.minimum((i + 1) * per, S // HALO - 1), 0)),
                  pl.BlockSpec((ts, C2), lambda i: (i, 0)),
                  pl.BlockSpec((CONV_A_WIDTH, C), lambda i: (0, 0))],
        out_specs=pl.BlockSpec((ts, C2), lambda i: (i, 0)),
        scratch_shapes=[pltpu.VMEM((ts + HALO, C), F32), pltpu.VMEM((8, ts + HALO, C), F32)],
        compiler_params=_cparams("parallel"),
    )(dc, dc, uc, conv_w)


SC_SHIFTS_BACK = ((HALO - 2) % SUBLANES, (HALO - 1) % SUBLANES)
SC_SHIFTS_AHEAD = (1, 2)


def _sc_fwd(u3, conv_w, *, name):
    S, W3 = u3.shape
    W = W3 // 3
    ts = _tile(S, (256,))
    per = ts // HALO

    def body(cur_ref, halo_ref, w_ref, y_ref, ext_ref, rot_ref):
        i = pl.program_id(0)
        cvh = halo_ref[:, W:2 * W].astype(F32) * halo_ref[:, 2 * W:W3].astype(F32)
        ext_ref[0:HALO, :] = jnp.where(i == 0, 0.0, cvh)
        ext_ref[HALO:HALO + ts, :] = cur_ref[:, W:2 * W].astype(F32) * cur_ref[:, 2 * W:W3].astype(F32)
        _build_shifted(ext_ref, rot_ref, ts, SC_SHIFTS_BACK)
        for c0 in range(0, W, LANES):
            cs = slice(c0, c0 + LANES)
            for r0 in range(0, ts, CONV_ROWS):
                rs = slice(r0, r0 + CONV_ROWS)
                k = (w_ref[0:1, cs] * _shifted(ext_ref, rot_ref, HALO - 2, r0, CONV_ROWS, cs)
                     + w_ref[1:2, cs] * _shifted(ext_ref, rot_ref, HALO - 1, r0, CONV_ROWS, cs)
                     + w_ref[2:3, cs] * _shifted(ext_ref, rot_ref, HALO, r0, CONV_ROWS, cs))
                y_ref[rs, cs] = (cur_ref[rs, cs].astype(F32) * k).astype(y_ref.dtype)

    return pl.pallas_call(
        body, name=name,
        out_shape=jax.ShapeDtypeStruct((S, W), BF16),
        grid=(S // ts,),
        in_specs=[pl.BlockSpec((ts, W3), lambda i: (i, 0)),
                  pl.BlockSpec((HALO, W3), lambda i: (jnp.maximum(i * per - 1, 0), 0)),
                  pl.BlockSpec((SC_CONV_WIDTH, W), lambda i: (0, 0))],
        out_specs=pl.BlockSpec((ts, W), lambda i: (i, 0)),
        scratch_shapes=[pltpu.VMEM((HALO + ts, W), F32), pltpu.VMEM((8, HALO + ts, W), F32)],
        compiler_params=_cparams("parallel"),
    )(u3, u3, conv_w)


def _sc_bwd(u3, dy, conv_w, *, name):
    S, W3 = u3.shape
    W = W3 // 3
    ts = _tile(S, (256,))
    per = ts // HALO
    nt = S // ts

    def body(cur_ref, prev_ref, next_ref, dy_ref, dyn_ref, w_ref, du_ref, dw_ref, cv_ext, dk_ext, cv_rot, dk_rot):
        i = pl.program_id(0)
        cvh = prev_ref[:, W:2 * W].astype(F32) * prev_ref[:, 2 * W:W3].astype(F32)
        cv_ext[0:HALO, :] = jnp.where(i == 0, 0.0, cvh)
        cv_ext[HALO:HALO + ts, :] = cur_ref[:, W:2 * W].astype(F32) * cur_ref[:, 2 * W:W3].astype(F32)
        dk_ext[0:ts, :] = dy_ref[...] * cur_ref[:, 0:W].astype(F32)
        dk_ext[ts:ts + HALO, :] = jnp.where(i == nt - 1, 0.0, dyn_ref[...] * next_ref[:, 0:W].astype(F32))
        _build_shifted(cv_ext, cv_rot, ts, SC_SHIFTS_BACK)
        _build_shifted(dk_ext, dk_rot, ts, SC_SHIFTS_AHEAD)

        @pl.when(i == 0)
        def _():
            dw_ref[...] = jnp.zeros_like(dw_ref)

        for c0 in range(0, W, LANES):
            cs = slice(c0, c0 + LANES)
            w0, w1, w2 = w_ref[0:1, cs], w_ref[1:2, cs], w_ref[2:3, cs]
            sums = [jnp.zeros((1, LANES), F32)] * SC_CONV_WIDTH
            for r0 in range(0, ts, CONV_ROWS):
                rs = slice(r0, r0 + CONV_ROWS)
                cv2 = _shifted(cv_ext, cv_rot, HALO - 2, r0, CONV_ROWS, cs)
                cv1 = _shifted(cv_ext, cv_rot, HALO - 1, r0, CONV_ROWS, cs)
                cv0 = _shifted(cv_ext, cv_rot, HALO, r0, CONV_ROWS, cs)
                dk = dk_ext[rs, cs]
                dcv = (w2 * dk + w1 * _shifted(dk_ext, dk_rot, 1, r0, CONV_ROWS, cs)
                       + w0 * _shifted(dk_ext, dk_rot, 2, r0, CONV_ROWS, cs))
                du_ref[rs, cs] = (dy_ref[rs, cs] * (w0 * cv2 + w1 * cv1 + w2 * cv0)).astype(du_ref.dtype)
                du_ref[rs, W + c0:W + c0 + LANES] = (
                    dcv * cur_ref[rs, 2 * W + c0:2 * W + c0 + LANES].astype(F32)).astype(du_ref.dtype)
                du_ref[rs, 2 * W + c0:2 * W + c0 + LANES] = (
                    dcv * cur_ref[rs, W + c0:W + c0 + LANES].astype(F32)).astype(du_ref.dtype)
                for t, cvt in enumerate((cv2, cv1, cv0)):
                    sums[t] = sums[t] + jnp.sum(dk * cvt, axis=0, keepdims=True)
            for t in range(SC_CONV_WIDTH):
                dw_ref[t:t + 1, cs] += sums[t]

    nxt = lambda i: (jnp.minimum((i + 1) * per, S // HALO - 1), 0)
    return pl.pallas_call(
        body, name=name,
        out_shape=(jax.ShapeDtypeStruct((S, W3), BF16), jax.ShapeDtypeStruct((8, W), F32)),
        grid=(nt,),
        in_specs=[pl.BlockSpec((ts, W3), lambda i: (i, 0)),
                  pl.BlockSpec((HALO, W3), lambda i: (jnp.maximum(i * per - 1, 0), 0)),
                  pl.BlockSpec((HALO, W3), nxt),
                  pl.BlockSpec((ts, W), lambda i: (i, 0)),
                  pl.BlockSpec((HALO, W), nxt),
                  pl.BlockSpec((SC_CONV_WIDTH, W), lambda i: (0, 0))],
        out_specs=(pl.BlockSpec((ts, W3), lambda i: (i, 0)), pl.BlockSpec((8, W), lambda i: (0, 0))),
        scratch_shapes=[pltpu.VMEM((HALO + ts, W), F32), pltpu.VMEM((ts + HALO, W), F32),
                        pltpu.VMEM((8, HALO + ts, W), F32), pltpu.VMEM((8, ts + HALO, W), F32)],
        compiler_params=_cparams("arbitrary"),
    )(u3, u3, u3, dy, dy, conv_w)


def _t5_causal_bucket(n):
    max_exact = NUM_BUCKETS // 2
    nf = jnp.maximum(n, 1).astype(F32)
    large = max_exact + (jnp.log(nf / max_exact) / math.log(REL_MAX_DISTANCE / max_exact)
                         * (NUM_BUCKETS - max_exact)).astype(jnp.int32)
    return jnp.where(n < max_exact, n, jnp.minimum(large, NUM_BUCKETS - 1))


def _bucket_tables():
    steps = ATTN_BLK
    m = jnp.arange(steps)[:, None] + steps - jnp.arange(2 * steps)[None, :]
    return jnp.stack([_t5_causal_bucket(jnp.clip(m, 0, steps) * dil).astype(F32) for _, dil in DILATED_GROUPS])


def _bias_tables(rel_bias, buckets, *, name):
    steps = ATTN_BLK

    def body(tab_ref, bk_ref, o_ref):
        g = pl.program_id(0)
        bk = bk_ref[0]
        a_idx = lax.broadcasted_iota(jnp.int32, (steps, 2 * steps), 0)
        c_idx = lax.broadcasted_iota(jnp.int32, (steps, 2 * steps), 1)
        m = a_idx + steps - c_idx
        band = (m >= 0) & (m <= steps)
        band_first = band & (c_idx >= steps)
        for h in range(HEADS_PER_GROUP):
            bias = jnp.zeros((steps, 2 * steps), F32)
            for b in range(NUM_BUCKETS):
                bias = jnp.where(bk == float(b), tab_ref[b, g * HEADS_PER_GROUP + h], bias)
            o_ref[0, 0, h] = jnp.where(band_first, bias, NEG_INF)
            o_ref[0, 1, h] = jnp.where(band, bias, NEG_INF)

    return pl.pallas_call(
        body, name=name,
        out_shape=jax.ShapeDtypeStruct((N_GROUPS, 2, HEADS_PER_GROUP, steps, 2 * steps), F32),
        grid=(N_GROUPS,),
        in_specs=[pl.BlockSpec(memory_space=pltpu.SMEM),
                  pl.BlockSpec((1, steps, 2 * steps), lambda g: (g, 0, 0))],
        out_specs=pl.BlockSpec((1, 2, HEADS_PER_GROUP, steps, 2 * steps), lambda g: (g, 0, 0, 0, 0)),
        compiler_params=_cparams("parallel"),
    )(rel_bias, buckets)


def _lane_is_low():
    return lax.broadcasted_iota(jnp.int32, (1, LANES), 1) < HEAD_DIM


def _stack_heads(x2, low):
    zero = jnp.zeros_like(x2)
    return jnp.concatenate([jnp.where(low, x2, zero), jnp.where(low, zero, x2)], axis=0)


def _qkv_specs(nb):
    nqb = GROUP_QKV // ATTN_OUT

    def spec(t, prev):
        def idx(r, n):
            nn = jnp.minimum(n, nb - 1)
            row = jnp.maximum(nn - 1, 0) if prev else nn
            return (row, r * nqb + t)
        return pl.BlockSpec((ATTN_BLK, ATTN_OUT), idx)

    return [spec(0, False), spec(1, False), spec(1, True), spec(2, False), spec(2, True)]


ATTN_FWD_BLOCKS = 2


def _attn_fwd(uv, bias, g, d, *, name):
    rows = uv.shape[0]
    nsub = ATTN_FWD_BLOCKS
    step_rows = nsub * ATTN_BLK
    nqb = GROUP_QKV // ATTN_OUT

    def body(q_ref, kc_ref, kp_ref, vc_ref, vp_ref, bias_ref, o_ref, l_ref):
        n = pl.program_id(1)
        low = _lane_is_low()
        slabs = [slice(hp * LANES, (hp + 1) * LANES) for hp in range(HEADS_PER_GROUP // 2)]
        for sub in range(nsub):
            qr = slice(sub * ATTN_BLK, (sub + 1) * ATTN_BLK)
            sel = jnp.minimum(n, 1) if sub == 0 else 1

            def with_prev(cur_ref, prev_ref, sl, sub=sub, qr=qr):
                prev = prev_ref[:, sl] if sub == 0 else cur_ref[(sub - 1) * ATTN_BLK:sub * ATTN_BLK, sl]
                return jnp.concatenate([prev, cur_ref[qr, sl]], axis=0)

            scores = [_dot_nt(_stack_heads(q_ref[qr, sl] * (HEAD_DIM ** -0.5), low), with_prev(kc_ref, kp_ref, sl))
                      for sl in slabs]
            probs, dens_all, lses_all = [], [], []
            for hp, s in enumerate(scores):
                ps, dens, lses = [], [], []
                for hh in range(2):
                    logits = s[hh * ATTN_BLK:(hh + 1) * ATTN_BLK] + bias_ref[sel, 2 * hp + hh]
                    mx = jnp.max(logits, axis=-1, keepdims=True)
                    p = jnp.exp(logits - mx)
                    den = jnp.sum(p, axis=-1, keepdims=True)
                    ps.append(p.astype(BF16))
                    dens.append(den)
                    lses.append(jnp.broadcast_to(mx + jnp.log(den), (ATTN_BLK, LANES)))
                probs.append(jnp.concatenate(ps, axis=0))
                dens_all.append(dens)
                lses_all.append(lses)
            for hp, sl in enumerate(slabs):
                pv = jnp.dot(probs[hp], with_prev(vc_ref, vp_ref, sl), preferred_element_type=F32)
                dens, lses = dens_all[hp], lses_all[hp]
                o_ref[qr, sl] = jnp.where(low, pv[0:ATTN_BLK] / dens[0], pv[ATTN_BLK:2 * ATTN_BLK] / dens[1])
                l_ref[qr, sl] = jnp.where(low, lses[0], lses[1])

    def cur(t):
        return pl.BlockSpec((step_rows, ATTN_OUT), lambda r, n: (n, r * nqb + t))

    def prev(t):
        return pl.BlockSpec((ATTN_BLK, ATTN_OUT), lambda r, n: (jnp.maximum(n * nsub - 1, 0), r * nqb + t))

    out_spec = pl.BlockSpec((step_rows, ATTN_OUT), lambda r, n: (n, r))
    return pl.pallas_call(
        body, name=name,
        out_shape=(jax.ShapeDtypeStruct((rows, d * ATTN_OUT), F32),) * 2,
        grid=(d, rows // step_rows),
        in_specs=[cur(0), cur(1), prev(1), cur(2), prev(2),
                  pl.BlockSpec((None, 2, HEADS_PER_GROUP, ATTN_BLK, 2 * ATTN_BLK), lambda r, n: (g, 0, 0, 0, 0))],
        out_specs=(out_spec, out_spec),
        compiler_params=_cparams("parallel", "parallel"),
    )(uv, uv, uv, uv, uv, bias)


def _chunk_scratch(n, width):
    return pltpu.VMEM((width // LANES, n, LANES), F32)


def _to_chunks(scr, val):
    for c in range(scr.shape[0]):
        scr[c] = val[:, c * LANES:(c + 1) * LANES]


def _from_chunks(scr):
    return jnp.concatenate([scr[c] for c in range(scr.shape[0])], axis=1)


def _slabs_from_chunks(dst_ref, scr, d, dtype):
    nc, n, _ = scr.shape
    for r in range(d):
        for c in range(nc):
            col = r * nc * LANES + c * LANES
            dst_ref[:, col:col + LANES] = scr[c, pl.ds(r, n // d, stride=d), :].astype(dtype)


def _chunks_from_slabs(scr, src_ref, d):
    nc, n, _ = scr.shape
    for r in range(d):
        for c in range(nc):
            col = r * nc * LANES + c * LANES
            scr[c, pl.ds(r, n // d, stride=d), :] = src_ref[:, col:col + LANES]


def _attn_merge(outs, lses, ya, *, name):
    S, C = ya.shape
    tm = _tile(S, (256,))
    dils = [dil for _, dil in DILATED_GROUPS]

    def body(o0, o1, o2, l0, l1, l2, ya_ref, cat_ref, out_ref, lse_ref, so1, so2, sl1, sl2):
        _chunks_from_slabs(so1, o1, dils[1])
        _chunks_from_slabs(so2, o2, dils[2])
        _chunks_from_slabs(sl1, l1, dils[1])
        _chunks_from_slabs(sl2, l2, dils[2])
        a0, a1, a2 = l0[...], _from_chunks(sl1), _from_chunks(sl2)
        m = jnp.maximum(jnp.maximum(a0, a1), a2)
        e0, e1, e2 = jnp.exp(a0 - m), jnp.exp(a1 - m), jnp.exp(a2 - m)
        den = e0 + e1 + e2
        out = (e0 * o0[...] + e1 * _from_chunks(so1) + e2 * _from_chunks(so2)) / den
        out_ref[...] = out
        lse_ref[...] = m + jnp.log(den)
        cat_ref[:, 0:C] = ya_ref[...]
        cat_ref[:, C:C + ATTN_OUT] = out.astype(cat_ref.dtype)

    blk = pl.BlockSpec((tm, ATTN_OUT), lambda i: (i, 0))
    vblk = [pl.BlockSpec((tm // d, d * ATTN_OUT), lambda i: (i, 0)) for d in dils]
    assert dils[0] == 1
    return pl.pallas_call(
        body, name=name,
        out_shape=(jax.ShapeDtypeStruct((S, C + ATTN_OUT), BF16), jax.ShapeDtypeStruct((S, ATTN_OUT), F32),
                   jax.ShapeDtypeStruct((S, ATTN_OUT), F32)),
        grid=(S // tm,),
        in_specs=vblk + vblk + [pl.BlockSpec((tm, C), lambda i: (i, 0))],
        out_specs=(pl.BlockSpec((tm, C + ATTN_OUT), lambda i: (i, 0)), blk, blk),
        scratch_shapes=[_chunk_scratch(tm, ATTN_OUT)] * 4,
        compiler_params=_cparams("parallel"),
    )(*outs, *lses, ya)


def _attn_prep(dcat, outf, lse, *, name):
    S, CW = dcat.shape
    C = CW - ATTN_OUT
    tm = _tile(S, (256,))
    dils = [dil for _, dil in DILATED_GROUPS]
    assert dils[0] == 1
    ones = np.kron(np.eye(HEADS_PER_GROUP, dtype=np.float32), np.ones((HEAD_DIM, HEAD_DIM), np.float32))

    nviews = 3 * (len(dils) - 1)

    def body(dcat_ref, out_ref, l_ref, ones_ref, dyb_ref, dl_ref, *rest):
        views = rest[:nviews]
        s_dyb, s_dl, s_l = rest[nviews:]
        dyb = dcat_ref[:, C:CW]
        dyb_ref[...] = dyb.astype(BF16)
        prod = dyb * out_ref[...]
        ov = ones_ref[...]
        hi, mid, lo = _split_bf16(prod)
        delta = (jnp.dot(hi, ov, preferred_element_type=F32)
                 + jnp.dot(mid, ov, preferred_element_type=F32)
                 + jnp.dot(lo, ov, preferred_element_type=F32))
        dl_ref[...] = delta
        _to_chunks(s_dyb, dyb)
        _to_chunks(s_dl, delta)
        _to_chunks(s_l, l_ref[...])
        for gi, d in enumerate(dils[1:]):
            dyb_v, dl_v, l_v = views[3 * gi:3 * gi + 3]
            _slabs_from_chunks(dyb_v, s_dyb, d, BF16)
            _slabs_from_chunks(dl_v, s_dl, d, F32)
            _slabs_from_chunks(l_v, s_l, d, F32)

    blk = pl.BlockSpec((tm, ATTN_OUT), lambda i: (i, 0))
    view_shapes, view_specs = [], []
    for d in dils[1:]:
        for dt in (BF16, F32, F32):
            view_shapes.append(jax.ShapeDtypeStruct((S // d, d * ATTN_OUT), dt))
            view_specs.append(pl.BlockSpec((tm // d, d * ATTN_OUT), lambda i: (i, 0)))
    res = pl.pallas_call(
        body, name=name,
        out_shape=(jax.ShapeDtypeStruct((S, ATTN_OUT), BF16), jax.ShapeDtypeStruct((S, ATTN_OUT), F32),
                   *view_shapes),
        grid=(S // tm,),
        in_specs=[pl.BlockSpec((tm, CW), lambda i: (i, 0)), blk, blk,
                  pl.BlockSpec((ATTN_OUT, ATTN_OUT), lambda i: (0, 0))],
        out_specs=(blk, blk, *view_specs),
        scratch_shapes=[_chunk_scratch(tm, ATTN_OUT)] * 3,
        compiler_params=_cparams("parallel"),
    )(dcat, outf, lse, jnp.asarray(ones, BF16))
    return [(res[0], res[1], lse)] + [tuple(res[2 + 3 * gi:5 + 3 * gi]) for gi in range(len(dils) - 1)]


def _attn_bwd(uv, dov, lv, dv_, bias, g, d, *, name):
    rows = uv.shape[0]
    nb = rows // ATTN_BLK
    steps = d * nb
    scale = HEAD_DIM ** -0.5

    def body(q_ref, kc_ref, kp_ref, vc_ref, vp_ref, do_ref, l_ref, dl_ref, bias_ref,
             out_ref, db_ref, dq_s, dk_s, dv_s):
        t = pl.program_id(0)
        n = t % nb
        low = _lane_is_low()

        @pl.when(t == 0)
        def _():
            db_ref[...] = jnp.zeros_like(db_ref)
            dq_s[...] = jnp.zeros_like(dq_s)
            dk_s[...] = jnp.zeros_like(dk_s)
            dv_s[...] = jnp.zeros_like(dv_s)

        @pl.when(t < steps)
        def _():
            sel = jnp.minimum(n, 1)
            lane = lax.broadcasted_iota(jnp.int32, (1, LANES), 1)
            slabs = [slice(hp * LANES, (hp + 1) * LANES) for hp in range(HEADS_PER_GROUP // 2)]
            keys = [jnp.concatenate([kp_ref[:, sl], kc_ref[:, sl]], axis=0) for sl in slabs]
            scores = [_dot_nt(_stack_heads(q_ref[:, sl] * scale, low), keys[hp]) for hp, sl in enumerate(slabs)]
            dps = [_dot_nt(_stack_heads(do_ref[:, sl], low),
                           jnp.concatenate([vp_ref[:, sl], vc_ref[:, sl]], axis=0)) for sl in slabs]
            stacked = []
            for hp, sl in enumerate(slabs):
                lse2 = l_ref[:, sl]
                dl2 = dl_ref[:, sl]
                pbs, dsbs = [], []
                for hh in range(2):
                    rows = slice(hh * ATTN_BLK, (hh + 1) * ATTN_BLK)
                    one = lane == hh * HEAD_DIM
                    lse_col = jnp.sum(jnp.where(one, lse2, 0.0), axis=-1, keepdims=True)
                    dl_col = jnp.sum(jnp.where(one, dl2, 0.0), axis=-1, keepdims=True)
                    p = jnp.exp(scores[hp][rows] + bias_ref[sel, 2 * hp + hh] - lse_col)
                    ds = p * (dps[hp][rows] - dl_col)
                    db_ref[2 * hp + hh] += ds
                    pbs.append(p.astype(BF16))
                    dsbs.append((ds * scale).astype(BF16))
                stacked.append((jnp.concatenate(dsbs, axis=0), jnp.concatenate(dsbs, axis=1),
                                jnp.concatenate(pbs, axis=1)))
            nk2 = 2 * ATTN_BLK
            for hp, sl in enumerate(slabs):
                ds_rows, ds_cols, p_cols = stacked[hp]
                dq = jnp.dot(ds_rows, keys[hp], preferred_element_type=F32)
                dk = _dot_tn(ds_cols, q_ref[:, sl])
                dv = _dot_tn(p_cols, do_ref[:, sl])
                dq2 = jnp.where(low, dq[0:ATTN_BLK], dq[ATTN_BLK:nk2])
                dk2 = jnp.where(low, dk[0:nk2], dk[nk2:2 * nk2])
                dv2 = jnp.where(low, dv[0:nk2], dv[nk2:2 * nk2])
                out_ref[:, sl] = dq_s[:, sl].astype(out_ref.dtype)
                dq_s[:, sl] = dq2
                ksl = slice(ATTN_OUT + hp * LANES, ATTN_OUT + (hp + 1) * LANES)
                vsl = slice(2 * ATTN_OUT + hp * LANES, 2 * ATTN_OUT + (hp + 1) * LANES)
                out_ref[:, ksl] = (dk_s[:, sl] + dk2[0:ATTN_BLK]).astype(out_ref.dtype)
                dk_s[:, sl] = dk2[ATTN_BLK:2 * ATTN_BLK]
                out_ref[:, vsl] = (dv_s[:, sl] + dv2[0:ATTN_BLK]).astype(out_ref.dtype)
                dv_s[:, sl] = dv2[ATTN_BLK:2 * ATTN_BLK]

        @pl.when(t == steps)
        def _():
            out_ref[:, 0:ATTN_OUT] = dq_s[...].astype(out_ref.dtype)
            out_ref[:, ATTN_OUT:2 * ATTN_OUT] = dk_s[...].astype(out_ref.dtype)
            out_ref[:, 2 * ATTN_OUT:GROUP_QKV] = dv_s[...].astype(out_ref.dtype)

    rowblk = pl.BlockSpec((ATTN_BLK, ATTN_OUT), lambda r, n: (jnp.minimum(n, nb - 1), r))

    def at_step(spec):
        def index_map(t):
            tt = jnp.minimum(t, steps - 1)
            return spec.index_map(tt // nb, tt % nb)
        return pl.BlockSpec(spec.block_shape, index_map)

    return pl.pallas_call(
        body, name=name,
        out_shape=(jax.ShapeDtypeStruct((rows, d * GROUP_QKV), BF16),
                   jax.ShapeDtypeStruct((HEADS_PER_GROUP, ATTN_BLK, 2 * ATTN_BLK), F32)),
        grid=(steps + 1,),
        in_specs=[at_step(sp) for sp in _qkv_specs(nb) + [rowblk, rowblk, rowblk]] + [
            pl.BlockSpec((None, 2, HEADS_PER_GROUP, ATTN_BLK, 2 * ATTN_BLK), lambda t: (g, 0, 0, 0, 0))],
        out_specs=(pl.BlockSpec((ATTN_BLK, GROUP_QKV),
                                lambda t: (jnp.maximum(t - 1, 0) % nb, jnp.maximum(t - 1, 0) // nb)),
                   pl.BlockSpec((HEADS_PER_GROUP, ATTN_BLK, 2 * ATTN_BLK), lambda t: (0, 0, 0))),
        scratch_shapes=[pltpu.VMEM((ATTN_BLK, ATTN_OUT), F32)] * 3,
        compiler_params=_cparams("arbitrary"),
    )(uv, uv, uv, uv, uv, dov, lv, dv_, bias)


def _split_bf16(x):
    hi = x.astype(BF16)
    r1 = x - hi.astype(F32)
    mid = r1.astype(BF16)
    lo = (r1 - mid.astype(F32)).astype(BF16)
    return hi, mid, lo


RELBIAS_CHUNK = 4096


def _relbias_reduce(dbs, buckets, *, name):
    flat = ATTN_BLK * 2 * ATTN_BLK
    dbf = jnp.stack([db.reshape(HEADS_PER_GROUP, flat) for db in dbs])
    bkf = buckets.reshape(N_GROUPS, 1, flat)

    def body(db_ref, bk_ref, o_ref):
        c = pl.program_id(1)
        rows = lax.broadcasted_iota(jnp.int32, (LANES, RELBIAS_CHUNK), 0).astype(F32)
        onehot = jnp.where(rows == bk_ref[0], 1.0, 0.0).astype(BF16)
        hi, mid, lo = _split_bf16(db_ref[0])
        part = _dot_nt(hi, onehot) + _dot_nt(mid, onehot) + _dot_nt(lo, onehot)

        @pl.when(c == 0)
        def _():
            o_ref[0] = part

        @pl.when(c > 0)
        def _():
            o_ref[0] += part

    return pl.pallas_call(
        body, name=name,
        out_shape=jax.ShapeDtypeStruct((N_GROUPS, HEADS_PER_GROUP, LANES), F32),
        grid=(N_GROUPS, flat // RELBIAS_CHUNK),
        in_specs=[pl.BlockSpec((1, HEADS_PER_GROUP, RELBIAS_CHUNK), lambda g, c: (g, 0, c)),
                  pl.BlockSpec((1, 1, RELBIAS_CHUNK), lambda g, c: (g, 0, c))],
        out_specs=pl.BlockSpec((1, HEADS_PER_GROUP, LANES), lambda g, c: (g, 0, 0)),
        compiler_params=_cparams("parallel", "arbitrary"),
    )(dbf, bkf)


def _my_position():
    x, y, c = lax.axis_index("x"), lax.axis_index("y"), lax.axis_index("c")
    return x, y, c


def _linear(pos):
    return 4 * pos[0] + 2 * pos[1] + pos[2]


def _peer(pos, k):
    x, y, c = pos
    return ((1 - x) if k & 4 else x, (1 - y) if k & 2 else y, (1 - c) if k & 1 else c)


HBM_SPEC = pl.BlockSpec(memory_space=pltpu.HBM)
SEM_SPEC = pl.BlockSpec(memory_space=pltpu.SEMAPHORE)
DATAFLOW = pltpu.SideEffectType.DATAFLOW_SIDE_EFFECTING


def _exchange_copies(src, land, sems, send_window, recv_window, with_arrivals):
    send_sems, recv_sems, local_sems = sems
    T = len(src)
    me = _my_position()
    me_lin = _linear(me)
    local = [pltpu.make_async_copy(send_window(t, src[t], me_lin), recv_window(t, land[t], me_lin),
                                   local_sems.at[t]) for t in range(T)]
    sends, arrivals = [], []
    for t in range(T):
        for k in range(1, N_DEV):
            peer = _peer(me, k)
            peer_lin = _linear(peer)
            sem = t * (N_DEV - 1) + k - 1
            sends.append(pltpu.make_async_remote_copy(
                src_ref=send_window(t, src[t], peer_lin), dst_ref=recv_window(t, land[t], me_lin),
                send_sem=send_sems.at[sem], recv_sem=recv_sems.at[sem],
                device_id=peer, device_id_type=MESH))
            if with_arrivals:
                arrivals.append(pltpu.make_async_remote_copy(
                    src_ref=send_window(t, src[t], me_lin), dst_ref=recv_window(t, land[t], peer_lin),
                    send_sem=send_sems.at[sem], recv_sem=recv_sems.at[sem],
                    device_id=peer, device_id_type=MESH))
    return local, sends, arrivals


def _exchange_start(srcs, land_shapes, send_window, recv_window, *, name, dep=None):
    T = len(srcs)
    n_in = 2 * T + (1 if dep is not None else 0)

    def body(*refs):
        src = refs[:T]
        land = refs[T:2 * T]
        sems = refs[n_in:n_in + 3]
        token = refs[-1]
        local, sends, _ = _exchange_copies(src, land, sems, send_window, recv_window, False)
        for cp in local + sends:
            cp.start()
        token[...] = jnp.zeros_like(token)

    lands = [lax.empty(ls.shape, ls.dtype) for ls in land_shapes]
    operands = [pltpu.with_memory_space_constraint(a, pltpu.HBM) for a in list(srcs) + lands]
    outs = pl.pallas_call(
        body, name=name,
        out_shape=(pltpu.SemaphoreType.DMA((T * (N_DEV - 1),)), pltpu.SemaphoreType.DMA((T * (N_DEV - 1),)),
                   pltpu.SemaphoreType.DMA((T,)),
                   *[pltpu.HBM(a.shape, a.dtype) for a in operands],
                   jax.ShapeDtypeStruct((8, LANES), F32)),
        in_specs=[HBM_SPEC] * (2 * T) + ([ANY] if dep is not None else []),
        out_specs=(SEM_SPEC,) * 3 + (HBM_SPEC,) * (2 * T) + (VMEM_SPEC,),
        input_output_aliases={i: 3 + i for i in range(2 * T)},
        compiler_params=pltpu.CompilerParams(has_side_effects=DATAFLOW),
    )(*operands, *([dep] if dep is not None else []))
    return outs[:3], outs[3:3 + T], outs[3 + T:3 + 2 * T], outs[-1]


def _exchange_wait(started, after, send_window, recv_window, *, name):
    sems, srcs, lands, _ = started
    T = len(srcs)

    def body(*refs):
        src = refs[:T]
        land = refs[T:2 * T]
        sem_refs = refs[2 * T:2 * T + 3]
        local, sends, arrivals = _exchange_copies(src, land, sem_refs, send_window, recv_window, True)
        for cp in arrivals:
            cp.wait_recv()
        for cp in sends:
            cp.wait_send()
        for cp in local:
            cp.wait()

    outs = pl.pallas_call(
        body, name=name,
        out_shape=tuple(pltpu.HBM(a.shape, a.dtype) for a in list(srcs) + list(lands)),
        in_specs=[HBM_SPEC] * (2 * T) + [SEM_SPEC] * 3 + [ANY],
        out_specs=(HBM_SPEC,) * (2 * T),
        input_output_aliases={i: i for i in range(2 * T)},
        compiler_params=pltpu.CompilerParams(has_side_effects=DATAFLOW),
    )(*srcs, *lands, *sems, after)
    return outs[T:]


def _shard_window(kind, width):
    def win(ref, lin):
        if kind == "slot":
            return ref.at[lin]
        if kind == "col":
            return ref.at[:, pl.ds(pl.multiple_of(lin * width, LANES), width)]
        if kind == "row":
            return ref.at[pl.ds(pl.multiple_of(lin * width, 8), width), :]
        if kind == "lcol":
            return ref.at[:, :, pl.ds(pl.multiple_of(lin * width, LANES), width)]
        if kind == "lrow":
            return ref.at[:, pl.ds(pl.multiple_of(lin * width, 8), width), :]
        raise ValueError(kind)
    return win


def _shard_windows(kinds, shard_shapes):
    return [_shard_window(k, (ss[-1] if k in ("col", "lcol") else ss[-2])) for k, ss in zip(kinds, shard_shapes)]


def _allgather_start(shards, kinds, full_shapes, *, name, dep=None):
    wins = _shard_windows(kinds, [s.shape for s in shards])
    send_window = lambda t, ref, lin: ref
    recv_window = lambda t, ref, lin: wins[t](ref, lin)
    started = _exchange_start(shards, [jax.ShapeDtypeStruct(fs, s.dtype) for fs, s in zip(full_shapes, shards)],
                              send_window, recv_window, name=name + "_start", dep=dep)
    return started, lambda after: _exchange_wait(started, after, send_window, recv_window, name=name + "_wait")


def _scatter_start(fulls, kinds, shard_shapes, *, name):
    wins = _shard_windows(kinds, shard_shapes)
    send_window = lambda t, ref, lin: wins[t](ref, lin)
    recv_window = lambda t, ref, lin: ref.at[lin]
    started = _exchange_start(
        fulls, [jax.ShapeDtypeStruct((N_DEV,) + tuple(ss), f.dtype) for ss, f in zip(shard_shapes, fulls)],
        send_window, recv_window, name=name + "_start")
    return started, lambda after: _exchange_wait(started, after, send_window, recv_window, name=name + "_wait")


def _small_gather(pack, *, reduce, name):
    R = pack.shape[0]

    def body(p_ref, o_ref, *rest):
        if reduce:
            buf, send_sems, recv_sems = rest
        else:
            buf = o_ref
            send_sems, recv_sems = rest
        me = _my_position()
        me_lin = _linear(me)
        buf[me_lin] = p_ref[...]
        sends = []
        for k in range(1, N_DEV):
            peer = _peer(me, k)
            cp = pltpu.make_async_remote_copy(
                src_ref=p_ref, dst_ref=buf.at[me_lin],
                send_sem=send_sems.at[k - 1], recv_sem=recv_sems.at[k - 1],
                device_id=peer, device_id_type=MESH)
            cp.start()
            sends.append(cp)
        for k in range(1, N_DEV):
            peer = _peer(me, k)
            pltpu.make_async_remote_copy(
                src_ref=p_ref, dst_ref=buf.at[_linear(peer)],
                send_sem=send_sems.at[k - 1], recv_sem=recv_sems.at[k - 1],
                device_id=peer, device_id_type=MESH).wait_recv()
        for cp in sends:
            cp.wait_send()
        if reduce:
            acc = buf[0]
            for s in range(1, N_DEV):
                acc = acc + buf[s]
            o_ref[...] = acc

    scratch = [pltpu.SemaphoreType.DMA((N_DEV - 1,)), pltpu.SemaphoreType.DMA((N_DEV - 1,))]
    if reduce:
        scratch = [pltpu.VMEM((N_DEV, R, LANES), F32)] + scratch
        out_shape = jax.ShapeDtypeStruct((R, LANES), F32)
    else:
        out_shape = jax.ShapeDtypeStruct((N_DEV, R, LANES), F32)
    return pl.pallas_call(
        body, name=name, out_shape=out_shape,
        in_specs=[VMEM_SPEC], out_specs=VMEM_SPEC, scratch_shapes=scratch,
        compiler_params=pltpu.CompilerParams(has_side_effects=True, vmem_limit_bytes=VMEM_LIMIT),
    )(pack)


def _adamw_math(w, g, m, v):
    m = ADAM_B1 * m + (1.0 - ADAM_B1) * g
    v = ADAM_B2 * v + (1.0 - ADAM_B2) * jnp.square(g)
    m_hat = m / (1.0 - ADAM_B1 ** ADAM_STEP)
    v_hat = v / (1.0 - ADAM_B2 ** ADAM_STEP)
    delta = -ADAM_LR * (m_hat / (jnp.sqrt(v_hat) + ADAM_EPS) + ADAM_WD * w)
    return delta, m, v


def _adamw_from_partials(parts, w, m, v, *, name):
    R, C = w.shape
    rl = parts[0].shape[1]
    assert all(p.shape == (N_DEV, rl, C) for p in parts) and rl * len(parts) == R
    tr = _tile(rl, (256, 128, 64, 32, 16))
    per = rl // tr
    L = len(parts)

    def body(*refs):
        p_refs = refs[:L]
        w_ref, m_ref, v_ref, g_ref, d_ref, nm_ref, nv_ref = refs[L:]
        i = pl.program_id(0)
        for l in range(L):
            @pl.when((i >= l * per) & (i < (l + 1) * per))
            def _(l=l):
                p_ref = p_refs[l]
                g = p_ref[0].astype(F32)
                for s in range(1, N_DEV):
                    g = g + p_ref[s].astype(F32)
                d, nm, nv = _adamw_math(w_ref[...], g, m_ref[...], v_ref[...])
                g_ref[...] = g
                d_ref[...] = d
                nm_ref[...] = nm
                nv_ref[...] = nv

    blk = pl.BlockSpec((tr, C), lambda i: (i, 0))
    part_specs = [pl.BlockSpec((N_DEV, tr, C), lambda i, l=l: (0, jnp.clip(i - l * per, 0, per - 1), 0))
                  for l in range(L)]
    return pl.pallas_call(
        body, name=name,
        out_shape=(jax.ShapeDtypeStruct((R, C), F32),) * 4,
        grid=(R // tr,),
        in_specs=part_specs + [blk, blk, blk],
        out_specs=(blk,) * 4,
        compiler_params=_cparams("parallel"),
    )(*parts, w, m, v)


def _adamw_small(g, w, m, v, *, name):
    def body(g_ref, w_ref, m_ref, v_ref, d_ref, nm_ref, nv_ref):
        d, nm, nv = _adamw_math(w_ref[...], g_ref[...], m_ref[...], v_ref[...])
        d_ref[...] = d
        nm_ref[...] = nm
        nv_ref[...] = nv

    return pl.pallas_call(
        body, name=name,
        out_shape=(jax.ShapeDtypeStruct(g.shape, F32),) * 3,
        in_specs=[VMEM_SPEC] * 4, out_specs=(VMEM_SPEC,) * 3,
    )(g, w, m, v)


def _pack_rows(pieces):
    flat = jnp.concatenate([p.reshape(-1) for p in pieces])
    n = flat.shape[0]
    padded = -(-n // (8 * LANES)) * (8 * LANES)
    return jnp.pad(flat, (0, padded - n)).reshape(padded // LANES, LANES)


def _unpack_rows(pack, shapes):
    flat = pack.reshape(-1)
    out, pos = [], 0
    for s in shapes:
        n = int(np.prod(s))
        out.append(flat[pos:pos + n].reshape(s))
        pos += n
    return out


def kernel(x, rel_bias, ab_norm, ab_w_in, ab_conv_w, ab_conv_b, ab_ln_g, ab_ln_b, ab_w_out, sc_norm, sc_w_in, sc_conv_w, sc_w_out, mlp_norm, mlp_w_up, mlp_w_down, final_norm, loss_target, m_rel_bias, m_ab_norm, m_ab_w_in, m_ab_conv_w, m_ab_conv_b, m_ab_ln_g, m_ab_ln_b, m_ab_w_out, m_sc_norm, m_sc_w_in, m_sc_conv_w, m_sc_w_out, m_mlp_norm, m_mlp_w_up, m_mlp_w_down, m_final_norm, v_rel_bias, v_ab_norm, v_ab_w_in, v_ab_conv_w, v_ab_conv_b, v_ab_ln_g, v_ab_ln_b, v_ab_w_out, v_sc_norm, v_sc_w_in, v_sc_conv_w, v_sc_w_out, v_mlp_norm, v_mlp_w_up, v_mlp_w_down, v_final_norm):
    S, D = x.shape[1], x.shape[2]
    CA = ab_conv_b.shape[1]
    C2 = 2 * CA
    AB_IN = C2 + ATTN_IN
    me_lin = _linear(_my_position())
    xs = x.reshape(S, D)
    tgt = loss_target.reshape(S, D)

    cw_sh = ab_conv_w.shape[2]
    scn_sh = sc_norm.shape[1]
    scw_sh = sc_conv_w.shape[2]
    small_sh_shapes = [(CONV_A_WIDTH, cw_sh), (scn_sh,), (SC_CONV_WIDTH, scw_sh)]
    small_pack = _pack_rows([ab_conv_w[0], sc_norm[0], sc_conv_w[0]])
    w_in_sh = ab_w_in[0].astype(BF16)
    ag_ab, wait_ab = _allgather_start(
        [small_pack, w_in_sh, ab_w_out[0].astype(BF16)], ["slot", "slot", "row"],
        [(N_DEV,) + small_pack.shape, (N_DEV,) + w_in_sh.shape, (N_DEV * ab_w_out.shape[1], D)],
        name="allgather_ab")
    ag_mlp, wait_mlp = _allgather_start(
        [mlp_w_up.astype(BF16), mlp_w_down.astype(BF16)], ["lcol", "lrow"],
        [(2, D, N_DEV * mlp_w_up.shape[2]), (2, N_DEV * mlp_w_down.shape[1], D)], name="allgather_mlp",
        dep=ag_ab[3])
    ag_sc, wait_sc = _allgather_start(
        [sc_w_in[0].astype(BF16), sc_w_out[0].astype(BF16)], ["col", "row"],
        [(D, N_DEV * sc_w_in.shape[2]), (N_DEV * sc_w_out.shape[1], D)], name="allgather_sc",
        dep=ag_mlp[3])

    buckets = _bucket_tables()
    biases = _bias_tables(rel_bias, buckets, name="bias_tables")

    dils = [dil for _, dil in DILATED_GROUPS]
    n0_all = _rmsnorm_fwd(xs, ab_norm, name="norm_ab", dep=ag_sc[3], views=dils[1:])
    n0 = n0_all[0]
    small_params, w_in_g, w_out = wait_ab(n0)
    per_dev = [_unpack_rows(small_params[s], small_sh_shapes) for s in range(N_DEV)]
    conv_w_full = jnp.concatenate([p[0] for p in per_dev], axis=1)
    sc_norm_full = jnp.concatenate([p[1] for p in per_dev], axis=0)[None]
    sc_conv_full = jnp.concatenate([p[2] for p in per_dev], axis=1)
    w_in = jnp.transpose(w_in_g, (1, 0, 2)).reshape(D, AB_IN)
    w_c = w_in[:, :C2]
    w_q = w_in[:, C2:]
    w_grp = [jnp.concatenate([w_q[:, t * N_GROUPS * ATTN_OUT + g * ATTN_OUT:][:, :ATTN_OUT] for t in range(3)], axis=1)
             for g in range(N_GROUPS)]
    uc = _mm_nn(n0, w_c, out_dtype=F32, name="mm_ab_in_conv", wide=True)
    uqs = [_mm_nn(n0_all[g], w_grp[g], out_dtype=BF16, slabs=dils[g], name=f"mm_ab_in_qkv{g}")
           for g in range(N_GROUPS)]
    ya, hglu, ct = _conv_fwd(uc, conv_w_full, ab_conv_b, ab_ln_g, ab_ln_b, name="conv_fwd")
    outs, lses = zip(*[_attn_fwd(uqs[g], biases, g, dils[g], name=f"attn_fwd_{g}") for g in range(N_GROUPS)])
    cat, outf, lse = _attn_merge(outs, lses, ya, name="attn_merge")
    h1 = _mm_nn(cat, w_out, out_dtype=F32, residual=xs, name="mm_ab_out", wide=True)
    w_up, w_dn = wait_mlp(h1)
    n1, z0 = _norm_mm_nn(h1, mlp_norm[0:1], (w_up, 0), out_dtype=BF16, name="norm_mm_up0", wide=True)
    h2 = _mm_nn(z0, (w_dn, 0), out_dtype=F32, residual=h1, a_fn=_relu_sq, name="mm_down0", wide=True)
    w_sc_in, w_sc_out = wait_sc(h2)
    n2, u3 = _norm_mm_nn(h2, sc_norm_full, w_sc_in, out_dtype=BF16, name="norm_mm_sc_in", wide=True)
    ysc = _sc_fwd(u3, sc_conv_full, name="sc_fwd")
    h3 = _mm_nn(ysc, w_sc_out, out_dtype=F32, residual=h2, name="mm_sc_out", wide=True)
    n3, z1 = _norm_mm_nn(h3, mlp_norm[1:2], (w_up, 1), out_dtype=BF16, name="norm_mm_up1", wide=True)
    h4 = _mm_nn(z1, (w_dn, 1), out_dtype=F32, residual=h3, a_fn=_relu_sq, name="mm_down1", wide=True)

    def dz_epilogue(acc, z):
        return acc * (2.0 * jnp.maximum(z.astype(F32), 0.0))

    dh4, dh4b, acc_final = _loss_bwd(h4, tgt, final_norm[None], name="loss_bwd")
    dz1 = _mm_nt([(dh4b, (w_dn, 1))], out_dtype=BF16, epilogue=dz_epilogue, extra=z1, name="mm_d_down1")
    g_dn1 = _mm_tn(z1, dh4b, a_fn=_relu_sq, name="mm_gw_down1")
    g_up1 = _mm_tn(n3, dz1, name="mm_gw_up1")
    rs_mlp1, wait_rs_mlp1 = _scatter_start([g_up1, g_dn1], ["col", "row"],
                                           [mlp_w_up.shape[1:], mlp_w_down.shape[1:]], name="scatter_mlp1")
    dh3, dh3b, acc_mlp1 = _mm_nt_rms_bwd(dz1, (w_up, 1), h3, mlp_norm[1:2], dh4, name="mm_d_up1_norm_bwd",
                                         dep=rs_mlp1[3])

    dysc = _mm_nt([(dh3b, w_sc_out)], out_dtype=F32, name="mm_d_sc_out")
    g_sc_out = _mm_tn(ysc, dh3b, name="mm_gw_sc_out")
    du3, acc_scw = _sc_bwd(u3, dysc, sc_conv_full, name="sc_bwd")
    g_sc_in = _mm_tn(n2, du3, name="mm_gw_sc_in")
    rs_sc, wait_rs_sc = _scatter_start([g_sc_in, g_sc_out], ["col", "row"],
                                       [sc_w_in.shape[1:], sc_w_out.shape[1:]], name="scatter_sc")
    dh2, dh2b, acc_sc = _mm_nt_rms_bwd(du3, w_sc_in, h2, sc_norm_full, dh3, name="mm_d_sc_in_norm_bwd",
                                       dep=rs_sc[3])

    dz0 = _mm_nt([(dh2b, (w_dn, 0))], out_dtype=BF16, epilogue=dz_epilogue, extra=z0, name="mm_d_down0")
    g_dn0 = _mm_tn(z0, dh2b, a_fn=_relu_sq, name="mm_gw_down0")
    g_up0 = _mm_tn(n1, dz0, name="mm_gw_up0")
    rs_mlp0, wait_rs_mlp0 = _scatter_start([g_up0, g_dn0], ["col", "row"],
                                           [mlp_w_up.shape[1:], mlp_w_down.shape[1:]], name="scatter_mlp0")
    dh1, dh1b, acc_mlp0 = _mm_nt_rms_bwd(dz0, (w_up, 0), h1, mlp_norm[0:1], dh2, name="mm_d_up0_norm_bwd",
                                         dep=rs_mlp0[3])

    dcat = _mm_nt([(dh1b, w_out)], out_dtype=F32, name="mm_d_ab_out")
    g_ab_out = _mm_tn(cat, dh1b, name="mm_gw_ab_out")
    prep = _attn_prep(dcat, outf, lse, name="attn_prep")
    dqkv, dbs = zip(*[_attn_bwd(uqs[g], prep[g][0], prep[g][2], prep[g][1], biases, g, dils[g],
                                name=f"attn_bwd_{g}") for g in range(N_GROUPS)])
    drel = _relbias_reduce(dbs, buckets, name="relbias_reduce")
    dc, acc_conv = _conv_bwd_ln(ct, dcat, hglu, ab_ln_g, ab_ln_b, name="conv_bwd_ln")
    duc = _conv_bwd_in(dc, uc, conv_w_full, name="conv_bwd_in")
    g_wc = _mm_tn(n0, duc, name="mm_gw_ab_in_conv")
    g_wgrp = [_mm_tn(n0_all[g], dqkv[g], slabs=dils[g], name=f"mm_gw_ab_in_qkv{g}") for g in range(N_GROUPS)]
    g_wq = jnp.concatenate([g_wgrp[g][:, t * ATTN_OUT:(t + 1) * ATTN_OUT]
                            for t in range(3) for g in range(N_GROUPS)], axis=1)
    g_w_in = jnp.concatenate([g_wc, g_wq], axis=1).reshape(D, N_DEV, AB_IN // N_DEV).transpose(1, 0, 2)
    rs_ab, wait_rs_ab = _scatter_start([g_w_in, g_ab_out], ["slot", "row"],
                                       [w_in_sh.shape, ab_w_out.shape[1:]], name="scatter_ab")
    dn0 = _mm_nt([(duc, w_c), (dqkv[0], w_grp[0])], out_dtype=F32, name="mm_d_ab_in", dep=rs_ab[3])
    dn0_views = [(_mm_nt([(dqkv[g], w_grp[g])], out_dtype=F32, slabs=dils[g], name=f"mm_d_ab_in_qkv{g}"), dils[g])
                 for g in range(1, N_GROUPS)]
    grad_x, grad_xb, acc_ab = _rms_bwd(xs, ab_norm, dn0, dh1, name="norm_ab_bwd", dn_views=dn0_views)

    small_full = [drel[:, :, :NUM_BUCKETS].transpose(2, 0, 1).reshape(NUM_BUCKETS, N_GROUPS * HEADS_PER_GROUP),
                  acc_ab[0], acc_conv[0:CONV_A_WIDTH], acc_conv[32],
                  acc_conv[33], acc_conv[34], acc_sc[0], acc_scw[0:SC_CONV_WIDTH],
                  jnp.stack([acc_mlp0[0], acc_mlp1[0]]), acc_final[0], acc_final[1]]
    small_full_shapes = [p.shape for p in small_full]
    summed = _unpack_rows(_small_gather(_pack_rows(small_full), reduce=True, name="allreduce_small"),
                          small_full_shapes)
    (s_rel, s_abn, s_cw, s_cb, s_lg, s_lb, s_scn, s_scw, s_mlpn, s_fn, s_err) = summed
    loss = (0.5 / D) * jnp.sum(s_err)
    small_grads = {
        "rel_bias": s_rel, "ab_norm": s_abn[None],
        "ab_conv_w": lax.dynamic_slice_in_dim(s_cw, me_lin * cw_sh, cw_sh, axis=1)[None],
        "ab_conv_b": s_cb[None], "ab_ln_g": s_lg[None], "ab_ln_b": s_lb[None],
        "sc_norm": lax.dynamic_slice_in_dim(s_scn, me_lin * scn_sh, scn_sh, axis=0)[None],
        "sc_conv_w": lax.dynamic_slice_in_dim(s_scw, me_lin * scw_sh, scw_sh, axis=1)[None],
        "mlp_norm": s_mlpn, "final_norm": s_fn,
    }
    small_w = {"rel_bias": (rel_bias, m_rel_bias, v_rel_bias), "ab_norm": (ab_norm, m_ab_norm, v_ab_norm),
               "ab_conv_w": (ab_conv_w, m_ab_conv_w, v_ab_conv_w), "ab_conv_b": (ab_conv_b, m_ab_conv_b, v_ab_conv_b),
               "ab_ln_g": (ab_ln_g, m_ab_ln_g, v_ab_ln_g), "ab_ln_b": (ab_ln_b, m_ab_ln_b, v_ab_ln_b),
               "sc_norm": (sc_norm, m_sc_norm, v_sc_norm), "sc_conv_w": (sc_conv_w, m_sc_conv_w, v_sc_conv_w),
               "mlp_norm": (mlp_norm, m_mlp_norm, v_mlp_norm), "final_norm": (final_norm, m_final_norm, v_final_norm)}
    small_names = list(small_grads)
    small_shapes = [small_grads[n].shape for n in small_names]
    d_pack, m_pack, v_pack = _adamw_small(
        _pack_rows([small_grads[n] for n in small_names]), _pack_rows([small_w[n][0] for n in small_names]),
        _pack_rows([small_w[n][1] for n in small_names]), _pack_rows([small_w[n][2] for n in small_names]),
        name="adamw_small")
    small = {n: (small_grads[n], d, nm_, nv_) for n, d, nm_, nv_ in zip(
        small_names, _unpack_rows(d_pack, small_shapes), _unpack_rows(m_pack, small_shapes),
        _unpack_rows(v_pack, small_shapes))}

    p_up1, p_dn1 = wait_rs_mlp1(grad_xb)
    p_sc_in, p_sc_out = wait_rs_sc(grad_xb)
    p_up0, p_dn0 = wait_rs_mlp0(grad_xb)
    p_w_in, p_ab_out = wait_rs_ab(grad_xb)
    big = {}
    for nm, parts, w, m, v in (("ab_w_in", [p_w_in], ab_w_in, m_ab_w_in, v_ab_w_in),
                               ("ab_w_out", [p_ab_out], ab_w_out, m_ab_w_out, v_ab_w_out),
                               ("sc_w_in", [p_sc_in], sc_w_in, m_sc_w_in, v_sc_w_in),
                               ("sc_w_out", [p_sc_out], sc_w_out, m_sc_w_out, v_sc_w_out),
                               ("mlp_w_up", [p_up0, p_up1], mlp_w_up, m_mlp_w_up, v_mlp_w_up),
                               ("mlp_w_down", [p_dn0, p_dn1], mlp_w_down, m_mlp_w_down, v_mlp_w_down)):
        C = w.shape[-1]
        res = _adamw_from_partials(parts, w.reshape(-1, C), m.reshape(-1, C), v.reshape(-1, C), name="adamw_" + nm)
        big[nm] = tuple(r.reshape(w.shape) for r in res)

    order = ["rel_bias", "ab_norm", "ab_w_in", "ab_conv_w", "ab_conv_b", "ab_ln_g", "ab_ln_b", "ab_w_out",
             "sc_norm", "sc_w_in", "sc_conv_w", "sc_w_out", "mlp_norm", "mlp_w_up", "mlp_w_down", "final_norm"]
    allres = {**big, **small}
    return (loss, grad_x.reshape(x.shape),
            *[allres[n][0] for n in order], *[allres[n][1] for n in order],
            *[allres[n][2] for n in order], *[allres[n][3] for n in order])
```

```python
import functools
import math

import numpy as np
import jax
import jax.numpy as jnp
from jax import lax
from jax.experimental import pallas as pl
from jax.experimental.pallas import tpu as pltpu

F32 = jnp.float32
BF16 = jnp.bfloat16

HEAD_DIM = 64
HEADS_PER_GROUP = 8
DILATED_GROUPS = ((128, 1), (512, 4), (2048, 16))
N_GROUPS = 3
ATTN_OUT = HEADS_PER_GROUP * HEAD_DIM
ATTN_IN = 3 * N_GROUPS * ATTN_OUT
GROUP_QKV = 3 * ATTN_OUT
ATTN_BLK = 128
CONV_A_WIDTH = 31
SC_CONV_WIDTH = 3
NUM_BUCKETS = 32
REL_MAX_DISTANCE = 2048
RMS_EPS = 1e-6
LN_EPS = 1e-5
NEG_INF = -1e30
ADAM_LR = 0.001
ADAM_B1 = 0.9
ADAM_B2 = 0.999
ADAM_EPS = 1e-08
ADAM_WD = 0.01
ADAM_STEP = 10

N_DEV = 8
HALO = 32
LANES = 128
VMEM_LIMIT = 56 * 1024 * 1024
MESH = pl.DeviceIdType.MESH
ANY = pl.BlockSpec(memory_space=pl.ANY)
VMEM_SPEC = pl.BlockSpec(memory_space=pltpu.VMEM)


def _tile(n, prefs):
    for t in prefs:
        if n % t == 0:
            return t
    return n


def _cparams(*sem):
    return pltpu.CompilerParams(dimension_semantics=sem, vmem_limit_bytes=VMEM_LIMIT)


def _relu_sq(z):
    return jnp.square(jnp.maximum(z, 0))


def _dot_nt(a, b):
    return lax.dot_general(a, b, (((1,), (1,)), ((), ())), preferred_element_type=F32)


def _dot_tn(a, b):
    return lax.dot_general(a, b, (((0,), (0,)), ((), ())), preferred_element_type=F32)


def _weight(b):
    if not isinstance(b, tuple):
        return b, b.shape, pl.BlockSpec
    arr, layer = b

    def spec(block, index_map):
        return pl.BlockSpec((None,) + tuple(block), lambda *g: (layer,) + tuple(index_map(*g)))

    return arr, arr.shape[1:], spec


def _mm_nn(a, b, *, out_dtype, name, residual=None, a_fn=None, slabs=1, wide=False):
    M, K = a.shape
    K //= slabs
    b, (_, N), b_spec = _weight(b)
    tm = _tile(M, (1024, 512, 256) if wide else (2048, 1024, 512, 256))
    tn = _tile(N, (1024, 512, 384, 256, 128) if wide else (512, 384, 256, 128))
    tk = _tile(K, (2048, 1024, 512, 256, 128) if wide else (1024, 512, 256, 128))
    nk = K // tk
    nj = N // tn
    has_res = residual is not None

    def body(*refs):
        if has_res:
            a_ref, b_ref, r_ref, o_ref = refs[:4]
        else:
            a_ref, b_ref, o_ref = refs[:3]
        av = a_ref[...]
        if a_fn is not None:
            av = a_fn(av)
        part = jnp.dot(av, b_ref[...], preferred_element_type=F32)

        def finish(acc):
            if has_res:
                acc = acc + r_ref[...]
            o_ref[...] = acc.astype(o_ref.dtype)

        if nk == 1:
            finish(part)
        else:
            acc_ref = refs[-1]
            k = pl.program_id(2)

            @pl.when(k == 0)
            def _():
                acc_ref[...] = part

            @pl.when((k > 0) & (k < nk - 1))
            def _():
                acc_ref[...] += part

            @pl.when(k == nk - 1)
            def _():
                finish(acc_ref[...] + part)

    in_specs = [pl.BlockSpec((tm, tk), lambda i, j, k: (i, (j // nj) * nk + k)),
                b_spec((tk, tn), lambda i, j, k: (k, j % nj))]
    args = [a, b]
    if has_res:
        in_specs.append(pl.BlockSpec((tm, tn), lambda i, j, k: (i, j)))
        args.append(residual)
    return pl.pallas_call(
        body, name=name,
        out_shape=jax.ShapeDtypeStruct((M, slabs * N), out_dtype),
        grid=(M // tm, slabs * nj, nk),
        in_specs=in_specs,
        out_specs=pl.BlockSpec((tm, tn), lambda i, j, k: (i, j)),
        scratch_shapes=[pltpu.VMEM((tm, tn), F32)] if nk > 1 else [],
        compiler_params=_cparams("parallel", "parallel", "arbitrary"),
    )(*args)


def _norm_mm_nn(h, g, b, *, out_dtype, name, wide=False):
    M, K = h.shape
    b, (_, N), b_spec = _weight(b)
    tm = _tile(M, (2048, 1024, 512, 256))
    tn = _tile(N, (1024, 512, 384, 256, 128) if wide else (512, 384, 256, 128))

    def body(h_ref, g_ref, b_ref, n_ref, o_ref):
        @pl.when(pl.program_id(1) == 0)
        def _():
            x = h_ref[...]
            r = lax.rsqrt(jnp.mean(x * x, axis=-1, keepdims=True) + RMS_EPS)
            n_ref[...] = (x * r * g_ref[...]).astype(BF16)

        o_ref[...] = jnp.dot(n_ref[...], b_ref[...], preferred_element_type=F32).astype(o_ref.dtype)

    return pl.pallas_call(
        body, name=name,
        out_shape=(jax.ShapeDtypeStruct((M, K), BF16), jax.ShapeDtypeStruct((M, N), out_dtype)),
        grid=(M // tm, N // tn),
        in_specs=[pl.BlockSpec((tm, K), lambda i, j: (i, 0)), pl.BlockSpec((1, K), lambda i, j: (0, 0)),
                  b_spec((K, tn), lambda i, j: (0, j))],
        out_specs=(pl.BlockSpec((tm, K), lambda i, j: (i, 0)), pl.BlockSpec((tm, tn), lambda i, j: (i, j))),
        compiler_params=_cparams("parallel", "arbitrary"),
    )(h, g, b)


def _mm_nt(pairs, *, out_dtype, name, epilogue=None, extra=None, dep=None, slabs=1):
    assert slabs == 1 or len(pairs) == 1
    M = pairs[0][0].shape[0]
    weights = [_weight(p[1]) for p in pairs]
    Ko = weights[0][1][0]
    tm = _tile(M, (2048, 1024, 512, 256))
    to = _tile(Ko, (1024, 512, 256, 128))
    njo = Ko // to
    tks = [_tile(p[0].shape[1] // slabs, (1024, 768, 512, 256, 128)) for p in pairs]
    steps = [p[0].shape[1] // slabs // tk for p, tk in zip(pairs, tks)]
    offs = [sum(steps[:i]) for i in range(len(pairs))]
    nk = sum(steps)
    npair = len(pairs)
    has_extra = extra is not None

    def body(*refs):
        ab = refs[:2 * npair]
        pos = 2 * npair
        e_ref = None
        if has_extra:
            e_ref = refs[pos]
            pos += 1
        if dep is not None:
            pos += 1
        o_ref = refs[pos]
        acc_ref = refs[pos + 1]
        k = pl.program_id(2)

        @pl.when(k == 0)
        def _():
            acc_ref[...] = jnp.zeros_like(acc_ref)

        for p in range(npair):
            @pl.when((k >= offs[p]) & (k < offs[p] + steps[p]))
            def _(p=p):
                acc_ref[...] += _dot_nt(ab[2 * p][...], ab[2 * p + 1][...])

        @pl.when(k == nk - 1)
        def _():
            acc = acc_ref[...]
            if epilogue is not None:
                acc = epilogue(acc, e_ref[...] if has_extra else None)
            o_ref[...] = acc.astype(o_ref.dtype)

    in_specs, args = [], []
    for p, (a, b) in enumerate(pairs):
        def kidx(k, p=p):
            return jnp.clip(k - offs[p], 0, steps[p] - 1)
        in_specs.append(pl.BlockSpec((tm, tks[p]),
                                     lambda i, j, k, kidx=kidx, p=p: (i, (j // njo) * steps[p] + kidx(k))))
        in_specs.append(weights[p][2]((to, tks[p]), lambda i, j, k, kidx=kidx: (j % njo, kidx(k))))
        args += [a, weights[p][0]]
    if has_extra:
        in_specs.append(pl.BlockSpec((tm, to), lambda i, j, k: (i, j)))
        args.append(extra)
    if dep is not None:
        in_specs.append(ANY)
        args.append(dep)
    return pl.pallas_call(
        body, name=name,
        out_shape=jax.ShapeDtypeStruct((M, slabs * Ko), out_dtype),
        grid=(M // tm, slabs * njo, nk),
        in_specs=in_specs,
        out_specs=pl.BlockSpec((tm, to), lambda i, j, k: (i, j)),
        scratch_shapes=[pltpu.VMEM((tm, to), F32)],
        compiler_params=_cparams("parallel", "parallel", "arbitrary"),
    )(*args)


def _mm_tn(a, b, *, name, a_fn=None, slabs=1):
    M, K = a.shape
    K //= slabs
    N = b.shape[1] // slabs
    tm = _tile(M, (4096, 2048, 1024, 512, 256))
    tk = _tile(K, (1024, 768, 512, 384, 256, 128))
    tn = _tile(N, (1024, 768, 512, 384, 256, 128))
    nmi = M // tm
    nm = slabs * nmi
    nki, nnj = K // tk, N // tn

    def body(a_ref, b_ref, o_ref, acc_ref):
        m = pl.program_id(2)
        av = a_ref[...]
        if a_fn is not None:
            av = a_fn(av)
        part = _dot_tn(av, b_ref[...])
        if nm == 1:
            o_ref[...] = part.astype(o_ref.dtype)
            return

        @pl.when(m == 0)
        def _():
            acc_ref[...] = part

        @pl.when((m > 0) & (m < nm - 1))
        def _():
            acc_ref[...] += part

        @pl.when(m == nm - 1)
        def _():
            o_ref[...] = (acc_ref[...] + part).astype(o_ref.dtype)

    return pl.pallas_call(
        body, name=name,
        out_shape=jax.ShapeDtypeStruct((K, N), BF16),
        grid=(K // tk, N // tn, nm),
        in_specs=[pl.BlockSpec((tm, tk), lambda i, j, m: (m % nmi, (m // nmi) * nki + i)),
                  pl.BlockSpec((tm, tn), lambda i, j, m: (m % nmi, (m // nmi) * nnj + j))],
        out_specs=pl.BlockSpec((tk, tn), lambda i, j, m: (i, j)),
        scratch_shapes=[pltpu.VMEM((tk, tn), F32)],
        compiler_params=_cparams("parallel", "parallel", "arbitrary"),
    )(a, b)


def _rmsnorm_fwd(h, g, *, name, dep=None, views=()):
    S, D = h.shape
    tm = _tile(S, (512, 256))
    nv = len(views)

    def body(h_ref, g_ref, *rest):
        n_out = 1 + nv
        outs = rest[len(rest) - n_out - (1 if nv else 0):len(rest) - (1 if nv else 0)]
        x = h_ref[...]
        r = lax.rsqrt(jnp.mean(x * x, axis=-1, keepdims=True) + RMS_EPS)
        y = x * r * g_ref[...]
        outs[0][...] = y.astype(BF16)
        if nv:
            scr = rest[-1]
            _to_chunks(scr, y)
            for v_ref, d in zip(outs[1:], views):
                _slabs_from_chunks(v_ref, scr, d, BF16)

    res = pl.pallas_call(
        body, name=name,
        out_shape=(jax.ShapeDtypeStruct((S, D), BF16),)
        + tuple(jax.ShapeDtypeStruct((S // d, d * D), BF16) for d in views),
        grid=(S // tm,),
        in_specs=[pl.BlockSpec((tm, D), lambda i: (i, 0)), pl.BlockSpec((1, D), lambda i: (0, 0))]
        + ([ANY] if dep is not None else []),
        out_specs=(pl.BlockSpec((tm, D), lambda i: (i, 0)),)
        + tuple(pl.BlockSpec((tm // d, d * D), lambda i: (i, 0)) for d in views),
        scratch_shapes=[_chunk_scratch(tm, D)] if nv else [],
        compiler_params=_cparams("parallel"),
    )(h, g, *([dep] if dep is not None else []))
    return res if nv else res[0]


def _rms_bwd_rows(x, g, dy):
    r = lax.rsqrt(jnp.mean(x * x, axis=-1, keepdims=True) + RMS_EPS)
    xh = x * r
    gy = dy * g
    dx = r * (gy - xh * jnp.mean(xh * gy, axis=-1, keepdims=True))
    return dx, dy * xh


def _rms_bwd(x, g, dn, dres, *, name, dn_views=()):
    S, D = x.shape
    tm = _tile(S, (512, 256))
    nv = len(dn_views)

    def body(x_ref, g_ref, dn_ref, dr_ref, *rest):
        v_refs = rest[:nv]
        dx_ref, dxb_ref, dg_ref = rest[nv:nv + 3]
        scr = rest[nv + 3:]
        i = pl.program_id(0)
        dn = dn_ref[...]
        for v_ref, s_ref, (_, d) in zip(v_refs, scr, dn_views):
            _chunks_from_slabs(s_ref, v_ref, d)
            dn = dn + _from_chunks(s_ref)
        dx, dgx = _rms_bwd_rows(x_ref[...], g_ref[...], dn)
        tot = dr_ref[...] + dx
        dx_ref[...] = tot
        dxb_ref[...] = tot.astype(BF16)

        @pl.when(i == 0)
        def _():
            dg_ref[...] = jnp.zeros_like(dg_ref)

        dg_ref[0:1, :] += jnp.sum(dgx, axis=0, keepdims=True)

    row = pl.BlockSpec((tm, D), lambda i: (i, 0))
    return pl.pallas_call(
        body, name=name,
        out_shape=(jax.ShapeDtypeStruct((S, D), F32), jax.ShapeDtypeStruct((S, D), BF16),
                   jax.ShapeDtypeStruct((8, D), F32)),
        grid=(S // tm,),
        in_specs=[row, pl.BlockSpec((1, D), lambda i: (0, 0)), row, row]
        + [pl.BlockSpec((tm // d, d * D), lambda i: (i, 0)) for _, d in dn_views],
        out_specs=(row, row, pl.BlockSpec((8, D), lambda i: (0, 0))),
        scratch_shapes=[_chunk_scratch(tm, D)] * nv,
        compiler_params=_cparams("arbitrary"),
    )(x, g, dn, dres, *[a for a, _ in dn_views])


def _mm_nt_rms_bwd(a, b, x, g, dres, *, name, dep=None):
    M, N = a.shape
    b, (D, _), b_spec = _weight(b)
    tm = _tile(M, (1024, 512, 256))
    tk = _tile(N, (1024, 512, 256, 128))
    nk = N // tk

    def body(a_ref, b_ref, x_ref, g_ref, dr_ref, *rest):
        dx_ref, dxb_ref, dg_ref, acc_ref = rest[-4:]
        i = pl.program_id(0)
        k = pl.program_id(1)
        part = _dot_nt(a_ref[...], b_ref[...])

        @pl.when((i == 0) & (k == 0))
        def _():
            dg_ref[...] = jnp.zeros_like(dg_ref)

        @pl.when(k == 0)
        def _():
            acc_ref[...] = part

        @pl.when((k > 0) & (k < nk - 1))
        def _():
            acc_ref[...] += part

        @pl.when(k == nk - 1)
        def _():
            dn = part if nk == 1 else acc_ref[...] + part
            dx, dgx = _rms_bwd_rows(x_ref[...], g_ref[...], dn)
            tot = dr_ref[...] + dx
            dx_ref[...] = tot
            dxb_ref[...] = tot.astype(BF16)
            dg_ref[0:1, :] += jnp.sum(dgx, axis=0, keepdims=True)

    row = pl.BlockSpec((tm, D), lambda i, k: (i, 0))
    in_specs = [pl.BlockSpec((tm, tk), lambda i, k: (i, k)), b_spec((D, tk), lambda i, k: (0, k)),
                row, pl.BlockSpec((1, D), lambda i, k: (0, 0)), row]
    args = [a, b, x, g, dres]
    if dep is not None:
        in_specs.append(ANY)
        args.append(dep)
    return pl.pallas_call(
        body, name=name,
        out_shape=(jax.ShapeDtypeStruct((M, D), F32), jax.ShapeDtypeStruct((M, D), BF16),
                   jax.ShapeDtypeStruct((8, D), F32)),
        grid=(M // tm, nk),
        in_specs=in_specs,
        out_specs=(row, row, pl.BlockSpec((8, D), lambda i, k: (0, 0))),
        scratch_shapes=[pltpu.VMEM((tm, D), F32)],
        compiler_params=_cparams("arbitrary", "arbitrary"),
    )(*args)


def _loss_bwd(h, target, g, *, name):
    S, D = h.shape
    tm = _tile(S, (512, 256))

    def body(h_ref, t_ref, g_ref, dx_ref, dxb_ref, acc_ref):
        i = pl.program_id(0)
        x = h_ref[...]
        gv = g_ref[...]
        r = lax.rsqrt(jnp.mean(x * x, axis=-1, keepdims=True) + RMS_EPS)
        err = x * r * gv - t_ref[...]
        dx, dgx = _rms_bwd_rows(x, gv, err * (1.0 / D))
        dx_ref[...] = dx
        dxb_ref[...] = dx.astype(BF16)

        @pl.when(i == 0)
        def _():
            acc_ref[...] = jnp.zeros_like(acc_ref)

        acc_ref[0:1, :] += jnp.sum(dgx, axis=0, keepdims=True)
        acc_ref[1:2, :] += jnp.sum(err * err, axis=0, keepdims=True)

    row = pl.BlockSpec((tm, D), lambda i: (i, 0))
    return pl.pallas_call(
        body, name=name,
        out_shape=(jax.ShapeDtypeStruct((S, D), F32), jax.ShapeDtypeStruct((S, D), BF16),
                   jax.ShapeDtypeStruct((8, D), F32)),
        grid=(S // tm,),
        in_specs=[row, row, pl.BlockSpec((1, D), lambda i: (0, 0))],
        out_specs=(row, row, pl.BlockSpec((8, D), lambda i: (0, 0))),
        compiler_params=_cparams("arbitrary"),
    )(h, target, g)


SUBLANES = 8
CONV_ROWS = 64


def _build_shifted(ext_ref, rot_ref, ts, shifts=tuple(range(1, SUBLANES))):
    rows = ts + HALO - SUBLANES
    for j in shifts:
        rot_ref[j, 0:rows, :] = ext_ref[j:j + rows, :]


def _shifted(ext_ref, rot_ref, off, r0, nrows, cs):
    q, j = divmod(off, SUBLANES)
    start = SUBLANES * q + r0
    if j == 0:
        return ext_ref[start:start + nrows, cs]
    return rot_ref[j, start:start + nrows, cs]


def _conv_fwd(uc, conv_w, conv_b, ln_g, ln_b, *, name):
    S, C2 = uc.shape
    C = C2 // 2
    ts = _tile(S, (512, 256))
    per = ts // HALO

    def body(cur_ref, halo_ref, w_ref, b_ref, g_ref, beta_ref, ya_ref, h_ref, ct_ref, ext_ref, rot_ref):
        i = pl.program_id(0)
        hh = halo_ref[:, 0:C] * jax.nn.sigmoid(halo_ref[:, C:C2])
        ext_ref[0:HALO, :] = jnp.where(i == 0, 0.0, hh)
        hc = cur_ref[:, 0:C] * jax.nn.sigmoid(cur_ref[:, C:C2])
        ext_ref[HALO:HALO + ts, :] = hc
        h_ref[...] = hc
        _build_shifted(ext_ref, rot_ref, ts)
        for c0 in range(0, C, LANES):
            cs = slice(c0, c0 + LANES)
            for r0 in range(0, ts, CONV_ROWS):
                acc = jnp.zeros((CONV_ROWS, LANES), F32)
                for k in range(CONV_A_WIDTH):
                    acc = acc + w_ref[k:k + 1, cs] * _shifted(ext_ref, rot_ref, k + 2, r0, CONV_ROWS, cs)
                ct_ref[r0:r0 + CONV_ROWS, cs] = acc + b_ref[:, cs]
        ct = ct_ref[...]
        mu = jnp.mean(ct, axis=-1, keepdims=True)
        xc = ct - mu
        var = jnp.mean(xc * xc, axis=-1, keepdims=True)
        l = xc * lax.rsqrt(var + LN_EPS) * g_ref[...] + beta_ref[...]
        ya_ref[...] = (l * jax.nn.sigmoid(l)).astype(ya_ref.dtype)

    vec = pl.BlockSpec((1, C), lambda i: (0, 0))
    row = pl.BlockSpec((ts, C), lambda i: (i, 0))
    return pl.pallas_call(
        body, name=name,
        out_shape=(jax.ShapeDtypeStruct((S, C), BF16), jax.ShapeDtypeStruct((S, C), F32),
                   jax.ShapeDtypeStruct((S, C), F32)),
        grid=(S // ts,),
        in_specs=[pl.BlockSpec((ts, C2), lambda i: (i, 0)),
                  pl.BlockSpec((HALO, C2), lambda i: (jnp.maximum(i * per - 1, 0), 0)),
                  pl.BlockSpec((CONV_A_WIDTH, C), lambda i: (0, 0)), vec, vec, vec],
        out_specs=(row, row, row),
        scratch_shapes=[pltpu.VMEM((HALO + ts, C), F32), pltpu.VMEM((8, HALO + ts, C), F32)],
        compiler_params=_cparams("parallel"),
    )(uc, uc, conv_w, conv_b, ln_g, ln_b)


CONV_ACC_ROWS = 40


def _conv_bwd_ln(ct, dcat, hglu, ln_g, ln_b, *, name):
    S, C = ct.shape
    CW = dcat.shape[1]
    ts = _tile(S, (512, 256))
    per = ts // HALO

    def body(ct_ref, dcat_ref, hc_ref, hh_ref, g_ref, beta_ref, dc_ref, acc_ref, ext_ref, rot_ref):
        i = pl.program_id(0)
        ct = ct_ref[...]
        gv = g_ref[...]
        mu = jnp.mean(ct, axis=-1, keepdims=True)
        xc = ct - mu
        rstd = lax.rsqrt(jnp.mean(xc * xc, axis=-1, keepdims=True) + LN_EPS)
        xh = xc * rstd
        l = xh * gv + beta_ref[...]
        sg = jax.nn.sigmoid(l)
        dl = dcat_ref[:, 0:C] * (sg * (1.0 + l * (1.0 - sg)))
        dxh = dl * gv
        dc = rstd * (dxh - jnp.mean(dxh, axis=-1, keepdims=True)
                     - xh * jnp.mean(dxh * xh, axis=-1, keepdims=True))
        dc_ref[...] = dc

        @pl.when(i == 0)
        def _():
            acc_ref[...] = jnp.zeros_like(acc_ref)

        acc_ref[32:33, :] += jnp.sum(dc, axis=0, keepdims=True)
        acc_ref[33:34, :] += jnp.sum(dl * xh, axis=0, keepdims=True)
        acc_ref[34:35, :] += jnp.sum(dl, axis=0, keepdims=True)
        ext_ref[0:HALO, :] = jnp.where(i == 0, 0.0, hh_ref[...])
        ext_ref[HALO:HALO + ts, :] = hc_ref[...]
        _build_shifted(ext_ref, rot_ref, ts)
        for c0 in range(0, C, LANES):
            cs = slice(c0, c0 + LANES)
            dcc = dc_ref[:, cs]
            for k in range(CONV_A_WIDTH):
                acc_ref[k:k + 1, cs] += jnp.sum(dcc * _shifted(ext_ref, rot_ref, k + 2, 0, ts, cs),
                                                axis=0, keepdims=True)

    vec = pl.BlockSpec((1, C), lambda i: (0, 0))
    row = pl.BlockSpec((ts, C), lambda i: (i, 0))
    return pl.pallas_call(
        body, name=name,
        out_shape=(jax.ShapeDtypeStruct((S, C), F32), jax.ShapeDtypeStruct((CONV_ACC_ROWS, C), F32)),
        grid=(S // ts,),
        in_specs=[row, pl.BlockSpec((ts, CW), lambda i: (i, 0)), row,
                  pl.BlockSpec((HALO, C), lambda i: (jnp.maximum(i * per - 1, 0), 0)), vec, vec],
        out_specs=(row, pl.BlockSpec((CONV_ACC_ROWS, C), lambda i: (0, 0))),
        scratch_shapes=[pltpu.VMEM((HALO + ts, C), F32), pltpu.VMEM((8, HALO + ts, C), F32)],
        compiler_params=_cparams("arbitrary"),
    )(ct, dcat, hglu, hglu, ln_g, ln_b)


def _conv_bwd_in(dc, uc, conv_w, *, name):
    S, C = dc.shape
    C2 = 2 * C
    ts = _tile(S, (512, 256))
    per = ts // HALO
    nt = S // ts

    def body(dc_ref, dn_ref, uc_ref, w_ref, du_ref, ext_ref, rot_ref):
        i = pl.program_id(0)
        ext_ref[0:ts, :] = dc_ref[...]
        ext_ref[ts:ts + HALO, :] = jnp.where(i == nt - 1, 0.0, dn_ref[...])
        _build_shifted(ext_ref, rot_ref, ts)
        for c0 in range(0, C, LANES):
            cs = slice(c0, c0 + LANES)
            gs = slice(C + c0, C + c0 + LANES)
            for r0 in range(0, ts, CONV_ROWS):
                rs = slice(r0, r0 + CONV_ROWS)
                acc = jnp.zeros((CONV_ROWS, LANES), F32)
                for k in range(CONV_A_WIDTH):
                    acc = acc + w_ref[k:k + 1, cs] * _shifted(ext_ref, rot_ref, 30 - k, r0, CONV_ROWS, cs)
                sg = jax.nn.sigmoid(uc_ref[rs, gs])
                du_ref[rs, cs] = (acc * sg).astype(du_ref.dtype)
                du_ref[rs, gs] = (acc * uc_ref[rs, cs] * sg * (1.0 - sg)).astype(du_ref.dtype)

    return pl.pallas_call(
        body, name=name,
        out_shape=jax.ShapeDtypeStruct((S, C2), BF16),
        grid=(nt,),
        in_specs=[pl.BlockSpec((ts, C), lambda i: (i, 0)),
                  pl.BlockSpec((HALO, C), lambda i: (jnp.minimum((i + 1) * per, S // HALO - 1), 0)),
                  pl.BlockSpec((ts, C2), lambda i: (i, 0)),
                  pl.BlockSpec((CONV_A_WIDTH, C), lambda i: (0, 0))],
        out_specs=pl.BlockSpec((ts, C2), lambda i: (i, 0)),
        scratch_shapes=[pltpu.VMEM((ts + HALO, C), F32), pltpu.VMEM((8, ts + HALO, C), F32)],
        compiler_params=_cparams("parallel"),
    )(dc, dc, uc, conv_w)


SC_SHIFTS_BACK = ((HALO - 2) % SUBLANES, (HALO - 1) % SUBLANES)
SC_SHIFTS_AHEAD = (1, 2)


def _sc_fwd(u3, conv_w, *, name):
    S, W3 = u3.shape
    W = W3 // 3
    ts = _tile(S, (256,))
    per = ts // HALO

    def body(cur_ref, halo_ref, w_ref, y_ref, ext_ref, rot_ref):
        i = pl.program_id(0)
        cvh = halo_ref[:, W:2 * W].astype(F32) * halo_ref[:, 2 * W:W3].astype(F32)
        ext_ref[0:HALO, :] = jnp.where(i == 0, 0.0, cvh)
        ext_ref[HALO:HALO + ts, :] = cur_ref[:, W:2 * W].astype(F32) * cur_ref[:, 2 * W:W3].astype(F32)
        _build_shifted(ext_ref, rot_ref, ts, SC_SHIFTS_BACK)
        for c0 in range(0, W, LANES):
            cs = slice(c0, c0 + LANES)
            for r0 in range(0, ts, CONV_ROWS):
                rs = slice(r0, r0 + CONV_ROWS)
                k = (w_ref[0:1, cs] * _shifted(ext_ref, rot_ref, HALO - 2, r0, CONV_ROWS, cs)
                     + w_ref[1:2, cs] * _shifted(ext_ref, rot_ref, HALO - 1, r0, CONV_ROWS, cs)
                     + w_ref[2:3, cs] * _shifted(ext_ref, rot_ref, HALO, r0, CONV_ROWS, cs))
                y_ref[rs, cs] = (cur_ref[rs, cs].astype(F32) * k).astype(y_ref.dtype)

    return pl.pallas_call(
        body, name=name,
        out_shape=jax.ShapeDtypeStruct((S, W), BF16),
        grid=(S // ts,),
        in_specs=[pl.BlockSpec((ts, W3), lambda i: (i, 0)),
                  pl.BlockSpec((HALO, W3), lambda i: (jnp.maximum(i * per - 1, 0), 0)),
                  pl.BlockSpec((SC_CONV_WIDTH, W), lambda i: (0, 0))],
        out_specs=pl.BlockSpec((ts, W), lambda i: (i, 0)),
        scratch_shapes=[pltpu.VMEM((HALO + ts, W), F32), pltpu.VMEM((8, HALO + ts, W), F32)],
        compiler_params=_cparams("parallel"),
    )(u3, u3, conv_w)


def _sc_bwd(u3, dy, conv_w, *, name):
    S, W3 = u3.shape
    W = W3 // 3
    ts = _tile(S, (256,))
    per = ts // HALO
    nt = S // ts

    def body(cur_ref, prev_ref, next_ref, dy_ref, dyn_ref, w_ref, du_ref, dw_ref, cv_ext, dk_ext, cv_rot, dk_rot):
        i = pl.program_id(0)
        cvh = prev_ref[:, W:2 * W].astype(F32) * prev_ref[:, 2 * W:W3].astype(F32)
        cv_ext[0:HALO, :] = jnp.where(i == 0, 0.0, cvh)
        cv_ext[HALO:HALO + ts, :] = cur_ref[:, W:2 * W].astype(F32) * cur_ref[:, 2 * W:W3].astype(F32)
        dk_ext[0:ts, :] = dy_ref[...] * cur_ref[:, 0:W].astype(F32)
        dk_ext[ts:ts + HALO, :] = jnp.where(i == nt - 1, 0.0, dyn_ref[...] * next_ref[:, 0:W].astype(F32))
        _build_shifted(cv_ext, cv_rot, ts, SC_SHIFTS_BACK)
        _build_shifted(dk_ext, dk_rot, ts, SC_SHIFTS_AHEAD)

        @pl.when(i == 0)
        def _():
            dw_ref[...] = jnp.zeros_like(dw_ref)

        for c0 in range(0, W, LANES):
            cs = slice(c0, c0 + LANES)
            w0, w1, w2 = w_ref[0:1, cs], w_ref[1:2, cs], w_ref[2:3, cs]
            sums = [jnp.zeros((1, LANES), F32)] * SC_CONV_WIDTH
            for r0 in range(0, ts, CONV_ROWS):
                rs = slice(r0, r0 + CONV_ROWS)
                cv2 = _shifted(cv_ext, cv_rot, HALO - 2, r0, CONV_ROWS, cs)
                cv1 = _shifted(cv_ext, cv_rot, HALO - 1, r0, CONV_ROWS, cs)
                cv0 = _shifted(cv_ext, cv_rot, HALO, r0, CONV_ROWS, cs)
                dk = dk_ext[rs, cs]
                dcv = (w2 * dk + w1 * _shifted(dk_ext, dk_rot, 1, r0, CONV_ROWS, cs)
                       + w0 * _shifted(dk_ext, dk_rot, 2, r0, CONV_ROWS, cs))
                du_ref[rs, cs] = (dy_ref[rs, cs] * (w0 * cv2 + w1 * cv1 + w2 * cv0)).astype(du_ref.dtype)
                du_ref[rs, W + c0:W + c0 + LANES] = (
                    dcv * cur_ref[rs, 2 * W + c0:2 * W + c0 + LANES].astype(F32)).astype(du_ref.dtype)
                du_ref[rs, 2 * W + c0:2 * W + c0 + LANES] = (
                    dcv * cur_ref[rs, W + c0:W + c0 + LANES].astype(F32)).astype(du_ref.dtype)
                for t, cvt in enumerate((cv2, cv1, cv0)):
                    sums[t] = sums[t] + jnp.sum(dk * cvt, axis=0, keepdims=True)
            for t in range(SC_CONV_WIDTH):
                dw_ref[t:t + 1, cs] += sums[t]

    nxt = lambda i: (jnp.minimum((i + 1) * per, S // HALO - 1), 0)
    return pl.pallas_call(
        body, name=name,
        out_shape=(jax.ShapeDtypeStruct((S, W3), BF16), jax.ShapeDtypeStruct((8, W), F32)),
        grid=(nt,),
        in_specs=[pl.BlockSpec((ts, W3), lambda i: (i, 0)),
                  pl.BlockSpec((HALO, W3), lambda i: (jnp.maximum(i * per - 1, 0), 0)),
                  pl.BlockSpec((HALO, W3), nxt),
                  pl.BlockSpec((ts, W), lambda i: (i, 0)),
                  pl.BlockSpec((HALO, W), nxt),
                  pl.BlockSpec((SC_CONV_WIDTH, W), lambda i: (0, 0))],
        out_specs=(pl.BlockSpec((ts, W3), lambda i: (i, 0)), pl.BlockSpec((8, W), lambda i: (0, 0))),
        scratch_shapes=[pltpu.VMEM((HALO + ts, W), F32), pltpu.VMEM((ts + HALO, W), F32),
                        pltpu.VMEM((8, HALO + ts, W), F32), pltpu.VMEM((8, ts + HALO, W), F32)],
        compiler_params=_cparams("arbitrary"),
    )(u3, u3, u3, dy, dy, conv_w)


def _t5_causal_bucket(n):
    max_exact = NUM_BUCKETS // 2
    nf = jnp.maximum(n, 1).astype(F32)
    large = max_exact + (jnp.log(nf / max_exact) / math.log(REL_MAX_DISTANCE / max_exact)
                         * (NUM_BUCKETS - max_exact)).astype(jnp.int32)
    return jnp.where(n < max_exact, n, jnp.minimum(large, NUM_BUCKETS - 1))


def _bucket_tables():
    steps = ATTN_BLK
    m = jnp.arange(steps)[:, None] + steps - jnp.arange(2 * steps)[None, :]
    return jnp.stack([_t5_causal_bucket(jnp.clip(m, 0, steps) * dil).astype(F32) for _, dil in DILATED_GROUPS])


def _bias_tables(rel_bias, buckets, *, name):
    steps = ATTN_BLK

    def body(tab_ref, bk_ref, o_ref):
        g = pl.program_id(0)
        bk = bk_ref[0]
        a_idx = lax.broadcasted_iota(jnp.int32, (steps, 2 * steps), 0)
        c_idx = lax.broadcasted_iota(jnp.int32, (steps, 2 * steps), 1)
        m = a_idx + steps - c_idx
        band = (m >= 0) & (m <= steps)
        band_first = band & (c_idx >= steps)
        for h in range(HEADS_PER_GROUP):
            bias = jnp.zeros((steps, 2 * steps), F32)
            for b in range(NUM_BUCKETS):
                bias = jnp.where(bk == float(b), tab_ref[b, g * HEADS_PER_GROUP + h], bias)
            o_ref[0, 0, h] = jnp.where(band_first, bias, NEG_INF)
            o_ref[0, 1, h] = jnp.where(band, bias, NEG_INF)

    return pl.pallas_call(
        body, name=name,
        out_shape=jax.ShapeDtypeStruct((N_GROUPS, 2, HEADS_PER_GROUP, steps, 2 * steps), F32),
        grid=(N_GROUPS,),
        in_specs=[pl.BlockSpec(memory_space=pltpu.SMEM),
                  pl.BlockSpec((1, steps, 2 * steps), lambda g: (g, 0, 0))],
        out_specs=pl.BlockSpec((1, 2, HEADS_PER_GROUP, steps, 2 * steps), lambda g: (g, 0, 0, 0, 0)),
        compiler_params=_cparams("parallel"),
    )(rel_bias, buckets)


def _lane_is_low():
    return lax.broadcasted_iota(jnp.int32, (1, LANES), 1) < HEAD_DIM


def _stack_heads(x2, low):
    zero = jnp.zeros_like(x2)
    return jnp.concatenate([jnp.where(low, x2, zero), jnp.where(low, zero, x2)], axis=0)


def _qkv_specs(nb):
    nqb = GROUP_QKV // ATTN_OUT

    def spec(t, prev):
        def idx(r, n):
            nn = jnp.minimum(n, nb - 1)
            row = jnp.maximum(nn - 1, 0) if prev else nn
            return (row, r * nqb + t)
        return pl.BlockSpec((ATTN_BLK, ATTN_OUT), idx)

    return [spec(0, False), spec(1, False), spec(1, True), spec(2, False), spec(2, True)]


ATTN_FWD_BLOCKS = 4


def _attn_fwd(uv, bias, g, d, *, name):
    rows = uv.shape[0]
    nsub = ATTN_FWD_BLOCKS
    step_rows = nsub * ATTN_BLK
    nqb = GROUP_QKV // ATTN_OUT

    def body(q_ref, kc_ref, kp_ref, vc_ref, vp_ref, bias_ref, o_ref, l_ref):
        n = pl.program_id(1)
        low = _lane_is_low()
        slabs = [slice(hp * LANES, (hp + 1) * LANES) for hp in range(HEADS_PER_GROUP // 2)]
        for sub in range(nsub):
            qr = slice(sub * ATTN_BLK, (sub + 1) * ATTN_BLK)
            sel = jnp.minimum(n, 1) if sub == 0 else 1

            def with_prev(cur_ref, prev_ref, sl, sub=sub, qr=qr):
                prev = prev_ref[:, sl] if sub == 0 else cur_ref[(sub - 1) * ATTN_BLK:sub * ATTN_BLK, sl]
                return jnp.concatenate([prev, cur_ref[qr, sl]], axis=0)

            scores = [_dot_nt(_stack_heads(q_ref[qr, sl] * (HEAD_DIM ** -0.5), low), with_prev(kc_ref, kp_ref, sl))
                      for sl in slabs]
            probs, dens_all, lses_all = [], [], []
            for hp, s in enumerate(scores):
                ps, dens, lses = [], [], []
                for hh in range(2):
                    logits = s[hh * ATTN_BLK:(hh + 1) * ATTN_BLK] + bias_ref[sel, 2 * hp + hh]
                    mx = jnp.max(logits, axis=-1, keepdims=True)
                    p = jnp.exp(logits - mx)
                    den = jnp.sum(p, axis=-1, keepdims=True)
                    ps.append(p.astype(BF16))
                    dens.append(den)
                    lses.append(jnp.broadcast_to(mx + jnp.log(den), (ATTN_BLK, LANES)))
                probs.append(jnp.concatenate(ps, axis=0))
                dens_all.append(dens)
                lses_all.append(lses)
            for hp, sl in enumerate(slabs):
                pv = jnp.dot(probs[hp], with_prev(vc_ref, vp_ref, sl), preferred_element_type=F32)
                dens, lses = dens_all[hp], lses_all[hp]
                o_ref[qr, sl] = jnp.where(low, pv[0:ATTN_BLK] / dens[0], pv[ATTN_BLK:2 * ATTN_BLK] / dens[1])
                l_ref[qr, sl] = jnp.where(low, lses[0], lses[1])

    def cur(t):
        return pl.BlockSpec((step_rows, ATTN_OUT), lambda r, n: (n, r * nqb + t))

    def prev(t):
        return pl.BlockSpec((ATTN_BLK, ATTN_OUT), lambda r, n: (jnp.maximum(n * nsub - 1, 0), r * nqb + t))

    out_spec = pl.BlockSpec((step_rows, ATTN_OUT), lambda r, n: (n, r))
    return pl.pallas_call(
        body, name=name,
        out_shape=(jax.ShapeDtypeStruct((rows, d * ATTN_OUT), F32),) * 2,
        grid=(d, rows // step_rows),
        in_specs=[cur(0), cur(1), prev(1), cur(2), prev(2),
                  pl.BlockSpec((None, 2, HEADS_PER_GROUP, ATTN_BLK, 2 * ATTN_BLK), lambda r, n: (g, 0, 0, 0, 0))],
        out_specs=(out_spec, out_spec),
        compiler_params=_cparams("parallel", "parallel"),
    )(uv, uv, uv, uv, uv, bias)


def _chunk_scratch(n, width):
    return pltpu.VMEM((width // LANES, n, LANES), F32)


def _to_chunks(scr, val):
    for c in range(scr.shape[0]):
        scr[c] = val[:, c * LANES:(c + 1) * LANES]


def _from_chunks(scr):
    return jnp.concatenate([scr[c] for c in range(scr.shape[0])], axis=1)


def _slabs_from_chunks(dst_ref, scr, d, dtype):
    nc, n, _ = scr.shape
    for r in range(d):
        for c in range(nc):
            col = r * nc * LANES + c * LANES
            dst_ref[:, col:col + LANES] = scr[c, pl.ds(r, n // d, stride=d), :].astype(dtype)


def _chunks_from_slabs(scr, src_ref, d):
    nc, n, _ = scr.shape
    for r in range(d):
        for c in range(nc):
            col = r * nc * LANES + c * LANES
            scr[c, pl.ds(r, n // d, stride=d), :] = src_ref[:, col:col + LANES]


def _attn_merge(outs, lses, ya, *, name):
    S, C = ya.shape
    tm = _tile(S, (512, 256))
    dils = [dil for _, dil in DILATED_GROUPS]

    def body(o0, o1, o2, l0, l1, l2, ya_ref, cat_ref, out_ref, lse_ref, so1, so2, sl1, sl2):
        _chunks_from_slabs(so1, o1, dils[1])
        _chunks_from_slabs(so2, o2, dils[2])
        _chunks_from_slabs(sl1, l1, dils[1])
        _chunks_from_slabs(sl2, l2, dils[2])
        a0, a1, a2 = l0[...], _from_chunks(sl1), _from_chunks(sl2)
        m = jnp.maximum(jnp.maximum(a0, a1), a2)
        e0, e1, e2 = jnp.exp(a0 - m), jnp.exp(a1 - m), jnp.exp(a2 - m)
        den = e0 + e1 + e2
        out = (e0 * o0[...] + e1 * _from_chunks(so1) + e2 * _from_chunks(so2)) / den
        out_ref[...] = out
        lse_ref[...] = m + jnp.log(den)
        cat_ref[:, 0:C] = ya_ref[...]
        cat_ref[:, C:C + ATTN_OUT] = out.astype(cat_ref.dtype)

    blk = pl.BlockSpec((tm, ATTN_OUT), lambda i: (i, 0))
    vblk = [pl.BlockSpec((tm // d, d * ATTN_OUT), lambda i: (i, 0)) for d in dils]
    assert dils[0] == 1
    return pl.pallas_call(
        body, name=name,
        out_shape=(jax.ShapeDtypeStruct((S, C + ATTN_OUT), BF16), jax.ShapeDtypeStruct((S, ATTN_OUT), F32),
                   jax.ShapeDtypeStruct((S, ATTN_OUT), F32)),
        grid=(S // tm,),
        in_specs=vblk + vblk + [pl.BlockSpec((tm, C), lambda i: (i, 0))],
        out_specs=(pl.BlockSpec((tm, C + ATTN_OUT), lambda i: (i, 0)), blk, blk),
        scratch_shapes=[_chunk_scratch(tm, ATTN_OUT)] * 4,
        compiler_params=_cparams("parallel"),
    )(*outs, *lses, ya)


def _attn_prep(dcat, outf, lse, *, name):
    S, CW = dcat.shape
    C = CW - ATTN_OUT
    tm = _tile(S, (512, 256))
    dils = [dil for _, dil in DILATED_GROUPS]
    assert dils[0] == 1
    ones = np.kron(np.eye(HEADS_PER_GROUP, dtype=np.float32), np.ones((HEAD_DIM, HEAD_DIM), np.float32))

    nviews = 3 * (len(dils) - 1)

    def body(dcat_ref, out_ref, l_ref, ones_ref, dyb_ref, dl_ref, *rest):
        views = rest[:nviews]
        s_dyb, s_dl, s_l = rest[nviews:]
        dyb = dcat_ref[:, C:CW]
        dyb_ref[...] = dyb.astype(BF16)
        prod = dyb * out_ref[...]
        ov = ones_ref[...]
        hi, mid, lo = _split_bf16(prod)
        delta = (jnp.dot(hi, ov, preferred_element_type=F32)
                 + jnp.dot(mid, ov, preferred_element_type=F32)
                 + jnp.dot(lo, ov, preferred_element_type=F32))
        dl_ref[...] = delta
        _to_chunks(s_dyb, dyb)
        _to_chunks(s_dl, delta)
        _to_chunks(s_l, l_ref[...])
        for gi, d in enumerate(dils[1:]):
            dyb_v, dl_v, l_v = views[3 * gi:3 * gi + 3]
            _slabs_from_chunks(dyb_v, s_dyb, d, BF16)
            _slabs_from_chunks(dl_v, s_dl, d, F32)
            _slabs_from_chunks(l_v, s_l, d, F32)

    blk = pl.BlockSpec((tm, ATTN_OUT), lambda i: (i, 0))
    view_shapes, view_specs = [], []
    for d in dils[1:]:
        for dt in (BF16, F32, F32):
            view_shapes.append(jax.ShapeDtypeStruct((S // d, d * ATTN_OUT), dt))
            view_specs.append(pl.BlockSpec((tm // d, d * ATTN_OUT), lambda i: (i, 0)))
    res = pl.pallas_call(
        body, name=name,
        out_shape=(jax.ShapeDtypeStruct((S, ATTN_OUT), BF16), jax.ShapeDtypeStruct((S, ATTN_OUT), F32),
                   *view_shapes),
        grid=(S // tm,),
        in_specs=[pl.BlockSpec((tm, CW), lambda i: (i, 0)), blk, blk,
                  pl.BlockSpec((ATTN_OUT, ATTN_OUT), lambda i: (0, 0))],
        out_specs=(blk, blk, *view_specs),
        scratch_shapes=[_chunk_scratch(tm, ATTN_OUT)] * 3,
        compiler_params=_cparams("parallel"),
    )(dcat, outf, lse, jnp.asarray(ones, BF16))
    return [(res[0], res[1], lse)] + [tuple(res[2 + 3 * gi:5 + 3 * gi]) for gi in range(len(dils) - 1)]


def _attn_bwd(uv, dov, lv, dv_, bias, g, d, *, name):
    rows = uv.shape[0]
    nb = rows // ATTN_BLK
    steps = d * nb
    scale = HEAD_DIM ** -0.5

    def body(q_ref, kc_ref, kp_ref, vc_ref, vp_ref, do_ref, l_ref, dl_ref, bias_ref,
             out_ref, db_ref, dq_s, dk_s, dv_s):
        t = pl.program_id(0)
        n = t % nb
        low = _lane_is_low()

        @pl.when(t == 0)
        def _():
            db_ref[...] = jnp.zeros_like(db_ref)
            dq_s[...] = jnp.zeros_like(dq_s)
            dk_s[...] = jnp.zeros_like(dk_s)
            dv_s[...] = jnp.zeros_like(dv_s)

        @pl.when(t < steps)
        def _():
            sel = jnp.minimum(n, 1)
            lane = lax.broadcasted_iota(jnp.int32, (1, LANES), 1)
            slabs = [slice(hp * LANES, (hp + 1) * LANES) for hp in range(HEADS_PER_GROUP // 2)]
            keys = [jnp.concatenate([kp_ref[:, sl], kc_ref[:, sl]], axis=0) for sl in slabs]
            scores = [_dot_nt(_stack_heads(q_ref[:, sl] * scale, low), keys[hp]) for hp, sl in enumerate(slabs)]
            dps = [_dot_nt(_stack_heads(do_ref[:, sl], low),
                           jnp.concatenate([vp_ref[:, sl], vc_ref[:, sl]], axis=0)) for sl in slabs]
            stacked = []
            for hp, sl in enumerate(slabs):
                lse2 = l_ref[:, sl]
                dl2 = dl_ref[:, sl]
                pbs, dsbs = [], []
                for hh in range(2):
                    rows = slice(hh * ATTN_BLK, (hh + 1) * ATTN_BLK)
                    one = lane == hh * HEAD_DIM
                    lse_col = jnp.sum(jnp.where(one, lse2, 0.0), axis=-1, keepdims=True)
                    dl_col = jnp.sum(jnp.where(one, dl2, 0.0), axis=-1, keepdims=True)
                    p = jnp.exp(scores[hp][rows] + bias_ref[sel, 2 * hp + hh] - lse_col)
                    ds = p * (dps[hp][rows] - dl_col)
                    db_ref[2 * hp + hh] += ds
                    pbs.append(p.astype(BF16))
                    dsbs.append((ds * scale).astype(BF16))
                stacked.append((jnp.concatenate(dsbs, axis=0), jnp.concatenate(dsbs, axis=1),
                                jnp.concatenate(pbs, axis=1)))
            nk2 = 2 * ATTN_BLK
            for hp, sl in enumerate(slabs):
                ds_rows, ds_cols, p_cols = stacked[hp]
                dq = jnp.dot(ds_rows, keys[hp], preferred_element_type=F32)
                dk = _dot_tn(ds_cols, q_ref[:, sl])
                dv = _dot_tn(p_cols, do_ref[:, sl])
                dq2 = jnp.where(low, dq[0:ATTN_BLK], dq[ATTN_BLK:nk2])
                dk2 = jnp.where(low, dk[0:nk2], dk[nk2:2 * nk2])
                dv2 = jnp.where(low, dv[0:nk2], dv[nk2:2 * nk2])
                out_ref[:, sl] = dq_s[:, sl].astype(out_ref.dtype)
                dq_s[:, sl] = dq2
                ksl = slice(ATTN_OUT + hp * LANES, ATTN_OUT + (hp + 1) * LANES)
                vsl = slice(2 * ATTN_OUT + hp * LANES, 2 * ATTN_OUT + (hp + 1) * LANES)
                out_ref[:, ksl] = (dk_s[:, sl] + dk2[0:ATTN_BLK]).astype(out_ref.dtype)
                dk_s[:, sl] = dk2[ATTN_BLK:2 * ATTN_BLK]
                out_ref[:, vsl] = (dv_s[:, sl] + dv2[0:ATTN_BLK]).astype(out_ref.dtype)
                dv_s[:, sl] = dv2[ATTN_BLK:2 * ATTN_BLK]

        @pl.when(t == steps)
        def _():
            out_ref[:, 0:ATTN_OUT] = dq_s[...].astype(out_ref.dtype)
            out_ref[:, ATTN_OUT:2 * ATTN_OUT] = dk_s[...].astype(out_ref.dtype)
            out_ref[:, 2 * ATTN_OUT:GROUP_QKV] = dv_s[...].astype(out_ref.dtype)

    rowblk = pl.BlockSpec((ATTN_BLK, ATTN_OUT), lambda r, n: (jnp.minimum(n, nb - 1), r))

    def at_step(spec):
        def index_map(t):
            tt = jnp.minimum(t, steps - 1)
            return spec.index_map(tt // nb, tt % nb)
        return pl.BlockSpec(spec.block_shape, index_map)

    return pl.pallas_call(
        body, name=name,
        out_shape=(jax.ShapeDtypeStruct((rows, d * GROUP_QKV), BF16),
                   jax.ShapeDtypeStruct((HEADS_PER_GROUP, ATTN_BLK, 2 * ATTN_BLK), F32)),
        grid=(steps + 1,),
        in_specs=[at_step(sp) for sp in _qkv_specs(nb) + [rowblk, rowblk, rowblk]] + [
            pl.BlockSpec((None, 2, HEADS_PER_GROUP, ATTN_BLK, 2 * ATTN_BLK), lambda t: (g, 0, 0, 0, 0))],
        out_specs=(pl.BlockSpec((ATTN_BLK, GROUP_QKV),
                                lambda t: (jnp.maximum(t - 1, 0) % nb, jnp.maximum(t - 1, 0) // nb)),
                   pl.BlockSpec((HEADS_PER_GROUP, ATTN_BLK, 2 * ATTN_BLK), lambda t: (0, 0, 0))),
        scratch_shapes=[pltpu.VMEM((ATTN_BLK, ATTN_OUT), F32)] * 3,
        compiler_params=_cparams("arbitrary"),
    )(uv, uv, uv, uv, uv, dov, lv, dv_, bias)


def _split_bf16(x):
    hi = x.astype(BF16)
    r1 = x - hi.astype(F32)
    mid = r1.astype(BF16)
    lo = (r1 - mid.astype(F32)).astype(BF16)
    return hi, mid, lo


RELBIAS_CHUNK = 4096


def _relbias_reduce(dbs, buckets, *, name):
    flat = ATTN_BLK * 2 * ATTN_BLK
    dbf = jnp.stack([db.reshape(HEADS_PER_GROUP, flat) for db in dbs])
    bkf = buckets.reshape(N_GROUPS, 1, flat)

    def body(db_ref, bk_ref, o_ref):
        c = pl.program_id(1)
        rows = lax.broadcasted_iota(jnp.int32, (LANES, RELBIAS_CHUNK), 0).astype(F32)
        onehot = jnp.where(rows == bk_ref[0], 1.0, 0.0).astype(BF16)
        hi, mid, lo = _split_bf16(db_ref[0])
        part = _dot_nt(hi, onehot) + _dot_nt(mid, onehot) + _dot_nt(lo, onehot)

        @pl.when(c == 0)
        def _():
            o_ref[0] = part

        @pl.when(c > 0)
        def _():
            o_ref[0] += part

    return pl.pallas_call(
        body, name=name,
        out_shape=jax.ShapeDtypeStruct((N_GROUPS, HEADS_PER_GROUP, LANES), F32),
        grid=(N_GROUPS, flat // RELBIAS_CHUNK),
        in_specs=[pl.BlockSpec((1, HEADS_PER_GROUP, RELBIAS_CHUNK), lambda g, c: (g, 0, c)),
                  pl.BlockSpec((1, 1, RELBIAS_CHUNK), lambda g, c: (g, 0, c))],
        out_specs=pl.BlockSpec((1, HEADS_PER_GROUP, LANES), lambda g, c: (g, 0, 0)),
        compiler_params=_cparams("parallel", "arbitrary"),
    )(dbf, bkf)


def _my_position():
    x, y, c = lax.axis_index("x"), lax.axis_index("y"), lax.axis_index("c")
    return x, y, c


def _linear(pos):
    return 4 * pos[0] + 2 * pos[1] + pos[2]


def _peer(pos, k):
    x, y, c = pos
    return ((1 - x) if k & 4 else x, (1 - y) if k & 2 else y, (1 - c) if k & 1 else c)


HBM_SPEC = pl.BlockSpec(memory_space=pltpu.HBM)
SEM_SPEC = pl.BlockSpec(memory_space=pltpu.SEMAPHORE)
DATAFLOW = pltpu.SideEffectType.DATAFLOW_SIDE_EFFECTING


def _exchange_copies(src, land, sems, send_window, recv_window, with_arrivals):
    send_sems, recv_sems, local_sems = sems
    T = len(src)
    me = _my_position()
    me_lin = _linear(me)
    local = [pltpu.make_async_copy(send_window(t, src[t], me_lin), recv_window(t, land[t], me_lin),
                                   local_sems.at[t]) for t in range(T)]
    sends, arrivals = [], []
    for t in range(T):
        for k in range(1, N_DEV):
            peer = _peer(me, k)
            peer_lin = _linear(peer)
            sem = t * (N_DEV - 1) + k - 1
            sends.append(pltpu.make_async_remote_copy(
                src_ref=send_window(t, src[t], peer_lin), dst_ref=recv_window(t, land[t], me_lin),
                send_sem=send_sems.at[sem], recv_sem=recv_sems.at[sem],
                device_id=peer, device_id_type=MESH))
            if with_arrivals:
                arrivals.append(pltpu.make_async_remote_copy(
                    src_ref=send_window(t, src[t], me_lin), dst_ref=recv_window(t, land[t], peer_lin),
                    send_sem=send_sems.at[sem], recv_sem=recv_sems.at[sem],
                    device_id=peer, device_id_type=MESH))
    return local, sends, arrivals


def _exchange_start(srcs, land_shapes, send_window, recv_window, *, name, dep=None):
    T = len(srcs)
    n_in = 2 * T + (1 if dep is not None else 0)

    def body(*refs):
        src = refs[:T]
        land = refs[T:2 * T]
        sems = refs[n_in:n_in + 3]
        token = refs[-1]
        local, sends, _ = _exchange_copies(src, land, sems, send_window, recv_window, False)
        for cp in local + sends:
            cp.start()
        token[...] = jnp.zeros_like(token)

    lands = [lax.empty(ls.shape, ls.dtype) for ls in land_shapes]
    operands = [pltpu.with_memory_space_constraint(a, pltpu.HBM) for a in list(srcs) + lands]
    outs = pl.pallas_call(
        body, name=name,
        out_shape=(pltpu.SemaphoreType.DMA((T * (N_DEV - 1),)), pltpu.SemaphoreType.DMA((T * (N_DEV - 1),)),
                   pltpu.SemaphoreType.DMA((T,)),
                   *[pltpu.HBM(a.shape, a.dtype) for a in operands],
                   jax.ShapeDtypeStruct((8, LANES), F32)),
        in_specs=[HBM_SPEC] * (2 * T) + ([ANY] if dep is not None else []),
        out_specs=(SEM_SPEC,) * 3 + (HBM_SPEC,) * (2 * T) + (VMEM_SPEC,),
        input_output_aliases={i: 3 + i for i in range(2 * T)},
        compiler_params=pltpu.CompilerParams(has_side_effects=DATAFLOW),
    )(*operands, *([dep] if dep is not None else []))
    return outs[:3], outs[3:3 + T], outs[3 + T:3 + 2 * T], outs[-1]


def _exchange_wait(started, after, send_window, recv_window, *, name):
    sems, srcs, lands, _ = started
    T = len(srcs)

    def body(*refs):
        src = refs[:T]
        land = refs[T:2 * T]
        sem_refs = refs[2 * T:2 * T + 3]
        local, sends, arrivals = _exchange_copies(src, land, sem_refs, send_window, recv_window, True)
        for cp in arrivals:
            cp.wait_recv()
        for cp in sends:
            cp.wait_send()
        for cp in local:
            cp.wait()

    outs = pl.pallas_call(
        body, name=name,
        out_shape=tuple(pltpu.HBM(a.shape, a.dtype) for a in list(srcs) + list(lands)),
        in_specs=[HBM_SPEC] * (2 * T) + [SEM_SPEC] * 3 + [ANY],
        out_specs=(HBM_SPEC,) * (2 * T),
        input_output_aliases={i: i for i in range(2 * T)},
        compiler_params=pltpu.CompilerParams(has_side_effects=DATAFLOW),
    )(*srcs, *lands, *sems, after)
    return outs[T:]


def _shard_window(kind, width):
    def win(ref, lin):
        if kind == "slot":
            return ref.at[lin]
        if kind == "col":
            return ref.at[:, pl.ds(pl.multiple_of(lin * width, LANES), width)]
        if kind == "row":
            return ref.at[pl.ds(pl.multiple_of(lin * width, 8), width), :]
        if kind == "lcol":
            return ref.at[:, :, pl.ds(pl.multiple_of(lin * width, LANES), width)]
        if kind == "lrow":
            return ref.at[:, pl.ds(pl.multiple_of(lin * width, 8), width), :]
        raise ValueError(kind)
    return win


def _shard_windows(kinds, shard_shapes):
    return [_shard_window(k, (ss[-1] if k in ("col", "lcol") else ss[-2])) for k, ss in zip(kinds, shard_shapes)]


def _allgather_start(shards, kinds, full_shapes, *, name, dep=None):
    wins = _shard_windows(kinds, [s.shape for s in shards])
    send_window = lambda t, ref, lin: ref
    recv_window = lambda t, ref, lin: wins[t](ref, lin)
    started = _exchange_start(shards, [jax.ShapeDtypeStruct(fs, s.dtype) for fs, s in zip(full_shapes, shards)],
                              send_window, recv_window, name=name + "_start", dep=dep)
    return started, lambda after: _exchange_wait(started, after, send_window, recv_window, name=name + "_wait")


def _scatter_start(fulls, kinds, shard_shapes, *, name):
    wins = _shard_windows(kinds, shard_shapes)
    send_window = lambda t, ref, lin: wins[t](ref, lin)
    recv_window = lambda t, ref, lin: ref.at[lin]
    started = _exchange_start(
        fulls, [jax.ShapeDtypeStruct((N_DEV,) + tuple(ss), f.dtype) for ss, f in zip(shard_shapes, fulls)],
        send_window, recv_window, name=name + "_start")
    return started, lambda after: _exchange_wait(started, after, send_window, recv_window, name=name + "_wait")


def _small_gather(pack, *, reduce, name):
    R = pack.shape[0]

    def body(p_ref, o_ref, *rest):
        if reduce:
            buf, send_sems, recv_sems = rest
        else:
            buf = o_ref
            send_sems, recv_sems = rest
        me = _my_position()
        me_lin = _linear(me)
        buf[me_lin] = p_ref[...]
        sends = []
        for k in range(1, N_DEV):
            peer = _peer(me, k)
            cp = pltpu.make_async_remote_copy(
                src_ref=p_ref, dst_ref=buf.at[me_lin],
                send_sem=send_sems.at[k - 1], recv_sem=recv_sems.at[k - 1],
                device_id=peer, device_id_type=MESH)
            cp.start()
            sends.append(cp)
        for k in range(1, N_DEV):
            peer = _peer(me, k)
            pltpu.make_async_remote_copy(
                src_ref=p_ref, dst_ref=buf.at[_linear(peer)],
                send_sem=send_sems.at[k - 1], recv_sem=recv_sems.at[k - 1],
                device_id=peer, device_id_type=MESH).wait_recv()
        for cp in sends:
            cp.wait_send()
        if reduce:
            acc = buf[0]
            for s in range(1, N_DEV):
                acc = acc + buf[s]
            o_ref[...] = acc

    scratch = [pltpu.SemaphoreType.DMA((N_DEV - 1,)), pltpu.SemaphoreType.DMA((N_DEV - 1,))]
    if reduce:
        scratch = [pltpu.VMEM((N_DEV, R, LANES), F32)] + scratch
        out_shape = jax.ShapeDtypeStruct((R, LANES), F32)
    else:
        out_shape = jax.ShapeDtypeStruct((N_DEV, R, LANES), F32)
    return pl.pallas_call(
        body, name=name, out_shape=out_shape,
        in_specs=[VMEM_SPEC], out_specs=VMEM_SPEC, scratch_shapes=scratch,
        compiler_params=pltpu.CompilerParams(has_side_effects=True, vmem_limit_bytes=VMEM_LIMIT),
    )(pack)


def _adamw_math(w, g, m, v):
    m = ADAM_B1 * m + (1.0 - ADAM_B1) * g
    v = ADAM_B2 * v + (1.0 - ADAM_B2) * jnp.square(g)
    m_hat = m / (1.0 - ADAM_B1 ** ADAM_STEP)
    v_hat = v / (1.0 - ADAM_B2 ** ADAM_STEP)
    delta = -ADAM_LR * (m_hat / (jnp.sqrt(v_hat) + ADAM_EPS) + ADAM_WD * w)
    return delta, m, v


def _adamw_from_partials(parts, w, m, v, *, name):
    R, C = w.shape
    rl = parts[0].shape[1]
    assert all(p.shape == (N_DEV, rl, C) for p in parts) and rl * len(parts) == R
    tr = _tile(rl, (256, 128, 64, 32, 16))
    per = rl // tr
    L = len(parts)

    def body(*refs):
        p_refs = refs[:L]
        w_ref, m_ref, v_ref, g_ref, d_ref, nm_ref, nv_ref = refs[L:]
        i = pl.program_id(0)
        for l in range(L):
            @pl.when((i >= l * per) & (i < (l + 1) * per))
            def _(l=l):
                p_ref = p_refs[l]
                g = p_ref[0].astype(F32)
                for s in range(1, N_DEV):
                    g = g + p_ref[s].astype(F32)
                d, nm, nv = _adamw_math(w_ref[...], g, m_ref[...], v_ref[...])
                g_ref[...] = g
                d_ref[...] = d
                nm_ref[...] = nm
                nv_ref[...] = nv

    blk = pl.BlockSpec((tr, C), lambda i: (i, 0))
    part_specs = [pl.BlockSpec((N_DEV, tr, C), lambda i, l=l: (0, jnp.clip(i - l * per, 0, per - 1), 0))
                  for l in range(L)]
    return pl.pallas_call(
        body, name=name,
        out_shape=(jax.ShapeDtypeStruct((R, C), F32),) * 4,
        grid=(R // tr,),
        in_specs=part_specs + [blk, blk, blk],
        out_specs=(blk,) * 4,
        compiler_params=_cparams("parallel"),
    )(*parts, w, m, v)


def _adamw_small(g, w, m, v, *, name):
    def body(g_ref, w_ref, m_ref, v_ref, d_ref, nm_ref, nv_ref):
        d, nm, nv = _adamw_math(w_ref[...], g_ref[...], m_ref[...], v_ref[...])
        d_ref[...] = d
        nm_ref[...] = nm
        nv_ref[...] = nv

    return pl.pallas_call(
        body, name=name,
        out_shape=(jax.ShapeDtypeStruct(g.shape, F32),) * 3,
        in_specs=[VMEM_SPEC] * 4, out_specs=(VMEM_SPEC,) * 3,
    )(g, w, m, v)


def _pack_rows(pieces):
    flat = jnp.concatenate([p.reshape(-1) for p in pieces])
    n = flat.shape[0]
    padded = -(-n // (8 * LANES)) * (8 * LANES)
    return jnp.pad(flat, (0, padded - n)).reshape(padded // LANES, LANES)


def _unpack_rows(pack, shapes):
    flat = pack.reshape(-1)
    out, pos = [], 0
    for s in shapes:
        n = int(np.prod(s))
        out.append(flat[pos:pos + n].reshape(s))
        pos += n
    return out


def kernel(x, rel_bias, ab_norm, ab_w_in, ab_conv_w, ab_conv_b, ab_ln_g, ab_ln_b, ab_w_out, sc_norm, sc_w_in, sc_conv_w, sc_w_out, mlp_norm, mlp_w_up, mlp_w_down, final_norm, loss_target, m_rel_bias, m_ab_norm, m_ab_w_in, m_ab_conv_w, m_ab_conv_b, m_ab_ln_g, m_ab_ln_b, m_ab_w_out, m_sc_norm, m_sc_w_in, m_sc_conv_w, m_sc_w_out, m_mlp_norm, m_mlp_w_up, m_mlp_w_down, m_final_norm, v_rel_bias, v_ab_norm, v_ab_w_in, v_ab_conv_w, v_ab_conv_b, v_ab_ln_g, v_ab_ln_b, v_ab_w_out, v_sc_norm, v_sc_w_in, v_sc_conv_w, v_sc_w_out, v_mlp_norm, v_mlp_w_up, v_mlp_w_down, v_final_norm):
    S, D = x.shape[1], x.shape[2]
    CA = ab_conv_b.shape[1]
    C2 = 2 * CA
    AB_IN = C2 + ATTN_IN
    me_lin = _linear(_my_position())
    xs = x.reshape(S, D)
    tgt = loss_target.reshape(S, D)

    cw_sh = ab_conv_w.shape[2]
    scn_sh = sc_norm.shape[1]
    scw_sh = sc_conv_w.shape[2]
    small_sh_shapes = [(CONV_A_WIDTH, cw_sh), (scn_sh,), (SC_CONV_WIDTH, scw_sh)]
    small_pack = _pack_rows([ab_conv_w[0], sc_norm[0], sc_conv_w[0]])
    w_in_sh = ab_w_in[0].astype(BF16)
    ag_ab, wait_ab = _allgather_start(
        [small_pack, w_in_sh, ab_w_out[0].astype(BF16)], ["slot", "slot", "row"],
        [(N_DEV,) + small_pack.shape, (N_DEV,) + w_in_sh.shape, (N_DEV * ab_w_out.shape[1], D)],
        name="allgather_ab")
    ag_mlp, wait_mlp = _allgather_start(
        [mlp_w_up.astype(BF16), mlp_w_down.astype(BF16)], ["lcol", "lrow"],
        [(2, D, N_DEV * mlp_w_up.shape[2]), (2, N_DEV * mlp_w_down.shape[1], D)], name="allgather_mlp",
        dep=ag_ab[3])
    ag_sc, wait_sc = _allgather_start(
        [sc_w_in[0].astype(BF16), sc_w_out[0].astype(BF16)], ["col", "row"],
        [(D, N_DEV * sc_w_in.shape[2]), (N_DEV * sc_w_out.shape[1], D)], name="allgather_sc",
        dep=ag_mlp[3])

    buckets = _bucket_tables()
    biases = _bias_tables(rel_bias, buckets, name="bias_tables")

    dils = [dil for _, dil in DILATED_GROUPS]
    n0_all = _rmsnorm_fwd(xs, ab_norm, name="norm_ab", dep=ag_sc[3], views=dils[1:])
    n0 = n0_all[0]
    small_params, w_in_g, w_out = wait_ab(n0)
    per_dev = [_unpack_rows(small_params[s], small_sh_shapes) for s in range(N_DEV)]
    conv_w_full = jnp.concatenate([p[0] for p in per_dev], axis=1)
    sc_norm_full = jnp.concatenate([p[1] for p in per_dev], axis=0)[None]
    sc_conv_full = jnp.concatenate([p[2] for p in per_dev], axis=1)
    w_in = jnp.transpose(w_in_g, (1, 0, 2)).reshape(D, AB_IN)
    w_c = w_in[:, :C2]
    w_q = w_in[:, C2:]
    w_grp = [jnp.concatenate([w_q[:, t * N_GROUPS * ATTN_OUT + g * ATTN_OUT:][:, :ATTN_OUT] for t in range(3)], axis=1)
             for g in range(N_GROUPS)]
    uc = _mm_nn(n0, w_c, out_dtype=F32, name="mm_ab_in_conv", wide=True)
    uqs = [_mm_nn(n0_all[g], w_grp[g], out_dtype=BF16, slabs=dils[g], name=f"mm_ab_in_qkv{g}")
           for g in range(N_GROUPS)]
    ya, hglu, ct = _conv_fwd(uc, conv_w_full, ab_conv_b, ab_ln_g, ab_ln_b, name="conv_fwd")
    outs, lses = zip(*[_attn_fwd(uqs[g], biases, g, dils[g], name=f"attn_fwd_{g}") for g in range(N_GROUPS)])
    cat, outf, lse = _attn_merge(outs, lses, ya, name="attn_merge")
    h1 = _mm_nn(cat, w_out, out_dtype=F32, residual=xs, name="mm_ab_out", wide=True)
    w_up, w_dn = wait_mlp(h1)
    n1, z0 = _norm_mm_nn(h1, mlp_norm[0:1], (w_up, 0), out_dtype=BF16, name="norm_mm_up0", wide=True)
    h2 = _mm_nn(z0, (w_dn, 0), out_dtype=F32, residual=h1, a_fn=_relu_sq, name="mm_down0", wide=True)
    w_sc_in, w_sc_out = wait_sc(h2)
    n2, u3 = _norm_mm_nn(h2, sc_norm_full, w_sc_in, out_dtype=BF16, name="norm_mm_sc_in", wide=True)
    ysc = _sc_fwd(u3, sc_conv_full, name="sc_fwd")
    h3 = _mm_nn(ysc, w_sc_out, out_dtype=F32, residual=h2, name="mm_sc_out", wide=True)
    n3, z1 = _norm_mm_nn(h3, mlp_norm[1:2], (w_up, 1), out_dtype=BF16, name="norm_mm_up1", wide=True)
    h4 = _mm_nn(z1, (w_dn, 1), out_dtype=F32, residual=h3, a_fn=_relu_sq, name="mm_down1", wide=True)

    def dz_epilogue(acc, z):
        return acc * (2.0 * jnp.maximum(z.astype(F32), 0.0))

    dh4, dh4b, acc_final = _loss_bwd(h4, tgt, final_norm[None], name="loss_bwd")
    dz1 = _mm_nt([(dh4b, (w_dn, 1))], out_dtype=BF16, epilogue=dz_epilogue, extra=z1, name="mm_d_down1")
    g_dn1 = _mm_tn(z1, dh4b, a_fn=_relu_sq, name="mm_gw_down1")
    g_up1 = _mm_tn(n3, dz1, name="mm_gw_up1")
    rs_mlp1, wait_rs_mlp1 = _scatter_start([g_up1, g_dn1], ["col", "row"],
                                           [mlp_w_up.shape[1:], mlp_w_down.shape[1:]], name="scatter_mlp1")
    dh3, dh3b, acc_mlp1 = _mm_nt_rms_bwd(dz1, (w_up, 1), h3, mlp_norm[1:2], dh4, name="mm_d_up1_norm_bwd",
                                         dep=rs_mlp1[3])

    dysc = _mm_nt([(dh3b, w_sc_out)], out_dtype=F32, name="mm_d_sc_out")
    g_sc_out = _mm_tn(ysc, dh3b, name="mm_gw_sc_out")
    du3, acc_scw = _sc_bwd(u3, dysc, sc_conv_full, name="sc_bwd")
    g_sc_in = _mm_tn(n2, du3, name="mm_gw_sc_in")
    rs_sc, wait_rs_sc = _scatter_start([g_sc_in, g_sc_out], ["col", "row"],
                                       [sc_w_in.shape[1:], sc_w_out.shape[1:]], name="scatter_sc")
    dh2, dh2b, acc_sc = _mm_nt_rms_bwd(du3, w_sc_in, h2, sc_norm_full, dh3, name="mm_d_sc_in_norm_bwd",
                                       dep=rs_sc[3])

    dz0 = _mm_nt([(dh2b, (w_dn, 0))], out_dtype=BF16, epilogue=dz_epilogue, extra=z0, name="mm_d_down0")
    g_dn0 = _mm_tn(z0, dh2b, a_fn=_relu_sq, name="mm_gw_down0")
    g_up0 = _mm_tn(n1, dz0, name="mm_gw_up0")
    rs_mlp0, wait_rs_mlp0 = _scatter_start([g_up0, g_dn0], ["col", "row"],
                                           [mlp_w_up.shape[1:], mlp_w_down.shape[1:]], name="scatter_mlp0")
    dh1, dh1b, acc_mlp0 = _mm_nt_rms_bwd(dz0, (w_up, 0), h1, mlp_norm[0:1], dh2, name="mm_d_up0_norm_bwd",
                                         dep=rs_mlp0[3])

    dcat = _mm_nt([(dh1b, w_out)], out_dtype=F32, name="mm_d_ab_out")
    g_ab_out = _mm_tn(cat, dh1b, name="mm_gw_ab_out")
    prep = _attn_prep(dcat, outf, lse, name="attn_prep")
    dqkv, dbs = zip(*[_attn_bwd(uqs[g], prep[g][0], prep[g][2], prep[g][1], biases, g, dils[g],
                                name=f"attn_bwd_{g}") for g in range(N_GROUPS)])
    drel = _relbias_reduce(dbs, buckets, name="relbias_reduce")
    dc, acc_conv = _conv_bwd_ln(ct, dcat, hglu, ab_ln_g, ab_ln_b, name="conv_bwd_ln")
    duc = _conv_bwd_in(dc, uc, conv_w_full, name="conv_bwd_in")
    g_wc = _mm_tn(n0, duc, name="mm_gw_ab_in_conv")
    g_wgrp = [_mm_tn(n0_all[g], dqkv[g], slabs=dils[g], name=f"mm_gw_ab_in_qkv{g}") for g in range(N_GROUPS)]
    g_wq = jnp.concatenate([g_wgrp[g][:, t * ATTN_OUT:(t + 1) * ATTN_OUT]
                            for t in range(3) for g in range(N_GROUPS)], axis=1)
    g_w_in = jnp.concatenate([g_wc, g_wq], axis=1).reshape(D, N_DEV, AB_IN // N_DEV).transpose(1, 0, 2)
    rs_ab, wait_rs_ab = _scatter_start([g_w_in, g_ab_out], ["slot", "row"],
                                       [w_in_sh.shape, ab_w_out.shape[1:]], name="scatter_ab")
    dn0 = _mm_nt([(duc, w_c), (dqkv[0], w_grp[0])], out_dtype=F32, name="mm_d_ab_in", dep=rs_ab[3])
    dn0_views = [(_mm_nt([(dqkv[g], w_grp[g])], out_dtype=F32, slabs=dils[g], name=f"mm_d_ab_in_qkv{g}"), dils[g])
                 for g in range(1, N_GROUPS)]
    grad_x, grad_xb, acc_ab = _rms_bwd(xs, ab_norm, dn0, dh1, name="norm_ab_bwd", dn_views=dn0_views)

    small_full = [drel[:, :, :NUM_BUCKETS].transpose(2, 0, 1).reshape(NUM_BUCKETS, N_GROUPS * HEADS_PER_GROUP),
                  acc_ab[0], acc_conv[0:CONV_A_WIDTH], acc_conv[32],
                  acc_conv[33], acc_conv[34], acc_sc[0], acc_scw[0:SC_CONV_WIDTH],
                  jnp.stack([acc_mlp0[0], acc_mlp1[0]]), acc_final[0], acc_final[1]]
    small_full_shapes = [p.shape for p in small_full]
    summed = _unpack_rows(_small_gather(_pack_rows(small_full), reduce=True, name="allreduce_small"),
                          small_full_shapes)
    (s_rel, s_abn, s_cw, s_cb, s_lg, s_lb, s_scn, s_scw, s_mlpn, s_fn, s_err) = summed
    loss = (0.5 / D) * jnp.sum(s_err)
    small_grads = {
        "rel_bias": s_rel, "ab_norm": s_abn[None],
        "ab_conv_w": lax.dynamic_slice_in_dim(s_cw, me_lin * cw_sh, cw_sh, axis=1)[None],
        "ab_conv_b": s_cb[None], "ab_ln_g": s_lg[None], "ab_ln_b": s_lb[None],
        "sc_norm": lax.dynamic_slice_in_dim(s_scn, me_lin * scn_sh, scn_sh, axis=0)[None],
        "sc_conv_w": lax.dynamic_slice_in_dim(s_scw, me_lin * scw_sh, scw_sh, axis=1)[None],
        "mlp_norm": s_mlpn, "final_norm": s_fn,
    }
    small_w = {"rel_bias": (rel_bias, m_rel_bias, v_rel_bias), "ab_norm": (ab_norm, m_ab_norm, v_ab_norm),
               "ab_conv_w": (ab_conv_w, m_ab_conv_w, v_ab_conv_w), "ab_conv_b": (ab_conv_b, m_ab_conv_b, v_ab_conv_b),
               "ab_ln_g": (ab_ln_g, m_ab_ln_g, v_ab_ln_g), "ab_ln_b": (ab_ln_b, m_ab_ln_b, v_ab_ln_b),
               "sc_norm": (sc_norm, m_sc_norm, v_sc_norm), "sc_conv_w": (sc_conv_w, m_sc_conv_w, v_sc_conv_w),
               "mlp_norm": (mlp_norm, m_mlp_norm, v_mlp_norm), "final_norm": (final_norm, m_final_norm, v_final_norm)}
    small_names = list(small_grads)
    small_shapes = [small_grads[n].shape for n in small_names]
    d_pack, m_pack, v_pack = _adamw_small(
        _pack_rows([small_grads[n] for n in small_names]), _pack_rows([small_w[n][0] for n in small_names]),
        _pack_rows([small_w[n][1] for n in small_names]), _pack_rows([small_w[n][2] for n in small_names]),
        name="adamw_small")
    small = {n: (small_grads[n], d, nm_, nv_) for n, d, nm_, nv_ in zip(
        small_names, _unpack_rows(d_pack, small_shapes), _unpack_rows(m_pack, small_shapes),
        _unpack_rows(v_pack, small_shapes))}

    p_up1, p_dn1 = wait_rs_mlp1(grad_xb)
    p_sc_in, p_sc_out = wait_rs_sc(grad_xb)
    p_up0, p_dn0 = wait_rs_mlp0(grad_xb)
    p_w_in, p_ab_out = wait_rs_ab(grad_xb)
    big = {}
    for nm, parts, w, m, v in (("ab_w_in", [p_w_in], ab_w_in, m_ab_w_in, v_ab_w_in),
                               ("ab_w_out", [p_ab_out], ab_w_out, m_ab_w_out, v_ab_w_out),
                               ("sc_w_in", [p_sc_in], sc_w_in, m_sc_w_in, v_sc_w_in),
                               ("sc_w_out", [p_sc_out], sc_w_out, m_sc_w_out, v_sc_w_out),
                               ("mlp_w_up", [p_up0, p_up1], mlp_w_up, m_mlp_w_up, v_mlp_w_up),
                               ("mlp_w_down", [p_dn0, p_dn1], mlp_w_down, m_mlp_w_down, v_mlp_w_down)):
        C = w.shape[-1]
        res = _adamw_from_partials(parts, w.reshape(-1, C), m.reshape(-1, C), v.reshape(-1, C), name="adamw_" + nm)
        big[nm] = tuple(r.reshape(w.shape) for r in res)

    order = ["rel_bias", "ab_norm", "ab_w_in", "ab_conv_w", "ab_conv_b", "ab_ln_g", "ab_ln_b", "ab_w_out",
             "sc_norm", "sc_w_in", "sc_conv_w", "sc_w_out", "mlp_norm", "mlp_w_up", "mlp_w_down", "final_norm"]
    allres = {**big, **small}
    return (loss, grad_x.reshape(x.shape),
            *[allres[n][0] for n in order], *[allres[n][1] for n in order],
            *[allres[n][2] for n in order], *[allres[n][3] for n in order])
```

```python
import functools
import math

import numpy as np
import jax
import jax.numpy as jnp
from jax import lax
from jax.experimental import pallas as pl
from jax.experimental.pallas import tpu as pltpu

F32 = jnp.float32
BF16 = jnp.bfloat16

HEAD_DIM = 64
HEADS_PER_GROUP = 8
DILATED_GROUPS = ((128, 1), (512, 4), (2048, 16))
N_GROUPS = 3
ATTN_OUT = HEADS_PER_GROUP * HEAD_DIM
ATTN_IN = 3 * N_GROUPS * ATTN_OUT
GROUP_QKV = 3 * ATTN_OUT
ATTN_BLK = 128
CONV_A_WIDTH = 31
SC_CONV_WIDTH = 3
NUM_BUCKETS = 32
REL_MAX_DISTANCE = 2048
RMS_EPS = 1e-6
LN_EPS = 1e-5
NEG_INF = -1e30
ADAM_LR = 0.001
ADAM_B1 = 0.9
ADAM_B2 = 0.999
ADAM_EPS = 1e-08
ADAM_WD = 0.01
ADAM_STEP = 10

N_DEV = 8
HALO = 32
LANES = 128
VMEM_LIMIT = 56 * 1024 * 1024
MESH = pl.DeviceIdType.MESH
ANY = pl.BlockSpec(memory_space=pl.ANY)
VMEM_SPEC = pl.BlockSpec(memory_space=pltpu.VMEM)


def _tile(n, prefs):
    for t in prefs:
        if n % t == 0:
            return t
    return n


def _cparams(*sem):
    return pltpu.CompilerParams(dimension_semantics=sem, vmem_limit_bytes=VMEM_LIMIT)


def _relu_sq(z):
    return jnp.square(jnp.maximum(z, 0))


def _dot_nt(a, b):
    return lax.dot_general(a, b, (((1,), (1,)), ((), ())), preferred_element_type=F32)


def _dot_tn(a, b):
    return lax.dot_general(a, b, (((0,), (0,)), ((), ())), preferred_element_type=F32)


def _weight(b):
    if not isinstance(b, tuple):
        return b, b.shape, pl.BlockSpec
    arr, layer = b

    def spec(block, index_map):
        return pl.BlockSpec((None,) + tuple(block), lambda *g: (layer,) + tuple(index_map(*g)))

    return arr, arr.shape[1:], spec


def _mm_nn(a, b, *, out_dtype, name, residual=None, a_fn=None, slabs=1, wide=False):
    M, K = a.shape
    K //= slabs
    b, (_, N), b_spec = _weight(b)
    tm = _tile(M, (1024, 512, 256) if wide else (2048, 1024, 512, 256))
    tn = _tile(N, (1024, 512, 384, 256, 128) if wide else (512, 384, 256, 128))
    tk = _tile(K, (2048, 1024, 512, 256, 128) if wide else (1024, 512, 256, 128))
    nk = K // tk
    nj = N // tn
    has_res = residual is not None

    def body(*refs):
        if has_res:
            a_ref, b_ref, r_ref, o_ref = refs[:4]
        else:
            a_ref, b_ref, o_ref = refs[:3]
        av = a_ref[...]
        if a_fn is not None:
            av = a_fn(av)
        part = jnp.dot(av, b_ref[...], preferred_element_type=F32)

        def finish(acc):
            if has_res:
                acc = acc + r_ref[...]
            o_ref[...] = acc.astype(o_ref.dtype)

        if nk == 1:
            finish(part)
        else:
            acc_ref = refs[-1]
            k = pl.program_id(2)

            @pl.when(k == 0)
            def _():
                acc_ref[...] = part

            @pl.when((k > 0) & (k < nk - 1))
            def _():
                acc_ref[...] += part

            @pl.when(k == nk - 1)
            def _():
                finish(acc_ref[...] + part)

    in_specs = [pl.BlockSpec((tm, tk), lambda i, j, k: (i, (j // nj) * nk + k)),
                b_spec((tk, tn), lambda i, j, k: (k, j % nj))]
    args = [a, b]
    if has_res:
        in_specs.append(pl.BlockSpec((tm, tn), lambda i, j, k: (i, j)))
        args.append(residual)
    return pl.pallas_call(
        body, name=name,
        out_shape=jax.ShapeDtypeStruct((M, slabs * N), out_dtype),
        grid=(M // tm, slabs * nj, nk),
        in_specs=in_specs,
        out_specs=pl.BlockSpec((tm, tn), lambda i, j, k: (i, j)),
        scratch_shapes=[pltpu.VMEM((tm, tn), F32)] if nk > 1 else [],
        compiler_params=_cparams("parallel", "parallel", "arbitrary"),
    )(*args)


def _norm_mm_nn(h, g, b, *, out_dtype, name, wide=False):
    M, K = h.shape
    b, (_, N), b_spec = _weight(b)
    tm = _tile(M, (2048, 1024, 512, 256))
    tn = _tile(N, (1024, 512, 384, 256, 128) if wide else (512, 384, 256, 128))

    def body(h_ref, g_ref, b_ref, n_ref, o_ref):
        @pl.when(pl.program_id(1) == 0)
        def _():
            x = h_ref[...]
            r = lax.rsqrt(jnp.mean(x * x, axis=-1, keepdims=True) + RMS_EPS)
            n_ref[...] = (x * r * g_ref[...]).astype(BF16)

        o_ref[...] = jnp.dot(n_ref[...], b_ref[...], preferred_element_type=F32).astype(o_ref.dtype)

    return pl.pallas_call(
        body, name=name,
        out_shape=(jax.ShapeDtypeStruct((M, K), BF16), jax.ShapeDtypeStruct((M, N), out_dtype)),
        grid=(M // tm, N // tn),
        in_specs=[pl.BlockSpec((tm, K), lambda i, j: (i, 0)), pl.BlockSpec((1, K), lambda i, j: (0, 0)),
                  b_spec((K, tn), lambda i, j: (0, j))],
        out_specs=(pl.BlockSpec((tm, K), lambda i, j: (i, 0)), pl.BlockSpec((tm, tn), lambda i, j: (i, j))),
        compiler_params=_cparams("parallel", "arbitrary"),
    )(h, g, b)


def _mm_nt(pairs, *, out_dtype, name, epilogue=None, extra=None, dep=None, slabs=1):
    assert slabs == 1 or len(pairs) == 1
    M = pairs[0][0].shape[0]
    weights = [_weight(p[1]) for p in pairs]
    Ko = weights[0][1][0]
    tm = _tile(M, (2048, 1024, 512, 256))
    to = _tile(Ko, (1024, 512, 256, 128))
    njo = Ko // to
    tks = [_tile(p[0].shape[1] // slabs, (1024, 768, 512, 256, 128)) for p in pairs]
    steps = [p[0].shape[1] // slabs // tk for p, tk in zip(pairs, tks)]
    offs = [sum(steps[:i]) for i in range(len(pairs))]
    nk = sum(steps)
    npair = len(pairs)
    has_extra = extra is not None

    def body(*refs):
        ab = refs[:2 * npair]
        pos = 2 * npair
        e_ref = None
        if has_extra:
            e_ref = refs[pos]
            pos += 1
        if dep is not None:
            pos += 1
        o_ref = refs[pos]
        acc_ref = refs[pos + 1]
        k = pl.program_id(2)

        @pl.when(k == 0)
        def _():
            acc_ref[...] = jnp.zeros_like(acc_ref)

        for p in range(npair):
            @pl.when((k >= offs[p]) & (k < offs[p] + steps[p]))
            def _(p=p):
                acc_ref[...] += _dot_nt(ab[2 * p][...], ab[2 * p + 1][...])

        @pl.when(k == nk - 1)
        def _():
            acc = acc_ref[...]
            if epilogue is not None:
                acc = epilogue(acc, e_ref[...] if has_extra else None)
            o_ref[...] = acc.astype(o_ref.dtype)

    in_specs, args = [], []
    for p, (a, b) in enumerate(pairs):
        def kidx(k, p=p):
            return jnp.clip(k - offs[p], 0, steps[p] - 1)
        in_specs.append(pl.BlockSpec((tm, tks[p]),
                                     lambda i, j, k, kidx=kidx, p=p: (i, (j // njo) * steps[p] + kidx(k))))
        in_specs.append(weights[p][2]((to, tks[p]), lambda i, j, k, kidx=kidx: (j % njo, kidx(k))))
        args += [a, weights[p][0]]
    if has_extra:
        in_specs.append(pl.BlockSpec((tm, to), lambda i, j, k: (i, j)))
        args.append(extra)
    if dep is not None:
        in_specs.append(ANY)
        args.append(dep)
    return pl.pallas_call(
        body, name=name,
        out_shape=jax.ShapeDtypeStruct((M, slabs * Ko), out_dtype),
        grid=(M // tm, slabs * njo, nk),
        in_specs=in_specs,
        out_specs=pl.BlockSpec((tm, to), lambda i, j, k: (i, j)),
        scratch_shapes=[pltpu.VMEM((tm, to), F32)],
        compiler_params=_cparams("parallel", "parallel", "arbitrary"),
    )(*args)


def _mm_tn(a, b, *, name, a_fn=None, slabs=1):
    M, K = a.shape
    K //= slabs
    N = b.shape[1] // slabs
    tm = _tile(M, (4096, 2048, 1024, 512, 256))
    tk = _tile(K, (1024, 768, 512, 384, 256, 128))
    tn = _tile(N, (1024, 768, 512, 384, 256, 128))
    nmi = M // tm
    nm = slabs * nmi
    nki, nnj = K // tk, N // tn

    def body(a_ref, b_ref, o_ref, acc_ref):
        m = pl.program_id(2)
        av = a_ref[...]
        if a_fn is not None:
            av = a_fn(av)
        part = _dot_tn(av, b_ref[...])
        if nm == 1:
            o_ref[...] = part.astype(o_ref.dtype)
            return

        @pl.when(m == 0)
        def _():
            acc_ref[...] = part

        @pl.when((m > 0) & (m < nm - 1))
        def _():
            acc_ref[...] += part

        @pl.when(m == nm - 1)
        def _():
            o_ref[...] = (acc_ref[...] + part).astype(o_ref.dtype)

    return pl.pallas_call(
        body, name=name,
        out_shape=jax.ShapeDtypeStruct((K, N), BF16),
        grid=(K // tk, N // tn, nm),
        in_specs=[pl.BlockSpec((tm, tk), lambda i, j, m: (m % nmi, (m // nmi) * nki + i)),
                  pl.BlockSpec((tm, tn), lambda i, j, m: (m % nmi, (m // nmi) * nnj + j))],
        out_specs=pl.BlockSpec((tk, tn), lambda i, j, m: (i, j)),
        scratch_shapes=[pltpu.VMEM((tk, tn), F32)],
        compiler_params=_cparams("parallel", "parallel", "arbitrary"),
    )(a, b)


def _rmsnorm_fwd(h, g, *, name, dep=None, views=()):
    S, D = h.shape
    tm = _tile(S, (512, 256))
    nv = len(views)

    def body(h_ref, g_ref, *rest):
        n_out = 1 + nv
        outs = rest[len(rest) - n_out - (1 if nv else 0):len(rest) - (1 if nv else 0)]
        x = h_ref[...]
        r = lax.rsqrt(jnp.mean(x * x, axis=-1, keepdims=True) + RMS_EPS)
        y = x * r * g_ref[...]
        outs[0][...] = y.astype(BF16)
        if nv:
            scr = rest[-1]
            _to_chunks(scr, y)
            for v_ref, d in zip(outs[1:], views):
                _slabs_from_chunks(v_ref, scr, d, BF16)

    res = pl.pallas_call(
        body, name=name,
        out_shape=(jax.ShapeDtypeStruct((S, D), BF16),)
        + tuple(jax.ShapeDtypeStruct((S // d, d * D), BF16) for d in views),
        grid=(S // tm,),
        in_specs=[pl.BlockSpec((tm, D), lambda i: (i, 0)), pl.BlockSpec((1, D), lambda i: (0, 0))]
        + ([ANY] if dep is not None else []),
        out_specs=(pl.BlockSpec((tm, D), lambda i: (i, 0)),)
        + tuple(pl.BlockSpec((tm // d, d * D), lambda i: (i, 0)) for d in views),
        scratch_shapes=[_chunk_scratch(tm, D)] if nv else [],
        compiler_params=_cparams("parallel"),
    )(h, g, *([dep] if dep is not None else []))
    return res if nv else res[0]


def _rms_bwd_rows(x, g, dy):
    r = lax.rsqrt(jnp.mean(x * x, axis=-1, keepdims=True) + RMS_EPS)
    xh = x * r
    gy = dy * g
    dx = r * (gy - xh * jnp.mean(xh * gy, axis=-1, keepdims=True))
    return dx, dy * xh


def _rms_bwd(x, g, dn, dres, *, name, dn_views=()):
    S, D = x.shape
    tm = _tile(S, (512, 256))
    nv = len(dn_views)

    def body(x_ref, g_ref, dn_ref, dr_ref, *rest):
        v_refs = rest[:nv]
        dx_ref, dxb_ref, dg_ref = rest[nv:nv + 3]
        scr = rest[nv + 3:]
        i = pl.program_id(0)
        dn = dn_ref[...]
        for v_ref, s_ref, (_, d) in zip(v_refs, scr, dn_views):
            _chunks_from_slabs(s_ref, v_ref, d)
            dn = dn + _from_chunks(s_ref)
        dx, dgx = _rms_bwd_rows(x_ref[...], g_ref[...], dn)
        tot = dr_ref[...] + dx
        dx_ref[...] = tot
        dxb_ref[...] = tot.astype(BF16)

        @pl.when(i == 0)
        def _():
            dg_ref[...] = jnp.zeros_like(dg_ref)

        dg_ref[0:1, :] += jnp.sum(dgx, axis=0, keepdims=True)

    row = pl.BlockSpec((tm, D), lambda i: (i, 0))
    return pl.pallas_call(
        body, name=name,
        out_shape=(jax.ShapeDtypeStruct((S, D), F32), jax.ShapeDtypeStruct((S, D), BF16),
                   jax.ShapeDtypeStruct((8, D), F32)),
        grid=(S // tm,),
        in_specs=[row, pl.BlockSpec((1, D), lambda i: (0, 0)), row, row]
        + [pl.BlockSpec((tm // d, d * D), lambda i: (i, 0)) for _, d in dn_views],
        out_specs=(row, row, pl.BlockSpec((8, D), lambda i: (0, 0))),
        scratch_shapes=[_chunk_scratch(tm, D)] * nv,
        compiler_params=_cparams("arbitrary"),
    )(x, g, dn, dres, *[a for a, _ in dn_views])


def _mm_nt_rms_bwd(a, b, x, g, dres, *, name, dep=None):
    M, N = a.shape
    b, (D, _), b_spec = _weight(b)
    tm = _tile(M, (1024, 512, 256))
    tk = _tile(N, (1024, 512, 256, 128))
    nk = N // tk

    def body(a_ref, b_ref, x_ref, g_ref, dr_ref, *rest):
        dx_ref, dxb_ref, dg_ref, acc_ref = rest[-4:]
        i = pl.program_id(0)
        k = pl.program_id(1)
        part = _dot_nt(a_ref[...], b_ref[...])

        @pl.when((i == 0) & (k == 0))
        def _():
            dg_ref[...] = jnp.zeros_like(dg_ref)

        @pl.when(k == 0)
        def _():
            acc_ref[...] = part

        @pl.when((k > 0) & (k < nk - 1))
        def _():
            acc_ref[...] += part

        @pl.when(k == nk - 1)
        def _():
            dn = part if nk == 1 else acc_ref[...] + part
            dx, dgx = _rms_bwd_rows(x_ref[...], g_ref[...], dn)
            tot = dr_ref[...] + dx
            dx_ref[...] = tot
            dxb_ref[...] = tot.astype(BF16)
            dg_ref[0:1, :] += jnp.sum(dgx, axis=0, keepdims=True)

    row = pl.BlockSpec((tm, D), lambda i, k: (i, 0))
    in_specs = [pl.BlockSpec((tm, tk), lambda i, k: (i, k)), b_spec((D, tk), lambda i, k: (0, k)),
                row, pl.BlockSpec((1, D), lambda i, k: (0, 0)), row]
    args = [a, b, x, g, dres]
    if dep is not None:
        in_specs.append(ANY)
        args.append(dep)
    return pl.pallas_call(
        body, name=name,
        out_shape=(jax.ShapeDtypeStruct((M, D), F32), jax.ShapeDtypeStruct((M, D), BF16),
                   jax.ShapeDtypeStruct((8, D), F32)),
        grid=(M // tm, nk),
        in_specs=in_specs,
        out_specs=(row, row, pl.BlockSpec((8, D), lambda i, k: (0, 0))),
        scratch_shapes=[pltpu.VMEM((tm, D), F32)],
        compiler_params=_cparams("arbitrary", "arbitrary"),
    )(*args)


def _loss_bwd(h, target, g, *, name):
    S, D = h.shape
    tm = _tile(S, (512, 256))

    def body(h_ref, t_ref, g_ref, dx_ref, dxb_ref, acc_ref):
        i = pl.program_id(0)
        x = h_ref[...]
        gv = g_ref[...]
        r = lax.rsqrt(jnp.mean(x * x, axis=-1, keepdims=True) + RMS_EPS)
        err = x * r * gv - t_ref[...]
        dx, dgx = _rms_bwd_rows(x, gv, err * (1.0 / D))
        dx_ref[...] = dx
        dxb_ref[...] = dx.astype(BF16)

        @pl.when(i == 0)
        def _():
            acc_ref[...] = jnp.zeros_like(acc_ref)

        acc_ref[0:1, :] += jnp.sum(dgx, axis=0, keepdims=True)
        acc_ref[1:2, :] += jnp.sum(err * err, axis=0, keepdims=True)

    row = pl.BlockSpec((tm, D), lambda i: (i, 0))
    return pl.pallas_call(
        body, name=name,
        out_shape=(jax.ShapeDtypeStruct((S, D), F32), jax.ShapeDtypeStruct((S, D), BF16),
                   jax.ShapeDtypeStruct((8, D), F32)),
        grid=(S // tm,),
        in_specs=[row, row, pl.BlockSpec((1, D), lambda i: (0, 0))],
        out_specs=(row, row, pl.BlockSpec((8, D), lambda i: (0, 0))),
        compiler_params=_cparams("arbitrary"),
    )(h, target, g)


SUBLANES = 8
CONV_ROWS = 64


def _build_shifted(ext_ref, rot_ref, ts, shifts=tuple(range(1, SUBLANES))):
    rows = ts + HALO - SUBLANES
    for j in shifts:
        rot_ref[j, 0:rows, :] = ext_ref[j:j + rows, :]


def _shifted(ext_ref, rot_ref, off, r0, nrows, cs):
    q, j = divmod(off, SUBLANES)
    start = SUBLANES * q + r0
    if j == 0:
        return ext_ref[start:start + nrows, cs]
    return rot_ref[j, start:start + nrows, cs]


def _conv_fwd(uc, conv_w, conv_b, ln_g, ln_b, *, name):
    S, C2 = uc.shape
    C = C2 // 2
    ts = _tile(S, (512, 256))
    per = ts // HALO

    def body(cur_ref, halo_ref, w_ref, b_ref, g_ref, beta_ref, ya_ref, h_ref, ct_ref, ext_ref, rot_ref):
        i = pl.program_id(0)
        hh = halo_ref[:, 0:C] * jax.nn.sigmoid(halo_ref[:, C:C2])
        ext_ref[0:HALO, :] = jnp.where(i == 0, 0.0, hh)
        hc = cur_ref[:, 0:C] * jax.nn.sigmoid(cur_ref[:, C:C2])
        ext_ref[HALO:HALO + ts, :] = hc
        h_ref[...] = hc
        _build_shifted(ext_ref, rot_ref, ts)
        for c0 in range(0, C, LANES):
            cs = slice(c0, c0 + LANES)
            for r0 in range(0, ts, CONV_ROWS):
                acc = jnp.zeros((CONV_ROWS, LANES), F32)
                for k in range(CONV_A_WIDTH):
                    acc = acc + w_ref[k:k + 1, cs] * _shifted(ext_ref, rot_ref, k + 2, r0, CONV_ROWS, cs)
                ct_ref[r0:r0 + CONV_ROWS, cs] = acc + b_ref[:, cs]
        ct = ct_ref[...]
        mu = jnp.mean(ct, axis=-1, keepdims=True)
        xc = ct - mu
        var = jnp.mean(xc * xc, axis=-1, keepdims=True)
        l = xc * lax.rsqrt(var + LN_EPS) * g_ref[...] + beta_ref[...]
        ya_ref[...] = (l * jax.nn.sigmoid(l)).astype(ya_ref.dtype)

    vec = pl.BlockSpec((1, C), lambda i: (0, 0))
    row = pl.BlockSpec((ts, C), lambda i: (i, 0))
    return pl.pallas_call(
        body, name=name,
        out_shape=(jax.ShapeDtypeStruct((S, C), BF16), jax.ShapeDtypeStruct((S, C), F32),
                   jax.ShapeDtypeStruct((S, C), F32)),
        grid=(S // ts,),
        in_specs=[pl.BlockSpec((ts, C2), lambda i: (i, 0)),
                  pl.BlockSpec((HALO, C2), lambda i: (jnp.maximum(i * per - 1, 0), 0)),
                  pl.BlockSpec((CONV_A_WIDTH, C), lambda i: (0, 0)), vec, vec, vec],
        out_specs=(row, row, row),
        scratch_shapes=[pltpu.VMEM((HALO + ts, C), F32), pltpu.VMEM((8, HALO + ts, C), F32)],
        compiler_params=_cparams("parallel"),
    )(uc, uc, conv_w, conv_b, ln_g, ln_b)


CONV_ACC_ROWS = 40


def _conv_bwd_ln(ct, dcat, hglu, ln_g, ln_b, *, name):
    S, C = ct.shape
    CW = dcat.shape[1]
    ts = _tile(S, (512, 256))
    per = ts // HALO

    def body(ct_ref, dcat_ref, hc_ref, hh_ref, g_ref, beta_ref, dc_ref, acc_ref, ext_ref, rot_ref):
        i = pl.program_id(0)
        ct = ct_ref[...]
        gv = g_ref[...]
        mu = jnp.mean(ct, axis=-1, keepdims=True)
        xc = ct - mu
        rstd = lax.rsqrt(jnp.mean(xc * xc, axis=-1, keepdims=True) + LN_EPS)
        xh = xc * rstd
        l = xh * gv + beta_ref[...]
        sg = jax.nn.sigmoid(l)
        dl = dcat_ref[:, 0:C] * (sg * (1.0 + l * (1.0 - sg)))
        dxh = dl * gv
        dc = rstd * (dxh - jnp.mean(dxh, axis=-1, keepdims=True)
                     - xh * jnp.mean(dxh * xh, axis=-1, keepdims=True))
        dc_ref[...] = dc

        @pl.when(i == 0)
        def _():
            acc_ref[...] = jnp.zeros_like(acc_ref)

        acc_ref[32:33, :] += jnp.sum(dc, axis=0, keepdims=True)
        acc_ref[33:34, :] += jnp.sum(dl * xh, axis=0, keepdims=True)
        acc_ref[34:35, :] += jnp.sum(dl, axis=0, keepdims=True)
        ext_ref[0:HALO, :] = jnp.where(i == 0, 0.0, hh_ref[...])
        ext_ref[HALO:HALO + ts, :] = hc_ref[...]
        _build_shifted(ext_ref, rot_ref, ts)
        for c0 in range(0, C, LANES):
            cs = slice(c0, c0 + LANES)
            dcc = dc_ref[:, cs]
            for k in range(CONV_A_WIDTH):
                acc_ref[k:k + 1, cs] += jnp.sum(dcc * _shifted(ext_ref, rot_ref, k + 2, 0, ts, cs),
                                                axis=0, keepdims=True)

    vec = pl.BlockSpec((1, C), lambda i: (0, 0))
    row = pl.BlockSpec((ts, C), lambda i: (i, 0))
    return pl.pallas_call(
        body, name=name,
        out_shape=(jax.ShapeDtypeStruct((S, C), F32), jax.ShapeDtypeStruct((CONV_ACC_ROWS, C), F32)),
        grid=(S // ts,),
        in_specs=[row, pl.BlockSpec((ts, CW), lambda i: (i, 0)), row,
                  pl.BlockSpec((HALO, C), lambda i: (jnp.maximum(i * per - 1, 0), 0)), vec, vec],
        out_specs=(row, pl.BlockSpec((CONV_ACC_ROWS, C), lambda i: (0, 0))),
        scratch_shapes=[pltpu.VMEM((HALO + ts, C), F32), pltpu.VMEM((8, HALO + ts, C), F32)],
        compiler_params=_cparams("arbitrary"),
    )(ct, dcat, hglu, hglu, ln_g, ln_b)


def _conv_bwd_in(dc, uc, conv_w, *, name):
    S, C = dc.shape
    C2 = 2 * C
    ts = _tile(S, (512, 256))
    per = ts // HALO
    nt = S // ts

    def body(dc_ref, dn_ref, uc_ref, w_ref, du_ref, ext_ref, rot_ref):
        i = pl.program_id(0)
        ext_ref[0:ts, :] = dc_ref[...]
        ext_ref[ts:ts + HALO, :] = jnp.where(i == nt - 1, 0.0, dn_ref[...])
        _build_shifted(ext_ref, rot_ref, ts)
        for c0 in range(0, C, LANES):
            cs = slice(c0, c0 + LANES)
            gs = slice(C + c0, C + c0 + LANES)
            for r0 in range(0, ts, CONV_ROWS):
                rs = slice(r0, r0 + CONV_ROWS)
                acc = jnp.zeros((CONV_ROWS, LANES), F32)
                for k in range(CONV_A_WIDTH):
                    acc = acc + w_ref[k:k + 1, cs] * _shifted(ext_ref, rot_ref, 30 - k, r0, CONV_ROWS, cs)
                sg = jax.nn.sigmoid(uc_ref[rs, gs])
                du_ref[rs, cs] = (acc * sg).astype(du_ref.dtype)
                du_ref[rs, gs] = (acc * uc_ref[rs, cs] * sg * (1.0 - sg)).astype(du_ref.dtype)

    return pl.pallas_call(
        body, name=name,
        out_shape=jax.ShapeDtypeStruct((S, C2), BF16),
        grid=(nt,),
        in_specs=[pl.BlockSpec((ts, C), lambda i: (i, 0)),
                  pl.BlockSpec((HALO, C), lambda i: (jnp.minimum((i + 1) * per, S // HALO - 1), 0)),
                  pl.BlockSpec((ts, C2), lambda i: (i, 0)),
                  pl.BlockSpec((CONV_A_WIDTH, C), lambda i: (0, 0))],
        out_specs=pl.BlockSpec((ts, C2), lambda i: (i, 0)),
        scratch_shapes=[pltpu.VMEM((ts + HALO, C), F32), pltpu.VMEM((8, ts + HALO, C), F32)],
        compiler_params=_cparams("parallel"),
    )(dc, dc, uc, conv_w)


SC_SHIFTS_BACK = ((HALO - 2) % SUBLANES, (HALO - 1) % SUBLANES)
SC_SHIFTS_AHEAD = (1, 2)


def _sc_fwd(u3, conv_w, *, name):
    S, W3 = u3.shape
    W = W3 // 3
    ts = _tile(S, (256,))
    per = ts // HALO

    def body(cur_ref, halo_ref, w_ref, y_ref, ext_ref, rot_ref):
        i = pl.program_id(0)
        cvh = halo_ref[:, W:2 * W].astype(F32) * halo_ref[:, 2 * W:W3].astype(F32)
        ext_ref[0:HALO, :] = jnp.where(i == 0, 0.0, cvh)
        ext_ref[HALO:HALO + ts, :] = cur_ref[:, W:2 * W].astype(F32) * cur_ref[:, 2 * W:W3].astype(F32)
        _build_shifted(ext_ref, rot_ref, ts, SC_SHIFTS_BACK)
        for c0 in range(0, W, LANES):
            cs = slice(c0, c0 + LANES)
            for r0 in range(0, ts, CONV_ROWS):
                rs = slice(r0, r0 + CONV_ROWS)
                k = (w_ref[0:1, cs] * _shifted(ext_ref, rot_ref, HALO - 2, r0, CONV_ROWS, cs)
                     + w_ref[1:2, cs] * _shifted(ext_ref, rot_ref, HALO - 1, r0, CONV_ROWS, cs)
                     + w_ref[2:3, cs] * _shifted(ext_ref, rot_ref, HALO, r0, CONV_ROWS, cs))
                y_ref[rs, cs] = (cur_ref[rs, cs].astype(F32) * k).astype(y_ref.dtype)

    return pl.pallas_call(
        body, name=name,
        out_shape=jax.ShapeDtypeStruct((S, W), BF16),
        grid=(S // ts,),
        in_specs=[pl.BlockSpec((ts, W3), lambda i: (i, 0)),
                  pl.BlockSpec((HALO, W3), lambda i: (jnp.maximum(i * per - 1, 0), 0)),
                  pl.BlockSpec((SC_CONV_WIDTH, W), lambda i: (0, 0))],
        out_specs=pl.BlockSpec((ts, W), lambda i: (i, 0)),
        scratch_shapes=[pltpu.VMEM((HALO + ts, W), F32), pltpu.VMEM((8, HALO + ts, W), F32)],
        compiler_params=_cparams("parallel"),
    )(u3, u3, conv_w)


def _sc_bwd(u3, dy, conv_w, *, name):
    S, W3 = u3.shape
    W = W3 // 3
    ts = _tile(S, (256,))
    per = ts // HALO
    nt = S // ts

    def body(cur_ref, prev_ref, next_ref, dy_ref, dyn_ref, w_ref, du_ref, dw_ref, cv_ext, dk_ext, cv_rot, dk_rot):
        i = pl.program_id(0)
        cvh = prev_ref[:, W:2 * W].astype(F32) * prev_ref[:, 2 * W:W3].astype(F32)
        cv_ext[0:HALO, :] = jnp.where(i == 0, 0.0, cvh)
        cv_ext[HALO:HALO + ts, :] = cur_ref[:, W:2 * W].astype(F32) * cur_ref[:, 2 * W:W3].astype(F32)
        dk_ext[0:ts, :] = dy_ref[...] * cur_ref[:, 0:W].astype(F32)
        dk_ext[ts:ts + HALO, :] = jnp.where(i == nt - 1, 0.0, dyn_ref[...] * next_ref[:, 0:W].astype(F32))
        _build_shifted(cv_ext, cv_rot, ts, SC_SHIFTS_BACK)
        _build_shifted(dk_ext, dk_rot, ts, SC_SHIFTS_AHEAD)

        @pl.when(i == 0)
        def _():
            dw_ref[...] = jnp.zeros_like(dw_ref)

        for c0 in range(0, W, LANES):
            cs = slice(c0, c0 + LANES)
            w0, w1, w2 = w_ref[0:1, cs], w_ref[1:2, cs], w_ref[2:3, cs]
            sums = [jnp.zeros((1, LANES), F32)] * SC_CONV_WIDTH
            for r0 in range(0, ts, CONV_ROWS):
                rs = slice(r0, r0 + CONV_ROWS)
                cv2 = _shifted(cv_ext, cv_rot, HALO - 2, r0, CONV_ROWS, cs)
                cv1 = _shifted(cv_ext, cv_rot, HALO - 1, r0, CONV_ROWS, cs)
                cv0 = _shifted(cv_ext, cv_rot, HALO, r0, CONV_ROWS, cs)
                dk = dk_ext[rs, cs]
                dcv = (w2 * dk + w1 * _shifted(dk_ext, dk_rot, 1, r0, CONV_ROWS, cs)
                       + w0 * _shifted(dk_ext, dk_rot, 2, r0, CONV_ROWS, cs))
                du_ref[rs, cs] = (dy_ref[rs, cs] * (w0 * cv2 + w1 * cv1 + w2 * cv0)).astype(du_ref.dtype)
                du_ref[rs, W + c0:W + c0 + LANES] = (
                    dcv * cur_ref[rs, 2 * W + c0:2 * W + c0 + LANES].astype(F32)).astype(du_ref.dtype)
                du_ref[rs, 2 * W + c0:2 * W + c0 + LANES] = (
                    dcv * cur_ref[rs, W + c0:W + c0 + LANES].astype(F32)).astype(du_ref.dtype)
                for t, cvt in enumerate((cv2, cv1, cv0)):
                    sums[t] = sums[t] + jnp.sum(dk * cvt, axis=0, keepdims=True)
            for t in range(SC_CONV_WIDTH):
                dw_ref[t:t + 1, cs] += sums[t]

    nxt = lambda i: (jnp.minimum((i + 1) * per, S // HALO - 1), 0)
    return pl.pallas_call(
        body, name=name,
        out_shape=(jax.ShapeDtypeStruct((S, W3), BF16), jax.ShapeDtypeStruct((8, W), F32)),
        grid=(nt,),
        in_specs=[pl.BlockSpec((ts, W3), lambda i: (i, 0)),
                  pl.BlockSpec((HALO, W3), lambda i: (jnp.maximum(i * per - 1, 0), 0)),
                  pl.BlockSpec((HALO, W3), nxt),
                  pl.BlockSpec((ts, W), lambda i: (i, 0)),
                  pl.BlockSpec((HALO, W), nxt),
                  pl.BlockSpec((SC_CONV_WIDTH, W), lambda i: (0, 0))],
        out_specs=(pl.BlockSpec((ts, W3), lambda i: (i, 0)), pl.BlockSpec((8, W), lambda i: (0, 0))),
        scratch_shapes=[pltpu.VMEM((HALO + ts, W), F32), pltpu.VMEM((ts + HALO, W), F32),
                        pltpu.VMEM((8, HALO + ts, W), F32), pltpu.VMEM((8, ts + HALO, W), F32)],
        compiler_params=_cparams("arbitrary"),
    )(u3, u3, u3, dy, dy, conv_w)


def _t5_causal_bucket(n):
    max_exact = NUM_BUCKETS // 2
    nf = jnp.maximum(n, 1).astype(F32)
    large = max_exact + (jnp.log(nf / max_exact) / math.log(REL_MAX_DISTANCE / max_exact)
                         * (NUM_BUCKETS - max_exact)).astype(jnp.int32)
    return jnp.where(n < max_exact, n, jnp.minimum(large, NUM_BUCKETS - 1))


def _bucket_tables():
    steps = ATTN_BLK
    m = jnp.arange(steps)[:, None] + steps - jnp.arange(2 * steps)[None, :]
    return jnp.stack([_t5_causal_bucket(jnp.clip(m, 0, steps) * dil).astype(F32) for _, dil in DILATED_GROUPS])


def _bias_tables(rel_bias, buckets, *, name):
    steps = ATTN_BLK

    def body(tab_ref, bk_ref, o_ref):
        g = pl.program_id(0)
        bk = bk_ref[0]
        a_idx = lax.broadcasted_iota(jnp.int32, (steps, 2 * steps), 0)
        c_idx = lax.broadcasted_iota(jnp.int32, (steps, 2 * steps), 1)
        m = a_idx + steps - c_idx
        band = (m >= 0) & (m <= steps)
        band_first = band & (c_idx >= steps)
        for h in range(HEADS_PER_GROUP):
            bias = jnp.zeros((steps, 2 * steps), F32)
            for b in range(NUM_BUCKETS):
                bias = jnp.where(bk == float(b), tab_ref[b, g * HEADS_PER_GROUP + h], bias)
            o_ref[0, 0, h] = jnp.where(band_first, bias, NEG_INF)
            o_ref[0, 1, h] = jnp.where(band, bias, NEG_INF)

    return pl.pallas_call(
        body, name=name,
        out_shape=jax.ShapeDtypeStruct((N_GROUPS, 2, HEADS_PER_GROUP, steps, 2 * steps), F32),
        grid=(N_GROUPS,),
        in_specs=[pl.BlockSpec(memory_space=pltpu.SMEM),
                  pl.BlockSpec((1, steps, 2 * steps), lambda g: (g, 0, 0))],
        out_specs=pl.BlockSpec((1, 2, HEADS_PER_GROUP, steps, 2 * steps), lambda g: (g, 0, 0, 0, 0)),
        compiler_params=_cparams("parallel"),
    )(rel_bias, buckets)


def _lane_is_low():
    return lax.broadcasted_iota(jnp.int32, (1, LANES), 1) < HEAD_DIM


def _stack_heads(x2, low):
    zero = jnp.zeros_like(x2)
    return jnp.concatenate([jnp.where(low, x2, zero), jnp.where(low, zero, x2)], axis=0)


ATTN_FWD_BLOCKS = 4


def _attn_fwd(uv, bias, g, d, *, name):
    rows = uv.shape[0]
    nsub = ATTN_FWD_BLOCKS
    step_rows = nsub * ATTN_BLK
    nqb = GROUP_QKV // ATTN_OUT

    def body(q_ref, kc_ref, kp_ref, vc_ref, vp_ref, bias_ref, o_ref, l_ref):
        n = pl.program_id(1)
        low = _lane_is_low()
        slabs = [slice(hp * LANES, (hp + 1) * LANES) for hp in range(HEADS_PER_GROUP // 2)]
        for sub in range(nsub):
            qr = slice(sub * ATTN_BLK, (sub + 1) * ATTN_BLK)
            sel = jnp.minimum(n, 1) if sub == 0 else 1

            def with_prev(cur_ref, prev_ref, sl, sub=sub, qr=qr):
                prev = prev_ref[:, sl] if sub == 0 else cur_ref[(sub - 1) * ATTN_BLK:sub * ATTN_BLK, sl]
                return jnp.concatenate([prev, cur_ref[qr, sl]], axis=0)

            scores = [_dot_nt(_stack_heads(q_ref[qr, sl] * (HEAD_DIM ** -0.5), low), with_prev(kc_ref, kp_ref, sl))
                      for sl in slabs]
            probs, dens_all, lses_all = [], [], []
            for hp, s in enumerate(scores):
                ps, dens, lses = [], [], []
                for hh in range(2):
                    logits = s[hh * ATTN_BLK:(hh + 1) * ATTN_BLK] + bias_ref[sel, 2 * hp + hh]
                    mx = jnp.max(logits, axis=-1, keepdims=True)
                    p = jnp.exp(logits - mx)
                    den = jnp.sum(p, axis=-1, keepdims=True)
                    ps.append(p.astype(BF16))
                    dens.append(den)
                    lses.append(jnp.broadcast_to(mx + jnp.log(den), (ATTN_BLK, LANES)))
                probs.append(jnp.concatenate(ps, axis=0))
                dens_all.append(dens)
                lses_all.append(lses)
            for hp, sl in enumerate(slabs):
                pv = jnp.dot(probs[hp], with_prev(vc_ref, vp_ref, sl), preferred_element_type=F32)
                dens, lses = dens_all[hp], lses_all[hp]
                o_ref[qr, sl] = jnp.where(low, pv[0:ATTN_BLK] / dens[0], pv[ATTN_BLK:2 * ATTN_BLK] / dens[1])
                l_ref[qr, sl] = jnp.where(low, lses[0], lses[1])

    def cur(t):
        return pl.BlockSpec((step_rows, ATTN_OUT), lambda r, n: (n, r * nqb + t))

    def prev(t):
        return pl.BlockSpec((ATTN_BLK, ATTN_OUT), lambda r, n: (jnp.maximum(n * nsub - 1, 0), r * nqb + t))

    out_spec = pl.BlockSpec((step_rows, ATTN_OUT), lambda r, n: (n, r))
    return pl.pallas_call(
        body, name=name,
        out_shape=(jax.ShapeDtypeStruct((rows, d * ATTN_OUT), F32),) * 2,
        grid=(d, rows // step_rows),
        in_specs=[cur(0), cur(1), prev(1), cur(2), prev(2),
                  pl.BlockSpec((None, 2, HEADS_PER_GROUP, ATTN_BLK, 2 * ATTN_BLK), lambda r, n: (g, 0, 0, 0, 0))],
        out_specs=(out_spec, out_spec),
        compiler_params=_cparams("parallel", "parallel"),
    )(uv, uv, uv, uv, uv, bias)


def _chunk_scratch(n, width):
    return pltpu.VMEM((width // LANES, n, LANES), F32)


def _to_chunks(scr, val):
    for c in range(scr.shape[0]):
        scr[c] = val[:, c * LANES:(c + 1) * LANES]


def _from_chunks(scr):
    return jnp.concatenate([scr[c] for c in range(scr.shape[0])], axis=1)


def _slabs_from_chunks(dst_ref, scr, d, dtype):
    nc, n, _ = scr.shape
    for r in range(d):
        for c in range(nc):
            col = r * nc * LANES + c * LANES
            dst_ref[:, col:col + LANES] = scr[c, pl.ds(r, n // d, stride=d), :].astype(dtype)


def _chunks_from_slabs(scr, src_ref, d):
    nc, n, _ = scr.shape
    for r in range(d):
        for c in range(nc):
            col = r * nc * LANES + c * LANES
            scr[c, pl.ds(r, n // d, stride=d), :] = src_ref[:, col:col + LANES]


def _attn_merge(outs, lses, ya, *, name):
    S, C = ya.shape
    tm = _tile(S, (512, 256))
    dils = [dil for _, dil in DILATED_GROUPS]

    def body(o0, o1, o2, l0, l1, l2, ya_ref, cat_ref, out_ref, lse_ref, so1, so2, sl1, sl2):
        _chunks_from_slabs(so1, o1, dils[1])
        _chunks_from_slabs(so2, o2, dils[2])
        _chunks_from_slabs(sl1, l1, dils[1])
        _chunks_from_slabs(sl2, l2, dils[2])
        a0, a1, a2 = l0[...], _from_chunks(sl1), _from_chunks(sl2)
        m = jnp.maximum(jnp.maximum(a0, a1), a2)
        e0, e1, e2 = jnp.exp(a0 - m), jnp.exp(a1 - m), jnp.exp(a2 - m)
        den = e0 + e1 + e2
        out = (e0 * o0[...] + e1 * _from_chunks(so1) + e2 * _from_chunks(so2)) / den
        out_ref[...] = out
        lse_ref[...] = m + jnp.log(den)
        cat_ref[:, 0:C] = ya_ref[...]
        cat_ref[:, C:C + ATTN_OUT] = out.astype(cat_ref.dtype)

    blk = pl.BlockSpec((tm, ATTN_OUT), lambda i: (i, 0))
    vblk = [pl.BlockSpec((tm // d, d * ATTN_OUT), lambda i: (i, 0)) for d in dils]
    assert dils[0] == 1
    return pl.pallas_call(
        body, name=name,
        out_shape=(jax.ShapeDtypeStruct((S, C + ATTN_OUT), BF16), jax.ShapeDtypeStruct((S, ATTN_OUT), F32),
                   jax.ShapeDtypeStruct((S, ATTN_OUT), F32)),
        grid=(S // tm,),
        in_specs=vblk + vblk + [pl.BlockSpec((tm, C), lambda i: (i, 0))],
        out_specs=(pl.BlockSpec((tm, C + ATTN_OUT), lambda i: (i, 0)), blk, blk),
        scratch_shapes=[_chunk_scratch(tm, ATTN_OUT)] * 4,
        compiler_params=_cparams("parallel"),
    )(*outs, *lses, ya)


def _attn_prep(dcat, outf, lse, *, name):
    S, CW = dcat.shape
    C = CW - ATTN_OUT
    tm = _tile(S, (512, 256))
    dils = [dil for _, dil in DILATED_GROUPS]
    assert dils[0] == 1
    ones = np.kron(np.eye(HEADS_PER_GROUP, dtype=np.float32), np.ones((HEAD_DIM, HEAD_DIM), np.float32))

    nviews = 3 * (len(dils) - 1)

    def body(dcat_ref, out_ref, l_ref, ones_ref, dyb_ref, dl_ref, *rest):
        views = rest[:nviews]
        s_dyb, s_dl, s_l = rest[nviews:]
        dyb = dcat_ref[:, C:CW]
        dyb_ref[...] = dyb.astype(BF16)
        prod = dyb * out_ref[...]
        ov = ones_ref[...]
        hi, mid, lo = _split_bf16(prod)
        delta = (jnp.dot(hi, ov, preferred_element_type=F32)
                 + jnp.dot(mid, ov, preferred_element_type=F32)
                 + jnp.dot(lo, ov, preferred_element_type=F32))
        dl_ref[...] = delta
        _to_chunks(s_dyb, dyb)
        _to_chunks(s_dl, delta)
        _to_chunks(s_l, l_ref[...])
        for gi, d in enumerate(dils[1:]):
            dyb_v, dl_v, l_v = views[3 * gi:3 * gi + 3]
            _slabs_from_chunks(dyb_v, s_dyb, d, BF16)
            _slabs_from_chunks(dl_v, s_dl, d, F32)
            _slabs_from_chunks(l_v, s_l, d, F32)

    blk = pl.BlockSpec((tm, ATTN_OUT), lambda i: (i, 0))
    view_shapes, view_specs = [], []
    for d in dils[1:]:
        for dt in (BF16, F32, F32):
            view_shapes.append(jax.ShapeDtypeStruct((S // d, d * ATTN_OUT), dt))
            view_specs.append(pl.BlockSpec((tm // d, d * ATTN_OUT), lambda i: (i, 0)))
    res = pl.pallas_call(
        body, name=name,
        out_shape=(jax.ShapeDtypeStruct((S, ATTN_OUT), BF16), jax.ShapeDtypeStruct((S, ATTN_OUT), F32),
                   *view_shapes),
        grid=(S // tm,),
        in_specs=[pl.BlockSpec((tm, CW), lambda i: (i, 0)), blk, blk,
                  pl.BlockSpec((ATTN_OUT, ATTN_OUT), lambda i: (0, 0))],
        out_specs=(blk, blk, *view_specs),
        scratch_shapes=[_chunk_scratch(tm, ATTN_OUT)] * 3,
        compiler_params=_cparams("parallel"),
    )(dcat, outf, lse, jnp.asarray(ones, BF16))
    return [(res[0], res[1], lse)] + [tuple(res[2 + 3 * gi:5 + 3 * gi]) for gi in range(len(dils) - 1)]


ATTN_BWD_RESIDUES = 4


def _attn_bwd(uv, dov, lv, dv_, bias, g, d, *, name):
    rows = uv.shape[0]
    nb = rows // ATTN_BLK
    nres = min(d, ATTN_BWD_RESIDUES)
    steps = (d // nres) * nb
    scale = HEAD_DIM ** -0.5
    Q0, K0, V0 = 0, ATTN_OUT, 2 * ATTN_OUT

    def body(cur_ref, prev_ref, do_ref, l_ref, dl_ref, bias_ref, out_ref, db_ref, dq_s, dk_s, dv_s):
        t = pl.program_id(0)
        n = t % nb
        low = _lane_is_low()

        @pl.when(t == 0)
        def _():
            db_ref[...] = jnp.zeros_like(db_ref)
            dq_s[...] = jnp.zeros_like(dq_s)
            dk_s[...] = jnp.zeros_like(dk_s)
            dv_s[...] = jnp.zeros_like(dv_s)

        @pl.when(t < steps)
        def _():
            sel = jnp.minimum(n, 1)
            lane = lax.broadcasted_iota(jnp.int32, (1, LANES), 1)
            nk2 = 2 * ATTN_BLK
            for rr in range(nres):
                u0 = rr * GROUP_QKV
                o0 = rr * ATTN_OUT
                slabs = [hp * LANES for hp in range(HEADS_PER_GROUP // 2)]

                def cols(base, c0):
                    return slice(base + c0, base + c0 + LANES)

                keys = [jnp.concatenate([prev_ref[:, cols(u0 + K0, c0)], cur_ref[:, cols(u0 + K0, c0)]], axis=0)
                        for c0 in slabs]
                scores = [_dot_nt(_stack_heads(cur_ref[:, cols(u0 + Q0, c0)] * scale, low), keys[hp])
                          for hp, c0 in enumerate(slabs)]
                dps = [_dot_nt(_stack_heads(do_ref[:, cols(o0, c0)], low),
                               jnp.concatenate([prev_ref[:, cols(u0 + V0, c0)], cur_ref[:, cols(u0 + V0, c0)]],
                                               axis=0)) for c0 in slabs]
                stacked = []
                for hp, c0 in enumerate(slabs):
                    lse2 = l_ref[:, cols(o0, c0)]
                    dl2 = dl_ref[:, cols(o0, c0)]
                    pbs, dsbs = [], []
                    for hh in range(2):
                        rws = slice(hh * ATTN_BLK, (hh + 1) * ATTN_BLK)
                        one = lane == hh * HEAD_DIM
                        lse_col = jnp.sum(jnp.where(one, lse2, 0.0), axis=-1, keepdims=True)
                        dl_col = jnp.sum(jnp.where(one, dl2, 0.0), axis=-1, keepdims=True)
                        p = jnp.exp(scores[hp][rws] + bias_ref[sel, 2 * hp + hh] - lse_col)
                        ds = p * (dps[hp][rws] - dl_col)
                        db_ref[2 * hp + hh] += ds
                        pbs.append(p.astype(BF16))
                        dsbs.append((ds * scale).astype(BF16))
                    stacked.append((jnp.concatenate(dsbs, axis=0), jnp.concatenate(dsbs, axis=1),
                                    jnp.concatenate(pbs, axis=1)))
                for hp, c0 in enumerate(slabs):
                    ds_rows, ds_cols, p_cols = stacked[hp]
                    dq = jnp.dot(ds_rows, keys[hp], preferred_element_type=F32)
                    dk = _dot_tn(ds_cols, cur_ref[:, cols(u0 + Q0, c0)])
                    dv = _dot_tn(p_cols, do_ref[:, cols(o0, c0)])
                    dq2 = jnp.where(low, dq[0:ATTN_BLK], dq[ATTN_BLK:nk2])
                    dk2 = jnp.where(low, dk[0:nk2], dk[nk2:2 * nk2])
                    dv2 = jnp.where(low, dv[0:nk2], dv[nk2:2 * nk2])
                    sl = cols(o0, c0)
                    out_ref[:, cols(u0 + Q0, c0)] = dq_s[:, sl].astype(out_ref.dtype)
                    dq_s[:, sl] = dq2
                    out_ref[:, cols(u0 + K0, c0)] = (dk_s[:, sl] + dk2[0:ATTN_BLK]).astype(out_ref.dtype)
                    dk_s[:, sl] = dk2[ATTN_BLK:2 * ATTN_BLK]
                    out_ref[:, cols(u0 + V0, c0)] = (dv_s[:, sl] + dv2[0:ATTN_BLK]).astype(out_ref.dtype)
                    dv_s[:, sl] = dv2[ATTN_BLK:2 * ATTN_BLK]

        @pl.when(t == steps)
        def _():
            for rr in range(nres):
                u0, o0 = rr * GROUP_QKV, rr * ATTN_OUT
                out_ref[:, u0 + Q0:u0 + Q0 + ATTN_OUT] = dq_s[:, o0:o0 + ATTN_OUT].astype(out_ref.dtype)
                out_ref[:, u0 + K0:u0 + K0 + ATTN_OUT] = dk_s[:, o0:o0 + ATTN_OUT].astype(out_ref.dtype)
                out_ref[:, u0 + V0:u0 + V0 + ATTN_OUT] = dv_s[:, o0:o0 + ATTN_OUT].astype(out_ref.dtype)

    def blocks(width, lag=0, back=0):
        def index_map(t):
            tt = jnp.maximum(jnp.minimum(t, steps - 1) - lag, 0)
            return (jnp.maximum(tt % nb - back, 0), tt // nb)
        return pl.BlockSpec((ATTN_BLK, nres * width), index_map)

    return pl.pallas_call(
        body, name=name,
        out_shape=(jax.ShapeDtypeStruct((rows, d * GROUP_QKV), BF16),
                   jax.ShapeDtypeStruct((HEADS_PER_GROUP, ATTN_BLK, 2 * ATTN_BLK), F32)),
        grid=(steps + 1,),
        in_specs=[blocks(GROUP_QKV), blocks(GROUP_QKV, back=1), blocks(ATTN_OUT), blocks(ATTN_OUT), blocks(ATTN_OUT),
                  pl.BlockSpec((None, 2, HEADS_PER_GROUP, ATTN_BLK, 2 * ATTN_BLK), lambda t: (g, 0, 0, 0, 0))],
        out_specs=(pl.BlockSpec((ATTN_BLK, nres * GROUP_QKV),
                                lambda t: (jnp.maximum(t - 1, 0) % nb, jnp.maximum(t - 1, 0) // nb)),
                   pl.BlockSpec((HEADS_PER_GROUP, ATTN_BLK, 2 * ATTN_BLK), lambda t: (0, 0, 0))),
        scratch_shapes=[pltpu.VMEM((ATTN_BLK, nres * ATTN_OUT), F32)] * 3,
        compiler_params=_cparams("arbitrary"),
    )(uv, uv, dov, lv, dv_, bias)


def _split_bf16(x):
    hi = x.astype(BF16)
    r1 = x - hi.astype(F32)
    mid = r1.astype(BF16)
    lo = (r1 - mid.astype(F32)).astype(BF16)
    return hi, mid, lo


RELBIAS_CHUNK = 4096


def _relbias_reduce(dbs, buckets, *, name):
    flat = ATTN_BLK * 2 * ATTN_BLK
    dbf = jnp.stack([db.reshape(HEADS_PER_GROUP, flat) for db in dbs])
    bkf = buckets.reshape(N_GROUPS, 1, flat)

    def body(db_ref, bk_ref, o_ref):
        c = pl.program_id(1)
        rows = lax.broadcasted_iota(jnp.int32, (LANES, RELBIAS_CHUNK), 0).astype(F32)
        onehot = jnp.where(rows == bk_ref[0], 1.0, 0.0).astype(BF16)
        hi, mid, lo = _split_bf16(db_ref[0])
        part = _dot_nt(hi, onehot) + _dot_nt(mid, onehot) + _dot_nt(lo, onehot)

        @pl.when(c == 0)
        def _():
            o_ref[0] = part

        @pl.when(c > 0)
        def _():
            o_ref[0] += part

    return pl.pallas_call(
        body, name=name,
        out_shape=jax.ShapeDtypeStruct((N_GROUPS, HEADS_PER_GROUP, LANES), F32),
        grid=(N_GROUPS, flat // RELBIAS_CHUNK),
        in_specs=[pl.BlockSpec((1, HEADS_PER_GROUP, RELBIAS_CHUNK), lambda g, c: (g, 0, c)),
                  pl.BlockSpec((1, 1, RELBIAS_CHUNK), lambda g, c: (g, 0, c))],
        out_specs=pl.BlockSpec((1, HEADS_PER_GROUP, LANES), lambda g, c: (g, 0, 0)),
        compiler_params=_cparams("parallel", "arbitrary"),
    )(dbf, bkf)


def _my_position():
    x, y, c = lax.axis_index("x"), lax.axis_index("y"), lax.axis_index("c")
    return x, y, c


def _linear(pos):
    return 4 * pos[0] + 2 * pos[1] + pos[2]


def _peer(pos, k):
    x, y, c = pos
    return ((1 - x) if k & 4 else x, (1 - y) if k & 2 else y, (1 - c) if k & 1 else c)


HBM_SPEC = pl.BlockSpec(memory_space=pltpu.HBM)
SEM_SPEC = pl.BlockSpec(memory_space=pltpu.SEMAPHORE)
DATAFLOW = pltpu.SideEffectType.DATAFLOW_SIDE_EFFECTING


def _exchange_copies(src, land, sems, send_window, recv_window, with_arrivals):
    send_sems, recv_sems, local_sems = sems
    T = len(src)
    me = _my_position()
    me_lin = _linear(me)
    local = [pltpu.make_async_copy(send_window(t, src[t], me_lin), recv_window(t, land[t], me_lin),
                                   local_sems.at[t]) for t in range(T)]
    sends, arrivals = [], []
    for t in range(T):
        for k in range(1, N_DEV):
            peer = _peer(me, k)
            peer_lin = _linear(peer)
            sem = t * (N_DEV - 1) + k - 1
            sends.append(pltpu.make_async_remote_copy(
                src_ref=send_window(t, src[t], peer_lin), dst_ref=recv_window(t, land[t], me_lin),
                send_sem=send_sems.at[sem], recv_sem=recv_sems.at[sem],
                device_id=peer, device_id_type=MESH))
            if with_arrivals:
                arrivals.append(pltpu.make_async_remote_copy(
                    src_ref=send_window(t, src[t], me_lin), dst_ref=recv_window(t, land[t], peer_lin),
                    send_sem=send_sems.at[sem], recv_sem=recv_sems.at[sem],
                    device_id=peer, device_id_type=MESH))
    return local, sends, arrivals


def _exchange_start(srcs, land_shapes, send_window, recv_window, *, name, dep=None):
    T = len(srcs)
    n_in = 2 * T + (1 if dep is not None else 0)

    def body(*refs):
        src = refs[:T]
        land = refs[T:2 * T]
        sems = refs[n_in:n_in + 3]
        token = refs[-1]
        local, sends, _ = _exchange_copies(src, land, sems, send_window, recv_window, False)
        for cp in local + sends:
            cp.start()
        token[...] = jnp.zeros_like(token)

    lands = [lax.empty(ls.shape, ls.dtype) for ls in land_shapes]
    operands = [pltpu.with_memory_space_constraint(a, pltpu.HBM) for a in list(srcs) + lands]
    outs = pl.pallas_call(
        body, name=name,
        out_shape=(pltpu.SemaphoreType.DMA((T * (N_DEV - 1),)), pltpu.SemaphoreType.DMA((T * (N_DEV - 1),)),
                   pltpu.SemaphoreType.DMA((T,)),
                   *[pltpu.HBM(a.shape, a.dtype) for a in operands],
                   jax.ShapeDtypeStruct((8, LANES), F32)),
        in_specs=[HBM_SPEC] * (2 * T) + ([ANY] if dep is not None else []),
        out_specs=(SEM_SPEC,) * 3 + (HBM_SPEC,) * (2 * T) + (VMEM_SPEC,),
        input_output_aliases={i: 3 + i for i in range(2 * T)},
        compiler_params=pltpu.CompilerParams(has_side_effects=DATAFLOW),
    )(*operands, *([dep] if dep is not None else []))
    return outs[:3], outs[3:3 + T], outs[3 + T:3 + 2 * T], outs[-1]


def _exchange_wait(started, after, send_window, recv_window, *, name):
    sems, srcs, lands, _ = started
    T = len(srcs)

    def body(*refs):
        src = refs[:T]
        land = refs[T:2 * T]
        sem_refs = refs[2 * T:2 * T + 3]
        local, sends, arrivals = _exchange_copies(src, land, sem_refs, send_window, recv_window, True)
        for cp in arrivals:
            cp.wait_recv()
        for cp in sends:
            cp.wait_send()
        for cp in local:
            cp.wait()

    outs = pl.pallas_call(
        body, name=name,
        out_shape=tuple(pltpu.HBM(a.shape, a.dtype) for a in list(srcs) + list(lands)),
        in_specs=[HBM_SPEC] * (2 * T) + [SEM_SPEC] * 3 + [ANY],
        out_specs=(HBM_SPEC,) * (2 * T),
        input_output_aliases={i: i for i in range(2 * T)},
        compiler_params=pltpu.CompilerParams(has_side_effects=DATAFLOW),
    )(*srcs, *lands, *sems, after)
    return outs[T:]


def _shard_window(kind, width):
    def win(ref, lin):
        if kind == "slot":
            return ref.at[lin]
        if kind == "col":
            return ref.at[:, pl.ds(pl.multiple_of(lin * width, LANES), width)]
        if kind == "row":
            return ref.at[pl.ds(pl.multiple_of(lin * width, 8), width), :]
        if kind == "lcol":
            return ref.at[:, :, pl.ds(pl.multiple_of(lin * width, LANES), width)]
        if kind == "lrow":
            return ref.at[:, pl.ds(pl.multiple_of(lin * width, 8), width), :]
        raise ValueError(kind)
    return win


def _shard_windows(kinds, shard_shapes):
    return [_shard_window(k, (ss[-1] if k in ("col", "lcol") else ss[-2])) for k, ss in zip(kinds, shard_shapes)]


def _allgather_start(shards, kinds, full_shapes, *, name, dep=None):
    wins = _shard_windows(kinds, [s.shape for s in shards])
    send_window = lambda t, ref, lin: ref
    recv_window = lambda t, ref, lin: wins[t](ref, lin)
    started = _exchange_start(shards, [jax.ShapeDtypeStruct(fs, s.dtype) for fs, s in zip(full_shapes, shards)],
                              send_window, recv_window, name=name + "_start", dep=dep)
    return started, lambda after: _exchange_wait(started, after, send_window, recv_window, name=name + "_wait")


def _scatter_start(fulls, kinds, shard_shapes, *, name):
    wins = _shard_windows(kinds, shard_shapes)
    send_window = lambda t, ref, lin: wins[t](ref, lin)
    recv_window = lambda t, ref, lin: ref.at[lin]
    started = _exchange_start(
        fulls, [jax.ShapeDtypeStruct((N_DEV,) + tuple(ss), f.dtype) for ss, f in zip(shard_shapes, fulls)],
        send_window, recv_window, name=name + "_start")
    return started, lambda after: _exchange_wait(started, after, send_window, recv_window, name=name + "_wait")


def _small_gather(pack, *, reduce, name):
    R = pack.shape[0]

    def body(p_ref, o_ref, *rest):
        if reduce:
            buf, send_sems, recv_sems = rest
        else:
            buf = o_ref
            send_sems, recv_sems = rest
        me = _my_position()
        me_lin = _linear(me)
        buf[me_lin] = p_ref[...]
        sends = []
        for k in range(1, N_DEV):
            peer = _peer(me, k)
            cp = pltpu.make_async_remote_copy(
                src_ref=p_ref, dst_ref=buf.at[me_lin],
                send_sem=send_sems.at[k - 1], recv_sem=recv_sems.at[k - 1],
                device_id=peer, device_id_type=MESH)
            cp.start()
            sends.append(cp)
        for k in range(1, N_DEV):
            peer = _peer(me, k)
            pltpu.make_async_remote_copy(
                src_ref=p_ref, dst_ref=buf.at[_linear(peer)],
                send_sem=send_sems.at[k - 1], recv_sem=recv_sems.at[k - 1],
                device_id=peer, device_id_type=MESH).wait_recv()
        for cp in sends:
            cp.wait_send()
        if reduce:
            acc = buf[0]
            for s in range(1, N_DEV):
                acc = acc + buf[s]
            o_ref[...] = acc

    scratch = [pltpu.SemaphoreType.DMA((N_DEV - 1,)), pltpu.SemaphoreType.DMA((N_DEV - 1,))]
    if reduce:
        scratch = [pltpu.VMEM((N_DEV, R, LANES), F32)] + scratch
        out_shape = jax.ShapeDtypeStruct((R, LANES), F32)
    else:
        out_shape = jax.ShapeDtypeStruct((N_DEV, R, LANES), F32)
    return pl.pallas_call(
        body, name=name, out_shape=out_shape,
        in_specs=[VMEM_SPEC], out_specs=VMEM_SPEC, scratch_shapes=scratch,
        compiler_params=pltpu.CompilerParams(has_side_effects=True, vmem_limit_bytes=VMEM_LIMIT),
    )(pack)


def _adamw_math(w, g, m, v):
    m = ADAM_B1 * m + (1.0 - ADAM_B1) * g
    v = ADAM_B2 * v + (1.0 - ADAM_B2) * jnp.square(g)
    m_hat = m / (1.0 - ADAM_B1 ** ADAM_STEP)
    v_hat = v / (1.0 - ADAM_B2 ** ADAM_STEP)
    delta = -ADAM_LR * (m_hat / (jnp.sqrt(v_hat) + ADAM_EPS) + ADAM_WD * w)
    return delta, m, v


def _adamw_from_partials(parts, w, m, v, *, name):
    R, C = w.shape
    rl = parts[0].shape[1]
    assert all(p.shape == (N_DEV, rl, C) for p in parts) and rl * len(parts) == R
    tr = _tile(rl, (256, 128, 64, 32, 16))
    per = rl // tr
    L = len(parts)

    def body(*refs):
        p_refs = refs[:L]
        w_ref, m_ref, v_ref, g_ref, d_ref, nm_ref, nv_ref = refs[L:]
        i = pl.program_id(0)
        for l in range(L):
            @pl.when((i >= l * per) & (i < (l + 1) * per))
            def _(l=l):
                p_ref = p_refs[l]
                g = p_ref[0].astype(F32)
                for s in range(1, N_DEV):
                    g = g + p_ref[s].astype(F32)
                d, nm, nv = _adamw_math(w_ref[...], g, m_ref[...], v_ref[...])
                g_ref[...] = g
                d_ref[...] = d
                nm_ref[...] = nm
                nv_ref[...] = nv

    blk = pl.BlockSpec((tr, C), lambda i: (i, 0))
    part_specs = [pl.BlockSpec((N_DEV, tr, C), lambda i, l=l: (0, jnp.clip(i - l * per, 0, per - 1), 0))
                  for l in range(L)]
    return pl.pallas_call(
        body, name=name,
        out_shape=(jax.ShapeDtypeStruct((R, C), F32),) * 4,
        grid=(R // tr,),
        in_specs=part_specs + [blk, blk, blk],
        out_specs=(blk,) * 4,
        compiler_params=_cparams("parallel"),
    )(*parts, w, m, v)


def _adamw_small(g, w, m, v, *, name):
    def body(g_ref, w_ref, m_ref, v_ref, d_ref, nm_ref, nv_ref):
        d, nm, nv = _adamw_math(w_ref[...], g_ref[...], m_ref[...], v_ref[...])
        d_ref[...] = d
        nm_ref[...] = nm
        nv_ref[...] = nv

    return pl.pallas_call(
        body, name=name,
        out_shape=(jax.ShapeDtypeStruct(g.shape, F32),) * 3,
        in_specs=[VMEM_SPEC] * 4, out_specs=(VMEM_SPEC,) * 3,
    )(g, w, m, v)


def _pack_rows(pieces):
    flat = jnp.concatenate([p.reshape(-1) for p in pieces])
    n = flat.shape[0]
    padded = -(-n // (8 * LANES)) * (8 * LANES)
    return jnp.pad(flat, (0, padded - n)).reshape(padded // LANES, LANES)


def _unpack_rows(pack, shapes):
    flat = pack.reshape(-1)
    out, pos = [], 0
    for s in shapes:
        n = int(np.prod(s))
        out.append(flat[pos:pos + n].reshape(s))
        pos += n
    return out


def kernel(x, rel_bias, ab_norm, ab_w_in, ab_conv_w, ab_conv_b, ab_ln_g, ab_ln_b, ab_w_out, sc_norm, sc_w_in, sc_conv_w, sc_w_out, mlp_norm, mlp_w_up, mlp_w_down, final_norm, loss_target, m_rel_bias, m_ab_norm, m_ab_w_in, m_ab_conv_w, m_ab_conv_b, m_ab_ln_g, m_ab_ln_b, m_ab_w_out, m_sc_norm, m_sc_w_in, m_sc_conv_w, m_sc_w_out, m_mlp_norm, m_mlp_w_up, m_mlp_w_down, m_final_norm, v_rel_bias, v_ab_norm, v_ab_w_in, v_ab_conv_w, v_ab_conv_b, v_ab_ln_g, v_ab_ln_b, v_ab_w_out, v_sc_norm, v_sc_w_in, v_sc_conv_w, v_sc_w_out, v_mlp_norm, v_mlp_w_up, v_mlp_w_down, v_final_norm):
    S, D = x.shape[1], x.shape[2]
    CA = ab_conv_b.shape[1]
    C2 = 2 * CA
    AB_IN = C2 + ATTN_IN
    me_lin = _linear(_my_position())
    xs = x.reshape(S, D)
    tgt = loss_target.reshape(S, D)

    cw_sh = ab_conv_w.shape[2]
    scn_sh = sc_norm.shape[1]
    scw_sh = sc_conv_w.shape[2]
    small_sh_shapes = [(CONV_A_WIDTH, cw_sh), (scn_sh,), (SC_CONV_WIDTH, scw_sh)]
    small_pack = _pack_rows([ab_conv_w[0], sc_norm[0], sc_conv_w[0]])
    w_in_sh = ab_w_in[0].astype(BF16)
    ag_ab, wait_ab = _allgather_start(
        [small_pack, w_in_sh, ab_w_out[0].astype(BF16)], ["slot", "slot", "row"],
        [(N_DEV,) + small_pack.shape, (N_DEV,) + w_in_sh.shape, (N_DEV * ab_w_out.shape[1], D)],
        name="allgather_ab")
    ag_mlp, wait_mlp = _allgather_start(
        [mlp_w_up.astype(BF16), mlp_w_down.astype(BF16)], ["lcol", "lrow"],
        [(2, D, N_DEV * mlp_w_up.shape[2]), (2, N_DEV * mlp_w_down.shape[1], D)], name="allgather_mlp",
        dep=ag_ab[3])
    ag_sc, wait_sc = _allgather_start(
        [sc_w_in[0].astype(BF16), sc_w_out[0].astype(BF16)], ["col", "row"],
        [(D, N_DEV * sc_w_in.shape[2]), (N_DEV * sc_w_out.shape[1], D)], name="allgather_sc",
        dep=ag_mlp[3])

    buckets = _bucket_tables()
    biases = _bias_tables(rel_bias, buckets, name="bias_tables")

    dils = [dil for _, dil in DILATED_GROUPS]
    n0_all = _rmsnorm_fwd(xs, ab_norm, name="norm_ab", dep=ag_sc[3], views=dils[1:])
    n0 = n0_all[0]
    small_params, w_in_g, w_out = wait_ab(n0)
    per_dev = [_unpack_rows(small_params[s], small_sh_shapes) for s in range(N_DEV)]
    conv_w_full = jnp.concatenate([p[0] for p in per_dev], axis=1)
    sc_norm_full = jnp.concatenate([p[1] for p in per_dev], axis=0)[None]
    sc_conv_full = jnp.concatenate([p[2] for p in per_dev], axis=1)
    w_in = jnp.transpose(w_in_g, (1, 0, 2)).reshape(D, AB_IN)
    w_c = w_in[:, :C2]
    w_q = w_in[:, C2:]
    w_grp = [jnp.concatenate([w_q[:, t * N_GROUPS * ATTN_OUT + g * ATTN_OUT:][:, :ATTN_OUT] for t in range(3)], axis=1)
             for g in range(N_GROUPS)]
    uc = _mm_nn(n0, w_c, out_dtype=F32, name="mm_ab_in_conv", wide=True)
    uqs = [_mm_nn(n0_all[g], w_grp[g], out_dtype=BF16, slabs=dils[g], name=f"mm_ab_in_qkv{g}")
           for g in range(N_GROUPS)]
    ya, hglu, ct = _conv_fwd(uc, conv_w_full, ab_conv_b, ab_ln_g, ab_ln_b, name="conv_fwd")
    outs, lses = zip(*[_attn_fwd(uqs[g], biases, g, dils[g], name=f"attn_fwd_{g}") for g in range(N_GROUPS)])
    cat, outf, lse = _attn_merge(outs, lses, ya, name="attn_merge")
    h1 = _mm_nn(cat, w_out, out_dtype=F32, residual=xs, name="mm_ab_out", wide=True)
    w_up, w_dn = wait_mlp(h1)
    n1, z0 = _norm_mm_nn(h1, mlp_norm[0:1], (w_up, 0), out_dtype=BF16, name="norm_mm_up0", wide=True)
    h2 = _mm_nn(z0, (w_dn, 0), out_dtype=F32, residual=h1, a_fn=_relu_sq, name="mm_down0", wide=True)
    w_sc_in, w_sc_out = wait_sc(h2)
    n2, u3 = _norm_mm_nn(h2, sc_norm_full, w_sc_in, out_dtype=BF16, name="norm_mm_sc_in", wide=True)
    ysc = _sc_fwd(u3, sc_conv_full, name="sc_fwd")
    h3 = _mm_nn(ysc, w_sc_out, out_dtype=F32, residual=h2, name="mm_sc_out", wide=True)
    n3, z1 = _norm_mm_nn(h3, mlp_norm[1:2], (w_up, 1), out_dtype=BF16, name="norm_mm_up1", wide=True)
    h4 = _mm_nn(z1, (w_dn, 1), out_dtype=F32, residual=h3, a_fn=_relu_sq, name="mm_down1", wide=True)

    def dz_epilogue(acc, z):
        return acc * (2.0 * jnp.maximum(z.astype(F32), 0.0))

    dh4, dh4b, acc_final = _loss_bwd(h4, tgt, final_norm[None], name="loss_bwd")
    dz1 = _mm_nt([(dh4b, (w_dn, 1))], out_dtype=BF16, epilogue=dz_epilogue, extra=z1, name="mm_d_down1")
    g_dn1 = _mm_tn(z1, dh4b, a_fn=_relu_sq, name="mm_gw_down1")
    g_up1 = _mm_tn(n3, dz1, name="mm_gw_up1")
    rs_mlp1, wait_rs_mlp1 = _scatter_start([g_up1, g_dn1], ["col", "row"],
                                           [mlp_w_up.shape[1:], mlp_w_down.shape[1:]], name="scatter_mlp1")
    dh3, dh3b, acc_mlp1 = _mm_nt_rms_bwd(dz1, (w_up, 1), h3, mlp_norm[1:2], dh4, name="mm_d_up1_norm_bwd",
                                         dep=rs_mlp1[3])

    dysc = _mm_nt([(dh3b, w_sc_out)], out_dtype=F32, name="mm_d_sc_out")
    g_sc_out = _mm_tn(ysc, dh3b, name="mm_gw_sc_out")
    du3, acc_scw = _sc_bwd(u3, dysc, sc_conv_full, name="sc_bwd")
    g_sc_in = _mm_tn(n2, du3, name="mm_gw_sc_in")
    rs_sc, wait_rs_sc = _scatter_start([g_sc_in, g_sc_out], ["col", "row"],
                                       [sc_w_in.shape[1:], sc_w_out.shape[1:]], name="scatter_sc")
    dh2, dh2b, acc_sc = _mm_nt_rms_bwd(du3, w_sc_in, h2, sc_norm_full, dh3, name="mm_d_sc_in_norm_bwd",
                                       dep=rs_sc[3])

    dz0 = _mm_nt([(dh2b, (w_dn, 0))], out_dtype=BF16, epilogue=dz_epilogue, extra=z0, name="mm_d_down0")
    g_dn0 = _mm_tn(z0, dh2b, a_fn=_relu_sq, name="mm_gw_down0")
    g_up0 = _mm_tn(n1, dz0, name="mm_gw_up0")
    rs_mlp0, wait_rs_mlp0 = _scatter_start([g_up0, g_dn0], ["col", "row"],
                                           [mlp_w_up.shape[1:], mlp_w_down.shape[1:]], name="scatter_mlp0")
    dh1, dh1b, acc_mlp0 = _mm_nt_rms_bwd(dz0, (w_up, 0), h1, mlp_norm[0:1], dh2, name="mm_d_up0_norm_bwd",
                                         dep=rs_mlp0[3])

    dcat = _mm_nt([(dh1b, w_out)], out_dtype=F32, name="mm_d_ab_out")
    g_ab_out = _mm_tn(cat, dh1b, name="mm_gw_ab_out")
    prep = _attn_prep(dcat, outf, lse, name="attn_prep")
    dqkv, dbs = zip(*[_attn_bwd(uqs[g], prep[g][0], prep[g][2], prep[g][1], biases, g, dils[g],
                                name=f"attn_bwd_{g}") for g in range(N_GROUPS)])
    drel = _relbias_reduce(dbs, buckets, name="relbias_reduce")
    dc, acc_conv = _conv_bwd_ln(ct, dcat, hglu, ab_ln_g, ab_ln_b, name="conv_bwd_ln")
    duc = _conv_bwd_in(dc, uc, conv_w_full, name="conv_bwd_in")
    g_wc = _mm_tn(n0, duc, name="mm_gw_ab_in_conv")
    g_wgrp = [_mm_tn(n0_all[g], dqkv[g], slabs=dils[g], name=f"mm_gw_ab_in_qkv{g}") for g in range(N_GROUPS)]
    g_wq = jnp.concatenate([g_wgrp[g][:, t * ATTN_OUT:(t + 1) * ATTN_OUT]
                            for t in range(3) for g in range(N_GROUPS)], axis=1)
    g_w_in = jnp.concatenate([g_wc, g_wq], axis=1).reshape(D, N_DEV, AB_IN // N_DEV).transpose(1, 0, 2)
    rs_ab, wait_rs_ab = _scatter_start([g_w_in, g_ab_out], ["slot", "row"],
                                       [w_in_sh.shape, ab_w_out.shape[1:]], name="scatter_ab")
    dn0 = _mm_nt([(duc, w_c), (dqkv[0], w_grp[0])], out_dtype=F32, name="mm_d_ab_in", dep=rs_ab[3])
    dn0_views = [(_mm_nt([(dqkv[g], w_grp[g])], out_dtype=F32, slabs=dils[g], name=f"mm_d_ab_in_qkv{g}"), dils[g])
                 for g in range(1, N_GROUPS)]
    grad_x, grad_xb, acc_ab = _rms_bwd(xs, ab_norm, dn0, dh1, name="norm_ab_bwd", dn_views=dn0_views)

    small_full = [drel[:, :, :NUM_BUCKETS].transpose(2, 0, 1).reshape(NUM_BUCKETS, N_GROUPS * HEADS_PER_GROUP),
                  acc_ab[0], acc_conv[0:CONV_A_WIDTH], acc_conv[32],
                  acc_conv[33], acc_conv[34], acc_sc[0], acc_scw[0:SC_CONV_WIDTH],
                  jnp.stack([acc_mlp0[0], acc_mlp1[0]]), acc_final[0], acc_final[1]]
    small_full_shapes = [p.shape for p in small_full]
    summed = _unpack_rows(_small_gather(_pack_rows(small_full), reduce=True, name="allreduce_small"),
                          small_full_shapes)
    (s_rel, s_abn, s_cw, s_cb, s_lg, s_lb, s_scn, s_scw, s_mlpn, s_fn, s_err) = summed
    loss = (0.5 / D) * jnp.sum(s_err)
    small_grads = {
        "rel_bias": s_rel, "ab_norm": s_abn[None],
        "ab_conv_w": lax.dynamic_slice_in_dim(s_cw, me_lin * cw_sh, cw_sh, axis=1)[None],
        "ab_conv_b": s_cb[None], "ab_ln_g": s_lg[None], "ab_ln_b": s_lb[None],
        "sc_norm": lax.dynamic_slice_in_dim(s_scn, me_lin * scn_sh, scn_sh, axis=0)[None],
        "sc_conv_w": lax.dynamic_slice_in_dim(s_scw, me_lin * scw_sh, scw_sh, axis=1)[None],
        "mlp_norm": s_mlpn, "final_norm": s_fn,
    }
    small_w = {"rel_bias": (rel_bias, m_rel_bias, v_rel_bias), "ab_norm": (ab_norm, m_ab_norm, v_ab_norm),
               "ab_conv_w": (ab_conv_w, m_ab_conv_w, v_ab_conv_w), "ab_conv_b": (ab_conv_b, m_ab_conv_b, v_ab_conv_b),
               "ab_ln_g": (ab_ln_g, m_ab_ln_g, v_ab_ln_g), "ab_ln_b": (ab_ln_b, m_ab_ln_b, v_ab_ln_b),
               "sc_norm": (sc_norm, m_sc_norm, v_sc_norm), "sc_conv_w": (sc_conv_w, m_sc_conv_w, v_sc_conv_w),
               "mlp_norm": (mlp_norm, m_mlp_norm, v_mlp_norm), "final_norm": (final_norm, m_final_norm, v_final_norm)}
    small_names = list(small_grads)
    small_shapes = [small_grads[n].shape for n in small_names]
    d_pack, m_pack, v_pack = _adamw_small(
        _pack_rows([small_grads[n] for n in small_names]), _pack_rows([small_w[n][0] for n in small_names]),
        _pack_rows([small_w[n][1] for n in small_names]), _pack_rows([small_w[n][2] for n in small_names]),
        name="adamw_small")
    small = {n: (small_grads[n], d, nm_, nv_) for n, d, nm_, nv_ in zip(
        small_names, _unpack_rows(d_pack, small_shapes), _unpack_rows(m_pack, small_shapes),
        _unpack_rows(v_pack, small_shapes))}

    p_up1, p_dn1 = wait_rs_mlp1(grad_xb)
    p_sc_in, p_sc_out = wait_rs_sc(grad_xb)
    p_up0, p_dn0 = wait_rs_mlp0(grad_xb)
    p_w_in, p_ab_out = wait_rs_ab(grad_xb)
    big = {}
    for nm, parts, w, m, v in (("ab_w_in", [p_w_in], ab_w_in, m_ab_w_in, v_ab_w_in),
                               ("ab_w_out", [p_ab_out], ab_w_out, m_ab_w_out, v_ab_w_out),
                               ("sc_w_in", [p_sc_in], sc_w_in, m_sc_w_in, v_sc_w_in),
                               ("sc_w_out", [p_sc_out], sc_w_out, m_sc_w_out, v_sc_w_out),
                               ("mlp_w_up", [p_up0, p_up1], mlp_w_up, m_mlp_w_up, v_mlp_w_up),
                               ("mlp_w_down", [p_dn0, p_dn1], mlp_w_down, m_mlp_w_down, v_mlp_w_down)):
        C = w.shape[-1]
        res = _adamw_from_partials(parts, w.reshape(-1, C), m.reshape(-1, C), v.reshape(-1, C), name="adamw_" + nm)
        big[nm] = tuple(r.reshape(w.shape) for r in res)

    order = ["rel_bias", "ab_norm", "ab_w_in", "ab_conv_w", "ab_conv_b", "ab_ln_g", "ab_ln_b", "ab_w_out",
             "sc_norm", "sc_w_in", "sc_conv_w", "sc_w_out", "mlp_norm", "mlp_w_up", "mlp_w_down", "final_norm"]
    allres = {**big, **small}
    return (loss, grad_x.reshape(x.shape),
            *[allres[n][0] for n in order], *[allres[n][1] for n in order],
            *[allres[n][2] for n in order], *[allres[n][3] for n in order])
```

```python
import functools
import math

import numpy as np
import jax
import jax.numpy as jnp
from jax import lax
from jax.experimental import pallas as pl
from jax.experimental.pallas import tpu as pltpu

F32 = jnp.float32
BF16 = jnp.bfloat16

HEAD_DIM = 64
HEADS_PER_GROUP = 8
DILATED_GROUPS = ((128, 1), (512, 4), (2048, 16))
N_GROUPS = 3
ATTN_OUT = HEADS_PER_GROUP * HEAD_DIM
ATTN_IN = 3 * N_GROUPS * ATTN_OUT
GROUP_QKV = 3 * ATTN_OUT
ATTN_BLK = 128
CONV_A_WIDTH = 31
SC_CONV_WIDTH = 3
NUM_BUCKETS = 32
REL_MAX_DISTANCE = 2048
RMS_EPS = 1e-6
LN_EPS = 1e-5
NEG_INF = -1e30
ADAM_LR = 0.001
ADAM_B1 = 0.9
ADAM_B2 = 0.999
ADAM_EPS = 1e-08
ADAM_WD = 0.01
ADAM_STEP = 10

N_DEV = 8
HALO = 32
LANES = 128
VMEM_LIMIT = 56 * 1024 * 1024
MESH = pl.DeviceIdType.MESH
ANY = pl.BlockSpec(memory_space=pl.ANY)
VMEM_SPEC = pl.BlockSpec(memory_space=pltpu.VMEM)


def _tile(n, prefs):
    for t in prefs:
        if n % t == 0:
            return t
    return n


def _cparams(*sem):
    return pltpu.CompilerParams(dimension_semantics=sem, vmem_limit_bytes=VMEM_LIMIT)


def _relu_sq(z):
    return jnp.square(jnp.maximum(z, 0))


def _dot_nt(a, b):
    return lax.dot_general(a, b, (((1,), (1,)), ((), ())), preferred_element_type=F32)


def _dot_tn(a, b):
    return lax.dot_general(a, b, (((0,), (0,)), ((), ())), preferred_element_type=F32)


def _weight(b):
    if not isinstance(b, tuple):
        return b, b.shape, pl.BlockSpec
    arr, layer = b

    def spec(block, index_map):
        return pl.BlockSpec((None,) + tuple(block), lambda *g: (layer,) + tuple(index_map(*g)))

    return arr, arr.shape[1:], spec


def _mm_nn(a, b, *, out_dtype, name, residual=None, a_fn=None, slabs=1, wide=False):
    M, K = a.shape
    K //= slabs
    b, (_, N), b_spec = _weight(b)
    tm = _tile(M, (1024, 512, 256) if wide else (2048, 1024, 512, 256))
    tn = _tile(N, (1024, 512, 384, 256, 128) if wide else (512, 384, 256, 128))
    tk = _tile(K, (2048, 1024, 512, 256, 128) if wide else (1024, 512, 256, 128))
    nk = K // tk
    nj = N // tn
    has_res = residual is not None

    def body(*refs):
        if has_res:
            a_ref, b_ref, r_ref, o_ref = refs[:4]
        else:
            a_ref, b_ref, o_ref = refs[:3]
        av = a_ref[...]
        if a_fn is not None:
            av = a_fn(av)
        part = jnp.dot(av, b_ref[...], preferred_element_type=F32)

        def finish(acc):
            if has_res:
                acc = acc + r_ref[...]
            o_ref[...] = acc.astype(o_ref.dtype)

        if nk == 1:
            finish(part)
        else:
            acc_ref = refs[-1]
            k = pl.program_id(2)

            @pl.when(k == 0)
            def _():
                acc_ref[...] = part

            @pl.when((k > 0) & (k < nk - 1))
            def _():
                acc_ref[...] += part

            @pl.when(k == nk - 1)
            def _():
                finish(acc_ref[...] + part)

    in_specs = [pl.BlockSpec((tm, tk), lambda i, j, k: (i, (j // nj) * nk + k)),
                b_spec((tk, tn), lambda i, j, k: (k, j % nj))]
    args = [a, b]
    if has_res:
        in_specs.append(pl.BlockSpec((tm, tn), lambda i, j, k: (i, j)))
        args.append(residual)
    return pl.pallas_call(
        body, name=name,
        out_shape=jax.ShapeDtypeStruct((M, slabs * N), out_dtype),
        grid=(M // tm, slabs * nj, nk),
        in_specs=in_specs,
        out_specs=pl.BlockSpec((tm, tn), lambda i, j, k: (i, j)),
        scratch_shapes=[pltpu.VMEM((tm, tn), F32)] if nk > 1 else [],
        compiler_params=_cparams("parallel", "parallel", "arbitrary"),
    )(*args)


def _norm_mm_nn(h, g, b, *, out_dtype, name, wide=False):
    M, K = h.shape
    b, (_, N), b_spec = _weight(b)
    tm = _tile(M, (2048, 1024, 512, 256))
    tn = _tile(N, (1024, 512, 384, 256, 128) if wide else (512, 384, 256, 128))

    def body(h_ref, g_ref, b_ref, n_ref, o_ref):
        @pl.when(pl.program_id(1) == 0)
        def _():
            x = h_ref[...]
            r = lax.rsqrt(jnp.mean(x * x, axis=-1, keepdims=True) + RMS_EPS)
            n_ref[...] = (x * r * g_ref[...]).astype(BF16)

        o_ref[...] = jnp.dot(n_ref[...], b_ref[...], preferred_element_type=F32).astype(o_ref.dtype)

    return pl.pallas_call(
        body, name=name,
        out_shape=(jax.ShapeDtypeStruct((M, K), BF16), jax.ShapeDtypeStruct((M, N), out_dtype)),
        grid=(M // tm, N // tn),
        in_specs=[pl.BlockSpec((tm, K), lambda i, j: (i, 0)), pl.BlockSpec((1, K), lambda i, j: (0, 0)),
                  b_spec((K, tn), lambda i, j: (0, j))],
        out_specs=(pl.BlockSpec((tm, K), lambda i, j: (i, 0)), pl.BlockSpec((tm, tn), lambda i, j: (i, j))),
        compiler_params=_cparams("parallel", "arbitrary"),
    )(h, g, b)


def _mm_nt(pairs, *, out_dtype, name, epilogue=None, extra=None, dep=None, slabs=1):
    assert slabs == 1 or len(pairs) == 1
    M = pairs[0][0].shape[0]
    weights = [_weight(p[1]) for p in pairs]
    Ko = weights[0][1][0]
    tm = _tile(M, (2048, 1024, 512, 256))
    to = _tile(Ko, (1024, 512, 256, 128))
    njo = Ko // to
    tks = [_tile(p[0].shape[1] // slabs, (1024, 768, 512, 256, 128)) for p in pairs]
    steps = [p[0].shape[1] // slabs // tk for p, tk in zip(pairs, tks)]
    offs = [sum(steps[:i]) for i in range(len(pairs))]
    nk = sum(steps)
    npair = len(pairs)
    has_extra = extra is not None

    def body(*refs):
        ab = refs[:2 * npair]
        pos = 2 * npair
        e_ref = None
        if has_extra:
            e_ref = refs[pos]
            pos += 1
        if dep is not None:
            pos += 1
        o_ref = refs[pos]
        acc_ref = refs[pos + 1]
        k = pl.program_id(2)

        @pl.when(k == 0)
        def _():
            acc_ref[...] = jnp.zeros_like(acc_ref)

        for p in range(npair):
            @pl.when((k >= offs[p]) & (k < offs[p] + steps[p]))
            def _(p=p):
                acc_ref[...] += _dot_nt(ab[2 * p][...], ab[2 * p + 1][...])

        @pl.when(k == nk - 1)
        def _():
            acc = acc_ref[...]
            if epilogue is not None:
                acc = epilogue(acc, e_ref[...] if has_extra else None)
            o_ref[...] = acc.astype(o_ref.dtype)

    in_specs, args = [], []
    for p, (a, b) in enumerate(pairs):
        def kidx(k, p=p):
            return jnp.clip(k - offs[p], 0, steps[p] - 1)
        in_specs.append(pl.BlockSpec((tm, tks[p]),
                                     lambda i, j, k, kidx=kidx, p=p: (i, (j // njo) * steps[p] + kidx(k))))
        in_specs.append(weights[p][2]((to, tks[p]), lambda i, j, k, kidx=kidx: (j % njo, kidx(k))))
        args += [a, weights[p][0]]
    if has_extra:
        in_specs.append(pl.BlockSpec((tm, to), lambda i, j, k: (i, j)))
        args.append(extra)
    if dep is not None:
        in_specs.append(ANY)
        args.append(dep)
    return pl.pallas_call(
        body, name=name,
        out_shape=jax.ShapeDtypeStruct((M, slabs * Ko), out_dtype),
        grid=(M // tm, slabs * njo, nk),
        in_specs=in_specs,
        out_specs=pl.BlockSpec((tm, to), lambda i, j, k: (i, j)),
        scratch_shapes=[pltpu.VMEM((tm, to), F32)],
        compiler_params=_cparams("parallel", "parallel", "arbitrary"),
    )(*args)


def _mm_tn(a, b, *, name, a_fn=None, slabs=1):
    M, K = a.shape
    K //= slabs
    N = b.shape[1] // slabs
    tm = _tile(M, (4096, 2048, 1024, 512, 256))
    tk = _tile(K, (1024, 768, 512, 384, 256, 128))
    tn = _tile(N, (1024, 768, 512, 384, 256, 128))
    nmi = M // tm
    nm = slabs * nmi
    nki, nnj = K // tk, N // tn

    def body(a_ref, b_ref, o_ref, acc_ref):
        m = pl.program_id(2)
        av = a_ref[...]
        if a_fn is not None:
            av = a_fn(av)
        part = _dot_tn(av, b_ref[...])
        if nm == 1:
            o_ref[...] = part.astype(o_ref.dtype)
            return

        @pl.when(m == 0)
        def _():
            acc_ref[...] = part

        @pl.when((m > 0) & (m < nm - 1))
        def _():
            acc_ref[...] += part

        @pl.when(m == nm - 1)
        def _():
            o_ref[...] = (acc_ref[...] + part).astype(o_ref.dtype)

    return pl.pallas_call(
        body, name=name,
        out_shape=jax.ShapeDtypeStruct((K, N), BF16),
        grid=(K // tk, N // tn, nm),
        in_specs=[pl.BlockSpec((tm, tk), lambda i, j, m: (m % nmi, (m // nmi) * nki + i)),
                  pl.BlockSpec((tm, tn), lambda i, j, m: (m % nmi, (m // nmi) * nnj + j))],
        out_specs=pl.BlockSpec((tk, tn), lambda i, j, m: (i, j)),
        scratch_shapes=[pltpu.VMEM((tk, tn), F32)],
        compiler_params=_cparams("parallel", "parallel", "arbitrary"),
    )(a, b)


def _rmsnorm_fwd(h, g, *, name, dep=None, views=()):
    S, D = h.shape
    tm = _tile(S, (512, 256))
    nv = len(views)

    def body(h_ref, g_ref, *rest):
        n_out = 1 + nv
        outs = rest[len(rest) - n_out - (1 if nv else 0):len(rest) - (1 if nv else 0)]
        x = h_ref[...]
        r = lax.rsqrt(jnp.mean(x * x, axis=-1, keepdims=True) + RMS_EPS)
        y = x * r * g_ref[...]
        outs[0][...] = y.astype(BF16)
        if nv:
            scr = rest[-1]
            _to_chunks(scr, y)
            for v_ref, d in zip(outs[1:], views):
                _slabs_from_chunks(v_ref, scr, d, BF16)

    res = pl.pallas_call(
        body, name=name,
        out_shape=(jax.ShapeDtypeStruct((S, D), BF16),)
        + tuple(jax.ShapeDtypeStruct((S // d, d * D), BF16) for d in views),
        grid=(S // tm,),
        in_specs=[pl.BlockSpec((tm, D), lambda i: (i, 0)), pl.BlockSpec((1, D), lambda i: (0, 0))]
        + ([ANY] if dep is not None else []),
        out_specs=(pl.BlockSpec((tm, D), lambda i: (i, 0)),)
        + tuple(pl.BlockSpec((tm // d, d * D), lambda i: (i, 0)) for d in views),
        scratch_shapes=[_chunk_scratch(tm, D)] if nv else [],
        compiler_params=_cparams("parallel"),
    )(h, g, *([dep] if dep is not None else []))
    return res if nv else res[0]


def _rms_bwd_rows(x, g, dy):
    r = lax.rsqrt(jnp.mean(x * x, axis=-1, keepdims=True) + RMS_EPS)
    xh = x * r
    gy = dy * g
    dx = r * (gy - xh * jnp.mean(xh * gy, axis=-1, keepdims=True))
    return dx, dy * xh


def _rms_bwd(x, g, dn, dres, *, name, dn_views=()):
    S, D = x.shape
    tm = _tile(S, (512, 256))
    nv = len(dn_views)

    def body(x_ref, g_ref, dn_ref, dr_ref, *rest):
        v_refs = rest[:nv]
        dx_ref, dxb_ref, dg_ref = rest[nv:nv + 3]
        scr = rest[nv + 3:]
        i = pl.program_id(0)
        dn = dn_ref[...]
        for v_ref, s_ref, (_, d) in zip(v_refs, scr, dn_views):
            _chunks_from_slabs(s_ref, v_ref, d)
            dn = dn + _from_chunks(s_ref)
        dx, dgx = _rms_bwd_rows(x_ref[...], g_ref[...], dn)
        tot = dr_ref[...] + dx
        dx_ref[...] = tot
        dxb_ref[...] = tot.astype(BF16)

        @pl.when(i == 0)
        def _():
            dg_ref[...] = jnp.zeros_like(dg_ref)

        dg_ref[0:1, :] += jnp.sum(dgx, axis=0, keepdims=True)

    row = pl.BlockSpec((tm, D), lambda i: (i, 0))
    return pl.pallas_call(
        body, name=name,
        out_shape=(jax.ShapeDtypeStruct((S, D), F32), jax.ShapeDtypeStruct((S, D), BF16),
                   jax.ShapeDtypeStruct((8, D), F32)),
        grid=(S // tm,),
        in_specs=[row, pl.BlockSpec((1, D), lambda i: (0, 0)), row, row]
        + [pl.BlockSpec((tm // d, d * D), lambda i: (i, 0)) for _, d in dn_views],
        out_specs=(row, row, pl.BlockSpec((8, D), lambda i: (0, 0))),
        scratch_shapes=[_chunk_scratch(tm, D)] * nv,
        compiler_params=_cparams("arbitrary"),
    )(x, g, dn, dres, *[a for a, _ in dn_views])


def _mm_nt_rms_bwd(a, b, x, g, dres, *, name, dep=None):
    M, N = a.shape
    b, (D, _), b_spec = _weight(b)
    tm = _tile(M, (1024, 512, 256))
    tk = _tile(N, (1024, 512, 256, 128))
    nk = N // tk

    def body(a_ref, b_ref, x_ref, g_ref, dr_ref, *rest):
        dx_ref, dxb_ref, dg_ref, acc_ref = rest[-4:]
        i = pl.program_id(0)
        k = pl.program_id(1)
        part = _dot_nt(a_ref[...], b_ref[...])

        @pl.when((i == 0) & (k == 0))
        def _():
            dg_ref[...] = jnp.zeros_like(dg_ref)

        @pl.when(k == 0)
        def _():
            acc_ref[...] = part

        @pl.when((k > 0) & (k < nk - 1))
        def _():
            acc_ref[...] += part

        @pl.when(k == nk - 1)
        def _():
            dn = part if nk == 1 else acc_ref[...] + part
            dx, dgx = _rms_bwd_rows(x_ref[...], g_ref[...], dn)
            tot = dr_ref[...] + dx
            dx_ref[...] = tot
            dxb_ref[...] = tot.astype(BF16)
            dg_ref[0:1, :] += jnp.sum(dgx, axis=0, keepdims=True)

    row = pl.BlockSpec((tm, D), lambda i, k: (i, 0))
    in_specs = [pl.BlockSpec((tm, tk), lambda i, k: (i, k)), b_spec((D, tk), lambda i, k: (0, k)),
                row, pl.BlockSpec((1, D), lambda i, k: (0, 0)), row]
    args = [a, b, x, g, dres]
    if dep is not None:
        in_specs.append(ANY)
        args.append(dep)
    return pl.pallas_call(
        body, name=name,
        out_shape=(jax.ShapeDtypeStruct((M, D), F32), jax.ShapeDtypeStruct((M, D), BF16),
                   jax.ShapeDtypeStruct((8, D), F32)),
        grid=(M // tm, nk),
        in_specs=in_specs,
        out_specs=(row, row, pl.BlockSpec((8, D), lambda i, k: (0, 0))),
        scratch_shapes=[pltpu.VMEM((tm, D), F32)],
        compiler_params=_cparams("arbitrary", "arbitrary"),
    )(*args)


def _loss_bwd(h, target, g, *, name):
    S, D = h.shape
    tm = _tile(S, (512, 256))

    def body(h_ref, t_ref, g_ref, dx_ref, dxb_ref, acc_ref):
        i = pl.program_id(0)
        x = h_ref[...]
        gv = g_ref[...]
        r = lax.rsqrt(jnp.mean(x * x, axis=-1, keepdims=True) + RMS_EPS)
        err = x * r * gv - t_ref[...]
        dx, dgx = _rms_bwd_rows(x, gv, err * (1.0 / D))
        dx_ref[...] = dx
        dxb_ref[...] = dx.astype(BF16)

        @pl.when(i == 0)
        def _():
            acc_ref[...] = jnp.zeros_like(acc_ref)

        acc_ref[0:1, :] += jnp.sum(dgx, axis=0, keepdims=True)
        acc_ref[1:2, :] += jnp.sum(err * err, axis=0, keepdims=True)

    row = pl.BlockSpec((tm, D), lambda i: (i, 0))
    return pl.pallas_call(
        body, name=name,
        out_shape=(jax.ShapeDtypeStruct((S, D), F32), jax.ShapeDtypeStruct((S, D), BF16),
                   jax.ShapeDtypeStruct((8, D), F32)),
        grid=(S // tm,),
        in_specs=[row, row, pl.BlockSpec((1, D), lambda i: (0, 0))],
        out_specs=(row, row, pl.BlockSpec((8, D), lambda i: (0, 0))),
        compiler_params=_cparams("arbitrary"),
    )(h, target, g)


SUBLANES = 8
CONV_ROWS = 64


def _build_shifted(ext_ref, rot_ref, ts, shifts=tuple(range(1, SUBLANES))):
    rows = ts + HALO - SUBLANES
    for j in shifts:
        rot_ref[j, 0:rows, :] = ext_ref[j:j + rows, :]


def _shifted(ext_ref, rot_ref, off, r0, nrows, cs):
    q, j = divmod(off, SUBLANES)
    start = SUBLANES * q + r0
    if j == 0:
        return ext_ref[start:start + nrows, cs]
    return rot_ref[j, start:start + nrows, cs]


def _conv_fwd(uc, conv_w, conv_b, ln_g, ln_b, *, name):
    S, C2 = uc.shape
    C = C2 // 2
    ts = _tile(S, (1024, 512, 256))
    per = ts // HALO

    def body(cur_ref, halo_ref, w_ref, b_ref, g_ref, beta_ref, ya_ref, h_ref, ct_ref, ext_ref, rot_ref):
        i = pl.program_id(0)
        hh = halo_ref[:, 0:C] * jax.nn.sigmoid(halo_ref[:, C:C2])
        ext_ref[0:HALO, :] = jnp.where(i == 0, 0.0, hh)
        hc = cur_ref[:, 0:C] * jax.nn.sigmoid(cur_ref[:, C:C2])
        ext_ref[HALO:HALO + ts, :] = hc
        h_ref[...] = hc
        _build_shifted(ext_ref, rot_ref, ts)
        for c0 in range(0, C, LANES):
            cs = slice(c0, c0 + LANES)
            for r0 in range(0, ts, CONV_ROWS):
                acc = jnp.zeros((CONV_ROWS, LANES), F32)
                for k in range(CONV_A_WIDTH):
                    acc = acc + w_ref[k:k + 1, cs] * _shifted(ext_ref, rot_ref, k + 2, r0, CONV_ROWS, cs)
                ct_ref[r0:r0 + CONV_ROWS, cs] = acc + b_ref[:, cs]
        ct = ct_ref[...]
        mu = jnp.mean(ct, axis=-1, keepdims=True)
        xc = ct - mu
        var = jnp.mean(xc * xc, axis=-1, keepdims=True)
        l = xc * lax.rsqrt(var + LN_EPS) * g_ref[...] + beta_ref[...]
        ya_ref[...] = (l * jax.nn.sigmoid(l)).astype(ya_ref.dtype)

    vec = pl.BlockSpec((1, C), lambda i: (0, 0))
    row = pl.BlockSpec((ts, C), lambda i: (i, 0))
    return pl.pallas_call(
        body, name=name,
        out_shape=(jax.ShapeDtypeStruct((S, C), BF16), jax.ShapeDtypeStruct((S, C), F32),
                   jax.ShapeDtypeStruct((S, C), F32)),
        grid=(S // ts,),
        in_specs=[pl.BlockSpec((ts, C2), lambda i: (i, 0)),
                  pl.BlockSpec((HALO, C2), lambda i: (jnp.maximum(i * per - 1, 0), 0)),
                  pl.BlockSpec((CONV_A_WIDTH, C), lambda i: (0, 0)), vec, vec, vec],
        out_specs=(row, row, row),
        scratch_shapes=[pltpu.VMEM((HALO + ts, C), F32), pltpu.VMEM((8, HALO + ts, C), F32)],
        compiler_params=_cparams("parallel"),
    )(uc, uc, conv_w, conv_b, ln_g, ln_b)


CONV_ACC_ROWS = 40


def _conv_bwd_ln(ct, dcat, hglu, ln_g, ln_b, *, name):
    S, C = ct.shape
    CW = dcat.shape[1]
    ts = _tile(S, (1024, 512, 256))
    per = ts // HALO

    def body(ct_ref, dcat_ref, hc_ref, hh_ref, g_ref, beta_ref, dc_ref, acc_ref, ext_ref, rot_ref):
        i = pl.program_id(0)
        ct = ct_ref[...]
        gv = g_ref[...]
        mu = jnp.mean(ct, axis=-1, keepdims=True)
        xc = ct - mu
        rstd = lax.rsqrt(jnp.mean(xc * xc, axis=-1, keepdims=True) + LN_EPS)
        xh = xc * rstd
        l = xh * gv + beta_ref[...]
        sg = jax.nn.sigmoid(l)
        dl = dcat_ref[:, 0:C] * (sg * (1.0 + l * (1.0 - sg)))
        dxh = dl * gv
        dc = rstd * (dxh - jnp.mean(dxh, axis=-1, keepdims=True)
                     - xh * jnp.mean(dxh * xh, axis=-1, keepdims=True))
        dc_ref[...] = dc

        @pl.when(i == 0)
        def _():
            acc_ref[...] = jnp.zeros_like(acc_ref)

        acc_ref[32:33, :] += jnp.sum(dc, axis=0, keepdims=True)
        acc_ref[33:34, :] += jnp.sum(dl * xh, axis=0, keepdims=True)
        acc_ref[34:35, :] += jnp.sum(dl, axis=0, keepdims=True)
        ext_ref[0:HALO, :] = jnp.where(i == 0, 0.0, hh_ref[...])
        ext_ref[HALO:HALO + ts, :] = hc_ref[...]
        _build_shifted(ext_ref, rot_ref, ts)
        for c0 in range(0, C, LANES):
            cs = slice(c0, c0 + LANES)
            dcc = dc_ref[:, cs]
            for k in range(CONV_A_WIDTH):
                acc_ref[k:k + 1, cs] += jnp.sum(dcc * _shifted(ext_ref, rot_ref, k + 2, 0, ts, cs),
                                                axis=0, keepdims=True)

    vec = pl.BlockSpec((1, C), lambda i: (0, 0))
    row = pl.BlockSpec((ts, C), lambda i: (i, 0))
    return pl.pallas_call(
        body, name=name,
        out_shape=(jax.ShapeDtypeStruct((S, C), F32), jax.ShapeDtypeStruct((CONV_ACC_ROWS, C), F32)),
        grid=(S // ts,),
        in_specs=[row, pl.BlockSpec((ts, CW), lambda i: (i, 0)), row,
                  pl.BlockSpec((HALO, C), lambda i: (jnp.maximum(i * per - 1, 0), 0)), vec, vec],
        out_specs=(row, pl.BlockSpec((CONV_ACC_ROWS, C), lambda i: (0, 0))),
        scratch_shapes=[pltpu.VMEM((HALO + ts, C), F32), pltpu.VMEM((8, HALO + ts, C), F32)],
        compiler_params=_cparams("arbitrary"),
    )(ct, dcat, hglu, hglu, ln_g, ln_b)


def _conv_bwd_in(dc, uc, conv_w, *, name):
    S, C = dc.shape
    C2 = 2 * C
    ts = _tile(S, (1024, 512, 256))
    per = ts // HALO
    nt = S // ts

    def body(dc_ref, dn_ref, uc_ref, w_ref, du_ref, ext_ref, rot_ref):
        i = pl.program_id(0)
        ext_ref[0:ts, :] = dc_ref[...]
        ext_ref[ts:ts + HALO, :] = jnp.where(i == nt - 1, 0.0, dn_ref[...])
        _build_shifted(ext_ref, rot_ref, ts)
        for c0 in range(0, C, LANES):
            cs = slice(c0, c0 + LANES)
            gs = slice(C + c0, C + c0 + LANES)
            for r0 in range(0, ts, CONV_ROWS):
                rs = slice(r0, r0 + CONV_ROWS)
                acc = jnp.zeros((CONV_ROWS, LANES), F32)
                for k in range(CONV_A_WIDTH):
                    acc = acc + w_ref[k:k + 1, cs] * _shifted(ext_ref, rot_ref, 30 - k, r0, CONV_ROWS, cs)
                sg = jax.nn.sigmoid(uc_ref[rs, gs])
                du_ref[rs, cs] = (acc * sg).astype(du_ref.dtype)
                du_ref[rs, gs] = (acc * uc_ref[rs, cs] * sg * (1.0 - sg)).astype(du_ref.dtype)

    return pl.pallas_call(
        body, name=name,
        out_shape=jax.ShapeDtypeStruct((S, C2), BF16),
        grid=(nt,),
        in_specs=[pl.BlockSpec((ts, C), lambda i: (i, 0)),
                  pl.BlockSpec((HALO, C), lambda i: (jnp.minimum((i + 1) * per, S // HALO - 1), 0)),
                  pl.BlockSpec((ts, C2), lambda i: (i, 0)),
                  pl.BlockSpec((CONV_A_WIDTH, C), lambda i: (0, 0))],
        out_specs=pl.BlockSpec((ts, C2), lambda i: (i, 0)),
        scratch_shapes=[pltpu.VMEM((ts + HALO, C), F32), pltpu.VMEM((8, ts + HALO, C), F32)],
        compiler_params=_cparams("parallel"),
    )(dc, dc, uc, conv_w)


SC_SHIFTS_BACK = ((HALO - 2) % SUBLANES, (HALO - 1) % SUBLANES)
SC_SHIFTS_AHEAD = (1, 2)


def _sc_fwd(u3, conv_w, *, name):
    S, W3 = u3.shape
    W = W3 // 3
    ts = _tile(S, (256,))
    per = ts // HALO

    def body(cur_ref, halo_ref, w_ref, y_ref, ext_ref, rot_ref):
        i = pl.program_id(0)
        cvh = halo_ref[:, W:2 * W].astype(F32) * halo_ref[:, 2 * W:W3].astype(F32)
        ext_ref[0:HALO, :] = jnp.where(i == 0, 0.0, cvh)
        ext_ref[HALO:HALO + ts, :] = cur_ref[:, W:2 * W].astype(F32) * cur_ref[:, 2 * W:W3].astype(F32)
        _build_shifted(ext_ref, rot_ref, ts, SC_SHIFTS_BACK)
        for c0 in range(0, W, LANES):
            cs = slice(c0, c0 + LANES)
            for r0 in range(0, ts, CONV_ROWS):
                rs = slice(r0, r0 + CONV_ROWS)
                k = (w_ref[0:1, cs] * _shifted(ext_ref, rot_ref, HALO - 2, r0, CONV_ROWS, cs)
                     + w_ref[1:2, cs] * _shifted(ext_ref, rot_ref, HALO - 1, r0, CONV_ROWS, cs)
                     + w_ref[2:3, cs] * _shifted(ext_ref, rot_ref, HALO, r0, CONV_ROWS, cs))
                y_ref[rs, cs] = (cur_ref[rs, cs].astype(F32) * k).astype(y_ref.dtype)

    return pl.pallas_call(
        body, name=name,
        out_shape=jax.ShapeDtypeStruct((S, W), BF16),
        grid=(S // ts,),
        in_specs=[pl.BlockSpec((ts, W3), lambda i: (i, 0)),
                  pl.BlockSpec((HALO, W3), lambda i: (jnp.maximum(i * per - 1, 0), 0)),
                  pl.BlockSpec((SC_CONV_WIDTH, W), lambda i: (0, 0))],
        out_specs=pl.BlockSpec((ts, W), lambda i: (i, 0)),
        scratch_shapes=[pltpu.VMEM((HALO + ts, W), F32), pltpu.VMEM((8, HALO + ts, W), F32)],
        compiler_params=_cparams("parallel"),
    )(u3, u3, conv_w)


def _sc_bwd(u3, dy, conv_w, *, name):
    S, W3 = u3.shape
    W = W3 // 3
    ts = _tile(S, (256,))
    per = ts // HALO
    nt = S // ts

    def body(cur_ref, prev_ref, next_ref, dy_ref, dyn_ref, w_ref, du_ref, dw_ref, cv_ext, dk_ext, cv_rot, dk_rot):
        i = pl.program_id(0)
        cvh = prev_ref[:, W:2 * W].astype(F32) * prev_ref[:, 2 * W:W3].astype(F32)
        cv_ext[0:HALO, :] = jnp.where(i == 0, 0.0, cvh)
        cv_ext[HALO:HALO + ts, :] = cur_ref[:, W:2 * W].astype(F32) * cur_ref[:, 2 * W:W3].astype(F32)
        dk_ext[0:ts, :] = dy_ref[...] * cur_ref[:, 0:W].astype(F32)
        dk_ext[ts:ts + HALO, :] = jnp.where(i == nt - 1, 0.0, dyn_ref[...] * next_ref[:, 0:W].astype(F32))
        _build_shifted(cv_ext, cv_rot, ts, SC_SHIFTS_BACK)
        _build_shifted(dk_ext, dk_rot, ts, SC_SHIFTS_AHEAD)

        @pl.when(i == 0)
        def _():
            dw_ref[...] = jnp.zeros_like(dw_ref)

        for c0 in range(0, W, LANES):
            cs = slice(c0, c0 + LANES)
            w0, w1, w2 = w_ref[0:1, cs], w_ref[1:2, cs], w_ref[2:3, cs]
            sums = [jnp.zeros((1, LANES), F32)] * SC_CONV_WIDTH
            for r0 in range(0, ts, CONV_ROWS):
                rs = slice(r0, r0 + CONV_ROWS)
                cv2 = _shifted(cv_ext, cv_rot, HALO - 2, r0, CONV_ROWS, cs)
                cv1 = _shifted(cv_ext, cv_rot, HALO - 1, r0, CONV_ROWS, cs)
                cv0 = _shifted(cv_ext, cv_rot, HALO, r0, CONV_ROWS, cs)
                dk = dk_ext[rs, cs]
                dcv = (w2 * dk + w1 * _shifted(dk_ext, dk_rot, 1, r0, CONV_ROWS, cs)
                       + w0 * _shifted(dk_ext, dk_rot, 2, r0, CONV_ROWS, cs))
                du_ref[rs, cs] = (dy_ref[rs, cs] * (w0 * cv2 + w1 * cv1 + w2 * cv0)).astype(du_ref.dtype)
                du_ref[rs, W + c0:W + c0 + LANES] = (
                    dcv * cur_ref[rs, 2 * W + c0:2 * W + c0 + LANES].astype(F32)).astype(du_ref.dtype)
                du_ref[rs, 2 * W + c0:2 * W + c0 + LANES] = (
                    dcv * cur_ref[rs, W + c0:W + c0 + LANES].astype(F32)).astype(du_ref.dtype)
                for t, cvt in enumerate((cv2, cv1, cv0)):
                    sums[t] = sums[t] + jnp.sum(dk * cvt, axis=0, keepdims=True)
            for t in range(SC_CONV_WIDTH):
                dw_ref[t:t + 1, cs] += sums[t]

    nxt = lambda i: (jnp.minimum((i + 1) * per, S // HALO - 1), 0)
    return pl.pallas_call(
        body, name=name,
        out_shape=(jax.ShapeDtypeStruct((S, W3), BF16), jax.ShapeDtypeStruct((8, W), F32)),
        grid=(nt,),
        in_specs=[pl.BlockSpec((ts, W3), lambda i: (i, 0)),
                  pl.BlockSpec((HALO, W3), lambda i: (jnp.maximum(i * per - 1, 0), 0)),
                  pl.BlockSpec((HALO, W3), nxt),
                  pl.BlockSpec((ts, W), lambda i: (i, 0)),
                  pl.BlockSpec((HALO, W), nxt),
                  pl.BlockSpec((SC_CONV_WIDTH, W), lambda i: (0, 0))],
        out_specs=(pl.BlockSpec((ts, W3), lambda i: (i, 0)), pl.BlockSpec((8, W), lambda i: (0, 0))),
        scratch_shapes=[pltpu.VMEM((HALO + ts, W), F32), pltpu.VMEM((ts + HALO, W), F32),
                        pltpu.VMEM((8, HALO + ts, W), F32), pltpu.VMEM((8, ts + HALO, W), F32)],
        compiler_params=_cparams("arbitrary"),
    )(u3, u3, u3, dy, dy, conv_w)


def _t5_causal_bucket(n):
    max_exact = NUM_BUCKETS // 2
    nf = jnp.maximum(n, 1).astype(F32)
    large = max_exact + (jnp.log(nf / max_exact) / math.log(REL_MAX_DISTANCE / max_exact)
                         * (NUM_BUCKETS - max_exact)).astype(jnp.int32)
    return jnp.where(n < max_exact, n, jnp.minimum(large, NUM_BUCKETS - 1))


def _bucket_tables():
    steps = ATTN_BLK
    m = jnp.arange(steps)[:, None] + steps - jnp.arange(2 * steps)[None, :]
    return jnp.stack([_t5_causal_bucket(jnp.clip(m, 0, steps) * dil).astype(F32) for _, dil in DILATED_GROUPS])


def _bias_tables(rel_bias, buckets, *, name):
    steps = ATTN_BLK

    def body(tab_ref, bk_ref, o_ref):
        g = pl.program_id(0)
        bk = bk_ref[0]
        a_idx = lax.broadcasted_iota(jnp.int32, (steps, 2 * steps), 0)
        c_idx = lax.broadcasted_iota(jnp.int32, (steps, 2 * steps), 1)
        m = a_idx + steps - c_idx
        band = (m >= 0) & (m <= steps)
        band_first = band & (c_idx >= steps)
        for h in range(HEADS_PER_GROUP):
            bias = jnp.zeros((steps, 2 * steps), F32)
            for b in range(NUM_BUCKETS):
                bias = jnp.where(bk == float(b), tab_ref[b, g * HEADS_PER_GROUP + h], bias)
            o_ref[0, 0, h] = jnp.where(band_first, bias, NEG_INF)
            o_ref[0, 1, h] = jnp.where(band, bias, NEG_INF)

    return pl.pallas_call(
        body, name=name,
        out_shape=jax.ShapeDtypeStruct((N_GROUPS, 2, HEADS_PER_GROUP, steps, 2 * steps), F32),
        grid=(N_GROUPS,),
        in_specs=[pl.BlockSpec(memory_space=pltpu.SMEM),
                  pl.BlockSpec((1, steps, 2 * steps), lambda g: (g, 0, 0))],
        out_specs=pl.BlockSpec((1, 2, HEADS_PER_GROUP, steps, 2 * steps), lambda g: (g, 0, 0, 0, 0)),
        compiler_params=_cparams("parallel"),
    )(rel_bias, buckets)


def _lane_is_low():
    return lax.broadcasted_iota(jnp.int32, (1, LANES), 1) < HEAD_DIM


def _stack_heads(x2, low):
    zero = jnp.zeros_like(x2)
    return jnp.concatenate([jnp.where(low, x2, zero), jnp.where(low, zero, x2)], axis=0)


ATTN_FWD_BLOCKS = 8


def _attn_fwd(uv, bias, g, d, *, name):
    rows = uv.shape[0]
    nsub = min(ATTN_FWD_BLOCKS, rows // ATTN_BLK)
    step_rows = nsub * ATTN_BLK
    nqb = GROUP_QKV // ATTN_OUT

    def body(q_ref, kc_ref, kp_ref, vc_ref, vp_ref, bias_ref, o_ref, l_ref):
        n = pl.program_id(1)
        low = _lane_is_low()
        slabs = [slice(hp * LANES, (hp + 1) * LANES) for hp in range(HEADS_PER_GROUP // 2)]
        for sub in range(nsub):
            qr = slice(sub * ATTN_BLK, (sub + 1) * ATTN_BLK)
            sel = jnp.minimum(n, 1) if sub == 0 else 1

            def with_prev(cur_ref, prev_ref, sl, sub=sub, qr=qr):
                prev = prev_ref[:, sl] if sub == 0 else cur_ref[(sub - 1) * ATTN_BLK:sub * ATTN_BLK, sl]
                return jnp.concatenate([prev, cur_ref[qr, sl]], axis=0)

            scores = [_dot_nt(_stack_heads(q_ref[qr, sl] * (HEAD_DIM ** -0.5), low), with_prev(kc_ref, kp_ref, sl))
                      for sl in slabs]
            probs, dens_all, lses_all = [], [], []
            for hp, s in enumerate(scores):
                ps, dens, lses = [], [], []
                for hh in range(2):
                    logits = s[hh * ATTN_BLK:(hh + 1) * ATTN_BLK] + bias_ref[sel, 2 * hp + hh]
                    mx = jnp.max(logits, axis=-1, keepdims=True)
                    p = jnp.exp(logits - mx)
                    den = jnp.sum(p, axis=-1, keepdims=True)
                    ps.append(p.astype(BF16))
                    dens.append(den)
                    lses.append(jnp.broadcast_to(mx + jnp.log(den), (ATTN_BLK, LANES)))
                probs.append(jnp.concatenate(ps, axis=0))
                dens_all.append(dens)
                lses_all.append(lses)
            for hp, sl in enumerate(slabs):
                pv = jnp.dot(probs[hp], with_prev(vc_ref, vp_ref, sl), preferred_element_type=F32)
                dens, lses = dens_all[hp], lses_all[hp]
                o_ref[qr, sl] = jnp.where(low, pv[0:ATTN_BLK] / dens[0], pv[ATTN_BLK:2 * ATTN_BLK] / dens[1])
                l_ref[qr, sl] = jnp.where(low, lses[0], lses[1])

    def cur(t):
        return pl.BlockSpec((step_rows, ATTN_OUT), lambda r, n: (n, r * nqb + t))

    def prev(t):
        return pl.BlockSpec((ATTN_BLK, ATTN_OUT), lambda r, n: (jnp.maximum(n * nsub - 1, 0), r * nqb + t))

    out_spec = pl.BlockSpec((step_rows, ATTN_OUT), lambda r, n: (n, r))
    return pl.pallas_call(
        body, name=name,
        out_shape=(jax.ShapeDtypeStruct((rows, d * ATTN_OUT), F32),) * 2,
        grid=(d, rows // step_rows),
        in_specs=[cur(0), cur(1), prev(1), cur(2), prev(2),
                  pl.BlockSpec((None, 2, HEADS_PER_GROUP, ATTN_BLK, 2 * ATTN_BLK), lambda r, n: (g, 0, 0, 0, 0))],
        out_specs=(out_spec, out_spec),
        compiler_params=_cparams("parallel", "parallel"),
    )(uv, uv, uv, uv, uv, bias)


def _chunk_scratch(n, width):
    return pltpu.VMEM((width // LANES, n, LANES), F32)


def _to_chunks(scr, val):
    for c in range(scr.shape[0]):
        scr[c] = val[:, c * LANES:(c + 1) * LANES]


def _from_chunks(scr):
    return jnp.concatenate([scr[c] for c in range(scr.shape[0])], axis=1)


def _slabs_from_chunks(dst_ref, scr, d, dtype):
    nc, n, _ = scr.shape
    for r in range(d):
        for c in range(nc):
            col = r * nc * LANES + c * LANES
            dst_ref[:, col:col + LANES] = scr[c, pl.ds(r, n // d, stride=d), :].astype(dtype)


def _chunks_from_slabs(scr, src_ref, d):
    nc, n, _ = scr.shape
    for r in range(d):
        for c in range(nc):
            col = r * nc * LANES + c * LANES
            scr[c, pl.ds(r, n // d, stride=d), :] = src_ref[:, col:col + LANES]


def _attn_merge(outs, lses, ya, *, name):
    S, C = ya.shape
    tm = _tile(S, (512, 256))
    dils = [dil for _, dil in DILATED_GROUPS]

    def body(o0, o1, o2, l0, l1, l2, ya_ref, cat_ref, out_ref, lse_ref, so1, so2, sl1, sl2):
        _chunks_from_slabs(so1, o1, dils[1])
        _chunks_from_slabs(so2, o2, dils[2])
        _chunks_from_slabs(sl1, l1, dils[1])
        _chunks_from_slabs(sl2, l2, dils[2])
        a0, a1, a2 = l0[...], _from_chunks(sl1), _from_chunks(sl2)
        m = jnp.maximum(jnp.maximum(a0, a1), a2)
        e0, e1, e2 = jnp.exp(a0 - m), jnp.exp(a1 - m), jnp.exp(a2 - m)
        den = e0 + e1 + e2
        out = (e0 * o0[...] + e1 * _from_chunks(so1) + e2 * _from_chunks(so2)) / den
        out_ref[...] = out
        lse_ref[...] = m + jnp.log(den)
        cat_ref[:, 0:C] = ya_ref[...]
        cat_ref[:, C:C + ATTN_OUT] = out.astype(cat_ref.dtype)

    blk = pl.BlockSpec((tm, ATTN_OUT), lambda i: (i, 0))
    vblk = [pl.BlockSpec((tm // d, d * ATTN_OUT), lambda i: (i, 0)) for d in dils]
    assert dils[0] == 1
    return pl.pallas_call(
        body, name=name,
        out_shape=(jax.ShapeDtypeStruct((S, C + ATTN_OUT), BF16), jax.ShapeDtypeStruct((S, ATTN_OUT), F32),
                   jax.ShapeDtypeStruct((S, ATTN_OUT), F32)),
        grid=(S // tm,),
        in_specs=vblk + vblk + [pl.BlockSpec((tm, C), lambda i: (i, 0))],
        out_specs=(pl.BlockSpec((tm, C + ATTN_OUT), lambda i: (i, 0)), blk, blk),
        scratch_shapes=[_chunk_scratch(tm, ATTN_OUT)] * 4,
        compiler_params=_cparams("parallel"),
    )(*outs, *lses, ya)


def _attn_prep(dcat, outf, lse, *, name):
    S, CW = dcat.shape
    C = CW - ATTN_OUT
    tm = _tile(S, (512, 256))
    dils = [dil for _, dil in DILATED_GROUPS]
    assert dils[0] == 1
    ones = np.kron(np.eye(HEADS_PER_GROUP, dtype=np.float32), np.ones((HEAD_DIM, HEAD_DIM), np.float32))

    nviews = 3 * (len(dils) - 1)

    def body(dcat_ref, out_ref, l_ref, ones_ref, dyb_ref, dl_ref, *rest):
        views = rest[:nviews]
        s_dyb, s_dl, s_l = rest[nviews:]
        dyb = dcat_ref[:, C:CW]
        dyb_ref[...] = dyb.astype(BF16)
        prod = dyb * out_ref[...]
        ov = ones_ref[...]
        hi, mid, lo = _split_bf16(prod)
        delta = (jnp.dot(hi, ov, preferred_element_type=F32)
                 + jnp.dot(mid, ov, preferred_element_type=F32)
                 + jnp.dot(lo, ov, preferred_element_type=F32))
        dl_ref[...] = delta
        _to_chunks(s_dyb, dyb)
        _to_chunks(s_dl, delta)
        _to_chunks(s_l, l_ref[...])
        for gi, d in enumerate(dils[1:]):
            dyb_v, dl_v, l_v = views[3 * gi:3 * gi + 3]
            _slabs_from_chunks(dyb_v, s_dyb, d, BF16)
            _slabs_from_chunks(dl_v, s_dl, d, F32)
            _slabs_from_chunks(l_v, s_l, d, F32)

    blk = pl.BlockSpec((tm, ATTN_OUT), lambda i: (i, 0))
    view_shapes, view_specs = [], []
    for d in dils[1:]:
        for dt in (BF16, F32, F32):
            view_shapes.append(jax.ShapeDtypeStruct((S // d, d * ATTN_OUT), dt))
            view_specs.append(pl.BlockSpec((tm // d, d * ATTN_OUT), lambda i: (i, 0)))
    res = pl.pallas_call(
        body, name=name,
        out_shape=(jax.ShapeDtypeStruct((S, ATTN_OUT), BF16), jax.ShapeDtypeStruct((S, ATTN_OUT), F32),
                   *view_shapes),
        grid=(S // tm,),
        in_specs=[pl.BlockSpec((tm, CW), lambda i: (i, 0)), blk, blk,
                  pl.BlockSpec((ATTN_OUT, ATTN_OUT), lambda i: (0, 0))],
        out_specs=(blk, blk, *view_specs),
        scratch_shapes=[_chunk_scratch(tm, ATTN_OUT)] * 3,
        compiler_params=_cparams("parallel"),
    )(dcat, outf, lse, jnp.asarray(ones, BF16))
    return [(res[0], res[1], lse)] + [tuple(res[2 + 3 * gi:5 + 3 * gi]) for gi in range(len(dils) - 1)]


ATTN_BWD_RESIDUES = 4


def _attn_bwd(uv, dov, lv, dv_, bias, g, d, *, name):
    rows = uv.shape[0]
    nb = rows // ATTN_BLK
    nres = min(d, ATTN_BWD_RESIDUES)
    steps = (d // nres) * nb
    scale = HEAD_DIM ** -0.5
    Q0, K0, V0 = 0, ATTN_OUT, 2 * ATTN_OUT

    def body(cur_ref, prev_ref, do_ref, l_ref, dl_ref, bias_ref, out_ref, db_ref, dq_s, dk_s, dv_s):
        t = pl.program_id(0)
        n = t % nb
        low = _lane_is_low()

        @pl.when(t == 0)
        def _():
            db_ref[...] = jnp.zeros_like(db_ref)
            dq_s[...] = jnp.zeros_like(dq_s)
            dk_s[...] = jnp.zeros_like(dk_s)
            dv_s[...] = jnp.zeros_like(dv_s)

        @pl.when(t < steps)
        def _():
            sel = jnp.minimum(n, 1)
            lane = lax.broadcasted_iota(jnp.int32, (1, LANES), 1)
            nk2 = 2 * ATTN_BLK
            for rr in range(nres):
                u0 = rr * GROUP_QKV
                o0 = rr * ATTN_OUT
                slabs = [hp * LANES for hp in range(HEADS_PER_GROUP // 2)]

                def cols(base, c0):
                    return slice(base + c0, base + c0 + LANES)

                keys = [jnp.concatenate([prev_ref[:, cols(u0 + K0, c0)], cur_ref[:, cols(u0 + K0, c0)]], axis=0)
                        for c0 in slabs]
                scores = [_dot_nt(_stack_heads(cur_ref[:, cols(u0 + Q0, c0)] * scale, low), keys[hp])
                          for hp, c0 in enumerate(slabs)]
                dps = [_dot_nt(_stack_heads(do_ref[:, cols(o0, c0)], low),
                               jnp.concatenate([prev_ref[:, cols(u0 + V0, c0)], cur_ref[:, cols(u0 + V0, c0)]],
                                               axis=0)) for c0 in slabs]
                stacked = []
                for hp, c0 in enumerate(slabs):
                    lse2 = l_ref[:, cols(o0, c0)]
                    dl2 = dl_ref[:, cols(o0, c0)]
                    pbs, dsbs = [], []
                    for hh in range(2):
                        rws = slice(hh * ATTN_BLK, (hh + 1) * ATTN_BLK)
                        one = lane == hh * HEAD_DIM
                        lse_col = jnp.sum(jnp.where(one, lse2, 0.0), axis=-1, keepdims=True)
                        dl_col = jnp.sum(jnp.where(one, dl2, 0.0), axis=-1, keepdims=True)
                        p = jnp.exp(scores[hp][rws] + bias_ref[sel, 2 * hp + hh] - lse_col)
                        ds = p * (dps[hp][rws] - dl_col)
                        db_ref[2 * hp + hh] += ds
                        pbs.append(p.astype(BF16))
                        dsbs.append((ds * scale).astype(BF16))
                    stacked.append((jnp.concatenate(dsbs, axis=0), jnp.concatenate(dsbs, axis=1),
                                    jnp.concatenate(pbs, axis=1)))
                for hp, c0 in enumerate(slabs):
                    ds_rows, ds_cols, p_cols = stacked[hp]
                    dq = jnp.dot(ds_rows, keys[hp], preferred_element_type=F32)
                    dk = _dot_tn(ds_cols, cur_ref[:, cols(u0 + Q0, c0)])
                    dv = _dot_tn(p_cols, do_ref[:, cols(o0, c0)])
                    dq2 = jnp.where(low, dq[0:ATTN_BLK], dq[ATTN_BLK:nk2])
                    dk2 = jnp.where(low, dk[0:nk2], dk[nk2:2 * nk2])
                    dv2 = jnp.where(low, dv[0:nk2], dv[nk2:2 * nk2])
                    sl = cols(o0, c0)
                    out_ref[:, cols(u0 + Q0, c0)] = dq_s[:, sl].astype(out_ref.dtype)
                    dq_s[:, sl] = dq2
                    out_ref[:, cols(u0 + K0, c0)] = (dk_s[:, sl] + dk2[0:ATTN_BLK]).astype(out_ref.dtype)
                    dk_s[:, sl] = dk2[ATTN_BLK:2 * ATTN_BLK]
                    out_ref[:, cols(u0 + V0, c0)] = (dv_s[:, sl] + dv2[0:ATTN_BLK]).astype(out_ref.dtype)
                    dv_s[:, sl] = dv2[ATTN_BLK:2 * ATTN_BLK]

        @pl.when(t == steps)
        def _():
            for rr in range(nres):
                u0, o0 = rr * GROUP_QKV, rr * ATTN_OUT
                out_ref[:, u0 + Q0:u0 + Q0 + ATTN_OUT] = dq_s[:, o0:o0 + ATTN_OUT].astype(out_ref.dtype)
                out_ref[:, u0 + K0:u0 + K0 + ATTN_OUT] = dk_s[:, o0:o0 + ATTN_OUT].astype(out_ref.dtype)
                out_ref[:, u0 + V0:u0 + V0 + ATTN_OUT] = dv_s[:, o0:o0 + ATTN_OUT].astype(out_ref.dtype)

    def blocks(width, lag=0, back=0):
        def index_map(t):
            tt = jnp.maximum(jnp.minimum(t, steps - 1) - lag, 0)
            return (jnp.maximum(tt % nb - back, 0), tt // nb)
        return pl.BlockSpec((ATTN_BLK, nres * width), index_map)

    return pl.pallas_call(
        body, name=name,
        out_shape=(jax.ShapeDtypeStruct((rows, d * GROUP_QKV), BF16),
                   jax.ShapeDtypeStruct((HEADS_PER_GROUP, ATTN_BLK, 2 * ATTN_BLK), F32)),
        grid=(steps + 1,),
        in_specs=[blocks(GROUP_QKV), blocks(GROUP_QKV, back=1), blocks(ATTN_OUT), blocks(ATTN_OUT), blocks(ATTN_OUT),
                  pl.BlockSpec((None, 2, HEADS_PER_GROUP, ATTN_BLK, 2 * ATTN_BLK), lambda t: (g, 0, 0, 0, 0))],
        out_specs=(pl.BlockSpec((ATTN_BLK, nres * GROUP_QKV),
                                lambda t: (jnp.maximum(t - 1, 0) % nb, jnp.maximum(t - 1, 0) // nb)),
                   pl.BlockSpec((HEADS_PER_GROUP, ATTN_BLK, 2 * ATTN_BLK), lambda t: (0, 0, 0))),
        scratch_shapes=[pltpu.VMEM((ATTN_BLK, nres * ATTN_OUT), F32)] * 3,
        compiler_params=_cparams("arbitrary"),
    )(uv, uv, dov, lv, dv_, bias)


def _split_bf16(x):
    hi = x.astype(BF16)
    r1 = x - hi.astype(F32)
    mid = r1.astype(BF16)
    lo = (r1 - mid.astype(F32)).astype(BF16)
    return hi, mid, lo


RELBIAS_CHUNK = 4096


def _relbias_reduce(dbs, buckets, *, name):
    flat = ATTN_BLK * 2 * ATTN_BLK
    dbf = jnp.stack([db.reshape(HEADS_PER_GROUP, flat) for db in dbs])
    bkf = buckets.reshape(N_GROUPS, 1, flat)

    def body(db_ref, bk_ref, o_ref):
        c = pl.program_id(1)
        rows = lax.broadcasted_iota(jnp.int32, (LANES, RELBIAS_CHUNK), 0).astype(F32)
        onehot = jnp.where(rows == bk_ref[0], 1.0, 0.0).astype(BF16)
        hi, mid, lo = _split_bf16(db_ref[0])
        part = _dot_nt(hi, onehot) + _dot_nt(mid, onehot) + _dot_nt(lo, onehot)

        @pl.when(c == 0)
        def _():
            o_ref[0] = part

        @pl.when(c > 0)
        def _():
            o_ref[0] += part

    return pl.pallas_call(
        body, name=name,
        out_shape=jax.ShapeDtypeStruct((N_GROUPS, HEADS_PER_GROUP, LANES), F32),
        grid=(N_GROUPS, flat // RELBIAS_CHUNK),
        in_specs=[pl.BlockSpec((1, HEADS_PER_GROUP, RELBIAS_CHUNK), lambda g, c: (g, 0, c)),
                  pl.BlockSpec((1, 1, RELBIAS_CHUNK), lambda g, c: (g, 0, c))],
        out_specs=pl.BlockSpec((1, HEADS_PER_GROUP, LANES), lambda g, c: (g, 0, 0)),
        compiler_params=_cparams("parallel", "arbitrary"),
    )(dbf, bkf)


def _my_position():
    x, y, c = lax.axis_index("x"), lax.axis_index("y"), lax.axis_index("c")
    return x, y, c


def _linear(pos):
    return 4 * pos[0] + 2 * pos[1] + pos[2]


def _peer(pos, k):
    x, y, c = pos
    return ((1 - x) if k & 4 else x, (1 - y) if k & 2 else y, (1 - c) if k & 1 else c)


HBM_SPEC = pl.BlockSpec(memory_space=pltpu.HBM)
SEM_SPEC = pl.BlockSpec(memory_space=pltpu.SEMAPHORE)
DATAFLOW = pltpu.SideEffectType.DATAFLOW_SIDE_EFFECTING


def _exchange_copies(src, land, sems, send_window, recv_window, with_arrivals):
    send_sems, recv_sems, local_sems = sems
    T = len(src)
    me = _my_position()
    me_lin = _linear(me)
    local = [pltpu.make_async_copy(send_window(t, src[t], me_lin), recv_window(t, land[t], me_lin),
                                   local_sems.at[t]) for t in range(T)]
    sends, arrivals = [], []
    for t in range(T):
        for k in range(1, N_DEV):
            peer = _peer(me, k)
            peer_lin = _linear(peer)
            sem = t * (N_DEV - 1) + k - 1
            sends.append(pltpu.make_async_remote_copy(
                src_ref=send_window(t, src[t], peer_lin), dst_ref=recv_window(t, land[t], me_lin),
                send_sem=send_sems.at[sem], recv_sem=recv_sems.at[sem],
                device_id=peer, device_id_type=MESH))
            if with_arrivals:
                arrivals.append(pltpu.make_async_remote_copy(
                    src_ref=send_window(t, src[t], me_lin), dst_ref=recv_window(t, land[t], peer_lin),
                    send_sem=send_sems.at[sem], recv_sem=recv_sems.at[sem],
                    device_id=peer, device_id_type=MESH))
    return local, sends, arrivals


def _exchange_start(srcs, land_shapes, send_window, recv_window, *, name, dep=None):
    T = len(srcs)
    n_in = 2 * T + (1 if dep is not None else 0)

    def body(*refs):
        src = refs[:T]
        land = refs[T:2 * T]
        sems = refs[n_in:n_in + 3]
        token = refs[-1]
        local, sends, _ = _exchange_copies(src, land, sems, send_window, recv_window, False)
        for cp in local + sends:
            cp.start()
        token[...] = jnp.zeros_like(token)

    lands = [lax.empty(ls.shape, ls.dtype) for ls in land_shapes]
    operands = [pltpu.with_memory_space_constraint(a, pltpu.HBM) for a in list(srcs) + lands]
    outs = pl.pallas_call(
        body, name=name,
        out_shape=(pltpu.SemaphoreType.DMA((T * (N_DEV - 1),)), pltpu.SemaphoreType.DMA((T * (N_DEV - 1),)),
                   pltpu.SemaphoreType.DMA((T,)),
                   *[pltpu.HBM(a.shape, a.dtype) for a in operands],
                   jax.ShapeDtypeStruct((8, LANES), F32)),
        in_specs=[HBM_SPEC] * (2 * T) + ([ANY] if dep is not None else []),
        out_specs=(SEM_SPEC,) * 3 + (HBM_SPEC,) * (2 * T) + (VMEM_SPEC,),
        input_output_aliases={i: 3 + i for i in range(2 * T)},
        compiler_params=pltpu.CompilerParams(has_side_effects=DATAFLOW),
    )(*operands, *([dep] if dep is not None else []))
    return outs[:3], outs[3:3 + T], outs[3 + T:3 + 2 * T], outs[-1]


def _exchange_wait(started, after, send_window, recv_window, *, name):
    sems, srcs, lands, _ = started
    T = len(srcs)

    def body(*refs):
        src = refs[:T]
        land = refs[T:2 * T]
        sem_refs = refs[2 * T:2 * T + 3]
        local, sends, arrivals = _exchange_copies(src, land, sem_refs, send_window, recv_window, True)
        for cp in arrivals:
            cp.wait_recv()
        for cp in sends:
            cp.wait_send()
        for cp in local:
            cp.wait()

    outs = pl.pallas_call(
        body, name=name,
        out_shape=tuple(pltpu.HBM(a.shape, a.dtype) for a in list(srcs) + list(lands)),
        in_specs=[HBM_SPEC] * (2 * T) + [SEM_SPEC] * 3 + [ANY],
        out_specs=(HBM_SPEC,) * (2 * T),
        input_output_aliases={i: i for i in range(2 * T)},
        compiler_params=pltpu.CompilerParams(has_side_effects=DATAFLOW),
    )(*srcs, *lands, *sems, after)
    return outs[T:]


def _shard_window(kind, width):
    def win(ref, lin):
        if kind == "slot":
            return ref.at[lin]
        if kind == "col":
            return ref.at[:, pl.ds(pl.multiple_of(lin * width, LANES), width)]
        if kind == "row":
            return ref.at[pl.ds(pl.multiple_of(lin * width, 8), width), :]
        if kind == "lcol":
            return ref.at[:, :, pl.ds(pl.multiple_of(lin * width, LANES), width)]
        if kind == "lrow":
            return ref.at[:, pl.ds(pl.multiple_of(lin * width, 8), width), :]
        raise ValueError(kind)
    return win


def _shard_windows(kinds, shard_shapes):
    return [_shard_window(k, (ss[-1] if k in ("col", "lcol") else ss[-2])) for k, ss in zip(kinds, shard_shapes)]


def _allgather_start(shards, kinds, full_shapes, *, name, dep=None):
    wins = _shard_windows(kinds, [s.shape for s in shards])
    send_window = lambda t, ref, lin: ref
    recv_window = lambda t, ref, lin: wins[t](ref, lin)
    started = _exchange_start(shards, [jax.ShapeDtypeStruct(fs, s.dtype) for fs, s in zip(full_shapes, shards)],
                              send_window, recv_window, name=name + "_start", dep=dep)
    return started, lambda after: _exchange_wait(started, after, send_window, recv_window, name=name + "_wait")


def _scatter_start(fulls, kinds, shard_shapes, *, name):
    wins = _shard_windows(kinds, shard_shapes)
    send_window = lambda t, ref, lin: wins[t](ref, lin)
    recv_window = lambda t, ref, lin: ref.at[lin]
    started = _exchange_start(
        fulls, [jax.ShapeDtypeStruct((N_DEV,) + tuple(ss), f.dtype) for ss, f in zip(shard_shapes, fulls)],
        send_window, recv_window, name=name + "_start")
    return started, lambda after: _exchange_wait(started, after, send_window, recv_window, name=name + "_wait")


def _small_gather(pack, *, reduce, name):
    R = pack.shape[0]

    def body(p_ref, o_ref, *rest):
        if reduce:
            buf, send_sems, recv_sems = rest
        else:
            buf = o_ref
            send_sems, recv_sems = rest
        me = _my_position()
        me_lin = _linear(me)
        buf[me_lin] = p_ref[...]
        sends = []
        for k in range(1, N_DEV):
            peer = _peer(me, k)
            cp = pltpu.make_async_remote_copy(
                src_ref=p_ref, dst_ref=buf.at[me_lin],
                send_sem=send_sems.at[k - 1], recv_sem=recv_sems.at[k - 1],
                device_id=peer, device_id_type=MESH)
            cp.start()
            sends.append(cp)
        for k in range(1, N_DEV):
            peer = _peer(me, k)
            pltpu.make_async_remote_copy(
                src_ref=p_ref, dst_ref=buf.at[_linear(peer)],
                send_sem=send_sems.at[k - 1], recv_sem=recv_sems.at[k - 1],
                device_id=peer, device_id_type=MESH).wait_recv()
        for cp in sends:
            cp.wait_send()
        if reduce:
            acc = buf[0]
            for s in range(1, N_DEV):
                acc = acc + buf[s]
            o_ref[...] = acc

    scratch = [pltpu.SemaphoreType.DMA((N_DEV - 1,)), pltpu.SemaphoreType.DMA((N_DEV - 1,))]
    if reduce:
        scratch = [pltpu.VMEM((N_DEV, R, LANES), F32)] + scratch
        out_shape = jax.ShapeDtypeStruct((R, LANES), F32)
    else:
        out_shape = jax.ShapeDtypeStruct((N_DEV, R, LANES), F32)
    return pl.pallas_call(
        body, name=name, out_shape=out_shape,
        in_specs=[VMEM_SPEC], out_specs=VMEM_SPEC, scratch_shapes=scratch,
        compiler_params=pltpu.CompilerParams(has_side_effects=True, vmem_limit_bytes=VMEM_LIMIT),
    )(pack)


def _adamw_math(w, g, m, v):
    m = ADAM_B1 * m + (1.0 - ADAM_B1) * g
    v = ADAM_B2 * v + (1.0 - ADAM_B2) * jnp.square(g)
    m_hat = m / (1.0 - ADAM_B1 ** ADAM_STEP)
    v_hat = v / (1.0 - ADAM_B2 ** ADAM_STEP)
    delta = -ADAM_LR * (m_hat / (jnp.sqrt(v_hat) + ADAM_EPS) + ADAM_WD * w)
    return delta, m, v


def _adamw_from_partials(parts, w, m, v, *, name):
    R, C = w.shape
    rl = parts[0].shape[1]
    assert all(p.shape == (N_DEV, rl, C) for p in parts) and rl * len(parts) == R
    tr = _tile(rl, (256, 128, 64, 32, 16))
    per = rl // tr
    L = len(parts)

    def body(*refs):
        p_refs = refs[:L]
        w_ref, m_ref, v_ref, g_ref, d_ref, nm_ref, nv_ref = refs[L:]
        i = pl.program_id(0)
        for l in range(L):
            @pl.when((i >= l * per) & (i < (l + 1) * per))
            def _(l=l):
                p_ref = p_refs[l]
                g = p_ref[0].astype(F32)
                for s in range(1, N_DEV):
                    g = g + p_ref[s].astype(F32)
                d, nm, nv = _adamw_math(w_ref[...], g, m_ref[...], v_ref[...])
                g_ref[...] = g
                d_ref[...] = d
                nm_ref[...] = nm
                nv_ref[...] = nv

    blk = pl.BlockSpec((tr, C), lambda i: (i, 0))
    part_specs = [pl.BlockSpec((N_DEV, tr, C), lambda i, l=l: (0, jnp.clip(i - l * per, 0, per - 1), 0))
                  for l in range(L)]
    return pl.pallas_call(
        body, name=name,
        out_shape=(jax.ShapeDtypeStruct((R, C), F32),) * 4,
        grid=(R // tr,),
        in_specs=part_specs + [blk, blk, blk],
        out_specs=(blk,) * 4,
        compiler_params=_cparams("parallel"),
    )(*parts, w, m, v)


def _adamw_small(g, w, m, v, *, name):
    def body(g_ref, w_ref, m_ref, v_ref, d_ref, nm_ref, nv_ref):
        d, nm, nv = _adamw_math(w_ref[...], g_ref[...], m_ref[...], v_ref[...])
        d_ref[...] = d
        nm_ref[...] = nm
        nv_ref[...] = nv

    return pl.pallas_call(
        body, name=name,
        out_shape=(jax.ShapeDtypeStruct(g.shape, F32),) * 3,
        in_specs=[VMEM_SPEC] * 4, out_specs=(VMEM_SPEC,) * 3,
    )(g, w, m, v)


def _pack_rows(pieces):
    flat = jnp.concatenate([p.reshape(-1) for p in pieces])
    n = flat.shape[0]
    padded = -(-n // (8 * LANES)) * (8 * LANES)
    return jnp.pad(flat, (0, padded - n)).reshape(padded // LANES, LANES)


def _unpack_rows(pack, shapes):
    flat = pack.reshape(-1)
    out, pos = [], 0
    for s in shapes:
        n = int(np.prod(s))
        out.append(flat[pos:pos + n].reshape(s))
        pos += n
    return out


def kernel(x, rel_bias, ab_norm, ab_w_in, ab_conv_w, ab_conv_b, ab_ln_g, ab_ln_b, ab_w_out, sc_norm, sc_w_in, sc_conv_w, sc_w_out, mlp_norm, mlp_w_up, mlp_w_down, final_norm, loss_target, m_rel_bias, m_ab_norm, m_ab_w_in, m_ab_conv_w, m_ab_conv_b, m_ab_ln_g, m_ab_ln_b, m_ab_w_out, m_sc_norm, m_sc_w_in, m_sc_conv_w, m_sc_w_out, m_mlp_norm, m_mlp_w_up, m_mlp_w_down, m_final_norm, v_rel_bias, v_ab_norm, v_ab_w_in, v_ab_conv_w, v_ab_conv_b, v_ab_ln_g, v_ab_ln_b, v_ab_w_out, v_sc_norm, v_sc_w_in, v_sc_conv_w, v_sc_w_out, v_mlp_norm, v_mlp_w_up, v_mlp_w_down, v_final_norm):
    S, D = x.shape[1], x.shape[2]
    CA = ab_conv_b.shape[1]
    C2 = 2 * CA
    AB_IN = C2 + ATTN_IN
    me_lin = _linear(_my_position())
    xs = x.reshape(S, D)
    tgt = loss_target.reshape(S, D)

    cw_sh = ab_conv_w.shape[2]
    scn_sh = sc_norm.shape[1]
    scw_sh = sc_conv_w.shape[2]
    small_sh_shapes = [(CONV_A_WIDTH, cw_sh), (scn_sh,), (SC_CONV_WIDTH, scw_sh)]
    small_pack = _pack_rows([ab_conv_w[0], sc_norm[0], sc_conv_w[0]])
    w_in_sh = ab_w_in[0].astype(BF16)
    ag_ab, wait_ab = _allgather_start(
        [small_pack, w_in_sh, ab_w_out[0].astype(BF16)], ["slot", "slot", "row"],
        [(N_DEV,) + small_pack.shape, (N_DEV,) + w_in_sh.shape, (N_DEV * ab_w_out.shape[1], D)],
        name="allgather_ab")
    ag_mlp, wait_mlp = _allgather_start(
        [mlp_w_up.astype(BF16), mlp_w_down.astype(BF16)], ["lcol", "lrow"],
        [(2, D, N_DEV * mlp_w_up.shape[2]), (2, N_DEV * mlp_w_down.shape[1], D)], name="allgather_mlp",
        dep=ag_ab[3])
    ag_sc, wait_sc = _allgather_start(
        [sc_w_in[0].astype(BF16), sc_w_out[0].astype(BF16)], ["col", "row"],
        [(D, N_DEV * sc_w_in.shape[2]), (N_DEV * sc_w_out.shape[1], D)], name="allgather_sc",
        dep=ag_mlp[3])

    buckets = _bucket_tables()
    biases = _bias_tables(rel_bias, buckets, name="bias_tables")

    dils = [dil for _, dil in DILATED_GROUPS]
    n0_all = _rmsnorm_fwd(xs, ab_norm, name="norm_ab", dep=ag_sc[3], views=dils[1:])
    n0 = n0_all[0]
    small_params, w_in_g, w_out = wait_ab(n0)
    per_dev = [_unpack_rows(small_params[s], small_sh_shapes) for s in range(N_DEV)]
    conv_w_full = jnp.concatenate([p[0] for p in per_dev], axis=1)
    sc_norm_full = jnp.concatenate([p[1] for p in per_dev], axis=0)[None]
    sc_conv_full = jnp.concatenate([p[2] for p in per_dev], axis=1)
    w_in = jnp.transpose(w_in_g, (1, 0, 2)).reshape(D, AB_IN)
    w_c = w_in[:, :C2]
    w_q = w_in[:, C2:]
    w_grp = [jnp.concatenate([w_q[:, t * N_GROUPS * ATTN_OUT + g * ATTN_OUT:][:, :ATTN_OUT] for t in range(3)], axis=1)
             for g in range(N_GROUPS)]
    uc = _mm_nn(n0, w_c, out_dtype=F32, name="mm_ab_in_conv", wide=True)
    uqs = [_mm_nn(n0_all[g], w_grp[g], out_dtype=BF16, slabs=dils[g], name=f"mm_ab_in_qkv{g}")
           for g in range(N_GROUPS)]
    ya, hglu, ct = _conv_fwd(uc, conv_w_full, ab_conv_b, ab_ln_g, ab_ln_b, name="conv_fwd")
    outs, lses = zip(*[_attn_fwd(uqs[g], biases, g, dils[g], name=f"attn_fwd_{g}") for g in range(N_GROUPS)])
    cat, outf, lse = _attn_merge(outs, lses, ya, name="attn_merge")
    h1 = _mm_nn(cat, w_out, out_dtype=F32, residual=xs, name="mm_ab_out", wide=True)
    w_up, w_dn = wait_mlp(h1)
    n1, z0 = _norm_mm_nn(h1, mlp_norm[0:1], (w_up, 0), out_dtype=BF16, name="norm_mm_up0", wide=True)
    h2 = _mm_nn(z0, (w_dn, 0), out_dtype=F32, residual=h1, a_fn=_relu_sq, name="mm_down0", wide=True)
    w_sc_in, w_sc_out = wait_sc(h2)
    n2, u3 = _norm_mm_nn(h2, sc_norm_full, w_sc_in, out_dtype=BF16, name="norm_mm_sc_in", wide=True)
    ysc = _sc_fwd(u3, sc_conv_full, name="sc_fwd")
    h3 = _mm_nn(ysc, w_sc_out, out_dtype=F32, residual=h2, name="mm_sc_out", wide=True)
    n3, z1 = _norm_mm_nn(h3, mlp_norm[1:2], (w_up, 1), out_dtype=BF16, name="norm_mm_up1", wide=True)
    h4 = _mm_nn(z1, (w_dn, 1), out_dtype=F32, residual=h3, a_fn=_relu_sq, name="mm_down1", wide=True)

    def dz_epilogue(acc, z):
        return acc * (2.0 * jnp.maximum(z.astype(F32), 0.0))

    dh4, dh4b, acc_final = _loss_bwd(h4, tgt, final_norm[None], name="loss_bwd")
    dz1 = _mm_nt([(dh4b, (w_dn, 1))], out_dtype=BF16, epilogue=dz_epilogue, extra=z1, name="mm_d_down1")
    g_dn1 = _mm_tn(z1, dh4b, a_fn=_relu_sq, name="mm_gw_down1")
    g_up1 = _mm_tn(n3, dz1, name="mm_gw_up1")
    rs_mlp1, wait_rs_mlp1 = _scatter_start([g_up1, g_dn1], ["col", "row"],
                                           [mlp_w_up.shape[1:], mlp_w_down.shape[1:]], name="scatter_mlp1")
    dh3, dh3b, acc_mlp1 = _mm_nt_rms_bwd(dz1, (w_up, 1), h3, mlp_norm[1:2], dh4, name="mm_d_up1_norm_bwd",
                                         dep=rs_mlp1[3])

    dysc = _mm_nt([(dh3b, w_sc_out)], out_dtype=F32, name="mm_d_sc_out")
    g_sc_out = _mm_tn(ysc, dh3b, name="mm_gw_sc_out")
    du3, acc_scw = _sc_bwd(u3, dysc, sc_conv_full, name="sc_bwd")
    g_sc_in = _mm_tn(n2, du3, name="mm_gw_sc_in")
    rs_sc, wait_rs_sc = _scatter_start([g_sc_in, g_sc_out], ["col", "row"],
                                       [sc_w_in.shape[1:], sc_w_out.shape[1:]], name="scatter_sc")
    dh2, dh2b, acc_sc = _mm_nt_rms_bwd(du3, w_sc_in, h2, sc_norm_full, dh3, name="mm_d_sc_in_norm_bwd",
                                       dep=rs_sc[3])

    dz0 = _mm_nt([(dh2b, (w_dn, 0))], out_dtype=BF16, epilogue=dz_epilogue, extra=z0, name="mm_d_down0")
    g_dn0 = _mm_tn(z0, dh2b, a_fn=_relu_sq, name="mm_gw_down0")
    g_up0 = _mm_tn(n1, dz0, name="mm_gw_up0")
    rs_mlp0, wait_rs_mlp0 = _scatter_start([g_up0, g_dn0], ["col", "row"],
                                           [mlp_w_up.shape[1:], mlp_w_down.shape[1:]], name="scatter_mlp0")
    dh1, dh1b, acc_mlp0 = _mm_nt_rms_bwd(dz0, (w_up, 0), h1, mlp_norm[0:1], dh2, name="mm_d_up0_norm_bwd",
                                         dep=rs_mlp0[3])

    dcat = _mm_nt([(dh1b, w_out)], out_dtype=F32, name="mm_d_ab_out")
    g_ab_out = _mm_tn(cat, dh1b, name="mm_gw_ab_out")
    prep = _attn_prep(dcat, outf, lse, name="attn_prep")
    dqkv, dbs = zip(*[_attn_bwd(uqs[g], prep[g][0], prep[g][2], prep[g][1], biases, g, dils[g],
                                name=f"attn_bwd_{g}") for g in range(N_GROUPS)])
    drel = _relbias_reduce(dbs, buckets, name="relbias_reduce")
    dc, acc_conv = _conv_bwd_ln(ct, dcat, hglu, ab_ln_g, ab_ln_b, name="conv_bwd_ln")
    duc = _conv_bwd_in(dc, uc, conv_w_full, name="conv_bwd_in")
    g_wc = _mm_tn(n0, duc, name="mm_gw_ab_in_conv")
    g_wgrp = [_mm_tn(n0_all[g], dqkv[g], slabs=dils[g], name=f"mm_gw_ab_in_qkv{g}") for g in range(N_GROUPS)]
    g_wq = jnp.concatenate([g_wgrp[g][:, t * ATTN_OUT:(t + 1) * ATTN_OUT]
                            for t in range(3) for g in range(N_GROUPS)], axis=1)
    g_w_in = jnp.concatenate([g_wc, g_wq], axis=1).reshape(D, N_DEV, AB_IN // N_DEV).transpose(1, 0, 2)
    rs_ab, wait_rs_ab = _scatter_start([g_w_in, g_ab_out], ["slot", "row"],
                                       [w_in_sh.shape, ab_w_out.shape[1:]], name="scatter_ab")
    dn0 = _mm_nt([(duc, w_c), (dqkv[0], w_grp[0])], out_dtype=F32, name="mm_d_ab_in", dep=rs_ab[3])
    dn0_views = [(_mm_nt([(dqkv[g], w_grp[g])], out_dtype=F32, slabs=dils[g], name=f"mm_d_ab_in_qkv{g}"), dils[g])
                 for g in range(1, N_GROUPS)]
    grad_x, grad_xb, acc_ab = _rms_bwd(xs, ab_norm, dn0, dh1, name="norm_ab_bwd", dn_views=dn0_views)

    small_full = [drel[:, :, :NUM_BUCKETS].transpose(2, 0, 1).reshape(NUM_BUCKETS, N_GROUPS * HEADS_PER_GROUP),
                  acc_ab[0], acc_conv[0:CONV_A_WIDTH], acc_conv[32],
                  acc_conv[33], acc_conv[34], acc_sc[0], acc_scw[0:SC_CONV_WIDTH],
                  jnp.stack([acc_mlp0[0], acc_mlp1[0]]), acc_final[0], acc_final[1]]
    small_full_shapes = [p.shape for p in small_full]
    summed = _unpack_rows(_small_gather(_pack_rows(small_full), reduce=True, name="allreduce_small"),
                          small_full_shapes)
    (s_rel, s_abn, s_cw, s_cb, s_lg, s_lb, s_scn, s_scw, s_mlpn, s_fn, s_err) = summed
    loss = (0.5 / D) * jnp.sum(s_err)
    small_grads = {
        "rel_bias": s_rel, "ab_norm": s_abn[None],
        "ab_conv_w": lax.dynamic_slice_in_dim(s_cw, me_lin * cw_sh, cw_sh, axis=1)[None],
        "ab_conv_b": s_cb[None], "ab_ln_g": s_lg[None], "ab_ln_b": s_lb[None],
        "sc_norm": lax.dynamic_slice_in_dim(s_scn, me_lin * scn_sh, scn_sh, axis=0)[None],
        "sc_conv_w": lax.dynamic_slice_in_dim(s_scw, me_lin * scw_sh, scw_sh, axis=1)[None],
        "mlp_norm": s_mlpn, "final_norm": s_fn,
    }
    small_w = {"rel_bias": (rel_bias, m_rel_bias, v_rel_bias), "ab_norm": (ab_norm, m_ab_norm, v_ab_norm),
               "ab_conv_w": (ab_conv_w, m_ab_conv_w, v_ab_conv_w), "ab_conv_b": (ab_conv_b, m_ab_conv_b, v_ab_conv_b),
               "ab_ln_g": (ab_ln_g, m_ab_ln_g, v_ab_ln_g), "ab_ln_b": (ab_ln_b, m_ab_ln_b, v_ab_ln_b),
               "sc_norm": (sc_norm, m_sc_norm, v_sc_norm), "sc_conv_w": (sc_conv_w, m_sc_conv_w, v_sc_conv_w),
               "mlp_norm": (mlp_norm, m_mlp_norm, v_mlp_norm), "final_norm": (final_norm, m_final_norm, v_final_norm)}
    small_names = list(small_grads)
    small_shapes = [small_grads[n].shape for n in small_names]
    d_pack, m_pack, v_pack = _adamw_small(
        _pack_rows([small_grads[n] for n in small_names]), _pack_rows([small_w[n][0] for n in small_names]),
        _pack_rows([small_w[n][1] for n in small_names]), _pack_rows([small_w[n][2] for n in small_names]),
        name="adamw_small")
    small = {n: (small_grads[n], d, nm_, nv_) for n, d, nm_, nv_ in zip(
        small_names, _unpack_rows(d_pack, small_shapes), _unpack_rows(m_pack, small_shapes),
        _unpack_rows(v_pack, small_shapes))}

    p_up1, p_dn1 = wait_rs_mlp1(grad_xb)
    p_sc_in, p_sc_out = wait_rs_sc(grad_xb)
    p_up0, p_dn0 = wait_rs_mlp0(grad_xb)
    p_w_in, p_ab_out = wait_rs_ab(grad_xb)
    big = {}
    for nm, parts, w, m, v in (("ab_w_in", [p_w_in], ab_w_in, m_ab_w_in, v_ab_w_in),
                               ("ab_w_out", [p_ab_out], ab_w_out, m_ab_w_out, v_ab_w_out),
                               ("sc_w_in", [p_sc_in], sc_w_in, m_sc_w_in, v_sc_w_in),
                               ("sc_w_out", [p_sc_out], sc_w_out, m_sc_w_out, v_sc_w_out),
                               ("mlp_w_up", [p_up0, p_up1], mlp_w_up, m_mlp_w_up, v_mlp_w_up),
                               ("mlp_w_down", [p_dn0, p_dn1], mlp_w_down, m_mlp_w_down, v_mlp_w_down)):
        C = w.shape[-1]
        res = _adamw_from_partials(parts, w.reshape(-1, C), m.reshape(-1, C), v.reshape(-1, C), name="adamw_" + nm)
        big[nm] = tuple(r.reshape(w.shape) for r in res)

    order = ["rel_bias", "ab_norm", "ab_w_in", "ab_conv_w", "ab_conv_b", "ab_ln_g", "ab_ln_b", "ab_w_out",
             "sc_norm", "sc_w_in", "sc_conv_w", "sc_w_out", "mlp_norm", "mlp_w_up", "mlp_w_down", "final_norm"]
    allres = {**big, **small}
    return (loss, grad_x.reshape(x.shape),
            *[allres[n][0] for n in order], *[allres[n][1] for n in order],
            *[allres[n][2] for n in order], *[allres[n][3] for n in order])
```

```python
import functools
import math

import numpy as np
import jax
import jax.numpy as jnp
from jax import lax
from jax.experimental import pallas as pl
from jax.experimental.pallas import tpu as pltpu

F32 = jnp.float32
BF16 = jnp.bfloat16

HEAD_DIM = 64
HEADS_PER_GROUP = 8
DILATED_GROUPS = ((128, 1), (512, 4), (2048, 16))
N_GROUPS = 3
ATTN_OUT = HEADS_PER_GROUP * HEAD_DIM
ATTN_IN = 3 * N_GROUPS * ATTN_OUT
GROUP_QKV = 3 * ATTN_OUT
ATTN_BLK = 128
CONV_A_WIDTH = 31
SC_CONV_WIDTH = 3
NUM_BUCKETS = 32
REL_MAX_DISTANCE = 2048
RMS_EPS = 1e-6
LN_EPS = 1e-5
NEG_INF = -1e30
ADAM_LR = 0.001
ADAM_B1 = 0.9
ADAM_B2 = 0.999
ADAM_EPS = 1e-08
ADAM_WD = 0.01
ADAM_STEP = 10

N_DEV = 8
HALO = 32
LANES = 128
VMEM_LIMIT = 56 * 1024 * 1024
MESH = pl.DeviceIdType.MESH
ANY = pl.BlockSpec(memory_space=pl.ANY)
VMEM_SPEC = pl.BlockSpec(memory_space=pltpu.VMEM)


def _tile(n, prefs):
    for t in prefs:
        if n % t == 0:
            return t
    return n


def _cparams(*sem):
    return pltpu.CompilerParams(dimension_semantics=sem, vmem_limit_bytes=VMEM_LIMIT)


def _relu_sq(z):
    return jnp.square(jnp.maximum(z, 0))


def _dot_nt(a, b):
    return lax.dot_general(a, b, (((1,), (1,)), ((), ())), preferred_element_type=F32)


def _dot_tn(a, b):
    return lax.dot_general(a, b, (((0,), (0,)), ((), ())), preferred_element_type=F32)


def _weight(b):
    if not isinstance(b, tuple):
        return b, b.shape, pl.BlockSpec
    arr, layer = b

    def spec(block, index_map):
        return pl.BlockSpec((None,) + tuple(block), lambda *g: (layer,) + tuple(index_map(*g)))

    return arr, arr.shape[1:], spec


def _mm_nn(a, b, *, out_dtype, name, residual=None, a_fn=None, slabs=1, wide=False):
    M, K = a.shape
    K //= slabs
    b, (_, N), b_spec = _weight(b)
    tm = _tile(M, (1024, 512, 256) if wide else (2048, 1024, 512, 256))
    tn = _tile(N, (1024, 512, 384, 256, 128) if wide else (512, 384, 256, 128))
    tk = _tile(K, (2048, 1024, 512, 256, 128) if wide else (1024, 512, 256, 128))
    nk = K // tk
    nj = N // tn
    has_res = residual is not None

    def body(*refs):
        if has_res:
            a_ref, b_ref, r_ref, o_ref = refs[:4]
        else:
            a_ref, b_ref, o_ref = refs[:3]
        av = a_ref[...]
        if a_fn is not None:
            av = a_fn(av)
        part = jnp.dot(av, b_ref[...], preferred_element_type=F32)

        def finish(acc):
            if has_res:
                acc = acc + r_ref[...]
            o_ref[...] = acc.astype(o_ref.dtype)

        if nk == 1:
            finish(part)
        else:
            acc_ref = refs[-1]
            k = pl.program_id(2)

            @pl.when(k == 0)
            def _():
                acc_ref[...] = part

            @pl.when((k > 0) & (k < nk - 1))
            def _():
                acc_ref[...] += part

            @pl.when(k == nk - 1)
            def _():
                finish(acc_ref[...] + part)

    in_specs = [pl.BlockSpec((tm, tk), lambda i, j, k: (i, (j // nj) * nk + k)),
                b_spec((tk, tn), lambda i, j, k: (k, j % nj))]
    args = [a, b]
    if has_res:
        in_specs.append(pl.BlockSpec((tm, tn), lambda i, j, k: (i, j)))
        args.append(residual)
    return pl.pallas_call(
        body, name=name,
        out_shape=jax.ShapeDtypeStruct((M, slabs * N), out_dtype),
        grid=(M // tm, slabs * nj, nk),
        in_specs=in_specs,
        out_specs=pl.BlockSpec((tm, tn), lambda i, j, k: (i, j)),
        scratch_shapes=[pltpu.VMEM((tm, tn), F32)] if nk > 1 else [],
        compiler_params=_cparams("parallel", "parallel", "arbitrary"),
    )(*args)


def _norm_mm_nn(h, g, b, *, out_dtype, name, wide=False):
    M, K = h.shape
    b, (_, N), b_spec = _weight(b)
    tm = _tile(M, (2048, 1024, 512, 256))
    tn = _tile(N, (1024, 512, 384, 256, 128) if wide else (512, 384, 256, 128))

    def body(h_ref, g_ref, b_ref, n_ref, o_ref):
        @pl.when(pl.program_id(1) == 0)
        def _():
            x = h_ref[...]
            r = lax.rsqrt(jnp.mean(x * x, axis=-1, keepdims=True) + RMS_EPS)
            n_ref[...] = (x * r * g_ref[...]).astype(BF16)

        o_ref[...] = jnp.dot(n_ref[...], b_ref[...], preferred_element_type=F32).astype(o_ref.dtype)

    return pl.pallas_call(
        body, name=name,
        out_shape=(jax.ShapeDtypeStruct((M, K), BF16), jax.ShapeDtypeStruct((M, N), out_dtype)),
        grid=(M // tm, N // tn),
        in_specs=[pl.BlockSpec((tm, K), lambda i, j: (i, 0)), pl.BlockSpec((1, K), lambda i, j: (0, 0)),
                  b_spec((K, tn), lambda i, j: (0, j))],
        out_specs=(pl.BlockSpec((tm, K), lambda i, j: (i, 0)), pl.BlockSpec((tm, tn), lambda i, j: (i, j))),
        compiler_params=_cparams("parallel", "arbitrary"),
    )(h, g, b)


def _mm_nt(pairs, *, out_dtype, name, epilogue=None, extra=None, dep=None, slabs=1):
    assert slabs == 1 or len(pairs) == 1
    M = pairs[0][0].shape[0]
    weights = [_weight(p[1]) for p in pairs]
    Ko = weights[0][1][0]
    tm = _tile(M, (2048, 1024, 512, 256))
    to = _tile(Ko, (1024, 512, 256, 128))
    njo = Ko // to
    tks = [_tile(p[0].shape[1] // slabs, (1024, 768, 512, 256, 128)) for p in pairs]
    steps = [p[0].shape[1] // slabs // tk for p, tk in zip(pairs, tks)]
    offs = [sum(steps[:i]) for i in range(len(pairs))]
    nk = sum(steps)
    npair = len(pairs)
    has_extra = extra is not None

    def body(*refs):
        ab = refs[:2 * npair]
        pos = 2 * npair
        e_ref = None
        if has_extra:
            e_ref = refs[pos]
            pos += 1
        if dep is not None:
            pos += 1
        o_ref = refs[pos]
        acc_ref = refs[pos + 1]
        k = pl.program_id(2)

        @pl.when(k == 0)
        def _():
            acc_ref[...] = jnp.zeros_like(acc_ref)

        for p in range(npair):
            @pl.when((k >= offs[p]) & (k < offs[p] + steps[p]))
            def _(p=p):
                acc_ref[...] += _dot_nt(ab[2 * p][...], ab[2 * p + 1][...])

        @pl.when(k == nk - 1)
        def _():
            acc = acc_ref[...]
            if epilogue is not None:
                acc = epilogue(acc, e_ref[...] if has_extra else None)
            o_ref[...] = acc.astype(o_ref.dtype)

    in_specs, args = [], []
    for p, (a, b) in enumerate(pairs):
        def kidx(k, p=p):
            return jnp.clip(k - offs[p], 0, steps[p] - 1)
        in_specs.append(pl.BlockSpec((tm, tks[p]),
                                     lambda i, j, k, kidx=kidx, p=p: (i, (j // njo) * steps[p] + kidx(k))))
        in_specs.append(weights[p][2]((to, tks[p]), lambda i, j, k, kidx=kidx: (j % njo, kidx(k))))
        args += [a, weights[p][0]]
    if has_extra:
        in_specs.append(pl.BlockSpec((tm, to), lambda i, j, k: (i, j)))
        args.append(extra)
    if dep is not None:
        in_specs.append(ANY)
        args.append(dep)
    return pl.pallas_call(
        body, name=name,
        out_shape=jax.ShapeDtypeStruct((M, slabs * Ko), out_dtype),
        grid=(M // tm, slabs * njo, nk),
        in_specs=in_specs,
        out_specs=pl.BlockSpec((tm, to), lambda i, j, k: (i, j)),
        scratch_shapes=[pltpu.VMEM((tm, to), F32)],
        compiler_params=_cparams("parallel", "parallel", "arbitrary"),
    )(*args)


def _mm_tn(a, b, *, name, a_fn=None, slabs=1):
    M, K = a.shape
    K //= slabs
    N = b.shape[1] // slabs
    tm = _tile(M, (4096, 2048, 1024, 512, 256))
    tk = _tile(K, (1024, 768, 512, 384, 256, 128))
    tn = _tile(N, (1024, 768, 512, 384, 256, 128))
    nmi = M // tm
    nm = slabs * nmi
    nki, nnj = K // tk, N // tn

    def body(a_ref, b_ref, o_ref, acc_ref):
        m = pl.program_id(2)
        av = a_ref[...]
        if a_fn is not None:
            av = a_fn(av)
        part = _dot_tn(av, b_ref[...])
        if nm == 1:
            o_ref[...] = part.astype(o_ref.dtype)
            return

        @pl.when(m == 0)
        def _():
            acc_ref[...] = part

        @pl.when((m > 0) & (m < nm - 1))
        def _():
            acc_ref[...] += part

        @pl.when(m == nm - 1)
        def _():
            o_ref[...] = (acc_ref[...] + part).astype(o_ref.dtype)

    return pl.pallas_call(
        body, name=name,
        out_shape=jax.ShapeDtypeStruct((K, N), BF16),
        grid=(K // tk, N // tn, nm),
        in_specs=[pl.BlockSpec((tm, tk), lambda i, j, m: (m % nmi, (m // nmi) * nki + i)),
                  pl.BlockSpec((tm, tn), lambda i, j, m: (m % nmi, (m // nmi) * nnj + j))],
        out_specs=pl.BlockSpec((tk, tn), lambda i, j, m: (i, j)),
        scratch_shapes=[pltpu.VMEM((tk, tn), F32)],
        compiler_params=_cparams("parallel", "parallel", "arbitrary"),
    )(a, b)


def _rmsnorm_fwd(h, g, *, name, dep=None, views=()):
    S, D = h.shape
    tm = _tile(S, (512, 256))
    nv = len(views)

    def body(h_ref, g_ref, *rest):
        n_out = 1 + nv
        outs = rest[len(rest) - n_out - (1 if nv else 0):len(rest) - (1 if nv else 0)]
        x = h_ref[...]
        r = lax.rsqrt(jnp.mean(x * x, axis=-1, keepdims=True) + RMS_EPS)
        y = x * r * g_ref[...]
        outs[0][...] = y.astype(BF16)
        if nv:
            scr = rest[-1]
            _to_chunks(scr, y)
            for v_ref, d in zip(outs[1:], views):
                _slabs_from_chunks(v_ref, scr, d, BF16)

    res = pl.pallas_call(
        body, name=name,
        out_shape=(jax.ShapeDtypeStruct((S, D), BF16),)
        + tuple(jax.ShapeDtypeStruct((S // d, d * D), BF16) for d in views),
        grid=(S // tm,),
        in_specs=[pl.BlockSpec((tm, D), lambda i: (i, 0)), pl.BlockSpec((1, D), lambda i: (0, 0))]
        + ([ANY] if dep is not None else []),
        out_specs=(pl.BlockSpec((tm, D), lambda i: (i, 0)),)
        + tuple(pl.BlockSpec((tm // d, d * D), lambda i: (i, 0)) for d in views),
        scratch_shapes=[_chunk_scratch(tm, D)] if nv else [],
        compiler_params=_cparams("parallel"),
    )(h, g, *([dep] if dep is not None else []))
    return res if nv else res[0]


def _rms_bwd_rows(x, g, dy):
    r = lax.rsqrt(jnp.mean(x * x, axis=-1, keepdims=True) + RMS_EPS)
    xh = x * r
    gy = dy * g
    dx = r * (gy - xh * jnp.mean(xh * gy, axis=-1, keepdims=True))
    return dx, dy * xh


def _rms_bwd(x, g, dn, dres, *, name, dn_views=()):
    S, D = x.shape
    tm = _tile(S, (512, 256))
    nv = len(dn_views)

    def body(x_ref, g_ref, dn_ref, dr_ref, *rest):
        v_refs = rest[:nv]
        dx_ref, dxb_ref, dg_ref = rest[nv:nv + 3]
        scr = rest[nv + 3:]
        i = pl.program_id(0)
        dn = dn_ref[...]
        for v_ref, s_ref, (_, d) in zip(v_refs, scr, dn_views):
            _chunks_from_slabs(s_ref, v_ref, d)
            dn = dn + _from_chunks(s_ref)
        dx, dgx = _rms_bwd_rows(x_ref[...], g_ref[...], dn)
        tot = dr_ref[...] + dx
        dx_ref[...] = tot
        dxb_ref[...] = tot.astype(BF16)

        @pl.when(i == 0)
        def _():
            dg_ref[...] = jnp.zeros_like(dg_ref)

        dg_ref[0:1, :] += jnp.sum(dgx, axis=0, keepdims=True)

    row = pl.BlockSpec((tm, D), lambda i: (i, 0))
    return pl.pallas_call(
        body, name=name,
        out_shape=(jax.ShapeDtypeStruct((S, D), F32), jax.ShapeDtypeStruct((S, D), BF16),
                   jax.ShapeDtypeStruct((8, D), F32)),
        grid=(S // tm,),
        in_specs=[row, pl.BlockSpec((1, D), lambda i: (0, 0)), row, row]
        + [pl.BlockSpec((tm // d, d * D), lambda i: (i, 0)) for _, d in dn_views],
        out_specs=(row, row, pl.BlockSpec((8, D), lambda i: (0, 0))),
        scratch_shapes=[_chunk_scratch(tm, D)] * nv,
        compiler_params=_cparams("arbitrary"),
    )(x, g, dn, dres, *[a for a, _ in dn_views])


def _mm_nt_rms_bwd(a, b, x, g, dres, *, name, dep=None):
    M, N = a.shape
    b, (D, _), b_spec = _weight(b)
    tm = _tile(M, (1024, 512, 256))
    tk = _tile(N, (1024, 512, 256, 128))
    nk = N // tk

    def body(a_ref, b_ref, x_ref, g_ref, dr_ref, *rest):
        dx_ref, dxb_ref, dg_ref, acc_ref = rest[-4:]
        i = pl.program_id(0)
        k = pl.program_id(1)
        part = _dot_nt(a_ref[...], b_ref[...])

        @pl.when((i == 0) & (k == 0))
        def _():
            dg_ref[...] = jnp.zeros_like(dg_ref)

        @pl.when(k == 0)
        def _():
            acc_ref[...] = part

        @pl.when((k > 0) & (k < nk - 1))
        def _():
            acc_ref[...] += part

        @pl.when(k == nk - 1)
        def _():
            dn = part if nk == 1 else acc_ref[...] + part
            dx, dgx = _rms_bwd_rows(x_ref[...], g_ref[...], dn)
            tot = dr_ref[...] + dx
            dx_ref[...] = tot
            dxb_ref[...] = tot.astype(BF16)
            dg_ref[0:1, :] += jnp.sum(dgx, axis=0, keepdims=True)

    row = pl.BlockSpec((tm, D), lambda i, k: (i, 0))
    in_specs = [pl.BlockSpec((tm, tk), lambda i, k: (i, k)), b_spec((D, tk), lambda i, k: (0, k)),
                row, pl.BlockSpec((1, D), lambda i, k: (0, 0)), row]
    args = [a, b, x, g, dres]
    if dep is not None:
        in_specs.append(ANY)
        args.append(dep)
    return pl.pallas_call(
        body, name=name,
        out_shape=(jax.ShapeDtypeStruct((M, D), F32), jax.ShapeDtypeStruct((M, D), BF16),
                   jax.ShapeDtypeStruct((8, D), F32)),
        grid=(M // tm, nk),
        in_specs=in_specs,
        out_specs=(row, row, pl.BlockSpec((8, D), lambda i, k: (0, 0))),
        scratch_shapes=[pltpu.VMEM((tm, D), F32)],
        compiler_params=_cparams("arbitrary", "arbitrary"),
    )(*args)


def _loss_bwd(h, target, g, *, name):
    S, D = h.shape
    tm = _tile(S, (512, 256))

    def body(h_ref, t_ref, g_ref, dx_ref, dxb_ref, acc_ref):
        i = pl.program_id(0)
        x = h_ref[...]
        gv = g_ref[...]
        r = lax.rsqrt(jnp.mean(x * x, axis=-1, keepdims=True) + RMS_EPS)
        err = x * r * gv - t_ref[...]
        dx, dgx = _rms_bwd_rows(x, gv, err * (1.0 / D))
        dx_ref[...] = dx
        dxb_ref[...] = dx.astype(BF16)

        @pl.when(i == 0)
        def _():
            acc_ref[...] = jnp.zeros_like(acc_ref)

        acc_ref[0:1, :] += jnp.sum(dgx, axis=0, keepdims=True)
        acc_ref[1:2, :] += jnp.sum(err * err, axis=0, keepdims=True)

    row = pl.BlockSpec((tm, D), lambda i: (i, 0))
    return pl.pallas_call(
        body, name=name,
        out_shape=(jax.ShapeDtypeStruct((S, D), F32), jax.ShapeDtypeStruct((S, D), BF16),
                   jax.ShapeDtypeStruct((8, D), F32)),
        grid=(S // tm,),
        in_specs=[row, row, pl.BlockSpec((1, D), lambda i: (0, 0))],
        out_specs=(row, row, pl.BlockSpec((8, D), lambda i: (0, 0))),
        compiler_params=_cparams("arbitrary"),
    )(h, target, g)


SUBLANES = 8
CONV_ROWS = 64


def _build_shifted(ext_ref, rot_ref, ts, shifts=tuple(range(1, SUBLANES))):
    rows = ts + HALO - SUBLANES
    for j in shifts:
        rot_ref[j, 0:rows, :] = ext_ref[j:j + rows, :]


def _shifted(ext_ref, rot_ref, off, r0, nrows, cs):
    q, j = divmod(off, SUBLANES)
    start = SUBLANES * q + r0
    if j == 0:
        return ext_ref[start:start + nrows, cs]
    return rot_ref[j, start:start + nrows, cs]


def _conv_fwd(uc, conv_w, conv_b, ln_g, ln_b, *, name):
    S, C2 = uc.shape
    C = C2 // 2
    ts = _tile(S, (512, 256))
    per = ts // HALO

    def body(cur_ref, halo_ref, w_ref, b_ref, g_ref, beta_ref, ya_ref, h_ref, ct_ref, ext_ref, rot_ref):
        i = pl.program_id(0)
        hh = halo_ref[:, 0:C] * jax.nn.sigmoid(halo_ref[:, C:C2])
        ext_ref[0:HALO, :] = jnp.where(i == 0, 0.0, hh)
        hc = cur_ref[:, 0:C] * jax.nn.sigmoid(cur_ref[:, C:C2])
        ext_ref[HALO:HALO + ts, :] = hc
        h_ref[...] = hc
        _build_shifted(ext_ref, rot_ref, ts)
        for c0 in range(0, C, LANES):
            cs = slice(c0, c0 + LANES)
            for r0 in range(0, ts, CONV_ROWS):
                acc = jnp.zeros((CONV_ROWS, LANES), F32)
                for k in range(CONV_A_WIDTH):
                    acc = acc + w_ref[k:k + 1, cs] * _shifted(ext_ref, rot_ref, k + 2, r0, CONV_ROWS, cs)
                ct_ref[r0:r0 + CONV_ROWS, cs] = acc + b_ref[:, cs]
        ct = ct_ref[...]
        mu = jnp.mean(ct, axis=-1, keepdims=True)
        xc = ct - mu
        var = jnp.mean(xc * xc, axis=-1, keepdims=True)
        l = xc * lax.rsqrt(var + LN_EPS) * g_ref[...] + beta_ref[...]
        ya_ref[...] = (l * jax.nn.sigmoid(l)).astype(ya_ref.dtype)

    vec = pl.BlockSpec((1, C), lambda i: (0, 0))
    row = pl.BlockSpec((ts, C), lambda i: (i, 0))
    return pl.pallas_call(
        body, name=name,
        out_shape=(jax.ShapeDtypeStruct((S, C), BF16), jax.ShapeDtypeStruct((S, C), F32),
                   jax.ShapeDtypeStruct((S, C), F32)),
        grid=(S // ts,),
        in_specs=[pl.BlockSpec((ts, C2), lambda i: (i, 0)),
                  pl.BlockSpec((HALO, C2), lambda i: (jnp.maximum(i * per - 1, 0), 0)),
                  pl.BlockSpec((CONV_A_WIDTH, C), lambda i: (0, 0)), vec, vec, vec],
        out_specs=(row, row, row),
        scratch_shapes=[pltpu.VMEM((HALO + ts, C), F32), pltpu.VMEM((8, HALO + ts, C), F32)],
        compiler_params=_cparams("parallel"),
    )(uc, uc, conv_w, conv_b, ln_g, ln_b)


CONV_ACC_ROWS = 40


def _conv_bwd_ln(ct, dcat, hglu, ln_g, ln_b, *, name):
    S, C = ct.shape
    CW = dcat.shape[1]
    ts = _tile(S, (512, 256))
    per = ts // HALO

    def body(ct_ref, dcat_ref, hc_ref, hh_ref, g_ref, beta_ref, dc_ref, acc_ref, ext_ref, rot_ref):
        i = pl.program_id(0)
        ct = ct_ref[...]
        gv = g_ref[...]
        mu = jnp.mean(ct, axis=-1, keepdims=True)
        xc = ct - mu
        rstd = lax.rsqrt(jnp.mean(xc * xc, axis=-1, keepdims=True) + LN_EPS)
        xh = xc * rstd
        l = xh * gv + beta_ref[...]
        sg = jax.nn.sigmoid(l)
        dl = dcat_ref[:, 0:C] * (sg * (1.0 + l * (1.0 - sg)))
        dxh = dl * gv
        dc = rstd * (dxh - jnp.mean(dxh, axis=-1, keepdims=True)
                     - xh * jnp.mean(dxh * xh, axis=-1, keepdims=True))
        dc_ref[...] = dc

        @pl.when(i == 0)
        def _():
            acc_ref[...] = jnp.zeros_like(acc_ref)

        acc_ref[32:33, :] += jnp.sum(dc, axis=0, keepdims=True)
        acc_ref[33:34, :] += jnp.sum(dl * xh, axis=0, keepdims=True)
        acc_ref[34:35, :] += jnp.sum(dl, axis=0, keepdims=True)
        ext_ref[0:HALO, :] = jnp.where(i == 0, 0.0, hh_ref[...])
        ext_ref[HALO:HALO + ts, :] = hc_ref[...]
        _build_shifted(ext_ref, rot_ref, ts)
        for c0 in range(0, C, LANES):
            cs = slice(c0, c0 + LANES)
            dcc = dc_ref[:, cs]
            for k in range(CONV_A_WIDTH):
                acc_ref[k:k + 1, cs] += jnp.sum(dcc * _shifted(ext_ref, rot_ref, k + 2, 0, ts, cs),
                                                axis=0, keepdims=True)

    vec = pl.BlockSpec((1, C), lambda i: (0, 0))
    row = pl.BlockSpec((ts, C), lambda i: (i, 0))
    return pl.pallas_call(
        body, name=name,
        out_shape=(jax.ShapeDtypeStruct((S, C), F32), jax.ShapeDtypeStruct((CONV_ACC_ROWS, C), F32)),
        grid=(S // ts,),
        in_specs=[row, pl.BlockSpec((ts, CW), lambda i: (i, 0)), row,
                  pl.BlockSpec((HALO, C), lambda i: (jnp.maximum(i * per - 1, 0), 0)), vec, vec],
        out_specs=(row, pl.BlockSpec((CONV_ACC_ROWS, C), lambda i: (0, 0))),
        scratch_shapes=[pltpu.VMEM((HALO + ts, C), F32), pltpu.VMEM((8, HALO + ts, C), F32)],
        compiler_params=_cparams("arbitrary"),
    )(ct, dcat, hglu, hglu, ln_g, ln_b)


def _conv_bwd_in(dc, uc, conv_w, *, name):
    S, C = dc.shape
    C2 = 2 * C
    ts = _tile(S, (512, 256))
    per = ts // HALO
    nt = S // ts

    def body(dc_ref, dn_ref, uc_ref, w_ref, du_ref, ext_ref, rot_ref):
        i = pl.program_id(0)
        ext_ref[0:ts, :] = dc_ref[...]
        ext_ref[ts:ts + HALO, :] = jnp.where(i == nt - 1, 0.0, dn_ref[...])
        _build_shifted(ext_ref, rot_ref, ts)
        for c0 in range(0, C, LANES):
            cs = slice(c0, c0 + LANES)
            gs = slice(C + c0, C + c0 + LANES)
            for r0 in range(0, ts, CONV_ROWS):
                rs = slice(r0, r0 + CONV_ROWS)
                acc = jnp.zeros((CONV_ROWS, LANES), F32)
                for k in range(CONV_A_WIDTH):
                    acc = acc + w_ref[k:k + 1, cs] * _shifted(ext_ref, rot_ref, 30 - k, r0, CONV_ROWS, cs)
                sg = jax.nn.sigmoid(uc_ref[rs, gs])
                du_ref[rs, cs] = (acc * sg).astype(du_ref.dtype)
                du_ref[rs, gs] = (acc * uc_ref[rs, cs] * sg * (1.0 - sg)).astype(du_ref.dtype)

    return pl.pallas_call(
        body, name=name,
        out_shape=jax.ShapeDtypeStruct((S, C2), BF16),
        grid=(nt,),
        in_specs=[pl.BlockSpec((ts, C), lambda i: (i, 0)),
                  pl.BlockSpec((HALO, C), lambda i: (jnp.minimum((i + 1) * per, S // HALO - 1), 0)),
                  pl.BlockSpec((ts, C2), lambda i: (i, 0)),
                  pl.BlockSpec((CONV_A_WIDTH, C), lambda i: (0, 0))],
        out_specs=pl.BlockSpec((ts, C2), lambda i: (i, 0)),
        scratch_shapes=[pltpu.VMEM((ts + HALO, C), F32), pltpu.VMEM((8, ts + HALO, C), F32)],
        compiler_params=_cparams("parallel"),
    )(dc, dc, uc, conv_w)


SC_SHIFTS_BACK = ((HALO - 2) % SUBLANES, (HALO - 1) % SUBLANES)
SC_SHIFTS_AHEAD = (1, 2)


def _sc_fwd(u3, conv_w, *, name):
    S, W3 = u3.shape
    W = W3 // 3
    ts = _tile(S, (256,))
    per = ts // HALO

    def body(cur_ref, halo_ref, w_ref, y_ref, ext_ref, rot_ref):
        i = pl.program_id(0)
        cvh = halo_ref[:, W:2 * W].astype(F32) * halo_ref[:, 2 * W:W3].astype(F32)
        ext_ref[0:HALO, :] = jnp.where(i == 0, 0.0, cvh)
        ext_ref[HALO:HALO + ts, :] = cur_ref[:, W:2 * W].astype(F32) * cur_ref[:, 2 * W:W3].astype(F32)
        _build_shifted(ext_ref, rot_ref, ts, SC_SHIFTS_BACK)
        for c0 in range(0, W, LANES):
            cs = slice(c0, c0 + LANES)
            for r0 in range(0, ts, CONV_ROWS):
                rs = slice(r0, r0 + CONV_ROWS)
                k = (w_ref[0:1, cs] * _shifted(ext_ref, rot_ref, HALO - 2, r0, CONV_ROWS, cs)
                     + w_ref[1:2, cs] * _shifted(ext_ref, rot_ref, HALO - 1, r0, CONV_ROWS, cs)
                     + w_ref[2:3, cs] * _shifted(ext_ref, rot_ref, HALO, r0, CONV_ROWS, cs))
                y_ref[rs, cs] = (cur_ref[rs, cs].astype(F32) * k).astype(y_ref.dtype)

    return pl.pallas_call(
        body, name=name,
        out_shape=jax.ShapeDtypeStruct((S, W), BF16),
        grid=(S // ts,),
        in_specs=[pl.BlockSpec((ts, W3), lambda i: (i, 0)),
                  pl.BlockSpec((HALO, W3), lambda i: (jnp.maximum(i * per - 1, 0), 0)),
                  pl.BlockSpec((SC_CONV_WIDTH, W), lambda i: (0, 0))],
        out_specs=pl.BlockSpec((ts, W), lambda i: (i, 0)),
        scratch_shapes=[pltpu.VMEM((HALO + ts, W), F32), pltpu.VMEM((8, HALO + ts, W), F32)],
        compiler_params=_cparams("parallel"),
    )(u3, u3, conv_w)


def _sc_bwd(u3, dy, conv_w, *, name):
    S, W3 = u3.shape
    W = W3 // 3
    ts = _tile(S, (256,))
    per = ts // HALO
    nt = S // ts

    def body(cur_ref, prev_ref, next_ref, dy_ref, dyn_ref, w_ref, du_ref, dw_ref, cv_ext, dk_ext, cv_rot, dk_rot):
        i = pl.program_id(0)
        cvh = prev_ref[:, W:2 * W].astype(F32) * prev_ref[:, 2 * W:W3].astype(F32)
        cv_ext[0:HALO, :] = jnp.where(i == 0, 0.0, cvh)
        cv_ext[HALO:HALO + ts, :] = cur_ref[:, W:2 * W].astype(F32) * cur_ref[:, 2 * W:W3].astype(F32)
        dk_ext[0:ts, :] = dy_ref[...] * cur_ref[:, 0:W].astype(F32)
        dk_ext[ts:ts + HALO, :] = jnp.where(i == nt - 1, 0.0, dyn_ref[...] * next_ref[:, 0:W].astype(F32))
        _build_shifted(cv_ext, cv_rot, ts, SC_SHIFTS_BACK)
        _build_shifted(dk_ext, dk_rot, ts, SC_SHIFTS_AHEAD)

        @pl.when(i == 0)
        def _():
            dw_ref[...] = jnp.zeros_like(dw_ref)

        for c0 in range(0, W, LANES):
            cs = slice(c0, c0 + LANES)
            w0, w1, w2 = w_ref[0:1, cs], w_ref[1:2, cs], w_ref[2:3, cs]
            sums = [jnp.zeros((1, LANES), F32)] * SC_CONV_WIDTH
            for r0 in range(0, ts, CONV_ROWS):
                rs = slice(r0, r0 + CONV_ROWS)
                cv2 = _shifted(cv_ext, cv_rot, HALO - 2, r0, CONV_ROWS, cs)
                cv1 = _shifted(cv_ext, cv_rot, HALO - 1, r0, CONV_ROWS, cs)
                cv0 = _shifted(cv_ext, cv_rot, HALO, r0, CONV_ROWS, cs)
                dk = dk_ext[rs, cs]
                dcv = (w2 * dk + w1 * _shifted(dk_ext, dk_rot, 1, r0, CONV_ROWS, cs)
                       + w0 * _shifted(dk_ext, dk_rot, 2, r0, CONV_ROWS, cs))
                du_ref[rs, cs] = (dy_ref[rs, cs] * (w0 * cv2 + w1 * cv1 + w2 * cv0)).astype(du_ref.dtype)
                du_ref[rs, W + c0:W + c0 + LANES] = (
                    dcv * cur_ref[rs, 2 * W + c0:2 * W + c0 + LANES].astype(F32)).astype(du_ref.dtype)
                du_ref[rs, 2 * W + c0:2 * W + c0 + LANES] = (
                    dcv * cur_ref[rs, W + c0:W + c0 + LANES].astype(F32)).astype(du_ref.dtype)
                for t, cvt in enumerate((cv2, cv1, cv0)):
                    sums[t] = sums[t] + jnp.sum(dk * cvt, axis=0, keepdims=True)
            for t in range(SC_CONV_WIDTH):
                dw_ref[t:t + 1, cs] += sums[t]

    nxt = lambda i: (jnp.minimum((i + 1) * per, S // HALO - 1), 0)
    return pl.pallas_call(
        body, name=name,
        out_shape=(jax.ShapeDtypeStruct((S, W3), BF16), jax.ShapeDtypeStruct((8, W), F32)),
        grid=(nt,),
        in_specs=[pl.BlockSpec((ts, W3), lambda i: (i, 0)),
                  pl.BlockSpec((HALO, W3), lambda i: (jnp.maximum(i * per - 1, 0), 0)),
                  pl.BlockSpec((HALO, W3), nxt),
                  pl.BlockSpec((ts, W), lambda i: (i, 0)),
                  pl.BlockSpec((HALO, W), nxt),
                  pl.BlockSpec((SC_CONV_WIDTH, W), lambda i: (0, 0))],
        out_specs=(pl.BlockSpec((ts, W3), lambda i: (i, 0)), pl.BlockSpec((8, W), lambda i: (0, 0))),
        scratch_shapes=[pltpu.VMEM((HALO + ts, W), F32), pltpu.VMEM((ts + HALO, W), F32),
                        pltpu.VMEM((8, HALO + ts, W), F32), pltpu.VMEM((8, ts + HALO, W), F32)],
        compiler_params=_cparams("arbitrary"),
    )(u3, u3, u3, dy, dy, conv_w)


def _t5_causal_bucket(n):
    max_exact = NUM_BUCKETS // 2
    nf = jnp.maximum(n, 1).astype(F32)
    large = max_exact + (jnp.log(nf / max_exact) / math.log(REL_MAX_DISTANCE / max_exact)
                         * (NUM_BUCKETS - max_exact)).astype(jnp.int32)
    return jnp.where(n < max_exact, n, jnp.minimum(large, NUM_BUCKETS - 1))


def _bucket_tables():
    steps = ATTN_BLK
    m = jnp.arange(steps)[:, None] + steps - jnp.arange(2 * steps)[None, :]
    return jnp.stack([_t5_causal_bucket(jnp.clip(m, 0, steps) * dil).astype(F32) for _, dil in DILATED_GROUPS])


def _bias_tables(rel_bias, buckets, *, name):
    steps = ATTN_BLK

    def body(tab_ref, bk_ref, o_ref):
        g = pl.program_id(0)
        bk = bk_ref[0]
        a_idx = lax.broadcasted_iota(jnp.int32, (steps, 2 * steps), 0)
        c_idx = lax.broadcasted_iota(jnp.int32, (steps, 2 * steps), 1)
        m = a_idx + steps - c_idx
        band = (m >= 0) & (m <= steps)
        band_first = band & (c_idx >= steps)
        for h in range(HEADS_PER_GROUP):
            bias = jnp.zeros((steps, 2 * steps), F32)
            for b in range(NUM_BUCKETS):
                bias = jnp.where(bk == float(b), tab_ref[b, g * HEADS_PER_GROUP + h], bias)
            o_ref[0, 0, h] = jnp.where(band_first, bias, NEG_INF)
            o_ref[0, 1, h] = jnp.where(band, bias, NEG_INF)

    return pl.pallas_call(
        body, name=name,
        out_shape=jax.ShapeDtypeStruct((N_GROUPS, 2, HEADS_PER_GROUP, steps, 2 * steps), F32),
        grid=(N_GROUPS,),
        in_specs=[pl.BlockSpec(memory_space=pltpu.SMEM),
                  pl.BlockSpec((1, steps, 2 * steps), lambda g: (g, 0, 0))],
        out_specs=pl.BlockSpec((1, 2, HEADS_PER_GROUP, steps, 2 * steps), lambda g: (g, 0, 0, 0, 0)),
        compiler_params=_cparams("parallel"),
    )(rel_bias, buckets)


def _lane_is_low():
    return lax.broadcasted_iota(jnp.int32, (1, LANES), 1) < HEAD_DIM


def _stack_heads(x2, low):
    zero = jnp.zeros_like(x2)
    return jnp.concatenate([jnp.where(low, x2, zero), jnp.where(low, zero, x2)], axis=0)


ATTN_FWD_BLOCKS = 8


def _attn_fwd(uv, bias, g, d, *, name):
    rows = uv.shape[0]
    nsub = min(ATTN_FWD_BLOCKS, rows // ATTN_BLK)
    step_rows = nsub * ATTN_BLK
    nqb = GROUP_QKV // ATTN_OUT

    def body(q_ref, kc_ref, kp_ref, vc_ref, vp_ref, bias_ref, o_ref, l_ref):
        n = pl.program_id(1)
        low = _lane_is_low()
        slabs = [slice(hp * LANES, (hp + 1) * LANES) for hp in range(HEADS_PER_GROUP // 2)]
        for sub in range(nsub):
            qr = slice(sub * ATTN_BLK, (sub + 1) * ATTN_BLK)
            sel = jnp.minimum(n, 1) if sub == 0 else 1

            def with_prev(cur_ref, prev_ref, sl, sub=sub, qr=qr):
                prev = prev_ref[:, sl] if sub == 0 else cur_ref[(sub - 1) * ATTN_BLK:sub * ATTN_BLK, sl]
                return jnp.concatenate([prev, cur_ref[qr, sl]], axis=0)

            scores = [_dot_nt(_stack_heads(q_ref[qr, sl] * (HEAD_DIM ** -0.5), low), with_prev(kc_ref, kp_ref, sl))
                      for sl in slabs]
            probs, dens_all, lses_all = [], [], []
            for hp, s in enumerate(scores):
                ps, dens, lses = [], [], []
                for hh in range(2):
                    logits = s[hh * ATTN_BLK:(hh + 1) * ATTN_BLK] + bias_ref[sel, 2 * hp + hh]
                    mx = jnp.max(logits, axis=-1, keepdims=True)
                    p = jnp.exp(logits - mx)
                    den = jnp.sum(p, axis=-1, keepdims=True)
                    ps.append(p.astype(BF16))
                    dens.append(den)
                    lses.append(jnp.broadcast_to(mx + jnp.log(den), (ATTN_BLK, LANES)))
                probs.append(jnp.concatenate(ps, axis=0))
                dens_all.append(dens)
                lses_all.append(lses)
            for hp, sl in enumerate(slabs):
                pv = jnp.dot(probs[hp], with_prev(vc_ref, vp_ref, sl), preferred_element_type=F32)
                dens, lses = dens_all[hp], lses_all[hp]
                o_ref[qr, sl] = jnp.where(low, pv[0:ATTN_BLK] / dens[0], pv[ATTN_BLK:2 * ATTN_BLK] / dens[1])
                l_ref[qr, sl] = jnp.where(low, lses[0], lses[1])

    def cur(t):
        return pl.BlockSpec((step_rows, ATTN_OUT), lambda r, n: (n, r * nqb + t))

    def prev(t):
        return pl.BlockSpec((ATTN_BLK, ATTN_OUT), lambda r, n: (jnp.maximum(n * nsub - 1, 0), r * nqb + t))

    out_spec = pl.BlockSpec((step_rows, ATTN_OUT), lambda r, n: (n, r))
    return pl.pallas_call(
        body, name=name,
        out_shape=(jax.ShapeDtypeStruct((rows, d * ATTN_OUT), F32),) * 2,
        grid=(d, rows // step_rows),
        in_specs=[cur(0), cur(1), prev(1), cur(2), prev(2),
                  pl.BlockSpec((None, 2, HEADS_PER_GROUP, ATTN_BLK, 2 * ATTN_BLK), lambda r, n: (g, 0, 0, 0, 0))],
        out_specs=(out_spec, out_spec),
        compiler_params=_cparams("parallel", "parallel"),
    )(uv, uv, uv, uv, uv, bias)


def _chunk_scratch(n, width):
    return pltpu.VMEM((width // LANES, n, LANES), F32)


def _to_chunks(scr, val):
    for c in range(scr.shape[0]):
        scr[c] = val[:, c * LANES:(c + 1) * LANES]


def _from_chunks(scr):
    return jnp.concatenate([scr[c] for c in range(scr.shape[0])], axis=1)


def _slabs_from_chunks(dst_ref, scr, d, dtype):
    nc, n, _ = scr.shape
    for r in range(d):
        for c in range(nc):
            col = r * nc * LANES + c * LANES
            dst_ref[:, col:col + LANES] = scr[c, pl.ds(r, n // d, stride=d), :].astype(dtype)


def _chunks_from_slabs(scr, src_ref, d):
    nc, n, _ = scr.shape
    for r in range(d):
        for c in range(nc):
            col = r * nc * LANES + c * LANES
            scr[c, pl.ds(r, n // d, stride=d), :] = src_ref[:, col:col + LANES]


def _attn_merge(outs, lses, ya, *, name):
    S, C = ya.shape
    tm = _tile(S, (512, 256))
    dils = [dil for _, dil in DILATED_GROUPS]

    def body(o0, o1, o2, l0, l1, l2, ya_ref, cat_ref, out_ref, lse_ref, so1, so2, sl1, sl2):
        _chunks_from_slabs(so1, o1, dils[1])
        _chunks_from_slabs(so2, o2, dils[2])
        _chunks_from_slabs(sl1, l1, dils[1])
        _chunks_from_slabs(sl2, l2, dils[2])
        a0, a1, a2 = l0[...], _from_chunks(sl1), _from_chunks(sl2)
        m = jnp.maximum(jnp.maximum(a0, a1), a2)
        e0, e1, e2 = jnp.exp(a0 - m), jnp.exp(a1 - m), jnp.exp(a2 - m)
        den = e0 + e1 + e2
        out = (e0 * o0[...] + e1 * _from_chunks(so1) + e2 * _from_chunks(so2)) / den
        out_ref[...] = out
        lse_ref[...] = m + jnp.log(den)
        cat_ref[:, 0:C] = ya_ref[...]
        cat_ref[:, C:C + ATTN_OUT] = out.astype(cat_ref.dtype)

    blk = pl.BlockSpec((tm, ATTN_OUT), lambda i: (i, 0))
    vblk = [pl.BlockSpec((tm // d, d * ATTN_OUT), lambda i: (i, 0)) for d in dils]
    assert dils[0] == 1
    return pl.pallas_call(
        body, name=name,
        out_shape=(jax.ShapeDtypeStruct((S, C + ATTN_OUT), BF16), jax.ShapeDtypeStruct((S, ATTN_OUT), F32),
                   jax.ShapeDtypeStruct((S, ATTN_OUT), F32)),
        grid=(S // tm,),
        in_specs=vblk + vblk + [pl.BlockSpec((tm, C), lambda i: (i, 0))],
        out_specs=(pl.BlockSpec((tm, C + ATTN_OUT), lambda i: (i, 0)), blk, blk),
        scratch_shapes=[_chunk_scratch(tm, ATTN_OUT)] * 4,
        compiler_params=_cparams("parallel"),
    )(*outs, *lses, ya)


def _attn_prep(dcat, outf, lse, *, name):
    S, CW = dcat.shape
    C = CW - ATTN_OUT
    tm = _tile(S, (512, 256))
    dils = [dil for _, dil in DILATED_GROUPS]
    assert dils[0] == 1
    ones = np.kron(np.eye(HEADS_PER_GROUP, dtype=np.float32), np.ones((HEAD_DIM, HEAD_DIM), np.float32))

    nviews = 3 * (len(dils) - 1)

    def body(dcat_ref, out_ref, l_ref, ones_ref, dyb_ref, dl_ref, *rest):
        views = rest[:nviews]
        s_dyb, s_dl, s_l = rest[nviews:]
        dyb = dcat_ref[:, C:CW]
        dyb_ref[...] = dyb.astype(BF16)
        prod = dyb * out_ref[...]
        ov = ones_ref[...]
        hi, mid, lo = _split_bf16(prod)
        delta = (jnp.dot(hi, ov, preferred_element_type=F32)
                 + jnp.dot(mid, ov, preferred_element_type=F32)
                 + jnp.dot(lo, ov, preferred_element_type=F32))
        dl_ref[...] = delta
        _to_chunks(s_dyb, dyb)
        _to_chunks(s_dl, delta)
        _to_chunks(s_l, l_ref[...])
        for gi, d in enumerate(dils[1:]):
            dyb_v, dl_v, l_v = views[3 * gi:3 * gi + 3]
            _slabs_from_chunks(dyb_v, s_dyb, d, BF16)
            _slabs_from_chunks(dl_v, s_dl, d, F32)
            _slabs_from_chunks(l_v, s_l, d, F32)

    blk = pl.BlockSpec((tm, ATTN_OUT), lambda i: (i, 0))
    view_shapes, view_specs = [], []
    for d in dils[1:]:
        for dt in (BF16, F32, F32):
            view_shapes.append(jax.ShapeDtypeStruct((S // d, d * ATTN_OUT), dt))
            view_specs.append(pl.BlockSpec((tm // d, d * ATTN_OUT), lambda i: (i, 0)))
    res = pl.pallas_call(
        body, name=name,
        out_shape=(jax.ShapeDtypeStruct((S, ATTN_OUT), BF16), jax.ShapeDtypeStruct((S, ATTN_OUT), F32),
                   *view_shapes),
        grid=(S // tm,),
        in_specs=[pl.BlockSpec((tm, CW), lambda i: (i, 0)), blk, blk,
                  pl.BlockSpec((ATTN_OUT, ATTN_OUT), lambda i: (0, 0))],
        out_specs=(blk, blk, *view_specs),
        scratch_shapes=[_chunk_scratch(tm, ATTN_OUT)] * 3,
        compiler_params=_cparams("parallel"),
    )(dcat, outf, lse, jnp.asarray(ones, BF16))
    return [(res[0], res[1], lse)] + [tuple(res[2 + 3 * gi:5 + 3 * gi]) for gi in range(len(dils) - 1)]


ATTN_BWD_RESIDUES = 4


def _attn_bwd(uv, dov, lv, dv_, bias, g, d, *, name):
    rows = uv.shape[0]
    nb = rows // ATTN_BLK
    nres = min(d, ATTN_BWD_RESIDUES)
    steps = (d // nres) * nb
    scale = HEAD_DIM ** -0.5
    Q0, K0, V0 = 0, ATTN_OUT, 2 * ATTN_OUT

    def body(cur_ref, prev_ref, do_ref, l_ref, dl_ref, bias_ref, out_ref, db_ref, dq_s, dk_s, dv_s):
        t = pl.program_id(0)
        n = t % nb
        low = _lane_is_low()

        @pl.when(t == 0)
        def _():
            db_ref[...] = jnp.zeros_like(db_ref)
            dq_s[...] = jnp.zeros_like(dq_s)
            dk_s[...] = jnp.zeros_like(dk_s)
            dv_s[...] = jnp.zeros_like(dv_s)

        @pl.when(t < steps)
        def _():
            sel = jnp.minimum(n, 1)
            lane = lax.broadcasted_iota(jnp.int32, (1, LANES), 1)
            nk2 = 2 * ATTN_BLK
            for rr in range(nres):
                u0 = rr * GROUP_QKV
                o0 = rr * ATTN_OUT
                slabs = [hp * LANES for hp in range(HEADS_PER_GROUP // 2)]

                def cols(base, c0):
                    return slice(base + c0, base + c0 + LANES)

                keys = [jnp.concatenate([prev_ref[:, cols(u0 + K0, c0)], cur_ref[:, cols(u0 + K0, c0)]], axis=0)
                        for c0 in slabs]
                scores = [_dot_nt(_stack_heads(cur_ref[:, cols(u0 + Q0, c0)] * scale, low), keys[hp])
                          for hp, c0 in enumerate(slabs)]
                dps = [_dot_nt(_stack_heads(do_ref[:, cols(o0, c0)], low),
                               jnp.concatenate([prev_ref[:, cols(u0 + V0, c0)], cur_ref[:, cols(u0 + V0, c0)]],
                                               axis=0)) for c0 in slabs]
                stacked = []
                for hp, c0 in enumerate(slabs):
                    lse2 = l_ref[:, cols(o0, c0)]
                    dl2 = dl_ref[:, cols(o0, c0)]
                    pbs, dsbs = [], []
                    for hh in range(2):
                        rws = slice(hh * ATTN_BLK, (hh + 1) * ATTN_BLK)
                        one = lane == hh * HEAD_DIM
                        lse_col = jnp.sum(jnp.where(one, lse2, 0.0), axis=-1, keepdims=True)
                        dl_col = jnp.sum(jnp.where(one, dl2, 0.0), axis=-1, keepdims=True)
                        p = jnp.exp(scores[hp][rws] + bias_ref[sel, 2 * hp + hh] - lse_col)
                        ds = p * (dps[hp][rws] - dl_col)
                        db_ref[2 * hp + hh] += ds
                        pbs.append(p.astype(BF16))
                        dsbs.append((ds * scale).astype(BF16))
                    stacked.append((jnp.concatenate(dsbs, axis=0), jnp.concatenate(dsbs, axis=1),
                                    jnp.concatenate(pbs, axis=1)))
                for hp, c0 in enumerate(slabs):
                    ds_rows, ds_cols, p_cols = stacked[hp]
                    dq = jnp.dot(ds_rows, keys[hp], preferred_element_type=F32)
                    dk = _dot_tn(ds_cols, cur_ref[:, cols(u0 + Q0, c0)])
                    dv = _dot_tn(p_cols, do_ref[:, cols(o0, c0)])
                    dq2 = jnp.where(low, dq[0:ATTN_BLK], dq[ATTN_BLK:nk2])
                    dk2 = jnp.where(low, dk[0:nk2], dk[nk2:2 * nk2])
                    dv2 = jnp.where(low, dv[0:nk2], dv[nk2:2 * nk2])
                    sl = cols(o0, c0)
                    out_ref[:, cols(u0 + Q0, c0)] = dq_s[:, sl].astype(out_ref.dtype)
                    dq_s[:, sl] = dq2
                    out_ref[:, cols(u0 + K0, c0)] = (dk_s[:, sl] + dk2[0:ATTN_BLK]).astype(out_ref.dtype)
                    dk_s[:, sl] = dk2[ATTN_BLK:2 * ATTN_BLK]
                    out_ref[:, cols(u0 + V0, c0)] = (dv_s[:, sl] + dv2[0:ATTN_BLK]).astype(out_ref.dtype)
                    dv_s[:, sl] = dv2[ATTN_BLK:2 * ATTN_BLK]

        @pl.when(t == steps)
        def _():
            for rr in range(nres):
                u0, o0 = rr * GROUP_QKV, rr * ATTN_OUT
                out_ref[:, u0 + Q0:u0 + Q0 + ATTN_OUT] = dq_s[:, o0:o0 + ATTN_OUT].astype(out_ref.dtype)
                out_ref[:, u0 + K0:u0 + K0 + ATTN_OUT] = dk_s[:, o0:o0 + ATTN_OUT].astype(out_ref.dtype)
                out_ref[:, u0 + V0:u0 + V0 + ATTN_OUT] = dv_s[:, o0:o0 + ATTN_OUT].astype(out_ref.dtype)

    def blocks(width, lag=0, back=0):
        def index_map(t):
            tt = jnp.maximum(jnp.minimum(t, steps - 1) - lag, 0)
            return (jnp.maximum(tt % nb - back, 0), tt // nb)
        return pl.BlockSpec((ATTN_BLK, nres * width), index_map)

    return pl.pallas_call(
        body, name=name,
        out_shape=(jax.ShapeDtypeStruct((rows, d * GROUP_QKV), BF16),
                   jax.ShapeDtypeStruct((HEADS_PER_GROUP, ATTN_BLK, 2 * ATTN_BLK), F32)),
        grid=(steps + 1,),
        in_specs=[blocks(GROUP_QKV), blocks(GROUP_QKV, back=1), blocks(ATTN_OUT), blocks(ATTN_OUT), blocks(ATTN_OUT),
                  pl.BlockSpec((None, 2, HEADS_PER_GROUP, ATTN_BLK, 2 * ATTN_BLK), lambda t: (g, 0, 0, 0, 0))],
        out_specs=(pl.BlockSpec((ATTN_BLK, nres * GROUP_QKV),
                                lambda t: (jnp.maximum(t - 1, 0) % nb, jnp.maximum(t - 1, 0) // nb)),
                   pl.BlockSpec((HEADS_PER_GROUP, ATTN_BLK, 2 * ATTN_BLK), lambda t: (0, 0, 0))),
        scratch_shapes=[pltpu.VMEM((ATTN_BLK, nres * ATTN_OUT), F32)] * 3,
        compiler_params=_cparams("arbitrary"),
    )(uv, uv, dov, lv, dv_, bias)


def _split_bf16(x):
    hi = x.astype(BF16)
    r1 = x - hi.astype(F32)
    mid = r1.astype(BF16)
    lo = (r1 - mid.astype(F32)).astype(BF16)
    return hi, mid, lo


RELBIAS_CHUNK = 4096


def _relbias_reduce(dbs, buckets, *, name):
    flat = ATTN_BLK * 2 * ATTN_BLK
    dbf = jnp.stack([db.reshape(HEADS_PER_GROUP, flat) for db in dbs])
    bkf = buckets.reshape(N_GROUPS, 1, flat)

    def body(db_ref, bk_ref, o_ref):
        c = pl.program_id(1)
        rows = lax.broadcasted_iota(jnp.int32, (LANES, RELBIAS_CHUNK), 0).astype(F32)
        onehot = jnp.where(rows == bk_ref[0], 1.0, 0.0).astype(BF16)
        hi, mid, lo = _split_bf16(db_ref[0])
        part = _dot_nt(hi, onehot) + _dot_nt(mid, onehot) + _dot_nt(lo, onehot)

        @pl.when(c == 0)
        def _():
            o_ref[0] = part

        @pl.when(c > 0)
        def _():
            o_ref[0] += part

    return pl.pallas_call(
        body, name=name,
        out_shape=jax.ShapeDtypeStruct((N_GROUPS, HEADS_PER_GROUP, LANES), F32),
        grid=(N_GROUPS, flat // RELBIAS_CHUNK),
        in_specs=[pl.BlockSpec((1, HEADS_PER_GROUP, RELBIAS_CHUNK), lambda g, c: (g, 0, c)),
                  pl.BlockSpec((1, 1, RELBIAS_CHUNK), lambda g, c: (g, 0, c))],
        out_specs=pl.BlockSpec((1, HEADS_PER_GROUP, LANES), lambda g, c: (g, 0, 0)),
        compiler_params=_cparams("parallel", "arbitrary"),
    )(dbf, bkf)


def _my_position():
    x, y, c = lax.axis_index("x"), lax.axis_index("y"), lax.axis_index("c")
    return x, y, c


def _linear(pos):
    return 4 * pos[0] + 2 * pos[1] + pos[2]


def _peer(pos, k):
    x, y, c = pos
    return ((1 - x) if k & 4 else x, (1 - y) if k & 2 else y, (1 - c) if k & 1 else c)


HBM_SPEC = pl.BlockSpec(memory_space=pltpu.HBM)
SEM_SPEC = pl.BlockSpec(memory_space=pltpu.SEMAPHORE)
DATAFLOW = pltpu.SideEffectType.DATAFLOW_SIDE_EFFECTING


def _exchange_copies(src, land, sems, send_window, recv_window, with_arrivals):
    send_sems, recv_sems, local_sems = sems
    T = len(src)
    me = _my_position()
    me_lin = _linear(me)
    local = [pltpu.make_async_copy(send_window(t, src[t], me_lin), recv_window(t, land[t], me_lin),
                                   local_sems.at[t]) for t in range(T)]
    sends, arrivals = [], []
    for t in range(T):
        for k in range(1, N_DEV):
            peer = _peer(me, k)
            peer_lin = _linear(peer)
            sem = t * (N_DEV - 1) + k - 1
            sends.append(pltpu.make_async_remote_copy(
                src_ref=send_window(t, src[t], peer_lin), dst_ref=recv_window(t, land[t], me_lin),
                send_sem=send_sems.at[sem], recv_sem=recv_sems.at[sem],
                device_id=peer, device_id_type=MESH))
            if with_arrivals:
                arrivals.append(pltpu.make_async_remote_copy(
                    src_ref=send_window(t, src[t], me_lin), dst_ref=recv_window(t, land[t], peer_lin),
                    send_sem=send_sems.at[sem], recv_sem=recv_sems.at[sem],
                    device_id=peer, device_id_type=MESH))
    return local, sends, arrivals


def _exchange_start(srcs, land_shapes, send_window, recv_window, *, name, dep=None):
    T = len(srcs)
    n_in = 2 * T + (1 if dep is not None else 0)

    def body(*refs):
        src = refs[:T]
        land = refs[T:2 * T]
        sems = refs[n_in:n_in + 3]
        token = refs[-1]
        local, sends, _ = _exchange_copies(src, land, sems, send_window, recv_window, False)
        for cp in local + sends:
            cp.start()
        token[...] = jnp.zeros_like(token)

    lands = [lax.empty(ls.shape, ls.dtype) for ls in land_shapes]
    operands = [pltpu.with_memory_space_constraint(a, pltpu.HBM) for a in list(srcs) + lands]
    outs = pl.pallas_call(
        body, name=name,
        out_shape=(pltpu.SemaphoreType.DMA((T * (N_DEV - 1),)), pltpu.SemaphoreType.DMA((T * (N_DEV - 1),)),
                   pltpu.SemaphoreType.DMA((T,)),
                   *[pltpu.HBM(a.shape, a.dtype) for a in operands],
                   jax.ShapeDtypeStruct((8, LANES), F32)),
        in_specs=[HBM_SPEC] * (2 * T) + ([ANY] if dep is not None else []),
        out_specs=(SEM_SPEC,) * 3 + (HBM_SPEC,) * (2 * T) + (VMEM_SPEC,),
        input_output_aliases={i: 3 + i for i in range(2 * T)},
        compiler_params=pltpu.CompilerParams(has_side_effects=DATAFLOW),
    )(*operands, *([dep] if dep is not None else []))
    return outs[:3], outs[3:3 + T], outs[3 + T:3 + 2 * T], outs[-1]


def _exchange_wait(started, after, send_window, recv_window, *, name):
    sems, srcs, lands, _ = started
    T = len(srcs)

    def body(*refs):
        src = refs[:T]
        land = refs[T:2 * T]
        sem_refs = refs[2 * T:2 * T + 3]
        local, sends, arrivals = _exchange_copies(src, land, sem_refs, send_window, recv_window, True)
        for cp in arrivals:
            cp.wait_recv()
        for cp in sends:
            cp.wait_send()
        for cp in local:
            cp.wait()

    outs = pl.pallas_call(
        body, name=name,
        out_shape=tuple(pltpu.HBM(a.shape, a.dtype) for a in list(srcs) + list(lands)),
        in_specs=[HBM_SPEC] * (2 * T) + [SEM_SPEC] * 3 + [ANY],
        out_specs=(HBM_SPEC,) * (2 * T),
        input_output_aliases={i: i for i in range(2 * T)},
        compiler_params=pltpu.CompilerParams(has_side_effects=DATAFLOW),
    )(*srcs, *lands, *sems, after)
    return outs[T:]


def _shard_window(kind, width):
    def win(ref, lin):
        if kind == "slot":
            return ref.at[lin]
        if kind == "col":
            return ref.at[:, pl.ds(pl.multiple_of(lin * width, LANES), width)]
        if kind == "row":
            return ref.at[pl.ds(pl.multiple_of(lin * width, 8), width), :]
        if kind == "lcol":
            return ref.at[:, :, pl.ds(pl.multiple_of(lin * width, LANES), width)]
        if kind == "lrow":
            return ref.at[:, pl.ds(pl.multiple_of(lin * width, 8), width), :]
        raise ValueError(kind)
    return win


def _shard_windows(kinds, shard_shapes):
    return [_shard_window(k, (ss[-1] if k in ("col", "lcol") else ss[-2])) for k, ss in zip(kinds, shard_shapes)]


def _allgather_start(shards, kinds, full_shapes, *, name, dep=None):
    wins = _shard_windows(kinds, [s.shape for s in shards])
    send_window = lambda t, ref, lin: ref
    recv_window = lambda t, ref, lin: wins[t](ref, lin)
    started = _exchange_start(shards, [jax.ShapeDtypeStruct(fs, s.dtype) for fs, s in zip(full_shapes, shards)],
                              send_window, recv_window, name=name + "_start", dep=dep)
    return started, lambda after: _exchange_wait(started, after, send_window, recv_window, name=name + "_wait")


def _scatter_start(fulls, kinds, shard_shapes, *, name):
    wins = _shard_windows(kinds, shard_shapes)
    send_window = lambda t, ref, lin: wins[t](ref, lin)
    recv_window = lambda t, ref, lin: ref.at[lin]
    started = _exchange_start(
        fulls, [jax.ShapeDtypeStruct((N_DEV,) + tuple(ss), f.dtype) for ss, f in zip(shard_shapes, fulls)],
        send_window, recv_window, name=name + "_start")
    return started, lambda after: _exchange_wait(started, after, send_window, recv_window, name=name + "_wait")


def _small_gather(pack, *, reduce, name):
    R = pack.shape[0]

    def body(p_ref, o_ref, *rest):
        if reduce:
            buf, send_sems, recv_sems = rest
        else:
            buf = o_ref
            send_sems, recv_sems = rest
        me = _my_position()
        me_lin = _linear(me)
        buf[me_lin] = p_ref[...]
        sends = []
        for k in range(1, N_DEV):
            peer = _peer(me, k)
            cp = pltpu.make_async_remote_copy(
                src_ref=p_ref, dst_ref=buf.at[me_lin],
                send_sem=send_sems.at[k - 1], recv_sem=recv_sems.at[k - 1],
                device_id=peer, device_id_type=MESH)
            cp.start()
            sends.append(cp)
        for k in range(1, N_DEV):
            peer = _peer(me, k)
            pltpu.make_async_remote_copy(
                src_ref=p_ref, dst_ref=buf.at[_linear(peer)],
                send_sem=send_sems.at[k - 1], recv_sem=recv_sems.at[k - 1],
                device_id=peer, device_id_type=MESH).wait_recv()
        for cp in sends:
            cp.wait_send()
        if reduce:
            acc = buf[0]
            for s in range(1, N_DEV):
                acc = acc + buf[s]
            o_ref[...] = acc

    scratch = [pltpu.SemaphoreType.DMA((N_DEV - 1,)), pltpu.SemaphoreType.DMA((N_DEV - 1,))]
    if reduce:
        scratch = [pltpu.VMEM((N_DEV, R, LANES), F32)] + scratch
        out_shape = jax.ShapeDtypeStruct((R, LANES), F32)
    else:
        out_shape = jax.ShapeDtypeStruct((N_DEV, R, LANES), F32)
    return pl.pallas_call(
        body, name=name, out_shape=out_shape,
        in_specs=[VMEM_SPEC], out_specs=VMEM_SPEC, scratch_shapes=scratch,
        compiler_params=pltpu.CompilerParams(has_side_effects=True, vmem_limit_bytes=VMEM_LIMIT),
    )(pack)


def _adamw_math(w, g, m, v):
    m = ADAM_B1 * m + (1.0 - ADAM_B1) * g
    v = ADAM_B2 * v + (1.0 - ADAM_B2) * jnp.square(g)
    m_hat = m / (1.0 - ADAM_B1 ** ADAM_STEP)
    v_hat = v / (1.0 - ADAM_B2 ** ADAM_STEP)
    delta = -ADAM_LR * (m_hat / (jnp.sqrt(v_hat) + ADAM_EPS) + ADAM_WD * w)
    return delta, m, v


def _adamw_from_partials(parts, w, m, v, *, name):
    R, C = w.shape
    rl = parts[0].shape[1]
    assert all(p.shape == (N_DEV, rl, C) for p in parts) and rl * len(parts) == R
    tr = _tile(rl, (256, 128, 64, 32, 16))
    per = rl // tr
    L = len(parts)

    def body(*refs):
        p_refs = refs[:L]
        w_ref, m_ref, v_ref, g_ref, d_ref, nm_ref, nv_ref = refs[L:]
        i = pl.program_id(0)
        for l in range(L):
            @pl.when((i >= l * per) & (i < (l + 1) * per))
            def _(l=l):
                p_ref = p_refs[l]
                g = p_ref[0].astype(F32)
                for s in range(1, N_DEV):
                    g = g + p_ref[s].astype(F32)
                d, nm, nv = _adamw_math(w_ref[...], g, m_ref[...], v_ref[...])
                g_ref[...] = g
                d_ref[...] = d
                nm_ref[...] = nm
                nv_ref[...] = nv

    blk = pl.BlockSpec((tr, C), lambda i: (i, 0))
    part_specs = [pl.BlockSpec((N_DEV, tr, C), lambda i, l=l: (0, jnp.clip(i - l * per, 0, per - 1), 0))
                  for l in range(L)]
    return pl.pallas_call(
        body, name=name,
        out_shape=(jax.ShapeDtypeStruct((R, C), F32),) * 4,
        grid=(R // tr,),
        in_specs=part_specs + [blk, blk, blk],
        out_specs=(blk,) * 4,
        compiler_params=_cparams("parallel"),
    )(*parts, w, m, v)


def _adamw_small(g, w, m, v, *, name):
    def body(g_ref, w_ref, m_ref, v_ref, d_ref, nm_ref, nv_ref):
        d, nm, nv = _adamw_math(w_ref[...], g_ref[...], m_ref[...], v_ref[...])
        d_ref[...] = d
        nm_ref[...] = nm
        nv_ref[...] = nv

    return pl.pallas_call(
        body, name=name,
        out_shape=(jax.ShapeDtypeStruct(g.shape, F32),) * 3,
        in_specs=[VMEM_SPEC] * 4, out_specs=(VMEM_SPEC,) * 3,
    )(g, w, m, v)


def _pack_rows(pieces):
    flat = jnp.concatenate([p.reshape(-1) for p in pieces])
    n = flat.shape[0]
    padded = -(-n // (8 * LANES)) * (8 * LANES)
    return jnp.pad(flat, (0, padded - n)).reshape(padded // LANES, LANES)


def _unpack_rows(pack, shapes):
    flat = pack.reshape(-1)
    out, pos = [], 0
    for s in shapes:
        n = int(np.prod(s))
        out.append(flat[pos:pos + n].reshape(s))
        pos += n
    return out


def kernel(x, rel_bias, ab_norm, ab_w_in, ab_conv_w, ab_conv_b, ab_ln_g, ab_ln_b, ab_w_out, sc_norm, sc_w_in, sc_conv_w, sc_w_out, mlp_norm, mlp_w_up, mlp_w_down, final_norm, loss_target, m_rel_bias, m_ab_norm, m_ab_w_in, m_ab_conv_w, m_ab_conv_b, m_ab_ln_g, m_ab_ln_b, m_ab_w_out, m_sc_norm, m_sc_w_in, m_sc_conv_w, m_sc_w_out, m_mlp_norm, m_mlp_w_up, m_mlp_w_down, m_final_norm, v_rel_bias, v_ab_norm, v_ab_w_in, v_ab_conv_w, v_ab_conv_b, v_ab_ln_g, v_ab_ln_b, v_ab_w_out, v_sc_norm, v_sc_w_in, v_sc_conv_w, v_sc_w_out, v_mlp_norm, v_mlp_w_up, v_mlp_w_down, v_final_norm):
    S, D = x.shape[1], x.shape[2]
    CA = ab_conv_b.shape[1]
    C2 = 2 * CA
    AB_IN = C2 + ATTN_IN
    me_lin = _linear(_my_position())
    xs = x.reshape(S, D)
    tgt = loss_target.reshape(S, D)

    cw_sh = ab_conv_w.shape[2]
    scn_sh = sc_norm.shape[1]
    scw_sh = sc_conv_w.shape[2]
    small_sh_shapes = [(CONV_A_WIDTH, cw_sh), (scn_sh,), (SC_CONV_WIDTH, scw_sh)]
    small_pack = _pack_rows([ab_conv_w[0], sc_norm[0], sc_conv_w[0]])
    w_in_sh = ab_w_in[0].astype(BF16)
    ag_ab, wait_ab = _allgather_start(
        [small_pack, w_in_sh, ab_w_out[0].astype(BF16)], ["slot", "slot", "row"],
        [(N_DEV,) + small_pack.shape, (N_DEV,) + w_in_sh.shape, (N_DEV * ab_w_out.shape[1], D)],
        name="allgather_ab")
    ag_mlp, wait_mlp = _allgather_start(
        [mlp_w_up.astype(BF16), mlp_w_down.astype(BF16)], ["lcol", "lrow"],
        [(2, D, N_DEV * mlp_w_up.shape[2]), (2, N_DEV * mlp_w_down.shape[1], D)], name="allgather_mlp",
        dep=ag_ab[3])
    ag_sc, wait_sc = _allgather_start(
        [sc_w_in[0].astype(BF16), sc_w_out[0].astype(BF16)], ["col", "row"],
        [(D, N_DEV * sc_w_in.shape[2]), (N_DEV * sc_w_out.shape[1], D)], name="allgather_sc",
        dep=ag_mlp[3])

    buckets = _bucket_tables()
    biases = _bias_tables(rel_bias, buckets, name="bias_tables")

    dils = [dil for _, dil in DILATED_GROUPS]
    n0_all = _rmsnorm_fwd(xs, ab_norm, name="norm_ab", dep=ag_sc[3], views=dils[1:])
    n0 = n0_all[0]
    small_params, w_in_g, w_out = wait_ab(n0)
    per_dev = [_unpack_rows(small_params[s], small_sh_shapes) for s in range(N_DEV)]
    conv_w_full = jnp.concatenate([p[0] for p in per_dev], axis=1)
    sc_norm_full = jnp.concatenate([p[1] for p in per_dev], axis=0)[None]
    sc_conv_full = jnp.concatenate([p[2] for p in per_dev], axis=1)
    w_in = jnp.transpose(w_in_g, (1, 0, 2)).reshape(D, AB_IN)
    w_c = w_in[:, :C2]
    w_q = w_in[:, C2:]
    w_grp = [jnp.concatenate([w_q[:, t * N_GROUPS * ATTN_OUT + g * ATTN_OUT:][:, :ATTN_OUT] for t in range(3)], axis=1)
             for g in range(N_GROUPS)]
    uc = _mm_nn(n0, w_c, out_dtype=F32, name="mm_ab_in_conv", wide=True)
    uqs = [_mm_nn(n0_all[g], w_grp[g], out_dtype=BF16, slabs=dils[g], name=f"mm_ab_in_qkv{g}")
           for g in range(N_GROUPS)]
    ya, hglu, ct = _conv_fwd(uc, conv_w_full, ab_conv_b, ab_ln_g, ab_ln_b, name="conv_fwd")
    outs, lses = zip(*[_attn_fwd(uqs[g], biases, g, dils[g], name=f"attn_fwd_{g}") for g in range(N_GROUPS)])
    cat, outf, lse = _attn_merge(outs, lses, ya, name="attn_merge")
    h1 = _mm_nn(cat, w_out, out_dtype=F32, residual=xs, name="mm_ab_out", wide=True)
    w_up, w_dn = wait_mlp(h1)
    n1, z0 = _norm_mm_nn(h1, mlp_norm[0:1], (w_up, 0), out_dtype=BF16, name="norm_mm_up0", wide=True)
    h2 = _mm_nn(z0, (w_dn, 0), out_dtype=F32, residual=h1, a_fn=_relu_sq, name="mm_down0", wide=True)
    w_sc_in, w_sc_out = wait_sc(h2)
    n2, u3 = _norm_mm_nn(h2, sc_norm_full, w_sc_in, out_dtype=BF16, name="norm_mm_sc_in", wide=True)
    ysc = _sc_fwd(u3, sc_conv_full, name="sc_fwd")
    h3 = _mm_nn(ysc, w_sc_out, out_dtype=F32, residual=h2, name="mm_sc_out", wide=True)
    n3, z1 = _norm_mm_nn(h3, mlp_norm[1:2], (w_up, 1), out_dtype=BF16, name="norm_mm_up1", wide=True)
    h4 = _mm_nn(z1, (w_dn, 1), out_dtype=F32, residual=h3, a_fn=_relu_sq, name="mm_down1", wide=True)

    def dz_epilogue(acc, z):
        return acc * (2.0 * jnp.maximum(z.astype(F32), 0.0))

    dh4, dh4b, acc_final = _loss_bwd(h4, tgt, final_norm[None], name="loss_bwd")
    dz1 = _mm_nt([(dh4b, (w_dn, 1))], out_dtype=BF16, epilogue=dz_epilogue, extra=z1, name="mm_d_down1")
    g_dn1 = _mm_tn(z1, dh4b, a_fn=_relu_sq, name="mm_gw_down1")
    g_up1 = _mm_tn(n3, dz1, name="mm_gw_up1")
    rs_mlp1, wait_rs_mlp1 = _scatter_start([g_up1, g_dn1], ["col", "row"],
                                           [mlp_w_up.shape[1:], mlp_w_down.shape[1:]], name="scatter_mlp1")
    dh3, dh3b, acc_mlp1 = _mm_nt_rms_bwd(dz1, (w_up, 1), h3, mlp_norm[1:2], dh4, name="mm_d_up1_norm_bwd",
                                         dep=rs_mlp1[3])

    dysc = _mm_nt([(dh3b, w_sc_out)], out_dtype=F32, name="mm_d_sc_out")
    g_sc_out = _mm_tn(ysc, dh3b, name="mm_gw_sc_out")
    du3, acc_scw = _sc_bwd(u3, dysc, sc_conv_full, name="sc_bwd")
    g_sc_in = _mm_tn(n2, du3, name="mm_gw_sc_in")
    rs_sc, wait_rs_sc = _scatter_start([g_sc_in, g_sc_out], ["col", "row"],
                                       [sc_w_in.shape[1:], sc_w_out.shape[1:]], name="scatter_sc")
    dh2, dh2b, acc_sc = _mm_nt_rms_bwd(du3, w_sc_in, h2, sc_norm_full, dh3, name="mm_d_sc_in_norm_bwd",
                                       dep=rs_sc[3])

    dz0 = _mm_nt([(dh2b, (w_dn, 0))], out_dtype=BF16, epilogue=dz_epilogue, extra=z0, name="mm_d_down0")
    g_dn0 = _mm_tn(z0, dh2b, a_fn=_relu_sq, name="mm_gw_down0")
    g_up0 = _mm_tn(n1, dz0, name="mm_gw_up0")
    rs_mlp0, wait_rs_mlp0 = _scatter_start([g_up0, g_dn0], ["col", "row"],
                                           [mlp_w_up.shape[1:], mlp_w_down.shape[1:]], name="scatter_mlp0")
    dh1, dh1b, acc_mlp0 = _mm_nt_rms_bwd(dz0, (w_up, 0), h1, mlp_norm[0:1], dh2, name="mm_d_up0_norm_bwd",
                                         dep=rs_mlp0[3])

    dcat = _mm_nt([(dh1b, w_out)], out_dtype=F32, name="mm_d_ab_out")
    g_ab_out = _mm_tn(cat, dh1b, name="mm_gw_ab_out")
    prep = _attn_prep(dcat, outf, lse, name="attn_prep")
    dqkv, dbs = zip(*[_attn_bwd(uqs[g], prep[g][0], prep[g][2], prep[g][1], biases, g, dils[g],
                                name=f"attn_bwd_{g}") for g in range(N_GROUPS)])
    drel = _relbias_reduce(dbs, buckets, name="relbias_reduce")
    dc, acc_conv = _conv_bwd_ln(ct, dcat, hglu, ab_ln_g, ab_ln_b, name="conv_bwd_ln")
    duc = _conv_bwd_in(dc, uc, conv_w_full, name="conv_bwd_in")
    g_wc = _mm_tn(n0, duc, name="mm_gw_ab_in_conv")
    g_wgrp = [_mm_tn(n0_all[g], dqkv[g], slabs=dils[g], name=f"mm_gw_ab_in_qkv{g}") for g in range(N_GROUPS)]
    g_wq = jnp.concatenate([g_wgrp[g][:, t * ATTN_OUT:(t + 1) * ATTN_OUT]
                            for t in range(3) for g in range(N_GROUPS)], axis=1)
    g_w_in = jnp.concatenate([g_wc, g_wq], axis=1).reshape(D, N_DEV, AB_IN // N_DEV).transpose(1, 0, 2)
    rs_ab, wait_rs_ab = _scatter_start([g_w_in, g_ab_out], ["slot", "row"],
                                       [w_in_sh.shape, ab_w_out.shape[1:]], name="scatter_ab")
    dn0 = _mm_nt([(duc, w_c), (dqkv[0], w_grp[0])], out_dtype=F32, name="mm_d_ab_in", dep=rs_ab[3])
    dn0_views = [(_mm_nt([(dqkv[g], w_grp[g])], out_dtype=F32, slabs=dils[g], name=f"mm_d_ab_in_qkv{g}"), dils[g])
                 for g in range(1, N_GROUPS)]
    grad_x, grad_xb, acc_ab = _rms_bwd(xs, ab_norm, dn0, dh1, name="norm_ab_bwd", dn_views=dn0_views)

    small_full = [drel[:, :, :NUM_BUCKETS].transpose(2, 0, 1).reshape(NUM_BUCKETS, N_GROUPS * HEADS_PER_GROUP),
                  acc_ab[0], acc_conv[0:CONV_A_WIDTH], acc_conv[32],
                  acc_conv[33], acc_conv[34], acc_sc[0], acc_scw[0:SC_CONV_WIDTH],
                  jnp.stack([acc_mlp0[0], acc_mlp1[0]]), acc_final[0], acc_final[1]]
    small_full_shapes = [p.shape for p in small_full]
    summed = _unpack_rows(_small_gather(_pack_rows(small_full), reduce=True, name="allreduce_small"),
                          small_full_shapes)
    (s_rel, s_abn, s_cw, s_cb, s_lg, s_lb, s_scn, s_scw, s_mlpn, s_fn, s_err) = summed
    loss = (0.5 / D) * jnp.sum(s_err)
    small_grads = {
        "rel_bias": s_rel, "ab_norm": s_abn[None],
        "ab_conv_w": lax.dynamic_slice_in_dim(s_cw, me_lin * cw_sh, cw_sh, axis=1)[None],
        "ab_conv_b": s_cb[None], "ab_ln_g": s_lg[None], "ab_ln_b": s_lb[None],
        "sc_norm": lax.dynamic_slice_in_dim(s_scn, me_lin * scn_sh, scn_sh, axis=0)[None],
        "sc_conv_w": lax.dynamic_slice_in_dim(s_scw, me_lin * scw_sh, scw_sh, axis=1)[None],
        "mlp_norm": s_mlpn, "final_norm": s_fn,
    }
    small_w = {"rel_bias": (rel_bias, m_rel_bias, v_rel_bias), "ab_norm": (ab_norm, m_ab_norm, v_ab_norm),
               "ab_conv_w": (ab_conv_w, m_ab_conv_w, v_ab_conv_w), "ab_conv_b": (ab_conv_b, m_ab_conv_b, v_ab_conv_b),
               "ab_ln_g": (ab_ln_g, m_ab_ln_g, v_ab_ln_g), "ab_ln_b": (ab_ln_b, m_ab_ln_b, v_ab_ln_b),
               "sc_norm": (sc_norm, m_sc_norm, v_sc_norm), "sc_conv_w": (sc_conv_w, m_sc_conv_w, v_sc_conv_w),
               "mlp_norm": (mlp_norm, m_mlp_norm, v_mlp_norm), "final_norm": (final_norm, m_final_norm, v_final_norm)}
    small_names = list(small_grads)
    small_shapes = [small_grads[n].shape for n in small_names]
    d_pack, m_pack, v_pack = _adamw_small(
        _pack_rows([small_grads[n] for n in small_names]), _pack_rows([small_w[n][0] for n in small_names]),
        _pack_rows([small_w[n][1] for n in small_names]), _pack_rows([small_w[n][2] for n in small_names]),
        name="adamw_small")
    small = {n: (small_grads[n], d, nm_, nv_) for n, d, nm_, nv_ in zip(
        small_names, _unpack_rows(d_pack, small_shapes), _unpack_rows(m_pack, small_shapes),
        _unpack_rows(v_pack, small_shapes))}

    p_up1, p_dn1 = wait_rs_mlp1(grad_xb)
    p_sc_in, p_sc_out = wait_rs_sc(grad_xb)
    p_up0, p_dn0 = wait_rs_mlp0(grad_xb)
    p_w_in, p_ab_out = wait_rs_ab(grad_xb)
    big = {}
    for nm, parts, w, m, v in (("ab_w_in", [p_w_in], ab_w_in, m_ab_w_in, v_ab_w_in),
                               ("ab_w_out", [p_ab_out], ab_w_out, m_ab_w_out, v_ab_w_out),
                               ("sc_w_in", [p_sc_in], sc_w_in, m_sc_w_in, v_sc_w_in),
                               ("sc_w_out", [p_sc_out], sc_w_out, m_sc_w_out, v_sc_w_out),
                               ("mlp_w_up", [p_up0, p_up1], mlp_w_up, m_mlp_w_up, v_mlp_w_up),
                               ("mlp_w_down", [p_dn0, p_dn1], mlp_w_down, m_mlp_w_down, v_mlp_w_down)):
        C = w.shape[-1]
        res = _adamw_from_partials(parts, w.reshape(-1, C), m.reshape(-1, C), v.reshape(-1, C), name="adamw_" + nm)
        big[nm] = tuple(r.reshape(w.shape) for r in res)

    order = ["rel_bias", "ab_norm", "ab_w_in", "ab_conv_w", "ab_conv_b", "ab_ln_g", "ab_ln_b", "ab_w_out",
             "sc_norm", "sc_w_in", "sc_conv_w", "sc_w_out", "mlp_norm", "mlp_w_up", "mlp_w_down", "final_norm"]
    allres = {**big, **small}
    return (loss, grad_x.reshape(x.shape),
            *[allres[n][0] for n in order], *[allres[n][1] for n in order],
            *[allres[n][2] for n in order], *[allres[n][3] for n in order])
```

```python
import functools
import math

import numpy as np
import jax
import jax.numpy as jnp
from jax import lax
from jax.experimental import pallas as pl
from jax.experimental.pallas import tpu as pltpu

F32 = jnp.float32
BF16 = jnp.bfloat16

HEAD_DIM = 64
HEADS_PER_GROUP = 8
DILATED_GROUPS = ((128, 1), (512, 4), (2048, 16))
N_GROUPS = 3
ATTN_OUT = HEADS_PER_GROUP * HEAD_DIM
ATTN_IN = 3 * N_GROUPS * ATTN_OUT
GROUP_QKV = 3 * ATTN_OUT
ATTN_BLK = 128
CONV_A_WIDTH = 31
SC_CONV_WIDTH = 3
NUM_BUCKETS = 32
REL_MAX_DISTANCE = 2048
RMS_EPS = 1e-6
LN_EPS = 1e-5
NEG_INF = -1e30
ADAM_LR = 0.001
ADAM_B1 = 0.9
ADAM_B2 = 0.999
ADAM_EPS = 1e-08
ADAM_WD = 0.01
ADAM_STEP = 10

N_DEV = 8
HALO = 32
LANES = 128
VMEM_LIMIT = 56 * 1024 * 1024
MESH = pl.DeviceIdType.MESH
ANY = pl.BlockSpec(memory_space=pl.ANY)
VMEM_SPEC = pl.BlockSpec(memory_space=pltpu.VMEM)


def _tile(n, prefs):
    for t in prefs:
        if n % t == 0:
            return t
    return n


def _cparams(*sem):
    return pltpu.CompilerParams(dimension_semantics=sem, vmem_limit_bytes=VMEM_LIMIT)


def _relu_sq(z):
    return jnp.square(jnp.maximum(z, 0))


def _dot_nt(a, b):
    return lax.dot_general(a, b, (((1,), (1,)), ((), ())), preferred_element_type=F32)


def _dot_tn(a, b):
    return lax.dot_general(a, b, (((0,), (0,)), ((), ())), preferred_element_type=F32)


def _weight(b):
    if not isinstance(b, tuple):
        return b, b.shape, pl.BlockSpec
    arr, layer = b

    def spec(block, index_map):
        return pl.BlockSpec((None,) + tuple(block), lambda *g: (layer,) + tuple(index_map(*g)))

    return arr, arr.shape[1:], spec


def _mm_nn(a, b, *, out_dtype, name, residual=None, a_fn=None, slabs=1, wide=False):
    M, K = a.shape
    K //= slabs
    b, (_, N), b_spec = _weight(b)
    tm = _tile(M, (1024, 512, 256) if wide else (2048, 1024, 512, 256))
    tn = _tile(N, (1024, 512, 384, 256, 128) if wide else (512, 384, 256, 128))
    tk = _tile(K, (2048, 1024, 512, 256, 128) if wide else (1024, 512, 256, 128))
    nk = K // tk
    nj = N // tn
    has_res = residual is not None

    def body(*refs):
        if has_res:
            a_ref, b_ref, r_ref, o_ref = refs[:4]
        else:
            a_ref, b_ref, o_ref = refs[:3]
        av = a_ref[...]
        if a_fn is not None:
            av = a_fn(av)
        part = jnp.dot(av, b_ref[...], preferred_element_type=F32)

        def finish(acc):
            if has_res:
                acc = acc + r_ref[...]
            o_ref[...] = acc.astype(o_ref.dtype)

        if nk == 1:
            finish(part)
        else:
            acc_ref = refs[-1]
            k = pl.program_id(2)

            @pl.when(k == 0)
            def _():
                acc_ref[...] = part

            @pl.when((k > 0) & (k < nk - 1))
            def _():
                acc_ref[...] += part

            @pl.when(k == nk - 1)
            def _():
                finish(acc_ref[...] + part)

    in_specs = [pl.BlockSpec((tm, tk), lambda i, j, k: (i, (j // nj) * nk + k)),
                b_spec((tk, tn), lambda i, j, k: (k, j % nj))]
    args = [a, b]
    if has_res:
        in_specs.append(pl.BlockSpec((tm, tn), lambda i, j, k: (i, j)))
        args.append(residual)
    return pl.pallas_call(
        body, name=name,
        out_shape=jax.ShapeDtypeStruct((M, slabs * N), out_dtype),
        grid=(M // tm, slabs * nj, nk),
        in_specs=in_specs,
        out_specs=pl.BlockSpec((tm, tn), lambda i, j, k: (i, j)),
        scratch_shapes=[pltpu.VMEM((tm, tn), F32)] if nk > 1 else [],
        compiler_params=_cparams("parallel", "parallel", "arbitrary"),
    )(*args)


def _norm_mm_nn(h, g, b, *, out_dtype, name, wide=False):
    M, K = h.shape
    b, (_, N), b_spec = _weight(b)
    tm = _tile(M, (2048, 1024, 512, 256))
    tn = _tile(N, (1024, 512, 384, 256, 128) if wide else (512, 384, 256, 128))

    def body(h_ref, g_ref, b_ref, n_ref, o_ref):
        @pl.when(pl.program_id(1) == 0)
        def _():
            x = h_ref[...]
            r = lax.rsqrt(jnp.mean(x * x, axis=-1, keepdims=True) + RMS_EPS)
            n_ref[...] = (x * r * g_ref[...]).astype(BF16)

        o_ref[...] = jnp.dot(n_ref[...], b_ref[...], preferred_element_type=F32).astype(o_ref.dtype)

    return pl.pallas_call(
        body, name=name,
        out_shape=(jax.ShapeDtypeStruct((M, K), BF16), jax.ShapeDtypeStruct((M, N), out_dtype)),
        grid=(M // tm, N // tn),
        in_specs=[pl.BlockSpec((tm, K), lambda i, j: (i, 0)), pl.BlockSpec((1, K), lambda i, j: (0, 0)),
                  b_spec((K, tn), lambda i, j: (0, j))],
        out_specs=(pl.BlockSpec((tm, K), lambda i, j: (i, 0)), pl.BlockSpec((tm, tn), lambda i, j: (i, j))),
        compiler_params=_cparams("parallel", "arbitrary"),
    )(h, g, b)


def _mm_nt(pairs, *, out_dtype, name, epilogue=None, extra=None, dep=None, slabs=1):
    assert slabs == 1 or len(pairs) == 1
    M = pairs[0][0].shape[0]
    weights = [_weight(p[1]) for p in pairs]
    Ko = weights[0][1][0]
    tm = _tile(M, (2048, 1024, 512, 256))
    to = _tile(Ko, (1024, 512, 256, 128))
    njo = Ko // to
    tks = [_tile(p[0].shape[1] // slabs, (1024, 768, 512, 256, 128)) for p in pairs]
    steps = [p[0].shape[1] // slabs // tk for p, tk in zip(pairs, tks)]
    offs = [sum(steps[:i]) for i in range(len(pairs))]
    nk = sum(steps)
    npair = len(pairs)
    has_extra = extra is not None

    def body(*refs):
        ab = refs[:2 * npair]
        pos = 2 * npair
        e_ref = None
        if has_extra:
            e_ref = refs[pos]
            pos += 1
        if dep is not None:
            pos += 1
        o_ref = refs[pos]
        acc_ref = refs[pos + 1]
        k = pl.program_id(2)

        @pl.when(k == 0)
        def _():
            acc_ref[...] = jnp.zeros_like(acc_ref)

        for p in range(npair):
            @pl.when((k >= offs[p]) & (k < offs[p] + steps[p]))
            def _(p=p):
                acc_ref[...] += _dot_nt(ab[2 * p][...], ab[2 * p + 1][...])

        @pl.when(k == nk - 1)
        def _():
            acc = acc_ref[...]
            if epilogue is not None:
                acc = epilogue(acc, e_ref[...] if has_extra else None)
            o_ref[...] = acc.astype(o_ref.dtype)

    in_specs, args = [], []
    for p, (a, b) in enumerate(pairs):
        def kidx(k, p=p):
            return jnp.clip(k - offs[p], 0, steps[p] - 1)
        in_specs.append(pl.BlockSpec((tm, tks[p]),
                                     lambda i, j, k, kidx=kidx, p=p: (i, (j // njo) * steps[p] + kidx(k))))
        in_specs.append(weights[p][2]((to, tks[p]), lambda i, j, k, kidx=kidx: (j % njo, kidx(k))))
        args += [a, weights[p][0]]
    if has_extra:
        in_specs.append(pl.BlockSpec((tm, to), lambda i, j, k: (i, j)))
        args.append(extra)
    if dep is not None:
        in_specs.append(ANY)
        args.append(dep)
    return pl.pallas_call(
        body, name=name,
        out_shape=jax.ShapeDtypeStruct((M, slabs * Ko), out_dtype),
        grid=(M // tm, slabs * njo, nk),
        in_specs=in_specs,
        out_specs=pl.BlockSpec((tm, to), lambda i, j, k: (i, j)),
        scratch_shapes=[pltpu.VMEM((tm, to), F32)],
        compiler_params=_cparams("parallel", "parallel", "arbitrary"),
    )(*args)


def _mm_tn(a, b, *, name, a_fn=None, slabs=1):
    M, K = a.shape
    K //= slabs
    N = b.shape[1] // slabs
    tm = _tile(M, (4096, 2048, 1024, 512, 256))
    tk = _tile(K, (1024, 768, 512, 384, 256, 128))
    tn = _tile(N, (1024, 768, 512, 384, 256, 128))
    nmi = M // tm
    nm = slabs * nmi
    nki, nnj = K // tk, N // tn

    def body(a_ref, b_ref, o_ref, acc_ref):
        m = pl.program_id(2)
        av = a_ref[...]
        if a_fn is not None:
            av = a_fn(av)
        part = _dot_tn(av, b_ref[...])
        if nm == 1:
            o_ref[...] = part.astype(o_ref.dtype)
            return

        @pl.when(m == 0)
        def _():
            acc_ref[...] = part

        @pl.when((m > 0) & (m < nm - 1))
        def _():
            acc_ref[...] += part

        @pl.when(m == nm - 1)
        def _():
            o_ref[...] = (acc_ref[...] + part).astype(o_ref.dtype)

    return pl.pallas_call(
        body, name=name,
        out_shape=jax.ShapeDtypeStruct((K, N), BF16),
        grid=(K // tk, N // tn, nm),
        in_specs=[pl.BlockSpec((tm, tk), lambda i, j, m: (m % nmi, (m // nmi) * nki + i)),
                  pl.BlockSpec((tm, tn), lambda i, j, m: (m % nmi, (m // nmi) * nnj + j))],
        out_specs=pl.BlockSpec((tk, tn), lambda i, j, m: (i, j)),
        scratch_shapes=[pltpu.VMEM((tk, tn), F32)],
        compiler_params=_cparams("parallel", "parallel", "arbitrary"),
    )(a, b)


def _rmsnorm_fwd(h, g, *, name, dep=None, views=()):
    S, D = h.shape
    tm = _tile(S, (512, 256))
    nv = len(views)

    def body(h_ref, g_ref, *rest):
        n_out = 1 + nv
        outs = rest[len(rest) - n_out - (1 if nv else 0):len(rest) - (1 if nv else 0)]
        x = h_ref[...]
        r = lax.rsqrt(jnp.mean(x * x, axis=-1, keepdims=True) + RMS_EPS)
        y = x * r * g_ref[...]
        outs[0][...] = y.astype(BF16)
        if nv:
            scr = rest[-1]
            _to_chunks(scr, y)
            for v_ref, d in zip(outs[1:], views):
                _slabs_from_chunks(v_ref, scr, d, BF16)

    res = pl.pallas_call(
        body, name=name,
        out_shape=(jax.ShapeDtypeStruct((S, D), BF16),)
        + tuple(jax.ShapeDtypeStruct((S // d, d * D), BF16) for d in views),
        grid=(S // tm,),
        in_specs=[pl.BlockSpec((tm, D), lambda i: (i, 0)), pl.BlockSpec((1, D), lambda i: (0, 0))]
        + ([ANY] if dep is not None else []),
        out_specs=(pl.BlockSpec((tm, D), lambda i: (i, 0)),)
        + tuple(pl.BlockSpec((tm // d, d * D), lambda i: (i, 0)) for d in views),
        scratch_shapes=[_chunk_scratch(tm, D)] if nv else [],
        compiler_params=_cparams("parallel"),
    )(h, g, *([dep] if dep is not None else []))
    return res if nv else res[0]


def _rms_bwd_rows(x, g, dy):
    r = lax.rsqrt(jnp.mean(x * x, axis=-1, keepdims=True) + RMS_EPS)
    xh = x * r
    gy = dy * g
    dx = r * (gy - xh * jnp.mean(xh * gy, axis=-1, keepdims=True))
    return dx, dy * xh


def _rms_bwd(x, g, dn, dres, *, name, dn_views=()):
    S, D = x.shape
    tm = _tile(S, (512, 256))
    nv = len(dn_views)

    def body(x_ref, g_ref, dn_ref, dr_ref, *rest):
        v_refs = rest[:nv]
        dx_ref, dxb_ref, dg_ref = rest[nv:nv + 3]
        scr = rest[nv + 3:]
        i = pl.program_id(0)
        dn = dn_ref[...]
        for v_ref, s_ref, (_, d) in zip(v_refs, scr, dn_views):
            _chunks_from_slabs(s_ref, v_ref, d)
            dn = dn + _from_chunks(s_ref)
        dx, dgx = _rms_bwd_rows(x_ref[...], g_ref[...], dn)
        tot = dr_ref[...] + dx
        dx_ref[...] = tot
        dxb_ref[...] = tot.astype(BF16)

        @pl.when(i == 0)
        def _():
            dg_ref[...] = jnp.zeros_like(dg_ref)

        dg_ref[0:1, :] += jnp.sum(dgx, axis=0, keepdims=True)

    row = pl.BlockSpec((tm, D), lambda i: (i, 0))
    return pl.pallas_call(
        body, name=name,
        out_shape=(jax.ShapeDtypeStruct((S, D), F32), jax.ShapeDtypeStruct((S, D), BF16),
                   jax.ShapeDtypeStruct((8, D), F32)),
        grid=(S // tm,),
        in_specs=[row, pl.BlockSpec((1, D), lambda i: (0, 0)), row, row]
        + [pl.BlockSpec((tm // d, d * D), lambda i: (i, 0)) for _, d in dn_views],
        out_specs=(row, row, pl.BlockSpec((8, D), lambda i: (0, 0))),
        scratch_shapes=[_chunk_scratch(tm, D)] * nv,
        compiler_params=_cparams("arbitrary"),
    )(x, g, dn, dres, *[a for a, _ in dn_views])


def _mm_nt_rms_bwd(a, b, x, g, dres, *, name, dep=None):
    M, N = a.shape
    b, (D, _), b_spec = _weight(b)
    tm = _tile(M, (1024, 512, 256))
    tk = _tile(N, (1024, 512, 256, 128))
    nk = N // tk

    def body(a_ref, b_ref, x_ref, g_ref, dr_ref, *rest):
        dx_ref, dxb_ref, dg_ref, acc_ref = rest[-4:]
        i = pl.program_id(0)
        k = pl.program_id(1)
        part = _dot_nt(a_ref[...], b_ref[...])

        @pl.when((i == 0) & (k == 0))
        def _():
            dg_ref[...] = jnp.zeros_like(dg_ref)

        @pl.when(k == 0)
        def _():
            acc_ref[...] = part

        @pl.when((k > 0) & (k < nk - 1))
        def _():
            acc_ref[...] += part

        @pl.when(k == nk - 1)
        def _():
            dn = part if nk == 1 else acc_ref[...] + part
            dx, dgx = _rms_bwd_rows(x_ref[...], g_ref[...], dn)
            tot = dr_ref[...] + dx
            dx_ref[...] = tot
            dxb_ref[...] = tot.astype(BF16)
            dg_ref[0:1, :] += jnp.sum(dgx, axis=0, keepdims=True)

    row = pl.BlockSpec((tm, D), lambda i, k: (i, 0))
    in_specs = [pl.BlockSpec((tm, tk), lambda i, k: (i, k)), b_spec((D, tk), lambda i, k: (0, k)),
                row, pl.BlockSpec((1, D), lambda i, k: (0, 0)), row]
    args = [a, b, x, g, dres]
    if dep is not None:
        in_specs.append(ANY)
        args.append(dep)
    return pl.pallas_call(
        body, name=name,
        out_shape=(jax.ShapeDtypeStruct((M, D), F32), jax.ShapeDtypeStruct((M, D), BF16),
                   jax.ShapeDtypeStruct((8, D), F32)),
        grid=(M // tm, nk),
        in_specs=in_specs,
        out_specs=(row, row, pl.BlockSpec((8, D), lambda i, k: (0, 0))),
        scratch_shapes=[pltpu.VMEM((tm, D), F32)],
        compiler_params=_cparams("arbitrary", "arbitrary"),
    )(*args)


def _loss_bwd(h, target, g, *, name):
    S, D = h.shape
    tm = _tile(S, (512, 256))

    def body(h_ref, t_ref, g_ref, dx_ref, dxb_ref, acc_ref):
        i = pl.program_id(0)
        x = h_ref[...]
        gv = g_ref[...]
        r = lax.rsqrt(jnp.mean(x * x, axis=-1, keepdims=True) + RMS_EPS)
        err = x * r * gv - t_ref[...]
        dx, dgx = _rms_bwd_rows(x, gv, err * (1.0 / D))
        dx_ref[...] = dx
        dxb_ref[...] = dx.astype(BF16)

        @pl.when(i == 0)
        def _():
            acc_ref[...] = jnp.zeros_like(acc_ref)

        acc_ref[0:1, :] += jnp.sum(dgx, axis=0, keepdims=True)
        acc_ref[1:2, :] += jnp.sum(err * err, axis=0, keepdims=True)

    row = pl.BlockSpec((tm, D), lambda i: (i, 0))
    return pl.pallas_call(
        body, name=name,
        out_shape=(jax.ShapeDtypeStruct((S, D), F32), jax.ShapeDtypeStruct((S, D), BF16),
                   jax.ShapeDtypeStruct((8, D), F32)),
        grid=(S // tm,),
        in_specs=[row, row, pl.BlockSpec((1, D), lambda i: (0, 0))],
        out_specs=(row, row, pl.BlockSpec((8, D), lambda i: (0, 0))),
        compiler_params=_cparams("arbitrary"),
    )(h, target, g)


SUBLANES = 8
CONV_ROWS = 64


def _build_shifted(ext_ref, rot_ref, ts, shifts=tuple(range(1, SUBLANES))):
    rows = ts + HALO - SUBLANES
    for j in shifts:
        rot_ref[j, 0:rows, :] = ext_ref[j:j + rows, :]


def _shifted(ext_ref, rot_ref, off, r0, nrows, cs):
    q, j = divmod(off, SUBLANES)
    start = SUBLANES * q + r0
    if j == 0:
        return ext_ref[start:start + nrows, cs]
    return rot_ref[j, start:start + nrows, cs]


def _conv_fwd(uc, conv_w, conv_b, ln_g, ln_b, *, name):
    S, C2 = uc.shape
    C = C2 // 2
    ts = _tile(S, (512, 256))
    per = ts // HALO

    def body(cur_ref, halo_ref, w_ref, b_ref, g_ref, beta_ref, ya_ref, h_ref, ct_ref, ext_ref, rot_ref):
        i = pl.program_id(0)
        hh = halo_ref[:, 0:C] * jax.nn.sigmoid(halo_ref[:, C:C2])
        ext_ref[0:HALO, :] = jnp.where(i == 0, 0.0, hh)
        hc = cur_ref[:, 0:C] * jax.nn.sigmoid(cur_ref[:, C:C2])
        ext_ref[HALO:HALO + ts, :] = hc
        h_ref[...] = hc
        _build_shifted(ext_ref, rot_ref, ts)
        for c0 in range(0, C, LANES):
            cs = slice(c0, c0 + LANES)
            for r0 in range(0, ts, CONV_ROWS):
                acc = jnp.zeros((CONV_ROWS, LANES), F32)
                for k in range(CONV_A_WIDTH):
                    acc = acc + w_ref[k:k + 1, cs] * _shifted(ext_ref, rot_ref, k + 2, r0, CONV_ROWS, cs)
                ct_ref[r0:r0 + CONV_ROWS, cs] = acc + b_ref[:, cs]
        ct = ct_ref[...]
        mu = jnp.mean(ct, axis=-1, keepdims=True)
        xc = ct - mu
        var = jnp.mean(xc * xc, axis=-1, keepdims=True)
        l = xc * lax.rsqrt(var + LN_EPS) * g_ref[...] + beta_ref[...]
        ya_ref[...] = (l * jax.nn.sigmoid(l)).astype(ya_ref.dtype)

    vec = pl.BlockSpec((1, C), lambda i: (0, 0))
    row = pl.BlockSpec((ts, C), lambda i: (i, 0))
    return pl.pallas_call(
        body, name=name,
        out_shape=(jax.ShapeDtypeStruct((S, C), BF16), jax.ShapeDtypeStruct((S, C), F32),
                   jax.ShapeDtypeStruct((S, C), F32)),
        grid=(S // ts,),
        in_specs=[pl.BlockSpec((ts, C2), lambda i: (i, 0)),
                  pl.BlockSpec((HALO, C2), lambda i: (jnp.maximum(i * per - 1, 0), 0)),
                  pl.BlockSpec((CONV_A_WIDTH, C), lambda i: (0, 0)), vec, vec, vec],
        out_specs=(row, row, row),
        scratch_shapes=[pltpu.VMEM((HALO + ts, C), F32), pltpu.VMEM((8, HALO + ts, C), F32)],
        compiler_params=_cparams("parallel"),
    )(uc, uc, conv_w, conv_b, ln_g, ln_b)


CONV_ACC_ROWS = 40


def _conv_bwd_ln(ct, dcat, hglu, ln_g, ln_b, *, name):
    S, C = ct.shape
    CW = dcat.shape[1]
    ts = _tile(S, (512, 256))
    per = ts // HALO

    def body(ct_ref, dcat_ref, hc_ref, hh_ref, g_ref, beta_ref, dc_ref, acc_ref, ext_ref, rot_ref):
        i = pl.program_id(0)
        ct = ct_ref[...]
        gv = g_ref[...]
        mu = jnp.mean(ct, axis=-1, keepdims=True)
        xc = ct - mu
        rstd = lax.rsqrt(jnp.mean(xc * xc, axis=-1, keepdims=True) + LN_EPS)
        xh = xc * rstd
        l = xh * gv + beta_ref[...]
        sg = jax.nn.sigmoid(l)
        dl = dcat_ref[:, 0:C] * (sg * (1.0 + l * (1.0 - sg)))
        dxh = dl * gv
        dc = rstd * (dxh - jnp.mean(dxh, axis=-1, keepdims=True)
                     - xh * jnp.mean(dxh * xh, axis=-1, keepdims=True))
        dc_ref[...] = dc

        @pl.when(i == 0)
        def _():
            acc_ref[...] = jnp.zeros_like(acc_ref)

        acc_ref[32:33, :] += jnp.sum(dc, axis=0, keepdims=True)
        acc_ref[33:34, :] += jnp.sum(dl * xh, axis=0, keepdims=True)
        acc_ref[34:35, :] += jnp.sum(dl, axis=0, keepdims=True)
        ext_ref[0:HALO, :] = jnp.where(i == 0, 0.0, hh_ref[...])
        ext_ref[HALO:HALO + ts, :] = hc_ref[...]
        _build_shifted(ext_ref, rot_ref, ts)
        for c0 in range(0, C, LANES):
            cs = slice(c0, c0 + LANES)
            dcc = dc_ref[:, cs]
            for k in range(CONV_A_WIDTH):
                acc_ref[k:k + 1, cs] += jnp.sum(dcc * _shifted(ext_ref, rot_ref, k + 2, 0, ts, cs),
                                                axis=0, keepdims=True)

    vec = pl.BlockSpec((1, C), lambda i: (0, 0))
    row = pl.BlockSpec((ts, C), lambda i: (i, 0))
    return pl.pallas_call(
        body, name=name,
        out_shape=(jax.ShapeDtypeStruct((S, C), F32), jax.ShapeDtypeStruct((CONV_ACC_ROWS, C), F32)),
        grid=(S // ts,),
        in_specs=[row, pl.BlockSpec((ts, CW), lambda i: (i, 0)), row,
                  pl.BlockSpec((HALO, C), lambda i: (jnp.maximum(i * per - 1, 0), 0)), vec, vec],
        out_specs=(row, pl.BlockSpec((CONV_ACC_ROWS, C), lambda i: (0, 0))),
        scratch_shapes=[pltpu.VMEM((HALO + ts, C), F32), pltpu.VMEM((8, HALO + ts, C), F32)],
        compiler_params=_cparams("arbitrary"),
    )(ct, dcat, hglu, hglu, ln_g, ln_b)


def _conv_bwd_in(dc, uc, conv_w, *, name):
    S, C = dc.shape
    C2 = 2 * C
    ts = _tile(S, (512, 256))
    per = ts // HALO
    nt = S // ts

    def body(dc_ref, dn_ref, uc_ref, w_ref, du_ref, ext_ref, rot_ref):
        i = pl.program_id(0)
        ext_ref[0:ts, :] = dc_ref[...]
        ext_ref[ts:ts + HALO, :] = jnp.where(i == nt - 1, 0.0, dn_ref[...])
        _build_shifted(ext_ref, rot_ref, ts)
        for c0 in range(0, C, LANES):
            cs = slice(c0, c0 + LANES)
            gs = slice(C + c0, C + c0 + LANES)
            for r0 in range(0, ts, CONV_ROWS):
                rs = slice(r0, r0 + CONV_ROWS)
                acc = jnp.zeros((CONV_ROWS, LANES), F32)
                for k in range(CONV_A_WIDTH):
                    acc = acc + w_ref[k:k + 1, cs] * _shifted(ext_ref, rot_ref, 30 - k, r0, CONV_ROWS, cs)
                sg = jax.nn.sigmoid(uc_ref[rs, gs])
                du_ref[rs, cs] = (acc * sg).astype(du_ref.dtype)
                du_ref[rs, gs] = (acc * uc_ref[rs, cs] * sg * (1.0 - sg)).astype(du_ref.dtype)

    return pl.pallas_call(
        body, name=name,
        out_shape=jax.ShapeDtypeStruct((S, C2), BF16),
        grid=(nt,),
        in_specs=[pl.BlockSpec((ts, C), lambda i: (i, 0)),
                  pl.BlockSpec((HALO, C), lambda i: (jnp.minimum((i + 1) * per, S // HALO - 1), 0)),
                  pl.BlockSpec((ts, C2), lambda i: (i, 0)),
                  pl.BlockSpec((CONV_A_WIDTH, C), lambda i: (0, 0))],
        out_specs=pl.BlockSpec((ts, C2), lambda i: (i, 0)),
        scratch_shapes=[pltpu.VMEM((ts + HALO, C), F32), pltpu.VMEM((8, ts + HALO, C), F32)],
        compiler_params=_cparams("parallel"),
    )(dc, dc, uc, conv_w)


SC_SHIFTS_BACK = ((HALO - 2) % SUBLANES, (HALO - 1) % SUBLANES)
SC_SHIFTS_AHEAD = (1, 2)


def _sc_fwd(u3, conv_w, *, name):
    S, W3 = u3.shape
    W = W3 // 3
    ts = _tile(S, (256,))
    per = ts // HALO

    def body(cur_ref, halo_ref, w_ref, y_ref, ext_ref, rot_ref):
        i = pl.program_id(0)
        cvh = halo_ref[:, W:2 * W].astype(F32) * halo_ref[:, 2 * W:W3].astype(F32)
        ext_ref[0:HALO, :] = jnp.where(i == 0, 0.0, cvh)
        ext_ref[HALO:HALO + ts, :] = cur_ref[:, W:2 * W].astype(F32) * cur_ref[:, 2 * W:W3].astype(F32)
        _build_shifted(ext_ref, rot_ref, ts, SC_SHIFTS_BACK)
        for c0 in range(0, W, LANES):
            cs = slice(c0, c0 + LANES)
            for r0 in range(0, ts, CONV_ROWS):
                rs = slice(r0, r0 + CONV_ROWS)
                k = (w_ref[0:1, cs] * _shifted(ext_ref, rot_ref, HALO - 2, r0, CONV_ROWS, cs)
                     + w_ref[1:2, cs] * _shifted(ext_ref, rot_ref, HALO - 1, r0, CONV_ROWS, cs)
                     + w_ref[2:3, cs] * _shifted(ext_ref, rot_ref, HALO, r0, CONV_ROWS, cs))
                y_ref[rs, cs] = (cur_ref[rs, cs].astype(F32) * k).astype(y_ref.dtype)

    return pl.pallas_call(
        body, name=name,
        out_shape=jax.ShapeDtypeStruct((S, W), BF16),
        grid=(S // ts,),
        in_specs=[pl.BlockSpec((ts, W3), lambda i: (i, 0)),
                  pl.BlockSpec((HALO, W3), lambda i: (jnp.maximum(i * per - 1, 0), 0)),
                  pl.BlockSpec((SC_CONV_WIDTH, W), lambda i: (0, 0))],
        out_specs=pl.BlockSpec((ts, W), lambda i: (i, 0)),
        scratch_shapes=[pltpu.VMEM((HALO + ts, W), F32), pltpu.VMEM((8, HALO + ts, W), F32)],
        compiler_params=_cparams("parallel"),
    )(u3, u3, conv_w)


def _sc_bwd(u3, dy, conv_w, *, name):
    S, W3 = u3.shape
    W = W3 // 3
    ts = _tile(S, (256,))
    per = ts // HALO
    nt = S // ts

    def body(cur_ref, prev_ref, next_ref, dy_ref, dyn_ref, w_ref, du_ref, dw_ref, cv_ext, dk_ext, cv_rot, dk_rot):
        i = pl.program_id(0)
        cvh = prev_ref[:, W:2 * W].astype(F32) * prev_ref[:, 2 * W:W3].astype(F32)
        cv_ext[0:HALO, :] = jnp.where(i == 0, 0.0, cvh)
        cv_ext[HALO:HALO + ts, :] = cur_ref[:, W:2 * W].astype(F32) * cur_ref[:, 2 * W:W3].astype(F32)
        dk_ext[0:ts, :] = dy_ref[...] * cur_ref[:, 0:W].astype(F32)
        dk_ext[ts:ts + HALO, :] = jnp.where(i == nt - 1, 0.0, dyn_ref[...] * next_ref[:, 0:W].astype(F32))
        _build_shifted(cv_ext, cv_rot, ts, SC_SHIFTS_BACK)
        _build_shifted(dk_ext, dk_rot, ts, SC_SHIFTS_AHEAD)

        @pl.when(i == 0)
        def _():
            dw_ref[...] = jnp.zeros_like(dw_ref)

        for c0 in range(0, W, LANES):
            cs = slice(c0, c0 + LANES)
            w0, w1, w2 = w_ref[0:1, cs], w_ref[1:2, cs], w_ref[2:3, cs]
            sums = [jnp.zeros((1, LANES), F32)] * SC_CONV_WIDTH
            for r0 in range(0, ts, CONV_ROWS):
                rs = slice(r0, r0 + CONV_ROWS)
                cv2 = _shifted(cv_ext, cv_rot, HALO - 2, r0, CONV_ROWS, cs)
                cv1 = _shifted(cv_ext, cv_rot, HALO - 1, r0, CONV_ROWS, cs)
                cv0 = _shifted(cv_ext, cv_rot, HALO, r0, CONV_ROWS, cs)
                dk = dk_ext[rs, cs]
                dcv = (w2 * dk + w1 * _shifted(dk_ext, dk_rot, 1, r0, CONV_ROWS, cs)
                       + w0 * _shifted(dk_ext, dk_rot, 2, r0, CONV_ROWS, cs))
                du_ref[rs, cs] = (dy_ref[rs, cs] * (w0 * cv2 + w1 * cv1 + w2 * cv0)).astype(du_ref.dtype)
                du_ref[rs, W + c0:W + c0 + LANES] = (
                    dcv * cur_ref[rs, 2 * W + c0:2 * W + c0 + LANES].astype(F32)).astype(du_ref.dtype)
                du_ref[rs, 2 * W + c0:2 * W + c0 + LANES] = (
                    dcv * cur_ref[rs, W + c0:W + c0 + LANES].astype(F32)).astype(du_ref.dtype)
                for t, cvt in enumerate((cv2, cv1, cv0)):
                    sums[t] = sums[t] + jnp.sum(dk * cvt, axis=0, keepdims=True)
            for t in range(SC_CONV_WIDTH):
                dw_ref[t:t + 1, cs] += sums[t]

    nxt = lambda i: (jnp.minimum((i + 1) * per, S // HALO - 1), 0)
    return pl.pallas_call(
        body, name=name,
        out_shape=(jax.ShapeDtypeStruct((S, W3), BF16), jax.ShapeDtypeStruct((8, W), F32)),
        grid=(nt,),
        in_specs=[pl.BlockSpec((ts, W3), lambda i: (i, 0)),
                  pl.BlockSpec((HALO, W3), lambda i: (jnp.maximum(i * per - 1, 0), 0)),
                  pl.BlockSpec((HALO, W3), nxt),
                  pl.BlockSpec((ts, W), lambda i: (i, 0)),
                  pl.BlockSpec((HALO, W), nxt),
                  pl.BlockSpec((SC_CONV_WIDTH, W), lambda i: (0, 0))],
        out_specs=(pl.BlockSpec((ts, W3), lambda i: (i, 0)), pl.BlockSpec((8, W), lambda i: (0, 0))),
        scratch_shapes=[pltpu.VMEM((HALO + ts, W), F32), pltpu.VMEM((ts + HALO, W), F32),
                        pltpu.VMEM((8, HALO + ts, W), F32), pltpu.VMEM((8, ts + HALO, W), F32)],
        compiler_params=_cparams("arbitrary"),
    )(u3, u3, u3, dy, dy, conv_w)


def _t5_causal_bucket(n):
    max_exact = NUM_BUCKETS // 2
    nf = jnp.maximum(n, 1).astype(F32)
    large = max_exact + (jnp.log(nf / max_exact) / math.log(REL_MAX_DISTANCE / max_exact)
                         * (NUM_BUCKETS - max_exact)).astype(jnp.int32)
    return jnp.where(n < max_exact, n, jnp.minimum(large, NUM_BUCKETS - 1))


def _bucket_tables():
    steps = ATTN_BLK
    m = jnp.arange(steps)[:, None] + steps - jnp.arange(2 * steps)[None, :]
    return jnp.stack([_t5_causal_bucket(jnp.clip(m, 0, steps) * dil).astype(F32) for _, dil in DILATED_GROUPS])


def _bias_tables(rel_bias, buckets, *, name):
    steps = ATTN_BLK

    def body(tab_ref, bk_ref, o_ref):
        g = pl.program_id(0)
        bk = bk_ref[0]
        a_idx = lax.broadcasted_iota(jnp.int32, (steps, 2 * steps), 0)
        c_idx = lax.broadcasted_iota(jnp.int32, (steps, 2 * steps), 1)
        m = a_idx + steps - c_idx
        band = (m >= 0) & (m <= steps)
        band_first = band & (c_idx >= steps)
        for h in range(HEADS_PER_GROUP):
            bias = jnp.zeros((steps, 2 * steps), F32)
            for b in range(NUM_BUCKETS):
                bias = jnp.where(bk == float(b), tab_ref[b, g * HEADS_PER_GROUP + h], bias)
            o_ref[0, 0, h] = jnp.where(band_first, bias, NEG_INF)
            o_ref[0, 1, h] = jnp.where(band, bias, NEG_INF)

    return pl.pallas_call(
        body, name=name,
        out_shape=jax.ShapeDtypeStruct((N_GROUPS, 2, HEADS_PER_GROUP, steps, 2 * steps), F32),
        grid=(N_GROUPS,),
        in_specs=[pl.BlockSpec(memory_space=pltpu.SMEM),
                  pl.BlockSpec((1, steps, 2 * steps), lambda g: (g, 0, 0))],
        out_specs=pl.BlockSpec((1, 2, HEADS_PER_GROUP, steps, 2 * steps), lambda g: (g, 0, 0, 0, 0)),
        compiler_params=_cparams("parallel"),
    )(rel_bias, buckets)


def _lane_is_low():
    return lax.broadcasted_iota(jnp.int32, (1, LANES), 1) < HEAD_DIM


def _stack_heads(x2, low):
    zero = jnp.zeros_like(x2)
    return jnp.concatenate([jnp.where(low, x2, zero), jnp.where(low, zero, x2)], axis=0)


ATTN_FWD_BLOCKS = 8


def _attn_fwd(uv, bias, g, d, *, name):
    rows = uv.shape[0]
    nsub = min(ATTN_FWD_BLOCKS, rows // ATTN_BLK)
    step_rows = nsub * ATTN_BLK
    nqb = GROUP_QKV // ATTN_OUT

    def body(q_ref, kc_ref, kp_ref, vc_ref, vp_ref, bias_ref, o_ref, l_ref):
        n = pl.program_id(1)
        low = _lane_is_low()
        slabs = [slice(hp * LANES, (hp + 1) * LANES) for hp in range(HEADS_PER_GROUP // 2)]
        for sub in range(nsub):
            qr = slice(sub * ATTN_BLK, (sub + 1) * ATTN_BLK)
            sel = jnp.minimum(n, 1) if sub == 0 else 1

            def with_prev(cur_ref, prev_ref, sl, sub=sub, qr=qr):
                prev = prev_ref[:, sl] if sub == 0 else cur_ref[(sub - 1) * ATTN_BLK:sub * ATTN_BLK, sl]
                return jnp.concatenate([prev, cur_ref[qr, sl]], axis=0)

            scores = [_dot_nt(_stack_heads(q_ref[qr, sl] * (HEAD_DIM ** -0.5), low), with_prev(kc_ref, kp_ref, sl))
                      for sl in slabs]
            probs, dens_all, lses_all = [], [], []
            for hp, s in enumerate(scores):
                ps, dens, lses = [], [], []
                for hh in range(2):
                    logits = s[hh * ATTN_BLK:(hh + 1) * ATTN_BLK] + bias_ref[sel, 2 * hp + hh]
                    mx = jnp.max(logits, axis=-1, keepdims=True)
                    p = jnp.exp(logits - mx)
                    den = jnp.sum(p, axis=-1, keepdims=True)
                    ps.append(p.astype(BF16))
                    dens.append(den)
                    lses.append(jnp.broadcast_to(mx + jnp.log(den), (ATTN_BLK, LANES)))
                probs.append(jnp.concatenate(ps, axis=0))
                dens_all.append(dens)
                lses_all.append(lses)
            for hp, sl in enumerate(slabs):
                pv = jnp.dot(probs[hp], with_prev(vc_ref, vp_ref, sl), preferred_element_type=F32)
                dens, lses = dens_all[hp], lses_all[hp]
                o_ref[qr, sl] = jnp.where(low, pv[0:ATTN_BLK] / dens[0], pv[ATTN_BLK:2 * ATTN_BLK] / dens[1])
                l_ref[qr, sl] = jnp.where(low, lses[0], lses[1])

    def cur(t):
        return pl.BlockSpec((step_rows, ATTN_OUT), lambda r, n: (n, r * nqb + t))

    def prev(t):
        return pl.BlockSpec((ATTN_BLK, ATTN_OUT), lambda r, n: (jnp.maximum(n * nsub - 1, 0), r * nqb + t))

    out_spec = pl.BlockSpec((step_rows, ATTN_OUT), lambda r, n: (n, r))
    return pl.pallas_call(
        body, name=name,
        out_shape=(jax.ShapeDtypeStruct((rows, d * ATTN_OUT), F32),) * 2,
        grid=(d, rows // step_rows),
        in_specs=[cur(0), cur(1), prev(1), cur(2), prev(2),
                  pl.BlockSpec((None, 2, HEADS_PER_GROUP, ATTN_BLK, 2 * ATTN_BLK), lambda r, n: (g, 0, 0, 0, 0))],
        out_specs=(out_spec, out_spec),
        compiler_params=_cparams("parallel", "parallel"),
    )(uv, uv, uv, uv, uv, bias)


def _chunk_scratch(n, width):
    return pltpu.VMEM((width // LANES, n, LANES), F32)


def _to_chunks(scr, val):
    for c in range(scr.shape[0]):
        scr[c] = val[:, c * LANES:(c + 1) * LANES]


def _from_chunks(scr):
    return jnp.concatenate([scr[c] for c in range(scr.shape[0])], axis=1)


def _slabs_from_chunks(dst_ref, scr, d, dtype):
    nc, n, _ = scr.shape
    for r in range(d):
        for c in range(nc):
            col = r * nc * LANES + c * LANES
            dst_ref[:, col:col + LANES] = scr[c, pl.ds(r, n // d, stride=d), :].astype(dtype)


def _chunks_from_slabs(scr, src_ref, d):
    nc, n, _ = scr.shape
    for r in range(d):
        for c in range(nc):
            col = r * nc * LANES + c * LANES
            scr[c, pl.ds(r, n // d, stride=d), :] = src_ref[:, col:col + LANES]


def _attn_merge(outs, lses, ya, *, name):
    S, C = ya.shape
    tm = _tile(S, (512, 256))
    dils = [dil for _, dil in DILATED_GROUPS]

    def body(o0, o1, o2, l0, l1, l2, ya_ref, cat_ref, out_ref, lse_ref, so1, so2, sl1, sl2):
        _chunks_from_slabs(so1, o1, dils[1])
        _chunks_from_slabs(so2, o2, dils[2])
        _chunks_from_slabs(sl1, l1, dils[1])
        _chunks_from_slabs(sl2, l2, dils[2])
        a0, a1, a2 = l0[...], _from_chunks(sl1), _from_chunks(sl2)
        m = jnp.maximum(jnp.maximum(a0, a1), a2)
        e0, e1, e2 = jnp.exp(a0 - m), jnp.exp(a1 - m), jnp.exp(a2 - m)
        den = e0 + e1 + e2
        out = (e0 * o0[...] + e1 * _from_chunks(so1) + e2 * _from_chunks(so2)) / den
        out_ref[...] = out
        lse_ref[...] = m + jnp.log(den)
        cat_ref[:, 0:C] = ya_ref[...]
        cat_ref[:, C:C + ATTN_OUT] = out.astype(cat_ref.dtype)

    blk = pl.BlockSpec((tm, ATTN_OUT), lambda i: (i, 0))
    vblk = [pl.BlockSpec((tm // d, d * ATTN_OUT), lambda i: (i, 0)) for d in dils]
    assert dils[0] == 1
    return pl.pallas_call(
        body, name=name,
        out_shape=(jax.ShapeDtypeStruct((S, C + ATTN_OUT), BF16), jax.ShapeDtypeStruct((S, ATTN_OUT), F32),
                   jax.ShapeDtypeStruct((S, ATTN_OUT), F32)),
        grid=(S // tm,),
        in_specs=vblk + vblk + [pl.BlockSpec((tm, C), lambda i: (i, 0))],
        out_specs=(pl.BlockSpec((tm, C + ATTN_OUT), lambda i: (i, 0)), blk, blk),
        scratch_shapes=[_chunk_scratch(tm, ATTN_OUT)] * 4,
        compiler_params=_cparams("parallel"),
    )(*outs, *lses, ya)


def _attn_prep(dcat, outf, lse, *, name):
    S, CW = dcat.shape
    C = CW - ATTN_OUT
    tm = _tile(S, (512, 256))
    dils = [dil for _, dil in DILATED_GROUPS]
    assert dils[0] == 1
    ones = np.kron(np.eye(HEADS_PER_GROUP, dtype=np.float32), np.ones((HEAD_DIM, HEAD_DIM), np.float32))

    nviews = 3 * (len(dils) - 1)

    def body(dcat_ref, out_ref, l_ref, ones_ref, dyb_ref, dl_ref, *rest):
        views = rest[:nviews]
        s_dyb, s_dl, s_l = rest[nviews:]
        dyb = dcat_ref[:, C:CW]
        dyb_ref[...] = dyb.astype(BF16)
        prod = dyb * out_ref[...]
        ov = ones_ref[...]
        hi, mid, lo = _split_bf16(prod)
        delta = (jnp.dot(hi, ov, preferred_element_type=F32)
                 + jnp.dot(mid, ov, preferred_element_type=F32)
                 + jnp.dot(lo, ov, preferred_element_type=F32))
        dl_ref[...] = delta
        _to_chunks(s_dyb, dyb)
        _to_chunks(s_dl, delta)
        _to_chunks(s_l, l_ref[...])
        for gi, d in enumerate(dils[1:]):
            dyb_v, dl_v, l_v = views[3 * gi:3 * gi + 3]
            _slabs_from_chunks(dyb_v, s_dyb, d, BF16)
            _slabs_from_chunks(dl_v, s_dl, d, F32)
            _slabs_from_chunks(l_v, s_l, d, F32)

    blk = pl.BlockSpec((tm, ATTN_OUT), lambda i: (i, 0))
    view_shapes, view_specs = [], []
    for d in dils[1:]:
        for dt in (BF16, F32, F32):
            view_shapes.append(jax.ShapeDtypeStruct((S // d, d * ATTN_OUT), dt))
            view_specs.append(pl.BlockSpec((tm // d, d * ATTN_OUT), lambda i: (i, 0)))
    res = pl.pallas_call(
        body, name=name,
        out_shape=(jax.ShapeDtypeStruct((S, ATTN_OUT), BF16), jax.ShapeDtypeStruct((S, ATTN_OUT), F32),
                   *view_shapes),
        grid=(S // tm,),
        in_specs=[pl.BlockSpec((tm, CW), lambda i: (i, 0)), blk, blk,
                  pl.BlockSpec((ATTN_OUT, ATTN_OUT), lambda i: (0, 0))],
        out_specs=(blk, blk, *view_specs),
        scratch_shapes=[_chunk_scratch(tm, ATTN_OUT)] * 3,
        compiler_params=_cparams("parallel"),
    )(dcat, outf, lse, jnp.asarray(ones, BF16))
    return [(res[0], res[1], lse)] + [tuple(res[2 + 3 * gi:5 + 3 * gi]) for gi in range(len(dils) - 1)]


ATTN_BWD_RESIDUES = 4


def _attn_bwd(uv, dov, lv, dv_, bias, g, d, *, name):
    rows = uv.shape[0]
    nb = rows // ATTN_BLK
    nres = min(d, ATTN_BWD_RESIDUES)
    steps = (d // nres) * nb
    scale = HEAD_DIM ** -0.5
    Q0, K0, V0 = 0, ATTN_OUT, 2 * ATTN_OUT

    def body(cur_ref, prev_ref, do_ref, l_ref, dl_ref, bias_ref, out_ref, db_ref, dq_s, dk_s, dv_s):
        t = pl.program_id(0)
        n = t % nb
        low = _lane_is_low()

        @pl.when(t == 0)
        def _():
            db_ref[...] = jnp.zeros_like(db_ref)
            dq_s[...] = jnp.zeros_like(dq_s)
            dk_s[...] = jnp.zeros_like(dk_s)
            dv_s[...] = jnp.zeros_like(dv_s)

        @pl.when(t < steps)
        def _():
            sel = jnp.minimum(n, 1)
            lane = lax.broadcasted_iota(jnp.int32, (1, LANES), 1)
            nk2 = 2 * ATTN_BLK
            for rr in range(nres):
                u0 = rr * GROUP_QKV
                o0 = rr * ATTN_OUT
                slabs = [hp * LANES for hp in range(HEADS_PER_GROUP // 2)]

                def cols(base, c0):
                    return slice(base + c0, base + c0 + LANES)

                keys = [jnp.concatenate([prev_ref[:, cols(u0 + K0, c0)], cur_ref[:, cols(u0 + K0, c0)]], axis=0)
                        for c0 in slabs]
                scores = [_dot_nt(_stack_heads(cur_ref[:, cols(u0 + Q0, c0)] * scale, low), keys[hp])
                          for hp, c0 in enumerate(slabs)]
                dps = [_dot_nt(_stack_heads(do_ref[:, cols(o0, c0)], low),
                               jnp.concatenate([prev_ref[:, cols(u0 + V0, c0)], cur_ref[:, cols(u0 + V0, c0)]],
                                               axis=0)) for c0 in slabs]
                stacked = []
                for hp, c0 in enumerate(slabs):
                    lse2 = l_ref[:, cols(o0, c0)]
                    dl2 = dl_ref[:, cols(o0, c0)]
                    pbs, dsbs = [], []
                    for hh in range(2):
                        rws = slice(hh * ATTN_BLK, (hh + 1) * ATTN_BLK)
                        one = lane == hh * HEAD_DIM
                        lse_col = jnp.sum(jnp.where(one, lse2, 0.0), axis=-1, keepdims=True)
                        dl_col = jnp.sum(jnp.where(one, dl2, 0.0), axis=-1, keepdims=True)
                        p = jnp.exp(scores[hp][rws] + bias_ref[sel, 2 * hp + hh] - lse_col)
                        ds = p * (dps[hp][rws] - dl_col)
                        db_ref[2 * hp + hh] += ds
                        pbs.append(p.astype(BF16))
                        dsbs.append((ds * scale).astype(BF16))
                    stacked.append((jnp.concatenate(dsbs, axis=0), jnp.concatenate(dsbs, axis=1),
                                    jnp.concatenate(pbs, axis=1)))
                for hp, c0 in enumerate(slabs):
                    ds_rows, ds_cols, p_cols = stacked[hp]
                    dq = jnp.dot(ds_rows, keys[hp], preferred_element_type=F32)
                    dk = _dot_tn(ds_cols, cur_ref[:, cols(u0 + Q0, c0)])
                    dv = _dot_tn(p_cols, do_ref[:, cols(o0, c0)])
                    dq2 = jnp.where(low, dq[0:ATTN_BLK], dq[ATTN_BLK:nk2])
                    dk2 = jnp.where(low, dk[0:nk2], dk[nk2:2 * nk2])
                    dv2 = jnp.where(low, dv[0:nk2], dv[nk2:2 * nk2])
                    sl = cols(o0, c0)
                    out_ref[:, cols(u0 + Q0, c0)] = dq_s[:, sl].astype(out_ref.dtype)
                    dq_s[:, sl] = dq2
                    out_ref[:, cols(u0 + K0, c0)] = (dk_s[:, sl] + dk2[0:ATTN_BLK]).astype(out_ref.dtype)
                    dk_s[:, sl] = dk2[ATTN_BLK:2 * ATTN_BLK]
                    out_ref[:, cols(u0 + V0, c0)] = (dv_s[:, sl] + dv2[0:ATTN_BLK]).astype(out_ref.dtype)
                    dv_s[:, sl] = dv2[ATTN_BLK:2 * ATTN_BLK]

        @pl.when(t == steps)
        def _():
            for rr in range(nres):
                u0, o0 = rr * GROUP_QKV, rr * ATTN_OUT
                out_ref[:, u0 + Q0:u0 + Q0 + ATTN_OUT] = dq_s[:, o0:o0 + ATTN_OUT].astype(out_ref.dtype)
                out_ref[:, u0 + K0:u0 + K0 + ATTN_OUT] = dk_s[:, o0:o0 + ATTN_OUT].astype(out_ref.dtype)
                out_ref[:, u0 + V0:u0 + V0 + ATTN_OUT] = dv_s[:, o0:o0 + ATTN_OUT].astype(out_ref.dtype)

    def blocks(width, lag=0, back=0):
        def index_map(t):
            tt = jnp.maximum(jnp.minimum(t, steps - 1) - lag, 0)
            return (jnp.maximum(tt % nb - back, 0), tt // nb)
        return pl.BlockSpec((ATTN_BLK, nres * width), index_map)

    return pl.pallas_call(
        body, name=name,
        out_shape=(jax.ShapeDtypeStruct((rows, d * GROUP_QKV), BF16),
                   jax.ShapeDtypeStruct((HEADS_PER_GROUP, ATTN_BLK, 2 * ATTN_BLK), F32)),
        grid=(steps + 1,),
        in_specs=[blocks(GROUP_QKV), blocks(GROUP_QKV, back=1), blocks(ATTN_OUT), blocks(ATTN_OUT), blocks(ATTN_OUT),
                  pl.BlockSpec((None, 2, HEADS_PER_GROUP, ATTN_BLK, 2 * ATTN_BLK), lambda t: (g, 0, 0, 0, 0))],
        out_specs=(pl.BlockSpec((ATTN_BLK, nres * GROUP_QKV),
                                lambda t: (jnp.maximum(t - 1, 0) % nb, jnp.maximum(t - 1, 0) // nb)),
                   pl.BlockSpec((HEADS_PER_GROUP, ATTN_BLK, 2 * ATTN_BLK), lambda t: (0, 0, 0))),
        scratch_shapes=[pltpu.VMEM((ATTN_BLK, nres * ATTN_OUT), F32)] * 3,
        compiler_params=_cparams("arbitrary"),
    )(uv, uv, dov, lv, dv_, bias)


def _split_bf16(x):
    hi = x.astype(BF16)
    r1 = x - hi.astype(F32)
    mid = r1.astype(BF16)
    lo = (r1 - mid.astype(F32)).astype(BF16)
    return hi, mid, lo


RELBIAS_CHUNK = 4096


def _relbias_reduce(dbs, buckets, *, name):
    flat = ATTN_BLK * 2 * ATTN_BLK
    dbf = jnp.stack([db.reshape(HEADS_PER_GROUP, flat) for db in dbs])
    bkf = buckets.reshape(N_GROUPS, 1, flat)

    def body(db_ref, bk_ref, o_ref):
        c = pl.program_id(1)
        rows = lax.broadcasted_iota(jnp.int32, (LANES, RELBIAS_CHUNK), 0).astype(F32)
        onehot = jnp.where(rows == bk_ref[0], 1.0, 0.0).astype(BF16)
        hi, mid, lo = _split_bf16(db_ref[0])
        part = _dot_nt(hi, onehot) + _dot_nt(mid, onehot) + _dot_nt(lo, onehot)

        @pl.when(c == 0)
        def _():
            o_ref[0] = part

        @pl.when(c > 0)
        def _():
            o_ref[0] += part

    return pl.pallas_call(
        body, name=name,
        out_shape=jax.ShapeDtypeStruct((N_GROUPS, HEADS_PER_GROUP, LANES), F32),
        grid=(N_GROUPS, flat // RELBIAS_CHUNK),
        in_specs=[pl.BlockSpec((1, HEADS_PER_GROUP, RELBIAS_CHUNK), lambda g, c: (g, 0, c)),
                  pl.BlockSpec((1, 1, RELBIAS_CHUNK), lambda g, c: (g, 0, c))],
        out_specs=pl.BlockSpec((1, HEADS_PER_GROUP, LANES), lambda g, c: (g, 0, 0)),
        compiler_params=_cparams("parallel", "arbitrary"),
    )(dbf, bkf)


def _my_position():
    x, y, c = lax.axis_index("x"), lax.axis_index("y"), lax.axis_index("c")
    return x, y, c


def _linear(pos):
    return 4 * pos[0] + 2 * pos[1] + pos[2]


def _peer(pos, k):
    x, y, c = pos
    return ((1 - x) if k & 4 else x, (1 - y) if k & 2 else y, (1 - c) if k & 1 else c)


HBM_SPEC = pl.BlockSpec(memory_space=pltpu.HBM)
SEM_SPEC = pl.BlockSpec(memory_space=pltpu.SEMAPHORE)
DATAFLOW = pltpu.SideEffectType.DATAFLOW_SIDE_EFFECTING


def _exchange_copies(src, land, sems, send_window, recv_window, with_arrivals):
    send_sems, recv_sems, local_sems = sems
    T = len(src)
    me = _my_position()
    me_lin = _linear(me)
    local = [pltpu.make_async_copy(send_window(t, src[t], me_lin), recv_window(t, land[t], me_lin),
                                   local_sems.at[t]) for t in range(T)]
    sends, arrivals = [], []
    for t in range(T):
        for k in range(1, N_DEV):
            peer = _peer(me, k)
            peer_lin = _linear(peer)
            sem = t * (N_DEV - 1) + k - 1
            sends.append(pltpu.make_async_remote_copy(
                src_ref=send_window(t, src[t], peer_lin), dst_ref=recv_window(t, land[t], me_lin),
                send_sem=send_sems.at[sem], recv_sem=recv_sems.at[sem],
                device_id=peer, device_id_type=MESH))
            if with_arrivals:
                arrivals.append(pltpu.make_async_remote_copy(
                    src_ref=send_window(t, src[t], me_lin), dst_ref=recv_window(t, land[t], peer_lin),
                    send_sem=send_sems.at[sem], recv_sem=recv_sems.at[sem],
                    device_id=peer, device_id_type=MESH))
    return local, sends, arrivals


def _exchange_start(srcs, land_shapes, send_window, recv_window, *, name, dep=None):
    T = len(srcs)
    n_in = 2 * T + (1 if dep is not None else 0)

    def body(*refs):
        src = refs[:T]
        land = refs[T:2 * T]
        sems = refs[n_in:n_in + 3]
        token = refs[-1]
        local, sends, _ = _exchange_copies(src, land, sems, send_window, recv_window, False)
        for cp in local + sends:
            cp.start()
        token[...] = jnp.zeros_like(token)

    lands = [lax.empty(ls.shape, ls.dtype) for ls in land_shapes]
    operands = [pltpu.with_memory_space_constraint(a, pltpu.HBM) for a in list(srcs) + lands]
    outs = pl.pallas_call(
        body, name=name,
        out_shape=(pltpu.SemaphoreType.DMA((T * (N_DEV - 1),)), pltpu.SemaphoreType.DMA((T * (N_DEV - 1),)),
                   pltpu.SemaphoreType.DMA((T,)),
                   *[pltpu.HBM(a.shape, a.dtype) for a in operands],
                   jax.ShapeDtypeStruct((8, LANES), F32)),
        in_specs=[HBM_SPEC] * (2 * T) + ([ANY] if dep is not None else []),
        out_specs=(SEM_SPEC,) * 3 + (HBM_SPEC,) * (2 * T) + (VMEM_SPEC,),
        input_output_aliases={i: 3 + i for i in range(2 * T)},
        compiler_params=pltpu.CompilerParams(has_side_effects=DATAFLOW),
    )(*operands, *([dep] if dep is not None else []))
    return outs[:3], outs[3:3 + T], outs[3 + T:3 + 2 * T], outs[-1]


def _exchange_wait(started, after, send_window, recv_window, *, name):
    sems, srcs, lands, _ = started
    T = len(srcs)

    def body(*refs):
        src = refs[:T]
        land = refs[T:2 * T]
        sem_refs = refs[2 * T:2 * T + 3]
        local, sends, arrivals = _exchange_copies(src, land, sem_refs, send_window, recv_window, True)
        for cp in arrivals:
            cp.wait_recv()
        for cp in sends:
            cp.wait_send()
        for cp in local:
            cp.wait()

    outs = pl.pallas_call(
        body, name=name,
        out_shape=tuple(pltpu.HBM(a.shape, a.dtype) for a in list(srcs) + list(lands)),
        in_specs=[HBM_SPEC] * (2 * T) + [SEM_SPEC] * 3 + [ANY],
        out_specs=(HBM_SPEC,) * (2 * T),
        input_output_aliases={i: i for i in range(2 * T)},
        compiler_params=pltpu.CompilerParams(has_side_effects=DATAFLOW),
    )(*srcs, *lands, *sems, after)
    return outs[T:]


def _shard_window(kind, width):
    def win(ref, lin):
        if kind == "slot":
            return ref.at[lin]
        if kind == "col":
            return ref.at[:, pl.ds(pl.multiple_of(lin * width, LANES), width)]
        if kind == "row":
            return ref.at[pl.ds(pl.multiple_of(lin * width, 8), width), :]
        if kind == "lcol":
            return ref.at[:, :, pl.ds(pl.multiple_of(lin * width, LANES), width)]
        if kind == "lrow":
            return ref.at[:, pl.ds(pl.multiple_of(lin * width, 8), width), :]
        raise ValueError(kind)
    return win


def _shard_windows(kinds, shard_shapes):
    return [_shard_window(k, (ss[-1] if k in ("col", "lcol") else ss[-2])) for k, ss in zip(kinds, shard_shapes)]


def _allgather_start(shards, kinds, full_shapes, *, name, dep=None):
    wins = _shard_windows(kinds, [s.shape for s in shards])
    send_window = lambda t, ref, lin: ref
    recv_window = lambda t, ref, lin: wins[t](ref, lin)
    started = _exchange_start(shards, [jax.ShapeDtypeStruct(fs, s.dtype) for fs, s in zip(full_shapes, shards)],
                              send_window, recv_window, name=name + "_start", dep=dep)
    return started, lambda after: _exchange_wait(started, after, send_window, recv_window, name=name + "_wait")


def _scatter_start(fulls, kinds, shard_shapes, *, name):
    wins = _shard_windows(kinds, shard_shapes)
    send_window = lambda t, ref, lin: wins[t](ref, lin)
    recv_window = lambda t, ref, lin: ref.at[lin]
    started = _exchange_start(
        fulls, [jax.ShapeDtypeStruct((N_DEV,) + tuple(ss), f.dtype) for ss, f in zip(shard_shapes, fulls)],
        send_window, recv_window, name=name + "_start")
    return started, lambda after: _exchange_wait(started, after, send_window, recv_window, name=name + "_wait")


def _small_allreduce(pack, *, name):
    R = pack.shape[0]

    def body(p_ref, o_ref, buf, send_sems, recv_sems):
        me = _my_position()
        me_lin = _linear(me)
        buf[me_lin] = p_ref[...]
        sends = []
        for k in range(1, N_DEV):
            peer = _peer(me, k)
            cp = pltpu.make_async_remote_copy(
                src_ref=p_ref, dst_ref=buf.at[me_lin],
                send_sem=send_sems.at[k - 1], recv_sem=recv_sems.at[k - 1],
                device_id=peer, device_id_type=MESH)
            cp.start()
            sends.append(cp)
        for k in range(1, N_DEV):
            peer = _peer(me, k)
            pltpu.make_async_remote_copy(
                src_ref=p_ref, dst_ref=buf.at[_linear(peer)],
                send_sem=send_sems.at[k - 1], recv_sem=recv_sems.at[k - 1],
                device_id=peer, device_id_type=MESH).wait_recv()
        for cp in sends:
            cp.wait_send()
        acc = buf[0]
        for s in range(1, N_DEV):
            acc = acc + buf[s]
        o_ref[...] = acc

    return pl.pallas_call(
        body, name=name, out_shape=jax.ShapeDtypeStruct((R, LANES), F32),
        in_specs=[VMEM_SPEC], out_specs=VMEM_SPEC,
        scratch_shapes=[pltpu.VMEM((N_DEV, R, LANES), F32), pltpu.SemaphoreType.DMA((N_DEV - 1,)),
                        pltpu.SemaphoreType.DMA((N_DEV - 1,))],
        compiler_params=pltpu.CompilerParams(has_side_effects=True, vmem_limit_bytes=VMEM_LIMIT),
    )(pack)


def _adamw_math(w, g, m, v):
    m = ADAM_B1 * m + (1.0 - ADAM_B1) * g
    v = ADAM_B2 * v + (1.0 - ADAM_B2) * jnp.square(g)
    m_hat = m / (1.0 - ADAM_B1 ** ADAM_STEP)
    v_hat = v / (1.0 - ADAM_B2 ** ADAM_STEP)
    delta = -ADAM_LR * (m_hat / (jnp.sqrt(v_hat) + ADAM_EPS) + ADAM_WD * w)
    return delta, m, v


def _adamw_from_partials(parts, w, m, v, *, name):
    R, C = w.shape
    rl = parts[0].shape[1]
    assert all(p.shape == (N_DEV, rl, C) for p in parts) and rl * len(parts) == R
    tr = _tile(rl, (256, 128, 64, 32, 16))
    per = rl // tr
    L = len(parts)

    def body(*refs):
        p_refs = refs[:L]
        w_ref, m_ref, v_ref, g_ref, d_ref, nm_ref, nv_ref = refs[L:]
        i = pl.program_id(0)
        for l in range(L):
            @pl.when((i >= l * per) & (i < (l + 1) * per))
            def _(l=l):
                p_ref = p_refs[l]
                g = p_ref[0].astype(F32)
                for s in range(1, N_DEV):
                    g = g + p_ref[s].astype(F32)
                d, nm, nv = _adamw_math(w_ref[...], g, m_ref[...], v_ref[...])
                g_ref[...] = g
                d_ref[...] = d
                nm_ref[...] = nm
                nv_ref[...] = nv

    blk = pl.BlockSpec((tr, C), lambda i: (i, 0))
    part_specs = [pl.BlockSpec((N_DEV, tr, C), lambda i, l=l: (0, jnp.clip(i - l * per, 0, per - 1), 0))
                  for l in range(L)]
    return pl.pallas_call(
        body, name=name,
        out_shape=(jax.ShapeDtypeStruct((R, C), F32),) * 4,
        grid=(R // tr,),
        in_specs=part_specs + [blk, blk, blk],
        out_specs=(blk,) * 4,
        compiler_params=_cparams("parallel"),
    )(*parts, w, m, v)


def _adamw_small(g, w, m, v, *, name):
    def body(g_ref, w_ref, m_ref, v_ref, d_ref, nm_ref, nv_ref):
        d, nm, nv = _adamw_math(w_ref[...], g_ref[...], m_ref[...], v_ref[...])
        d_ref[...] = d
        nm_ref[...] = nm
        nv_ref[...] = nv

    return pl.pallas_call(
        body, name=name,
        out_shape=(jax.ShapeDtypeStruct(g.shape, F32),) * 3,
        in_specs=[VMEM_SPEC] * 4, out_specs=(VMEM_SPEC,) * 3,
    )(g, w, m, v)


def _pack_rows(pieces):
    flat = jnp.concatenate([p.reshape(-1) for p in pieces])
    n = flat.shape[0]
    padded = -(-n // (8 * LANES)) * (8 * LANES)
    return jnp.pad(flat, (0, padded - n)).reshape(padded // LANES, LANES)


def _unpack_rows(pack, shapes):
    flat = pack.reshape(-1)
    out, pos = [], 0
    for s in shapes:
        n = int(np.prod(s))
        out.append(flat[pos:pos + n].reshape(s))
        pos += n
    return out


def kernel(x, rel_bias, ab_norm, ab_w_in, ab_conv_w, ab_conv_b, ab_ln_g, ab_ln_b, ab_w_out, sc_norm, sc_w_in, sc_conv_w, sc_w_out, mlp_norm, mlp_w_up, mlp_w_down, final_norm, loss_target, m_rel_bias, m_ab_norm, m_ab_w_in, m_ab_conv_w, m_ab_conv_b, m_ab_ln_g, m_ab_ln_b, m_ab_w_out, m_sc_norm, m_sc_w_in, m_sc_conv_w, m_sc_w_out, m_mlp_norm, m_mlp_w_up, m_mlp_w_down, m_final_norm, v_rel_bias, v_ab_norm, v_ab_w_in, v_ab_conv_w, v_ab_conv_b, v_ab_ln_g, v_ab_ln_b, v_ab_w_out, v_sc_norm, v_sc_w_in, v_sc_conv_w, v_sc_w_out, v_mlp_norm, v_mlp_w_up, v_mlp_w_down, v_final_norm):
    S, D = x.shape[1], x.shape[2]
    CA = ab_conv_b.shape[1]
    C2 = 2 * CA
    AB_IN = C2 + ATTN_IN
    me_lin = _linear(_my_position())
    xs = x.reshape(S, D)
    tgt = loss_target.reshape(S, D)

    cw_sh = ab_conv_w.shape[2]
    scn_sh = sc_norm.shape[1]
    scw_sh = sc_conv_w.shape[2]
    small_sh_shapes = [(CONV_A_WIDTH, cw_sh), (scn_sh,), (SC_CONV_WIDTH, scw_sh)]
    small_pack = _pack_rows([ab_conv_w[0], sc_norm[0], sc_conv_w[0]])
    w_in_sh = ab_w_in[0].astype(BF16)
    ag_ab, wait_ab = _allgather_start(
        [small_pack, w_in_sh], ["slot", "slot"],
        [(N_DEV,) + small_pack.shape, (N_DEV,) + w_in_sh.shape], name="allgather_ab")
    ag_mlp, wait_mlp = _allgather_start(
        [ab_w_out[0].astype(BF16), mlp_w_up.astype(BF16), mlp_w_down.astype(BF16)], ["row", "lcol", "lrow"],
        [(N_DEV * ab_w_out.shape[1], D), (2, D, N_DEV * mlp_w_up.shape[2]), (2, N_DEV * mlp_w_down.shape[1], D)],
        name="allgather_mlp", dep=ag_ab[3])
    ag_sc, wait_sc = _allgather_start(
        [sc_w_in[0].astype(BF16), sc_w_out[0].astype(BF16)], ["col", "row"],
        [(D, N_DEV * sc_w_in.shape[2]), (N_DEV * sc_w_out.shape[1], D)], name="allgather_sc",
        dep=ag_mlp[3])

    buckets = _bucket_tables()
    biases = _bias_tables(rel_bias, buckets, name="bias_tables")

    dils = [dil for _, dil in DILATED_GROUPS]
    n0_all = _rmsnorm_fwd(xs, ab_norm, name="norm_ab", dep=ag_sc[3], views=dils[1:])
    n0 = n0_all[0]
    small_params, w_in_g = wait_ab(n0)
    per_dev = [_unpack_rows(small_params[s], small_sh_shapes) for s in range(N_DEV)]
    conv_w_full = jnp.concatenate([p[0] for p in per_dev], axis=1)
    sc_norm_full = jnp.concatenate([p[1] for p in per_dev], axis=0)[None]
    sc_conv_full = jnp.concatenate([p[2] for p in per_dev], axis=1)
    w_in = jnp.transpose(w_in_g, (1, 0, 2)).reshape(D, AB_IN)
    w_c = w_in[:, :C2]
    w_q = w_in[:, C2:]
    w_grp = [jnp.concatenate([w_q[:, t * N_GROUPS * ATTN_OUT + g * ATTN_OUT:][:, :ATTN_OUT] for t in range(3)], axis=1)
             for g in range(N_GROUPS)]
    uc = _mm_nn(n0, w_c, out_dtype=F32, name="mm_ab_in_conv", wide=True)
    uqs = [_mm_nn(n0_all[g], w_grp[g], out_dtype=BF16, slabs=dils[g], name=f"mm_ab_in_qkv{g}")
           for g in range(N_GROUPS)]
    ya, hglu, ct = _conv_fwd(uc, conv_w_full, ab_conv_b, ab_ln_g, ab_ln_b, name="conv_fwd")
    outs, lses = zip(*[_attn_fwd(uqs[g], biases, g, dils[g], name=f"attn_fwd_{g}") for g in range(N_GROUPS)])
    cat, outf, lse = _attn_merge(outs, lses, ya, name="attn_merge")
    w_out, w_up, w_dn = wait_mlp(cat)
    h1 = _mm_nn(cat, w_out, out_dtype=F32, residual=xs, name="mm_ab_out", wide=True)
    n1, z0 = _norm_mm_nn(h1, mlp_norm[0:1], (w_up, 0), out_dtype=BF16, name="norm_mm_up0", wide=True)
    h2 = _mm_nn(z0, (w_dn, 0), out_dtype=F32, residual=h1, a_fn=_relu_sq, name="mm_down0", wide=True)
    w_sc_in, w_sc_out = wait_sc(h2)
    n2, u3 = _norm_mm_nn(h2, sc_norm_full, w_sc_in, out_dtype=BF16, name="norm_mm_sc_in", wide=True)
    ysc = _sc_fwd(u3, sc_conv_full, name="sc_fwd")
    h3 = _mm_nn(ysc, w_sc_out, out_dtype=F32, residual=h2, name="mm_sc_out", wide=True)
    n3, z1 = _norm_mm_nn(h3, mlp_norm[1:2], (w_up, 1), out_dtype=BF16, name="norm_mm_up1", wide=True)
    h4 = _mm_nn(z1, (w_dn, 1), out_dtype=F32, residual=h3, a_fn=_relu_sq, name="mm_down1", wide=True)

    def dz_epilogue(acc, z):
        return acc * (2.0 * jnp.maximum(z.astype(F32), 0.0))

    dh4, dh4b, acc_final = _loss_bwd(h4, tgt, final_norm[None], name="loss_bwd")
    dz1 = _mm_nt([(dh4b, (w_dn, 1))], out_dtype=BF16, epilogue=dz_epilogue, extra=z1, name="mm_d_down1")
    g_dn1 = _mm_tn(z1, dh4b, a_fn=_relu_sq, name="mm_gw_down1")
    g_up1 = _mm_tn(n3, dz1, name="mm_gw_up1")
    rs_mlp1, wait_rs_mlp1 = _scatter_start([g_up1, g_dn1], ["col", "row"],
                                           [mlp_w_up.shape[1:], mlp_w_down.shape[1:]], name="scatter_mlp1")
    dh3, dh3b, acc_mlp1 = _mm_nt_rms_bwd(dz1, (w_up, 1), h3, mlp_norm[1:2], dh4, name="mm_d_up1_norm_bwd",
                                         dep=rs_mlp1[3])

    dysc = _mm_nt([(dh3b, w_sc_out)], out_dtype=F32, name="mm_d_sc_out")
    g_sc_out = _mm_tn(ysc, dh3b, name="mm_gw_sc_out")
    du3, acc_scw = _sc_bwd(u3, dysc, sc_conv_full, name="sc_bwd")
    g_sc_in = _mm_tn(n2, du3, name="mm_gw_sc_in")
    rs_sc, wait_rs_sc = _scatter_start([g_sc_in, g_sc_out], ["col", "row"],
                                       [sc_w_in.shape[1:], sc_w_out.shape[1:]], name="scatter_sc")
    dh2, dh2b, acc_sc = _mm_nt_rms_bwd(du3, w_sc_in, h2, sc_norm_full, dh3, name="mm_d_sc_in_norm_bwd",
                                       dep=rs_sc[3])

    dz0 = _mm_nt([(dh2b, (w_dn, 0))], out_dtype=BF16, epilogue=dz_epilogue, extra=z0, name="mm_d_down0")
    g_dn0 = _mm_tn(z0, dh2b, a_fn=_relu_sq, name="mm_gw_down0")
    g_up0 = _mm_tn(n1, dz0, name="mm_gw_up0")
    rs_mlp0, wait_rs_mlp0 = _scatter_start([g_up0, g_dn0], ["col", "row"],
                                           [mlp_w_up.shape[1:], mlp_w_down.shape[1:]], name="scatter_mlp0")
    dh1, dh1b, acc_mlp0 = _mm_nt_rms_bwd(dz0, (w_up, 0), h1, mlp_norm[0:1], dh2, name="mm_d_up0_norm_bwd",
                                         dep=rs_mlp0[3])

    dcat = _mm_nt([(dh1b, w_out)], out_dtype=F32, name="mm_d_ab_out")
    g_ab_out = _mm_tn(cat, dh1b, name="mm_gw_ab_out")
    prep = _attn_prep(dcat, outf, lse, name="attn_prep")
    dqkv, dbs = zip(*[_attn_bwd(uqs[g], prep[g][0], prep[g][2], prep[g][1], biases, g, dils[g],
                                name=f"attn_bwd_{g}") for g in range(N_GROUPS)])
    drel = _relbias_reduce(dbs, buckets, name="relbias_reduce")
    dc, acc_conv = _conv_bwd_ln(ct, dcat, hglu, ab_ln_g, ab_ln_b, name="conv_bwd_ln")
    duc = _conv_bwd_in(dc, uc, conv_w_full, name="conv_bwd_in")
    g_wc = _mm_tn(n0, duc, name="mm_gw_ab_in_conv")
    g_wgrp = [_mm_tn(n0_all[g], dqkv[g], slabs=dils[g], name=f"mm_gw_ab_in_qkv{g}") for g in range(N_GROUPS)]
    g_wq = jnp.concatenate([g_wgrp[g][:, t * ATTN_OUT:(t + 1) * ATTN_OUT]
                            for t in range(3) for g in range(N_GROUPS)], axis=1)
    g_w_in = jnp.concatenate([g_wc, g_wq], axis=1).reshape(D, N_DEV, AB_IN // N_DEV).transpose(1, 0, 2)
    rs_ab, wait_rs_ab = _scatter_start([g_w_in, g_ab_out], ["slot", "row"],
                                       [w_in_sh.shape, ab_w_out.shape[1:]], name="scatter_ab")
    dn0 = _mm_nt([(duc, w_c), (dqkv[0], w_grp[0])], out_dtype=F32, name="mm_d_ab_in", dep=rs_ab[3])
    dn0_views = [(_mm_nt([(dqkv[g], w_grp[g])], out_dtype=F32, slabs=dils[g], name=f"mm_d_ab_in_qkv{g}"), dils[g])
                 for g in range(1, N_GROUPS)]
    grad_x, grad_xb, acc_ab = _rms_bwd(xs, ab_norm, dn0, dh1, name="norm_ab_bwd", dn_views=dn0_views)

    small_full = [drel[:, :, :NUM_BUCKETS].transpose(2, 0, 1).reshape(NUM_BUCKETS, N_GROUPS * HEADS_PER_GROUP),
                  acc_ab[0], acc_conv[0:CONV_A_WIDTH], acc_conv[32],
                  acc_conv[33], acc_conv[34], acc_sc[0], acc_scw[0:SC_CONV_WIDTH],
                  jnp.stack([acc_mlp0[0], acc_mlp1[0]]), acc_final[0], acc_final[1]]
    small_full_shapes = [p.shape for p in small_full]
    summed = _unpack_rows(_small_allreduce(_pack_rows(small_full), name="allreduce_small"),
                          small_full_shapes)
    (s_rel, s_abn, s_cw, s_cb, s_lg, s_lb, s_scn, s_scw, s_mlpn, s_fn, s_err) = summed
    loss = (0.5 / D) * jnp.sum(s_err)
    small_grads = {
        "rel_bias": s_rel, "ab_norm": s_abn[None],
        "ab_conv_w": lax.dynamic_slice_in_dim(s_cw, me_lin * cw_sh, cw_sh, axis=1)[None],
        "ab_conv_b": s_cb[None], "ab_ln_g": s_lg[None], "ab_ln_b": s_lb[None],
        "sc_norm": lax.dynamic_slice_in_dim(s_scn, me_lin * scn_sh, scn_sh, axis=0)[None],
        "sc_conv_w": lax.dynamic_slice_in_dim(s_scw, me_lin * scw_sh, scw_sh, axis=1)[None],
        "mlp_norm": s_mlpn, "final_norm": s_fn,
    }
    small_w = {"rel_bias": (rel_bias, m_rel_bias, v_rel_bias), "ab_norm": (ab_norm, m_ab_norm, v_ab_norm),
               "ab_conv_w": (ab_conv_w, m_ab_conv_w, v_ab_conv_w), "ab_conv_b": (ab_conv_b, m_ab_conv_b, v_ab_conv_b),
               "ab_ln_g": (ab_ln_g, m_ab_ln_g, v_ab_ln_g), "ab_ln_b": (ab_ln_b, m_ab_ln_b, v_ab_ln_b),
               "sc_norm": (sc_norm, m_sc_norm, v_sc_norm), "sc_conv_w": (sc_conv_w, m_sc_conv_w, v_sc_conv_w),
               "mlp_norm": (mlp_norm, m_mlp_norm, v_mlp_norm), "final_norm": (final_norm, m_final_norm, v_final_norm)}
    small_names = list(small_grads)
    small_shapes = [small_grads[n].shape for n in small_names]
    d_pack, m_pack, v_pack = _adamw_small(
        _pack_rows([small_grads[n] for n in small_names]), _pack_rows([small_w[n][0] for n in small_names]),
        _pack_rows([small_w[n][1] for n in small_names]), _pack_rows([small_w[n][2] for n in small_names]),
        name="adamw_small")
    small = {n: (small_grads[n], d, nm_, nv_) for n, d, nm_, nv_ in zip(
        small_names, _unpack_rows(d_pack, small_shapes), _unpack_rows(m_pack, small_shapes),
        _unpack_rows(v_pack, small_shapes))}

    p_up1, p_dn1 = wait_rs_mlp1(grad_xb)
    p_sc_in, p_sc_out = wait_rs_sc(grad_xb)
    p_up0, p_dn0 = wait_rs_mlp0(grad_xb)
    p_w_in, p_ab_out = wait_rs_ab(grad_xb)
    big = {}
    for nm, parts, w, m, v in (("ab_w_in", [p_w_in], ab_w_in, m_ab_w_in, v_ab_w_in),
                               ("ab_w_out", [p_ab_out], ab_w_out, m_ab_w_out, v_ab_w_out),
                               ("sc_w_in", [p_sc_in], sc_w_in, m_sc_w_in, v_sc_w_in),
                               ("sc_w_out", [p_sc_out], sc_w_out, m_sc_w_out, v_sc_w_out),
                               ("mlp_w_up", [p_up0, p_up1], mlp_w_up, m_mlp_w_up, v_mlp_w_up),
                               ("mlp_w_down", [p_dn0, p_dn1], mlp_w_down, m_mlp_w_down, v_mlp_w_down)):
        C = w.shape[-1]
        res = _adamw_from_partials(parts, w.reshape(-1, C), m.reshape(-1, C), v.reshape(-1, C), name="adamw_" + nm)
        big[nm] = tuple(r.reshape(w.shape) for r in res)

    order = ["rel_bias", "ab_norm", "ab_w_in", "ab_conv_w", "ab_conv_b", "ab_ln_g", "ab_ln_b", "ab_w_out",
             "sc_norm", "sc_w_in", "sc_conv_w", "sc_w_out", "mlp_norm", "mlp_w_up", "mlp_w_down", "final_norm"]
    allres = {**big, **small}
    return (loss, grad_x.reshape(x.shape),
            *[allres[n][0] for n in order], *[allres[n][1] for n in order],
            *[allres[n][2] for n in order], *[allres[n][3] for n in order])
```
